```python
import math
import jax, jax.numpy as jnp
from jax import lax
import numpy as np

D_MODEL = 1024
BATCH = 8
SEQ = 2048
DEPTH = 1
DEC_BATCH = 32
DEC_SEQ = 8
PAST_LEN = 8192
PAGE_SIZE = 128

MIX_WIDTH = D_MODEL
N_HEADS = 8
HEAD_DIM = 64
N_KV_HEADS = 2
GROUP = N_HEADS // N_KV_HEADS
N_IDX_HEADS = 8
IDX_DIM = 64
TOPK_MAX = 256
Q_BLOCK = 128
N_BUCKETS = 32
MAX_DISTANCE = 128
M_HEADS = 4
M_HEAD_DIM = 128
MLSTM_CHUNK = 64
D_FF = 4 * D_MODEL
EPS = 1e-6
NEG = -1e30
ATT_WIDTH = N_HEADS * HEAD_DIM
M_WIDTH = M_HEADS * M_HEAD_DIM
SPLIT_SIZES = (ATT_WIDTH, N_KV_HEADS * HEAD_DIM, N_KV_HEADS * HEAD_DIM,
               N_IDX_HEADS * IDX_DIM, IDX_DIM, N_IDX_HEADS,
               M_WIDTH, M_WIDTH, M_WIDTH, M_WIDTH, M_HEADS, M_HEADS)
N_IN = sum(SPLIT_SIZES)
SPLIT_POINTS = tuple(int(v) for v in np.cumsum(SPLIT_SIZES)[:-1])

kernel_name = "hymba_dsa_mlstm_decode_step"

F32 = jnp.float32


def rmsnorm(x, g):
    xf = x.astype(F32)
    y = xf * lax.rsqrt(jnp.mean(xf * xf, axis=-1, keepdims=True) + EPS)
    return (y * g.astype(F32)).astype(x.dtype)


def t5_bucket(dist):
    n = jnp.maximum(dist, 0)
    max_exact = N_BUCKETS // 2
    scale = (N_BUCKETS - max_exact) / math.log(MAX_DISTANCE / max_exact)
    large = max_exact + (jnp.log(jnp.maximum(n, 1).astype(F32) / max_exact) * scale).astype(jnp.int32)
    large = jnp.minimum(large, N_BUCKETS - 1)
    return jnp.where(n < max_exact, n, large)


def dsa_block(q, qi, wi, q_pos, kv_all, ki_all, rel_bias, n_sel):
    B, T = q.shape[:2]
    L = kv_all.shape[1]
    key_pos = jnp.arange(L, dtype=jnp.int32)
    causal = key_pos[None, :] <= q_pos[:, None]
    dots = jnp.einsum('btjd,bsd->btjs', qi.astype(F32), ki_all.astype(F32))
    score = jnp.einsum('btj,btjs->bts', wi.astype(F32), jax.nn.relu(dots))
    score = jnp.where(causal[None], score, NEG)
    _, idx = lax.top_k(score, n_sel)
    valid = idx <= q_pos[None, :, None]
    sel = jax.vmap(lambda a, i: a[i])(kv_all, idx.reshape(B, T * n_sel))
    sel = sel.reshape(B, T, n_sel, 2, N_KV_HEADS, HEAD_DIM)
    k_sel, v_sel = sel[:, :, :, 0], sel[:, :, :, 1]
    qg = q.reshape(B, T, N_KV_HEADS, GROUP, HEAD_DIM)
    logits = jnp.einsum('btngd,btknd->btngk', qg, k_sel).astype(F32) * (HEAD_DIM ** -0.5)
    bias = rel_bias[t5_bucket(q_pos[None, :, None] - idx)]
    bias = bias.reshape(B, T, n_sel, N_KV_HEADS, GROUP).transpose(0, 1, 3, 4, 2)
    logits = jnp.where(valid[:, :, None, None, :], logits + bias.astype(F32), NEG)
    p = jax.nn.softmax(logits, axis=-1)
    out = jnp.einsum('btngk,btknd->btngd', p, v_sel.astype(F32))
    return out.reshape(B, T, ATT_WIDTH).astype(q.dtype)


def dsa_sweep(q, qi, wi, pos0, kv_all, ki_all, rel_bias, n_sel):
    B, T = q.shape[:2]
    qb = T if T <= Q_BLOCK else math.gcd(T, Q_BLOCK)
    nb = T // qb

    def blocks(a):
        return a.reshape(B, nb, qb, *a.shape[2:]).swapaxes(0, 1)

    starts = pos0 + jnp.arange(nb, dtype=jnp.int32) * qb

    def one(args):
        qx, qix, wix, s0 = args
        q_pos = s0 + jnp.arange(qb, dtype=jnp.int32)
        return dsa_block(qx, qix, wix, q_pos, kv_all, ki_all, rel_bias, n_sel)

    out = lax.map(one, (blocks(q), blocks(qi), blocks(wi), starts))
    return out.swapaxes(0, 1).reshape(B, T, ATT_WIDTH)


def mlstm_chunkwise(q, k, v, ig, lf, S0, n0, m0, chunk):
    B, T = q.shape[:2]
    nc = T // chunk
    tril = jnp.tril(jnp.ones((chunk, chunk), dtype=bool))

    def to_chunks(a):
        return a.astype(F32).reshape(B, nc, chunk, *a.shape[2:]).swapaxes(0, 1)

    def step(carry, xs):
        S, n, m = carry
        qc, kc, vc, ic, fc = xs
        b = jnp.cumsum(fc, axis=1)
        g = b + m[:, None, :]
        D = b[:, :, None, :] - b[:, None, :, :] + ic[:, None, :, :]
        D = jnp.where(tril[None, :, :, None], D, NEG)
        mt = jnp.maximum(g, jnp.max(D, axis=2))
        Dw = jnp.exp(D - mt[:, :, None, :])
        gw = jnp.exp(g - mt)
        qk = jnp.einsum('bthd,bshd->btsh', qc, kc) * Dw
        num = gw[..., None] * jnp.einsum('bthd,bhde->bthe', qc, S) + jnp.einsum('btsh,bshe->bthe', qk, vc)
        den = gw * jnp.einsum('bthd,bhd->bth', qc, n) + jnp.sum(qk, axis=2)
        h = num / jnp.maximum(jnp.abs(den), jnp.exp(-mt))[..., None]
        bL = b[:, -1]
        gL = bL + m
        a = bL[:, None, :] - b + ic
        m_new = jnp.maximum(gL, jnp.max(a, axis=1))
        aw = jnp.exp(a - m_new[:, None, :])
        sw = jnp.exp(gL - m_new)
        S_new = sw[..., None, None] * S + jnp.einsum('bsh,bshd,bshe->bhde', aw, kc, vc)
        n_new = sw[..., None] * n + jnp.einsum('bsh,bshd->bhd', aw, kc)
        return (S_new, n_new, m_new), h

    carry0 = (S0.astype(F32), n0.astype(F32), m0.astype(F32))
    (S, n, m), h = lax.scan(step, carry0, (to_chunks(q), to_chunks(k), to_chunks(v), to_chunks(ig), to_chunks(lf)))
    h = h.swapaxes(0, 1).reshape(B, T, M_HEADS, M_HEAD_DIM)
    return h, S, n, m


def hybrid_layer(x, k_past, v_past, ki_past, S0, n0, m0,
                 w_in, b_i, b_f, m_norm, rel_bias, w_out, g1, g2, w_up, w_down):
    B, T, _ = x.shape
    P = k_past.shape[1]
    u = rmsnorm(x, g1)
    z = u @ w_in
    (q_a, k_a, v_a, q_i, k_i, w_i, q_m, k_m, v_m, o_m, i_m, f_m) = jnp.split(z, SPLIT_POINTS, axis=-1)

    q_a = q_a.reshape(B, T, N_HEADS, HEAD_DIM)
    k_new = k_a.reshape(B, T, N_KV_HEADS, HEAD_DIM)
    v_new = v_a.reshape(B, T, N_KV_HEADS, HEAD_DIM)
    q_i = q_i.reshape(B, T, N_IDX_HEADS, IDX_DIM)
    w_i = w_i * (N_IDX_HEADS ** -0.5 * IDX_DIM ** -0.5)
    k_all = jnp.concatenate([k_past.astype(k_new.dtype), k_new], axis=1)
    v_all = jnp.concatenate([v_past.astype(v_new.dtype), v_new], axis=1)
    ki_all = jnp.concatenate([ki_past.astype(k_i.dtype), k_i], axis=1)
    L = P + T
    kv_all = jnp.stack([k_all, v_all], axis=2).reshape(B, L, 2 * N_KV_HEADS * HEAD_DIM)
    n_sel = min(TOPK_MAX, L // 4)
    attn = dsa_sweep(q_a, q_i, w_i, P, kv_all, ki_all, rel_bias, n_sel)

    q_m = q_m.reshape(B, T, M_HEADS, M_HEAD_DIM)
    k_m = k_m.reshape(B, T, M_HEADS, M_HEAD_DIM) * (M_HEAD_DIM ** -0.5)
    v_m = v_m.reshape(B, T, M_HEADS, M_HEAD_DIM)
    ig = i_m.astype(F32) + b_i.astype(F32)
    lf = jax.nn.log_sigmoid(f_m.astype(F32) + b_f.astype(F32))
    chunk = T if T <= MLSTM_CHUNK else math.gcd(T, MLSTM_CHUNK)
    h, S, n, m = mlstm_chunkwise(q_m, k_m, v_m, ig, lf, S0, n0, m0, chunk)
    h = h * lax.rsqrt(jnp.mean(h * h, axis=-1, keepdims=True) + EPS)
    h = h.reshape(B, T, M_WIDTH) * m_norm.astype(F32) * jax.nn.sigmoid(o_m.astype(F32))

    mix = jnp.concatenate([attn, h.astype(x.dtype)], axis=-1) @ w_out
    hres = x + mix
    f = rmsnorm(hres, g2)
    y = hres + jnp.square(jax.nn.relu(f @ w_up)) @ w_down
    return y, k_new, v_new, k_i, S, n, m


def setup_inputs(seed: int = 0) -> dict:
    key = jax.random.key(seed)
    ks = jax.random.split(key, 24)
    n_pages = PAST_LEN // PAGE_SIZE
    n_used = DEC_BATCH * n_pages
    n_pool = n_used + max(1, n_used // 4)
    nrm = jax.random.normal
    page_table = jax.random.permutation(ks[0], n_pool)[:n_used].reshape(DEC_BATCH, n_pages).astype(jnp.int32)
    return {
        "x_prompt": nrm(ks[1], (BATCH, SEQ, D_MODEL), F32),
        "x_sample": nrm(ks[2], (DEC_BATCH, DEC_SEQ, D_MODEL), F32),
        "cache_k": nrm(ks[3], (DEPTH, n_pool, PAGE_SIZE, N_KV_HEADS, HEAD_DIM), F32),
        "cache_v": nrm(ks[4], (DEPTH, n_pool, PAGE_SIZE, N_KV_HEADS, HEAD_DIM), F32),
        "cache_kidx": nrm(ks[5], (DEPTH, n_pool, PAGE_SIZE, IDX_DIM), F32),
        "page_table": page_table,
        "state_C": 0.1 * nrm(ks[6], (DEPTH, DEC_BATCH, M_HEADS, M_HEAD_DIM, M_HEAD_DIM), F32),
        "state_n": jnp.abs(nrm(ks[7], (DEPTH, DEC_BATCH, M_HEADS, M_HEAD_DIM), F32)),
        "state_m": 0.5 * nrm(ks[8], (DEPTH, DEC_BATCH, M_HEADS), F32),
        "w_in": nrm(ks[9], (DEPTH, D_MODEL, N_IN), F32) * D_MODEL ** -0.5,
        "b_igate": 0.1 * nrm(ks[10], (DEPTH, M_HEADS), F32),
        "b_fgate": 3.0 + 0.5 * nrm(ks[11], (DEPTH, M_HEADS), F32),
        "mlstm_norm": 1.0 + 0.01 * nrm(ks[12], (DEPTH, M_WIDTH), F32),
        "rel_bias": 0.1 * nrm(ks[13], (N_BUCKETS, N_HEADS), F32),
        "w_out": nrm(ks[14], (DEPTH, MIX_WIDTH, D_MODEL), F32) * MIX_WIDTH ** -0.5,
        "norm1": 1.0 + 0.01 * nrm(ks[15], (DEPTH, D_MODEL), F32),
        "norm2": 1.0 + 0.01 * nrm(ks[16], (DEPTH, D_MODEL), F32),
        "w_up": nrm(ks[17], (DEPTH, D_MODEL, D_FF), F32) * D_MODEL ** -0.5,
        "w_down": nrm(ks[18], (DEPTH, D_FF, D_MODEL), F32) * D_FF ** -0.5,
        "norm_f": 1.0 + 0.01 * nrm(ks[19], (D_MODEL,), F32),
    }


def reference(x_prompt, x_sample, cache_k, cache_v, cache_kidx, page_table, state_C, state_n, state_m,
              w_in, b_igate, b_fgate, mlstm_norm, rel_bias, w_out, norm1, norm2, w_up, w_down, norm_f):
    DB, NP = page_table.shape
    past = NP * PAGE_SIZE
    B = x_prompt.shape[0]
    xp, xs = x_prompt, x_sample
    kp_l, vp_l, kip_l, Cp_l, np_l, mp_l = [], [], [], [], [], []
    ks_l, vs_l, kis_l, Cs_l, ns_l, ms_l = [], [], [], [], [], []
    for l in range(DEPTH):
        wts = (w_in[l], b_igate[l], b_fgate[l], mlstm_norm[l], rel_bias, w_out[l],
               norm1[l], norm2[l], w_up[l], w_down[l])
        xp, kp, vp, kip, Cp, n_p, mp = hybrid_layer(
            xp,
            jnp.zeros((B, 0, N_KV_HEADS, HEAD_DIM), xp.dtype),
            jnp.zeros((B, 0, N_KV_HEADS, HEAD_DIM), xp.dtype),
            jnp.zeros((B, 0, IDX_DIM), xp.dtype),
            jnp.zeros((B, M_HEADS, M_HEAD_DIM, M_HEAD_DIM), F32),
            jnp.zeros((B, M_HEADS, M_HEAD_DIM), F32),
            jnp.zeros((B, M_HEADS), F32),
            *wts)
        k_past = cache_k[l][page_table].reshape(DB, past, N_KV_HEADS, HEAD_DIM)
        v_past = cache_v[l][page_table].reshape(DB, past, N_KV_HEADS, HEAD_DIM)
        ki_past = cache_kidx[l][page_table].reshape(DB, past, IDX_DIM)
        xs, k_s, v_s, ki_s, Cs, n_s, m_s = hybrid_layer(
            xs, k_past, v_past, ki_past, state_C[l], state_n[l], state_m[l], *wts)
        kp_l.append(kp); vp_l.append(vp); kip_l.append(kip); Cp_l.append(Cp); np_l.append(n_p); mp_l.append(mp)
        ks_l.append(k_s); vs_l.append(v_s); kis_l.append(ki_s); Cs_l.append(Cs); ns_l.append(n_s); ms_l.append(m_s)
    y_prompt = rmsnorm(xp, norm_f)
    y_sample = rmsnorm(xs, norm_f)
    return (y_prompt, y_sample,
            jnp.stack(kp_l), jnp.stack(vp_l), jnp.stack(kip_l),
            jnp.stack(Cp_l), jnp.stack(np_l), jnp.stack(mp_l),
            jnp.stack(ks_l), jnp.stack(vs_l), jnp.stack(kis_l),
            jnp.stack(Cs_l), jnp.stack(ns_l), jnp.stack(ms_l))
```

```python
import functools
import math

import numpy as np
import jax
import jax.numpy as jnp
from jax import lax
from jax.experimental import pallas as pl
from jax.experimental.pallas import tpu as pltpu

F32 = jnp.float32
BF16 = jnp.bfloat16
I32 = jnp.int32

N_HEADS = 8
HEAD_DIM = 64
N_KV_HEADS = 2
GROUP = N_HEADS // N_KV_HEADS
N_IDX_HEADS = 8
IDX_DIM = 64
TOPK_MAX = 256
N_BUCKETS = 32
MAX_DISTANCE = 128
M_HEADS = 4
M_HEAD_DIM = 128
PAGE_SIZE = 128
EPS = 1e-6
NEG = -1e30
ATT_WIDTH = N_HEADS * HEAD_DIM
M_WIDTH = M_HEADS * M_HEAD_DIM

LANES = 128
SUBLANES = 8
VMEM_LIMIT = 56 * 1024 * 1024

C_QA = 0
C_K = C_QA + ATT_WIDTH
C_V = C_K + LANES
C_QI = C_V + LANES
C_KI2 = C_QI + N_IDX_HEADS * IDX_DIM
C_MISC = C_KI2 + LANES
C_QM = C_MISC + LANES
C_KM = C_QM + M_WIDTH
C_VM = C_KM + M_WIDTH
C_OM = C_VM + M_WIDTH
N_PACK = C_OM + M_WIDTH
MISC_W = 0
MISC_I = 8
MISC_F = 12

HEAD_PERM = (0, 4, 1, 5, 2, 6, 3, 7)

TQ = 128
CKP = 256
ML = 128

NT_DIMS = (((1,), (1,)), ((), ()))
TN_DIMS = (((0,), (0,)), ((), ()))


def _bucket_bounds():
    max_exact = N_BUCKETS // 2
    scale = (N_BUCKETS - max_exact) / math.log(MAX_DISTANCE / max_exact)

    def bucket(n, dt):
        if n < max_exact:
            return n
        val = np.log(np.asarray(max(n, 1), dt) / dt(max_exact)) * dt(scale)
        return min(max_exact + int(val), N_BUCKETS - 1)

    table = [bucket(n, np.float32) for n in range(MAX_DISTANCE + 2)]
    assert table == [bucket(n, np.float64) for n in range(MAX_DISTANCE + 2)]
    assert table[MAX_DISTANCE] == N_BUCKETS - 1
    return [next(d for d, b in enumerate(table) if b >= k) for k in range(N_BUCKETS)]


BUCKET_BOUNDS = _bucket_bounds()


def _cparams(n_axes):
    return pltpu.CompilerParams(dimension_semantics=("arbitrary",) * n_axes,
                                vmem_limit_bytes=VMEM_LIMIT)


def _const_spec(shape):
    nd = len(shape)
    return pl.BlockSpec(shape, lambda *_: (0,) * nd, pipeline_mode=pl.Buffered(1))


def _rms(x, g):
    return x * lax.rsqrt(jnp.mean(x * x, axis=-1, keepdims=True) + EPS) * g


def _inproj_kernel(x_ref, g_ref, w_ref, qa_ref, k_ref, v_ref, kb_ref, vb_ref, qi_ref, ki_ref,
                   ki2_ref, misc_ref, qm_ref, km_ref, vm_ref, om_ref):
    ub = _rms(x_ref[...], g_ref[...]).astype(BF16)

    def mm(c0, n):
        return jnp.dot(ub, w_ref[:, c0:c0 + n], preferred_element_type=F32)

    qa_ref[...] = (mm(C_QA, ATT_WIDTH) * (HEAD_DIM ** -0.5)).astype(qa_ref.dtype)
    k = mm(C_K, LANES)
    k_ref[...] = k
    kb_ref[...] = k.astype(kb_ref.dtype)
    v = mm(C_V, LANES)
    v_ref[...] = v
    vb_ref[...] = v.astype(vb_ref.dtype)
    qi_ref[...] = mm(C_QI, N_IDX_HEADS * IDX_DIM).astype(qi_ref.dtype)
    ki2 = mm(C_KI2, LANES)
    ki_ref[...] = ki2[:, :IDX_DIM]
    ki2_ref[...] = ki2.astype(ki2_ref.dtype)
    misc_ref[...] = mm(C_MISC, LANES)
    qm_ref[...] = mm(C_QM, M_WIDTH).astype(qm_ref.dtype)
    km_ref[...] = (mm(C_KM, M_WIDTH) * (M_HEAD_DIM ** -0.5)).astype(km_ref.dtype)
    vm_ref[...] = mm(C_VM, M_WIDTH).astype(vm_ref.dtype)
    om_ref[...] = mm(C_OM, M_WIDTH)


def _inproj(x2, g1, wp, adt, tm):
    R, D = x2.shape
    assert R % tm == 0
    widths = (ATT_WIDTH, LANES, LANES, LANES, LANES, N_IDX_HEADS * IDX_DIM, IDX_DIM, LANES, LANES,
              M_WIDTH, M_WIDTH, M_WIDTH, M_WIDTH)
    dtypes = (adt, F32, F32, adt, adt, adt, F32, adt, F32, adt, adt, adt, F32)
    row = lambda i: (i, 0)
    return pl.pallas_call(
        _inproj_kernel,
        grid=(R // tm,),
        in_specs=[pl.BlockSpec((tm, D), row), _const_spec((1, D)), _const_spec((D, N_PACK))],
        out_specs=[pl.BlockSpec((tm, w), row) for w in widths],
        out_shape=[jax.ShapeDtypeStruct((R, w), dt) for w, dt in zip(widths, dtypes)],
        compiler_params=_cparams(1),
        name="inproj",
    )(x2, g1, wp)


def _post_kernel(ff_chunk, final_norm, x_ref, a_ref, h_ref, woa_ref, woh_ref, g2_ref, wup_ref,
                 wdn_ref, gf_ref, y_ref):
    mix = jnp.dot(a_ref[...].astype(BF16), woa_ref[...], preferred_element_type=F32)
    mix = mix + jnp.dot(h_ref[...].astype(BF16), woh_ref[...], preferred_element_type=F32)
    hres = x_ref[...] + mix
    f = _rms(hres, g2_ref[...]).astype(BF16)
    acc = hres
    for c0 in range(0, wup_ref.shape[1], ff_chunk):
        up = jnp.dot(f, wup_ref[:, c0:c0 + ff_chunk], preferred_element_type=F32)
        r = jnp.maximum(up, 0.0)
        acc = acc + jnp.dot((r * r).astype(BF16), wdn_ref[c0:c0 + ff_chunk, :],
                            preferred_element_type=F32)
    y_ref[...] = _rms(acc, gf_ref[...]) if final_norm else acc


def _post(x2, attn, h, woa, woh, g2, wup, wdn, gf, final_norm, tm):
    R, D = x2.shape
    dff = wup.shape[1]
    assert R % tm == 0
    row = lambda i: (i, 0)
    return pl.pallas_call(
        functools.partial(_post_kernel, min(dff, 1024), final_norm),
        grid=(R // tm,),
        in_specs=[pl.BlockSpec((tm, D), row), pl.BlockSpec((tm, ATT_WIDTH), row),
                  pl.BlockSpec((tm, M_WIDTH), row), _const_spec(woa.shape), _const_spec(woh.shape),
                  _const_spec((1, D)), _const_spec(wup.shape), _const_spec(wdn.shape),
                  _const_spec((1, D))],
        out_specs=pl.BlockSpec((tm, D), row),
        out_shape=jax.ShapeDtypeStruct((R, D), F32),
        compiler_params=_cparams(1),
        name="post",
    )(x2, attn, h, woa, woh, g2, wup, wdn, gf)


def _log_sigmoid(x):
    return -(jnp.maximum(-x, 0.0) + jnp.log1p(jnp.exp(-jnp.abs(x))))


def _mlstm_kernel(nvalid, has_state, *refs):
    if has_state:
        (q_ref, k_ref, v_ref, o_ref, misc_ref, gb_ref, mn_ref, c0_ref, n0_ref, m0_ref,
         h_ref, c_ref, n_ref, m_ref) = refs
    else:
        (q_ref, k_ref, v_ref, o_ref, misc_ref, gb_ref, mn_ref,
         h_ref, c_ref, n_ref, m_ref) = refs
    L = ML

    @pl.when(pl.program_id(1) == 0)
    def _():
        if has_state:
            c_ref[...] = c0_ref[...]
            n_ref[...] = n0_ref[...]
            m_ref[...] = m0_ref[...]
        else:
            c_ref[...] = jnp.zeros_like(c_ref)
            n_ref[...] = jnp.zeros_like(n_ref)
            m_ref[...] = jnp.zeros_like(m_ref)

    def padded(ref, dt):
        x = ref[...].astype(dt)
        if nvalid == L:
            return x
        return jnp.concatenate([x, jnp.zeros((L - nvalid, x.shape[1]), dt)], axis=0)

    lane = lax.broadcasted_iota(I32, (L, LANES), 1)
    rowi = lax.broadcasted_iota(I32, (L, LANES), 0)
    is_i = (lane >= MISC_I) & (lane < MISC_I + M_HEADS)
    is_f = (lane >= MISC_F) & (lane < MISC_F + M_HEADS)
    gx = padded(misc_ref, F32) + gb_ref[...]
    gates = jnp.where(is_f, _log_sigmoid(gx), gx)
    if nvalid != L:
        gates = jnp.where(rowi < nvalid, gates, jnp.where(is_i, NEG, 0.0))

    r2 = lax.broadcasted_iota(I32, (L, L), 0)
    c2 = lax.broadcasted_iota(I32, (L, L), 1)
    tril = r2 >= c2
    bcum = jnp.dot(tril.astype(F32), gates, precision=lax.Precision.HIGHEST,
                   preferred_element_type=F32)
    z = jnp.where(is_i, gates, bcum)
    sr = lax.broadcasted_iota(I32, (SUBLANES, LANES), 0)
    sc = lax.broadcasted_iota(I32, (SUBLANES, LANES), 1)
    sel = (sc == jnp.where(sr < M_HEADS, MISC_F + sr, MISC_I + sr - M_HEADS)).astype(F32)
    rows = lax.dot_general(sel, z, NT_DIMS, precision=lax.Precision.HIGHEST,
                           preferred_element_type=F32)

    qb = padded(q_ref, BF16)
    kb = padded(k_ref, BF16)
    vb = padded(v_ref, BF16)
    ob = padded(o_ref, F32)

    for hd in range(M_HEADS):
        sl = slice(hd * M_HEAD_DIM, (hd + 1) * M_HEAD_DIM)
        q, k, v = qb[:, sl], kb[:, sl], vb[:, sl]
        bcol = z[:, MISC_F + hd:MISC_F + hd + 1]
        icol = z[:, MISC_I + hd:MISC_I + hd + 1]
        brow = rows[hd:hd + 1, :]
        irow = rows[M_HEADS + hd:M_HEADS + hd + 1, :]
        m_prev = m_ref[0, hd:hd + 1, 0:1]
        s_prev = c_ref[0, hd]
        n_prev = n_ref[0, hd:hd + 1, :]

        g = bcol + m_prev
        dm = jnp.where(tril, bcol - brow + irow, NEG)
        mt = jnp.maximum(g, jnp.max(dm, axis=1, keepdims=True))
        dw = jnp.exp(dm - mt)
        gw = jnp.exp(g - mt)
        qk = lax.dot_general(q, k, NT_DIMS, preferred_element_type=F32) * dw
        num = gw * jnp.dot(q, s_prev.astype(BF16), preferred_element_type=F32)
        num = num + jnp.dot(qk.astype(BF16), v, preferred_element_type=F32)
        den = gw * jnp.sum(q.astype(F32) * n_prev, axis=1, keepdims=True)
        den = den + jnp.sum(qk, axis=1, keepdims=True)
        hh = num / jnp.maximum(jnp.abs(den), jnp.exp(-mt))
        hh = hh * lax.rsqrt(jnp.mean(hh * hh, axis=-1, keepdims=True) + EPS)
        hh = hh * mn_ref[:, sl] * jax.nn.sigmoid(ob[:, sl])
        h_ref[:, sl] = hh[:nvalid].astype(h_ref.dtype)

        b_last = bcol[L - 1:L, :]
        g_last = b_last + m_prev
        a = b_last - bcol + icol
        m_new = jnp.maximum(g_last, jnp.max(a, axis=0, keepdims=True))
        aw = jnp.exp(a - m_new)
        sw = jnp.exp(g_last - m_new)
        ak = aw * k.astype(F32)
        c_ref[0, hd] = sw * s_prev + lax.dot_general(ak.astype(BF16), v, TN_DIMS,
                                                     preferred_element_type=F32)
        n_ref[0, hd:hd + 1, :] = sw * n_prev + jnp.sum(ak, axis=0, keepdims=True)
        m_ref[0, hd:hd + 1, :] = jnp.broadcast_to(m_new, (1, LANES))


def _mlstm(qm, km, vm, om, misc, gate_bias, mnorm, state, B, T, hdt):
    nvalid = min(T, ML)
    assert T % nvalid == 0
    nc = T // nvalid
    blk = lambda w: pl.BlockSpec((nvalid, w), lambda b, c: (b * nc + c, 0))
    st_specs = [pl.BlockSpec((1, M_HEADS, M_HEAD_DIM, M_HEAD_DIM), lambda b, c: (b, 0, 0, 0)),
                pl.BlockSpec((1, M_HEADS, M_HEAD_DIM), lambda b, c: (b, 0, 0)),
                pl.BlockSpec((1, M_HEADS, LANES), lambda b, c: (b, 0, 0))]
    in_specs = [blk(M_WIDTH), blk(M_WIDTH), blk(M_WIDTH), blk(M_WIDTH), blk(LANES),
                pl.BlockSpec((1, LANES), lambda b, c: (0, 0)),
                pl.BlockSpec((1, M_WIDTH), lambda b, c: (0, 0))]
    args = [qm, km, vm, om, misc, gate_bias, mnorm]
    if state is not None:
        in_specs += st_specs
        args += list(state)
    return pl.pallas_call(
        functools.partial(_mlstm_kernel, nvalid, state is not None),
        grid=(B, nc),
        in_specs=in_specs,
        out_specs=[blk(M_WIDTH)] + st_specs,
        out_shape=[jax.ShapeDtypeStruct((B * T, M_WIDTH), hdt),
                   jax.ShapeDtypeStruct((B, M_HEADS, M_HEAD_DIM, M_HEAD_DIM), F32),
                   jax.ShapeDtypeStruct((B, M_HEADS, M_HEAD_DIM), F32),
                   jax.ShapeDtypeStruct((B, M_HEADS, LANES), F32)],
        compiler_params=_cparams(2),
        name="mlstm",
    )(*args)


def _sortable_key(score):
    bits = lax.bitcast_convert_type(score, I32)
    return bits ^ (lax.shift_right_arithmetic(bits, 31) & 0x7FFFFFFF)


def _build_bias_strip(strip_ref, rb_ref, off):
    _, ntiles, rows, _ = strip_ref.shape
    i = lax.broadcasted_iota(I32, (rows, LANES), 0)
    x = lax.broadcasted_iota(I32, (rows, LANES), 1)
    for t in range(ntiles):
        dist = i + (off - LANES * t) - x
        for h in range(N_HEADS):
            val = jnp.full((rows, LANES), rb_ref[0, h], F32)
            for b in range(1, N_BUCKETS):
                val = jnp.where(dist >= BUCKET_BOUNDS[b], rb_ref[b, h], val)
            strip_ref[h, t] = val


def _count(key_ref, nk, ck, pred):
    rows = key_ref.shape[0]

    def body(c, acc):
        c0 = pl.multiple_of(c * ck, ck)
        hit = jnp.where(pred(key_ref[:, pl.ds(c0, ck)], c0), 1, 0)
        for j in range(ck // LANES):
            acc = acc + hit[:, j * LANES:(j + 1) * LANES]
        return acc

    acc = lax.fori_loop(0, nk, body, jnp.zeros((rows, LANES), I32))
    return jnp.sum(acc, axis=1, keepdims=True)


def _select_topk(key_ref, nk, ck, n_sel, idx_bits):
    rows = key_ref.shape[0]

    def bit_step(i, thr):
        cand = thr + lax.shift_left(jnp.int32(1), 31 - i)
        cnt = _count(key_ref, nk, ck, lambda kk, c0: kk >= cand)
        return jnp.where(cnt >= n_sel, cand, thr)

    thr = lax.fori_loop(0, 32, bit_step, jnp.full((rows, 1), jnp.iinfo(jnp.int32).min, I32))
    n_gt = _count(key_ref, nk, ck, lambda kk, c0: kk > thr)
    n_ge = _count(key_ref, nk, ck, lambda kk, c0: kk >= thr)
    need = n_sel - n_gt

    def tie_search(_):
        def idx_step(i, lo):
            cand = lo + lax.shift_left(jnp.int32(1), idx_bits - 1 - i)

            def pred(kk, c0):
                idx = c0 + lax.broadcasted_iota(I32, kk.shape, 1)
                return jnp.where(kk == thr, idx, jnp.iinfo(jnp.int32).max) < cand

            cnt = _count(key_ref, nk, ck, pred)
            return jnp.where(cnt < need, cand, lo)

        return lax.fori_loop(0, idx_bits, idx_step, jnp.zeros((rows, 1), I32))

    all_ties = jnp.full((rows, 1), jnp.iinfo(jnp.int32).max, I32)
    some_row_splits = jnp.max(n_ge) > n_sel
    jstar = lax.cond(some_row_splits, tie_search, lambda _: all_ties, 0)
    jstar = jnp.where(n_ge > n_sel, jstar, all_ties)
    return thr, jstar


def _valid_mask(keys, c0, thr, jstar, qpos):
    idx = c0 + lax.broadcasted_iota(I32, keys.shape, 1)
    sel = jnp.where(keys > thr, 1, jnp.where(keys == thr, jnp.where(idx <= jstar, 1, 0), 0))
    return jnp.where(idx <= qpos, sel, 0) > 0


def _flash_step(carry, qg, kc, vc, bias, valid, tq):
    m_old, l_old, acc = carry
    s = lax.dot_general(qg, kc, NT_DIMS, preferred_element_type=F32)
    parts = [jnp.where(valid, s[g * tq:(g + 1) * tq] + bias[g], NEG) for g in range(GROUP)]
    sm = jnp.concatenate(parts, axis=0)
    m_new = jnp.maximum(m_old, jnp.max(sm, axis=1, keepdims=True))
    alpha = jnp.exp(m_old - m_new)
    p = jnp.exp(sm - m_new)
    l_new = alpha * l_old + jnp.sum(p, axis=1, keepdims=True)
    acc = alpha * acc + jnp.dot(p.astype(BF16), vc, preferred_element_type=F32)
    return m_new, l_new, acc


def _flash_init(tq):
    return (jnp.full((GROUP * tq, 1), NEG, F32), jnp.zeros((GROUP * tq, 1), F32),
            jnp.zeros((GROUP * tq, LANES), F32))


def _group_queries(qa, tq):
    lane = lax.broadcasted_iota(I32, (tq, LANES), 1)
    out = []
    for n in range(N_KV_HEADS):
        keep = (lane < HEAD_DIM) if n == 0 else (lane >= HEAD_DIM)
        tiles = [jnp.where(keep, qa[:, j * LANES:(j + 1) * LANES], jnp.zeros((), qa.dtype))
                 for j in range(GROUP)]
        out.append(jnp.concatenate(tiles, axis=0).astype(BF16))
    return out


def _write_attn(out_ref, carries, tq):
    lane = lax.broadcasted_iota(I32, (tq, LANES), 1)
    res = [acc / l for (_, l, acc) in carries]
    for j in range(GROUP):
        tile = jnp.where(lane < HEAD_DIM, res[0][j * tq:(j + 1) * tq], res[1][j * tq:(j + 1) * tq])
        out_ref[:, j * LANES:(j + 1) * LANES] = tile.astype(out_ref.dtype)


def _dsa_prompt_kernel(n_sel, idx_bits, rb_ref, qa_ref, qi_ref, misc_ref, ki2_ref, kb_ref, vb_ref,
                       out_ref, key_ref, strip_ref):
    qb = pl.program_id(1)
    q0 = qb * TQ
    nk = (q0 + TQ + CKP - 1) // CKP
    tiles_per_chunk = CKP // LANES
    strip_off = LANES * (strip_ref.shape[1] - tiles_per_chunk)

    @pl.when((pl.program_id(0) == 0) & (qb == 0))
    def _():
        _build_bias_strip(strip_ref, rb_ref, strip_off)

    qpos = q0 + lax.broadcasted_iota(I32, (TQ, 1), 0)

    qi = qi_ref[...]
    lane = lax.broadcasted_iota(I32, (TQ, LANES), 1)
    zero = jnp.zeros((), qi.dtype)
    stack = []
    for h in range(N_IDX_HEADS):
        tile = qi[:, (h // 2) * LANES:(h // 2 + 1) * LANES]
        keep = (lane < IDX_DIM) if h % 2 == 0 else (lane >= IDX_DIM)
        stack.append(jnp.where(keep, tile, zero))
    qstack = jnp.concatenate(stack, axis=0)
    w = misc_ref[:, MISC_W:MISC_W + N_IDX_HEADS] * (N_IDX_HEADS ** -0.5 * IDX_DIM ** -0.5)

    def score_chunk(c, _):
        c0 = pl.multiple_of(c * CKP, CKP)
        d = lax.dot_general(qstack, ki2_ref[pl.ds(c0, CKP), :], NT_DIMS, preferred_element_type=F32)
        d = jnp.maximum(d, 0.0)
        s = jnp.zeros((TQ, CKP), F32)
        for h in range(N_IDX_HEADS):
            s = s + d[h * TQ:(h + 1) * TQ] * w[:, h:h + 1]
        idx = c0 + lax.broadcasted_iota(I32, (TQ, CKP), 1)
        key_ref[:, pl.ds(c0, CKP)] = _sortable_key(jnp.where(idx <= qpos, s, NEG))
        return 0

    lax.fori_loop(0, nk, score_chunk, 0)

    thr, jstar = _select_topk(key_ref, nk, CKP, n_sel, idx_bits)

    qgroups = _group_queries(qa_ref[...], TQ)

    def attend_chunk(c, carries):
        c0 = pl.multiple_of(c * CKP, CKP)
        valid = _valid_mask(key_ref[:, pl.ds(c0, CKP)], c0, thr, jstar, qpos)
        kc = kb_ref[pl.ds(c0, CKP), :]
        vc = vb_ref[pl.ds(c0, CKP), :]
        t0 = jnp.maximum(strip_off // LANES - (qb - tiles_per_chunk * c), 0)
        out = []
        for n in range(N_KV_HEADS):
            bias = [jnp.concatenate([strip_ref[n * GROUP + g, t0 + j] for j in range(tiles_per_chunk)],
                                    axis=1)
                    for g in range(GROUP)]
            out.append(_flash_step(carries[n], qgroups[n], kc, vc, bias, valid, TQ))
        return tuple(out)

    carries = lax.fori_loop(0, nk, attend_chunk, (_flash_init(TQ), _flash_init(TQ)))
    _write_attn(out_ref, carries, TQ)


def _dsa_prompt(rel_bias, qa, qi, misc, ki2b, kb, vb, B, T):
    assert T % CKP == 0 and T % TQ == 0
    nq = T // TQ
    n_sel = min(TOPK_MAX, T // 4)
    idx_bits = max(1, (T - 1).bit_length())
    blk = lambda w: pl.BlockSpec((TQ, w), lambda b, q: (b * nq + q, 0))
    seq = lambda w: pl.BlockSpec((T, w), lambda b, q: (b, 0))
    return pl.pallas_call(
        functools.partial(_dsa_prompt_kernel, n_sel, idx_bits),
        grid=(B, nq),
        in_specs=[pl.BlockSpec(memory_space=pltpu.SMEM), blk(ATT_WIDTH), blk(N_IDX_HEADS * IDX_DIM),
                  blk(LANES), seq(LANES), seq(LANES), seq(LANES)],
        out_specs=blk(ATT_WIDTH),
        out_shape=jax.ShapeDtypeStruct((B * T, ATT_WIDTH), BF16),
        scratch_shapes=[pltpu.VMEM((TQ, T), I32),
                        pltpu.VMEM((N_HEADS, 3 + CKP // LANES, TQ, LANES), F32)],
        compiler_params=_cparams(2),
        name="dsa_prompt",
    )(rel_bias, qa, qi, misc, ki2b, kb, vb)


def _dsa_sample_kernel(n_sel, idx_bits, n_pages, ts, pt_ref, rb_ref, qa_ref, qi_ref, misc_ref,
                       kin_ref, kn_ref, vn_ref, cki_hbm, ck_hbm, cv_hbm, out_ref,
                       ki_buf, k_buf, v_buf, sems, key_ref, strip_ref):
    b = pl.program_id(0)
    nb = pl.num_programs(0)
    slot = b % 2
    past = n_pages * PAGE_SIZE
    nk = n_pages + 1
    ck = PAGE_SIZE

    def page_copies(bb, sl, j):
        pid = pt_ref[bb, j]
        rows = pl.ds(j * PAGE_SIZE, PAGE_SIZE)
        return (pltpu.make_async_copy(cki_hbm.at[pid], ki_buf.at[sl, rows], sems.at[sl, 0]),
                pltpu.make_async_copy(ck_hbm.at[pid], k_buf.at[sl, rows], sems.at[sl, 1]),
                pltpu.make_async_copy(cv_hbm.at[pid], v_buf.at[sl, rows], sems.at[sl, 2]))

    def start_all(bb, sl):
        def body(j, _):
            for cp in page_copies(bb, sl, j):
                cp.start()
            return 0
        lax.fori_loop(0, n_pages, body, 0)

    def wait_all(bb, sl):
        def body(j, _):
            for cp in page_copies(bb, sl, j):
                cp.wait()
            return 0
        lax.fori_loop(0, n_pages, body, 0)

    @pl.when(b == 0)
    def _():
        _build_bias_strip(strip_ref, rb_ref, LANES * (strip_ref.shape[1] - 1))
        start_all(0, 0)

    @pl.when(b + 1 < nb)
    def _():
        start_all(b + 1, 1 - slot)

    def own_chunk(ref):
        x = ref[...]
        return jnp.concatenate([x, jnp.zeros((ck - ts, x.shape[1]), x.dtype)], axis=0)

    ki_buf[slot, pl.ds(past, ck), :] = own_chunk(kin_ref)
    k_buf[slot, pl.ds(past, ck), :] = own_chunk(kn_ref)
    v_buf[slot, pl.ds(past, ck), :] = own_chunk(vn_ref)

    wait_all(b, slot)

    qpos = past + lax.broadcasted_iota(I32, (ts, 1), 0)

    qi = qi_ref[...]
    qstack = jnp.concatenate([qi[:, h * IDX_DIM:(h + 1) * IDX_DIM] for h in range(N_IDX_HEADS)],
                             axis=0).astype(BF16)
    w = misc_ref[:, MISC_W:MISC_W + N_IDX_HEADS] * (N_IDX_HEADS ** -0.5 * IDX_DIM ** -0.5)

    def score_chunk(c, _):
        c0 = pl.multiple_of(c * ck, ck)
        kc = ki_buf[slot, pl.ds(c0, ck), :].astype(BF16)
        d = lax.dot_general(qstack, kc, NT_DIMS, preferred_element_type=F32)
        d = jnp.maximum(d, 0.0)
        s = jnp.zeros((ts, ck), F32)
        for h in range(N_IDX_HEADS):
            s = s + d[h * ts:(h + 1) * ts] * w[:, h:h + 1]
        idx = c0 + lax.broadcasted_iota(I32, (ts, ck), 1)
        key_ref[:, pl.ds(c0, ck)] = _sortable_key(jnp.where(idx <= qpos, s, NEG))
        return 0

    lax.fori_loop(0, nk, score_chunk, 0)

    thr, jstar = _select_topk(key_ref, nk, ck, n_sel, idx_bits)

    qgroups = _group_queries(qa_ref[...], ts)
    last_tile = strip_ref.shape[1] - 1

    def attend_chunk(c, carries):
        c0 = pl.multiple_of(c * ck, ck)
        valid = _valid_mask(key_ref[:, pl.ds(c0, ck)], c0, thr, jstar, qpos)
        kc = k_buf[slot, pl.ds(c0, ck), :].astype(BF16)
        vc = v_buf[slot, pl.ds(c0, ck), :].astype(BF16)
        t0 = jnp.maximum(c - (nk - 1 - last_tile), 0)
        out = []
        for n in range(N_KV_HEADS):
            bias = [strip_ref[n * GROUP + g, t0] for g in range(GROUP)]
            out.append(_flash_step(carries[n], qgroups[n], kc, vc, bias, valid, ts))
        return tuple(out)

    carries = lax.fori_loop(0, nk, attend_chunk, (_flash_init(ts), _flash_init(ts)))
    _write_attn(out_ref, carries, ts)


def _dsa_sample(page_table, rel_bias, qa, qi, misc, ki_new, k_new, v_new, cache_kidx, cache_k,
                cache_v, DB, ts):
    n_pages = page_table.shape[1]
    past = n_pages * PAGE_SIZE
    ltot = past + ts
    n_sel = min(TOPK_MAX, ltot // 4)
    lpad = past + PAGE_SIZE
    idx_bits = max(1, (lpad - 1).bit_length())
    assert ts % SUBLANES == 0 and ts <= PAGE_SIZE
    blk = lambda w: pl.BlockSpec((ts, w), lambda b, pt: (b, 0))
    grid_spec = pltpu.PrefetchScalarGridSpec(
        num_scalar_prefetch=1,
        grid=(DB,),
        in_specs=[pl.BlockSpec(memory_space=pltpu.SMEM), blk(ATT_WIDTH), blk(N_IDX_HEADS * IDX_DIM),
                  blk(LANES), blk(IDX_DIM), blk(LANES), blk(LANES),
                  pl.BlockSpec(memory_space=pl.ANY), pl.BlockSpec(memory_space=pl.ANY),
                  pl.BlockSpec(memory_space=pl.ANY)],
        out_specs=blk(ATT_WIDTH),
        scratch_shapes=[pltpu.VMEM((2, lpad, IDX_DIM), F32),
                        pltpu.VMEM((2, lpad, LANES), F32),
                        pltpu.VMEM((2, lpad, LANES), F32),
                        pltpu.SemaphoreType.DMA((2, 3)),
                        pltpu.VMEM((ts, lpad), I32),
                        pltpu.VMEM((N_HEADS, 3, ts, LANES), F32)],
    )
    return pl.pallas_call(
        functools.partial(_dsa_sample_kernel, n_sel, idx_bits, n_pages, ts),
        grid_spec=grid_spec,
        out_shape=jax.ShapeDtypeStruct((DB * ts, ATT_WIDTH), F32),
        compiler_params=_cparams(1),
        name="dsa_sample",
    )(page_table, rel_bias, qa, qi, misc, ki_new, k_new, v_new, cache_kidx, cache_k, cache_v)


def _pack_layer_weights(w_in, b_i, b_f, w_out, w_up, w_down):
    D = w_in.shape[0]
    sizes = (ATT_WIDTH, N_KV_HEADS * HEAD_DIM, N_KV_HEADS * HEAD_DIM, N_IDX_HEADS * IDX_DIM, IDX_DIM,
             N_IDX_HEADS, M_WIDTH, M_WIDTH, M_WIDTH, M_WIDTH, M_HEADS, M_HEADS)
    assert w_in.shape[1] == sum(sizes)
    pts = np.cumsum((0,) + sizes)
    seg = [w_in[:, pts[i]:pts[i + 1]] for i in range(len(sizes))]
    qa, k, v, qi, ki, wi, qm, km, vm, om, im, fm = seg
    perm = np.asarray(HEAD_PERM)
    qa = qa.reshape(D, N_HEADS, HEAD_DIM)[:, perm].reshape(D, ATT_WIDTH)
    misc = jnp.concatenate([wi, im, fm, jnp.zeros((D, LANES - N_IDX_HEADS - 2 * M_HEADS), w_in.dtype)], axis=1)
    wp = jnp.concatenate([qa, k, v, qi, ki, ki, misc, qm, km, vm, om], axis=1).astype(BF16)
    assert wp.shape[1] == N_PACK
    gate_bias = jnp.concatenate([jnp.zeros((MISC_I,), F32), b_i.astype(F32), b_f.astype(F32),
                                 jnp.zeros((LANES - MISC_F - M_HEADS,), F32)]).reshape(1, LANES)
    woa = w_out[:ATT_WIDTH].reshape(N_HEADS, HEAD_DIM, -1)[perm].reshape(ATT_WIDTH, -1).astype(BF16)
    woh = w_out[ATT_WIDTH:].astype(BF16)
    return wp, gate_bias, woa, woh, w_up.astype(BF16), w_down.astype(BF16)


def _layer(x, packed, g1, g2, mnorm, rel_bias, gf, final_norm, past):
    wp, gate_bias, woa, woh, wup, wdn = packed
    B, T, D = x.shape
    x2 = x.reshape(B * T, D)
    lowp = past is None
    adt = BF16 if lowp else F32
    tm = math.gcd(B * T, 512)
    (qa, k, v, kb, vb, qi, ki, ki2, misc, qm, km, vm, om) = _inproj(x2, g1.reshape(1, D), wp, adt, tm)
    if past is None:
        attn = _dsa_prompt(rel_bias, qa, qi, misc, ki2, kb, vb, B, T)
        state = None
    else:
        page_table, cache_k, cache_v, cache_kidx, c0, n0, m0 = past
        n_pool = cache_k.shape[0]
        attn = _dsa_sample(page_table, rel_bias, qa, qi, misc, ki, k, v, cache_kidx,
                           cache_k.reshape(n_pool, PAGE_SIZE, N_KV_HEADS * HEAD_DIM),
                           cache_v.reshape(n_pool, PAGE_SIZE, N_KV_HEADS * HEAD_DIM), B, T)
        state = (c0, n0, jnp.broadcast_to(m0[..., None], m0.shape + (LANES,)))
    h, c_new, n_new, m_new = _mlstm(qm, km, vm, om, misc, gate_bias, mnorm.reshape(1, M_WIDTH), state,
                                    B, T, BF16 if lowp else F32)
    y = _post(x2, attn, h, woa, woh, g2.reshape(1, D), wup, wdn, gf.reshape(1, D), final_norm, tm)
    return (y.reshape(B, T, D), k.reshape(B, T, N_KV_HEADS, HEAD_DIM), v.reshape(B, T, N_KV_HEADS, HEAD_DIM),
            ki.reshape(B, T, IDX_DIM), c_new, n_new, m_new[..., 0])


def kernel(x_prompt, x_sample, cache_k, cache_v, cache_kidx, page_table, state_C, state_n, state_m,
           w_in, b_igate, b_fgate, mlstm_norm, rel_bias, w_out, norm1, norm2, w_up, w_down, norm_f):
    depth = w_in.shape[0]
    xp, xs = x_prompt, x_sample
    outs_p, outs_s = [], []
    for l in range(depth):
        packed = _pack_layer_weights(w_in[l], b_igate[l], b_fgate[l], w_out[l], w_up[l], w_down[l])
        last = l == depth - 1
        common = (packed, norm1[l], norm2[l], mlstm_norm[l], rel_bias, norm_f, last)
        rp = _layer(xp, *common, None)
        rs = _layer(xs, *common, (page_table, cache_k[l], cache_v[l], cache_kidx[l],
                                  state_C[l], state_n[l], state_m[l]))
        xp, xs = rp[0], rs[0]
        outs_p.append(rp[1:])
        outs_s.append(rs[1:])
    stack = lambda outs, i: jnp.stack([o[i] for o in outs])
    return ((xp, xs) + tuple(stack(outs_p, i) for i in range(6))
            + tuple(stack(outs_s, i) for i in range(6)))
```

```python
import functools
import math

import numpy as np
import jax
import jax.numpy as jnp
from jax import lax
from jax.experimental import pallas as pl
from jax.experimental.pallas import tpu as pltpu

F32 = jnp.float32
BF16 = jnp.bfloat16
I32 = jnp.int32

N_HEADS = 8
HEAD_DIM = 64
N_KV_HEADS = 2
GROUP = N_HEADS // N_KV_HEADS
N_IDX_HEADS = 8
IDX_DIM = 64
TOPK_MAX = 256
N_BUCKETS = 32
MAX_DISTANCE = 128
M_HEADS = 4
M_HEAD_DIM = 128
PAGE_SIZE = 128
EPS = 1e-6
NEG = -1e30
ATT_WIDTH = N_HEADS * HEAD_DIM
M_WIDTH = M_HEADS * M_HEAD_DIM

LANES = 128
SUBLANES = 8
VMEM_LIMIT = 56 * 1024 * 1024

C_QA = 0
C_K = C_QA + ATT_WIDTH
C_V = C_K + LANES
C_QI = C_V + LANES
C_KI2 = C_QI + N_IDX_HEADS * IDX_DIM
C_MISC = C_KI2 + LANES
C_QM = C_MISC + LANES
C_KM = C_QM + M_WIDTH
C_VM = C_KM + M_WIDTH
C_OM = C_VM + M_WIDTH
N_PACK = C_OM + M_WIDTH
MISC_W = 0
MISC_I = 8
MISC_F = 12

HEAD_PERM = (0, 4, 1, 5, 2, 6, 3, 7)

TQ = 128
CKP = 256
ML = 128

NT_DIMS = (((1,), (1,)), ((), ()))
TN_DIMS = (((0,), (0,)), ((), ()))


def _bucket_bounds():
    max_exact = N_BUCKETS // 2
    scale = (N_BUCKETS - max_exact) / math.log(MAX_DISTANCE / max_exact)

    def bucket(n, dt):
        if n < max_exact:
            return n
        val = np.log(np.asarray(max(n, 1), dt) / dt(max_exact)) * dt(scale)
        return min(max_exact + int(val), N_BUCKETS - 1)

    table = [bucket(n, np.float32) for n in range(MAX_DISTANCE + 2)]
    assert table == [bucket(n, np.float64) for n in range(MAX_DISTANCE + 2)]
    assert table[MAX_DISTANCE] == N_BUCKETS - 1
    return [next(d for d, b in enumerate(table) if b >= k) for k in range(N_BUCKETS)]


BUCKET_BOUNDS = _bucket_bounds()


def _cparams(n_axes):
    return pltpu.CompilerParams(dimension_semantics=("arbitrary",) * n_axes,
                                vmem_limit_bytes=VMEM_LIMIT)


def _const_spec(shape):
    nd = len(shape)
    return pl.BlockSpec(shape, lambda *_: (0,) * nd, pipeline_mode=pl.Buffered(1))


def _rms(x, g):
    return x * lax.rsqrt(jnp.mean(x * x, axis=-1, keepdims=True) + EPS) * g


def _inproj_common(ub, w_ref, qa_ref, qi_ref, misc_ref, qm_ref, km_ref, vm_ref, om_ref):
    def mm(c0, n):
        return jnp.dot(ub, w_ref[:, c0:c0 + n], preferred_element_type=F32)

    qa_ref[...] = (mm(C_QA, ATT_WIDTH) * (HEAD_DIM ** -0.5)).astype(qa_ref.dtype)
    qi_ref[...] = mm(C_QI, N_IDX_HEADS * IDX_DIM).astype(qi_ref.dtype)
    misc_ref[...] = mm(C_MISC, LANES)
    qm_ref[...] = mm(C_QM, M_WIDTH).astype(qm_ref.dtype)
    km_ref[...] = (mm(C_KM, M_WIDTH) * (M_HEAD_DIM ** -0.5)).astype(km_ref.dtype)
    vm_ref[...] = mm(C_VM, M_WIDTH).astype(vm_ref.dtype)
    om_ref[...] = mm(C_OM, M_WIDTH)
    return mm


def _inproj_rows_kernel(x_ref, g_ref, w_ref, qa_ref, qi_ref, misc_ref, qm_ref, km_ref, vm_ref,
                        om_ref, k_ref, v_ref, ki_ref):
    ub = _rms(x_ref[...], g_ref[...]).astype(BF16)
    mm = _inproj_common(ub, w_ref, qa_ref, qi_ref, misc_ref, qm_ref, km_ref, vm_ref, om_ref)
    k_ref[...] = mm(C_K, LANES)
    v_ref[...] = mm(C_V, LANES)
    ki_ref[...] = mm(C_KI2, LANES)[:, :IDX_DIM]


def _inproj_cols_kernel(x_ref, g_ref, w_ref, wt_ref, qa_ref, qi_ref, misc_ref, qm_ref, km_ref,
                        vm_ref, om_ref, kt_ref, vt_ref, kit_ref, ktb_ref, vtb_ref, ki2tb_ref):
    ub = _rms(x_ref[...], g_ref[...]).astype(BF16)
    _inproj_common(ub, w_ref, qa_ref, qi_ref, misc_ref, qm_ref, km_ref, vm_ref, om_ref)
    t3 = lax.dot_general(wt_ref[...], ub, NT_DIMS, preferred_element_type=F32)
    kt, vt, ki2t = t3[:LANES], t3[LANES:2 * LANES], t3[2 * LANES:]
    kt_ref[0] = kt
    vt_ref[0] = vt
    kit_ref[0] = ki2t[:IDX_DIM]
    for j in range(ktb_ref.shape[1]):
        cols = slice(j * CKP, (j + 1) * CKP)
        ktb_ref[0, j] = kt[:, cols].astype(BF16)
        vtb_ref[0, j] = vt[:, cols].astype(BF16)
        ki2tb_ref[0, j] = ki2t[:, cols].astype(BF16)


def _inproj(x2, g1, wp, wt, B, T, tm):
    R, D = x2.shape
    assert R == B * T and R % tm == 0
    adt = F32 if wt is None else BF16
    row = lambda i: (i, 0)
    widths = (ATT_WIDTH, N_IDX_HEADS * IDX_DIM, LANES, M_WIDTH, M_WIDTH, M_WIDTH, M_WIDTH)
    dtypes = (adt, adt, F32, adt, adt, adt, F32)
    out_specs = [pl.BlockSpec((tm, w), row) for w in widths]
    out_shape = [jax.ShapeDtypeStruct((R, w), dt) for w, dt in zip(widths, dtypes)]
    in_specs = [pl.BlockSpec((tm, D), row), _const_spec((1, D)), _const_spec((D, N_PACK))]
    if wt is None:
        kern, args = _inproj_rows_kernel, (x2, g1, wp)
        for w in (LANES, LANES, IDX_DIM):
            out_specs.append(pl.BlockSpec((tm, w), row))
            out_shape.append(jax.ShapeDtypeStruct((R, w), F32))
    else:
        assert T % tm == 0 and tm % CKP == 0
        tpb, cpt = T // tm, tm // CKP
        kern, args = _inproj_cols_kernel, (x2, g1, wp, wt)
        in_specs.append(_const_spec(wt.shape))
        for w in (LANES, LANES, IDX_DIM):
            out_specs.append(pl.BlockSpec((1, w, tm), lambda i: (i // tpb, 0, i % tpb)))
            out_shape.append(jax.ShapeDtypeStruct((B, w, T), F32))
        for _ in range(3):
            out_specs.append(pl.BlockSpec((1, cpt, LANES, CKP), lambda i: (i // tpb, i % tpb, 0, 0)))
            out_shape.append(jax.ShapeDtypeStruct((B, T // CKP, LANES, CKP), BF16))
    return pl.pallas_call(
        kern,
        grid=(R // tm,),
        in_specs=in_specs,
        out_specs=out_specs,
        out_shape=out_shape,
        compiler_params=_cparams(1),
        name="inproj",
    )(*args)


def _post_kernel(ff_chunk, final_norm, x_ref, a_ref, h_ref, woa_ref, woh_ref, g2_ref, wup_ref,
                 wdn_ref, gf_ref, y_ref):
    mix = jnp.dot(a_ref[...].astype(BF16), woa_ref[...], preferred_element_type=F32)
    mix = mix + jnp.dot(h_ref[...].astype(BF16), woh_ref[...], preferred_element_type=F32)
    hres = x_ref[...] + mix
    f = _rms(hres, g2_ref[...]).astype(BF16)
    acc = hres
    for c0 in range(0, wup_ref.shape[1], ff_chunk):
        up = jnp.dot(f, wup_ref[:, c0:c0 + ff_chunk], preferred_element_type=F32)
        r = jnp.maximum(up, 0.0)
        acc = acc + jnp.dot((r * r).astype(BF16), wdn_ref[c0:c0 + ff_chunk, :],
                            preferred_element_type=F32)
    y_ref[...] = _rms(acc, gf_ref[...]) if final_norm else acc


def _post(x2, attn, h, woa, woh, g2, wup, wdn, gf, final_norm, tm):
    R, D = x2.shape
    dff = wup.shape[1]
    assert R % tm == 0
    row = lambda i: (i, 0)
    return pl.pallas_call(
        functools.partial(_post_kernel, min(dff, 1024), final_norm),
        grid=(R // tm,),
        in_specs=[pl.BlockSpec((tm, D), row), pl.BlockSpec((tm, ATT_WIDTH), row),
                  pl.BlockSpec((tm, M_WIDTH), row), _const_spec(woa.shape), _const_spec(woh.shape),
                  _const_spec((1, D)), _const_spec(wup.shape), _const_spec(wdn.shape),
                  _const_spec((1, D))],
        out_specs=pl.BlockSpec((tm, D), row),
        out_shape=jax.ShapeDtypeStruct((R, D), F32),
        compiler_params=_cparams(1),
        name="post",
    )(x2, attn, h, woa, woh, g2, wup, wdn, gf)


def _log_sigmoid(x):
    return -(jnp.maximum(-x, 0.0) + jnp.log1p(jnp.exp(-jnp.abs(x))))


def _mlstm_kernel(nvalid, has_state, *refs):
    if has_state:
        (q_ref, k_ref, v_ref, o_ref, misc_ref, gb_ref, mn_ref, c0_ref, n0_ref, m0_ref,
         h_ref, c_ref, n_ref, m_ref) = refs
    else:
        (q_ref, k_ref, v_ref, o_ref, misc_ref, gb_ref, mn_ref,
         h_ref, c_ref, n_ref, m_ref) = refs
    L = ML

    @pl.when(pl.program_id(1) == 0)
    def _():
        if has_state:
            c_ref[...] = c0_ref[...]
            n_ref[...] = n0_ref[...]
            m_ref[...] = m0_ref[...]
        else:
            c_ref[...] = jnp.zeros_like(c_ref)
            n_ref[...] = jnp.zeros_like(n_ref)
            m_ref[...] = jnp.zeros_like(m_ref)

    def padded(ref, dt):
        x = ref[...].astype(dt)
        if nvalid == L:
            return x
        return jnp.concatenate([x, jnp.zeros((L - nvalid, x.shape[1]), dt)], axis=0)

    lane = lax.broadcasted_iota(I32, (L, LANES), 1)
    rowi = lax.broadcasted_iota(I32, (L, LANES), 0)
    is_i = (lane >= MISC_I) & (lane < MISC_I + M_HEADS)
    is_f = (lane >= MISC_F) & (lane < MISC_F + M_HEADS)
    gx = padded(misc_ref, F32) + gb_ref[...]
    gates = jnp.where(is_f, _log_sigmoid(gx), gx)
    if nvalid != L:
        gates = jnp.where(rowi < nvalid, gates, jnp.where(is_i, NEG, 0.0))

    r2 = lax.broadcasted_iota(I32, (L, L), 0)
    c2 = lax.broadcasted_iota(I32, (L, L), 1)
    tril = r2 >= c2
    bcum = jnp.dot(tril.astype(F32), gates, precision=lax.Precision.HIGHEST,
                   preferred_element_type=F32)
    z = jnp.where(is_i, gates, bcum)
    sr = lax.broadcasted_iota(I32, (SUBLANES, LANES), 0)
    sc = lax.broadcasted_iota(I32, (SUBLANES, LANES), 1)
    sel = (sc == jnp.where(sr < M_HEADS, MISC_F + sr, MISC_I + sr - M_HEADS)).astype(F32)
    rows = lax.dot_general(sel, z, NT_DIMS, precision=lax.Precision.HIGHEST,
                           preferred_element_type=F32)

    qb = padded(q_ref, BF16)
    kb = padded(k_ref, BF16)
    vb = padded(v_ref, BF16)
    ob = padded(o_ref, F32)

    for hd in range(M_HEADS):
        sl = slice(hd * M_HEAD_DIM, (hd + 1) * M_HEAD_DIM)
        q, k, v = qb[:, sl], kb[:, sl], vb[:, sl]
        bcol = z[:, MISC_F + hd:MISC_F + hd + 1]
        icol = z[:, MISC_I + hd:MISC_I + hd + 1]
        brow = rows[hd:hd + 1, :]
        irow = rows[M_HEADS + hd:M_HEADS + hd + 1, :]
        m_prev = m_ref[0, hd:hd + 1, 0:1]
        s_prev = c_ref[0, hd]
        n_prev = n_ref[0, hd:hd + 1, :]

        g = bcol + m_prev
        dm = jnp.where(tril, bcol - brow + irow, NEG)
        mt = jnp.maximum(g, jnp.max(dm, axis=1, keepdims=True))
        dw = jnp.exp(dm - mt)
        gw = jnp.exp(g - mt)
        qk = lax.dot_general(q, k, NT_DIMS, preferred_element_type=F32) * dw
        num = gw * jnp.dot(q, s_prev.astype(BF16), preferred_element_type=F32)
        num = num + jnp.dot(qk.astype(BF16), v, preferred_element_type=F32)
        den = gw * jnp.sum(q.astype(F32) * n_prev, axis=1, keepdims=True)
        den = den + jnp.sum(qk, axis=1, keepdims=True)
        hh = num / jnp.maximum(jnp.abs(den), jnp.exp(-mt))
        hh = hh * lax.rsqrt(jnp.mean(hh * hh, axis=-1, keepdims=True) + EPS)
        hh = hh * mn_ref[:, sl] * jax.nn.sigmoid(ob[:, sl])
        h_ref[:, sl] = hh[:nvalid].astype(h_ref.dtype)

        b_last = bcol[L - 1:L, :]
        g_last = b_last + m_prev
        a = b_last - bcol + icol
        m_new = jnp.maximum(g_last, jnp.max(a, axis=0, keepdims=True))
        aw = jnp.exp(a - m_new)
        sw = jnp.exp(g_last - m_new)
        ak = aw * k.astype(F32)
        c_ref[0, hd] = sw * s_prev + lax.dot_general(ak.astype(BF16), v, TN_DIMS,
                                                     preferred_element_type=F32)
        n_ref[0, hd:hd + 1, :] = sw * n_prev + jnp.sum(ak, axis=0, keepdims=True)
        m_ref[0, hd:hd + 1, :] = jnp.broadcast_to(m_new, (1, LANES))


def _mlstm(qm, km, vm, om, misc, gate_bias, mnorm, state, B, T, hdt):
    nvalid = min(T, ML)
    assert T % nvalid == 0
    nc = T // nvalid
    blk = lambda w: pl.BlockSpec((nvalid, w), lambda b, c: (b * nc + c, 0))
    st_specs = [pl.BlockSpec((1, M_HEADS, M_HEAD_DIM, M_HEAD_DIM), lambda b, c: (b, 0, 0, 0)),
                pl.BlockSpec((1, M_HEADS, M_HEAD_DIM), lambda b, c: (b, 0, 0)),
                pl.BlockSpec((1, M_HEADS, LANES), lambda b, c: (b, 0, 0))]
    in_specs = [blk(M_WIDTH), blk(M_WIDTH), blk(M_WIDTH), blk(M_WIDTH), blk(LANES),
                pl.BlockSpec((1, LANES), lambda b, c: (0, 0)),
                pl.BlockSpec((1, M_WIDTH), lambda b, c: (0, 0))]
    args = [qm, km, vm, om, misc, gate_bias, mnorm]
    if state is not None:
        in_specs += st_specs
        args += list(state)
    return pl.pallas_call(
        functools.partial(_mlstm_kernel, nvalid, state is not None),
        grid=(B, nc),
        in_specs=in_specs,
        out_specs=[blk(M_WIDTH)] + st_specs,
        out_shape=[jax.ShapeDtypeStruct((B * T, M_WIDTH), hdt),
                   jax.ShapeDtypeStruct((B, M_HEADS, M_HEAD_DIM, M_HEAD_DIM), F32),
                   jax.ShapeDtypeStruct((B, M_HEADS, M_HEAD_DIM), F32),
                   jax.ShapeDtypeStruct((B, M_HEADS, LANES), F32)],
        compiler_params=_cparams(2),
        name="mlstm",
    )(*args)


def _sortable_key(score):
    bits = lax.bitcast_convert_type(score, I32)
    return bits ^ (lax.shift_right_arithmetic(bits, 31) & 0x7FFFFFFF)


def _build_bias_strip(strip_ref, rb_ref, off):
    _, ntiles, rows, _ = strip_ref.shape
    i = lax.broadcasted_iota(I32, (rows, LANES), 0)
    x = lax.broadcasted_iota(I32, (rows, LANES), 1)
    for t in range(ntiles):
        dist = i + (off - LANES * t) - x
        for h in range(N_HEADS):
            val = jnp.full((rows, LANES), rb_ref[0, h], F32)
            for b in range(1, N_BUCKETS):
                val = jnp.where(dist >= BUCKET_BOUNDS[b], rb_ref[b, h], val)
            strip_ref[h, t] = val


def _count(key_ref, nk, ck, pred):
    rows = key_ref.shape[0]

    def body(c, acc):
        c0 = pl.multiple_of(c * ck, ck)
        hit = jnp.where(pred(key_ref[:, pl.ds(c0, ck)], c0), 1, 0)
        for j in range(ck // LANES):
            acc = acc + hit[:, j * LANES:(j + 1) * LANES]
        return acc

    acc = lax.fori_loop(0, nk, body, jnp.zeros((rows, LANES), I32))
    return jnp.sum(acc, axis=1, keepdims=True)


def _select_topk(key_ref, nk, ck, n_sel, idx_bits):
    rows = key_ref.shape[0]

    def bit_step(i, thr):
        cand = thr + lax.shift_left(jnp.int32(1), 31 - i)
        cnt = _count(key_ref, nk, ck, lambda kk, c0: kk >= cand)
        return jnp.where(cnt >= n_sel, cand, thr)

    thr = lax.fori_loop(0, 32, bit_step, jnp.full((rows, 1), jnp.iinfo(jnp.int32).min, I32))
    n_gt = _count(key_ref, nk, ck, lambda kk, c0: kk > thr)
    n_ge = _count(key_ref, nk, ck, lambda kk, c0: kk >= thr)
    need = n_sel - n_gt

    def tie_search(_):
        def idx_step(i, lo):
            cand = lo + lax.shift_left(jnp.int32(1), idx_bits - 1 - i)

            def pred(kk, c0):
                idx = c0 + lax.broadcasted_iota(I32, kk.shape, 1)
                return jnp.where(kk == thr, idx, jnp.iinfo(jnp.int32).max) < cand

            cnt = _count(key_ref, nk, ck, pred)
            return jnp.where(cnt < need, cand, lo)

        return lax.fori_loop(0, idx_bits, idx_step, jnp.zeros((rows, 1), I32))

    all_ties = jnp.full((rows, 1), jnp.iinfo(jnp.int32).max, I32)
    some_row_splits = jnp.max(n_ge) > n_sel
    jstar = lax.cond(some_row_splits, tie_search, lambda _: all_ties, 0)
    jstar = jnp.where(n_ge > n_sel, jstar, all_ties)
    return thr, jstar


def _valid_mask(keys, c0, thr, jstar, qpos):
    idx = c0 + lax.broadcasted_iota(I32, keys.shape, 1)
    sel = jnp.where(keys > thr, 1, jnp.where(keys == thr, jnp.where(idx <= jstar, 1, 0), 0))
    return jnp.where(idx <= qpos, sel, 0) > 0


def _flash_step(carry, qg, kct, vct, bias, valid, tq):
    m_old, l_old, acc = carry
    s = jnp.dot(qg, kct, preferred_element_type=F32)
    parts = [jnp.where(valid, s[g * tq:(g + 1) * tq] + bias[g], NEG) for g in range(GROUP)]
    sm = jnp.concatenate(parts, axis=0)
    m_new = jnp.maximum(m_old, jnp.max(sm, axis=1, keepdims=True))
    alpha = jnp.exp(m_old - m_new)
    p = jnp.exp(sm - m_new)
    l_new = alpha * l_old + jnp.sum(p, axis=1, keepdims=True)
    acc = alpha * acc + lax.dot_general(p.astype(BF16), vct, NT_DIMS, preferred_element_type=F32)
    return m_new, l_new, acc


def _flash_init(tq):
    return (jnp.full((GROUP * tq, 1), NEG, F32), jnp.zeros((GROUP * tq, 1), F32),
            jnp.zeros((GROUP * tq, LANES), F32))


def _group_queries(qa, tq):
    lane = lax.broadcasted_iota(I32, (tq, LANES), 1)
    out = []
    for n in range(N_KV_HEADS):
        keep = (lane < HEAD_DIM) if n == 0 else (lane >= HEAD_DIM)
        tiles = [jnp.where(keep, qa[:, j * LANES:(j + 1) * LANES], jnp.zeros((), qa.dtype))
                 for j in range(GROUP)]
        out.append(jnp.concatenate(tiles, axis=0).astype(BF16))
    return out


def _write_attn(out_ref, carries, tq):
    lane = lax.broadcasted_iota(I32, (tq, LANES), 1)
    res = [acc / l for (_, l, acc) in carries]
    for j in range(GROUP):
        tile = jnp.where(lane < HEAD_DIM, res[0][j * tq:(j + 1) * tq], res[1][j * tq:(j + 1) * tq])
        out_ref[:, j * LANES:(j + 1) * LANES] = tile.astype(out_ref.dtype)


def _dsa_prompt_kernel(n_sel, idx_bits, rb_ref, qa_ref, qi_ref, misc_ref, ki2_ref, kb_ref, vb_ref,
                       out_ref, key_ref, strip_ref):
    qb = pl.program_id(1)
    q0 = qb * TQ
    nk = (q0 + TQ + CKP - 1) // CKP
    tiles_per_chunk = CKP // LANES
    strip_off = LANES * (strip_ref.shape[1] - tiles_per_chunk)

    @pl.when((pl.program_id(0) == 0) & (qb == 0))
    def _():
        _build_bias_strip(strip_ref, rb_ref, strip_off)

    qpos = q0 + lax.broadcasted_iota(I32, (TQ, 1), 0)

    qi = qi_ref[...]
    lane = lax.broadcasted_iota(I32, (TQ, LANES), 1)
    zero = jnp.zeros((), qi.dtype)
    stack = []
    for h in range(N_IDX_HEADS):
        tile = qi[:, (h // 2) * LANES:(h // 2 + 1) * LANES]
        keep = (lane < IDX_DIM) if h % 2 == 0 else (lane >= IDX_DIM)
        stack.append(jnp.where(keep, tile, zero))
    qstack = jnp.concatenate(stack, axis=0)
    w = misc_ref[:, MISC_W:MISC_W + N_IDX_HEADS] * (N_IDX_HEADS ** -0.5 * IDX_DIM ** -0.5)

    def score_chunk(c, _):
        c0 = pl.multiple_of(c * CKP, CKP)
        d = jnp.dot(qstack, ki2_ref[0, c], preferred_element_type=F32)
        d = jnp.maximum(d, 0.0)
        s = jnp.zeros((TQ, CKP), F32)
        for h in range(N_IDX_HEADS):
            s = s + d[h * TQ:(h + 1) * TQ] * w[:, h:h + 1]
        idx = c0 + lax.broadcasted_iota(I32, (TQ, CKP), 1)
        key_ref[:, pl.ds(c0, CKP)] = _sortable_key(jnp.where(idx <= qpos, s, NEG))
        return 0

    lax.fori_loop(0, nk, score_chunk, 0)

    thr, jstar = _select_topk(key_ref, nk, CKP, n_sel, idx_bits)

    qgroups = _group_queries(qa_ref[...], TQ)

    def attend_chunk(c, carries):
        c0 = pl.multiple_of(c * CKP, CKP)
        valid = _valid_mask(key_ref[:, pl.ds(c0, CKP)], c0, thr, jstar, qpos)
        kc = kb_ref[0, c]
        vc = vb_ref[0, c]
        t0 = jnp.maximum(strip_off // LANES - (qb - tiles_per_chunk * c), 0)
        out = []
        for n in range(N_KV_HEADS):
            bias = [jnp.concatenate([strip_ref[n * GROUP + g, t0 + j] for j in range(tiles_per_chunk)],
                                    axis=1)
                    for g in range(GROUP)]
            out.append(_flash_step(carries[n], qgroups[n], kc, vc, bias, valid, TQ))
        return tuple(out)

    carries = lax.fori_loop(0, nk, attend_chunk, (_flash_init(TQ), _flash_init(TQ)))
    _write_attn(out_ref, carries, TQ)


def _dsa_prompt(rel_bias, qa, qi, misc, ki2tb, ktb, vtb, B, T):
    assert T % CKP == 0 and T % TQ == 0
    nq = T // TQ
    n_sel = min(TOPK_MAX, T // 4)
    idx_bits = max(1, (T - 1).bit_length())
    blk = lambda w: pl.BlockSpec((TQ, w), lambda b, q: (b * nq + q, 0))
    seq = lambda w: pl.BlockSpec((1, T // CKP, LANES, CKP), lambda b, q: (b, 0, 0, 0))
    return pl.pallas_call(
        functools.partial(_dsa_prompt_kernel, n_sel, idx_bits),
        grid=(B, nq),
        in_specs=[pl.BlockSpec(memory_space=pltpu.SMEM), blk(ATT_WIDTH), blk(N_IDX_HEADS * IDX_DIM),
                  blk(LANES), seq(LANES), seq(LANES), seq(LANES)],
        out_specs=blk(ATT_WIDTH),
        out_shape=jax.ShapeDtypeStruct((B * T, ATT_WIDTH), BF16),
        scratch_shapes=[pltpu.VMEM((TQ, T), I32),
                        pltpu.VMEM((N_HEADS, 3 + CKP // LANES, TQ, LANES), F32)],
        compiler_params=_cparams(2),
        name="dsa_prompt",
    )(rel_bias, qa, qi, misc, ki2tb, ktb, vtb)


def _page_pipeline(pt_ref, n_pages, caches, bufs, sems):
    def copies(bb, sl, j):
        pid = pt_ref[bb, j]
        cols = pl.ds(pl.multiple_of(j * PAGE_SIZE, PAGE_SIZE), PAGE_SIZE)
        return [pltpu.make_async_copy(c.at[pid], buf.at[sl, :, cols], sems.at[sl, i])
                for i, (c, buf) in enumerate(zip(caches, bufs))]

    def start_all(bb, sl):
        def body(j, _):
            for cp in copies(bb, sl, j):
                cp.start()
            return 0
        lax.fori_loop(0, n_pages, body, 0)

    def wait_all(bb, sl):
        def body(j, _):
            for cp in copies(bb, sl, j):
                cp.wait()
            return 0
        lax.fori_loop(0, n_pages, body, 0)

    def step():
        b = pl.program_id(0)
        slot = b % 2

        @pl.when(b == 0)
        def _():
            start_all(0, 0)

        @pl.when(b + 1 < pl.num_programs(0))
        def _():
            start_all(b + 1, 1 - slot)

        wait_all(b, slot)
        return slot

    return step


def _pad_rows(x, rows):
    return jnp.concatenate([x, jnp.zeros((rows - x.shape[0], x.shape[1]), x.dtype)], axis=0)


def _sample_score_kernel(n_sel, idx_bits, n_pages, ts, ck, pt_ref, qi_ref, misc_ref, kin_ref,
                         ckit_hbm, key_ref, thr_ref, jst_ref, ki_buf, sems):
    b = pl.program_id(0)
    past = n_pages * PAGE_SIZE
    slot = _page_pipeline(pt_ref, n_pages, [ckit_hbm], [ki_buf], sems)()

    qi = qi_ref[...]
    qstack = jnp.concatenate([qi[:, h * IDX_DIM:(h + 1) * IDX_DIM] for h in range(N_IDX_HEADS)],
                             axis=0).astype(BF16)
    w = misc_ref[:, MISC_W:MISC_W + N_IDX_HEADS] * (N_IDX_HEADS ** -0.5 * IDX_DIM ** -0.5)
    d_past = jnp.dot(qstack, ki_buf[slot].astype(BF16), preferred_element_type=F32)
    d_own = lax.dot_general(qstack, _pad_rows(kin_ref[...], PAGE_SIZE).astype(BF16), NT_DIMS,
                            preferred_element_type=F32)
    d = jnp.maximum(jnp.concatenate([d_past, d_own], axis=1), 0.0)
    s = jnp.zeros((ts, past + PAGE_SIZE), F32)
    for h in range(N_IDX_HEADS):
        s = s + d[h * ts:(h + 1) * ts] * w[:, h:h + 1]
    idx = lax.broadcasted_iota(I32, s.shape, 1)
    qpos = past + lax.broadcasted_iota(I32, (ts, 1), 0)
    key_ref[pl.ds(pl.multiple_of(b * ts, ts), ts), :] = _sortable_key(jnp.where(idx <= qpos, s, NEG))

    @pl.when(b == pl.num_programs(0) - 1)
    def _():
        thr, jstar = _select_topk(key_ref, key_ref.shape[1] // ck, ck, n_sel, idx_bits)
        thr_ref[...] = jnp.broadcast_to(thr, thr_ref.shape)
        jst_ref[...] = jnp.broadcast_to(jstar, jst_ref.shape)


def _sample_attend_kernel(n_pages, ts, pt_ref, rb_ref, qa_ref, key_ref, thr_ref, jst_ref, kn_ref,
                          vn_ref, ckt_hbm, cvt_hbm, out_ref, k_buf, v_buf, sems, strip_ref):
    b = pl.program_id(0)
    past = n_pages * PAGE_SIZE
    rows = N_HEADS * ts

    @pl.when(b == 0)
    def _():
        _build_bias_strip(strip_ref, rb_ref, LANES * (strip_ref.shape[1] - 1))

    slot = _page_pipeline(pt_ref, n_pages, [ckt_hbm, cvt_hbm], [k_buf, v_buf], sems)()

    q2 = jnp.concatenate(_group_queries(qa_ref[...], ts), axis=0)
    k_own = _pad_rows(kn_ref[...], PAGE_SIZE).astype(BF16)
    v_own = _pad_rows(vn_ref[...], PAGE_SIZE).astype(BF16)
    s_past = jnp.dot(q2, k_buf[slot].astype(BF16), preferred_element_type=F32)
    s_own = lax.dot_general(q2, k_own, NT_DIMS, preferred_element_type=F32)
    far = strip_ref[:, 0].reshape(rows, LANES)[:, 0:1]
    near = [strip_ref[:, t].reshape(rows, LANES) for t in (1, 2)]
    s = jnp.concatenate([s_past[:, :past - PAGE_SIZE] + far, s_past[:, past - PAGE_SIZE:] + near[0],
                         s_own + near[1]], axis=1)

    qpos = past + lax.broadcasted_iota(I32, (ts, 1), 0)
    valid = _valid_mask(key_ref[...], 0, thr_ref[:, 0:1], jst_ref[:, 0:1], qpos)
    s = jnp.where(valid[None], s.reshape(N_HEADS, ts, past + PAGE_SIZE), NEG).reshape(rows, -1)
    m = jnp.max(s, axis=1, keepdims=True)
    p = jnp.exp(s - m)
    l = jnp.sum(p, axis=1, keepdims=True)
    pb = p.astype(BF16)
    pv = lax.dot_general(pb[:, :past], v_buf[slot].astype(BF16), NT_DIMS, preferred_element_type=F32)
    pv = pv + jnp.dot(pb[:, past:], v_own, preferred_element_type=F32)
    half = GROUP * ts
    carries = [(None, l[n * half:(n + 1) * half], pv[n * half:(n + 1) * half]) for n in range(N_KV_HEADS)]
    _write_attn(out_ref, carries, ts)


def _dsa_sample(page_table, rel_bias, qa, qi, misc, ki_new, k_new, v_new, ckit, ckt, cvt, DB, ts):
    n_pages = page_table.shape[1]
    past = n_pages * PAGE_SIZE
    n_sel = min(TOPK_MAX, (past + ts) // 4)
    lpad = past + PAGE_SIZE
    idx_bits = max(1, (lpad - 1).bit_length())
    ck = LANES * math.gcd(lpad // LANES, 5)
    assert ts % SUBLANES == 0 and ts <= PAGE_SIZE and n_pages >= 1
    blk = lambda w: pl.BlockSpec((ts, w), lambda b, pt: (b, 0))
    whole = lambda w: pl.BlockSpec((DB * ts, w), lambda b, pt: (0, 0))
    hbm = pl.BlockSpec(memory_space=pl.ANY)
    keys, thr, jstar = pl.pallas_call(
        functools.partial(_sample_score_kernel, n_sel, idx_bits, n_pages, ts, ck),
        grid_spec=pltpu.PrefetchScalarGridSpec(
            num_scalar_prefetch=1,
            grid=(DB,),
            in_specs=[blk(N_IDX_HEADS * IDX_DIM), blk(LANES), blk(IDX_DIM), hbm],
            out_specs=[whole(lpad), whole(LANES), whole(LANES)],
            scratch_shapes=[pltpu.VMEM((2, IDX_DIM, past), F32), pltpu.SemaphoreType.DMA((2, 1))]),
        out_shape=[jax.ShapeDtypeStruct((DB * ts, lpad), I32),
                   jax.ShapeDtypeStruct((DB * ts, LANES), I32),
                   jax.ShapeDtypeStruct((DB * ts, LANES), I32)],
        compiler_params=_cparams(1),
        name="sample_score",
    )(page_table, qi, misc, ki_new, ckit)
    return pl.pallas_call(
        functools.partial(_sample_attend_kernel, n_pages, ts),
        grid_spec=pltpu.PrefetchScalarGridSpec(
            num_scalar_prefetch=1,
            grid=(DB,),
            in_specs=[pl.BlockSpec(memory_space=pltpu.SMEM), blk(ATT_WIDTH), blk(lpad), blk(LANES),
                      blk(LANES), blk(LANES), blk(LANES), hbm, hbm],
            out_specs=blk(ATT_WIDTH),
            scratch_shapes=[pltpu.VMEM((2, LANES, past), F32), pltpu.VMEM((2, LANES, past), F32),
                            pltpu.SemaphoreType.DMA((2, 2)),
                            pltpu.VMEM((N_HEADS, 3, ts, LANES), F32)]),
        out_shape=jax.ShapeDtypeStruct((DB * ts, ATT_WIDTH), F32),
        compiler_params=_cparams(1),
        name="sample_attend",
    )(page_table, rel_bias, qa, keys, thr, jstar, k_new, v_new, ckt, cvt)


def _pack_layer_weights(w_in, b_i, b_f, w_out, w_up, w_down):
    D = w_in.shape[0]
    sizes = (ATT_WIDTH, N_KV_HEADS * HEAD_DIM, N_KV_HEADS * HEAD_DIM, N_IDX_HEADS * IDX_DIM, IDX_DIM,
             N_IDX_HEADS, M_WIDTH, M_WIDTH, M_WIDTH, M_WIDTH, M_HEADS, M_HEADS)
    assert w_in.shape[1] == sum(sizes)
    pts = np.cumsum((0,) + sizes)
    seg = [w_in[:, pts[i]:pts[i + 1]] for i in range(len(sizes))]
    qa, k, v, qi, ki, wi, qm, km, vm, om, im, fm = seg
    perm = np.asarray(HEAD_PERM)
    qa = qa.reshape(D, N_HEADS, HEAD_DIM)[:, perm].reshape(D, ATT_WIDTH)
    misc = jnp.concatenate([wi, im, fm, jnp.zeros((D, LANES - N_IDX_HEADS - 2 * M_HEADS), w_in.dtype)], axis=1)
    wp = jnp.concatenate([qa, k, v, qi, ki, ki, misc, qm, km, vm, om], axis=1).astype(BF16)
    assert wp.shape[1] == N_PACK
    gate_bias = jnp.concatenate([jnp.zeros((MISC_I,), F32), b_i.astype(F32), b_f.astype(F32),
                                 jnp.zeros((LANES - MISC_F - M_HEADS,), F32)]).reshape(1, LANES)
    woa = w_out[:ATT_WIDTH].reshape(N_HEADS, HEAD_DIM, -1)[perm].reshape(ATT_WIDTH, -1).astype(BF16)
    woh = w_out[ATT_WIDTH:].astype(BF16)
    wt = jnp.concatenate([k.T, v.T, ki.T, ki.T], axis=0).astype(BF16)
    return wp, wt, gate_bias, woa, woh, w_up.astype(BF16), w_down.astype(BF16)


def _layer(x, packed, g1, g2, mnorm, rel_bias, gf, final_norm, past):
    wp, wt, gate_bias, woa, woh, wup, wdn = packed
    B, T, D = x.shape
    x2 = x.reshape(B * T, D)
    tm = math.gcd(T if past is None else B * T, 512)
    kv_w = N_KV_HEADS * HEAD_DIM
    if past is None:
        (qa, qi, misc, qm, km, vm, om, kt, vt, kit, ktb, vtb, ki2tb) = _inproj(
            x2, g1.reshape(1, D), wp, wt, B, T, tm)
        attn = _dsa_prompt(rel_bias, qa, qi, misc, ki2tb, ktb, vtb, B, T)
        state = None
        k_new = kt.reshape(B, N_KV_HEADS, HEAD_DIM, T).transpose(0, 3, 1, 2)
        v_new = vt.reshape(B, N_KV_HEADS, HEAD_DIM, T).transpose(0, 3, 1, 2)
        ki_new = kit.transpose(0, 2, 1)
    else:
        (qa, qi, misc, qm, km, vm, om, k, v, ki) = _inproj(x2, g1.reshape(1, D), wp, None, B, T, tm)
        page_table, cache_k, cache_v, cache_kidx, c0, n0, m0 = past
        n_pool = cache_k.shape[0]
        ckt = cache_k.transpose(0, 2, 3, 1).reshape(n_pool, kv_w, PAGE_SIZE)
        cvt = cache_v.transpose(0, 2, 3, 1).reshape(n_pool, kv_w, PAGE_SIZE)
        ckit = cache_kidx.transpose(0, 2, 1)
        attn = _dsa_sample(page_table, rel_bias, qa, qi, misc, ki, k, v, ckit, ckt, cvt, B, T)
        state = (c0, n0, jnp.broadcast_to(m0[..., None], m0.shape + (LANES,)))
        k_new = k.reshape(B, T, N_KV_HEADS, HEAD_DIM)
        v_new = v.reshape(B, T, N_KV_HEADS, HEAD_DIM)
        ki_new = ki.reshape(B, T, IDX_DIM)
    h, c_new, n_new, m_new = _mlstm(qm, km, vm, om, misc, gate_bias, mnorm.reshape(1, M_WIDTH), state,
                                    B, T, BF16 if past is None else F32)
    y = _post(x2, attn, h, woa, woh, g2.reshape(1, D), wup, wdn, gf.reshape(1, D), final_norm, tm)
    return (y.reshape(B, T, D), k_new, v_new, ki_new, c_new, n_new, m_new[..., 0])


def kernel(x_prompt, x_sample, cache_k, cache_v, cache_kidx, page_table, state_C, state_n, state_m,
           w_in, b_igate, b_fgate, mlstm_norm, rel_bias, w_out, norm1, norm2, w_up, w_down, norm_f):
    depth = w_in.shape[0]
    xp, xs = x_prompt, x_sample
    outs_p, outs_s = [], []
    for l in range(depth):
        packed = _pack_layer_weights(w_in[l], b_igate[l], b_fgate[l], w_out[l], w_up[l], w_down[l])
        last = l == depth - 1
        common = (packed, norm1[l], norm2[l], mlstm_norm[l], rel_bias, norm_f, last)
        rp = _layer(xp, *common, None)
        rs = _layer(xs, *common, (page_table, cache_k[l], cache_v[l], cache_kidx[l],
                                  state_C[l], state_n[l], state_m[l]))
        xp, xs = rp[0], rs[0]
        outs_p.append(rp[1:])
        outs_s.append(rs[1:])
    stack = lambda outs, i: jnp.stack([o[i] for o in outs])
    return ((xp, xs) + tuple(stack(outs_p, i) for i in range(6))
            + tuple(stack(outs_s, i) for i in range(6)))
```

```python
import functools
import math

import numpy as np
import jax
import jax.numpy as jnp
from jax import lax
from jax.experimental import pallas as pl
from jax.experimental.pallas import tpu as pltpu

F32 = jnp.float32
BF16 = jnp.bfloat16
I32 = jnp.int32

N_HEADS = 8
HEAD_DIM = 64
N_KV_HEADS = 2
GROUP = N_HEADS // N_KV_HEADS
N_IDX_HEADS = 8
IDX_DIM = 64
TOPK_MAX = 256
N_BUCKETS = 32
MAX_DISTANCE = 128
M_HEADS = 4
M_HEAD_DIM = 128
PAGE_SIZE = 128
EPS = 1e-6
NEG = -1e30
ATT_WIDTH = N_HEADS * HEAD_DIM
M_WIDTH = M_HEADS * M_HEAD_DIM

LANES = 128
SUBLANES = 8
VMEM_LIMIT = 56 * 1024 * 1024

C_QA = 0
C_K = C_QA + ATT_WIDTH
C_V = C_K + LANES
C_QI = C_V + LANES
C_KI2 = C_QI + N_IDX_HEADS * IDX_DIM
C_MISC = C_KI2 + LANES
C_QM = C_MISC + LANES
C_KM = C_QM + M_WIDTH
C_VM = C_KM + M_WIDTH
C_OM = C_VM + M_WIDTH
N_PACK = C_OM + M_WIDTH
MISC_W = 0
MISC_I = 8
MISC_F = 12

HEAD_PERM = (0, 4, 1, 5, 2, 6, 3, 7)

TQ = 128
CKP = 256
ML = 128

NT_DIMS = (((1,), (1,)), ((), ()))
TN_DIMS = (((0,), (0,)), ((), ()))


def _bucket_bounds():
    max_exact = N_BUCKETS // 2
    scale = (N_BUCKETS - max_exact) / math.log(MAX_DISTANCE / max_exact)

    def bucket(n, dt):
        if n < max_exact:
            return n
        val = np.log(np.asarray(max(n, 1), dt) / dt(max_exact)) * dt(scale)
        return min(max_exact + int(val), N_BUCKETS - 1)

    table = [bucket(n, np.float32) for n in range(MAX_DISTANCE + 2)]
    assert table == [bucket(n, np.float64) for n in range(MAX_DISTANCE + 2)]
    assert table[MAX_DISTANCE] == N_BUCKETS - 1
    return [next(d for d, b in enumerate(table) if b >= k) for k in range(N_BUCKETS)]


BUCKET_BOUNDS = _bucket_bounds()


def _cparams(n_axes):
    return pltpu.CompilerParams(dimension_semantics=("arbitrary",) * n_axes,
                                vmem_limit_bytes=VMEM_LIMIT)


def _const_spec(shape):
    nd = len(shape)
    return pl.BlockSpec(shape, lambda *_: (0,) * nd, pipeline_mode=pl.Buffered(1))


def _rms(x, g):
    return x * lax.rsqrt(jnp.mean(x * x, axis=-1, keepdims=True) + EPS) * g


def _inproj_mlstm(mm, misc_ref, qm_ref, km_ref, vm_ref, om_ref):
    misc_ref[...] = mm(C_MISC, LANES)
    qm_ref[...] = mm(C_QM, M_WIDTH).astype(qm_ref.dtype)
    km_ref[...] = (mm(C_KM, M_WIDTH) * (M_HEAD_DIM ** -0.5)).astype(km_ref.dtype)
    vm_ref[...] = mm(C_VM, M_WIDTH).astype(vm_ref.dtype)
    om_ref[...] = mm(C_OM, M_WIDTH)


def _inproj_rows_kernel(x_ref, g_ref, w_ref, misc_ref, qm_ref, km_ref, vm_ref, om_ref,
                        qa_ref, qi_ref, k_ref, v_ref, ki_ref):
    ub = _rms(x_ref[...], g_ref[...]).astype(BF16)
    mm = lambda c0, n: jnp.dot(ub, w_ref[:, c0:c0 + n], preferred_element_type=F32)
    _inproj_mlstm(mm, misc_ref, qm_ref, km_ref, vm_ref, om_ref)
    qa_ref[...] = mm(C_QA, ATT_WIDTH) * (HEAD_DIM ** -0.5)
    qi_ref[...] = mm(C_QI, N_IDX_HEADS * IDX_DIM)
    k_ref[...] = mm(C_K, LANES)
    v_ref[...] = mm(C_V, LANES)
    ki_ref[...] = mm(C_KI2, LANES)[:, :IDX_DIM]


R_K = 0
R_V = R_K + LANES
R_KI = R_V + LANES
R_QA = R_KI + LANES
R_QI = R_QA + ATT_WIDTH
R_MISC = R_QI + N_IDX_HEADS * IDX_DIM
N_TPACK = R_MISC + LANES


def _inproj_cols_kernel(x_ref, g_ref, w_ref, wt_ref, misc_ref, qm_ref, km_ref, vm_ref, om_ref,
                        kb_ref, kib_ref, kt_ref, vt_ref, kit_ref, vtb_ref, qat_ref, qit_ref,
                        misct_ref):
    ub = _rms(x_ref[...], g_ref[...]).astype(BF16)
    mm = lambda c0, n: jnp.dot(ub, w_ref[:, c0:c0 + n], preferred_element_type=F32)
    _inproj_mlstm(mm, misc_ref, qm_ref, km_ref, vm_ref, om_ref)
    kb_ref[...] = mm(C_K, LANES).astype(BF16)
    kib_ref[...] = mm(C_KI2, LANES)[:, :IDX_DIM].astype(BF16)

    def mt(r0, n):
        return lax.dot_general(wt_ref[r0:r0 + n, :], ub, NT_DIMS, preferred_element_type=F32)

    kt_ref[0] = mt(R_K, LANES)
    vt = mt(R_V, LANES)
    vt_ref[0] = vt
    for j in range(vtb_ref.shape[1]):
        vtb_ref[0, j] = vt[:, j * CKP:(j + 1) * CKP].astype(BF16)
    kit_ref[0] = mt(R_KI, IDX_DIM)
    qat_ref[0] = (mt(R_QA, ATT_WIDTH) * (HEAD_DIM ** -0.5)).astype(BF16)
    qit_ref[0] = mt(R_QI, N_IDX_HEADS * IDX_DIM).astype(BF16)
    misct_ref[0] = mt(R_MISC, LANES)


def _inproj(x2, g1, wp, wt, B, T, tm):
    R, D = x2.shape
    assert R == B * T and R % tm == 0
    mdt = F32 if wt is None else BF16
    row = lambda i: (i, 0)
    outs = [(LANES, F32), (M_WIDTH, mdt), (M_WIDTH, mdt), (M_WIDTH, mdt), (M_WIDTH, F32)]
    in_specs = [pl.BlockSpec((tm, D), row), _const_spec((1, D)), _const_spec((D, N_PACK))]
    if wt is None:
        kern, args = _inproj_rows_kernel, (x2, g1, wp)
        outs += [(ATT_WIDTH, F32), (N_IDX_HEADS * IDX_DIM, F32), (LANES, F32), (LANES, F32), (IDX_DIM, F32)]
    else:
        kern, args = _inproj_cols_kernel, (x2, g1, wp, wt)
        in_specs.append(_const_spec(wt.shape))
        outs += [(LANES, BF16), (IDX_DIM, BF16)]
    out_specs = [pl.BlockSpec((tm, w), row) for w, _ in outs]
    out_shape = [jax.ShapeDtypeStruct((R, w), dt) for w, dt in outs]
    if wt is not None:
        assert T % tm == 0 and tm % CKP == 0
        tpb, cpt = T // tm, tm // CKP
        cols = lambda i: (i // tpb, 0, i % tpb)
        for w, dt in ((LANES, F32), (LANES, F32), (IDX_DIM, F32)):
            out_specs.append(pl.BlockSpec((1, w, tm), cols))
            out_shape.append(jax.ShapeDtypeStruct((B, w, T), dt))
        out_specs.append(pl.BlockSpec((1, cpt, LANES, CKP), lambda i: (i // tpb, i % tpb, 0, 0)))
        out_shape.append(jax.ShapeDtypeStruct((B, T // CKP, LANES, CKP), BF16))
        for w, dt in ((ATT_WIDTH, BF16), (N_IDX_HEADS * IDX_DIM, BF16), (LANES, F32)):
            out_specs.append(pl.BlockSpec((1, w, tm), cols))
            out_shape.append(jax.ShapeDtypeStruct((B, w, T), dt))
    return pl.pallas_call(
        kern,
        grid=(R // tm,),
        in_specs=in_specs,
        out_specs=out_specs,
        out_shape=out_shape,
        compiler_params=_cparams(1),
        name="inproj",
    )(*args)


def _post_kernel(ff_chunk, final_norm, x_ref, a_ref, h_ref, woa_ref, woh_ref, g2_ref, wup_ref,
                 wdn_ref, gf_ref, y_ref):
    mix = jnp.dot(a_ref[...].astype(BF16), woa_ref[...], preferred_element_type=F32)
    mix = mix + jnp.dot(h_ref[...].astype(BF16), woh_ref[...], preferred_element_type=F32)
    hres = x_ref[...] + mix
    f = _rms(hres, g2_ref[...]).astype(BF16)
    acc = hres
    for c0 in range(0, wup_ref.shape[1], ff_chunk):
        up = jnp.dot(f, wup_ref[:, c0:c0 + ff_chunk], preferred_element_type=F32)
        r = jnp.maximum(up, 0.0)
        acc = acc + jnp.dot((r * r).astype(BF16), wdn_ref[c0:c0 + ff_chunk, :],
                            preferred_element_type=F32)
    y_ref[...] = _rms(acc, gf_ref[...]) if final_norm else acc


def _post(x2, attn, h, woa, woh, g2, wup, wdn, gf, final_norm, tm):
    R, D = x2.shape
    dff = wup.shape[1]
    assert R % tm == 0
    row = lambda i: (i, 0)
    return pl.pallas_call(
        functools.partial(_post_kernel, min(dff, 1024), final_norm),
        grid=(R // tm,),
        in_specs=[pl.BlockSpec((tm, D), row), pl.BlockSpec((tm, ATT_WIDTH), row),
                  pl.BlockSpec((tm, M_WIDTH), row), _const_spec(woa.shape), _const_spec(woh.shape),
                  _const_spec((1, D)), _const_spec(wup.shape), _const_spec(wdn.shape),
                  _const_spec((1, D))],
        out_specs=pl.BlockSpec((tm, D), row),
        out_shape=jax.ShapeDtypeStruct((R, D), F32),
        compiler_params=_cparams(1),
        name="post",
    )(x2, attn, h, woa, woh, g2, wup, wdn, gf)


def _log_sigmoid(x):
    return -(jnp.maximum(-x, 0.0) + jnp.log1p(jnp.exp(-jnp.abs(x))))


def _mlstm_kernel(nvalid, has_state, *refs):
    if has_state:
        (q_ref, k_ref, v_ref, o_ref, misc_ref, gb_ref, mn_ref, c0_ref, n0_ref, m0_ref,
         h_ref, c_ref, n_ref, m_ref) = refs
    else:
        (q_ref, k_ref, v_ref, o_ref, misc_ref, gb_ref, mn_ref,
         h_ref, c_ref, n_ref, m_ref) = refs
    L = ML

    @pl.when(pl.program_id(1) == 0)
    def _():
        if has_state:
            c_ref[...] = c0_ref[...]
            n_ref[...] = n0_ref[...]
            m_ref[...] = m0_ref[...]
        else:
            c_ref[...] = jnp.zeros_like(c_ref)
            n_ref[...] = jnp.zeros_like(n_ref)
            m_ref[...] = jnp.zeros_like(m_ref)

    def padded(ref, dt):
        x = ref[...].astype(dt)
        if nvalid == L:
            return x
        return jnp.concatenate([x, jnp.zeros((L - nvalid, x.shape[1]), dt)], axis=0)

    lane = lax.broadcasted_iota(I32, (L, LANES), 1)
    rowi = lax.broadcasted_iota(I32, (L, LANES), 0)
    is_i = (lane >= MISC_I) & (lane < MISC_I + M_HEADS)
    is_f = (lane >= MISC_F) & (lane < MISC_F + M_HEADS)
    gx = padded(misc_ref, F32) + gb_ref[...]
    gates = jnp.where(is_f, _log_sigmoid(gx), gx)
    if nvalid != L:
        gates = jnp.where(rowi < nvalid, gates, jnp.where(is_i, NEG, 0.0))

    r2 = lax.broadcasted_iota(I32, (L, L), 0)
    c2 = lax.broadcasted_iota(I32, (L, L), 1)
    tril = r2 >= c2
    bcum = jnp.dot(tril.astype(F32), gates, precision=lax.Precision.HIGHEST,
                   preferred_element_type=F32)
    z = jnp.where(is_i, gates, bcum)
    sr = lax.broadcasted_iota(I32, (SUBLANES, LANES), 0)
    sc = lax.broadcasted_iota(I32, (SUBLANES, LANES), 1)
    sel = (sc == jnp.where(sr < M_HEADS, MISC_F + sr, MISC_I + sr - M_HEADS)).astype(F32)
    rows = lax.dot_general(sel, z, NT_DIMS, precision=lax.Precision.HIGHEST,
                           preferred_element_type=F32)

    qb = padded(q_ref, BF16)
    kb = padded(k_ref, BF16)
    vb = padded(v_ref, BF16)
    ob = padded(o_ref, F32)

    for hd in range(M_HEADS):
        sl = slice(hd * M_HEAD_DIM, (hd + 1) * M_HEAD_DIM)
        q, k, v = qb[:, sl], kb[:, sl], vb[:, sl]
        bcol = z[:, MISC_F + hd:MISC_F + hd + 1]
        icol = z[:, MISC_I + hd:MISC_I + hd + 1]
        brow = rows[hd:hd + 1, :]
        irow = rows[M_HEADS + hd:M_HEADS + hd + 1, :]
        m_prev = m_ref[0, hd:hd + 1, 0:1]
        s_prev = c_ref[0, hd]
        n_prev = n_ref[0, hd:hd + 1, :]

        g = bcol + m_prev
        dm = jnp.where(tril, bcol - brow + irow, NEG)
        mt = jnp.maximum(g, jnp.max(dm, axis=1, keepdims=True))
        dw = jnp.exp(dm - mt)
        gw = jnp.exp(g - mt)
        qk = lax.dot_general(q, k, NT_DIMS, preferred_element_type=F32) * dw
        num = gw * jnp.dot(q, s_prev.astype(BF16), preferred_element_type=F32)
        num = num + jnp.dot(qk.astype(BF16), v, preferred_element_type=F32)
        den = gw * jnp.sum(q.astype(F32) * n_prev, axis=1, keepdims=True)
        den = den + jnp.sum(qk, axis=1, keepdims=True)
        hh = num / jnp.maximum(jnp.abs(den), jnp.exp(-mt))
        hh = hh * lax.rsqrt(jnp.mean(hh * hh, axis=-1, keepdims=True) + EPS)
        hh = hh * mn_ref[:, sl] * jax.nn.sigmoid(ob[:, sl])
        h_ref[:, sl] = hh[:nvalid].astype(h_ref.dtype)

        b_last = bcol[L - 1:L, :]
        g_last = b_last + m_prev
        a = b_last - bcol + icol
        m_new = jnp.maximum(g_last, jnp.max(a, axis=0, keepdims=True))
        aw = jnp.exp(a - m_new)
        sw = jnp.exp(g_last - m_new)
        ak = aw * k.astype(F32)
        c_ref[0, hd] = sw * s_prev + lax.dot_general(ak.astype(BF16), v, TN_DIMS,
                                                     preferred_element_type=F32)
        n_ref[0, hd:hd + 1, :] = sw * n_prev + jnp.sum(ak, axis=0, keepdims=True)
        m_ref[0, hd:hd + 1, :] = jnp.broadcast_to(m_new, (1, LANES))


def _mlstm(qm, km, vm, om, misc, gate_bias, mnorm, state, B, T, hdt):
    nvalid = min(T, ML)
    assert T % nvalid == 0
    nc = T // nvalid
    blk = lambda w: pl.BlockSpec((nvalid, w), lambda b, c: (b * nc + c, 0))
    st_specs = [pl.BlockSpec((1, M_HEADS, M_HEAD_DIM, M_HEAD_DIM), lambda b, c: (b, 0, 0, 0)),
                pl.BlockSpec((1, M_HEADS, M_HEAD_DIM), lambda b, c: (b, 0, 0)),
                pl.BlockSpec((1, M_HEADS, LANES), lambda b, c: (b, 0, 0))]
    in_specs = [blk(M_WIDTH), blk(M_WIDTH), blk(M_WIDTH), blk(M_WIDTH), blk(LANES),
                pl.BlockSpec((1, LANES), lambda b, c: (0, 0)),
                pl.BlockSpec((1, M_WIDTH), lambda b, c: (0, 0))]
    args = [qm, km, vm, om, misc, gate_bias, mnorm]
    if state is not None:
        in_specs += st_specs
        args += list(state)
    return pl.pallas_call(
        functools.partial(_mlstm_kernel, nvalid, state is not None),
        grid=(B, nc),
        in_specs=in_specs,
        out_specs=[blk(M_WIDTH)] + st_specs,
        out_shape=[jax.ShapeDtypeStruct((B * T, M_WIDTH), hdt),
                   jax.ShapeDtypeStruct((B, M_HEADS, M_HEAD_DIM, M_HEAD_DIM), F32),
                   jax.ShapeDtypeStruct((B, M_HEADS, M_HEAD_DIM), F32),
                   jax.ShapeDtypeStruct((B, M_HEADS, LANES), F32)],
        compiler_params=_cparams(2),
        name="mlstm",
    )(*args)


def _sortable_key(score):
    bits = lax.bitcast_convert_type(score, I32)
    return bits ^ (lax.shift_right_arithmetic(bits, 31) & 0x7FFFFFFF)


def _build_bias_strip(strip_ref, rb_ref, off, key_axis):
    _, ntiles, rows, _ = strip_ref.shape
    i = lax.broadcasted_iota(I32, (rows, LANES), 1 - key_axis)
    x = lax.broadcasted_iota(I32, (rows, LANES), key_axis)
    for t in range(ntiles):
        dist = i + (off - LANES * t) - x
        for h in range(N_HEADS):
            val = jnp.full((rows, LANES), rb_ref[0, h], F32)
            for b in range(1, N_BUCKETS):
                val = jnp.where(dist >= BUCKET_BOUNDS[b], rb_ref[b, h], val)
            strip_ref[h, t] = val


def _counter(key_ref, nk, ck, key_axis):
    def count(pred):
        def body(c, acc):
            c0 = pl.multiple_of(c * ck, ck)
            kk = key_ref[:, pl.ds(c0, ck)] if key_axis == 1 else key_ref[pl.ds(c0, ck), :]
            idx = c0 + lax.broadcasted_iota(I32, kk.shape, key_axis)
            hit = jnp.where(pred(kk, idx), 1, 0)
            if key_axis == 1:
                for j in range(ck // LANES):
                    acc = acc + hit[:, j * LANES:(j + 1) * LANES]
                return acc
            return acc + jnp.sum(hit.reshape(ck // SUBLANES, SUBLANES, hit.shape[1]), axis=0)

        nq = key_ref.shape[1 - key_axis]
        acc0 = jnp.zeros((nq, LANES) if key_axis == 1 else (SUBLANES, nq), I32)
        return jnp.sum(lax.fori_loop(0, nk, body, acc0), axis=key_axis, keepdims=True)

    return count


def _select_topk(count, qshape, n_sel, idx_bits):
    imin, imax = jnp.iinfo(jnp.int32).min, jnp.iinfo(jnp.int32).max

    def bit_step(i, thr):
        cand = thr + lax.shift_left(jnp.int32(1), 31 - i)
        return jnp.where(count(lambda kk, idx: kk >= cand) >= n_sel, cand, thr)

    thr = lax.fori_loop(0, 32, bit_step, jnp.full(qshape, imin, I32))
    n_gt = count(lambda kk, idx: kk > thr)
    n_ge = count(lambda kk, idx: kk >= thr)
    need = n_sel - n_gt

    def tie_search(_):
        def idx_step(i, lo):
            cand = lo + lax.shift_left(jnp.int32(1), idx_bits - 1 - i)
            cnt = count(lambda kk, idx: jnp.where(kk == thr, idx, imax) < cand)
            return jnp.where(cnt < need, cand, lo)

        return lax.fori_loop(0, idx_bits, idx_step, jnp.zeros(qshape, I32))

    all_ties = jnp.full(qshape, imax, I32)
    jstar = lax.cond(jnp.max(n_ge) > n_sel, tie_search, lambda _: all_ties, 0)
    jstar = jnp.where(n_ge > n_sel, jstar, all_ties)
    return thr, jstar


def _valid_mask(keys, idx, thr, jstar, qpos):
    sel = jnp.where(keys > thr, 1, jnp.where(keys == thr, jnp.where(idx <= jstar, 1, 0), 0))
    return jnp.where(idx <= qpos, sel, 0) > 0


def _group_queries(qa, tq):
    lane = lax.broadcasted_iota(I32, (tq, LANES), 1)
    out = []
    for n in range(N_KV_HEADS):
        keep = (lane < HEAD_DIM) if n == 0 else (lane >= HEAD_DIM)
        tiles = [jnp.where(keep, qa[:, j * LANES:(j + 1) * LANES], jnp.zeros((), qa.dtype))
                 for j in range(GROUP)]
        out.append(jnp.concatenate(tiles, axis=0).astype(BF16))
    return out


def _write_attn(out_ref, carries, tq):
    lane = lax.broadcasted_iota(I32, (tq, LANES), 1)
    res = [acc / l for (_, l, acc) in carries]
    for j in range(GROUP):
        tile = jnp.where(lane < HEAD_DIM, res[0][j * tq:(j + 1) * tq], res[1][j * tq:(j + 1) * tq])
        out_ref[:, j * LANES:(j + 1) * LANES] = tile.astype(out_ref.dtype)


def _dsa_prompt_kernel(n_sel, idx_bits, rb_ref, qat_ref, qit_ref, misct_ref, ki_ref, k_ref, vt_ref,
                       out_ref, key_ref, strip_ref):
    qb = pl.program_id(1)
    q0 = qb * TQ
    nk = (q0 + TQ + CKP - 1) // CKP
    tiles_per_chunk = CKP // LANES
    strip_off = LANES * (strip_ref.shape[1] - tiles_per_chunk)

    @pl.when((pl.program_id(0) == 0) & (qb == 0))
    def _():
        _build_bias_strip(strip_ref, rb_ref, strip_off, 0)

    qpos = q0 + lax.broadcasted_iota(I32, (1, TQ), 1)

    qit = qit_ref[0]
    qstack = jnp.concatenate([qit[h * IDX_DIM:(h + 1) * IDX_DIM] for h in range(N_IDX_HEADS)],
                             axis=1)
    w = misct_ref[0, MISC_W:MISC_W + N_IDX_HEADS, :] * (N_IDX_HEADS ** -0.5 * IDX_DIM ** -0.5)

    def score_chunk(c, _):
        c0 = pl.multiple_of(c * CKP, CKP)
        d = jnp.dot(ki_ref[pl.ds(c0, CKP), :], qstack, preferred_element_type=F32)
        d = jnp.maximum(d, 0.0)
        s = jnp.zeros((CKP, TQ), F32)
        for h in range(N_IDX_HEADS):
            s = s + d[:, h * TQ:(h + 1) * TQ] * w[h:h + 1, :]
        idx = c0 + lax.broadcasted_iota(I32, (CKP, TQ), 0)
        key_ref[pl.ds(c0, CKP), :] = _sortable_key(jnp.where(idx <= qpos, s, NEG))
        return 0

    lax.fori_loop(0, nk, score_chunk, 0)

    thr, jstar = _select_topk(_counter(key_ref, nk, CKP, 0), (1, TQ), n_sel, idx_bits)

    qat = qat_ref[0]
    zeros = jnp.zeros((HEAD_DIM, TQ), qat.dtype)
    qgroups = []
    for n in range(N_KV_HEADS):
        tiles = []
        for g in range(GROUP):
            h = n * GROUP + g
            x = qat[h * HEAD_DIM:(h + 1) * HEAD_DIM]
            tiles.append(jnp.concatenate([x, zeros] if n == 0 else [zeros, x], axis=0))
        qgroups.append(jnp.concatenate(tiles, axis=1))

    def attend_chunk(c, carries):
        c0 = pl.multiple_of(c * CKP, CKP)
        idx = c0 + lax.broadcasted_iota(I32, (CKP, TQ), 0)
        valid = _valid_mask(key_ref[pl.ds(c0, CKP), :], idx, thr, jstar, qpos)
        kc = k_ref[pl.ds(c0, CKP), :]
        vct = vt_ref[0, c]
        t0 = jnp.maximum(strip_off // LANES - (qb - tiles_per_chunk * c), 0)
        out = []
        for n in range(N_KV_HEADS):
            m_old, l_old, acc = carries[n]
            s = jnp.dot(kc, qgroups[n], preferred_element_type=F32)
            parts = []
            for g in range(GROUP):
                bias = jnp.concatenate([strip_ref[n * GROUP + g, t0 + j] for j in range(tiles_per_chunk)],
                                       axis=0)
                parts.append(jnp.where(valid, s[:, g * TQ:(g + 1) * TQ] + bias, NEG))
            sm = jnp.concatenate(parts, axis=1)
            m_new = jnp.maximum(m_old, jnp.max(sm, axis=0, keepdims=True))
            alpha = jnp.exp(m_old - m_new)
            p = jnp.exp(sm - m_new)
            l_new = alpha * l_old + jnp.sum(p, axis=0, keepdims=True)
            acc = alpha * acc + jnp.dot(vct, p.astype(BF16), preferred_element_type=F32)
            out.append((m_new, l_new, acc))
        return tuple(out)

    init = (jnp.full((1, GROUP * TQ), NEG, F32), jnp.zeros((1, GROUP * TQ), F32),
            jnp.zeros((LANES, GROUP * TQ), F32))
    carries = lax.fori_loop(0, nk, attend_chunk, (init, init))
    res = [acc / l for (_, l, acc) in carries]
    row = lax.broadcasted_iota(I32, (LANES, TQ), 0)
    for j in range(GROUP):
        cols = slice(j * TQ, (j + 1) * TQ)
        tile_t = jnp.where(row < HEAD_DIM, res[0][:, cols], res[1][:, cols])
        out_ref[:, j * LANES:(j + 1) * LANES] = tile_t.T.astype(out_ref.dtype)


def _dsa_prompt(rel_bias, qat, qit, misct, kib, kb, vtb, B, T):
    assert T % CKP == 0 and T % TQ == 0
    nq = T // TQ
    n_sel = min(TOPK_MAX, T // 4)
    idx_bits = max(1, (T - 1).bit_length())
    qcols = lambda w: pl.BlockSpec((1, w, TQ), lambda b, q: (b, 0, q))
    seq = lambda w: pl.BlockSpec((T, w), lambda b, q: (b, 0))
    return pl.pallas_call(
        functools.partial(_dsa_prompt_kernel, n_sel, idx_bits),
        grid=(B, nq),
        in_specs=[pl.BlockSpec(memory_space=pltpu.SMEM), qcols(ATT_WIDTH), qcols(N_IDX_HEADS * IDX_DIM),
                  qcols(LANES), seq(IDX_DIM), seq(LANES),
                  pl.BlockSpec((1, T // CKP, LANES, CKP), lambda b, q: (b, 0, 0, 0))],
        out_specs=pl.BlockSpec((TQ, ATT_WIDTH), lambda b, q: (b * nq + q, 0)),
        out_shape=jax.ShapeDtypeStruct((B * T, ATT_WIDTH), BF16),
        scratch_shapes=[pltpu.VMEM((T, TQ), I32),
                        pltpu.VMEM((N_HEADS, 3 + CKP // LANES, LANES, TQ), F32)],
        compiler_params=_cparams(2),
        name="dsa_prompt",
    )(rel_bias, qat, qit, misct, kib, kb, vtb)


def _page_pipeline(pt_ref, n_pages, caches, bufs, sems):
    def copies(bb, sl, j):
        pid = pt_ref[bb, j]
        cols = pl.ds(pl.multiple_of(j * PAGE_SIZE, PAGE_SIZE), PAGE_SIZE)
        return [pltpu.make_async_copy(c.at[pid], buf.at[sl, :, cols], sems.at[sl, i])
                for i, (c, buf) in enumerate(zip(caches, bufs))]

    def start_all(bb, sl):
        def body(j, _):
            for cp in copies(bb, sl, j):
                cp.start()
            return 0
        lax.fori_loop(0, n_pages, body, 0)

    def wait_all(bb, sl):
        def body(j, _):
            for cp in copies(bb, sl, j):
                cp.wait()
            return 0
        lax.fori_loop(0, n_pages, body, 0)

    def step():
        b = pl.program_id(0)
        slot = b % 2

        @pl.when(b == 0)
        def _():
            start_all(0, 0)

        @pl.when(b + 1 < pl.num_programs(0))
        def _():
            start_all(b + 1, 1 - slot)

        wait_all(b, slot)
        return slot

    return step


def _pad_rows(x, rows):
    return jnp.concatenate([x, jnp.zeros((rows - x.shape[0], x.shape[1]), x.dtype)], axis=0)


def _sample_score_kernel(n_sel, idx_bits, n_pages, ts, ck, pt_ref, qi_ref, misc_ref, kin_ref,
                         ckit_hbm, key_ref, thr_ref, jst_ref, ki_buf, sems):
    b = pl.program_id(0)
    past = n_pages * PAGE_SIZE
    slot = _page_pipeline(pt_ref, n_pages, [ckit_hbm], [ki_buf], sems)()

    qi = qi_ref[...]
    qstack = jnp.concatenate([qi[:, h * IDX_DIM:(h + 1) * IDX_DIM] for h in range(N_IDX_HEADS)],
                             axis=0).astype(BF16)
    w = misc_ref[:, MISC_W:MISC_W + N_IDX_HEADS] * (N_IDX_HEADS ** -0.5 * IDX_DIM ** -0.5)
    d_past = jnp.dot(qstack, ki_buf[slot].astype(BF16), preferred_element_type=F32)
    d_own = lax.dot_general(qstack, _pad_rows(kin_ref[...], PAGE_SIZE).astype(BF16), NT_DIMS,
                            preferred_element_type=F32)
    d = jnp.maximum(jnp.concatenate([d_past, d_own], axis=1), 0.0)
    s = jnp.zeros((ts, past + PAGE_SIZE), F32)
    for h in range(N_IDX_HEADS):
        s = s + d[h * ts:(h + 1) * ts] * w[:, h:h + 1]
    idx = lax.broadcasted_iota(I32, s.shape, 1)
    qpos = past + lax.broadcasted_iota(I32, (ts, 1), 0)
    key_ref[pl.ds(pl.multiple_of(b * ts, ts), ts), :] = _sortable_key(jnp.where(idx <= qpos, s, NEG))

    @pl.when(b == pl.num_programs(0) - 1)
    def _():
        count = _counter(key_ref, key_ref.shape[1] // ck, ck, 1)
        thr, jstar = _select_topk(count, (key_ref.shape[0], 1), n_sel, idx_bits)
        thr_ref[...] = jnp.broadcast_to(thr, thr_ref.shape)
        jst_ref[...] = jnp.broadcast_to(jstar, jst_ref.shape)


def _sample_attend_kernel(n_pages, ts, pt_ref, rb_ref, qa_ref, key_ref, thr_ref, jst_ref, kn_ref,
                          vn_ref, ckt_hbm, cvt_hbm, out_ref, k_buf, v_buf, sems, strip_ref):
    b = pl.program_id(0)
    past = n_pages * PAGE_SIZE
    rows = N_HEADS * ts

    @pl.when(b == 0)
    def _():
        _build_bias_strip(strip_ref, rb_ref, LANES * (strip_ref.shape[1] - 1), 1)

    slot = _page_pipeline(pt_ref, n_pages, [ckt_hbm, cvt_hbm], [k_buf, v_buf], sems)()

    q2 = jnp.concatenate(_group_queries(qa_ref[...], ts), axis=0)
    k_own = _pad_rows(kn_ref[...], PAGE_SIZE).astype(BF16)
    v_own = _pad_rows(vn_ref[...], PAGE_SIZE).astype(BF16)
    s_past = jnp.dot(q2, k_buf[slot].astype(BF16), preferred_element_type=F32)
    s_own = lax.dot_general(q2, k_own, NT_DIMS, preferred_element_type=F32)
    far = strip_ref[:, 0].reshape(rows, LANES)[:, 0:1]
    near = [strip_ref[:, t].reshape(rows, LANES) for t in (1, 2)]
    s = jnp.concatenate([s_past[:, :past - PAGE_SIZE] + far, s_past[:, past - PAGE_SIZE:] + near[0],
                         s_own + near[1]], axis=1)

    qpos = past + lax.broadcasted_iota(I32, (ts, 1), 0)
    keys = key_ref[...]
    valid = _valid_mask(keys, lax.broadcasted_iota(I32, keys.shape, 1), thr_ref[:, 0:1],
                        jst_ref[:, 0:1], qpos)
    s = jnp.where(valid[None], s.reshape(N_HEADS, ts, past + PAGE_SIZE), NEG).reshape(rows, -1)
    m = jnp.max(s, axis=1, keepdims=True)
    p = jnp.exp(s - m)
    l = jnp.sum(p, axis=1, keepdims=True)
    pb = p.astype(BF16)
    pv = lax.dot_general(pb[:, :past], v_buf[slot].astype(BF16), NT_DIMS, preferred_element_type=F32)
    pv = pv + jnp.dot(pb[:, past:], v_own, preferred_element_type=F32)
    half = GROUP * ts
    carries = [(None, l[n * half:(n + 1) * half], pv[n * half:(n + 1) * half]) for n in range(N_KV_HEADS)]
    _write_attn(out_ref, carries, ts)


def _dsa_sample(page_table, rel_bias, qa, qi, misc, ki_new, k_new, v_new, ckit, ckt, cvt, DB, ts):
    n_pages = page_table.shape[1]
    past = n_pages * PAGE_SIZE
    n_sel = min(TOPK_MAX, (past + ts) // 4)
    lpad = past + PAGE_SIZE
    idx_bits = max(1, (lpad - 1).bit_length())
    ck = LANES * math.gcd(lpad // LANES, 5)
    assert ts % SUBLANES == 0 and ts <= PAGE_SIZE and n_pages >= 1
    blk = lambda w: pl.BlockSpec((ts, w), lambda b, pt: (b, 0))
    whole = lambda w: pl.BlockSpec((DB * ts, w), lambda b, pt: (0, 0))
    hbm = pl.BlockSpec(memory_space=pl.ANY)
    keys, thr, jstar = pl.pallas_call(
        functools.partial(_sample_score_kernel, n_sel, idx_bits, n_pages, ts, ck),
        grid_spec=pltpu.PrefetchScalarGridSpec(
            num_scalar_prefetch=1,
            grid=(DB,),
            in_specs=[blk(N_IDX_HEADS * IDX_DIM), blk(LANES), blk(IDX_DIM), hbm],
            out_specs=[whole(lpad), whole(LANES), whole(LANES)],
            scratch_shapes=[pltpu.VMEM((2, IDX_DIM, past), F32), pltpu.SemaphoreType.DMA((2, 1))]),
        out_shape=[jax.ShapeDtypeStruct((DB * ts, lpad), I32),
                   jax.ShapeDtypeStruct((DB * ts, LANES), I32),
                   jax.ShapeDtypeStruct((DB * ts, LANES), I32)],
        compiler_params=_cparams(1),
        name="sample_score",
    )(page_table, qi, misc, ki_new, ckit)
    return pl.pallas_call(
        functools.partial(_sample_attend_kernel, n_pages, ts),
        grid_spec=pltpu.PrefetchScalarGridSpec(
            num_scalar_prefetch=1,
            grid=(DB,),
            in_specs=[pl.BlockSpec(memory_space=pltpu.SMEM), blk(ATT_WIDTH), blk(lpad), blk(LANES),
                      blk(LANES), blk(LANES), blk(LANES), hbm, hbm],
            out_specs=blk(ATT_WIDTH),
            scratch_shapes=[pltpu.VMEM((2, LANES, past), F32), pltpu.VMEM((2, LANES, past), F32),
                            pltpu.SemaphoreType.DMA((2, 2)),
                            pltpu.VMEM((N_HEADS, 3, ts, LANES), F32)]),
        out_shape=jax.ShapeDtypeStruct((DB * ts, ATT_WIDTH), F32),
        compiler_params=_cparams(1),
        name="sample_attend",
    )(page_table, rel_bias, qa, keys, thr, jstar, k_new, v_new, ckt, cvt)


def _pack_layer_weights(w_in, b_i, b_f, w_out, w_up, w_down):
    D = w_in.shape[0]
    sizes = (ATT_WIDTH, N_KV_HEADS * HEAD_DIM, N_KV_HEADS * HEAD_DIM, N_IDX_HEADS * IDX_DIM, IDX_DIM,
             N_IDX_HEADS, M_WIDTH, M_WIDTH, M_WIDTH, M_WIDTH, M_HEADS, M_HEADS)
    assert w_in.shape[1] == sum(sizes)
    pts = np.cumsum((0,) + sizes)
    seg = [w_in[:, pts[i]:pts[i + 1]] for i in range(len(sizes))]
    qa, k, v, qi, ki, wi, qm, km, vm, om, im, fm = seg
    perm = np.asarray(HEAD_PERM)
    qa = qa.reshape(D, N_HEADS, HEAD_DIM)[:, perm].reshape(D, ATT_WIDTH)
    misc = jnp.concatenate([wi, im, fm, jnp.zeros((D, LANES - N_IDX_HEADS - 2 * M_HEADS), w_in.dtype)], axis=1)
    wp = jnp.concatenate([qa, k, v, qi, ki, ki, misc, qm, km, vm, om], axis=1).astype(BF16)
    assert wp.shape[1] == N_PACK
    gate_bias = jnp.concatenate([jnp.zeros((MISC_I,), F32), b_i.astype(F32), b_f.astype(F32),
                                 jnp.zeros((LANES - MISC_F - M_HEADS,), F32)]).reshape(1, LANES)
    woa = w_out[:ATT_WIDTH].reshape(N_HEADS, HEAD_DIM, -1)[perm].reshape(ATT_WIDTH, -1).astype(BF16)
    woh = w_out[ATT_WIDTH:].astype(BF16)
    wt = jnp.concatenate([seg[1].T, seg[2].T, ki.T, jnp.zeros((LANES - IDX_DIM, D), w_in.dtype),
                          seg[0].T, qi.T, misc.T], axis=0).astype(BF16)
    assert wt.shape[0] == N_TPACK
    return wp, wt, gate_bias, woa, woh, w_up.astype(BF16), w_down.astype(BF16)


def _layer(x, packed, g1, g2, mnorm, rel_bias, gf, final_norm, past):
    wp, wt, gate_bias, woa, woh, wup, wdn = packed
    B, T, D = x.shape
    x2 = x.reshape(B * T, D)
    tm = math.gcd(T if past is None else B * T, 512)
    kv_w = N_KV_HEADS * HEAD_DIM
    if past is None:
        (misc, qm, km, vm, om, kb, kib, kt, vt, kit, vtb, qat, qit, misct) = _inproj(
            x2, g1.reshape(1, D), wp, wt, B, T, tm)
        attn = _dsa_prompt(rel_bias, qat, qit, misct, kib, kb, vtb, B, T)
        state = None
        k_new = kt.reshape(B, N_KV_HEADS, HEAD_DIM, T).transpose(0, 3, 1, 2)
        v_new = vt.reshape(B, N_KV_HEADS, HEAD_DIM, T).transpose(0, 3, 1, 2)
        ki_new = kit.transpose(0, 2, 1)
    else:
        (misc, qm, km, vm, om, qa, qi, k, v, ki) = _inproj(x2, g1.reshape(1, D), wp, None, B, T, tm)
        page_table, cache_k, cache_v, cache_kidx, c0, n0, m0 = past
        n_pool = cache_k.shape[0]
        ckt = cache_k.transpose(0, 2, 3, 1).reshape(n_pool, kv_w, PAGE_SIZE)
        cvt = cache_v.transpose(0, 2, 3, 1).reshape(n_pool, kv_w, PAGE_SIZE)
        ckit = cache_kidx.transpose(0, 2, 1)
        attn = _dsa_sample(page_table, rel_bias, qa, qi, misc, ki, k, v, ckit, ckt, cvt, B, T)
        state = (c0, n0, jnp.broadcast_to(m0[..., None], m0.shape + (LANES,)))
        k_new = k.reshape(B, T, N_KV_HEADS, HEAD_DIM)
        v_new = v.reshape(B, T, N_KV_HEADS, HEAD_DIM)
        ki_new = ki.reshape(B, T, IDX_DIM)
    h, c_new, n_new, m_new = _mlstm(qm, km, vm, om, misc, gate_bias, mnorm.reshape(1, M_WIDTH), state,
                                    B, T, BF16 if past is None else F32)
    y = _post(x2, attn, h, woa, woh, g2.reshape(1, D), wup, wdn, gf.reshape(1, D), final_norm, tm)
    return (y.reshape(B, T, D), k_new, v_new, ki_new, c_new, n_new, m_new[..., 0])


def kernel(x_prompt, x_sample, cache_k, cache_v, cache_kidx, page_table, state_C, state_n, state_m,
           w_in, b_igate, b_fgate, mlstm_norm, rel_bias, w_out, norm1, norm2, w_up, w_down, norm_f):
    depth = w_in.shape[0]
    xp, xs = x_prompt, x_sample
    outs_p, outs_s = [], []
    for l in range(depth):
        packed = _pack_layer_weights(w_in[l], b_igate[l], b_fgate[l], w_out[l], w_up[l], w_down[l])
        last = l == depth - 1
        common = (packed, norm1[l], norm2[l], mlstm_norm[l], rel_bias, norm_f, last)
        rp = _layer(xp, *common, None)
        rs = _layer(xs, *common, (page_table, cache_k[l], cache_v[l], cache_kidx[l],
                                  state_C[l], state_n[l], state_m[l]))
        xp, xs = rp[0], rs[0]
        outs_p.append(rp[1:])
        outs_s.append(rs[1:])
    stack = lambda outs, i: jnp.stack([o[i] for o in outs])
    return ((xp, xs) + tuple(stack(outs_p, i) for i in range(6))
            + tuple(stack(outs_s, i) for i in range(6)))
```

```python
import functools
import math

import numpy as np
import jax
import jax.numpy as jnp
from jax import lax
from jax.experimental import pallas as pl
from jax.experimental.pallas import tpu as pltpu

F32 = jnp.float32
BF16 = jnp.bfloat16
I32 = jnp.int32

N_HEADS = 8
HEAD_DIM = 64
N_KV_HEADS = 2
GROUP = N_HEADS // N_KV_HEADS
N_IDX_HEADS = 8
IDX_DIM = 64
TOPK_MAX = 256
N_BUCKETS = 32
MAX_DISTANCE = 128
M_HEADS = 4
M_HEAD_DIM = 128
PAGE_SIZE = 128
EPS = 1e-6
NEG = -1e30
ATT_WIDTH = N_HEADS * HEAD_DIM
M_WIDTH = M_HEADS * M_HEAD_DIM

LANES = 128
SUBLANES = 8
VMEM_LIMIT = 56 * 1024 * 1024

C_QA = 0
C_K = C_QA + ATT_WIDTH
C_V = C_K + LANES
C_QI = C_V + LANES
C_KI2 = C_QI + N_IDX_HEADS * IDX_DIM
C_MISC = C_KI2 + LANES
C_QM = C_MISC + LANES
C_KM = C_QM + M_WIDTH
C_VM = C_KM + M_WIDTH
C_OM = C_VM + M_WIDTH
N_PACK = C_OM + M_WIDTH
MISC_W = 0
MISC_I = 8
MISC_F = 12

HEAD_PERM = (0, 4, 1, 5, 2, 6, 3, 7)

TQ = 128
CKP = 256
ML = 128

NT_DIMS = (((1,), (1,)), ((), ()))
TN_DIMS = (((0,), (0,)), ((), ()))


def _bucket_bounds():
    max_exact = N_BUCKETS // 2
    scale = (N_BUCKETS - max_exact) / math.log(MAX_DISTANCE / max_exact)

    def bucket(n, dt):
        if n < max_exact:
            return n
        val = np.log(np.asarray(max(n, 1), dt) / dt(max_exact)) * dt(scale)
        return min(max_exact + int(val), N_BUCKETS - 1)

    table = [bucket(n, np.float32) for n in range(MAX_DISTANCE + 2)]
    assert table == [bucket(n, np.float64) for n in range(MAX_DISTANCE + 2)]
    assert table[MAX_DISTANCE] == N_BUCKETS - 1
    return [next(d for d, b in enumerate(table) if b >= k) for k in range(N_BUCKETS)]


BUCKET_BOUNDS = _bucket_bounds()


def _cparams(n_axes):
    return pltpu.CompilerParams(dimension_semantics=("arbitrary",) * n_axes,
                                vmem_limit_bytes=VMEM_LIMIT)


def _const_spec(shape):
    nd = len(shape)
    return pl.BlockSpec(shape, lambda *_: (0,) * nd, pipeline_mode=pl.Buffered(1))


def _rms(x, g):
    return x * lax.rsqrt(jnp.mean(x * x, axis=-1, keepdims=True) + EPS) * g


def _inproj_mlstm(mm, misc_ref, qm_ref, km_ref, vm_ref, om_ref):
    misc_ref[...] = mm(C_MISC, LANES)
    qm_ref[...] = mm(C_QM, M_WIDTH).astype(qm_ref.dtype)
    km_ref[...] = (mm(C_KM, M_WIDTH) * (M_HEAD_DIM ** -0.5)).astype(km_ref.dtype)
    vm_ref[...] = mm(C_VM, M_WIDTH).astype(vm_ref.dtype)
    om_ref[...] = mm(C_OM, M_WIDTH)


def _inproj_rows_kernel(x_ref, g_ref, w_ref, misc_ref, qm_ref, km_ref, vm_ref, om_ref,
                        qa_ref, qi_ref, k_ref, v_ref, ki_ref):
    ub = _rms(x_ref[...], g_ref[...]).astype(BF16)
    mm = lambda c0, n: jnp.dot(ub, w_ref[:, c0:c0 + n], preferred_element_type=F32)
    _inproj_mlstm(mm, misc_ref, qm_ref, km_ref, vm_ref, om_ref)
    qa_ref[...] = mm(C_QA, ATT_WIDTH) * (HEAD_DIM ** -0.5)
    qi_ref[...] = mm(C_QI, N_IDX_HEADS * IDX_DIM)
    k_ref[...] = mm(C_K, LANES)
    v_ref[...] = mm(C_V, LANES)
    ki_ref[...] = mm(C_KI2, LANES)[:, :IDX_DIM]


R_K = 0
R_V = R_K + LANES
R_KI = R_V + LANES
R_QA = R_KI + LANES
R_QI = R_QA + ATT_WIDTH
R_MISC = R_QI + N_IDX_HEADS * IDX_DIM
N_TPACK = R_MISC + LANES


def _inproj_cols_kernel(x_ref, g_ref, w_ref, wt_ref, misc_ref, qm_ref, km_ref, vm_ref, om_ref,
                        kb_ref, kib_ref, kt_ref, vt_ref, kit_ref, vtb_ref, qat_ref, qit_ref,
                        misct_ref):
    ub = _rms(x_ref[...], g_ref[...]).astype(BF16)
    mm = lambda c0, n: jnp.dot(ub, w_ref[:, c0:c0 + n], preferred_element_type=F32)
    _inproj_mlstm(mm, misc_ref, qm_ref, km_ref, vm_ref, om_ref)
    kb_ref[...] = mm(C_K, LANES).astype(BF16)
    kib_ref[...] = mm(C_KI2, LANES)[:, :IDX_DIM].astype(BF16)

    def mt(r0, n):
        return lax.dot_general(wt_ref[r0:r0 + n, :], ub, NT_DIMS, preferred_element_type=F32)

    kt_ref[0] = mt(R_K, LANES)
    vt = mt(R_V, LANES)
    vt_ref[0] = vt
    for j in range(vtb_ref.shape[1]):
        vtb_ref[0, j] = vt[:, j * CKP:(j + 1) * CKP].astype(BF16)
    kit_ref[0] = mt(R_KI, IDX_DIM)
    qat_ref[0] = (mt(R_QA, ATT_WIDTH) * (HEAD_DIM ** -0.5)).astype(BF16)
    qit_ref[0] = mt(R_QI, N_IDX_HEADS * IDX_DIM).astype(BF16)
    misct_ref[0] = mt(R_MISC, LANES)


def _inproj(x2, g1, wp, wt, B, T, tm):
    R, D = x2.shape
    assert R == B * T and R % tm == 0
    mdt = F32 if wt is None else BF16
    row = lambda i: (i, 0)
    outs = [(LANES, F32), (M_WIDTH, mdt), (M_WIDTH, mdt), (M_WIDTH, mdt), (M_WIDTH, F32)]
    in_specs = [pl.BlockSpec((tm, D), row), _const_spec((1, D)), _const_spec((D, N_PACK))]
    if wt is None:
        kern, args = _inproj_rows_kernel, (x2, g1, wp)
        outs += [(ATT_WIDTH, F32), (N_IDX_HEADS * IDX_DIM, F32), (LANES, F32), (LANES, F32), (IDX_DIM, F32)]
    else:
        kern, args = _inproj_cols_kernel, (x2, g1, wp, wt)
        in_specs.append(_const_spec(wt.shape))
        outs += [(LANES, BF16), (IDX_DIM, BF16)]
    out_specs = [pl.BlockSpec((tm, w), row) for w, _ in outs]
    out_shape = [jax.ShapeDtypeStruct((R, w), dt) for w, dt in outs]
    if wt is not None:
        assert T % tm == 0 and tm % CKP == 0
        tpb, cpt = T // tm, tm // CKP
        cols = lambda i: (i // tpb, 0, i % tpb)
        for w, dt in ((LANES, F32), (LANES, F32), (IDX_DIM, F32)):
            out_specs.append(pl.BlockSpec((1, w, tm), cols))
            out_shape.append(jax.ShapeDtypeStruct((B, w, T), dt))
        out_specs.append(pl.BlockSpec((1, cpt, LANES, CKP), lambda i: (i // tpb, i % tpb, 0, 0)))
        out_shape.append(jax.ShapeDtypeStruct((B, T // CKP, LANES, CKP), BF16))
        for w, dt in ((ATT_WIDTH, BF16), (N_IDX_HEADS * IDX_DIM, BF16), (LANES, F32)):
            out_specs.append(pl.BlockSpec((1, w, tm), cols))
            out_shape.append(jax.ShapeDtypeStruct((B, w, T), dt))
    return pl.pallas_call(
        kern,
        grid=(R // tm,),
        in_specs=in_specs,
        out_specs=out_specs,
        out_shape=out_shape,
        compiler_params=_cparams(1),
        name="inproj",
    )(*args)


def _post_kernel(ff_chunk, final_norm, x_ref, a_ref, h_ref, woa_ref, woh_ref, g2_ref, wup_ref,
                 wdn_ref, gf_ref, y_ref):
    mix = jnp.dot(a_ref[...].astype(BF16), woa_ref[...], preferred_element_type=F32)
    mix = mix + jnp.dot(h_ref[...].astype(BF16), woh_ref[...], preferred_element_type=F32)
    hres = x_ref[...] + mix
    f = _rms(hres, g2_ref[...]).astype(BF16)
    acc = hres
    for c0 in range(0, wup_ref.shape[1], ff_chunk):
        up = jnp.dot(f, wup_ref[:, c0:c0 + ff_chunk], preferred_element_type=F32)
        r = jnp.maximum(up, 0.0)
        acc = acc + jnp.dot((r * r).astype(BF16), wdn_ref[c0:c0 + ff_chunk, :],
                            preferred_element_type=F32)
    y_ref[...] = _rms(acc, gf_ref[...]) if final_norm else acc


def _post(x2, attn, h, woa, woh, g2, wup, wdn, gf, final_norm, tm):
    R, D = x2.shape
    dff = wup.shape[1]
    assert R % tm == 0
    row = lambda i: (i, 0)
    return pl.pallas_call(
        functools.partial(_post_kernel, min(dff, 1024), final_norm),
        grid=(R // tm,),
        in_specs=[pl.BlockSpec((tm, D), row), pl.BlockSpec((tm, ATT_WIDTH), row),
                  pl.BlockSpec((tm, M_WIDTH), row), _const_spec(woa.shape), _const_spec(woh.shape),
                  _const_spec((1, D)), _const_spec(wup.shape), _const_spec(wdn.shape),
                  _const_spec((1, D))],
        out_specs=pl.BlockSpec((tm, D), row),
        out_shape=jax.ShapeDtypeStruct((R, D), F32),
        compiler_params=_cparams(1),
        name="post",
    )(x2, attn, h, woa, woh, g2, wup, wdn, gf)


def _log_sigmoid(x):
    return -(jnp.maximum(-x, 0.0) + jnp.log1p(jnp.exp(-jnp.abs(x))))


def _mlstm_kernel(nvalid, has_state, *refs):
    if has_state:
        (q_ref, k_ref, v_ref, o_ref, misc_ref, gb_ref, mn_ref, c0_ref, n0_ref, m0_ref,
         h_ref, c_ref, n_ref, m_ref) = refs
    else:
        (q_ref, k_ref, v_ref, o_ref, misc_ref, gb_ref, mn_ref,
         h_ref, c_ref, n_ref, m_ref) = refs
    L = ML

    @pl.when(pl.program_id(1) == 0)
    def _():
        if has_state:
            c_ref[...] = c0_ref[...]
            n_ref[...] = n0_ref[...]
            m_ref[...] = m0_ref[...]
        else:
            c_ref[...] = jnp.zeros_like(c_ref)
            n_ref[...] = jnp.zeros_like(n_ref)
            m_ref[...] = jnp.zeros_like(m_ref)

    def padded(ref, dt):
        x = ref[...].astype(dt)
        if nvalid == L:
            return x
        return jnp.concatenate([x, jnp.zeros((L - nvalid, x.shape[1]), dt)], axis=0)

    lane = lax.broadcasted_iota(I32, (L, LANES), 1)
    rowi = lax.broadcasted_iota(I32, (L, LANES), 0)
    is_i = (lane >= MISC_I) & (lane < MISC_I + M_HEADS)
    is_f = (lane >= MISC_F) & (lane < MISC_F + M_HEADS)
    gx = padded(misc_ref, F32) + gb_ref[...]
    gates = jnp.where(is_f, _log_sigmoid(gx), gx)
    if nvalid != L:
        gates = jnp.where(rowi < nvalid, gates, jnp.where(is_i, NEG, 0.0))

    r2 = lax.broadcasted_iota(I32, (L, L), 0)
    c2 = lax.broadcasted_iota(I32, (L, L), 1)
    tril = r2 >= c2
    bcum = jnp.dot(tril.astype(F32), gates, precision=lax.Precision.HIGHEST,
                   preferred_element_type=F32)
    z = jnp.where(is_i, gates, bcum)
    sr = lax.broadcasted_iota(I32, (SUBLANES, LANES), 0)
    sc = lax.broadcasted_iota(I32, (SUBLANES, LANES), 1)
    sel = (sc == jnp.where(sr < M_HEADS, MISC_F + sr, MISC_I + sr - M_HEADS)).astype(F32)
    rows = lax.dot_general(sel, z, NT_DIMS, precision=lax.Precision.HIGHEST,
                           preferred_element_type=F32)

    qb = padded(q_ref, BF16)
    kb = padded(k_ref, BF16)
    vb = padded(v_ref, BF16)
    ob = padded(o_ref, F32)

    for hd in range(M_HEADS):
        sl = slice(hd * M_HEAD_DIM, (hd + 1) * M_HEAD_DIM)
        q, k, v = qb[:, sl], kb[:, sl], vb[:, sl]
        bcol = z[:, MISC_F + hd:MISC_F + hd + 1]
        icol = z[:, MISC_I + hd:MISC_I + hd + 1]
        brow = rows[hd:hd + 1, :]
        irow = rows[M_HEADS + hd:M_HEADS + hd + 1, :]
        m_prev = m_ref[0, hd:hd + 1, 0:1]
        s_prev = c_ref[0, hd]
        n_prev = n_ref[0, hd:hd + 1, :]

        g = bcol + m_prev
        dm = jnp.where(tril, bcol - brow + irow, NEG)
        mt = jnp.maximum(g, jnp.max(dm, axis=1, keepdims=True))
        dw = jnp.exp(dm - mt)
        gw = jnp.exp(g - mt)
        qk = lax.dot_general(q, k, NT_DIMS, preferred_element_type=F32) * dw
        num = gw * jnp.dot(q, s_prev.astype(BF16), preferred_element_type=F32)
        num = num + jnp.dot(qk.astype(BF16), v, preferred_element_type=F32)
        den = gw * jnp.sum(q.astype(F32) * n_prev, axis=1, keepdims=True)
        den = den + jnp.sum(qk, axis=1, keepdims=True)
        hh = num / jnp.maximum(jnp.abs(den), jnp.exp(-mt))
        hh = hh * lax.rsqrt(jnp.mean(hh * hh, axis=-1, keepdims=True) + EPS)
        hh = hh * mn_ref[:, sl] * jax.nn.sigmoid(ob[:, sl])
        h_ref[:, sl] = hh[:nvalid].astype(h_ref.dtype)

        b_last = bcol[L - 1:L, :]
        g_last = b_last + m_prev
        a = b_last - bcol + icol
        m_new = jnp.maximum(g_last, jnp.max(a, axis=0, keepdims=True))
        aw = jnp.exp(a - m_new)
        sw = jnp.exp(g_last - m_new)
        ak = aw * k.astype(F32)
        c_ref[0, hd] = sw * s_prev + lax.dot_general(ak.astype(BF16), v, TN_DIMS,
                                                     preferred_element_type=F32)
        n_ref[0, hd:hd + 1, :] = sw * n_prev + jnp.sum(ak, axis=0, keepdims=True)
        m_ref[0, hd:hd + 1, :] = jnp.broadcast_to(m_new, (1, LANES))


def _mlstm(qm, km, vm, om, misc, gate_bias, mnorm, state, B, T, hdt):
    nvalid = min(T, ML)
    assert T % nvalid == 0
    nc = T // nvalid
    blk = lambda w: pl.BlockSpec((nvalid, w), lambda b, c: (b * nc + c, 0))
    st_specs = [pl.BlockSpec((1, M_HEADS, M_HEAD_DIM, M_HEAD_DIM), lambda b, c: (b, 0, 0, 0)),
                pl.BlockSpec((1, M_HEADS, M_HEAD_DIM), lambda b, c: (b, 0, 0)),
                pl.BlockSpec((1, M_HEADS, LANES), lambda b, c: (b, 0, 0))]
    in_specs = [blk(M_WIDTH), blk(M_WIDTH), blk(M_WIDTH), blk(M_WIDTH), blk(LANES),
                pl.BlockSpec((1, LANES), lambda b, c: (0, 0)),
                pl.BlockSpec((1, M_WIDTH), lambda b, c: (0, 0))]
    args = [qm, km, vm, om, misc, gate_bias, mnorm]
    if state is not None:
        in_specs += st_specs
        args += list(state)
    return pl.pallas_call(
        functools.partial(_mlstm_kernel, nvalid, state is not None),
        grid=(B, nc),
        in_specs=in_specs,
        out_specs=[blk(M_WIDTH)] + st_specs,
        out_shape=[jax.ShapeDtypeStruct((B * T, M_WIDTH), hdt),
                   jax.ShapeDtypeStruct((B, M_HEADS, M_HEAD_DIM, M_HEAD_DIM), F32),
                   jax.ShapeDtypeStruct((B, M_HEADS, M_HEAD_DIM), F32),
                   jax.ShapeDtypeStruct((B, M_HEADS, LANES), F32)],
        compiler_params=_cparams(2),
        name="mlstm",
    )(*args)


def _sortable_key(score):
    bits = lax.bitcast_convert_type(score, I32)
    return bits ^ (lax.shift_right_arithmetic(bits, 31) & 0x7FFFFFFF)


def _key_to_score(key):
    return lax.bitcast_convert_type(key ^ (lax.shift_right_arithmetic(key, 31) & 0x7FFFFFFF), F32)


def _build_bias_strip(strip_ref, rb_ref, off, key_axis):
    _, ntiles, rows, _ = strip_ref.shape
    i = lax.broadcasted_iota(I32, (rows, LANES), 1 - key_axis)
    x = lax.broadcasted_iota(I32, (rows, LANES), key_axis)
    for t in range(ntiles):
        dist = i + (off - LANES * t) - x
        for h in range(N_HEADS):
            val = jnp.full((rows, LANES), rb_ref[0, h], F32)
            for b in range(1, N_BUCKETS):
                val = jnp.where(dist >= BUCKET_BOUNDS[b], rb_ref[b, h], val)
            strip_ref[h, t] = val


def _counter(score_ref, nk, ck, key_axis):
    def count(pred):
        def body(c, acc):
            c0 = pl.multiple_of(c * ck, ck)
            sc = score_ref[:, pl.ds(c0, ck)] if key_axis == 1 else score_ref[pl.ds(c0, ck), :]
            idx = c0 + lax.broadcasted_iota(I32, sc.shape, key_axis)
            hit = jnp.where(pred(sc, idx), 1, 0)
            if key_axis == 1:
                for j in range(ck // LANES):
                    acc = acc + hit[:, j * LANES:(j + 1) * LANES]
                return acc
            return acc + jnp.sum(hit.reshape(ck // SUBLANES, SUBLANES, hit.shape[1]), axis=0)

        nq = score_ref.shape[1 - key_axis]
        acc0 = jnp.zeros((nq, LANES) if key_axis == 1 else (SUBLANES, nq), I32)
        return jnp.sum(lax.fori_loop(0, nk, body, acc0), axis=key_axis, keepdims=True)

    return count


IMIN = jnp.iinfo(jnp.int32).min


def _kth_largest_by_count(count, qshape, n_sel):
    def bit_step(i, key):
        cand = key + lax.shift_left(jnp.int32(1), 31 - i)
        cand_score = _key_to_score(cand)
        return jnp.where(count(lambda sc, idx: sc >= cand_score) >= n_sel, cand, key)

    return _key_to_score(lax.fori_loop(0, 32, bit_step, jnp.full(qshape, IMIN, I32)))


def _bit_planes(words):
    a = list(words)
    j, m = 16, 0x0000FFFF
    while j:
        k = 0
        while k < 32:
            t = (a[k] ^ lax.shift_right_logical(a[k + j], j)) & m
            a[k] = a[k] ^ t
            a[k + j] = a[k + j] ^ lax.shift_left(t, j)
            k = (k + j + 1) & ~j
        j >>= 1
        m = (m ^ (m << j)) & 0xFFFFFFFF
        m = m - (1 << 32) if m >= (1 << 31) else m
    return a[::-1]


def _kth_largest_by_planes(score_ref, planes_ref, nk, ck, n_sel):
    nc = planes_ref.shape[1]
    nq = score_ref.shape[1]
    assert ck == 32 * SUBLANES

    def pack_chunk(c, _):
        c0 = pl.multiple_of(c * ck, ck)
        u = _sortable_key(score_ref[pl.ds(c0, ck), :]) ^ IMIN
        u = u.reshape(32, SUBLANES, nq)
        for b, plane in enumerate(_bit_planes([u[v] for v in range(32)])):
            planes_ref[b, c] = plane
        return 0

    lax.fori_loop(0, nk, pack_chunk, 0)

    def bit_step(i, carry):
        cand, n_above, thr_u = carry
        b = 31 - i
        ones = [cand[c] & planes_ref[b, c] for c in range(nc)]
        pop = lax.population_count(ones[0])
        for c in range(1, nc):
            pop = pop + lax.population_count(ones[c])
        tot = jnp.sum(pop, axis=0, keepdims=True)
        take = n_above + tot >= n_sel
        cand = tuple(jnp.where(take, o, m ^ o) for o, m in zip(ones, cand))
        n_above = jnp.where(take, n_above, n_above + tot)
        thr_u = thr_u | jnp.where(take, lax.shift_left(jnp.int32(1), b), 0)
        return cand, n_above, thr_u

    cand0 = tuple(jnp.where(c < nk, jnp.full((SUBLANES, nq), -1, I32), 0) for c in range(nc))
    zero = jnp.zeros((1, nq), I32)
    _, _, thr_u = lax.fori_loop(0, 32, bit_step, (cand0, zero, zero))
    return _key_to_score(thr_u ^ IMIN)


def _select_topk(thr, check, count, qshape, n_sel, idx_bits):
    imax = jnp.iinfo(jnp.int32).max

    def with_counts(t):
        return t, count(lambda sc, idx: sc > t), count(lambda sc, idx: sc >= t)

    thr, n_gt, n_ge = with_counts(thr)
    if check:
        good = jnp.min(jnp.where(n_gt < n_sel, jnp.where(n_ge >= n_sel, 1, 0), 0)) > 0
        thr, n_gt, n_ge = lax.cond(
            good, lambda _: (thr, n_gt, n_ge),
            lambda _: with_counts(_kth_largest_by_count(count, qshape, n_sel)), 0)
    need = n_sel - n_gt

    def tie_search(_):
        def idx_step(i, lo):
            cand = lo + lax.shift_left(jnp.int32(1), idx_bits - 1 - i)
            cnt = count(lambda sc, idx: jnp.where(sc == thr, idx, imax) < cand)
            return jnp.where(cnt < need, cand, lo)

        return lax.fori_loop(0, idx_bits, idx_step, jnp.zeros(qshape, I32))

    all_ties = jnp.full(qshape, imax, I32)
    jstar = lax.cond(jnp.max(n_ge) > n_sel, tie_search, lambda _: all_ties, 0)
    jstar = jnp.where(n_ge > n_sel, jstar, all_ties)
    return thr, jstar


def _valid_mask(scores, idx, thr, jstar, qpos):
    sel = jnp.where(scores > thr, 1, jnp.where(scores == thr, jnp.where(idx <= jstar, 1, 0), 0))
    return jnp.where(idx <= qpos, sel, 0) > 0


def _group_queries(qa, tq):
    lane = lax.broadcasted_iota(I32, (tq, LANES), 1)
    out = []
    for n in range(N_KV_HEADS):
        keep = (lane < HEAD_DIM) if n == 0 else (lane >= HEAD_DIM)
        tiles = [jnp.where(keep, qa[:, j * LANES:(j + 1) * LANES], jnp.zeros((), qa.dtype))
                 for j in range(GROUP)]
        out.append(jnp.concatenate(tiles, axis=0).astype(BF16))
    return out


def _write_attn(out_ref, carries, tq):
    lane = lax.broadcasted_iota(I32, (tq, LANES), 1)
    res = [acc / l for (_, l, acc) in carries]
    for j in range(GROUP):
        tile = jnp.where(lane < HEAD_DIM, res[0][j * tq:(j + 1) * tq], res[1][j * tq:(j + 1) * tq])
        out_ref[:, j * LANES:(j + 1) * LANES] = tile.astype(out_ref.dtype)


def _dsa_prompt_kernel(n_sel, idx_bits, rb_ref, qat_ref, qit_ref, misct_ref, ki_ref, k_ref, vt_ref,
                       out_ref, score_ref, planes_ref, strip_ref):
    qb = pl.program_id(1)
    q0 = qb * TQ
    nk = (q0 + TQ + CKP - 1) // CKP
    tiles_per_chunk = CKP // LANES
    strip_off = LANES * (strip_ref.shape[1] - tiles_per_chunk)

    @pl.when((pl.program_id(0) == 0) & (qb == 0))
    def _():
        _build_bias_strip(strip_ref, rb_ref, strip_off, 0)
        planes_ref[...] = jnp.zeros(planes_ref.shape, I32)

    qpos = q0 + lax.broadcasted_iota(I32, (1, TQ), 1)

    qit = qit_ref[0]
    qstack = jnp.concatenate([qit[h * IDX_DIM:(h + 1) * IDX_DIM] for h in range(N_IDX_HEADS)],
                             axis=1)
    w = misct_ref[0, MISC_W:MISC_W + N_IDX_HEADS, :] * (N_IDX_HEADS ** -0.5 * IDX_DIM ** -0.5)

    def score_chunk(c, _):
        c0 = pl.multiple_of(c * CKP, CKP)
        d = jnp.dot(ki_ref[pl.ds(c0, CKP), :], qstack, preferred_element_type=F32)
        d = jnp.maximum(d, 0.0)
        s = jnp.zeros((CKP, TQ), F32)
        for h in range(N_IDX_HEADS):
            s = s + d[:, h * TQ:(h + 1) * TQ] * w[h:h + 1, :]
        idx = c0 + lax.broadcasted_iota(I32, (CKP, TQ), 0)
        score_ref[pl.ds(c0, CKP), :] = jnp.where(idx <= qpos, s, NEG)
        return 0

    lax.fori_loop(0, nk, score_chunk, 0)

    thr = _kth_largest_by_planes(score_ref, planes_ref, nk, CKP, n_sel)
    thr, jstar = _select_topk(thr, True, _counter(score_ref, nk, CKP, 0), (1, TQ), n_sel, idx_bits)

    qat = qat_ref[0]
    zeros = jnp.zeros((HEAD_DIM, TQ), qat.dtype)
    qgroups = []
    for n in range(N_KV_HEADS):
        tiles = []
        for g in range(GROUP):
            h = n * GROUP + g
            x = qat[h * HEAD_DIM:(h + 1) * HEAD_DIM]
            tiles.append(jnp.concatenate([x, zeros] if n == 0 else [zeros, x], axis=0))
        qgroups.append(jnp.concatenate(tiles, axis=1))

    def attend_chunk(c, carries):
        c0 = pl.multiple_of(c * CKP, CKP)
        idx = c0 + lax.broadcasted_iota(I32, (CKP, TQ), 0)
        valid = _valid_mask(score_ref[pl.ds(c0, CKP), :], idx, thr, jstar, qpos)
        kc = k_ref[pl.ds(c0, CKP), :]
        vct = vt_ref[0, c]
        t0 = jnp.maximum(strip_off // LANES - (qb - tiles_per_chunk * c), 0)
        out = []
        for n in range(N_KV_HEADS):
            m_old, l_old, acc = carries[n]
            s = jnp.dot(kc, qgroups[n], preferred_element_type=F32)
            parts = []
            for g in range(GROUP):
                bias = jnp.concatenate([strip_ref[n * GROUP + g, t0 + j] for j in range(tiles_per_chunk)],
                                       axis=0)
                parts.append(jnp.where(valid, s[:, g * TQ:(g + 1) * TQ] + bias, NEG))
            sm = jnp.concatenate(parts, axis=1)
            m_new = jnp.maximum(m_old, jnp.max(sm, axis=0, keepdims=True))
            alpha = jnp.exp(m_old - m_new)
            p = jnp.exp(sm - m_new)
            l_new = alpha * l_old + jnp.sum(p, axis=0, keepdims=True)
            acc = alpha * acc + jnp.dot(vct, p.astype(BF16), preferred_element_type=F32)
            out.append((m_new, l_new, acc))
        return tuple(out)

    init = (jnp.full((1, GROUP * TQ), NEG, F32), jnp.zeros((1, GROUP * TQ), F32),
            jnp.zeros((LANES, GROUP * TQ), F32))
    carries = lax.fori_loop(0, nk, attend_chunk, (init, init))
    res = [acc / l for (_, l, acc) in carries]
    row = lax.broadcasted_iota(I32, (LANES, TQ), 0)
    for j in range(GROUP):
        cols = slice(j * TQ, (j + 1) * TQ)
        tile_t = jnp.where(row < HEAD_DIM, res[0][:, cols], res[1][:, cols])
        out_ref[:, j * LANES:(j + 1) * LANES] = tile_t.T.astype(out_ref.dtype)


def _dsa_prompt(rel_bias, qat, qit, misct, kib, kb, vtb, B, T):
    assert T % CKP == 0 and T % TQ == 0
    nq = T // TQ
    n_sel = min(TOPK_MAX, T // 4)
    idx_bits = max(1, (T - 1).bit_length())
    qcols = lambda w: pl.BlockSpec((1, w, TQ), lambda b, q: (b, 0, q))
    seq = lambda w: pl.BlockSpec((T, w), lambda b, q: (b, 0))
    return pl.pallas_call(
        functools.partial(_dsa_prompt_kernel, n_sel, idx_bits),
        grid=(B, nq),
        in_specs=[pl.BlockSpec(memory_space=pltpu.SMEM), qcols(ATT_WIDTH), qcols(N_IDX_HEADS * IDX_DIM),
                  qcols(LANES), seq(IDX_DIM), seq(LANES),
                  pl.BlockSpec((1, T // CKP, LANES, CKP), lambda b, q: (b, 0, 0, 0))],
        out_specs=pl.BlockSpec((TQ, ATT_WIDTH), lambda b, q: (b * nq + q, 0)),
        out_shape=jax.ShapeDtypeStruct((B * T, ATT_WIDTH), BF16),
        scratch_shapes=[pltpu.VMEM((T, TQ), F32),
                        pltpu.VMEM((32, T // CKP, SUBLANES, TQ), I32),
                        pltpu.VMEM((N_HEADS, 3 + CKP // LANES, LANES, TQ), F32)],
        compiler_params=_cparams(2),
        name="dsa_prompt",
    )(rel_bias, qat, qit, misct, kib, kb, vtb)


def _page_pipeline(pt_ref, n_pages, caches, bufs, sems):
    def copies(bb, sl, j):
        pid = pt_ref[bb, j]
        cols = pl.ds(pl.multiple_of(j * PAGE_SIZE, PAGE_SIZE), PAGE_SIZE)
        return [pltpu.make_async_copy(c.at[pid], buf.at[sl, :, cols], sems.at[sl, i])
                for i, (c, buf) in enumerate(zip(caches, bufs))]

    def start_all(bb, sl):
        def body(j, _):
            for cp in copies(bb, sl, j):
                cp.start()
            return 0
        lax.fori_loop(0, n_pages, body, 0)

    def wait_all(bb, sl):
        def body(j, _):
            for cp in copies(bb, sl, j):
                cp.wait()
            return 0
        lax.fori_loop(0, n_pages, body, 0)

    def step():
        b = pl.program_id(0)
        slot = b % 2

        @pl.when(b == 0)
        def _():
            start_all(0, 0)

        @pl.when(b + 1 < pl.num_programs(0))
        def _():
            start_all(b + 1, 1 - slot)

        wait_all(b, slot)
        return slot

    return step


def _pad_rows(x, rows):
    return jnp.concatenate([x, jnp.zeros((rows - x.shape[0], x.shape[1]), x.dtype)], axis=0)


def _sample_score_kernel(n_sel, idx_bits, n_pages, ts, ck, pt_ref, qi_ref, misc_ref, kin_ref,
                         ckit_hbm, score_ref, thr_ref, jst_ref, ki_buf, sems):
    b = pl.program_id(0)
    past = n_pages * PAGE_SIZE
    slot = _page_pipeline(pt_ref, n_pages, [ckit_hbm], [ki_buf], sems)()

    qi = qi_ref[...]
    qstack = jnp.concatenate([qi[:, h * IDX_DIM:(h + 1) * IDX_DIM] for h in range(N_IDX_HEADS)],
                             axis=0).astype(BF16)
    w = misc_ref[:, MISC_W:MISC_W + N_IDX_HEADS] * (N_IDX_HEADS ** -0.5 * IDX_DIM ** -0.5)
    d_past = jnp.dot(qstack, ki_buf[slot].astype(BF16), preferred_element_type=F32)
    d_own = lax.dot_general(qstack, _pad_rows(kin_ref[...], PAGE_SIZE).astype(BF16), NT_DIMS,
                            preferred_element_type=F32)
    d = jnp.maximum(jnp.concatenate([d_past, d_own], axis=1), 0.0)
    s = jnp.zeros((ts, past + PAGE_SIZE), F32)
    for h in range(N_IDX_HEADS):
        s = s + d[h * ts:(h + 1) * ts] * w[:, h:h + 1]
    idx = lax.broadcasted_iota(I32, s.shape, 1)
    qpos = past + lax.broadcasted_iota(I32, (ts, 1), 0)
    score_ref[pl.ds(pl.multiple_of(b * ts, ts), ts), :] = jnp.where(idx <= qpos, s, NEG)

    @pl.when(b == pl.num_programs(0) - 1)
    def _():
        count = _counter(score_ref, score_ref.shape[1] // ck, ck, 1)
        qshape = (score_ref.shape[0], 1)
        thr = _kth_largest_by_count(count, qshape, n_sel)
        thr, jstar = _select_topk(thr, False, count, qshape, n_sel, idx_bits)
        thr_ref[...] = jnp.broadcast_to(thr, thr_ref.shape)
        jst_ref[...] = jnp.broadcast_to(jstar, jst_ref.shape)


def _sample_attend_kernel(n_pages, ts, pt_ref, rb_ref, qa_ref, score_ref, thr_ref, jst_ref, kn_ref,
                          vn_ref, ckt_hbm, cvt_hbm, out_ref, k_buf, v_buf, sems, strip_ref):
    b = pl.program_id(0)
    past = n_pages * PAGE_SIZE
    rows = N_HEADS * ts

    @pl.when(b == 0)
    def _():
        _build_bias_strip(strip_ref, rb_ref, LANES * (strip_ref.shape[1] - 1), 1)

    slot = _page_pipeline(pt_ref, n_pages, [ckt_hbm, cvt_hbm], [k_buf, v_buf], sems)()

    q2 = jnp.concatenate(_group_queries(qa_ref[...], ts), axis=0)
    k_own = _pad_rows(kn_ref[...], PAGE_SIZE).astype(BF16)
    v_own = _pad_rows(vn_ref[...], PAGE_SIZE).astype(BF16)
    s_past = jnp.dot(q2, k_buf[slot].astype(BF16), preferred_element_type=F32)
    s_own = lax.dot_general(q2, k_own, NT_DIMS, preferred_element_type=F32)
    far = strip_ref[:, 0].reshape(rows, LANES)[:, 0:1]
    near = [strip_ref[:, t].reshape(rows, LANES) for t in (1, 2)]
    s = jnp.concatenate([s_past[:, :past - PAGE_SIZE] + far, s_past[:, past - PAGE_SIZE:] + near[0],
                         s_own + near[1]], axis=1)

    qpos = past + lax.broadcasted_iota(I32, (ts, 1), 0)
    scores = score_ref[...]
    valid = _valid_mask(scores, lax.broadcasted_iota(I32, scores.shape, 1), thr_ref[:, 0:1],
                        jst_ref[:, 0:1], qpos)
    s = jnp.where(valid[None], s.reshape(N_HEADS, ts, past + PAGE_SIZE), NEG).reshape(rows, -1)
    m = jnp.max(s, axis=1, keepdims=True)
    p = jnp.exp(s - m)
    l = jnp.sum(p, axis=1, keepdims=True)
    pb = p.astype(BF16)
    pv = lax.dot_general(pb[:, :past], v_buf[slot].astype(BF16), NT_DIMS, preferred_element_type=F32)
    pv = pv + jnp.dot(pb[:, past:], v_own, preferred_element_type=F32)
    half = GROUP * ts
    carries = [(None, l[n * half:(n + 1) * half], pv[n * half:(n + 1) * half]) for n in range(N_KV_HEADS)]
    _write_attn(out_ref, carries, ts)


def _dsa_sample(page_table, rel_bias, qa, qi, misc, ki_new, k_new, v_new, ckit, ckt, cvt, DB, ts):
    n_pages = page_table.shape[1]
    past = n_pages * PAGE_SIZE
    n_sel = min(TOPK_MAX, (past + ts) // 4)
    lpad = past + PAGE_SIZE
    idx_bits = max(1, (lpad - 1).bit_length())
    ck = LANES * math.gcd(lpad // LANES, 5)
    assert ts % SUBLANES == 0 and ts <= PAGE_SIZE and n_pages >= 1
    blk = lambda w: pl.BlockSpec((ts, w), lambda b, pt: (b, 0))
    whole = lambda w: pl.BlockSpec((DB * ts, w), lambda b, pt: (0, 0))
    hbm = pl.BlockSpec(memory_space=pl.ANY)
    keys, thr, jstar = pl.pallas_call(
        functools.partial(_sample_score_kernel, n_sel, idx_bits, n_pages, ts, ck),
        grid_spec=pltpu.PrefetchScalarGridSpec(
            num_scalar_prefetch=1,
            grid=(DB,),
            in_specs=[blk(N_IDX_HEADS * IDX_DIM), blk(LANES), blk(IDX_DIM), hbm],
            out_specs=[whole(lpad), whole(LANES), whole(LANES)],
            scratch_shapes=[pltpu.VMEM((2, IDX_DIM, past), F32), pltpu.SemaphoreType.DMA((2, 1))]),
        out_shape=[jax.ShapeDtypeStruct((DB * ts, lpad), F32),
                   jax.ShapeDtypeStruct((DB * ts, LANES), F32),
                   jax.ShapeDtypeStruct((DB * ts, LANES), I32)],
        compiler_params=_cparams(1),
        name="sample_score",
    )(page_table, qi, misc, ki_new, ckit)
    return pl.pallas_call(
        functools.partial(_sample_attend_kernel, n_pages, ts),
        grid_spec=pltpu.PrefetchScalarGridSpec(
            num_scalar_prefetch=1,
            grid=(DB,),
            in_specs=[pl.BlockSpec(memory_space=pltpu.SMEM), blk(ATT_WIDTH), blk(lpad), blk(LANES),
                      blk(LANES), blk(LANES), blk(LANES), hbm, hbm],
            out_specs=blk(ATT_WIDTH),
            scratch_shapes=[pltpu.VMEM((2, LANES, past), F32), pltpu.VMEM((2, LANES, past), F32),
                            pltpu.SemaphoreType.DMA((2, 2)),
                            pltpu.VMEM((N_HEADS, 3, ts, LANES), F32)]),
        out_shape=jax.ShapeDtypeStruct((DB * ts, ATT_WIDTH), F32),
        compiler_params=_cparams(1),
        name="sample_attend",
    )(page_table, rel_bias, qa, keys, thr, jstar, k_new, v_new, ckt, cvt)


def _pack_layer_weights(w_in, b_i, b_f, w_out, w_up, w_down):
    D = w_in.shape[0]
    sizes = (ATT_WIDTH, N_KV_HEADS * HEAD_DIM, N_KV_HEADS * HEAD_DIM, N_IDX_HEADS * IDX_DIM, IDX_DIM,
             N_IDX_HEADS, M_WIDTH, M_WIDTH, M_WIDTH, M_WIDTH, M_HEADS, M_HEADS)
    assert w_in.shape[1] == sum(sizes)
    pts = np.cumsum((0,) + sizes)
    seg = [w_in[:, pts[i]:pts[i + 1]] for i in range(len(sizes))]
    qa, k, v, qi, ki, wi, qm, km, vm, om, im, fm = seg
    perm = np.asarray(HEAD_PERM)
    qa = qa.reshape(D, N_HEADS, HEAD_DIM)[:, perm].reshape(D, ATT_WIDTH)
    misc = jnp.concatenate([wi, im, fm, jnp.zeros((D, LANES - N_IDX_HEADS - 2 * M_HEADS), w_in.dtype)], axis=1)
    wp = jnp.concatenate([qa, k, v, qi, ki, ki, misc, qm, km, vm, om], axis=1).astype(BF16)
    assert wp.shape[1] == N_PACK
    gate_bias = jnp.concatenate([jnp.zeros((MISC_I,), F32), b_i.astype(F32), b_f.astype(F32),
                                 jnp.zeros((LANES - MISC_F - M_HEADS,), F32)]).reshape(1, LANES)
    woa = w_out[:ATT_WIDTH].reshape(N_HEADS, HEAD_DIM, -1)[perm].reshape(ATT_WIDTH, -1).astype(BF16)
    woh = w_out[ATT_WIDTH:].astype(BF16)
    wt = jnp.concatenate([seg[1].T, seg[2].T, ki.T, jnp.zeros((LANES - IDX_DIM, D), w_in.dtype),
                          seg[0].T, qi.T, misc.T], axis=0).astype(BF16)
    assert wt.shape[0] == N_TPACK
    return wp, wt, gate_bias, woa, woh, w_up.astype(BF16), w_down.astype(BF16)


def _layer(x, packed, g1, g2, mnorm, rel_bias, gf, final_norm, past):
    wp, wt, gate_bias, woa, woh, wup, wdn = packed
    B, T, D = x.shape
    x2 = x.reshape(B * T, D)
    tm = math.gcd(T if past is None else B * T, 512)
    kv_w = N_KV_HEADS * HEAD_DIM
    if past is None:
        (misc, qm, km, vm, om, kb, kib, kt, vt, kit, vtb, qat, qit, misct) = _inproj(
            x2, g1.reshape(1, D), wp, wt, B, T, tm)
        attn = _dsa_prompt(rel_bias, qat, qit, misct, kib, kb, vtb, B, T)
        state = None
        k_new = kt.reshape(B, N_KV_HEADS, HEAD_DIM, T).transpose(0, 3, 1, 2)
        v_new = vt.reshape(B, N_KV_HEADS, HEAD_DIM, T).transpose(0, 3, 1, 2)
        ki_new = kit.transpose(0, 2, 1)
    else:
        (misc, qm, km, vm, om, qa, qi, k, v, ki) = _inproj(x2, g1.reshape(1, D), wp, None, B, T, tm)
        page_table, cache_k, cache_v, cache_kidx, c0, n0, m0 = past
        n_pool = cache_k.shape[0]
        ckt = cache_k.transpose(0, 2, 3, 1).reshape(n_pool, kv_w, PAGE_SIZE)
        cvt = cache_v.transpose(0, 2, 3, 1).reshape(n_pool, kv_w, PAGE_SIZE)
        ckit = cache_kidx.transpose(0, 2, 1)
        attn = _dsa_sample(page_table, rel_bias, qa, qi, misc, ki, k, v, ckit, ckt, cvt, B, T)
        state = (c0, n0, jnp.broadcast_to(m0[..., None], m0.shape + (LANES,)))
        k_new = k.reshape(B, T, N_KV_HEADS, HEAD_DIM)
        v_new = v.reshape(B, T, N_KV_HEADS, HEAD_DIM)
        ki_new = ki.reshape(B, T, IDX_DIM)
    h, c_new, n_new, m_new = _mlstm(qm, km, vm, om, misc, gate_bias, mnorm.reshape(1, M_WIDTH), state,
                                    B, T, BF16 if past is None else F32)
    y = _post(x2, attn, h, woa, woh, g2.reshape(1, D), wup, wdn, gf.reshape(1, D), final_norm, tm)
    return (y.reshape(B, T, D), k_new, v_new, ki_new, c_new, n_new, m_new[..., 0])


def kernel(x_prompt, x_sample, cache_k, cache_v, cache_kidx, page_table, state_C, state_n, state_m,
           w_in, b_igate, b_fgate, mlstm_norm, rel_bias, w_out, norm1, norm2, w_up, w_down, norm_f):
    depth = w_in.shape[0]
    xp, xs = x_prompt, x_sample
    outs_p, outs_s = [], []
    for l in range(depth):
        packed = _pack_layer_weights(w_in[l], b_igate[l], b_fgate[l], w_out[l], w_up[l], w_down[l])
        last = l == depth - 1
        common = (packed, norm1[l], norm2[l], mlstm_norm[l], rel_bias, norm_f, last)
        rp = _layer(xp, *common, None)
        rs = _layer(xs, *common, (page_table, cache_k[l], cache_v[l], cache_kidx[l],
                                  state_C[l], state_n[l], state_m[l]))
        xp, xs = rp[0], rs[0]
        outs_p.append(rp[1:])
        outs_s.append(rs[1:])
    stack = lambda outs, i: jnp.stack([o[i] for o in outs])
    return ((xp, xs) + tuple(stack(outs_p, i) for i in range(6))
            + tuple(stack(outs_s, i) for i in range(6)))
```

```python
import functools
import math

import numpy as np
import jax
import jax.numpy as jnp
from jax import lax
from jax.experimental import pallas as pl
from jax.experimental.pallas import tpu as pltpu

F32 = jnp.float32
BF16 = jnp.bfloat16
I32 = jnp.int32

N_HEADS = 8
HEAD_DIM = 64
N_KV_HEADS = 2
GROUP = N_HEADS // N_KV_HEADS
N_IDX_HEADS = 8
IDX_DIM = 64
TOPK_MAX = 256
N_BUCKETS = 32
MAX_DISTANCE = 128
M_HEADS = 4
M_HEAD_DIM = 128
PAGE_SIZE = 128
EPS = 1e-6
NEG = -1e30
ATT_WIDTH = N_HEADS * HEAD_DIM
M_WIDTH = M_HEADS * M_HEAD_DIM

LANES = 128
SUBLANES = 8
VMEM_LIMIT = 56 * 1024 * 1024

C_QA = 0
C_K = C_QA + ATT_WIDTH
C_V = C_K + LANES
C_QI = C_V + LANES
C_KI2 = C_QI + N_IDX_HEADS * IDX_DIM
C_MISC = C_KI2 + LANES
C_QM = C_MISC + LANES
C_KM = C_QM + M_WIDTH
C_VM = C_KM + M_WIDTH
C_OM = C_VM + M_WIDTH
N_PACK = C_OM + M_WIDTH
MISC_W = 0
MISC_I = 8
MISC_F = 12

HEAD_PERM = (0, 4, 1, 5, 2, 6, 3, 7)

TQ = 256
CKP = 256
ML = 128

NT_DIMS = (((1,), (1,)), ((), ()))
TN_DIMS = (((0,), (0,)), ((), ()))


def _bucket_bounds():
    max_exact = N_BUCKETS // 2
    scale = (N_BUCKETS - max_exact) / math.log(MAX_DISTANCE / max_exact)

    def bucket(n, dt):
        if n < max_exact:
            return n
        val = np.log(np.asarray(max(n, 1), dt) / dt(max_exact)) * dt(scale)
        return min(max_exact + int(val), N_BUCKETS - 1)

    table = [bucket(n, np.float32) for n in range(MAX_DISTANCE + 2)]
    assert table == [bucket(n, np.float64) for n in range(MAX_DISTANCE + 2)]
    assert table[MAX_DISTANCE] == N_BUCKETS - 1
    return [next(d for d, b in enumerate(table) if b >= k) for k in range(N_BUCKETS)]


BUCKET_BOUNDS = _bucket_bounds()


def _cparams(n_axes):
    return pltpu.CompilerParams(dimension_semantics=("arbitrary",) * n_axes,
                                vmem_limit_bytes=VMEM_LIMIT)


def _const_spec(shape):
    nd = len(shape)
    return pl.BlockSpec(shape, lambda *_: (0,) * nd, pipeline_mode=pl.Buffered(1))


def _rms(x, g):
    return x * lax.rsqrt(jnp.mean(x * x, axis=-1, keepdims=True) + EPS) * g


def _inproj_mlstm(mm, misc_ref, qm_ref, km_ref, vm_ref, om_ref):
    misc_ref[...] = mm(C_MISC, LANES)
    qm_ref[...] = mm(C_QM, M_WIDTH).astype(qm_ref.dtype)
    km_ref[...] = (mm(C_KM, M_WIDTH) * (M_HEAD_DIM ** -0.5)).astype(km_ref.dtype)
    vm_ref[...] = mm(C_VM, M_WIDTH).astype(vm_ref.dtype)
    om_ref[...] = mm(C_OM, M_WIDTH)


def _inproj_rows_kernel(x_ref, g_ref, w_ref, misc_ref, qm_ref, km_ref, vm_ref, om_ref,
                        qa_ref, qi_ref, k_ref, v_ref, ki_ref):
    ub = _rms(x_ref[...], g_ref[...]).astype(BF16)
    mm = lambda c0, n: jnp.dot(ub, w_ref[:, c0:c0 + n], preferred_element_type=F32)
    _inproj_mlstm(mm, misc_ref, qm_ref, km_ref, vm_ref, om_ref)
    qa_ref[...] = mm(C_QA, ATT_WIDTH) * (HEAD_DIM ** -0.5)
    qi_ref[...] = mm(C_QI, N_IDX_HEADS * IDX_DIM)
    k_ref[...] = mm(C_K, LANES)
    v_ref[...] = mm(C_V, LANES)
    ki_ref[...] = mm(C_KI2, LANES)[:, :IDX_DIM]


R_K = 0
R_V = R_K + LANES
R_KI = R_V + LANES
R_QA = R_KI + LANES
R_QI = R_QA + ATT_WIDTH
R_MISC = R_QI + N_IDX_HEADS * IDX_DIM
N_TPACK = R_MISC + LANES


def _inproj_cols_kernel(x_ref, g_ref, w_ref, wt_ref, misc_ref, qm_ref, km_ref, vm_ref, om_ref,
                        kb_ref, kib_ref, kt_ref, vt_ref, kit_ref, vtb_ref, qat_ref, qit_ref,
                        misct_ref):
    ub = _rms(x_ref[...], g_ref[...]).astype(BF16)
    mm = lambda c0, n: jnp.dot(ub, w_ref[:, c0:c0 + n], preferred_element_type=F32)
    _inproj_mlstm(mm, misc_ref, qm_ref, km_ref, vm_ref, om_ref)
    kb_ref[...] = mm(C_K, LANES).astype(BF16)
    kib_ref[...] = mm(C_KI2, LANES)[:, :IDX_DIM].astype(BF16)

    def mt(r0, n):
        return lax.dot_general(wt_ref[r0:r0 + n, :], ub, NT_DIMS, preferred_element_type=F32)

    kt_ref[0] = mt(R_K, LANES)
    vt = mt(R_V, LANES)
    vt_ref[0] = vt
    for j in range(vtb_ref.shape[1]):
        vtb_ref[0, j] = vt[:, j * CKP:(j + 1) * CKP].astype(BF16)
    kit_ref[0] = mt(R_KI, IDX_DIM)
    qat_ref[0] = (mt(R_QA, ATT_WIDTH) * (HEAD_DIM ** -0.5)).astype(BF16)
    qit_ref[0] = mt(R_QI, N_IDX_HEADS * IDX_DIM).astype(BF16)
    misct_ref[0] = mt(R_MISC, LANES)


def _inproj(x2, g1, wp, wt, B, T, tm):
    R, D = x2.shape
    assert R == B * T and R % tm == 0
    mdt = F32 if wt is None else BF16
    row = lambda i: (i, 0)
    outs = [(LANES, F32), (M_WIDTH, mdt), (M_WIDTH, mdt), (M_WIDTH, mdt), (M_WIDTH, F32)]
    in_specs = [pl.BlockSpec((tm, D), row), _const_spec((1, D)), _const_spec((D, N_PACK))]
    if wt is None:
        kern, args = _inproj_rows_kernel, (x2, g1, wp)
        outs += [(ATT_WIDTH, F32), (N_IDX_HEADS * IDX_DIM, F32), (LANES, F32), (LANES, F32), (IDX_DIM, F32)]
    else:
        kern, args = _inproj_cols_kernel, (x2, g1, wp, wt)
        in_specs.append(_const_spec(wt.shape))
        outs += [(LANES, BF16), (IDX_DIM, BF16)]
    out_specs = [pl.BlockSpec((tm, w), row) for w, _ in outs]
    out_shape = [jax.ShapeDtypeStruct((R, w), dt) for w, dt in outs]
    if wt is not None:
        assert T % tm == 0 and tm % CKP == 0
        tpb, cpt = T // tm, tm // CKP
        cols = lambda i: (i // tpb, 0, i % tpb)
        for w, dt in ((LANES, F32), (LANES, F32), (IDX_DIM, F32)):
            out_specs.append(pl.BlockSpec((1, w, tm), cols))
            out_shape.append(jax.ShapeDtypeStruct((B, w, T), dt))
        out_specs.append(pl.BlockSpec((1, cpt, LANES, CKP), lambda i: (i // tpb, i % tpb, 0, 0)))
        out_shape.append(jax.ShapeDtypeStruct((B, T // CKP, LANES, CKP), BF16))
        for w, dt in ((ATT_WIDTH, BF16), (N_IDX_HEADS * IDX_DIM, BF16), (LANES, F32)):
            out_specs.append(pl.BlockSpec((1, w, tm), cols))
            out_shape.append(jax.ShapeDtypeStruct((B, w, T), dt))
    return pl.pallas_call(
        kern,
        grid=(R // tm,),
        in_specs=in_specs,
        out_specs=out_specs,
        out_shape=out_shape,
        compiler_params=_cparams(1),
        name="inproj",
    )(*args)


def _post_kernel(ff_chunk, final_norm, x_ref, a_ref, h_ref, woa_ref, woh_ref, g2_ref, wup_ref,
                 wdn_ref, gf_ref, y_ref):
    mix = jnp.dot(a_ref[...].astype(BF16), woa_ref[...], preferred_element_type=F32)
    mix = mix + jnp.dot(h_ref[...].astype(BF16), woh_ref[...], preferred_element_type=F32)
    hres = x_ref[...] + mix
    f = _rms(hres, g2_ref[...]).astype(BF16)
    acc = hres
    for c0 in range(0, wup_ref.shape[1], ff_chunk):
        up = jnp.dot(f, wup_ref[:, c0:c0 + ff_chunk], preferred_element_type=F32)
        r = jnp.maximum(up, 0.0)
        acc = acc + jnp.dot((r * r).astype(BF16), wdn_ref[c0:c0 + ff_chunk, :],
                            preferred_element_type=F32)
    y_ref[...] = _rms(acc, gf_ref[...]) if final_norm else acc


def _post(x2, attn, h, woa, woh, g2, wup, wdn, gf, final_norm, tm):
    R, D = x2.shape
    dff = wup.shape[1]
    assert R % tm == 0
    row = lambda i: (i, 0)
    return pl.pallas_call(
        functools.partial(_post_kernel, min(dff, 1024), final_norm),
        grid=(R // tm,),
        in_specs=[pl.BlockSpec((tm, D), row), pl.BlockSpec((tm, ATT_WIDTH), row),
                  pl.BlockSpec((tm, M_WIDTH), row), _const_spec(woa.shape), _const_spec(woh.shape),
                  _const_spec((1, D)), _const_spec(wup.shape), _const_spec(wdn.shape),
                  _const_spec((1, D))],
        out_specs=pl.BlockSpec((tm, D), row),
        out_shape=jax.ShapeDtypeStruct((R, D), F32),
        compiler_params=_cparams(1),
        name="post",
    )(x2, attn, h, woa, woh, g2, wup, wdn, gf)


def _log_sigmoid(x):
    return -(jnp.maximum(-x, 0.0) + jnp.log1p(jnp.exp(-jnp.abs(x))))


def _mlstm_kernel(nvalid, has_state, *refs):
    if has_state:
        (q_ref, k_ref, v_ref, o_ref, misc_ref, gb_ref, mn_ref, c0_ref, n0_ref, m0_ref,
         h_ref, c_ref, n_ref, m_ref) = refs
    else:
        (q_ref, k_ref, v_ref, o_ref, misc_ref, gb_ref, mn_ref,
         h_ref, c_ref, n_ref, m_ref) = refs
    L = ML

    @pl.when(pl.program_id(1) == 0)
    def _():
        if has_state:
            c_ref[...] = c0_ref[...]
            n_ref[...] = n0_ref[...]
            m_ref[...] = m0_ref[...]
        else:
            c_ref[...] = jnp.zeros_like(c_ref)
            n_ref[...] = jnp.zeros_like(n_ref)
            m_ref[...] = jnp.zeros_like(m_ref)

    def padded(ref, dt):
        x = ref[...].astype(dt)
        if nvalid == L:
            return x
        return jnp.concatenate([x, jnp.zeros((L - nvalid, x.shape[1]), dt)], axis=0)

    lane = lax.broadcasted_iota(I32, (L, LANES), 1)
    rowi = lax.broadcasted_iota(I32, (L, LANES), 0)
    is_i = (lane >= MISC_I) & (lane < MISC_I + M_HEADS)
    is_f = (lane >= MISC_F) & (lane < MISC_F + M_HEADS)
    gx = padded(misc_ref, F32) + gb_ref[...]
    gates = jnp.where(is_f, _log_sigmoid(gx), gx)
    if nvalid != L:
        gates = jnp.where(rowi < nvalid, gates, jnp.where(is_i, NEG, 0.0))

    r2 = lax.broadcasted_iota(I32, (L, L), 0)
    c2 = lax.broadcasted_iota(I32, (L, L), 1)
    tril = r2 >= c2
    bcum = jnp.dot(tril.astype(F32), gates, precision=lax.Precision.HIGHEST,
                   preferred_element_type=F32)
    z = jnp.where(is_i, gates, bcum)
    sr = lax.broadcasted_iota(I32, (SUBLANES, LANES), 0)
    sc = lax.broadcasted_iota(I32, (SUBLANES, LANES), 1)
    sel = (sc == jnp.where(sr < M_HEADS, MISC_F + sr, MISC_I + sr - M_HEADS)).astype(F32)
    rows = lax.dot_general(sel, z, NT_DIMS, precision=lax.Precision.HIGHEST,
                           preferred_element_type=F32)

    qb = padded(q_ref, BF16)
    kb = padded(k_ref, BF16)
    vb = padded(v_ref, BF16)
    ob = padded(o_ref, F32)

    for hd in range(M_HEADS):
        sl = slice(hd * M_HEAD_DIM, (hd + 1) * M_HEAD_DIM)
        q, k, v = qb[:, sl], kb[:, sl], vb[:, sl]
        bcol = z[:, MISC_F + hd:MISC_F + hd + 1]
        icol = z[:, MISC_I + hd:MISC_I + hd + 1]
        brow = rows[hd:hd + 1, :]
        irow = rows[M_HEADS + hd:M_HEADS + hd + 1, :]
        m_prev = m_ref[0, hd:hd + 1, 0:1]
        s_prev = c_ref[0, hd]
        n_prev = n_ref[0, hd:hd + 1, :]

        g = bcol + m_prev
        dm = jnp.where(tril, bcol - brow + irow, NEG)
        mt = jnp.maximum(g, jnp.max(dm, axis=1, keepdims=True))
        dw = jnp.exp(dm - mt)
        gw = jnp.exp(g - mt)
        qk = lax.dot_general(q, k, NT_DIMS, preferred_element_type=F32) * dw
        num = gw * jnp.dot(q, s_prev.astype(BF16), preferred_element_type=F32)
        num = num + jnp.dot(qk.astype(BF16), v, preferred_element_type=F32)
        den = gw * jnp.sum(q.astype(F32) * n_prev, axis=1, keepdims=True)
        den = den + jnp.sum(qk, axis=1, keepdims=True)
        hh = num / jnp.maximum(jnp.abs(den), jnp.exp(-mt))
        hh = hh * lax.rsqrt(jnp.mean(hh * hh, axis=-1, keepdims=True) + EPS)
        hh = hh * mn_ref[:, sl] * jax.nn.sigmoid(ob[:, sl])
        h_ref[:, sl] = hh[:nvalid].astype(h_ref.dtype)

        b_last = bcol[L - 1:L, :]
        g_last = b_last + m_prev
        a = b_last - bcol + icol
        m_new = jnp.maximum(g_last, jnp.max(a, axis=0, keepdims=True))
        aw = jnp.exp(a - m_new)
        sw = jnp.exp(g_last - m_new)
        ak = aw * k.astype(F32)
        c_ref[0, hd] = sw * s_prev + lax.dot_general(ak.astype(BF16), v, TN_DIMS,
                                                     preferred_element_type=F32)
        n_ref[0, hd:hd + 1, :] = sw * n_prev + jnp.sum(ak, axis=0, keepdims=True)
        m_ref[0, hd:hd + 1, :] = jnp.broadcast_to(m_new, (1, LANES))


def _mlstm(qm, km, vm, om, misc, gate_bias, mnorm, state, B, T, hdt):
    nvalid = min(T, ML)
    assert T % nvalid == 0
    nc = T // nvalid
    blk = lambda w: pl.BlockSpec((nvalid, w), lambda b, c: (b * nc + c, 0))
    st_specs = [pl.BlockSpec((1, M_HEADS, M_HEAD_DIM, M_HEAD_DIM), lambda b, c: (b, 0, 0, 0)),
                pl.BlockSpec((1, M_HEADS, M_HEAD_DIM), lambda b, c: (b, 0, 0)),
                pl.BlockSpec((1, M_HEADS, LANES), lambda b, c: (b, 0, 0))]
    in_specs = [blk(M_WIDTH), blk(M_WIDTH), blk(M_WIDTH), blk(M_WIDTH), blk(LANES),
                pl.BlockSpec((1, LANES), lambda b, c: (0, 0)),
                pl.BlockSpec((1, M_WIDTH), lambda b, c: (0, 0))]
    args = [qm, km, vm, om, misc, gate_bias, mnorm]
    if state is not None:
        in_specs += st_specs
        args += list(state)
    return pl.pallas_call(
        functools.partial(_mlstm_kernel, nvalid, state is not None),
        grid=(B, nc),
        in_specs=in_specs,
        out_specs=[blk(M_WIDTH)] + st_specs,
        out_shape=[jax.ShapeDtypeStruct((B * T, M_WIDTH), hdt),
                   jax.ShapeDtypeStruct((B, M_HEADS, M_HEAD_DIM, M_HEAD_DIM), F32),
                   jax.ShapeDtypeStruct((B, M_HEADS, M_HEAD_DIM), F32),
                   jax.ShapeDtypeStruct((B, M_HEADS, LANES), F32)],
        compiler_params=_cparams(2),
        name="mlstm",
    )(*args)


def _sortable_key(score):
    bits = lax.bitcast_convert_type(score, I32)
    return bits ^ (lax.shift_right_arithmetic(bits, 31) & 0x7FFFFFFF)


def _key_to_score(key):
    return lax.bitcast_convert_type(key ^ (lax.shift_right_arithmetic(key, 31) & 0x7FFFFFFF), F32)


def _build_bias_strip(strip_ref, rb_ref, off, key_axis):
    ntiles, tile = strip_ref.shape[1], strip_ref.shape[2:]
    i = lax.broadcasted_iota(I32, tile, 1 - key_axis)
    x = lax.broadcasted_iota(I32, tile, key_axis)
    for t in range(ntiles):
        dist = i + (off - LANES * t) - x
        for h in range(N_HEADS):
            val = jnp.full(tile, rb_ref[0, h], F32)
            for b in range(1, N_BUCKETS):
                val = jnp.where(dist >= BUCKET_BOUNDS[b], rb_ref[b, h], val)
            strip_ref[h, t] = val


def _counter(score_ref, nk, ck, key_axis):
    def count(pred):
        def body(c, acc):
            c0 = pl.multiple_of(c * ck, ck)
            sc = score_ref[:, pl.ds(c0, ck)] if key_axis == 1 else score_ref[pl.ds(c0, ck), :]
            idx = c0 + lax.broadcasted_iota(I32, sc.shape, key_axis)
            hit = jnp.where(pred(sc, idx), 1, 0)
            if key_axis == 1:
                for j in range(ck // LANES):
                    acc = acc + hit[:, j * LANES:(j + 1) * LANES]
                return acc
            return acc + jnp.sum(hit.reshape(ck // SUBLANES, SUBLANES, hit.shape[1]), axis=0)

        nq = score_ref.shape[1 - key_axis]
        acc0 = jnp.zeros((nq, LANES) if key_axis == 1 else (SUBLANES, nq), I32)
        return jnp.sum(lax.fori_loop(0, nk, body, acc0), axis=key_axis, keepdims=True)

    return count


IMIN = jnp.iinfo(jnp.int32).min


def _kth_largest_by_count(count, qshape, n_sel):
    def bit_step(i, key):
        cand = key + lax.shift_left(jnp.int32(1), 31 - i)
        cand_score = _key_to_score(cand)
        return jnp.where(count(lambda sc, idx: sc >= cand_score) >= n_sel, cand, key)

    return _key_to_score(lax.fori_loop(0, 32, bit_step, jnp.full(qshape, IMIN, I32)))


def _bit_planes(words):
    a = list(words)
    j, m = 16, 0x0000FFFF
    while j:
        k = 0
        while k < 32:
            t = (a[k] ^ lax.shift_right_logical(a[k + j], j)) & m
            a[k] = a[k] ^ t
            a[k + j] = a[k + j] ^ lax.shift_left(t, j)
            k = (k + j + 1) & ~j
        j >>= 1
        m = (m ^ (m << j)) & 0xFFFFFFFF
        m = m - (1 << 32) if m >= (1 << 31) else m
    return a[::-1]


def _kth_largest_by_planes(score_ref, planes_ref, nk, ck, n_sel):
    nc = planes_ref.shape[1]
    nq = score_ref.shape[1]
    assert ck == 32 * SUBLANES

    def pack_chunk(c, _):
        c0 = pl.multiple_of(c * ck, ck)
        u = _sortable_key(score_ref[pl.ds(c0, ck), :]) ^ IMIN
        u = u.reshape(32, SUBLANES, nq)
        for b, plane in enumerate(_bit_planes([u[v] for v in range(32)])):
            planes_ref[b, c] = plane
        return 0

    lax.fori_loop(0, nk, pack_chunk, 0)

    def bit_step(i, carry):
        cand, n_above, thr_u = carry
        b = 31 - i
        ones = [cand[c] & planes_ref[b, c] for c in range(nc)]
        pop = lax.population_count(ones[0])
        for c in range(1, nc):
            pop = pop + lax.population_count(ones[c])
        tot = jnp.sum(pop, axis=0, keepdims=True)
        take = n_above + tot >= n_sel
        cand = tuple(jnp.where(take, o, m ^ o) for o, m in zip(ones, cand))
        n_above = jnp.where(take, n_above, n_above + tot)
        thr_u = thr_u | jnp.where(take, lax.shift_left(jnp.int32(1), b), 0)
        return cand, n_above, thr_u

    cand0 = tuple(jnp.where(c < nk, jnp.full((SUBLANES, nq), -1, I32), 0) for c in range(nc))
    zero = jnp.zeros((1, nq), I32)
    _, _, thr_u = lax.fori_loop(0, 32, bit_step, (cand0, zero, zero))
    return _key_to_score(thr_u ^ IMIN)


def _select_topk(thr, check, count, qshape, n_sel, idx_bits):
    imax = jnp.iinfo(jnp.int32).max

    def with_counts(t):
        return t, count(lambda sc, idx: sc > t), count(lambda sc, idx: sc >= t)

    thr, n_gt, n_ge = with_counts(thr)
    if check:
        good = jnp.min(jnp.where(n_gt < n_sel, jnp.where(n_ge >= n_sel, 1, 0), 0)) > 0
        thr, n_gt, n_ge = lax.cond(
            good, lambda _: (thr, n_gt, n_ge),
            lambda _: with_counts(_kth_largest_by_count(count, qshape, n_sel)), 0)
    need = n_sel - n_gt

    def tie_search(_):
        def idx_step(i, lo):
            cand = lo + lax.shift_left(jnp.int32(1), idx_bits - 1 - i)
            cnt = count(lambda sc, idx: jnp.where(sc == thr, idx, imax) < cand)
            return jnp.where(cnt < need, cand, lo)

        return lax.fori_loop(0, idx_bits, idx_step, jnp.zeros(qshape, I32))

    all_ties = jnp.full(qshape, imax, I32)
    jstar = lax.cond(jnp.max(n_ge) > n_sel, tie_search, lambda _: all_ties, 0)
    jstar = jnp.where(n_ge > n_sel, jstar, all_ties)
    return thr, jstar


def _valid_mask(scores, idx, thr, jstar, qpos):
    sel = jnp.where(scores > thr, 1, jnp.where(scores == thr, jnp.where(idx <= jstar, 1, 0), 0))
    return jnp.where(idx <= qpos, sel, 0) > 0


def _group_queries(qa, tq):
    lane = lax.broadcasted_iota(I32, (tq, LANES), 1)
    out = []
    for n in range(N_KV_HEADS):
        keep = (lane < HEAD_DIM) if n == 0 else (lane >= HEAD_DIM)
        tiles = [jnp.where(keep, qa[:, j * LANES:(j + 1) * LANES], jnp.zeros((), qa.dtype))
                 for j in range(GROUP)]
        out.append(jnp.concatenate(tiles, axis=0).astype(BF16))
    return out


def _write_attn(out_ref, carries, tq):
    lane = lax.broadcasted_iota(I32, (tq, LANES), 1)
    res = [acc / l for (_, l, acc) in carries]
    for j in range(GROUP):
        tile = jnp.where(lane < HEAD_DIM, res[0][j * tq:(j + 1) * tq], res[1][j * tq:(j + 1) * tq])
        out_ref[:, j * LANES:(j + 1) * LANES] = tile.astype(out_ref.dtype)


def _dsa_prompt_kernel(n_sel, idx_bits, rb_ref, qat_ref, qit_ref, misct_ref, ki_ref, k_ref, vt_ref,
                       out_ref, score_ref, planes_ref, strip_ref):
    qb = pl.program_id(1)
    q0 = qb * TQ
    nk = (q0 + TQ + CKP - 1) // CKP
    tiles_per_chunk = CKP // LANES
    back_tiles = strip_ref.shape[1] - tiles_per_chunk
    strip_off = LANES * back_tiles

    @pl.when((pl.program_id(0) == 0) & (qb == 0))
    def _():
        _build_bias_strip(strip_ref, rb_ref, strip_off, 0)
        planes_ref[...] = jnp.zeros(planes_ref.shape, I32)

    qpos = q0 + lax.broadcasted_iota(I32, (1, TQ), 1)

    qit = qit_ref[0]
    qstack = jnp.concatenate([qit[h * IDX_DIM:(h + 1) * IDX_DIM] for h in range(N_IDX_HEADS)],
                             axis=1)
    w = misct_ref[0, MISC_W:MISC_W + N_IDX_HEADS, :] * (N_IDX_HEADS ** -0.5 * IDX_DIM ** -0.5)

    def score_chunk(c, _):
        c0 = pl.multiple_of(c * CKP, CKP)
        d = jnp.dot(ki_ref[pl.ds(c0, CKP), :], qstack, preferred_element_type=F32)
        d = jnp.maximum(d, 0.0)
        s = jnp.zeros((CKP, TQ), F32)
        for h in range(N_IDX_HEADS):
            s = s + d[:, h * TQ:(h + 1) * TQ] * w[h:h + 1, :]
        idx = c0 + lax.broadcasted_iota(I32, (CKP, TQ), 0)
        score_ref[pl.ds(c0, CKP), :] = jnp.where(idx <= qpos, s, NEG)
        return 0

    lax.fori_loop(0, nk, score_chunk, 0)

    thr = _kth_largest_by_planes(score_ref, planes_ref, nk, CKP, n_sel)
    thr, jstar = _select_topk(thr, True, _counter(score_ref, nk, CKP, 0), (1, TQ), n_sel, idx_bits)

    qat = qat_ref[0]
    zeros = jnp.zeros((HEAD_DIM, TQ), qat.dtype)
    qgroups = []
    for n in range(N_KV_HEADS):
        tiles = []
        for g in range(GROUP):
            h = n * GROUP + g
            x = qat[h * HEAD_DIM:(h + 1) * HEAD_DIM]
            tiles.append(jnp.concatenate([x, zeros] if n == 0 else [zeros, x], axis=0))
        qgroups.append(jnp.concatenate(tiles, axis=1))

    def attend_chunk(c, carries):
        c0 = pl.multiple_of(c * CKP, CKP)
        idx = c0 + lax.broadcasted_iota(I32, (CKP, TQ), 0)
        valid = _valid_mask(score_ref[pl.ds(c0, CKP), :], idx, thr, jstar, qpos)
        kc = k_ref[pl.ds(c0, CKP), :]
        vct = vt_ref[0, c]
        tiles = [jnp.maximum(back_tiles + j - (qb * (TQ // LANES) - c * tiles_per_chunk), 0)
                 for j in range(tiles_per_chunk)]
        out = []
        for n in range(N_KV_HEADS):
            m_old, l_old, acc = carries[n]
            s = jnp.dot(kc, qgroups[n], preferred_element_type=F32)
            parts = []
            for g in range(GROUP):
                bias = jnp.concatenate([strip_ref[n * GROUP + g, t] for t in tiles], axis=0)
                parts.append(jnp.where(valid, s[:, g * TQ:(g + 1) * TQ] + bias, NEG))
            sm = jnp.concatenate(parts, axis=1)
            m_new = jnp.maximum(m_old, jnp.max(sm, axis=0, keepdims=True))
            alpha = jnp.exp(m_old - m_new)
            p = jnp.exp(sm - m_new)
            l_new = alpha * l_old + jnp.sum(p, axis=0, keepdims=True)
            acc = alpha * acc + jnp.dot(vct, p.astype(BF16), preferred_element_type=F32)
            out.append((m_new, l_new, acc))
        return tuple(out)

    init = (jnp.full((1, GROUP * TQ), NEG, F32), jnp.zeros((1, GROUP * TQ), F32),
            jnp.zeros((LANES, GROUP * TQ), F32))
    carries = lax.fori_loop(0, nk, attend_chunk, (init, init))
    res = [acc / l for (_, l, acc) in carries]
    row = lax.broadcasted_iota(I32, (LANES, TQ), 0)
    for j in range(GROUP):
        cols = slice(j * TQ, (j + 1) * TQ)
        tile_t = jnp.where(row < HEAD_DIM, res[0][:, cols], res[1][:, cols])
        out_ref[:, j * LANES:(j + 1) * LANES] = tile_t.T.astype(out_ref.dtype)


def _dsa_prompt(rel_bias, qat, qit, misct, kib, kb, vtb, B, T):
    assert T % CKP == 0 and T % TQ == 0
    nq = T // TQ
    n_sel = min(TOPK_MAX, T // 4)
    idx_bits = max(1, (T - 1).bit_length())
    qcols = lambda w: pl.BlockSpec((1, w, TQ), lambda b, q: (b, 0, q))
    seq = lambda w: pl.BlockSpec((T, w), lambda b, q: (b, 0))
    return pl.pallas_call(
        functools.partial(_dsa_prompt_kernel, n_sel, idx_bits),
        grid=(B, nq),
        in_specs=[pl.BlockSpec(memory_space=pltpu.SMEM), qcols(ATT_WIDTH), qcols(N_IDX_HEADS * IDX_DIM),
                  qcols(LANES), seq(IDX_DIM), seq(LANES),
                  pl.BlockSpec((1, T // CKP, LANES, CKP), lambda b, q: (b, 0, 0, 0))],
        out_specs=pl.BlockSpec((TQ, ATT_WIDTH), lambda b, q: (b * nq + q, 0)),
        out_shape=jax.ShapeDtypeStruct((B * T, ATT_WIDTH), BF16),
        scratch_shapes=[pltpu.VMEM((T, TQ), F32),
                        pltpu.VMEM((32, T // CKP, SUBLANES, TQ), I32),
                        pltpu.VMEM((N_HEADS, 2 + CKP // LANES, LANES, TQ), F32)],
        compiler_params=_cparams(2),
        name="dsa_prompt",
    )(rel_bias, qat, qit, misct, kib, kb, vtb)


def _page_pipeline(pt_ref, n_pages, caches, bufs, sems):
    def copies(bb, sl, j):
        pid = pt_ref[bb, j]
        cols = pl.ds(pl.multiple_of(j * PAGE_SIZE, PAGE_SIZE), PAGE_SIZE)
        return [pltpu.make_async_copy(c.at[pid], buf.at[sl, :, cols], sems.at[sl, i])
                for i, (c, buf) in enumerate(zip(caches, bufs))]

    def start_all(bb, sl):
        def body(j, _):
            for cp in copies(bb, sl, j):
                cp.start()
            return 0
        lax.fori_loop(0, n_pages, body, 0)

    def wait_all(bb, sl):
        def body(j, _):
            for cp in copies(bb, sl, j):
                cp.wait()
            return 0
        lax.fori_loop(0, n_pages, body, 0)

    def step():
        b = pl.program_id(0)
        slot = b % 2

        @pl.when(b == 0)
        def _():
            start_all(0, 0)

        @pl.when(b + 1 < pl.num_programs(0))
        def _():
            start_all(b + 1, 1 - slot)

        wait_all(b, slot)
        return slot

    return step


def _pad_rows(x, rows):
    return jnp.concatenate([x, jnp.zeros((rows - x.shape[0], x.shape[1]), x.dtype)], axis=0)


def _sample_score_kernel(n_sel, idx_bits, n_pages, ts, ck, pt_ref, qi_ref, misc_ref, kin_ref,
                         ckit_hbm, score_ref, thr_ref, jst_ref, ki_buf, sems):
    b = pl.program_id(0)
    past = n_pages * PAGE_SIZE
    slot = _page_pipeline(pt_ref, n_pages, [ckit_hbm], [ki_buf], sems)()

    qi = qi_ref[...]
    qstack = jnp.concatenate([qi[:, h * IDX_DIM:(h + 1) * IDX_DIM] for h in range(N_IDX_HEADS)],
                             axis=0).astype(BF16)
    w = misc_ref[:, MISC_W:MISC_W + N_IDX_HEADS] * (N_IDX_HEADS ** -0.5 * IDX_DIM ** -0.5)
    d_past = jnp.dot(qstack, ki_buf[slot].astype(BF16), preferred_element_type=F32)
    d_own = lax.dot_general(qstack, _pad_rows(kin_ref[...], PAGE_SIZE).astype(BF16), NT_DIMS,
                            preferred_element_type=F32)
    d = jnp.maximum(jnp.concatenate([d_past, d_own], axis=1), 0.0)
    s = jnp.zeros((ts, past + PAGE_SIZE), F32)
    for h in range(N_IDX_HEADS):
        s = s + d[h * ts:(h + 1) * ts] * w[:, h:h + 1]
    idx = lax.broadcasted_iota(I32, s.shape, 1)
    qpos = past + lax.broadcasted_iota(I32, (ts, 1), 0)
    score_ref[pl.ds(pl.multiple_of(b * ts, ts), ts), :] = jnp.where(idx <= qpos, s, NEG)

    @pl.when(b == pl.num_programs(0) - 1)
    def _():
        count = _counter(score_ref, score_ref.shape[1] // ck, ck, 1)
        qshape = (score_ref.shape[0], 1)
        thr = _kth_largest_by_count(count, qshape, n_sel)
        thr, jstar = _select_topk(thr, False, count, qshape, n_sel, idx_bits)
        thr_ref[...] = jnp.broadcast_to(thr, thr_ref.shape)
        jst_ref[...] = jnp.broadcast_to(jstar, jst_ref.shape)


def _sample_attend_kernel(n_pages, ts, pt_ref, rb_ref, qa_ref, score_ref, thr_ref, jst_ref, kn_ref,
                          vn_ref, ckt_hbm, cvt_hbm, out_ref, k_buf, v_buf, sems, strip_ref):
    b = pl.program_id(0)
    past = n_pages * PAGE_SIZE
    rows = N_HEADS * ts

    @pl.when(b == 0)
    def _():
        _build_bias_strip(strip_ref, rb_ref, LANES * (strip_ref.shape[1] - 1), 1)

    slot = _page_pipeline(pt_ref, n_pages, [ckt_hbm, cvt_hbm], [k_buf, v_buf], sems)()

    q2 = jnp.concatenate(_group_queries(qa_ref[...], ts), axis=0)
    k_own = _pad_rows(kn_ref[...], PAGE_SIZE).astype(BF16)
    v_own = _pad_rows(vn_ref[...], PAGE_SIZE).astype(BF16)
    s_past = jnp.dot(q2, k_buf[slot].astype(BF16), preferred_element_type=F32)
    s_own = lax.dot_general(q2, k_own, NT_DIMS, preferred_element_type=F32)
    far = strip_ref[:, 0].reshape(rows, LANES)[:, 0:1]
    near = [strip_ref[:, t].reshape(rows, LANES) for t in (1, 2)]
    s = jnp.concatenate([s_past[:, :past - PAGE_SIZE] + far, s_past[:, past - PAGE_SIZE:] + near[0],
                         s_own + near[1]], axis=1)

    qpos = past + lax.broadcasted_iota(I32, (ts, 1), 0)
    scores = score_ref[...]
    valid = _valid_mask(scores, lax.broadcasted_iota(I32, scores.shape, 1), thr_ref[:, 0:1],
                        jst_ref[:, 0:1], qpos)
    s = jnp.where(valid[None], s.reshape(N_HEADS, ts, past + PAGE_SIZE), NEG).reshape(rows, -1)
    m = jnp.max(s, axis=1, keepdims=True)
    p = jnp.exp(s - m)
    l = jnp.sum(p, axis=1, keepdims=True)
    pb = p.astype(BF16)
    pv = lax.dot_general(pb[:, :past], v_buf[slot].astype(BF16), NT_DIMS, preferred_element_type=F32)
    pv = pv + jnp.dot(pb[:, past:], v_own, preferred_element_type=F32)
    half = GROUP * ts
    carries = [(None, l[n * half:(n + 1) * half], pv[n * half:(n + 1) * half]) for n in range(N_KV_HEADS)]
    _write_attn(out_ref, carries, ts)


def _dsa_sample(page_table, rel_bias, qa, qi, misc, ki_new, k_new, v_new, ckit, ckt, cvt, DB, ts):
    n_pages = page_table.shape[1]
    past = n_pages * PAGE_SIZE
    n_sel = min(TOPK_MAX, (past + ts) // 4)
    lpad = past + PAGE_SIZE
    idx_bits = max(1, (lpad - 1).bit_length())
    ck = LANES * math.gcd(lpad // LANES, 5)
    assert ts % SUBLANES == 0 and ts <= PAGE_SIZE and n_pages >= 1
    blk = lambda w: pl.BlockSpec((ts, w), lambda b, pt: (b, 0))
    whole = lambda w: pl.BlockSpec((DB * ts, w), lambda b, pt: (0, 0))
    hbm = pl.BlockSpec(memory_space=pl.ANY)
    keys, thr, jstar = pl.pallas_call(
        functools.partial(_sample_score_kernel, n_sel, idx_bits, n_pages, ts, ck),
        grid_spec=pltpu.PrefetchScalarGridSpec(
            num_scalar_prefetch=1,
            grid=(DB,),
            in_specs=[blk(N_IDX_HEADS * IDX_DIM), blk(LANES), blk(IDX_DIM), hbm],
            out_specs=[whole(lpad), whole(LANES), whole(LANES)],
            scratch_shapes=[pltpu.VMEM((2, IDX_DIM, past), F32), pltpu.SemaphoreType.DMA((2, 1))]),
        out_shape=[jax.ShapeDtypeStruct((DB * ts, lpad), F32),
                   jax.ShapeDtypeStruct((DB * ts, LANES), F32),
                   jax.ShapeDtypeStruct((DB * ts, LANES), I32)],
        compiler_params=_cparams(1),
        name="sample_score",
    )(page_table, qi, misc, ki_new, ckit)
    return pl.pallas_call(
        functools.partial(_sample_attend_kernel, n_pages, ts),
        grid_spec=pltpu.PrefetchScalarGridSpec(
            num_scalar_prefetch=1,
            grid=(DB,),
            in_specs=[pl.BlockSpec(memory_space=pltpu.SMEM), blk(ATT_WIDTH), blk(lpad), blk(LANES),
                      blk(LANES), blk(LANES), blk(LANES), hbm, hbm],
            out_specs=blk(ATT_WIDTH),
            scratch_shapes=[pltpu.VMEM((2, LANES, past), F32), pltpu.VMEM((2, LANES, past), F32),
                            pltpu.SemaphoreType.DMA((2, 2)),
                            pltpu.VMEM((N_HEADS, 3, ts, LANES), F32)]),
        out_shape=jax.ShapeDtypeStruct((DB * ts, ATT_WIDTH), F32),
        compiler_params=_cparams(1),
        name="sample_attend",
    )(page_table, rel_bias, qa, keys, thr, jstar, k_new, v_new, ckt, cvt)


def _pack_layer_weights(w_in, b_i, b_f, w_out, w_up, w_down):
    D = w_in.shape[0]
    sizes = (ATT_WIDTH, N_KV_HEADS * HEAD_DIM, N_KV_HEADS * HEAD_DIM, N_IDX_HEADS * IDX_DIM, IDX_DIM,
             N_IDX_HEADS, M_WIDTH, M_WIDTH, M_WIDTH, M_WIDTH, M_HEADS, M_HEADS)
    assert w_in.shape[1] == sum(sizes)
    pts = np.cumsum((0,) + sizes)
    seg = [w_in[:, pts[i]:pts[i + 1]] for i in range(len(sizes))]
    qa, k, v, qi, ki, wi, qm, km, vm, om, im, fm = seg
    perm = np.asarray(HEAD_PERM)
    qa = qa.reshape(D, N_HEADS, HEAD_DIM)[:, perm].reshape(D, ATT_WIDTH)
    misc = jnp.concatenate([wi, im, fm, jnp.zeros((D, LANES - N_IDX_HEADS - 2 * M_HEADS), w_in.dtype)], axis=1)
    wp = jnp.concatenate([qa, k, v, qi, ki, ki, misc, qm, km, vm, om], axis=1).astype(BF16)
    assert wp.shape[1] == N_PACK
    gate_bias = jnp.concatenate([jnp.zeros((MISC_I,), F32), b_i.astype(F32), b_f.astype(F32),
                                 jnp.zeros((LANES - MISC_F - M_HEADS,), F32)]).reshape(1, LANES)
    woa = w_out[:ATT_WIDTH].reshape(N_HEADS, HEAD_DIM, -1)[perm].reshape(ATT_WIDTH, -1).astype(BF16)
    woh = w_out[ATT_WIDTH:].astype(BF16)
    wt = jnp.concatenate([seg[1].T, seg[2].T, ki.T, jnp.zeros((LANES - IDX_DIM, D), w_in.dtype),
                          seg[0].T, qi.T, misc.T], axis=0).astype(BF16)
    assert wt.shape[0] == N_TPACK
    return wp, wt, gate_bias, woa, woh, w_up.astype(BF16), w_down.astype(BF16)


def _layer(x, packed, g1, g2, mnorm, rel_bias, gf, final_norm, past):
    wp, wt, gate_bias, woa, woh, wup, wdn = packed
    B, T, D = x.shape
    x2 = x.reshape(B * T, D)
    tm = math.gcd(T if past is None else B * T, 512)
    kv_w = N_KV_HEADS * HEAD_DIM
    if past is None:
        (misc, qm, km, vm, om, kb, kib, kt, vt, kit, vtb, qat, qit, misct) = _inproj(
            x2, g1.reshape(1, D), wp, wt, B, T, tm)
        attn = _dsa_prompt(rel_bias, qat, qit, misct, kib, kb, vtb, B, T)
        state = None
        k_new = kt.reshape(B, N_KV_HEADS, HEAD_DIM, T).transpose(0, 3, 1, 2)
        v_new = vt.reshape(B, N_KV_HEADS, HEAD_DIM, T).transpose(0, 3, 1, 2)
        ki_new = kit.transpose(0, 2, 1)
    else:
        (misc, qm, km, vm, om, qa, qi, k, v, ki) = _inproj(x2, g1.reshape(1, D), wp, None, B, T, tm)
        page_table, cache_k, cache_v, cache_kidx, c0, n0, m0 = past
        n_pool = cache_k.shape[0]
        ckt = cache_k.transpose(0, 2, 3, 1).reshape(n_pool, kv_w, PAGE_SIZE)
        cvt = cache_v.transpose(0, 2, 3, 1).reshape(n_pool, kv_w, PAGE_SIZE)
        ckit = cache_kidx.transpose(0, 2, 1)
        attn = _dsa_sample(page_table, rel_bias, qa, qi, misc, ki, k, v, ckit, ckt, cvt, B, T)
        state = (c0, n0, jnp.broadcast_to(m0[..., None], m0.shape + (LANES,)))
        k_new = k.reshape(B, T, N_KV_HEADS, HEAD_DIM)
        v_new = v.reshape(B, T, N_KV_HEADS, HEAD_DIM)
        ki_new = ki.reshape(B, T, IDX_DIM)
    h, c_new, n_new, m_new = _mlstm(qm, km, vm, om, misc, gate_bias, mnorm.reshape(1, M_WIDTH), state,
                                    B, T, BF16 if past is None else F32)
    y = _post(x2, attn, h, woa, woh, g2.reshape(1, D), wup, wdn, gf.reshape(1, D), final_norm, tm)
    return (y.reshape(B, T, D), k_new, v_new, ki_new, c_new, n_new, m_new[..., 0])


def kernel(x_prompt, x_sample, cache_k, cache_v, cache_kidx, page_table, state_C, state_n, state_m,
           w_in, b_igate, b_fgate, mlstm_norm, rel_bias, w_out, norm1, norm2, w_up, w_down, norm_f):
    depth = w_in.shape[0]
    xp, xs = x_prompt, x_sample
    outs_p, outs_s = [], []
    for l in range(depth):
        packed = _pack_layer_weights(w_in[l], b_igate[l], b_fgate[l], w_out[l], w_up[l], w_down[l])
        last = l == depth - 1
        common = (packed, norm1[l], norm2[l], mlstm_norm[l], rel_bias, norm_f, last)
        rp = _layer(xp, *common, None)
        rs = _layer(xs, *common, (page_table, cache_k[l], cache_v[l], cache_kidx[l],
                                  state_C[l], state_n[l], state_m[l]))
        xp, xs = rp[0], rs[0]
        outs_p.append(rp[1:])
        outs_s.append(rs[1:])
    stack = lambda outs, i: jnp.stack([o[i] for o in outs])
    return ((xp, xs) + tuple(stack(outs_p, i) for i in range(6))
            + tuple(stack(outs_s, i) for i in range(6)))
```

```python
import functools
import math

import numpy as np
import jax
import jax.numpy as jnp
from jax import lax
from jax.experimental import pallas as pl
from jax.experimental.pallas import tpu as pltpu

F32 = jnp.float32
BF16 = jnp.bfloat16
I32 = jnp.int32

N_HEADS = 8
HEAD_DIM = 64
N_KV_HEADS = 2
GROUP = N_HEADS // N_KV_HEADS
N_IDX_HEADS = 8
IDX_DIM = 64
TOPK_MAX = 256
N_BUCKETS = 32
MAX_DISTANCE = 128
M_HEADS = 4
M_HEAD_DIM = 128
PAGE_SIZE = 128
EPS = 1e-6
NEG = -1e30
ATT_WIDTH = N_HEADS * HEAD_DIM
M_WIDTH = M_HEADS * M_HEAD_DIM

LANES = 128
SUBLANES = 8
VMEM_LIMIT = 56 * 1024 * 1024

C_QA = 0
C_K = C_QA + ATT_WIDTH
C_V = C_K + LANES
C_QI = C_V + LANES
C_KI2 = C_QI + N_IDX_HEADS * IDX_DIM
C_MISC = C_KI2 + LANES
C_QM = C_MISC + LANES
C_KM = C_QM + M_WIDTH
C_VM = C_KM + M_WIDTH
C_OM = C_VM + M_WIDTH
N_PACK = C_OM + M_WIDTH
MISC_W = 0
MISC_I = 8
MISC_F = 12

HEAD_PERM = (0, 4, 1, 5, 2, 6, 3, 7)

TQ = 256
CKP = 256
ML = 256

NT_DIMS = (((1,), (1,)), ((), ()))
TN_DIMS = (((0,), (0,)), ((), ()))


def _bucket_bounds():
    max_exact = N_BUCKETS // 2
    scale = (N_BUCKETS - max_exact) / math.log(MAX_DISTANCE / max_exact)

    def bucket(n, dt):
        if n < max_exact:
            return n
        val = np.log(np.asarray(max(n, 1), dt) / dt(max_exact)) * dt(scale)
        return min(max_exact + int(val), N_BUCKETS - 1)

    table = [bucket(n, np.float32) for n in range(MAX_DISTANCE + 2)]
    assert table == [bucket(n, np.float64) for n in range(MAX_DISTANCE + 2)]
    assert table[MAX_DISTANCE] == N_BUCKETS - 1
    return [next(d for d, b in enumerate(table) if b >= k) for k in range(N_BUCKETS)]


BUCKET_BOUNDS = _bucket_bounds()


def _cparams(n_axes):
    return pltpu.CompilerParams(dimension_semantics=("arbitrary",) * n_axes,
                                vmem_limit_bytes=VMEM_LIMIT)


def _const_spec(shape):
    nd = len(shape)
    return pl.BlockSpec(shape, lambda *_: (0,) * nd, pipeline_mode=pl.Buffered(1))


def _rms(x, g):
    return x * lax.rsqrt(jnp.mean(x * x, axis=-1, keepdims=True) + EPS) * g


def _inproj_mlstm(mm, misc_ref, qm_ref, km_ref, vm_ref, om_ref):
    misc_ref[...] = mm(C_MISC, LANES)
    qm_ref[...] = mm(C_QM, M_WIDTH).astype(qm_ref.dtype)
    km_ref[...] = (mm(C_KM, M_WIDTH) * (M_HEAD_DIM ** -0.5)).astype(km_ref.dtype)
    vm_ref[...] = mm(C_VM, M_WIDTH).astype(vm_ref.dtype)
    om_ref[...] = mm(C_OM, M_WIDTH)


def _inproj_rows_kernel(x_ref, g_ref, w_ref, misc_ref, qm_ref, km_ref, vm_ref, om_ref,
                        qa_ref, qi_ref, k_ref, v_ref, ki_ref):
    ub = _rms(x_ref[...], g_ref[...]).astype(BF16)
    mm = lambda c0, n: jnp.dot(ub, w_ref[:, c0:c0 + n], preferred_element_type=F32)
    _inproj_mlstm(mm, misc_ref, qm_ref, km_ref, vm_ref, om_ref)
    qa_ref[...] = mm(C_QA, ATT_WIDTH) * (HEAD_DIM ** -0.5)
    qi_ref[...] = mm(C_QI, N_IDX_HEADS * IDX_DIM)
    k_ref[...] = mm(C_K, LANES)
    v_ref[...] = mm(C_V, LANES)
    ki_ref[...] = mm(C_KI2, LANES)[:, :IDX_DIM]


R_K = 0
R_V = R_K + LANES
R_KI = R_V + LANES
R_QA = R_KI + LANES
R_QI = R_QA + ATT_WIDTH
R_MISC = R_QI + N_IDX_HEADS * IDX_DIM
N_TPACK = R_MISC + LANES


def _inproj_cols_kernel(x_ref, g_ref, w_ref, wt_ref, misc_ref, qm_ref, km_ref, vm_ref, om_ref,
                        kb_ref, kib_ref, kt_ref, vt_ref, kit_ref, vtb_ref, qat_ref, qit_ref,
                        misct_ref):
    ub = _rms(x_ref[...], g_ref[...]).astype(BF16)
    mm = lambda c0, n: jnp.dot(ub, w_ref[:, c0:c0 + n], preferred_element_type=F32)
    _inproj_mlstm(mm, misc_ref, qm_ref, km_ref, vm_ref, om_ref)
    kb_ref[...] = mm(C_K, LANES).astype(BF16)
    kib_ref[...] = mm(C_KI2, LANES)[:, :IDX_DIM].astype(BF16)

    def mt(r0, n):
        return lax.dot_general(wt_ref[r0:r0 + n, :], ub, NT_DIMS, preferred_element_type=F32)

    kt_ref[0] = mt(R_K, LANES)
    vt = mt(R_V, LANES)
    vt_ref[0] = vt
    for j in range(vtb_ref.shape[1]):
        vtb_ref[0, j] = vt[:, j * CKP:(j + 1) * CKP].astype(BF16)
    kit_ref[0] = mt(R_KI, IDX_DIM)
    qat_ref[0] = (mt(R_QA, ATT_WIDTH) * (HEAD_DIM ** -0.5)).astype(BF16)
    qit_ref[0] = mt(R_QI, N_IDX_HEADS * IDX_DIM).astype(BF16)
    misct_ref[0] = mt(R_MISC, LANES)


def _inproj(x2, g1, wp, wt, B, T, tm):
    R, D = x2.shape
    assert R == B * T and R % tm == 0
    mdt = F32 if wt is None else BF16
    row = lambda i: (i, 0)
    outs = [(LANES, F32), (M_WIDTH, mdt), (M_WIDTH, mdt), (M_WIDTH, mdt), (M_WIDTH, F32)]
    in_specs = [pl.BlockSpec((tm, D), row), _const_spec((1, D)), _const_spec((D, N_PACK))]
    if wt is None:
        kern, args = _inproj_rows_kernel, (x2, g1, wp)
        outs += [(ATT_WIDTH, F32), (N_IDX_HEADS * IDX_DIM, F32), (LANES, F32), (LANES, F32), (IDX_DIM, F32)]
    else:
        kern, args = _inproj_cols_kernel, (x2, g1, wp, wt)
        in_specs.append(_const_spec(wt.shape))
        outs += [(LANES, BF16), (IDX_DIM, BF16)]
    out_specs = [pl.BlockSpec((tm, w), row) for w, _ in outs]
    out_shape = [jax.ShapeDtypeStruct((R, w), dt) for w, dt in outs]
    if wt is not None:
        assert T % tm == 0 and tm % CKP == 0
        tpb, cpt = T // tm, tm // CKP
        cols = lambda i: (i // tpb, 0, i % tpb)
        for w, dt in ((LANES, F32), (LANES, F32), (IDX_DIM, F32)):
            out_specs.append(pl.BlockSpec((1, w, tm), cols))
            out_shape.append(jax.ShapeDtypeStruct((B, w, T), dt))
        out_specs.append(pl.BlockSpec((1, cpt, LANES, CKP), lambda i: (i // tpb, i % tpb, 0, 0)))
        out_shape.append(jax.ShapeDtypeStruct((B, T // CKP, LANES, CKP), BF16))
        for w, dt in ((ATT_WIDTH, BF16), (N_IDX_HEADS * IDX_DIM, BF16), (LANES, F32)):
            out_specs.append(pl.BlockSpec((1, w, tm), cols))
            out_shape.append(jax.ShapeDtypeStruct((B, w, T), dt))
    return pl.pallas_call(
        kern,
        grid=(R // tm,),
        in_specs=in_specs,
        out_specs=out_specs,
        out_shape=out_shape,
        compiler_params=_cparams(1),
        name="inproj",
    )(*args)


def _post_kernel(ff_chunk, final_norm, x_ref, a_ref, h_ref, woa_ref, woh_ref, g2_ref, wup_ref,
                 wdn_ref, gf_ref, y_ref):
    mix = jnp.dot(a_ref[...].astype(BF16), woa_ref[...], preferred_element_type=F32)
    mix = mix + jnp.dot(h_ref[...].astype(BF16), woh_ref[...], preferred_element_type=F32)
    hres = x_ref[...] + mix
    f = _rms(hres, g2_ref[...]).astype(BF16)
    acc = hres
    for c0 in range(0, wup_ref.shape[1], ff_chunk):
        up = jnp.dot(f, wup_ref[:, c0:c0 + ff_chunk], preferred_element_type=F32)
        r = jnp.maximum(up, 0.0)
        acc = acc + jnp.dot((r * r).astype(BF16), wdn_ref[c0:c0 + ff_chunk, :],
                            preferred_element_type=F32)
    y_ref[...] = _rms(acc, gf_ref[...]) if final_norm else acc


def _post(x2, attn, h, woa, woh, g2, wup, wdn, gf, final_norm, tm):
    R, D = x2.shape
    dff = wup.shape[1]
    assert R % tm == 0
    row = lambda i: (i, 0)
    return pl.pallas_call(
        functools.partial(_post_kernel, min(dff, 1024), final_norm),
        grid=(R // tm,),
        in_specs=[pl.BlockSpec((tm, D), row), pl.BlockSpec((tm, ATT_WIDTH), row),
                  pl.BlockSpec((tm, M_WIDTH), row), _const_spec(woa.shape), _const_spec(woh.shape),
                  _const_spec((1, D)), _const_spec(wup.shape), _const_spec(wdn.shape),
                  _const_spec((1, D))],
        out_specs=pl.BlockSpec((tm, D), row),
        out_shape=jax.ShapeDtypeStruct((R, D), F32),
        compiler_params=_cparams(1),
        name="post",
    )(x2, attn, h, woa, woh, g2, wup, wdn, gf)


def _log_sigmoid(x):
    return -(jnp.maximum(-x, 0.0) + jnp.log1p(jnp.exp(-jnp.abs(x))))


def _mlstm_kernel(nvalid, has_state, *refs):
    if has_state:
        (q_ref, k_ref, v_ref, o_ref, misc_ref, gb_ref, mn_ref, c0_ref, n0_ref, m0_ref,
         h_ref, c_ref, n_ref, m_ref, z_scr, rows_scr) = refs
    else:
        (q_ref, k_ref, v_ref, o_ref, misc_ref, gb_ref, mn_ref,
         h_ref, c_ref, n_ref, m_ref, z_scr, rows_scr) = refs
    nb, nc, L = z_scr.shape[0], z_scr.shape[1], z_scr.shape[2]
    c = pl.program_id(1)

    def padded(x, dt):
        x = x.astype(dt)
        if nvalid == L:
            return x
        return jnp.concatenate([x, jnp.zeros((L - nvalid, x.shape[1]), dt)], axis=0)

    r2 = lax.broadcasted_iota(I32, (L, L), 0)
    c2 = lax.broadcasted_iota(I32, (L, L), 1)
    tril = r2 >= c2

    @pl.when(c == 0)
    def _():
        if has_state:
            c_ref[...] = c0_ref[...]
            n_ref[...] = n0_ref[...]
            m_ref[...] = m0_ref[...]
        else:
            c_ref[...] = jnp.zeros_like(c_ref)
            n_ref[...] = jnp.zeros_like(n_ref)
            m_ref[...] = jnp.zeros_like(m_ref)
        assert MISC_F == MISC_I + M_HEADS and MISC_I % SUBLANES == 0
        row8 = lax.broadcasted_iota(I32, (SUBLANES, L), 0)
        tok8 = lax.broadcasted_iota(I32, (SUBLANES, L), 1)
        bias8 = jnp.concatenate([gb_ref[...]] * (L // LANES), axis=1)
        triu = (r2 <= c2).astype(F32)
        for s in range(nb):
            for cc in range(nc):
                misc_t = padded(misc_ref[s, cc * nvalid:(cc + 1) * nvalid, :], F32).T
                gx = misc_t[MISC_I:MISC_I + SUBLANES] + bias8
                gates = jnp.where(row8 >= M_HEADS, _log_sigmoid(gx), gx)
                if nvalid != L:
                    gates = jnp.where(tok8 < nvalid, gates, jnp.where(row8 >= M_HEADS, 0.0, NEG))
                cum = jnp.dot(gates, triu, precision=lax.Precision.HIGHEST,
                              preferred_element_type=F32)
                rows = jnp.where(row8 >= M_HEADS, cum, gates)
                rows_scr[s, cc] = rows
                z_scr[s, cc] = jnp.concatenate([rows, jnp.zeros((LANES - SUBLANES, L), F32)], axis=0).T

    loaded = []
    for s in range(nb):
        loaded.append((z_scr[s, c], rows_scr[s, c], padded(q_ref[s], BF16), padded(k_ref[s], BF16),
                       padded(v_ref[s], BF16), padded(o_ref[s], F32), m_ref[s], c_ref[s], n_ref[s]))
    chains = [(s, hd) for s in range(nb) for hd in range(M_HEADS)]

    def operands(s, hd):
        z, rows, qb, kb, vb, ob, m_all, c_all, n_all = loaded[s]
        sl = slice(hd * M_HEAD_DIM, (hd + 1) * M_HEAD_DIM)
        return dict(
            sl=sl, q=qb[:, sl], k=kb[:, sl], v=vb[:, sl], o=ob[:, sl],
            icol=z[:, hd:hd + 1], bcol=z[:, M_HEADS + hd:M_HEADS + hd + 1],
            irow=rows[hd:hd + 1, :], brow=rows[M_HEADS + hd:M_HEADS + hd + 1, :],
            m_prev=m_all[hd:hd + 1, 0:1], s_prev=c_all[hd], n_prev=n_all[hd:hd + 1, :])

    st = [operands(s, hd) for s, hd in chains]
    for x in st:
        x["qk"] = lax.dot_general(x["q"], x["k"], NT_DIMS, preferred_element_type=F32)
        x["qs"] = jnp.dot(x["q"], x["s_prev"].astype(BF16), preferred_element_type=F32)
    for x in st:
        g = x["bcol"] + x["m_prev"]
        dm = jnp.where(tril, x["bcol"] - x["brow"] + x["irow"], NEG)
        x["mt"] = jnp.maximum(g, jnp.max(dm, axis=1, keepdims=True))
        x["gw"] = jnp.exp(g - x["mt"])
        x["qk"] = x["qk"] * jnp.exp(dm - x["mt"])
        b_last = x["bcol"][L - 1:L, :]
        g_last = b_last + x["m_prev"]
        a = b_last - x["bcol"] + x["icol"]
        x["m_new"] = jnp.maximum(g_last, jnp.max(a, axis=0, keepdims=True))
        x["sw"] = jnp.exp(g_last - x["m_new"])
        x["ak"] = jnp.exp(a - x["m_new"]) * x["k"].astype(F32)
    for x in st:
        x["pv"] = jnp.dot(x["qk"].astype(BF16), x["v"], preferred_element_type=F32)
        x["kv"] = lax.dot_general(x["ak"].astype(BF16), x["v"], TN_DIMS, preferred_element_type=F32)
    stores = []
    for (s, hd), x in zip(chains, st):
        num = x["gw"] * x["qs"] + x["pv"]
        den = x["gw"] * jnp.sum(x["q"].astype(F32) * x["n_prev"], axis=1, keepdims=True)
        den = den + jnp.sum(x["qk"], axis=1, keepdims=True)
        hh = num / jnp.maximum(jnp.abs(den), jnp.exp(-x["mt"]))
        hh = hh * lax.rsqrt(jnp.mean(hh * hh, axis=-1, keepdims=True) + EPS)
        hh = hh * mn_ref[:, x["sl"]] * jax.nn.sigmoid(x["o"])
        c_new = x["sw"] * x["s_prev"] + x["kv"]
        n_new = x["sw"] * x["n_prev"] + jnp.sum(x["ak"], axis=0, keepdims=True)
        stores.append((s, hd, x["sl"], hh[:nvalid].astype(h_ref.dtype), c_new, n_new,
                       jnp.broadcast_to(x["m_new"], (1, LANES))))
    for s, hd, sl, h_new, c_new, n_new, m_new in stores:
        h_ref[s, :, sl] = h_new
        c_ref[s, hd] = c_new
        n_ref[s, hd:hd + 1, :] = n_new
        m_ref[s, hd:hd + 1, :] = m_new


def _mlstm(qm, km, vm, om, misc, gate_bias, mnorm, state, B, T, hdt):
    L = next((c for c in (ML, LANES) if T % c == 0), LANES)
    nvalid = L if T % L == 0 else T
    assert nvalid <= L and T % nvalid == 0 and nvalid % SUBLANES == 0
    nc = T // nvalid
    nb = next(n for n in ((2, 1) if nvalid == L else (4, 2, 1)) if B % n == 0)
    seq3 = lambda a: a.reshape(B, T, a.shape[-1])
    blk = lambda w: pl.BlockSpec((nb, nvalid, w), lambda b, c: (b, c, 0))
    st_specs = [pl.BlockSpec((nb, M_HEADS, M_HEAD_DIM, M_HEAD_DIM), lambda b, c: (b, 0, 0, 0)),
                pl.BlockSpec((nb, M_HEADS, M_HEAD_DIM), lambda b, c: (b, 0, 0)),
                pl.BlockSpec((nb, M_HEADS, LANES), lambda b, c: (b, 0, 0))]
    in_specs = [blk(M_WIDTH), blk(M_WIDTH), blk(M_WIDTH), blk(M_WIDTH),
                pl.BlockSpec((nb, T, LANES), lambda b, c: (b, 0, 0)),
                pl.BlockSpec((SUBLANES, LANES), lambda b, c: (0, 0)),
                pl.BlockSpec((1, M_WIDTH), lambda b, c: (0, 0))]
    args = [seq3(qm), seq3(km), seq3(vm), seq3(om), seq3(misc), gate_bias, mnorm]
    if state is not None:
        in_specs += st_specs
        args += list(state)
    h, c_new, n_new, m_new = pl.pallas_call(
        functools.partial(_mlstm_kernel, nvalid, state is not None),
        grid=(B // nb, nc),
        in_specs=in_specs,
        out_specs=[blk(M_WIDTH)] + st_specs,
        out_shape=[jax.ShapeDtypeStruct((B, T, M_WIDTH), hdt),
                   jax.ShapeDtypeStruct((B, M_HEADS, M_HEAD_DIM, M_HEAD_DIM), F32),
                   jax.ShapeDtypeStruct((B, M_HEADS, M_HEAD_DIM), F32),
                   jax.ShapeDtypeStruct((B, M_HEADS, LANES), F32)],
        scratch_shapes=[pltpu.VMEM((nb, nc, L, LANES), F32), pltpu.VMEM((nb, nc, SUBLANES, L), F32)],
        compiler_params=_cparams(2),
        name="mlstm",
    )(*args)
    return h.reshape(B * T, M_WIDTH), c_new, n_new, m_new


def _sortable_key(score):
    bits = lax.bitcast_convert_type(score, I32)
    return bits ^ (lax.shift_right_arithmetic(bits, 31) & 0x7FFFFFFF)


def _key_to_score(key):
    return lax.bitcast_convert_type(key ^ (lax.shift_right_arithmetic(key, 31) & 0x7FFFFFFF), F32)


def _build_bias_strip(strip_ref, rb_ref, off, key_axis):
    ntiles, tile = strip_ref.shape[1], strip_ref.shape[2:]
    i = lax.broadcasted_iota(I32, tile, 1 - key_axis)
    x = lax.broadcasted_iota(I32, tile, key_axis)
    for t in range(ntiles):
        dist = i + (off - LANES * t) - x
        for h in range(N_HEADS):
            val = jnp.full(tile, rb_ref[0, h], F32)
            for b in range(1, N_BUCKETS):
                val = jnp.where(dist >= BUCKET_BOUNDS[b], rb_ref[b, h], val)
            strip_ref[h, t] = val


def _counter(score_ref, nk, ck, key_axis):
    def count(pred):
        def body(c, acc):
            c0 = pl.multiple_of(c * ck, ck)
            sc = score_ref[:, pl.ds(c0, ck)] if key_axis == 1 else score_ref[pl.ds(c0, ck), :]
            idx = c0 + lax.broadcasted_iota(I32, sc.shape, key_axis)
            hit = jnp.where(pred(sc, idx), 1, 0)
            if key_axis == 1:
                for j in range(ck // LANES):
                    acc = acc + hit[:, j * LANES:(j + 1) * LANES]
                return acc
            return acc + jnp.sum(hit.reshape(ck // SUBLANES, SUBLANES, hit.shape[1]), axis=0)

        nq = score_ref.shape[1 - key_axis]
        acc0 = jnp.zeros((nq, LANES) if key_axis == 1 else (SUBLANES, nq), I32)
        return jnp.sum(lax.fori_loop(0, nk, body, acc0), axis=key_axis, keepdims=True)

    return count


IMIN = jnp.iinfo(jnp.int32).min


def _kth_largest_by_count(count, qshape, n_sel):
    def bit_step(i, key):
        cand = key + lax.shift_left(jnp.int32(1), 31 - i)
        cand_score = _key_to_score(cand)
        return jnp.where(count(lambda sc, idx: sc >= cand_score) >= n_sel, cand, key)

    return _key_to_score(lax.fori_loop(0, 32, bit_step, jnp.full(qshape, IMIN, I32)))


def _bit_planes(words):
    a = list(words)
    j, m = 16, 0x0000FFFF
    while j:
        k = 0
        while k < 32:
            t = (a[k] ^ lax.shift_right_logical(a[k + j], j)) & m
            a[k] = a[k] ^ t
            a[k + j] = a[k + j] ^ lax.shift_left(t, j)
            k = (k + j + 1) & ~j
        j >>= 1
        m = (m ^ (m << j)) & 0xFFFFFFFF
        m = m - (1 << 32) if m >= (1 << 31) else m
    return a[::-1]


def _kth_largest_by_planes(score_ref, planes_ref, nk, ck, n_sel):
    nc = planes_ref.shape[1]
    nq = score_ref.shape[1]
    assert ck == 32 * SUBLANES

    def pack_chunk(c, _):
        c0 = pl.multiple_of(c * ck, ck)
        u = _sortable_key(score_ref[pl.ds(c0, ck), :]) ^ IMIN
        u = u.reshape(32, SUBLANES, nq)
        for b, plane in enumerate(_bit_planes([u[v] for v in range(32)])):
            planes_ref[b, c] = plane
        return 0

    lax.fori_loop(0, nk, pack_chunk, 0)

    def bit_step(i, carry):
        cand, n_above, thr_u = carry
        b = 31 - i
        ones = [cand[c] & planes_ref[b, c] for c in range(nc)]
        pop = lax.population_count(ones[0])
        for c in range(1, nc):
            pop = pop + lax.population_count(ones[c])
        tot = jnp.sum(pop, axis=0, keepdims=True)
        take = n_above + tot >= n_sel
        cand = tuple(jnp.where(take, o, m ^ o) for o, m in zip(ones, cand))
        n_above = jnp.where(take, n_above, n_above + tot)
        thr_u = thr_u | jnp.where(take, lax.shift_left(jnp.int32(1), b), 0)
        return cand, n_above, thr_u

    cand0 = tuple(jnp.where(c < nk, jnp.full((SUBLANES, nq), -1, I32), 0) for c in range(nc))
    zero = jnp.zeros((1, nq), I32)
    _, _, thr_u = lax.fori_loop(0, 32, bit_step, (cand0, zero, zero))
    return _key_to_score(thr_u ^ IMIN)


def _select_topk(thr, check, count, qshape, n_sel, idx_bits):
    imax = jnp.iinfo(jnp.int32).max

    def with_counts(t):
        return t, count(lambda sc, idx: sc > t), count(lambda sc, idx: sc >= t)

    thr, n_gt, n_ge = with_counts(thr)
    if check:
        good = jnp.min(jnp.where(n_gt < n_sel, jnp.where(n_ge >= n_sel, 1, 0), 0)) > 0
        thr, n_gt, n_ge = lax.cond(
            good, lambda _: (thr, n_gt, n_ge),
            lambda _: with_counts(_kth_largest_by_count(count, qshape, n_sel)), 0)
    need = n_sel - n_gt

    def tie_search(_):
        def idx_step(i, lo):
            cand = lo + lax.shift_left(jnp.int32(1), idx_bits - 1 - i)
            cnt = count(lambda sc, idx: jnp.where(sc == thr, idx, imax) < cand)
            return jnp.where(cnt < need, cand, lo)

        return lax.fori_loop(0, idx_bits, idx_step, jnp.zeros(qshape, I32))

    all_ties = jnp.full(qshape, imax, I32)
    jstar = lax.cond(jnp.max(n_ge) > n_sel, tie_search, lambda _: all_ties, 0)
    jstar = jnp.where(n_ge > n_sel, jstar, all_ties)
    return thr, jstar


def _valid_mask(scores, idx, thr, jstar, qpos):
    sel = jnp.where(scores > thr, 1, jnp.where(scores == thr, jnp.where(idx <= jstar, 1, 0), 0))
    return jnp.where(idx <= qpos, sel, 0) > 0


def _group_queries(qa, tq):
    lane = lax.broadcasted_iota(I32, (tq, LANES), 1)
    out = []
    for n in range(N_KV_HEADS):
        keep = (lane < HEAD_DIM) if n == 0 else (lane >= HEAD_DIM)
        tiles = [jnp.where(keep, qa[:, j * LANES:(j + 1) * LANES], jnp.zeros((), qa.dtype))
                 for j in range(GROUP)]
        out.append(jnp.concatenate(tiles, axis=0).astype(BF16))
    return out


def _write_attn(out_ref, carries, tq):
    lane = lax.broadcasted_iota(I32, (tq, LANES), 1)
    res = [acc / l for (_, l, acc) in carries]
    for j in range(GROUP):
        tile = jnp.where(lane < HEAD_DIM, res[0][j * tq:(j + 1) * tq], res[1][j * tq:(j + 1) * tq])
        out_ref[:, j * LANES:(j + 1) * LANES] = tile.astype(out_ref.dtype)


def _dsa_prompt_kernel(n_sel, idx_bits, rb_ref, qat_ref, qit_ref, misct_ref, ki_ref, k_ref, vt_ref,
                       out_ref, score_ref, planes_ref, strip_ref):
    qb = pl.program_id(1)
    q0 = qb * TQ
    nk = (q0 + TQ + CKP - 1) // CKP
    tiles_per_chunk = CKP // LANES
    back_tiles = strip_ref.shape[1] - tiles_per_chunk
    strip_off = LANES * back_tiles

    @pl.when((pl.program_id(0) == 0) & (qb == 0))
    def _():
        _build_bias_strip(strip_ref, rb_ref, strip_off, 0)
        planes_ref[...] = jnp.zeros(planes_ref.shape, I32)

    qpos = q0 + lax.broadcasted_iota(I32, (1, TQ), 1)

    qit = qit_ref[0]
    qstack = jnp.concatenate([qit[h * IDX_DIM:(h + 1) * IDX_DIM] for h in range(N_IDX_HEADS)],
                             axis=1)
    w = misct_ref[0, MISC_W:MISC_W + N_IDX_HEADS, :] * (N_IDX_HEADS ** -0.5 * IDX_DIM ** -0.5)

    def score_chunk(c, _):
        c0 = pl.multiple_of(c * CKP, CKP)
        d = jnp.dot(ki_ref[pl.ds(c0, CKP), :], qstack, preferred_element_type=F32)
        d = jnp.maximum(d, 0.0)
        s = jnp.zeros((CKP, TQ), F32)
        for h in range(N_IDX_HEADS):
            s = s + d[:, h * TQ:(h + 1) * TQ] * w[h:h + 1, :]
        idx = c0 + lax.broadcasted_iota(I32, (CKP, TQ), 0)
        score_ref[pl.ds(c0, CKP), :] = jnp.where(idx <= qpos, s, NEG)
        return 0

    lax.fori_loop(0, nk, score_chunk, 0)

    thr = _kth_largest_by_planes(score_ref, planes_ref, nk, CKP, n_sel)
    thr, jstar = _select_topk(thr, True, _counter(score_ref, nk, CKP, 0), (1, TQ), n_sel, idx_bits)

    qat = qat_ref[0]
    zeros = jnp.zeros((HEAD_DIM, TQ), qat.dtype)
    qgroups = []
    for n in range(N_KV_HEADS):
        tiles = []
        for g in range(GROUP):
            h = n * GROUP + g
            x = qat[h * HEAD_DIM:(h + 1) * HEAD_DIM]
            tiles.append(jnp.concatenate([x, zeros] if n == 0 else [zeros, x], axis=0))
        qgroups.append(jnp.concatenate(tiles, axis=1))

    def attend_chunk(c, carries):
        c0 = pl.multiple_of(c * CKP, CKP)
        idx = c0 + lax.broadcasted_iota(I32, (CKP, TQ), 0)
        valid = _valid_mask(score_ref[pl.ds(c0, CKP), :], idx, thr, jstar, qpos)
        kc = k_ref[pl.ds(c0, CKP), :]
        vct = vt_ref[0, c]
        tiles = [jnp.maximum(back_tiles + j - (qb * (TQ // LANES) - c * tiles_per_chunk), 0)
                 for j in range(tiles_per_chunk)]
        out = []
        for n in range(N_KV_HEADS):
            m_old, l_old, acc = carries[n]
            s = jnp.dot(kc, qgroups[n], preferred_element_type=F32)
            parts = []
            for g in range(GROUP):
                bias = jnp.concatenate([strip_ref[n * GROUP + g, t] for t in tiles], axis=0)
                parts.append(jnp.where(valid, s[:, g * TQ:(g + 1) * TQ] + bias, NEG))
            sm = jnp.concatenate(parts, axis=1)
            m_new = jnp.maximum(m_old, jnp.max(sm, axis=0, keepdims=True))
            alpha = jnp.exp(m_old - m_new)
            p = jnp.exp(sm - m_new)
            l_new = alpha * l_old + jnp.sum(p, axis=0, keepdims=True)
            acc = alpha * acc + jnp.dot(vct, p.astype(BF16), preferred_element_type=F32)
            out.append((m_new, l_new, acc))
        return tuple(out)

    init = (jnp.full((1, GROUP * TQ), NEG, F32), jnp.zeros((1, GROUP * TQ), F32),
            jnp.zeros((LANES, GROUP * TQ), F32))
    carries = lax.fori_loop(0, nk // 2, lambda i, cr: attend_chunk(2 * i + 1, attend_chunk(2 * i, cr)),
                            (init, init))
    carries = lax.fori_loop(2 * (nk // 2), nk, attend_chunk, carries)
    res = [acc / l for (_, l, acc) in carries]
    row = lax.broadcasted_iota(I32, (LANES, TQ), 0)
    for j in range(GROUP):
        cols = slice(j * TQ, (j + 1) * TQ)
        tile_t = jnp.where(row < HEAD_DIM, res[0][:, cols], res[1][:, cols])
        out_ref[:, j * LANES:(j + 1) * LANES] = tile_t.T.astype(out_ref.dtype)


def _dsa_prompt(rel_bias, qat, qit, misct, kib, kb, vtb, B, T):
    assert T % CKP == 0 and T % TQ == 0
    nq = T // TQ
    n_sel = min(TOPK_MAX, T // 4)
    idx_bits = max(1, (T - 1).bit_length())
    qcols = lambda w: pl.BlockSpec((1, w, TQ), lambda b, q: (b, 0, q))
    seq = lambda w: pl.BlockSpec((T, w), lambda b, q: (b, 0))
    return pl.pallas_call(
        functools.partial(_dsa_prompt_kernel, n_sel, idx_bits),
        grid=(B, nq),
        in_specs=[pl.BlockSpec(memory_space=pltpu.SMEM), qcols(ATT_WIDTH), qcols(N_IDX_HEADS * IDX_DIM),
                  qcols(LANES), seq(IDX_DIM), seq(LANES),
                  pl.BlockSpec((1, T // CKP, LANES, CKP), lambda b, q: (b, 0, 0, 0))],
        out_specs=pl.BlockSpec((TQ, ATT_WIDTH), lambda b, q: (b * nq + q, 0)),
        out_shape=jax.ShapeDtypeStruct((B * T, ATT_WIDTH), BF16),
        scratch_shapes=[pltpu.VMEM((T, TQ), F32),
                        pltpu.VMEM((32, T // CKP, SUBLANES, TQ), I32),
                        pltpu.VMEM((N_HEADS, 2 + CKP // LANES, LANES, TQ), F32)],
        compiler_params=_cparams(2),
        name="dsa_prompt",
    )(rel_bias, qat, qit, misct, kib, kb, vtb)


def _page_pipeline(pt_ref, n_pages, caches, bufs, sems):
    def copies(bb, sl, j):
        pid = pt_ref[bb, j]
        cols = pl.ds(pl.multiple_of(j * PAGE_SIZE, PAGE_SIZE), PAGE_SIZE)
        return [pltpu.make_async_copy(c.at[pid], buf.at[sl, :, cols], sems.at[sl, i])
                for i, (c, buf) in enumerate(zip(caches, bufs))]

    def start_all(bb, sl):
        def body(j, _):
            for cp in copies(bb, sl, j):
                cp.start()
            return 0
        lax.fori_loop(0, n_pages, body, 0)

    def wait_all(bb, sl):
        def body(j, _):
            for cp in copies(bb, sl, j):
                cp.wait()
            return 0
        lax.fori_loop(0, n_pages, body, 0)

    def step():
        b = pl.program_id(0)
        slot = b % 2

        @pl.when(b == 0)
        def _():
            start_all(0, 0)

        @pl.when(b + 1 < pl.num_programs(0))
        def _():
            start_all(b + 1, 1 - slot)

        wait_all(b, slot)
        return slot

    return step


def _pad_rows(x, rows):
    return jnp.concatenate([x, jnp.zeros((rows - x.shape[0], x.shape[1]), x.dtype)], axis=0)


def _sample_score_kernel(n_sel, idx_bits, n_pages, ts, ck, pt_ref, qi_ref, misc_ref, kin_ref,
                         ckit_hbm, score_ref, thr_ref, jst_ref, ki_buf, sems):
    b = pl.program_id(0)
    past = n_pages * PAGE_SIZE
    slot = _page_pipeline(pt_ref, n_pages, [ckit_hbm], [ki_buf], sems)()

    qi = qi_ref[...]
    qstack = jnp.concatenate([qi[:, h * IDX_DIM:(h + 1) * IDX_DIM] for h in range(N_IDX_HEADS)],
                             axis=0).astype(BF16)
    w = misc_ref[:, MISC_W:MISC_W + N_IDX_HEADS] * (N_IDX_HEADS ** -0.5 * IDX_DIM ** -0.5)
    d_past = jnp.dot(qstack, ki_buf[slot].astype(BF16), preferred_element_type=F32)
    d_own = lax.dot_general(qstack, _pad_rows(kin_ref[...], PAGE_SIZE).astype(BF16), NT_DIMS,
                            preferred_element_type=F32)
    d = jnp.maximum(jnp.concatenate([d_past, d_own], axis=1), 0.0)
    s = jnp.zeros((ts, past + PAGE_SIZE), F32)
    for h in range(N_IDX_HEADS):
        s = s + d[h * ts:(h + 1) * ts] * w[:, h:h + 1]
    idx = lax.broadcasted_iota(I32, s.shape, 1)
    qpos = past + lax.broadcasted_iota(I32, (ts, 1), 0)
    score_ref[pl.ds(pl.multiple_of(b * ts, ts), ts), :] = jnp.where(idx <= qpos, s, NEG)

    @pl.when(b == pl.num_programs(0) - 1)
    def _():
        count = _counter(score_ref, score_ref.shape[1] // ck, ck, 1)
        qshape = (score_ref.shape[0], 1)
        thr = _kth_largest_by_count(count, qshape, n_sel)
        thr, jstar = _select_topk(thr, False, count, qshape, n_sel, idx_bits)
        thr_ref[...] = jnp.broadcast_to(thr, thr_ref.shape)
        jst_ref[...] = jnp.broadcast_to(jstar, jst_ref.shape)


def _sample_attend_kernel(n_pages, ts, pt_ref, rb_ref, qa_ref, score_ref, thr_ref, jst_ref, kn_ref,
                          vn_ref, ckt_hbm, cvt_hbm, out_ref, k_buf, v_buf, sems, strip_ref):
    b = pl.program_id(0)
    past = n_pages * PAGE_SIZE
    rows = N_HEADS * ts

    @pl.when(b == 0)
    def _():
        _build_bias_strip(strip_ref, rb_ref, LANES * (strip_ref.shape[1] - 1), 1)

    slot = _page_pipeline(pt_ref, n_pages, [ckt_hbm, cvt_hbm], [k_buf, v_buf], sems)()

    q2 = jnp.concatenate(_group_queries(qa_ref[...], ts), axis=0)
    k_own = _pad_rows(kn_ref[...], PAGE_SIZE).astype(BF16)
    v_own = _pad_rows(vn_ref[...], PAGE_SIZE).astype(BF16)
    s_past = jnp.dot(q2, k_buf[slot].astype(BF16), preferred_element_type=F32)
    s_own = lax.dot_general(q2, k_own, NT_DIMS, preferred_element_type=F32)
    far = strip_ref[:, 0].reshape(rows, LANES)[:, 0:1]
    near = [strip_ref[:, t].reshape(rows, LANES) for t in (1, 2)]
    s = jnp.concatenate([s_past[:, :past - PAGE_SIZE] + far, s_past[:, past - PAGE_SIZE:] + near[0],
                         s_own + near[1]], axis=1)

    qpos = past + lax.broadcasted_iota(I32, (ts, 1), 0)
    scores = score_ref[...]
    valid = _valid_mask(scores, lax.broadcasted_iota(I32, scores.shape, 1), thr_ref[:, 0:1],
                        jst_ref[:, 0:1], qpos)
    s = jnp.where(valid[None], s.reshape(N_HEADS, ts, past + PAGE_SIZE), NEG).reshape(rows, -1)
    m = jnp.max(s, axis=1, keepdims=True)
    p = jnp.exp(s - m)
    l = jnp.sum(p, axis=1, keepdims=True)
    pb = p.astype(BF16)
    pv = lax.dot_general(pb[:, :past], v_buf[slot].astype(BF16), NT_DIMS, preferred_element_type=F32)
    pv = pv + jnp.dot(pb[:, past:], v_own, preferred_element_type=F32)
    half = GROUP * ts
    carries = [(None, l[n * half:(n + 1) * half], pv[n * half:(n + 1) * half]) for n in range(N_KV_HEADS)]
    _write_attn(out_ref, carries, ts)


def _dsa_sample(page_table, rel_bias, qa, qi, misc, ki_new, k_new, v_new, ckit, ckt, cvt, DB, ts):
    n_pages = page_table.shape[1]
    past = n_pages * PAGE_SIZE
    n_sel = min(TOPK_MAX, (past + ts) // 4)
    lpad = past + PAGE_SIZE
    idx_bits = max(1, (lpad - 1).bit_length())
    ck = LANES * math.gcd(lpad // LANES, 5)
    assert ts % SUBLANES == 0 and ts <= PAGE_SIZE and n_pages >= 1
    blk = lambda w: pl.BlockSpec((ts, w), lambda b, pt: (b, 0))
    whole = lambda w: pl.BlockSpec((DB * ts, w), lambda b, pt: (0, 0))
    hbm = pl.BlockSpec(memory_space=pl.ANY)
    keys, thr, jstar = pl.pallas_call(
        functools.partial(_sample_score_kernel, n_sel, idx_bits, n_pages, ts, ck),
        grid_spec=pltpu.PrefetchScalarGridSpec(
            num_scalar_prefetch=1,
            grid=(DB,),
            in_specs=[blk(N_IDX_HEADS * IDX_DIM), blk(LANES), blk(IDX_DIM), hbm],
            out_specs=[whole(lpad), whole(LANES), whole(LANES)],
            scratch_shapes=[pltpu.VMEM((2, IDX_DIM, past), F32), pltpu.SemaphoreType.DMA((2, 1))]),
        out_shape=[jax.ShapeDtypeStruct((DB * ts, lpad), F32),
                   jax.ShapeDtypeStruct((DB * ts, LANES), F32),
                   jax.ShapeDtypeStruct((DB * ts, LANES), I32)],
        compiler_params=_cparams(1),
        name="sample_score",
    )(page_table, qi, misc, ki_new, ckit)
    return pl.pallas_call(
        functools.partial(_sample_attend_kernel, n_pages, ts),
        grid_spec=pltpu.PrefetchScalarGridSpec(
            num_scalar_prefetch=1,
            grid=(DB,),
            in_specs=[pl.BlockSpec(memory_space=pltpu.SMEM), blk(ATT_WIDTH), blk(lpad), blk(LANES),
                      blk(LANES), blk(LANES), blk(LANES), hbm, hbm],
            out_specs=blk(ATT_WIDTH),
            scratch_shapes=[pltpu.VMEM((2, LANES, past), F32), pltpu.VMEM((2, LANES, past), F32),
                            pltpu.SemaphoreType.DMA((2, 2)),
                            pltpu.VMEM((N_HEADS, 3, ts, LANES), F32)]),
        out_shape=jax.ShapeDtypeStruct((DB * ts, ATT_WIDTH), F32),
        compiler_params=_cparams(1),
        name="sample_attend",
    )(page_table, rel_bias, qa, keys, thr, jstar, k_new, v_new, ckt, cvt)


def _pack_layer_weights(w_in, b_i, b_f, w_out, w_up, w_down):
    D = w_in.shape[0]
    sizes = (ATT_WIDTH, N_KV_HEADS * HEAD_DIM, N_KV_HEADS * HEAD_DIM, N_IDX_HEADS * IDX_DIM, IDX_DIM,
             N_IDX_HEADS, M_WIDTH, M_WIDTH, M_WIDTH, M_WIDTH, M_HEADS, M_HEADS)
    assert w_in.shape[1] == sum(sizes)
    pts = np.cumsum((0,) + sizes)
    seg = [w_in[:, pts[i]:pts[i + 1]] for i in range(len(sizes))]
    qa, k, v, qi, ki, wi, qm, km, vm, om, im, fm = seg
    perm = np.asarray(HEAD_PERM)
    qa = qa.reshape(D, N_HEADS, HEAD_DIM)[:, perm].reshape(D, ATT_WIDTH)
    misc = jnp.concatenate([wi, im, fm, jnp.zeros((D, LANES - N_IDX_HEADS - 2 * M_HEADS), w_in.dtype)], axis=1)
    wp = jnp.concatenate([qa, k, v, qi, ki, ki, misc, qm, km, vm, om], axis=1).astype(BF16)
    assert wp.shape[1] == N_PACK
    gate_bias = jnp.broadcast_to(jnp.concatenate([b_i, b_f]).astype(F32)[:, None], (2 * M_HEADS, LANES))
    woa = w_out[:ATT_WIDTH].reshape(N_HEADS, HEAD_DIM, -1)[perm].reshape(ATT_WIDTH, -1).astype(BF16)
    woh = w_out[ATT_WIDTH:].astype(BF16)
    wt = jnp.concatenate([seg[1].T, seg[2].T, ki.T, jnp.zeros((LANES - IDX_DIM, D), w_in.dtype),
                          seg[0].T, qi.T, misc.T], axis=0).astype(BF16)
    assert wt.shape[0] == N_TPACK
    return wp, wt, gate_bias, woa, woh, w_up.astype(BF16), w_down.astype(BF16)


def _layer(x, packed, g1, g2, mnorm, rel_bias, gf, final_norm, past):
    wp, wt, gate_bias, woa, woh, wup, wdn = packed
    B, T, D = x.shape
    x2 = x.reshape(B * T, D)
    tm = math.gcd(T if past is None else B * T, 512)
    kv_w = N_KV_HEADS * HEAD_DIM
    if past is None:
        (misc, qm, km, vm, om, kb, kib, kt, vt, kit, vtb, qat, qit, misct) = _inproj(
            x2, g1.reshape(1, D), wp, wt, B, T, tm)
        attn = _dsa_prompt(rel_bias, qat, qit, misct, kib, kb, vtb, B, T)
        state = None
        k_new = kt.reshape(B, N_KV_HEADS, HEAD_DIM, T).transpose(0, 3, 1, 2)
        v_new = vt.reshape(B, N_KV_HEADS, HEAD_DIM, T).transpose(0, 3, 1, 2)
        ki_new = kit.transpose(0, 2, 1)
    else:
        (misc, qm, km, vm, om, qa, qi, k, v, ki) = _inproj(x2, g1.reshape(1, D), wp, None, B, T, tm)
        page_table, cache_k, cache_v, cache_kidx, c0, n0, m0 = past
        n_pool = cache_k.shape[0]
        ckt = cache_k.transpose(0, 2, 3, 1).reshape(n_pool, kv_w, PAGE_SIZE)
        cvt = cache_v.transpose(0, 2, 3, 1).reshape(n_pool, kv_w, PAGE_SIZE)
        ckit = cache_kidx.transpose(0, 2, 1)
        attn = _dsa_sample(page_table, rel_bias, qa, qi, misc, ki, k, v, ckit, ckt, cvt, B, T)
        state = (c0, n0, jnp.broadcast_to(m0[..., None], m0.shape + (LANES,)))
        k_new = k.reshape(B, T, N_KV_HEADS, HEAD_DIM)
        v_new = v.reshape(B, T, N_KV_HEADS, HEAD_DIM)
        ki_new = ki.reshape(B, T, IDX_DIM)
    h, c_new, n_new, m_new = _mlstm(qm, km, vm, om, misc, gate_bias, mnorm.reshape(1, M_WIDTH), state,
                                    B, T, BF16 if past is None else F32)
    y = _post(x2, attn, h, woa, woh, g2.reshape(1, D), wup, wdn, gf.reshape(1, D), final_norm, tm)
    return (y.reshape(B, T, D), k_new, v_new, ki_new, c_new, n_new, m_new[..., 0])


def kernel(x_prompt, x_sample, cache_k, cache_v, cache_kidx, page_table, state_C, state_n, state_m,
           w_in, b_igate, b_fgate, mlstm_norm, rel_bias, w_out, norm1, norm2, w_up, w_down, norm_f):
    depth = w_in.shape[0]
    xp, xs = x_prompt, x_sample
    outs_p, outs_s = [], []
    for l in range(depth):
        packed = _pack_layer_weights(w_in[l], b_igate[l], b_fgate[l], w_out[l], w_up[l], w_down[l])
        last = l == depth - 1
        common = (packed, norm1[l], norm2[l], mlstm_norm[l], rel_bias, norm_f, last)
        rp = _layer(xp, *common, None)
        rs = _layer(xs, *common, (page_table, cache_k[l], cache_v[l], cache_kidx[l],
                                  state_C[l], state_n[l], state_m[l]))
        xp, xs = rp[0], rs[0]
        outs_p.append(rp[1:])
        outs_s.append(rs[1:])
    stack = lambda outs, i: jnp.stack([o[i] for o in outs])
    return ((xp, xs) + tuple(stack(outs_p, i) for i in range(6))
            + tuple(stack(outs_s, i) for i in range(6)))
```

```python
import functools
import math

import numpy as np
import jax
import jax.numpy as jnp
from jax import lax
from jax.experimental import pallas as pl
from jax.experimental.pallas import tpu as pltpu

F32 = jnp.float32
BF16 = jnp.bfloat16
I32 = jnp.int32

N_HEADS = 8
HEAD_DIM = 64
N_KV_HEADS = 2
GROUP = N_HEADS // N_KV_HEADS
N_IDX_HEADS = 8
IDX_DIM = 64
TOPK_MAX = 256
N_BUCKETS = 32
MAX_DISTANCE = 128
M_HEADS = 4
M_HEAD_DIM = 128
PAGE_SIZE = 128
EPS = 1e-6
NEG = -1e30
LOG2E = math.log2(math.e)
ATT_WIDTH = N_HEADS * HEAD_DIM
M_WIDTH = M_HEADS * M_HEAD_DIM

LANES = 128
SUBLANES = 8
VMEM_LIMIT = 56 * 1024 * 1024

C_QA = 0
C_K = C_QA + ATT_WIDTH
C_V = C_K + LANES
C_QI = C_V + LANES
C_KI2 = C_QI + N_IDX_HEADS * IDX_DIM
C_MISC = C_KI2 + LANES
C_QM = C_MISC + LANES
C_KM = C_QM + M_WIDTH
C_VM = C_KM + M_WIDTH
C_OM = C_VM + M_WIDTH
N_PACK = C_OM + M_WIDTH
MISC_W = 0
MISC_I = 8
MISC_F = 12

HEAD_PERM = (0, 4, 1, 5, 2, 6, 3, 7)

TQ = 256
CKP = 256
ML = 256

NT_DIMS = (((1,), (1,)), ((), ()))
TN_DIMS = (((0,), (0,)), ((), ()))


def _bucket_bounds():
    max_exact = N_BUCKETS // 2
    scale = (N_BUCKETS - max_exact) / math.log(MAX_DISTANCE / max_exact)

    def bucket(n, dt):
        if n < max_exact:
            return n
        val = np.log(np.asarray(max(n, 1), dt) / dt(max_exact)) * dt(scale)
        return min(max_exact + int(val), N_BUCKETS - 1)

    table = [bucket(n, np.float32) for n in range(MAX_DISTANCE + 2)]
    assert table == [bucket(n, np.float64) for n in range(MAX_DISTANCE + 2)]
    assert table[MAX_DISTANCE] == N_BUCKETS - 1
    return [next(d for d, b in enumerate(table) if b >= k) for k in range(N_BUCKETS)]


BUCKET_BOUNDS = _bucket_bounds()


def _cparams(n_axes):
    return pltpu.CompilerParams(dimension_semantics=("arbitrary",) * n_axes,
                                vmem_limit_bytes=VMEM_LIMIT)


def _const_spec(shape):
    nd = len(shape)
    return pl.BlockSpec(shape, lambda *_: (0,) * nd, pipeline_mode=pl.Buffered(1))


def _rms(x, g):
    return x * lax.rsqrt(jnp.mean(x * x, axis=-1, keepdims=True) + EPS) * g


def _inproj_mlstm(mm, misc_ref, qm_ref, km_ref, vm_ref, om_ref):
    misc_ref[...] = mm(C_MISC, LANES)
    qm_ref[...] = mm(C_QM, M_WIDTH).astype(qm_ref.dtype)
    km_ref[...] = (mm(C_KM, M_WIDTH) * (M_HEAD_DIM ** -0.5)).astype(km_ref.dtype)
    vm_ref[...] = mm(C_VM, M_WIDTH).astype(vm_ref.dtype)
    om_ref[...] = mm(C_OM, M_WIDTH)


def _inproj_rows_kernel(x_ref, g_ref, w_ref, misc_ref, qm_ref, km_ref, vm_ref, om_ref,
                        qa_ref, qi_ref, k_ref, v_ref, ki_ref):
    ub = _rms(x_ref[...], g_ref[...]).astype(BF16)
    mm = lambda c0, n: jnp.dot(ub, w_ref[:, c0:c0 + n], preferred_element_type=F32)
    _inproj_mlstm(mm, misc_ref, qm_ref, km_ref, vm_ref, om_ref)
    qa_ref[...] = mm(C_QA, ATT_WIDTH) * (HEAD_DIM ** -0.5)
    qi_ref[...] = mm(C_QI, N_IDX_HEADS * IDX_DIM)
    k_ref[...] = mm(C_K, LANES)
    v_ref[...] = mm(C_V, LANES)
    ki_ref[...] = mm(C_KI2, LANES)[:, :IDX_DIM]


R_K = 0
R_V = R_K + LANES
R_KI = R_V + LANES
R_QA = R_KI + LANES
R_QI = R_QA + ATT_WIDTH
R_MISC = R_QI + N_IDX_HEADS * IDX_DIM
N_TPACK = R_MISC + LANES


def _inproj_cols_kernel(x_ref, g_ref, w_ref, wt_ref, misc_ref, qm_ref, km_ref, vm_ref, om_ref,
                        kb_ref, kib_ref, kt_ref, vt_ref, kit_ref, vtb_ref, qat_ref, qit_ref,
                        misct_ref):
    ub = _rms(x_ref[...], g_ref[...]).astype(BF16)
    mm = lambda c0, n: jnp.dot(ub, w_ref[:, c0:c0 + n], preferred_element_type=F32)
    _inproj_mlstm(mm, misc_ref, qm_ref, km_ref, vm_ref, om_ref)
    kb_ref[...] = mm(C_K, LANES).astype(BF16)
    kib_ref[...] = mm(C_KI2, LANES)[:, :IDX_DIM].astype(BF16)

    def mt(r0, n):
        return lax.dot_general(wt_ref[r0:r0 + n, :], ub, NT_DIMS, preferred_element_type=F32)

    kt_ref[0] = mt(R_K, LANES)
    vt = mt(R_V, LANES)
    vt_ref[0] = vt
    for j in range(vtb_ref.shape[1]):
        vtb_ref[0, j] = vt[:, j * CKP:(j + 1) * CKP].astype(BF16)
    kit_ref[0] = mt(R_KI, IDX_DIM)
    qat_ref[0] = (mt(R_QA, ATT_WIDTH) * (HEAD_DIM ** -0.5 * LOG2E)).astype(BF16)
    qit_ref[0] = mt(R_QI, N_IDX_HEADS * IDX_DIM).astype(BF16)
    misct_ref[0] = mt(R_MISC, LANES)


def _inproj(x2, g1, wp, wt, B, T, tm):
    R, D = x2.shape
    assert R == B * T and R % tm == 0
    mdt = F32 if wt is None else BF16
    row = lambda i: (i, 0)
    outs = [(LANES, F32), (M_WIDTH, mdt), (M_WIDTH, mdt), (M_WIDTH, mdt), (M_WIDTH, F32)]
    in_specs = [pl.BlockSpec((tm, D), row), _const_spec((1, D)), _const_spec((D, N_PACK))]
    if wt is None:
        kern, args = _inproj_rows_kernel, (x2, g1, wp)
        outs += [(ATT_WIDTH, F32), (N_IDX_HEADS * IDX_DIM, F32), (LANES, F32), (LANES, F32), (IDX_DIM, F32)]
    else:
        kern, args = _inproj_cols_kernel, (x2, g1, wp, wt)
        in_specs.append(_const_spec(wt.shape))
        outs += [(LANES, BF16), (IDX_DIM, BF16)]
    out_specs = [pl.BlockSpec((tm, w), row) for w, _ in outs]
    out_shape = [jax.ShapeDtypeStruct((R, w), dt) for w, dt in outs]
    if wt is not None:
        assert T % tm == 0 and tm % CKP == 0
        tpb, cpt = T // tm, tm // CKP
        cols = lambda i: (i // tpb, 0, i % tpb)
        for w, dt in ((LANES, F32), (LANES, F32), (IDX_DIM, F32)):
            out_specs.append(pl.BlockSpec((1, w, tm), cols))
            out_shape.append(jax.ShapeDtypeStruct((B, w, T), dt))
        out_specs.append(pl.BlockSpec((1, cpt, LANES, CKP), lambda i: (i // tpb, i % tpb, 0, 0)))
        out_shape.append(jax.ShapeDtypeStruct((B, T // CKP, LANES, CKP), BF16))
        for w, dt in ((ATT_WIDTH, BF16), (N_IDX_HEADS * IDX_DIM, BF16), (LANES, F32)):
            out_specs.append(pl.BlockSpec((1, w, tm), cols))
            out_shape.append(jax.ShapeDtypeStruct((B, w, T), dt))
    return pl.pallas_call(
        kern,
        grid=(R // tm,),
        in_specs=in_specs,
        out_specs=out_specs,
        out_shape=out_shape,
        compiler_params=_cparams(1),
        name="inproj",
    )(*args)


def _post_kernel(ff_chunk, final_norm, x_ref, a_ref, h_ref, woa_ref, woh_ref, g2_ref, wup_ref,
                 wdn_ref, gf_ref, y_ref):
    mix = jnp.dot(a_ref[...].astype(BF16), woa_ref[...], preferred_element_type=F32)
    mix = mix + jnp.dot(h_ref[...].astype(BF16), woh_ref[...], preferred_element_type=F32)
    hres = x_ref[...] + mix
    f = _rms(hres, g2_ref[...]).astype(BF16)
    acc = hres
    for c0 in range(0, wup_ref.shape[1], ff_chunk):
        up = jnp.dot(f, wup_ref[:, c0:c0 + ff_chunk], preferred_element_type=F32)
        r = jnp.maximum(up, 0.0)
        acc = acc + jnp.dot((r * r).astype(BF16), wdn_ref[c0:c0 + ff_chunk, :],
                            preferred_element_type=F32)
    y_ref[...] = _rms(acc, gf_ref[...]) if final_norm else acc


def _post(x2, attn, h, woa, woh, g2, wup, wdn, gf, final_norm, tm):
    R, D = x2.shape
    dff = wup.shape[1]
    assert R % tm == 0
    row = lambda i: (i, 0)
    return pl.pallas_call(
        functools.partial(_post_kernel, min(dff, 1024), final_norm),
        grid=(R // tm,),
        in_specs=[pl.BlockSpec((tm, D), row), pl.BlockSpec((tm, ATT_WIDTH), row),
                  pl.BlockSpec((tm, M_WIDTH), row), _const_spec(woa.shape), _const_spec(woh.shape),
                  _const_spec((1, D)), _const_spec(wup.shape), _const_spec(wdn.shape),
                  _const_spec((1, D))],
        out_specs=pl.BlockSpec((tm, D), row),
        out_shape=jax.ShapeDtypeStruct((R, D), F32),
        compiler_params=_cparams(1),
        name="post",
    )(x2, attn, h, woa, woh, g2, wup, wdn, gf)


def _log_sigmoid(x):
    return -(jnp.maximum(-x, 0.0) + jnp.log1p(jnp.exp(-jnp.abs(x))))


def _mlstm_kernel(nvalid, has_state, *refs):
    if has_state:
        (q_ref, k_ref, v_ref, o_ref, misc_ref, gb_ref, mn_ref, c0_ref, n0_ref, m0_ref,
         h_ref, c_ref, n_ref, m_ref, z_scr, rows_scr) = refs
    else:
        (q_ref, k_ref, v_ref, o_ref, misc_ref, gb_ref, mn_ref,
         h_ref, c_ref, n_ref, m_ref, z_scr, rows_scr) = refs
    nb, nc, L = z_scr.shape[0], z_scr.shape[1], z_scr.shape[2]
    c = pl.program_id(1)

    def padded(x, dt):
        x = x.astype(dt)
        if nvalid == L:
            return x
        return jnp.concatenate([x, jnp.zeros((L - nvalid, x.shape[1]), dt)], axis=0)

    r2 = lax.broadcasted_iota(I32, (L, L), 0)
    c2 = lax.broadcasted_iota(I32, (L, L), 1)
    tril = r2 >= c2

    @pl.when(c == 0)
    def _():
        if has_state:
            c_ref[...] = c0_ref[...]
            n_ref[...] = n0_ref[...]
            m_ref[...] = m0_ref[...]
        else:
            c_ref[...] = jnp.zeros_like(c_ref)
            n_ref[...] = jnp.zeros_like(n_ref)
            m_ref[...] = jnp.zeros_like(m_ref)
        assert MISC_F == MISC_I + M_HEADS and MISC_I % SUBLANES == 0
        row8 = lax.broadcasted_iota(I32, (SUBLANES, L), 0)
        tok8 = lax.broadcasted_iota(I32, (SUBLANES, L), 1)
        bias8 = jnp.concatenate([gb_ref[...]] * (L // LANES), axis=1)
        triu = (r2 <= c2).astype(F32)
        for s in range(nb):
            for cc in range(nc):
                misc_t = padded(misc_ref[s, cc * nvalid:(cc + 1) * nvalid, :], F32).T
                gx = misc_t[MISC_I:MISC_I + SUBLANES] + bias8
                gates = jnp.where(row8 >= M_HEADS, _log_sigmoid(gx), gx)
                if nvalid != L:
                    gates = jnp.where(tok8 < nvalid, gates, jnp.where(row8 >= M_HEADS, 0.0, NEG))
                cum = jnp.dot(gates, triu, precision=lax.Precision.HIGHEST,
                              preferred_element_type=F32)
                rows = jnp.where(row8 >= M_HEADS, cum, gates)
                rows_scr[s, cc] = rows
                z_scr[s, cc] = jnp.concatenate([rows, jnp.zeros((LANES - SUBLANES, L), F32)], axis=0).T

    loaded = []
    for s in range(nb):
        loaded.append((z_scr[s, c], rows_scr[s, c], padded(q_ref[s], BF16), padded(k_ref[s], BF16),
                       padded(v_ref[s], BF16), padded(o_ref[s], F32), m_ref[s], c_ref[s], n_ref[s]))
    chains = [(s, hd) for s in range(nb) for hd in range(M_HEADS)]

    def operands(s, hd):
        z, rows, qb, kb, vb, ob, m_all, c_all, n_all = loaded[s]
        sl = slice(hd * M_HEAD_DIM, (hd + 1) * M_HEAD_DIM)
        return dict(
            sl=sl, q=qb[:, sl], k=kb[:, sl], v=vb[:, sl], o=ob[:, sl],
            icol=z[:, hd:hd + 1], bcol=z[:, M_HEADS + hd:M_HEADS + hd + 1],
            irow=rows[hd:hd + 1, :], brow=rows[M_HEADS + hd:M_HEADS + hd + 1, :],
            m_prev=m_all[hd:hd + 1, 0:1], s_prev=c_all[hd], n_prev=n_all[hd:hd + 1, :])

    st = [operands(s, hd) for s, hd in chains]
    for x in st:
        x["qk"] = lax.dot_general(x["q"], x["k"], NT_DIMS, preferred_element_type=F32)
        x["qs"] = jnp.dot(x["q"], x["s_prev"].astype(BF16), preferred_element_type=F32)
    for x in st:
        g = x["bcol"] + x["m_prev"]
        dm = jnp.where(tril, x["bcol"] - x["brow"] + x["irow"], NEG)
        x["mt"] = jnp.maximum(g, jnp.max(dm, axis=1, keepdims=True))
        x["gw"] = jnp.exp(g - x["mt"])
        x["qk"] = x["qk"] * jnp.exp(dm - x["mt"])
        b_last = x["bcol"][L - 1:L, :]
        g_last = b_last + x["m_prev"]
        a = b_last - x["bcol"] + x["icol"]
        x["m_new"] = jnp.maximum(g_last, jnp.max(a, axis=0, keepdims=True))
        x["sw"] = jnp.exp(g_last - x["m_new"])
        x["ak"] = jnp.exp(a - x["m_new"]) * x["k"].astype(F32)
    for x in st:
        x["pv"] = jnp.dot(x["qk"].astype(BF16), x["v"], preferred_element_type=F32)
        x["kv"] = lax.dot_general(x["ak"].astype(BF16), x["v"], TN_DIMS, preferred_element_type=F32)
    stores = []
    for (s, hd), x in zip(chains, st):
        num = x["gw"] * x["qs"] + x["pv"]
        den = x["gw"] * jnp.sum(x["q"].astype(F32) * x["n_prev"], axis=1, keepdims=True)
        den = den + jnp.sum(x["qk"], axis=1, keepdims=True)
        hh = num / jnp.maximum(jnp.abs(den), jnp.exp(-x["mt"]))
        hh = hh * lax.rsqrt(jnp.mean(hh * hh, axis=-1, keepdims=True) + EPS)
        hh = hh * mn_ref[:, x["sl"]] * jax.nn.sigmoid(x["o"])
        c_new = x["sw"] * x["s_prev"] + x["kv"]
        n_new = x["sw"] * x["n_prev"] + jnp.sum(x["ak"], axis=0, keepdims=True)
        stores.append((s, hd, x["sl"], hh[:nvalid].astype(h_ref.dtype), c_new, n_new,
                       jnp.broadcast_to(x["m_new"], (1, LANES))))
    for s, hd, sl, h_new, c_new, n_new, m_new in stores:
        h_ref[s, :, sl] = h_new
        c_ref[s, hd] = c_new
        n_ref[s, hd:hd + 1, :] = n_new
        m_ref[s, hd:hd + 1, :] = m_new


def _mlstm(qm, km, vm, om, misc, gate_bias, mnorm, state, B, T, hdt):
    L = next((c for c in (ML, LANES) if T % c == 0), LANES)
    nvalid = L if T % L == 0 else T
    assert nvalid <= L and T % nvalid == 0 and nvalid % SUBLANES == 0
    nc = T // nvalid
    nb = next(n for n in ((2, 1) if nvalid == L else (4, 2, 1)) if B % n == 0)
    seq3 = lambda a: a.reshape(B, T, a.shape[-1])
    blk = lambda w: pl.BlockSpec((nb, nvalid, w), lambda b, c: (b, c, 0))
    st_specs = [pl.BlockSpec((nb, M_HEADS, M_HEAD_DIM, M_HEAD_DIM), lambda b, c: (b, 0, 0, 0)),
                pl.BlockSpec((nb, M_HEADS, M_HEAD_DIM), lambda b, c: (b, 0, 0)),
                pl.BlockSpec((nb, M_HEADS, LANES), lambda b, c: (b, 0, 0))]
    in_specs = [blk(M_WIDTH), blk(M_WIDTH), blk(M_WIDTH), blk(M_WIDTH),
                pl.BlockSpec((nb, T, LANES), lambda b, c: (b, 0, 0)),
                pl.BlockSpec((SUBLANES, LANES), lambda b, c: (0, 0)),
                pl.BlockSpec((1, M_WIDTH), lambda b, c: (0, 0))]
    args = [seq3(qm), seq3(km), seq3(vm), seq3(om), seq3(misc), gate_bias, mnorm]
    if state is not None:
        in_specs += st_specs
        args += list(state)
    h, c_new, n_new, m_new = pl.pallas_call(
        functools.partial(_mlstm_kernel, nvalid, state is not None),
        grid=(B // nb, nc),
        in_specs=in_specs,
        out_specs=[blk(M_WIDTH)] + st_specs,
        out_shape=[jax.ShapeDtypeStruct((B, T, M_WIDTH), hdt),
                   jax.ShapeDtypeStruct((B, M_HEADS, M_HEAD_DIM, M_HEAD_DIM), F32),
                   jax.ShapeDtypeStruct((B, M_HEADS, M_HEAD_DIM), F32),
                   jax.ShapeDtypeStruct((B, M_HEADS, LANES), F32)],
        scratch_shapes=[pltpu.VMEM((nb, nc, L, LANES), F32), pltpu.VMEM((nb, nc, SUBLANES, L), F32)],
        compiler_params=_cparams(2),
        name="mlstm",
    )(*args)
    return h.reshape(B * T, M_WIDTH), c_new, n_new, m_new


def _sortable_key(score):
    bits = lax.bitcast_convert_type(score, I32)
    return bits ^ (lax.shift_right_arithmetic(bits, 31) & 0x7FFFFFFF)


def _key_to_score(key):
    return lax.bitcast_convert_type(key ^ (lax.shift_right_arithmetic(key, 31) & 0x7FFFFFFF), F32)


def _build_bias_strip(strip_ref, rb_ref, off, key_axis, log2_relative=False):
    ntiles, tile = strip_ref.shape[1], strip_ref.shape[2:]
    i = lax.broadcasted_iota(I32, tile, 1 - key_axis)
    x = lax.broadcasted_iota(I32, tile, key_axis)

    def entry(b, h):
        if log2_relative:
            return (rb_ref[b, h] - rb_ref[N_BUCKETS - 1, h]) * LOG2E
        return rb_ref[b, h]

    for t in range(ntiles):
        dist = i + (off - LANES * t) - x
        for h in range(N_HEADS):
            val = jnp.full(tile, entry(0, h), F32)
            for b in range(1, N_BUCKETS):
                val = jnp.where(dist >= BUCKET_BOUNDS[b], entry(b, h), val)
            strip_ref[h, t] = val


def _counter(score_ref, nk, ck, key_axis):
    def count(pred):
        def body(c, acc):
            c0 = pl.multiple_of(c * ck, ck)
            sc = score_ref[:, pl.ds(c0, ck)] if key_axis == 1 else score_ref[pl.ds(c0, ck), :]
            idx = c0 + lax.broadcasted_iota(I32, sc.shape, key_axis)
            hit = jnp.where(pred(sc, idx), 1, 0)
            if key_axis == 1:
                for j in range(ck // LANES):
                    acc = acc + hit[:, j * LANES:(j + 1) * LANES]
                return acc
            return acc + jnp.sum(hit.reshape(ck // SUBLANES, SUBLANES, hit.shape[1]), axis=0)

        nq = score_ref.shape[1 - key_axis]
        acc0 = jnp.zeros((nq, LANES) if key_axis == 1 else (SUBLANES, nq), I32)
        return jnp.sum(lax.fori_loop(0, nk, body, acc0), axis=key_axis, keepdims=True)

    return count


IMIN = jnp.iinfo(jnp.int32).min


def _kth_largest_by_count(count, qshape, n_sel):
    def bit_step(i, key):
        cand = key + lax.shift_left(jnp.int32(1), 31 - i)
        cand_score = _key_to_score(cand)
        return jnp.where(count(lambda sc, idx: sc >= cand_score) >= n_sel, cand, key)

    return _key_to_score(lax.fori_loop(0, 32, bit_step, jnp.full(qshape, IMIN, I32)))


def _bit_planes(words):
    a = list(words)
    j, m = 16, 0x0000FFFF
    while j:
        k = 0
        while k < 32:
            t = (a[k] ^ lax.shift_right_logical(a[k + j], j)) & m
            a[k] = a[k] ^ t
            a[k + j] = a[k + j] ^ lax.shift_left(t, j)
            k = (k + j + 1) & ~j
        j >>= 1
        m = (m ^ (m << j)) & 0xFFFFFFFF
        m = m - (1 << 32) if m >= (1 << 31) else m
    return a[::-1]


def _kth_largest_by_planes(score_ref, planes_ref, nk, ck, n_sel):
    nc = planes_ref.shape[1]
    nq = score_ref.shape[1]
    assert ck == 32 * SUBLANES

    def pack_chunk(c, _):
        c0 = pl.multiple_of(c * ck, ck)
        u = _sortable_key(score_ref[pl.ds(c0, ck), :]) ^ IMIN
        u = u.reshape(32, SUBLANES, nq)
        for b, plane in enumerate(_bit_planes([u[v] for v in range(32)])):
            planes_ref[b, c] = plane
        return 0

    lax.fori_loop(0, nk, pack_chunk, 0)

    def bit_step(i, carry):
        cand, n_above, thr_u = carry
        b = 31 - i
        ones = [cand[c] & planes_ref[b, c] for c in range(nc)]
        pop = lax.population_count(ones[0])
        for c in range(1, nc):
            pop = pop + lax.population_count(ones[c])
        tot = jnp.sum(pop, axis=0, keepdims=True)
        take = n_above + tot >= n_sel
        cand = tuple(jnp.where(take, o, m ^ o) for o, m in zip(ones, cand))
        n_above = jnp.where(take, n_above, n_above + tot)
        thr_u = thr_u | jnp.where(take, lax.shift_left(jnp.int32(1), b), 0)
        return cand, n_above, thr_u

    cand0 = tuple(jnp.where(c < nk, jnp.full((SUBLANES, nq), -1, I32), 0) for c in range(nc))
    zero = jnp.zeros((1, nq), I32)
    _, _, thr_u = lax.fori_loop(0, 32, bit_step, (cand0, zero, zero))
    return _key_to_score(thr_u ^ IMIN)


def _select_topk(thr, check, count, qshape, n_sel, idx_bits):
    imax = jnp.iinfo(jnp.int32).max

    def with_counts(t):
        return t, count(lambda sc, idx: sc > t), count(lambda sc, idx: sc >= t)

    thr, n_gt, n_ge = with_counts(thr)
    if check:
        good = jnp.min(jnp.where(n_gt < n_sel, jnp.where(n_ge >= n_sel, 1, 0), 0)) > 0
        thr, n_gt, n_ge = lax.cond(
            good, lambda _: (thr, n_gt, n_ge),
            lambda _: with_counts(_kth_largest_by_count(count, qshape, n_sel)), 0)
    need = n_sel - n_gt

    def tie_search(_):
        def idx_step(i, lo):
            cand = lo + lax.shift_left(jnp.int32(1), idx_bits - 1 - i)
            cnt = count(lambda sc, idx: jnp.where(sc == thr, idx, imax) < cand)
            return jnp.where(cnt < need, cand, lo)

        return lax.fori_loop(0, idx_bits, idx_step, jnp.zeros(qshape, I32))

    all_ties = jnp.full(qshape, imax, I32)
    jstar = lax.cond(jnp.max(n_ge) > n_sel, tie_search, lambda _: all_ties, 0)
    jstar = jnp.where(n_ge > n_sel, jstar, all_ties)
    return thr, jstar


def _valid_mask(scores, idx, thr, jstar, qpos):
    sel = jnp.where(scores > thr, 1, jnp.where(scores == thr, jnp.where(idx <= jstar, 1, 0), 0))
    return jnp.where(idx <= qpos, sel, 0) > 0


def _group_queries(qa, tq):
    lane = lax.broadcasted_iota(I32, (tq, LANES), 1)
    out = []
    for n in range(N_KV_HEADS):
        keep = (lane < HEAD_DIM) if n == 0 else (lane >= HEAD_DIM)
        tiles = [jnp.where(keep, qa[:, j * LANES:(j + 1) * LANES], jnp.zeros((), qa.dtype))
                 for j in range(GROUP)]
        out.append(jnp.concatenate(tiles, axis=0).astype(BF16))
    return out


def _write_attn(out_ref, carries, tq):
    lane = lax.broadcasted_iota(I32, (tq, LANES), 1)
    res = [acc / l for (_, l, acc) in carries]
    for j in range(GROUP):
        tile = jnp.where(lane < HEAD_DIM, res[0][j * tq:(j + 1) * tq], res[1][j * tq:(j + 1) * tq])
        out_ref[:, j * LANES:(j + 1) * LANES] = tile.astype(out_ref.dtype)


def _dsa_prompt_kernel(n_sel, idx_bits, rb_ref, qat_ref, qit_ref, misct_ref, ki_ref, k_ref, vt_ref,
                       out_ref, score_ref, planes_ref, strip_ref):
    qb = pl.program_id(1)
    q0 = qb * TQ
    nk = (q0 + TQ + CKP - 1) // CKP
    tiles_per_chunk = CKP // LANES
    back_tiles = strip_ref.shape[1] - tiles_per_chunk
    strip_off = LANES * back_tiles

    @pl.when((pl.program_id(0) == 0) & (qb == 0))
    def _():
        _build_bias_strip(strip_ref, rb_ref, strip_off, 0, log2_relative=True)
        planes_ref[...] = jnp.zeros(planes_ref.shape, I32)

    qpos = q0 + lax.broadcasted_iota(I32, (1, TQ), 1)

    qit = qit_ref[0]
    qstack = jnp.concatenate([qit[h * IDX_DIM:(h + 1) * IDX_DIM] for h in range(N_IDX_HEADS)],
                             axis=1)
    w = misct_ref[0, MISC_W:MISC_W + N_IDX_HEADS, :] * (N_IDX_HEADS ** -0.5 * IDX_DIM ** -0.5)

    def score_chunk(c, _):
        c0 = pl.multiple_of(c * CKP, CKP)
        d = jnp.dot(ki_ref[pl.ds(c0, CKP), :], qstack, preferred_element_type=F32)
        d = jnp.maximum(d, 0.0)
        s = jnp.zeros((CKP, TQ), F32)
        for h in range(N_IDX_HEADS):
            s = s + d[:, h * TQ:(h + 1) * TQ] * w[h:h + 1, :]
        idx = c0 + lax.broadcasted_iota(I32, (CKP, TQ), 0)
        score_ref[pl.ds(c0, CKP), :] = jnp.where(idx <= qpos, s, NEG)
        return 0

    lax.fori_loop(0, nk, score_chunk, 0)

    thr = _kth_largest_by_planes(score_ref, planes_ref, nk, CKP, n_sel)
    thr, jstar = _select_topk(thr, True, _counter(score_ref, nk, CKP, 0), (1, TQ), n_sel, idx_bits)

    qat = qat_ref[0]
    zeros = jnp.zeros((HEAD_DIM, TQ), qat.dtype)
    qgroups = []
    for n in range(N_KV_HEADS):
        tiles = []
        for g in range(GROUP):
            h = n * GROUP + g
            x = qat[h * HEAD_DIM:(h + 1) * HEAD_DIM]
            tiles.append(jnp.concatenate([x, zeros] if n == 0 else [zeros, x], axis=0))
        qgroups.append(jnp.concatenate(tiles, axis=1))

    def attend_chunk(near, c, carries):
        c0 = pl.multiple_of(c * CKP, CKP)
        idx = c0 + lax.broadcasted_iota(I32, (CKP, TQ), 0)
        valid = _valid_mask(score_ref[pl.ds(c0, CKP), :], idx, thr, jstar, qpos)
        kc = k_ref[pl.ds(c0, CKP), :]
        vct = vt_ref[0, c]
        tiles = [jnp.maximum(back_tiles + j - (qb * (TQ // LANES) - c * tiles_per_chunk), 0)
                 for j in range(tiles_per_chunk)]
        out = []
        for n in range(N_KV_HEADS):
            m_old, l_old, acc = carries[n]
            s = jnp.dot(kc, qgroups[n], preferred_element_type=F32)
            parts = []
            for g in range(GROUP):
                sg = s[:, g * TQ:(g + 1) * TQ]
                if near:
                    sg = sg + jnp.concatenate([strip_ref[n * GROUP + g, t] for t in tiles], axis=0)
                parts.append(jnp.where(valid, sg, NEG))
            sm = jnp.concatenate(parts, axis=1)
            m_new = jnp.maximum(m_old, jnp.max(sm, axis=0, keepdims=True))
            alpha = jnp.exp2(m_old - m_new)
            p = jnp.exp2(sm - m_new)
            l_new = alpha * l_old + jnp.sum(p, axis=0, keepdims=True)
            acc = alpha * acc + jnp.dot(vct, p.astype(BF16), preferred_element_type=F32)
            out.append((m_new, l_new, acc))
        return tuple(out)

    far, near = functools.partial(attend_chunk, False), functools.partial(attend_chunk, True)
    init = (jnp.full((1, GROUP * TQ), NEG, F32), jnp.zeros((1, GROUP * TQ), F32),
            jnp.zeros((LANES, GROUP * TQ), F32))
    n_far = jnp.clip((qb * (TQ // LANES) - back_tiles + 1) // tiles_per_chunk, 0, nk)
    carries = lax.fori_loop(0, n_far // 2, lambda i, cr: far(2 * i + 1, far(2 * i, cr)), (init, init))
    carries = lax.fori_loop(2 * (n_far // 2), n_far, far, carries)
    carries = lax.fori_loop(n_far, nk, near, carries)
    res = [acc / l for (_, l, acc) in carries]
    row = lax.broadcasted_iota(I32, (LANES, TQ), 0)
    for j in range(GROUP):
        cols = slice(j * TQ, (j + 1) * TQ)
        tile_t = jnp.where(row < HEAD_DIM, res[0][:, cols], res[1][:, cols])
        out_ref[:, j * LANES:(j + 1) * LANES] = tile_t.T.astype(out_ref.dtype)


def _dsa_prompt(rel_bias, qat, qit, misct, kib, kb, vtb, B, T):
    assert T % CKP == 0 and T % TQ == 0
    nq = T // TQ
    n_sel = min(TOPK_MAX, T // 4)
    idx_bits = max(1, (T - 1).bit_length())
    qcols = lambda w: pl.BlockSpec((1, w, TQ), lambda b, q: (b, 0, q))
    seq = lambda w: pl.BlockSpec((T, w), lambda b, q: (b, 0))
    return pl.pallas_call(
        functools.partial(_dsa_prompt_kernel, n_sel, idx_bits),
        grid=(B, nq),
        in_specs=[pl.BlockSpec(memory_space=pltpu.SMEM), qcols(ATT_WIDTH), qcols(N_IDX_HEADS * IDX_DIM),
                  qcols(LANES), seq(IDX_DIM), seq(LANES),
                  pl.BlockSpec((1, T // CKP, LANES, CKP), lambda b, q: (b, 0, 0, 0))],
        out_specs=pl.BlockSpec((TQ, ATT_WIDTH), lambda b, q: (b * nq + q, 0)),
        out_shape=jax.ShapeDtypeStruct((B * T, ATT_WIDTH), BF16),
        scratch_shapes=[pltpu.VMEM((T, TQ), F32),
                        pltpu.VMEM((32, T // CKP, SUBLANES, TQ), I32),
                        pltpu.VMEM((N_HEADS, 2 + CKP // LANES, LANES, TQ), F32)],
        compiler_params=_cparams(2),
        name="dsa_prompt",
    )(rel_bias, qat, qit, misct, kib, kb, vtb)


def _page_pipeline(pt_ref, n_pages, caches, bufs, sems):
    def copies(bb, sl, j):
        pid = pt_ref[bb, j]
        cols = pl.ds(pl.multiple_of(j * PAGE_SIZE, PAGE_SIZE), PAGE_SIZE)
        return [pltpu.make_async_copy(c.at[pid], buf.at[sl, :, cols], sems.at[sl, i])
                for i, (c, buf) in enumerate(zip(caches, bufs))]

    def start_all(bb, sl):
        def body(j, _):
            for cp in copies(bb, sl, j):
                cp.start()
            return 0
        lax.fori_loop(0, n_pages, body, 0)

    def wait_all(bb, sl):
        def body(j, _):
            for cp in copies(bb, sl, j):
                cp.wait()
            return 0
        lax.fori_loop(0, n_pages, body, 0)

    def step():
        b = pl.program_id(0)
        slot = b % 2

        @pl.when(b == 0)
        def _():
            start_all(0, 0)

        @pl.when(b + 1 < pl.num_programs(0))
        def _():
            start_all(b + 1, 1 - slot)

        wait_all(b, slot)
        return slot

    return step


def _pad_rows(x, rows):
    return jnp.concatenate([x, jnp.zeros((rows - x.shape[0], x.shape[1]), x.dtype)], axis=0)


def _sample_score_kernel(n_sel, idx_bits, n_pages, ts, ck, pt_ref, qi_ref, misc_ref, kin_ref,
                         ckit_hbm, score_ref, thr_ref, jst_ref, ki_buf, sems):
    b = pl.program_id(0)
    past = n_pages * PAGE_SIZE
    slot = _page_pipeline(pt_ref, n_pages, [ckit_hbm], [ki_buf], sems)()

    qi = qi_ref[...]
    qstack = jnp.concatenate([qi[:, h * IDX_DIM:(h + 1) * IDX_DIM] for h in range(N_IDX_HEADS)],
                             axis=0).astype(BF16)
    w = misc_ref[:, MISC_W:MISC_W + N_IDX_HEADS] * (N_IDX_HEADS ** -0.5 * IDX_DIM ** -0.5)
    d_past = jnp.dot(qstack, ki_buf[slot].astype(BF16), preferred_element_type=F32)
    d_own = lax.dot_general(qstack, _pad_rows(kin_ref[...], PAGE_SIZE).astype(BF16), NT_DIMS,
                            preferred_element_type=F32)
    d = jnp.maximum(jnp.concatenate([d_past, d_own], axis=1), 0.0)
    s = jnp.zeros((ts, past + PAGE_SIZE), F32)
    for h in range(N_IDX_HEADS):
        s = s + d[h * ts:(h + 1) * ts] * w[:, h:h + 1]
    idx = lax.broadcasted_iota(I32, s.shape, 1)
    qpos = past + lax.broadcasted_iota(I32, (ts, 1), 0)
    score_ref[pl.ds(pl.multiple_of(b * ts, ts), ts), :] = jnp.where(idx <= qpos, s, NEG)

    @pl.when(b == pl.num_programs(0) - 1)
    def _():
        count = _counter(score_ref, score_ref.shape[1] // ck, ck, 1)
        qshape = (score_ref.shape[0], 1)
        thr = _kth_largest_by_count(count, qshape, n_sel)
        thr, jstar = _select_topk(thr, False, count, qshape, n_sel, idx_bits)
        thr_ref[...] = jnp.broadcast_to(thr, thr_ref.shape)
        jst_ref[...] = jnp.broadcast_to(jstar, jst_ref.shape)


def _sample_attend_kernel(n_pages, ts, pt_ref, rb_ref, qa_ref, score_ref, thr_ref, jst_ref, kn_ref,
                          vn_ref, ckt_hbm, cvt_hbm, out_ref, k_buf, v_buf, sems, strip_ref):
    b = pl.program_id(0)
    past = n_pages * PAGE_SIZE
    rows = N_HEADS * ts

    @pl.when(b == 0)
    def _():
        _build_bias_strip(strip_ref, rb_ref, LANES * (strip_ref.shape[1] - 1), 1)

    slot = _page_pipeline(pt_ref, n_pages, [ckt_hbm, cvt_hbm], [k_buf, v_buf], sems)()

    q2 = jnp.concatenate(_group_queries(qa_ref[...], ts), axis=0)
    k_own = _pad_rows(kn_ref[...], PAGE_SIZE).astype(BF16)
    v_own = _pad_rows(vn_ref[...], PAGE_SIZE).astype(BF16)
    s_past = jnp.dot(q2, k_buf[slot].astype(BF16), preferred_element_type=F32)
    s_own = lax.dot_general(q2, k_own, NT_DIMS, preferred_element_type=F32)
    far = strip_ref[:, 0].reshape(rows, LANES)[:, 0:1]
    near = [strip_ref[:, t].reshape(rows, LANES) for t in (1, 2)]
    s = jnp.concatenate([s_past[:, :past - PAGE_SIZE] + far, s_past[:, past - PAGE_SIZE:] + near[0],
                         s_own + near[1]], axis=1)

    qpos = past + lax.broadcasted_iota(I32, (ts, 1), 0)
    scores = score_ref[...]
    valid = _valid_mask(scores, lax.broadcasted_iota(I32, scores.shape, 1), thr_ref[:, 0:1],
                        jst_ref[:, 0:1], qpos)
    s = jnp.where(valid[None], s.reshape(N_HEADS, ts, past + PAGE_SIZE), NEG).reshape(rows, -1)
    m = jnp.max(s, axis=1, keepdims=True)
    p = jnp.exp(s - m)
    l = jnp.sum(p, axis=1, keepdims=True)
    pb = p.astype(BF16)
    pv = lax.dot_general(pb[:, :past], v_buf[slot].astype(BF16), NT_DIMS, preferred_element_type=F32)
    pv = pv + jnp.dot(pb[:, past:], v_own, preferred_element_type=F32)
    half = GROUP * ts
    carries = [(None, l[n * half:(n + 1) * half], pv[n * half:(n + 1) * half]) for n in range(N_KV_HEADS)]
    _write_attn(out_ref, carries, ts)


def _dsa_sample(page_table, rel_bias, qa, qi, misc, ki_new, k_new, v_new, ckit, ckt, cvt, DB, ts):
    n_pages = page_table.shape[1]
    past = n_pages * PAGE_SIZE
    n_sel = min(TOPK_MAX, (past + ts) // 4)
    lpad = past + PAGE_SIZE
    idx_bits = max(1, (lpad - 1).bit_length())
    ck = LANES * math.gcd(lpad // LANES, 5)
    assert ts % SUBLANES == 0 and ts <= PAGE_SIZE and n_pages >= 1
    blk = lambda w: pl.BlockSpec((ts, w), lambda b, pt: (b, 0))
    whole = lambda w: pl.BlockSpec((DB * ts, w), lambda b, pt: (0, 0))
    hbm = pl.BlockSpec(memory_space=pl.ANY)
    keys, thr, jstar = pl.pallas_call(
        functools.partial(_sample_score_kernel, n_sel, idx_bits, n_pages, ts, ck),
        grid_spec=pltpu.PrefetchScalarGridSpec(
            num_scalar_prefetch=1,
            grid=(DB,),
            in_specs=[blk(N_IDX_HEADS * IDX_DIM), blk(LANES), blk(IDX_DIM), hbm],
            out_specs=[whole(lpad), whole(LANES), whole(LANES)],
            scratch_shapes=[pltpu.VMEM((2, IDX_DIM, past), F32), pltpu.SemaphoreType.DMA((2, 1))]),
        out_shape=[jax.ShapeDtypeStruct((DB * ts, lpad), F32),
                   jax.ShapeDtypeStruct((DB * ts, LANES), F32),
                   jax.ShapeDtypeStruct((DB * ts, LANES), I32)],
        compiler_params=_cparams(1),
        name="sample_score",
    )(page_table, qi, misc, ki_new, ckit)
    return pl.pallas_call(
        functools.partial(_sample_attend_kernel, n_pages, ts),
        grid_spec=pltpu.PrefetchScalarGridSpec(
            num_scalar_prefetch=1,
            grid=(DB,),
            in_specs=[pl.BlockSpec(memory_space=pltpu.SMEM), blk(ATT_WIDTH), blk(lpad), blk(LANES),
                      blk(LANES), blk(LANES), blk(LANES), hbm, hbm],
            out_specs=blk(ATT_WIDTH),
            scratch_shapes=[pltpu.VMEM((2, LANES, past), F32), pltpu.VMEM((2, LANES, past), F32),
                            pltpu.SemaphoreType.DMA((2, 2)),
                            pltpu.VMEM((N_HEADS, 3, ts, LANES), F32)]),
        out_shape=jax.ShapeDtypeStruct((DB * ts, ATT_WIDTH), F32),
        compiler_params=_cparams(1),
        name="sample_attend",
    )(page_table, rel_bias, qa, keys, thr, jstar, k_new, v_new, ckt, cvt)


def _pack_layer_weights(w_in, b_i, b_f, w_out, w_up, w_down):
    D = w_in.shape[0]
    sizes = (ATT_WIDTH, N_KV_HEADS * HEAD_DIM, N_KV_HEADS * HEAD_DIM, N_IDX_HEADS * IDX_DIM, IDX_DIM,
             N_IDX_HEADS, M_WIDTH, M_WIDTH, M_WIDTH, M_WIDTH, M_HEADS, M_HEADS)
    assert w_in.shape[1] == sum(sizes)
    pts = np.cumsum((0,) + sizes)
    seg = [w_in[:, pts[i]:pts[i + 1]] for i in range(len(sizes))]
    qa, k, v, qi, ki, wi, qm, km, vm, om, im, fm = seg
    perm = np.asarray(HEAD_PERM)
    qa = qa.reshape(D, N_HEADS, HEAD_DIM)[:, perm].reshape(D, ATT_WIDTH)
    misc = jnp.concatenate([wi, im, fm, jnp.zeros((D, LANES - N_IDX_HEADS - 2 * M_HEADS), w_in.dtype)], axis=1)
    wp = jnp.concatenate([qa, k, v, qi, ki, ki, misc, qm, km, vm, om], axis=1).astype(BF16)
    assert wp.shape[1] == N_PACK
    gate_bias = jnp.broadcast_to(jnp.concatenate([b_i, b_f]).astype(F32)[:, None], (2 * M_HEADS, LANES))
    woa = w_out[:ATT_WIDTH].reshape(N_HEADS, HEAD_DIM, -1)[perm].reshape(ATT_WIDTH, -1).astype(BF16)
    woh = w_out[ATT_WIDTH:].astype(BF16)
    wt = jnp.concatenate([seg[1].T, seg[2].T, ki.T, jnp.zeros((LANES - IDX_DIM, D), w_in.dtype),
                          seg[0].T, qi.T, misc.T], axis=0).astype(BF16)
    assert wt.shape[0] == N_TPACK
    return wp, wt, gate_bias, woa, woh, w_up.astype(BF16), w_down.astype(BF16)


def _layer(x, packed, g1, g2, mnorm, rel_bias, gf, final_norm, past):
    wp, wt, gate_bias, woa, woh, wup, wdn = packed
    B, T, D = x.shape
    x2 = x.reshape(B * T, D)
    tm = math.gcd(T if past is None else B * T, 512)
    kv_w = N_KV_HEADS * HEAD_DIM
    if past is None:
        (misc, qm, km, vm, om, kb, kib, kt, vt, kit, vtb, qat, qit, misct) = _inproj(
            x2, g1.reshape(1, D), wp, wt, B, T, tm)
        attn = _dsa_prompt(rel_bias, qat, qit, misct, kib, kb, vtb, B, T)
        state = None
        k_new = kt.reshape(B, N_KV_HEADS, HEAD_DIM, T).transpose(0, 3, 1, 2)
        v_new = vt.reshape(B, N_KV_HEADS, HEAD_DIM, T).transpose(0, 3, 1, 2)
        ki_new = kit.transpose(0, 2, 1)
    else:
        (misc, qm, km, vm, om, qa, qi, k, v, ki) = _inproj(x2, g1.reshape(1, D), wp, None, B, T, tm)
        page_table, cache_k, cache_v, cache_kidx, c0, n0, m0 = past
        n_pool = cache_k.shape[0]
        ckt = cache_k.transpose(0, 2, 3, 1).reshape(n_pool, kv_w, PAGE_SIZE)
        cvt = cache_v.transpose(0, 2, 3, 1).reshape(n_pool, kv_w, PAGE_SIZE)
        ckit = cache_kidx.transpose(0, 2, 1)
        attn = _dsa_sample(page_table, rel_bias, qa, qi, misc, ki, k, v, ckit, ckt, cvt, B, T)
        state = (c0, n0, jnp.broadcast_to(m0[..., None], m0.shape + (LANES,)))
        k_new = k.reshape(B, T, N_KV_HEADS, HEAD_DIM)
        v_new = v.reshape(B, T, N_KV_HEADS, HEAD_DIM)
        ki_new = ki.reshape(B, T, IDX_DIM)
    h, c_new, n_new, m_new = _mlstm(qm, km, vm, om, misc, gate_bias, mnorm.reshape(1, M_WIDTH), state,
                                    B, T, BF16 if past is None else F32)
    y = _post(x2, attn, h, woa, woh, g2.reshape(1, D), wup, wdn, gf.reshape(1, D), final_norm, tm)
    return (y.reshape(B, T, D), k_new, v_new, ki_new, c_new, n_new, m_new[..., 0])


def kernel(x_prompt, x_sample, cache_k, cache_v, cache_kidx, page_table, state_C, state_n, state_m,
           w_in, b_igate, b_fgate, mlstm_norm, rel_bias, w_out, norm1, norm2, w_up, w_down, norm_f):
    depth = w_in.shape[0]
    xp, xs = x_prompt, x_sample
    outs_p, outs_s = [], []
    for l in range(depth):
        packed = _pack_layer_weights(w_in[l], b_igate[l], b_fgate[l], w_out[l], w_up[l], w_down[l])
        last = l == depth - 1
        common = (packed, norm1[l], norm2[l], mlstm_norm[l], rel_bias, norm_f, last)
        rp = _layer(xp, *common, None)
        rs = _layer(xs, *common, (page_table, cache_k[l], cache_v[l], cache_kidx[l],
                                  state_C[l], state_n[l], state_m[l]))
        xp, xs = rp[0], rs[0]
        outs_p.append(rp[1:])
        outs_s.append(rs[1:])
    stack = lambda outs, i: jnp.stack([o[i] for o in outs])
    return ((xp, xs) + tuple(stack(outs_p, i) for i in range(6))
            + tuple(stack(outs_s, i) for i in range(6)))
```

```python
import functools
import math

import numpy as np
import jax
import jax.numpy as jnp
from jax import lax
from jax.experimental import pallas as pl
from jax.experimental.pallas import tpu as pltpu

F32 = jnp.float32
BF16 = jnp.bfloat16
I32 = jnp.int32

N_HEADS = 8
HEAD_DIM = 64
N_KV_HEADS = 2
GROUP = N_HEADS // N_KV_HEADS
N_IDX_HEADS = 8
IDX_DIM = 64
TOPK_MAX = 256
N_BUCKETS = 32
MAX_DISTANCE = 128
M_HEADS = 4
M_HEAD_DIM = 128
PAGE_SIZE = 128
EPS = 1e-6
NEG = -1e30
LOG2E = math.log2(math.e)
ATT_WIDTH = N_HEADS * HEAD_DIM
M_WIDTH = M_HEADS * M_HEAD_DIM

LANES = 128
SUBLANES = 8
VMEM_LIMIT = 56 * 1024 * 1024

C_QA = 0
C_K = C_QA + ATT_WIDTH
C_V = C_K + LANES
C_QI = C_V + LANES
C_KI2 = C_QI + N_IDX_HEADS * IDX_DIM
C_MISC = C_KI2 + LANES
C_QM = C_MISC + LANES
C_KM = C_QM + M_WIDTH
C_VM = C_KM + M_WIDTH
C_OM = C_VM + M_WIDTH
N_PACK = C_OM + M_WIDTH
MISC_W = 0
MISC_I = 8
MISC_F = 12

HEAD_PERM = (0, 4, 1, 5, 2, 6, 3, 7)

TQ = 256
CKP = 256
ML = 256

NT_DIMS = (((1,), (1,)), ((), ()))
TN_DIMS = (((0,), (0,)), ((), ()))


def _bucket_bounds():
    max_exact = N_BUCKETS // 2
    scale = (N_BUCKETS - max_exact) / math.log(MAX_DISTANCE / max_exact)

    def bucket(n, dt):
        if n < max_exact:
            return n
        val = np.log(np.asarray(max(n, 1), dt) / dt(max_exact)) * dt(scale)
        return min(max_exact + int(val), N_BUCKETS - 1)

    table = [bucket(n, np.float32) for n in range(MAX_DISTANCE + 2)]
    assert table == [bucket(n, np.float64) for n in range(MAX_DISTANCE + 2)]
    assert table[MAX_DISTANCE] == N_BUCKETS - 1
    return [next(d for d, b in enumerate(table) if b >= k) for k in range(N_BUCKETS)]


BUCKET_BOUNDS = _bucket_bounds()


def _cparams(n_axes):
    return pltpu.CompilerParams(dimension_semantics=("arbitrary",) * n_axes,
                                vmem_limit_bytes=VMEM_LIMIT)


def _const_spec(shape):
    nd = len(shape)
    return pl.BlockSpec(shape, lambda *_: (0,) * nd, pipeline_mode=pl.Buffered(1))


def _rms(x, g):
    return x * lax.rsqrt(jnp.mean(x * x, axis=-1, keepdims=True) + EPS) * g


def _inproj_mlstm(mm, misc_ref, qm_ref, km_ref, vm_ref, om_ref):
    misc_ref[...] = mm(C_MISC, LANES)
    qm_ref[...] = mm(C_QM, M_WIDTH).astype(qm_ref.dtype)
    km_ref[...] = (mm(C_KM, M_WIDTH) * (M_HEAD_DIM ** -0.5)).astype(km_ref.dtype)
    vm_ref[...] = mm(C_VM, M_WIDTH).astype(vm_ref.dtype)
    om_ref[...] = mm(C_OM, M_WIDTH)


def _inproj_rows_kernel(x_ref, g_ref, w_ref, misc_ref, qm_ref, km_ref, vm_ref, om_ref,
                        qa_ref, qi_ref, k_ref, v_ref, ki_ref):
    ub = _rms(x_ref[...], g_ref[...]).astype(BF16)
    mm = lambda c0, n: jnp.dot(ub, w_ref[:, c0:c0 + n], preferred_element_type=F32)
    _inproj_mlstm(mm, misc_ref, qm_ref, km_ref, vm_ref, om_ref)
    qa_ref[...] = mm(C_QA, ATT_WIDTH) * (HEAD_DIM ** -0.5)
    qi_ref[...] = mm(C_QI, N_IDX_HEADS * IDX_DIM)
    k_ref[...] = mm(C_K, LANES)
    v_ref[...] = mm(C_V, LANES)
    ki_ref[...] = mm(C_KI2, LANES)[:, :IDX_DIM]


R_K = 0
R_V = R_K + LANES
R_KI = R_V + LANES
R_QA = R_KI + LANES
R_QI = R_QA + ATT_WIDTH
R_MISC = R_QI + N_IDX_HEADS * IDX_DIM
N_TPACK = R_MISC + LANES


def _inproj_cols_kernel(x_ref, g_ref, w_ref, wt_ref, misc_ref, qm_ref, km_ref, vm_ref, om_ref,
                        kb_ref, kib_ref, kt_ref, vt_ref, kit_ref, vtb_ref, qat_ref, qit_ref,
                        misct_ref):
    ub = _rms(x_ref[...], g_ref[...]).astype(BF16)
    mm = lambda c0, n: jnp.dot(ub, w_ref[:, c0:c0 + n], preferred_element_type=F32)
    _inproj_mlstm(mm, misc_ref, qm_ref, km_ref, vm_ref, om_ref)
    kb_ref[...] = mm(C_K, LANES).astype(BF16)
    kib_ref[...] = mm(C_KI2, LANES)[:, :IDX_DIM].astype(BF16)

    def mt(r0, n):
        return lax.dot_general(wt_ref[r0:r0 + n, :], ub, NT_DIMS, preferred_element_type=F32)

    kt_ref[0] = mt(R_K, LANES)
    vt = mt(R_V, LANES)
    vt_ref[0] = vt
    for j in range(vtb_ref.shape[1]):
        vtb_ref[0, j] = vt[:, j * CKP:(j + 1) * CKP].astype(BF16)
    kit_ref[0] = mt(R_KI, IDX_DIM)
    qat_ref[0] = (mt(R_QA, ATT_WIDTH) * (HEAD_DIM ** -0.5 * LOG2E)).astype(BF16)
    qit_ref[0] = mt(R_QI, N_IDX_HEADS * IDX_DIM).astype(BF16)
    misct_ref[0] = mt(R_MISC, LANES)


def _inproj(x2, g1, wp, wt, B, T, tm):
    R, D = x2.shape
    assert R == B * T and R % tm == 0
    mdt = F32 if wt is None else BF16
    row = lambda i: (i, 0)
    outs = [(LANES, F32), (M_WIDTH, mdt), (M_WIDTH, mdt), (M_WIDTH, mdt), (M_WIDTH, F32)]
    in_specs = [pl.BlockSpec((tm, D), row), _const_spec((1, D)), _const_spec((D, N_PACK))]
    if wt is None:
        kern, args = _inproj_rows_kernel, (x2, g1, wp)
        outs += [(ATT_WIDTH, F32), (N_IDX_HEADS * IDX_DIM, F32), (LANES, F32), (LANES, F32), (IDX_DIM, F32)]
    else:
        kern, args = _inproj_cols_kernel, (x2, g1, wp, wt)
        in_specs.append(_const_spec(wt.shape))
        outs += [(LANES, BF16), (IDX_DIM, BF16)]
    out_specs = [pl.BlockSpec((tm, w), row) for w, _ in outs]
    out_shape = [jax.ShapeDtypeStruct((R, w), dt) for w, dt in outs]
    if wt is not None:
        assert T % tm == 0 and tm % CKP == 0
        tpb, cpt = T // tm, tm // CKP
        cols = lambda i: (i // tpb, 0, i % tpb)
        for w, dt in ((LANES, F32), (LANES, F32), (IDX_DIM, F32)):
            out_specs.append(pl.BlockSpec((1, w, tm), cols))
            out_shape.append(jax.ShapeDtypeStruct((B, w, T), dt))
        out_specs.append(pl.BlockSpec((1, cpt, LANES, CKP), lambda i: (i // tpb, i % tpb, 0, 0)))
        out_shape.append(jax.ShapeDtypeStruct((B, T // CKP, LANES, CKP), BF16))
        for w, dt in ((ATT_WIDTH, BF16), (N_IDX_HEADS * IDX_DIM, BF16), (LANES, F32)):
            out_specs.append(pl.BlockSpec((1, w, tm), cols))
            out_shape.append(jax.ShapeDtypeStruct((B, w, T), dt))
    return pl.pallas_call(
        kern,
        grid=(R // tm,),
        in_specs=in_specs,
        out_specs=out_specs,
        out_shape=out_shape,
        compiler_params=_cparams(1),
        name="inproj",
    )(*args)


def _post_kernel(ff_chunk, final_norm, x_ref, a_ref, h_ref, woa_ref, woh_ref, g2_ref, wup_ref,
                 wdn_ref, gf_ref, y_ref):
    mix = jnp.dot(a_ref[...].astype(BF16), woa_ref[...], preferred_element_type=F32)
    mix = mix + jnp.dot(h_ref[...].astype(BF16), woh_ref[...], preferred_element_type=F32)
    hres = x_ref[...] + mix
    f = _rms(hres, g2_ref[...]).astype(BF16)
    acc = hres
    for c0 in range(0, wup_ref.shape[1], ff_chunk):
        up = jnp.dot(f, wup_ref[:, c0:c0 + ff_chunk], preferred_element_type=F32)
        r = jnp.maximum(up, 0.0)
        acc = acc + jnp.dot((r * r).astype(BF16), wdn_ref[c0:c0 + ff_chunk, :],
                            preferred_element_type=F32)
    y_ref[...] = _rms(acc, gf_ref[...]) if final_norm else acc


def _post(x2, attn, h, woa, woh, g2, wup, wdn, gf, final_norm, tm):
    R, D = x2.shape
    dff = wup.shape[1]
    assert R % tm == 0
    row = lambda i: (i, 0)
    return pl.pallas_call(
        functools.partial(_post_kernel, min(dff, 1024), final_norm),
        grid=(R // tm,),
        in_specs=[pl.BlockSpec((tm, D), row), pl.BlockSpec((tm, ATT_WIDTH), row),
                  pl.BlockSpec((tm, M_WIDTH), row), _const_spec(woa.shape), _const_spec(woh.shape),
                  _const_spec((1, D)), _const_spec(wup.shape), _const_spec(wdn.shape),
                  _const_spec((1, D))],
        out_specs=pl.BlockSpec((tm, D), row),
        out_shape=jax.ShapeDtypeStruct((R, D), F32),
        compiler_params=_cparams(1),
        name="post",
    )(x2, attn, h, woa, woh, g2, wup, wdn, gf)


def _log_sigmoid(x):
    return -(jnp.maximum(-x, 0.0) + jnp.log1p(jnp.exp(-jnp.abs(x))))


def _mlstm_kernel(nvalid, has_state, *refs):
    if has_state:
        (q_ref, k_ref, v_ref, o_ref, misc_ref, gb_ref, mn_ref, c0_ref, n0_ref, m0_ref,
         h_ref, c_ref, n_ref, m_ref, z_scr, rows_scr) = refs
    else:
        (q_ref, k_ref, v_ref, o_ref, misc_ref, gb_ref, mn_ref,
         h_ref, c_ref, n_ref, m_ref, z_scr, rows_scr) = refs
    nb, nc, L = z_scr.shape[0], z_scr.shape[1], z_scr.shape[2]
    c = pl.program_id(1)

    def padded(x, dt):
        x = x.astype(dt)
        if nvalid == L:
            return x
        return jnp.concatenate([x, jnp.zeros((L - nvalid, x.shape[1]), dt)], axis=0)

    r2 = lax.broadcasted_iota(I32, (L, L), 0)
    c2 = lax.broadcasted_iota(I32, (L, L), 1)
    tril = r2 >= c2

    @pl.when(c == 0)
    def _():
        if has_state:
            c_ref[...] = c0_ref[...]
            n_ref[...] = n0_ref[...]
            m_ref[...] = m0_ref[...]
        else:
            c_ref[...] = jnp.zeros_like(c_ref)
            n_ref[...] = jnp.zeros_like(n_ref)
            m_ref[...] = jnp.zeros_like(m_ref)
        assert MISC_F == MISC_I + M_HEADS and MISC_I % SUBLANES == 0
        row8 = lax.broadcasted_iota(I32, (SUBLANES, L), 0)
        tok8 = lax.broadcasted_iota(I32, (SUBLANES, L), 1)
        bias8 = jnp.concatenate([gb_ref[...]] * (L // LANES), axis=1)
        triu = (r2 <= c2).astype(F32)
        for s in range(nb):
            for cc in range(nc):
                misc_t = padded(misc_ref[s, cc * nvalid:(cc + 1) * nvalid, :], F32).T
                gx = misc_t[MISC_I:MISC_I + SUBLANES] + bias8
                gates = jnp.where(row8 >= M_HEADS, _log_sigmoid(gx), gx)
                if nvalid != L:
                    gates = jnp.where(tok8 < nvalid, gates, jnp.where(row8 >= M_HEADS, 0.0, NEG))
                cum = jnp.dot(gates, triu, precision=lax.Precision.HIGHEST,
                              preferred_element_type=F32)
                rows = jnp.where(row8 >= M_HEADS, cum, gates)
                rows_scr[s, cc] = rows
                z_scr[s, cc] = jnp.concatenate([rows, jnp.zeros((LANES - SUBLANES, L), F32)], axis=0).T

    loaded = []
    for s in range(nb):
        loaded.append((z_scr[s, c], rows_scr[s, c], padded(q_ref[s], BF16), padded(k_ref[s], BF16),
                       padded(v_ref[s], BF16), padded(o_ref[s], F32), m_ref[s], c_ref[s], n_ref[s]))
    chains = [(s, hd) for s in range(nb) for hd in range(M_HEADS)]

    def operands(s, hd):
        z, rows, qb, kb, vb, ob, m_all, c_all, n_all = loaded[s]
        sl = slice(hd * M_HEAD_DIM, (hd + 1) * M_HEAD_DIM)
        return dict(
            sl=sl, q=qb[:, sl], k=kb[:, sl], v=vb[:, sl], o=ob[:, sl],
            icol=z[:, hd:hd + 1], bcol=z[:, M_HEADS + hd:M_HEADS + hd + 1],
            irow=rows[hd:hd + 1, :], brow=rows[M_HEADS + hd:M_HEADS + hd + 1, :],
            m_prev=m_all[hd:hd + 1, 0:1], s_prev=c_all[hd], n_prev=n_all[hd:hd + 1, :])

    def lane_sum(x):
        ones = jnp.ones((x.shape[1], LANES), BF16)
        head = x.astype(BF16)
        tail = (x - head.astype(F32)).astype(BF16)
        return (jnp.dot(head, ones, preferred_element_type=F32)
                + jnp.dot(tail, ones, preferred_element_type=F32))

    st = [operands(s, hd) for s, hd in chains]
    for x in st:
        x["qk"] = lax.dot_general(x["q"], x["k"], NT_DIMS, preferred_element_type=F32)
        x["qs"] = jnp.dot(x["q"], x["s_prev"].astype(BF16), preferred_element_type=F32)
        x["qn"] = lane_sum(x["q"].astype(F32) * x["n_prev"])
    for x in st:
        g = x["bcol"] + x["m_prev"]
        dm = jnp.where(tril, x["bcol"] - x["brow"] + x["irow"], NEG)
        x["mt"] = jnp.maximum(g, jnp.max(dm, axis=1, keepdims=True))
        x["gw"] = jnp.exp(g - x["mt"])
        x["qk"] = x["qk"] * jnp.exp(dm - x["mt"])
        b_last = x["bcol"][L - 1:L, :]
        g_last = b_last + x["m_prev"]
        a = b_last - x["bcol"] + x["icol"]
        x["m_new"] = jnp.maximum(g_last, jnp.max(a, axis=0, keepdims=True))
        x["sw"] = jnp.exp(g_last - x["m_new"])
        x["ak"] = jnp.exp(a - x["m_new"]) * x["k"].astype(F32)
    for x in st:
        x["pv"] = jnp.dot(x["qk"].astype(BF16), x["v"], preferred_element_type=F32)
        x["qksum"] = lane_sum(x["qk"])
        x["kv"] = lax.dot_general(x["ak"].astype(BF16), x["v"], TN_DIMS, preferred_element_type=F32)
    stores = []
    for (s, hd), x in zip(chains, st):
        num = x["gw"] * x["qs"] + x["pv"]
        den = x["gw"] * x["qn"] + x["qksum"]
        hh = num / jnp.maximum(jnp.abs(den), jnp.exp(-x["mt"]))
        hh = hh * lax.rsqrt(lane_sum(hh * hh) * (1.0 / M_HEAD_DIM) + EPS)
        hh = hh * mn_ref[:, x["sl"]] * jax.nn.sigmoid(x["o"])
        c_new = x["sw"] * x["s_prev"] + x["kv"]
        n_new = x["sw"] * x["n_prev"] + jnp.sum(x["ak"], axis=0, keepdims=True)
        stores.append((s, hd, x["sl"], hh[:nvalid].astype(h_ref.dtype), c_new, n_new,
                       jnp.broadcast_to(x["m_new"], (1, LANES))))
    for s, hd, sl, h_new, c_new, n_new, m_new in stores:
        h_ref[s, :, sl] = h_new
        c_ref[s, hd] = c_new
        n_ref[s, hd:hd + 1, :] = n_new
        m_ref[s, hd:hd + 1, :] = m_new


def _mlstm(qm, km, vm, om, misc, gate_bias, mnorm, state, B, T, hdt):
    L = next((c for c in (ML, LANES) if T % c == 0), LANES)
    nvalid = L if T % L == 0 else T
    assert nvalid <= L and T % nvalid == 0 and nvalid % SUBLANES == 0
    nc = T // nvalid
    nb = next(n for n in ((2, 1) if nvalid == L else (4, 2, 1)) if B % n == 0)
    seq3 = lambda a: a.reshape(B, T, a.shape[-1])
    blk = lambda w: pl.BlockSpec((nb, nvalid, w), lambda b, c: (b, c, 0))
    st_specs = [pl.BlockSpec((nb, M_HEADS, M_HEAD_DIM, M_HEAD_DIM), lambda b, c: (b, 0, 0, 0)),
                pl.BlockSpec((nb, M_HEADS, M_HEAD_DIM), lambda b, c: (b, 0, 0)),
                pl.BlockSpec((nb, M_HEADS, LANES), lambda b, c: (b, 0, 0))]
    in_specs = [blk(M_WIDTH), blk(M_WIDTH), blk(M_WIDTH), blk(M_WIDTH),
                pl.BlockSpec((nb, T, LANES), lambda b, c: (b, 0, 0)),
                pl.BlockSpec((SUBLANES, LANES), lambda b, c: (0, 0)),
                pl.BlockSpec((1, M_WIDTH), lambda b, c: (0, 0))]
    args = [seq3(qm), seq3(km), seq3(vm), seq3(om), seq3(misc), gate_bias, mnorm]
    if state is not None:
        in_specs += st_specs
        args += list(state)
    h, c_new, n_new, m_new = pl.pallas_call(
        functools.partial(_mlstm_kernel, nvalid, state is not None),
        grid=(B // nb, nc),
        in_specs=in_specs,
        out_specs=[blk(M_WIDTH)] + st_specs,
        out_shape=[jax.ShapeDtypeStruct((B, T, M_WIDTH), hdt),
                   jax.ShapeDtypeStruct((B, M_HEADS, M_HEAD_DIM, M_HEAD_DIM), F32),
                   jax.ShapeDtypeStruct((B, M_HEADS, M_HEAD_DIM), F32),
                   jax.ShapeDtypeStruct((B, M_HEADS, LANES), F32)],
        scratch_shapes=[pltpu.VMEM((nb, nc, L, LANES), F32), pltpu.VMEM((nb, nc, SUBLANES, L), F32)],
        compiler_params=_cparams(2),
        name="mlstm",
    )(*args)
    return h.reshape(B * T, M_WIDTH), c_new, n_new, m_new


def _sortable_key(score):
    bits = lax.bitcast_convert_type(score, I32)
    return bits ^ (lax.shift_right_arithmetic(bits, 31) & 0x7FFFFFFF)


def _key_to_score(key):
    return lax.bitcast_convert_type(key ^ (lax.shift_right_arithmetic(key, 31) & 0x7FFFFFFF), F32)


def _build_bias_strip(strip_ref, rb_ref, off, key_axis, log2_relative=False):
    ntiles, tile = strip_ref.shape[1], strip_ref.shape[2:]
    i = lax.broadcasted_iota(I32, tile, 1 - key_axis)
    x = lax.broadcasted_iota(I32, tile, key_axis)

    def entry(b, h):
        if log2_relative:
            return (rb_ref[b, h] - rb_ref[N_BUCKETS - 1, h]) * LOG2E
        return rb_ref[b, h]

    for t in range(ntiles):
        dist = i + (off - LANES * t) - x
        for h in range(N_HEADS):
            val = jnp.full(tile, entry(0, h), F32)
            for b in range(1, N_BUCKETS):
                val = jnp.where(dist >= BUCKET_BOUNDS[b], entry(b, h), val)
            strip_ref[h, t] = val


def _counter(score_ref, nk, ck, key_axis):
    def count(pred):
        def body(c, acc):
            c0 = pl.multiple_of(c * ck, ck)
            sc = score_ref[:, pl.ds(c0, ck)] if key_axis == 1 else score_ref[pl.ds(c0, ck), :]
            idx = c0 + lax.broadcasted_iota(I32, sc.shape, key_axis)
            hit = jnp.where(pred(sc, idx), 1, 0)
            if key_axis == 1:
                for j in range(ck // LANES):
                    acc = acc + hit[:, j * LANES:(j + 1) * LANES]
                return acc
            return acc + jnp.sum(hit.reshape(ck // SUBLANES, SUBLANES, hit.shape[1]), axis=0)

        nq = score_ref.shape[1 - key_axis]
        acc0 = jnp.zeros((nq, LANES) if key_axis == 1 else (SUBLANES, nq), I32)
        return jnp.sum(lax.fori_loop(0, nk, body, acc0), axis=key_axis, keepdims=True)

    return count


IMIN = jnp.iinfo(jnp.int32).min


def _kth_largest_by_count(count, qshape, n_sel):
    def bit_step(i, key):
        cand = key + lax.shift_left(jnp.int32(1), 31 - i)
        cand_score = _key_to_score(cand)
        return jnp.where(count(lambda sc, idx: sc >= cand_score) >= n_sel, cand, key)

    return _key_to_score(lax.fori_loop(0, 32, bit_step, jnp.full(qshape, IMIN, I32)))


def _bit_planes(words):
    a = list(words)
    j, m = 16, 0x0000FFFF
    while j:
        k = 0
        while k < 32:
            t = (a[k] ^ lax.shift_right_logical(a[k + j], j)) & m
            a[k] = a[k] ^ t
            a[k + j] = a[k + j] ^ lax.shift_left(t, j)
            k = (k + j + 1) & ~j
        j >>= 1
        m = (m ^ (m << j)) & 0xFFFFFFFF
        m = m - (1 << 32) if m >= (1 << 31) else m
    return a[::-1]


def _kth_largest_by_planes(score_ref, planes_ref, nk, ck, n_sel):
    nc = planes_ref.shape[1]
    nq = score_ref.shape[1]
    assert ck == 32 * SUBLANES

    def pack_chunk(c, _):
        c0 = pl.multiple_of(c * ck, ck)
        u = _sortable_key(score_ref[pl.ds(c0, ck), :]) ^ IMIN
        u = u.reshape(32, SUBLANES, nq)
        for b, plane in enumerate(_bit_planes([u[v] for v in range(32)])):
            planes_ref[b, c] = plane
        return 0

    lax.fori_loop(0, nk, pack_chunk, 0)
    cand0 = tuple(jnp.where(c < nk, jnp.full((SUBLANES, nq), -1, I32), 0) for c in range(nc))
    return _plane_search(planes_ref, cand0, n_sel, 0)


def _kth_largest_by_planes_lanes(score_ref, planes_ref, n_sel):
    nq, ntiles = score_ref.shape[0], score_ref.shape[1] // LANES
    cand0 = []
    for g in range(planes_ref.shape[1]):
        real = min(32, ntiles - 32 * g)
        words = [_sortable_key(score_ref[:, (32 * g + v) * LANES:(32 * g + v + 1) * LANES]) ^ IMIN
                 if v < real else jnp.zeros((nq, LANES), I32) for v in range(32)]
        for b, plane in enumerate(_bit_planes(words)):
            planes_ref[b, g] = plane
        cand0.append(jnp.full((nq, LANES), -(1 << (32 - real)), I32))
    return _plane_search(planes_ref, tuple(cand0), n_sel, 1)


def _plane_search(planes_ref, cand0, n_sel, key_axis):
    def bit_step(i, carry):
        cand, n_above, thr_u = carry
        b = 31 - i
        ones = [m & planes_ref[b, c] for c, m in enumerate(cand)]
        pop = lax.population_count(ones[0])
        for o in ones[1:]:
            pop = pop + lax.population_count(o)
        tot = jnp.sum(pop, axis=key_axis, keepdims=True)
        take = n_above + tot >= n_sel
        cand = tuple(jnp.where(take, o, m ^ o) for o, m in zip(ones, cand))
        n_above = jnp.where(take, n_above, n_above + tot)
        thr_u = thr_u | jnp.where(take, lax.shift_left(jnp.int32(1), b), 0)
        return cand, n_above, thr_u

    zero = jnp.zeros_like(jnp.sum(cand0[0], axis=key_axis, keepdims=True))
    _, _, thr_u = lax.fori_loop(0, 32, bit_step, (cand0, zero, zero))
    return _key_to_score(thr_u ^ IMIN)


def _select_topk(thr, check, count, qshape, n_sel, idx_bits):
    imax = jnp.iinfo(jnp.int32).max

    def with_counts(t):
        return t, count(lambda sc, idx: sc > t), count(lambda sc, idx: sc >= t)

    thr, n_gt, n_ge = with_counts(thr)
    if check:
        good = jnp.min(jnp.where(n_gt < n_sel, jnp.where(n_ge >= n_sel, 1, 0), 0)) > 0
        thr, n_gt, n_ge = lax.cond(
            good, lambda _: (thr, n_gt, n_ge),
            lambda _: with_counts(_kth_largest_by_count(count, qshape, n_sel)), 0)
    need = n_sel - n_gt

    def tie_search(_):
        def idx_step(i, lo):
            cand = lo + lax.shift_left(jnp.int32(1), idx_bits - 1 - i)
            cnt = count(lambda sc, idx: jnp.where(sc == thr, idx, imax) < cand)
            return jnp.where(cnt < need, cand, lo)

        return lax.fori_loop(0, idx_bits, idx_step, jnp.zeros(qshape, I32))

    all_ties = jnp.full(qshape, imax, I32)
    jstar = lax.cond(jnp.max(n_ge) > n_sel, tie_search, lambda _: all_ties, 0)
    jstar = jnp.where(n_ge > n_sel, jstar, all_ties)
    return thr, jstar


def _valid_mask(scores, idx, thr, jstar, qpos):
    sel = jnp.where(scores > thr, 1, jnp.where(scores == thr, jnp.where(idx <= jstar, 1, 0), 0))
    return jnp.where(idx <= qpos, sel, 0) > 0


def _group_queries(qa, tq):
    lane = lax.broadcasted_iota(I32, (tq, LANES), 1)
    out = []
    for n in range(N_KV_HEADS):
        keep = (lane < HEAD_DIM) if n == 0 else (lane >= HEAD_DIM)
        tiles = [jnp.where(keep, qa[:, j * LANES:(j + 1) * LANES], jnp.zeros((), qa.dtype))
                 for j in range(GROUP)]
        out.append(jnp.concatenate(tiles, axis=0).astype(BF16))
    return out


def _write_attn(out_ref, carries, tq):
    lane = lax.broadcasted_iota(I32, (tq, LANES), 1)
    res = [acc / l for (_, l, acc) in carries]
    for j in range(GROUP):
        tile = jnp.where(lane < HEAD_DIM, res[0][j * tq:(j + 1) * tq], res[1][j * tq:(j + 1) * tq])
        out_ref[:, j * LANES:(j + 1) * LANES] = tile.astype(out_ref.dtype)


def _dsa_prompt_kernel(n_sel, idx_bits, rb_ref, qat_ref, qit_ref, misct_ref, ki_ref, k_ref, vt_ref,
                       out_ref, score_ref, planes_ref, strip_ref):
    qb = pl.program_id(1)
    q0 = qb * TQ
    nk = (q0 + TQ + CKP - 1) // CKP
    tiles_per_chunk = CKP // LANES
    back_tiles = strip_ref.shape[1] - tiles_per_chunk
    strip_off = LANES * back_tiles

    @pl.when((pl.program_id(0) == 0) & (qb == 0))
    def _():
        _build_bias_strip(strip_ref, rb_ref, strip_off, 0, log2_relative=True)
        planes_ref[...] = jnp.zeros(planes_ref.shape, I32)

    qpos = q0 + lax.broadcasted_iota(I32, (1, TQ), 1)

    qit = qit_ref[0]
    qstack = jnp.concatenate([qit[h * IDX_DIM:(h + 1) * IDX_DIM] for h in range(N_IDX_HEADS)],
                             axis=1)
    w = misct_ref[0, MISC_W:MISC_W + N_IDX_HEADS, :] * (N_IDX_HEADS ** -0.5 * IDX_DIM ** -0.5)

    def score_chunk(c, _):
        c0 = pl.multiple_of(c * CKP, CKP)
        d = jnp.dot(ki_ref[pl.ds(c0, CKP), :], qstack, preferred_element_type=F32)
        d = jnp.maximum(d, 0.0)
        s = jnp.zeros((CKP, TQ), F32)
        for h in range(N_IDX_HEADS):
            s = s + d[:, h * TQ:(h + 1) * TQ] * w[h:h + 1, :]
        idx = c0 + lax.broadcasted_iota(I32, (CKP, TQ), 0)
        score_ref[pl.ds(c0, CKP), :] = jnp.where(idx <= qpos, s, NEG)
        return 0

    lax.fori_loop(0, nk, score_chunk, 0)

    thr = _kth_largest_by_planes(score_ref, planes_ref, nk, CKP, n_sel)
    thr, jstar = _select_topk(thr, True, _counter(score_ref, nk, CKP, 0), (1, TQ), n_sel, idx_bits)

    qat = qat_ref[0]
    zeros = jnp.zeros((HEAD_DIM, TQ), qat.dtype)
    qgroups = []
    for n in range(N_KV_HEADS):
        tiles = []
        for g in range(GROUP):
            h = n * GROUP + g
            x = qat[h * HEAD_DIM:(h + 1) * HEAD_DIM]
            tiles.append(jnp.concatenate([x, zeros] if n == 0 else [zeros, x], axis=0))
        qgroups.append(jnp.concatenate(tiles, axis=1))

    def attend_chunk(near, c, carries):
        c0 = pl.multiple_of(c * CKP, CKP)
        idx = c0 + lax.broadcasted_iota(I32, (CKP, TQ), 0)
        valid = _valid_mask(score_ref[pl.ds(c0, CKP), :], idx, thr, jstar, qpos)
        kc = k_ref[pl.ds(c0, CKP), :]
        vct = vt_ref[0, c]
        tiles = [jnp.maximum(back_tiles + j - (qb * (TQ // LANES) - c * tiles_per_chunk), 0)
                 for j in range(tiles_per_chunk)]
        out = []
        for n in range(N_KV_HEADS):
            m_old, l_old, acc = carries[n]
            s = jnp.dot(kc, qgroups[n], preferred_element_type=F32)
            parts = []
            for g in range(GROUP):
                sg = s[:, g * TQ:(g + 1) * TQ]
                if near:
                    sg = sg + jnp.concatenate([strip_ref[n * GROUP + g, t] for t in tiles], axis=0)
                parts.append(jnp.where(valid, sg, NEG))
            sm = jnp.concatenate(parts, axis=1)
            m_new = jnp.maximum(m_old, jnp.max(sm, axis=0, keepdims=True))
            alpha = jnp.exp2(m_old - m_new)
            p = jnp.exp2(sm - m_new)
            l_new = alpha * l_old + jnp.sum(p, axis=0, keepdims=True)
            acc = alpha * acc + jnp.dot(vct, p.astype(BF16), preferred_element_type=F32)
            out.append((m_new, l_new, acc))
        return tuple(out)

    far, near = functools.partial(attend_chunk, False), functools.partial(attend_chunk, True)
    init = (jnp.full((1, GROUP * TQ), NEG, F32), jnp.zeros((1, GROUP * TQ), F32),
            jnp.zeros((LANES, GROUP * TQ), F32))
    n_far = jnp.clip((qb * (TQ // LANES) - back_tiles + 1) // tiles_per_chunk, 0, nk)
    carries = lax.fori_loop(0, n_far // 2, lambda i, cr: far(2 * i + 1, far(2 * i, cr)), (init, init))
    carries = lax.fori_loop(2 * (n_far // 2), n_far, far, carries)
    carries = lax.fori_loop(n_far, nk, near, carries)
    res = [acc / l for (_, l, acc) in carries]
    row = lax.broadcasted_iota(I32, (LANES, TQ), 0)
    for j in range(GROUP):
        cols = slice(j * TQ, (j + 1) * TQ)
        tile_t = jnp.where(row < HEAD_DIM, res[0][:, cols], res[1][:, cols])
        out_ref[:, j * LANES:(j + 1) * LANES] = tile_t.T.astype(out_ref.dtype)


def _dsa_prompt(rel_bias, qat, qit, misct, kib, kb, vtb, B, T):
    assert T % CKP == 0 and T % TQ == 0
    nq = T // TQ
    n_sel = min(TOPK_MAX, T // 4)
    idx_bits = max(1, (T - 1).bit_length())
    qcols = lambda w: pl.BlockSpec((1, w, TQ), lambda b, q: (b, 0, q))
    seq = lambda w: pl.BlockSpec((T, w), lambda b, q: (b, 0))
    return pl.pallas_call(
        functools.partial(_dsa_prompt_kernel, n_sel, idx_bits),
        grid=(B, nq),
        in_specs=[pl.BlockSpec(memory_space=pltpu.SMEM), qcols(ATT_WIDTH), qcols(N_IDX_HEADS * IDX_DIM),
                  qcols(LANES), seq(IDX_DIM), seq(LANES),
                  pl.BlockSpec((1, T // CKP, LANES, CKP), lambda b, q: (b, 0, 0, 0))],
        out_specs=pl.BlockSpec((TQ, ATT_WIDTH), lambda b, q: (b * nq + q, 0)),
        out_shape=jax.ShapeDtypeStruct((B * T, ATT_WIDTH), BF16),
        scratch_shapes=[pltpu.VMEM((T, TQ), F32),
                        pltpu.VMEM((32, T // CKP, SUBLANES, TQ), I32),
                        pltpu.VMEM((N_HEADS, 2 + CKP // LANES, LANES, TQ), F32)],
        compiler_params=_cparams(2),
        name="dsa_prompt",
    )(rel_bias, qat, qit, misct, kib, kb, vtb)


def _page_pipeline(pt_ref, n_pages, caches, bufs, sems):
    def copies(bb, sl, j):
        pid = pt_ref[bb, j]
        cols = pl.ds(pl.multiple_of(j * PAGE_SIZE, PAGE_SIZE), PAGE_SIZE)
        return [pltpu.make_async_copy(c.at[pid], buf.at[sl, :, cols], sems.at[sl, i])
                for i, (c, buf) in enumerate(zip(caches, bufs))]

    def start_all(bb, sl):
        def body(j, _):
            for cp in copies(bb, sl, j):
                cp.start()
            return 0
        lax.fori_loop(0, n_pages, body, 0)

    def wait_all(bb, sl):
        def body(j, _):
            for cp in copies(bb, sl, j):
                cp.wait()
            return 0
        lax.fori_loop(0, n_pages, body, 0)

    def step():
        b = pl.program_id(0)
        slot = b % 2

        @pl.when(b == 0)
        def _():
            start_all(0, 0)

        @pl.when(b + 1 < pl.num_programs(0))
        def _():
            start_all(b + 1, 1 - slot)

        wait_all(b, slot)
        return slot

    return step


def _pad_rows(x, rows):
    return jnp.concatenate([x, jnp.zeros((rows - x.shape[0], x.shape[1]), x.dtype)], axis=0)


def _sample_score_kernel(n_sel, idx_bits, n_pages, ts, ck, pt_ref, qi_ref, misc_ref, kin_ref,
                         ckit_hbm, score_ref, thr_ref, jst_ref, ki_buf, sems, planes_ref):
    b = pl.program_id(0)
    past = n_pages * PAGE_SIZE
    slot = _page_pipeline(pt_ref, n_pages, [ckit_hbm], [ki_buf], sems)()

    qi = qi_ref[...]
    qstack = jnp.concatenate([qi[:, h * IDX_DIM:(h + 1) * IDX_DIM] for h in range(N_IDX_HEADS)],
                             axis=0).astype(BF16)
    w = misc_ref[:, MISC_W:MISC_W + N_IDX_HEADS] * (N_IDX_HEADS ** -0.5 * IDX_DIM ** -0.5)
    d_past = jnp.dot(qstack, ki_buf[slot].astype(BF16), preferred_element_type=F32)
    d_own = lax.dot_general(qstack, _pad_rows(kin_ref[...], PAGE_SIZE).astype(BF16), NT_DIMS,
                            preferred_element_type=F32)
    d = jnp.maximum(jnp.concatenate([d_past, d_own], axis=1), 0.0)
    s = jnp.zeros((ts, past + PAGE_SIZE), F32)
    for h in range(N_IDX_HEADS):
        s = s + d[h * ts:(h + 1) * ts] * w[:, h:h + 1]
    idx = lax.broadcasted_iota(I32, s.shape, 1)
    qpos = past + lax.broadcasted_iota(I32, (ts, 1), 0)
    score_ref[pl.ds(pl.multiple_of(b * ts, ts), ts), :] = jnp.where(idx <= qpos, s, NEG)

    @pl.when(b == pl.num_programs(0) - 1)
    def _():
        count = _counter(score_ref, score_ref.shape[1] // ck, ck, 1)
        qshape = (score_ref.shape[0], 1)
        thr = _kth_largest_by_planes_lanes(score_ref, planes_ref, n_sel)
        thr, jstar = _select_topk(thr, True, count, qshape, n_sel, idx_bits)
        thr_ref[...] = jnp.broadcast_to(thr, thr_ref.shape)
        jst_ref[...] = jnp.broadcast_to(jstar, jst_ref.shape)


def _sample_attend_kernel(n_pages, ts, pt_ref, rb_ref, qa_ref, score_ref, thr_ref, jst_ref, kn_ref,
                          vn_ref, ckt_hbm, cvt_hbm, out_ref, k_buf, v_buf, sems, strip_ref):
    b = pl.program_id(0)
    past = n_pages * PAGE_SIZE
    rows = N_HEADS * ts

    @pl.when(b == 0)
    def _():
        _build_bias_strip(strip_ref, rb_ref, LANES * (strip_ref.shape[1] - 1), 1)

    slot = _page_pipeline(pt_ref, n_pages, [ckt_hbm, cvt_hbm], [k_buf, v_buf], sems)()

    q2 = jnp.concatenate(_group_queries(qa_ref[...], ts), axis=0)
    k_own = _pad_rows(kn_ref[...], PAGE_SIZE).astype(BF16)
    v_own = _pad_rows(vn_ref[...], PAGE_SIZE).astype(BF16)
    s_past = jnp.dot(q2, k_buf[slot].astype(BF16), preferred_element_type=F32)
    s_own = lax.dot_general(q2, k_own, NT_DIMS, preferred_element_type=F32)
    far = strip_ref[:, 0].reshape(rows, LANES)[:, 0:1]
    near = [strip_ref[:, t].reshape(rows, LANES) for t in (1, 2)]
    s = jnp.concatenate([s_past[:, :past - PAGE_SIZE] + far, s_past[:, past - PAGE_SIZE:] + near[0],
                         s_own + near[1]], axis=1)

    qpos = past + lax.broadcasted_iota(I32, (ts, 1), 0)
    scores = score_ref[...]
    valid = _valid_mask(scores, lax.broadcasted_iota(I32, scores.shape, 1), thr_ref[:, 0:1],
                        jst_ref[:, 0:1], qpos)
    s = jnp.where(valid[None], s.reshape(N_HEADS, ts, past + PAGE_SIZE), NEG).reshape(rows, -1)
    m = jnp.max(s, axis=1, keepdims=True)
    p = jnp.exp(s - m)
    l = jnp.sum(p, axis=1, keepdims=True)
    pb = p.astype(BF16)
    pv = lax.dot_general(pb[:, :past], v_buf[slot].astype(BF16), NT_DIMS, preferred_element_type=F32)
    pv = pv + jnp.dot(pb[:, past:], v_own, preferred_element_type=F32)
    half = GROUP * ts
    carries = [(None, l[n * half:(n + 1) * half], pv[n * half:(n + 1) * half]) for n in range(N_KV_HEADS)]
    _write_attn(out_ref, carries, ts)


def _dsa_sample(page_table, rel_bias, qa, qi, misc, ki_new, k_new, v_new, ckit, ckt, cvt, DB, ts):
    n_pages = page_table.shape[1]
    past = n_pages * PAGE_SIZE
    n_sel = min(TOPK_MAX, (past + ts) // 4)
    lpad = past + PAGE_SIZE
    idx_bits = max(1, (lpad - 1).bit_length())
    ck = LANES * math.gcd(lpad // LANES, 5)
    assert ts % SUBLANES == 0 and ts <= PAGE_SIZE and n_pages >= 1
    blk = lambda w: pl.BlockSpec((ts, w), lambda b, pt: (b, 0))
    whole = lambda w: pl.BlockSpec((DB * ts, w), lambda b, pt: (0, 0))
    hbm = pl.BlockSpec(memory_space=pl.ANY)
    keys, thr, jstar = pl.pallas_call(
        functools.partial(_sample_score_kernel, n_sel, idx_bits, n_pages, ts, ck),
        grid_spec=pltpu.PrefetchScalarGridSpec(
            num_scalar_prefetch=1,
            grid=(DB,),
            in_specs=[blk(N_IDX_HEADS * IDX_DIM), blk(LANES), blk(IDX_DIM), hbm],
            out_specs=[whole(lpad), whole(LANES), whole(LANES)],
            scratch_shapes=[pltpu.VMEM((2, IDX_DIM, past), F32), pltpu.SemaphoreType.DMA((2, 1)),
                            pltpu.VMEM((32, -(-lpad // (32 * LANES)), DB * ts, LANES), I32)]),
        out_shape=[jax.ShapeDtypeStruct((DB * ts, lpad), F32),
                   jax.ShapeDtypeStruct((DB * ts, LANES), F32),
                   jax.ShapeDtypeStruct((DB * ts, LANES), I32)],
        compiler_params=_cparams(1),
        name="sample_score",
    )(page_table, qi, misc, ki_new, ckit)
    return pl.pallas_call(
        functools.partial(_sample_attend_kernel, n_pages, ts),
        grid_spec=pltpu.PrefetchScalarGridSpec(
            num_scalar_prefetch=1,
            grid=(DB,),
            in_specs=[pl.BlockSpec(memory_space=pltpu.SMEM), blk(ATT_WIDTH), blk(lpad), blk(LANES),
                      blk(LANES), blk(LANES), blk(LANES), hbm, hbm],
            out_specs=blk(ATT_WIDTH),
            scratch_shapes=[pltpu.VMEM((2, LANES, past), F32), pltpu.VMEM((2, LANES, past), F32),
                            pltpu.SemaphoreType.DMA((2, 2)),
                            pltpu.VMEM((N_HEADS, 3, ts, LANES), F32)]),
        out_shape=jax.ShapeDtypeStruct((DB * ts, ATT_WIDTH), F32),
        compiler_params=_cparams(1),
        name="sample_attend",
    )(page_table, rel_bias, qa, keys, thr, jstar, k_new, v_new, ckt, cvt)


def _pack_layer_weights(w_in, b_i, b_f, w_out, w_up, w_down):
    D = w_in.shape[0]
    sizes = (ATT_WIDTH, N_KV_HEADS * HEAD_DIM, N_KV_HEADS * HEAD_DIM, N_IDX_HEADS * IDX_DIM, IDX_DIM,
             N_IDX_HEADS, M_WIDTH, M_WIDTH, M_WIDTH, M_WIDTH, M_HEADS, M_HEADS)
    assert w_in.shape[1] == sum(sizes)
    pts = np.cumsum((0,) + sizes)
    seg = [w_in[:, pts[i]:pts[i + 1]] for i in range(len(sizes))]
    qa, k, v, qi, ki, wi, qm, km, vm, om, im, fm = seg
    perm = np.asarray(HEAD_PERM)
    qa = qa.reshape(D, N_HEADS, HEAD_DIM)[:, perm].reshape(D, ATT_WIDTH)
    misc = jnp.concatenate([wi, im, fm, jnp.zeros((D, LANES - N_IDX_HEADS - 2 * M_HEADS), w_in.dtype)], axis=1)
    wp = jnp.concatenate([qa, k, v, qi, ki, ki, misc, qm, km, vm, om], axis=1).astype(BF16)
    assert wp.shape[1] == N_PACK
    gate_bias = jnp.broadcast_to(jnp.concatenate([b_i, b_f]).astype(F32)[:, None], (2 * M_HEADS, LANES))
    woa = w_out[:ATT_WIDTH].reshape(N_HEADS, HEAD_DIM, -1)[perm].reshape(ATT_WIDTH, -1).astype(BF16)
    woh = w_out[ATT_WIDTH:].astype(BF16)
    wt = jnp.concatenate([seg[1].T, seg[2].T, ki.T, jnp.zeros((LANES - IDX_DIM, D), w_in.dtype),
                          seg[0].T, qi.T, misc.T], axis=0).astype(BF16)
    assert wt.shape[0] == N_TPACK
    return wp, wt, gate_bias, woa, woh, w_up.astype(BF16), w_down.astype(BF16)


def _layer(x, packed, g1, g2, mnorm, rel_bias, gf, final_norm, past):
    wp, wt, gate_bias, woa, woh, wup, wdn = packed
    B, T, D = x.shape
    x2 = x.reshape(B * T, D)
    tm = math.gcd(T if past is None else B * T, 512)
    kv_w = N_KV_HEADS * HEAD_DIM
    if past is None:
        (misc, qm, km, vm, om, kb, kib, kt, vt, kit, vtb, qat, qit, misct) = _inproj(
            x2, g1.reshape(1, D), wp, wt, B, T, tm)
        attn = _dsa_prompt(rel_bias, qat, qit, misct, kib, kb, vtb, B, T)
        state = None
        k_new = kt.reshape(B, N_KV_HEADS, HEAD_DIM, T).transpose(0, 3, 1, 2)
        v_new = vt.reshape(B, N_KV_HEADS, HEAD_DIM, T).transpose(0, 3, 1, 2)
        ki_new = kit.transpose(0, 2, 1)
    else:
        (misc, qm, km, vm, om, qa, qi, k, v, ki) = _inproj(x2, g1.reshape(1, D), wp, None, B, T, tm)
        page_table, cache_k, cache_v, cache_kidx, c0, n0, m0 = past
        n_pool = cache_k.shape[0]
        ckt = cache_k.transpose(0, 2, 3, 1).reshape(n_pool, kv_w, PAGE_SIZE)
        cvt = cache_v.transpose(0, 2, 3, 1).reshape(n_pool, kv_w, PAGE_SIZE)
        ckit = cache_kidx.transpose(0, 2, 1)
        attn = _dsa_sample(page_table, rel_bias, qa, qi, misc, ki, k, v, ckit, ckt, cvt, B, T)
        state = (c0, n0, jnp.broadcast_to(m0[..., None], m0.shape + (LANES,)))
        k_new = k.reshape(B, T, N_KV_HEADS, HEAD_DIM)
        v_new = v.reshape(B, T, N_KV_HEADS, HEAD_DIM)
        ki_new = ki.reshape(B, T, IDX_DIM)
    h, c_new, n_new, m_new = _mlstm(qm, km, vm, om, misc, gate_bias, mnorm.reshape(1, M_WIDTH), state,
                                    B, T, BF16 if past is None else F32)
    y = _post(x2, attn, h, woa, woh, g2.reshape(1, D), wup, wdn, gf.reshape(1, D), final_norm, tm)
    return (y.reshape(B, T, D), k_new, v_new, ki_new, c_new, n_new, m_new[..., 0])


def kernel(x_prompt, x_sample, cache_k, cache_v, cache_kidx, page_table, state_C, state_n, state_m,
           w_in, b_igate, b_fgate, mlstm_norm, rel_bias, w_out, norm1, norm2, w_up, w_down, norm_f):
    depth = w_in.shape[0]
    xp, xs = x_prompt, x_sample
    outs_p, outs_s = [], []
    for l in range(depth):
        packed = _pack_layer_weights(w_in[l], b_igate[l], b_fgate[l], w_out[l], w_up[l], w_down[l])
        last = l == depth - 1
        common = (packed, norm1[l], norm2[l], mlstm_norm[l], rel_bias, norm_f, last)
        rp = _layer(xp, *common, None)
        rs = _layer(xs, *common, (page_table, cache_k[l], cache_v[l], cache_kidx[l],
                                  state_C[l], state_n[l], state_m[l]))
        xp, xs = rp[0], rs[0]
        outs_p.append(rp[1:])
        outs_s.append(rs[1:])
    stack = lambda outs, i: jnp.stack([o[i] for o in outs])
    return ((xp, xs) + tuple(stack(outs_p, i) for i in range(6))
            + tuple(stack(outs_s, i) for i in range(6)))
```

```python
import functools
import math

import numpy as np
import jax
import jax.numpy as jnp
from jax import lax
from jax.experimental import pallas as pl
from jax.experimental.pallas import tpu as pltpu

F32 = jnp.float32
BF16 = jnp.bfloat16
I32 = jnp.int32

N_HEADS = 8
HEAD_DIM = 64
N_KV_HEADS = 2
GROUP = N_HEADS // N_KV_HEADS
N_IDX_HEADS = 8
IDX_DIM = 64
TOPK_MAX = 256
N_BUCKETS = 32
MAX_DISTANCE = 128
M_HEADS = 4
M_HEAD_DIM = 128
PAGE_SIZE = 128
EPS = 1e-6
NEG = -1e30
LOG2E = math.log2(math.e)
ATT_WIDTH = N_HEADS * HEAD_DIM
M_WIDTH = M_HEADS * M_HEAD_DIM

LANES = 128
SUBLANES = 8
VMEM_LIMIT = 56 * 1024 * 1024

C_QA = 0
C_K = C_QA + ATT_WIDTH
C_V = C_K + LANES
C_QI = C_V + LANES
C_KI2 = C_QI + N_IDX_HEADS * IDX_DIM
C_MISC = C_KI2 + LANES
C_QM = C_MISC + LANES
C_KM = C_QM + M_WIDTH
C_VM = C_KM + M_WIDTH
C_OM = C_VM + M_WIDTH
N_PACK = C_OM + M_WIDTH
MISC_W = 0
MISC_I = 8
MISC_F = 12

HEAD_PERM = (0, 4, 1, 5, 2, 6, 3, 7)

TQ = 256
STRIP_BACK = 2
assert LANES * (STRIP_BACK - 1) >= MAX_DISTANCE
CKP = 256
VT_ROWS = LANES + 16
ML = 256

NT_DIMS = (((1,), (1,)), ((), ()))
TN_DIMS = (((0,), (0,)), ((), ()))


def _bucket_bounds():
    max_exact = N_BUCKETS // 2
    scale = (N_BUCKETS - max_exact) / math.log(MAX_DISTANCE / max_exact)

    def bucket(n, dt):
        if n < max_exact:
            return n
        val = np.log(np.asarray(max(n, 1), dt) / dt(max_exact)) * dt(scale)
        return min(max_exact + int(val), N_BUCKETS - 1)

    table = [bucket(n, np.float32) for n in range(MAX_DISTANCE + 2)]
    assert table == [bucket(n, np.float64) for n in range(MAX_DISTANCE + 2)]
    assert table[MAX_DISTANCE] == N_BUCKETS - 1
    return [next(d for d, b in enumerate(table) if b >= k) for k in range(N_BUCKETS)]


BUCKET_BOUNDS = _bucket_bounds()


def _cparams(n_axes):
    return pltpu.CompilerParams(dimension_semantics=("arbitrary",) * n_axes,
                                vmem_limit_bytes=VMEM_LIMIT)


def _const_spec(shape):
    nd = len(shape)
    return pl.BlockSpec(shape, lambda *_: (0,) * nd, pipeline_mode=pl.Buffered(1))


def _rms(x, g):
    return x * lax.rsqrt(jnp.mean(x * x, axis=-1, keepdims=True) + EPS) * g


def _inproj_mlstm(mm, misc_ref, qm_ref, km_ref, vm_ref, om_ref):
    misc_ref[...] = mm(C_MISC, LANES)
    qm_ref[...] = mm(C_QM, M_WIDTH).astype(qm_ref.dtype)
    km_ref[...] = (mm(C_KM, M_WIDTH) * (M_HEAD_DIM ** -0.5)).astype(km_ref.dtype)
    vm_ref[...] = mm(C_VM, M_WIDTH).astype(vm_ref.dtype)
    om_ref[...] = mm(C_OM, M_WIDTH)


def _inproj_rows_kernel(x_ref, g_ref, w_ref, misc_ref, qm_ref, km_ref, vm_ref, om_ref,
                        qa_ref, qi_ref, k_ref, v_ref, ki_ref):
    ub = _rms(x_ref[...], g_ref[...]).astype(BF16)
    mm = lambda c0, n: jnp.dot(ub, w_ref[:, c0:c0 + n], preferred_element_type=F32)
    _inproj_mlstm(mm, misc_ref, qm_ref, km_ref, vm_ref, om_ref)
    qa_ref[...] = mm(C_QA, ATT_WIDTH) * (HEAD_DIM ** -0.5)
    qi_ref[...] = mm(C_QI, N_IDX_HEADS * IDX_DIM)
    k_ref[...] = mm(C_K, LANES)
    v_ref[...] = mm(C_V, LANES)
    ki_ref[...] = mm(C_KI2, LANES)[:, :IDX_DIM]


R_K = 0
R_V = R_K + LANES
R_KI = R_V + LANES
R_QA = R_KI + LANES
R_QI = R_QA + ATT_WIDTH
R_MISC = R_QI + N_IDX_HEADS * IDX_DIM
N_TPACK = R_MISC + LANES


def _inproj_cols_kernel(x_ref, g_ref, w_ref, wt_ref, misc_ref, qm_ref, km_ref, vm_ref, om_ref,
                        kb_ref, kib_ref, kt_ref, vt_ref, kit_ref, vtb_ref, qat_ref, qit_ref,
                        misct_ref):
    ub = _rms(x_ref[...], g_ref[...]).astype(BF16)
    mm = lambda c0, n: jnp.dot(ub, w_ref[:, c0:c0 + n], preferred_element_type=F32)
    _inproj_mlstm(mm, misc_ref, qm_ref, km_ref, vm_ref, om_ref)
    kb_ref[...] = mm(C_K, LANES).astype(BF16)
    kib_ref[...] = mm(C_KI2, LANES)[:, :IDX_DIM].astype(BF16)

    def mt(r0, n):
        return lax.dot_general(wt_ref[r0:r0 + n, :], ub, NT_DIMS, preferred_element_type=F32)

    kt_ref[0] = mt(R_K, LANES)
    vt = mt(R_V, LANES)
    vt_ref[0] = vt
    ones = jnp.ones((VT_ROWS - LANES, CKP), BF16)
    for j in range(vtb_ref.shape[1]):
        vtb_ref[0, j] = jnp.concatenate([vt[:, j * CKP:(j + 1) * CKP].astype(BF16), ones], axis=0)
    kit_ref[0] = mt(R_KI, IDX_DIM)
    qat_ref[0] = (mt(R_QA, ATT_WIDTH) * (HEAD_DIM ** -0.5 * LOG2E)).astype(BF16)
    qit_ref[0] = mt(R_QI, N_IDX_HEADS * IDX_DIM).astype(BF16)
    misct_ref[0] = mt(R_MISC, LANES)


def _inproj(x2, g1, wp, wt, B, T, tm):
    R, D = x2.shape
    assert R == B * T and R % tm == 0
    mdt = F32 if wt is None else BF16
    row = lambda i: (i, 0)
    outs = [(LANES, F32), (M_WIDTH, mdt), (M_WIDTH, mdt), (M_WIDTH, mdt), (M_WIDTH, F32)]
    in_specs = [pl.BlockSpec((tm, D), row), _const_spec((1, D)), _const_spec((D, N_PACK))]
    if wt is None:
        kern, args = _inproj_rows_kernel, (x2, g1, wp)
        outs += [(ATT_WIDTH, F32), (N_IDX_HEADS * IDX_DIM, F32), (LANES, F32), (LANES, F32), (IDX_DIM, F32)]
    else:
        kern, args = _inproj_cols_kernel, (x2, g1, wp, wt)
        in_specs.append(_const_spec(wt.shape))
        outs += [(LANES, BF16), (IDX_DIM, BF16)]
    out_specs = [pl.BlockSpec((tm, w), row) for w, _ in outs]
    out_shape = [jax.ShapeDtypeStruct((R, w), dt) for w, dt in outs]
    if wt is not None:
        assert T % tm == 0 and tm % CKP == 0
        tpb, cpt = T // tm, tm // CKP
        cols = lambda i: (i // tpb, 0, i % tpb)
        for w, dt in ((LANES, F32), (LANES, F32), (IDX_DIM, F32)):
            out_specs.append(pl.BlockSpec((1, w, tm), cols))
            out_shape.append(jax.ShapeDtypeStruct((B, w, T), dt))
        out_specs.append(pl.BlockSpec((1, cpt, VT_ROWS, CKP), lambda i: (i // tpb, i % tpb, 0, 0)))
        out_shape.append(jax.ShapeDtypeStruct((B, T // CKP, VT_ROWS, CKP), BF16))
        for w, dt in ((ATT_WIDTH, BF16), (N_IDX_HEADS * IDX_DIM, BF16), (LANES, F32)):
            out_specs.append(pl.BlockSpec((1, w, tm), cols))
            out_shape.append(jax.ShapeDtypeStruct((B, w, T), dt))
    return pl.pallas_call(
        kern,
        grid=(R // tm,),
        in_specs=in_specs,
        out_specs=out_specs,
        out_shape=out_shape,
        compiler_params=_cparams(1),
        name="inproj",
    )(*args)


def _post_kernel(ff_chunk, final_norm, x_ref, a_ref, h_ref, woa_ref, woh_ref, g2_ref, wup_ref,
                 wdn_ref, gf_ref, y_ref):
    mix = jnp.dot(a_ref[...].astype(BF16), woa_ref[...], preferred_element_type=F32)
    mix = mix + jnp.dot(h_ref[...].astype(BF16), woh_ref[...], preferred_element_type=F32)
    hres = x_ref[...] + mix
    f = _rms(hres, g2_ref[...]).astype(BF16)
    acc = hres
    for c0 in range(0, wup_ref.shape[1], ff_chunk):
        up = jnp.dot(f, wup_ref[:, c0:c0 + ff_chunk], preferred_element_type=F32)
        r = jnp.maximum(up, 0.0)
        acc = acc + jnp.dot((r * r).astype(BF16), wdn_ref[c0:c0 + ff_chunk, :],
                            preferred_element_type=F32)
    y_ref[...] = _rms(acc, gf_ref[...]) if final_norm else acc


def _post(x2, attn, h, woa, woh, g2, wup, wdn, gf, final_norm, tm):
    R, D = x2.shape
    dff = wup.shape[1]
    assert R % tm == 0
    row = lambda i: (i, 0)
    return pl.pallas_call(
        functools.partial(_post_kernel, min(dff, 1024), final_norm),
        grid=(R // tm,),
        in_specs=[pl.BlockSpec((tm, D), row), pl.BlockSpec((tm, ATT_WIDTH), row),
                  pl.BlockSpec((tm, M_WIDTH), row), _const_spec(woa.shape), _const_spec(woh.shape),
                  _const_spec((1, D)), _const_spec(wup.shape), _const_spec(wdn.shape),
                  _const_spec((1, D))],
        out_specs=pl.BlockSpec((tm, D), row),
        out_shape=jax.ShapeDtypeStruct((R, D), F32),
        compiler_params=_cparams(1),
        name="post",
    )(x2, attn, h, woa, woh, g2, wup, wdn, gf)


def _log_sigmoid(x):
    return -(jnp.maximum(-x, 0.0) + jnp.log1p(jnp.exp(-jnp.abs(x))))


def _mlstm_kernel(nvalid, has_state, *refs):
    if has_state:
        (q_ref, k_ref, v_ref, o_ref, misc_ref, gb_ref, mn_ref, c0_ref, n0_ref, m0_ref,
         h_ref, c_ref, n_ref, m_ref, z_scr, rows_scr) = refs
    else:
        (q_ref, k_ref, v_ref, o_ref, misc_ref, gb_ref, mn_ref,
         h_ref, c_ref, n_ref, m_ref, z_scr, rows_scr) = refs
    nb, nc, L = z_scr.shape[0], z_scr.shape[1], z_scr.shape[2]
    c = pl.program_id(1)

    def padded(x, dt):
        x = x.astype(dt)
        if nvalid == L:
            return x
        return jnp.concatenate([x, jnp.zeros((L - nvalid, x.shape[1]), dt)], axis=0)

    r2 = lax.broadcasted_iota(I32, (L, L), 0)
    c2 = lax.broadcasted_iota(I32, (L, L), 1)
    tril = r2 >= c2

    @pl.when(c == 0)
    def _():
        if has_state:
            c_ref[...] = c0_ref[...]
            n_ref[...] = n0_ref[...]
            m_ref[...] = m0_ref[...]
        else:
            c_ref[...] = jnp.zeros_like(c_ref)
            n_ref[...] = jnp.zeros_like(n_ref)
            m_ref[...] = jnp.zeros_like(m_ref)
        assert MISC_F == MISC_I + M_HEADS and MISC_I % SUBLANES == 0
        row8 = lax.broadcasted_iota(I32, (SUBLANES, L), 0)
        tok8 = lax.broadcasted_iota(I32, (SUBLANES, L), 1)
        bias8 = jnp.concatenate([gb_ref[...]] * (L // LANES), axis=1)
        triu = (r2 <= c2).astype(F32)
        for s in range(nb):
            for cc in range(nc):
                misc_t = padded(misc_ref[s, cc * nvalid:(cc + 1) * nvalid, :], F32).T
                gx = misc_t[MISC_I:MISC_I + SUBLANES] + bias8
                gates = jnp.where(row8 >= M_HEADS, _log_sigmoid(gx), gx)
                if nvalid != L:
                    gates = jnp.where(tok8 < nvalid, gates, jnp.where(row8 >= M_HEADS, 0.0, NEG))
                cum = jnp.dot(gates, triu, precision=lax.Precision.HIGHEST,
                              preferred_element_type=F32)
                rows = jnp.where(row8 >= M_HEADS, cum, gates)
                rows_scr[s, cc] = rows
                z_scr[s, cc] = jnp.concatenate([rows, jnp.zeros((LANES - SUBLANES, L), F32)], axis=0).T

    loaded = []
    for s in range(nb):
        loaded.append((z_scr[s, c], rows_scr[s, c], padded(q_ref[s], BF16), padded(k_ref[s], BF16),
                       padded(v_ref[s], BF16), padded(o_ref[s], F32), m_ref[s], c_ref[s], n_ref[s]))
    chains = [(s, hd) for s in range(nb) for hd in range(M_HEADS)]

    def operands(s, hd):
        z, rows, qb, kb, vb, ob, m_all, c_all, n_all = loaded[s]
        sl = slice(hd * M_HEAD_DIM, (hd + 1) * M_HEAD_DIM)
        return dict(
            sl=sl, q=qb[:, sl], k=kb[:, sl], v=vb[:, sl], o=ob[:, sl],
            icol=z[:, hd:hd + 1], bcol=z[:, M_HEADS + hd:M_HEADS + hd + 1],
            irow=rows[hd:hd + 1, :], brow=rows[M_HEADS + hd:M_HEADS + hd + 1, :],
            m_prev=m_all[hd:hd + 1, 0:1], s_prev=c_all[hd], n_prev=n_all[hd:hd + 1, :])

    def lane_sum(x):
        ones = jnp.ones((x.shape[1], LANES), BF16)
        head = x.astype(BF16)
        tail = (x - head.astype(F32)).astype(BF16)
        return (jnp.dot(head, ones, preferred_element_type=F32)
                + jnp.dot(tail, ones, preferred_element_type=F32))

    st = [operands(s, hd) for s, hd in chains]
    for x in st:
        x["qk"] = lax.dot_general(x["q"], x["k"], NT_DIMS, preferred_element_type=F32)
        x["qs"] = jnp.dot(x["q"], x["s_prev"].astype(BF16), preferred_element_type=F32)
        x["qn"] = lane_sum(x["q"].astype(F32) * x["n_prev"])
    for x in st:
        g = x["bcol"] + x["m_prev"]
        dm = jnp.where(tril, x["bcol"] - x["brow"] + x["irow"], NEG)
        x["mt"] = jnp.maximum(g, jnp.max(dm, axis=1, keepdims=True))
        x["gw"] = jnp.exp(g - x["mt"])
        x["qk"] = x["qk"] * jnp.exp(dm - x["mt"])
        b_last = x["bcol"][L - 1:L, :]
        g_last = b_last + x["m_prev"]
        a = b_last - x["bcol"] + x["icol"]
        x["m_new"] = jnp.maximum(g_last, jnp.max(a, axis=0, keepdims=True))
        x["sw"] = jnp.exp(g_last - x["m_new"])
        x["ak"] = jnp.exp(a - x["m_new"]) * x["k"].astype(F32)
    for x in st:
        x["pv"] = jnp.dot(x["qk"].astype(BF16), x["v"], preferred_element_type=F32)
        x["qksum"] = lane_sum(x["qk"])
        x["kv"] = lax.dot_general(x["ak"].astype(BF16), x["v"], TN_DIMS, preferred_element_type=F32)
    stores = []
    for (s, hd), x in zip(chains, st):
        num = x["gw"] * x["qs"] + x["pv"]
        den = x["gw"] * x["qn"] + x["qksum"]
        hh = num / jnp.maximum(jnp.abs(den), jnp.exp(-x["mt"]))
        hh = hh * lax.rsqrt(lane_sum(hh * hh) * (1.0 / M_HEAD_DIM) + EPS)
        hh = hh * mn_ref[:, x["sl"]] * jax.nn.sigmoid(x["o"])
        c_new = x["sw"] * x["s_prev"] + x["kv"]
        n_new = x["sw"] * x["n_prev"] + jnp.sum(x["ak"], axis=0, keepdims=True)
        stores.append((s, hd, x["sl"], hh[:nvalid].astype(h_ref.dtype), c_new, n_new,
                       jnp.broadcast_to(x["m_new"], (1, LANES))))
    for s, hd, sl, h_new, c_new, n_new, m_new in stores:
        h_ref[s, :, sl] = h_new
        c_ref[s, hd] = c_new
        n_ref[s, hd:hd + 1, :] = n_new
        m_ref[s, hd:hd + 1, :] = m_new


def _mlstm(qm, km, vm, om, misc, gate_bias, mnorm, state, B, T, hdt):
    L = next((c for c in (ML, LANES) if T % c == 0), LANES)
    nvalid = L if T % L == 0 else T
    assert nvalid <= L and T % nvalid == 0 and nvalid % SUBLANES == 0
    nc = T // nvalid
    nb = next(n for n in ((2, 1) if nvalid == L else (4, 2, 1)) if B % n == 0)
    seq3 = lambda a: a.reshape(B, T, a.shape[-1])
    blk = lambda w: pl.BlockSpec((nb, nvalid, w), lambda b, c: (b, c, 0))
    st_specs = [pl.BlockSpec((nb, M_HEADS, M_HEAD_DIM, M_HEAD_DIM), lambda b, c: (b, 0, 0, 0)),
                pl.BlockSpec((nb, M_HEADS, M_HEAD_DIM), lambda b, c: (b, 0, 0)),
                pl.BlockSpec((nb, M_HEADS, LANES), lambda b, c: (b, 0, 0))]
    in_specs = [blk(M_WIDTH), blk(M_WIDTH), blk(M_WIDTH), blk(M_WIDTH),
                pl.BlockSpec((nb, T, LANES), lambda b, c: (b, 0, 0)),
                pl.BlockSpec((SUBLANES, LANES), lambda b, c: (0, 0)),
                pl.BlockSpec((1, M_WIDTH), lambda b, c: (0, 0))]
    args = [seq3(qm), seq3(km), seq3(vm), seq3(om), seq3(misc), gate_bias, mnorm]
    if state is not None:
        in_specs += st_specs
        args += list(state)
    h, c_new, n_new, m_new = pl.pallas_call(
        functools.partial(_mlstm_kernel, nvalid, state is not None),
        grid=(B // nb, nc),
        in_specs=in_specs,
        out_specs=[blk(M_WIDTH)] + st_specs,
        out_shape=[jax.ShapeDtypeStruct((B, T, M_WIDTH), hdt),
                   jax.ShapeDtypeStruct((B, M_HEADS, M_HEAD_DIM, M_HEAD_DIM), F32),
                   jax.ShapeDtypeStruct((B, M_HEADS, M_HEAD_DIM), F32),
                   jax.ShapeDtypeStruct((B, M_HEADS, LANES), F32)],
        scratch_shapes=[pltpu.VMEM((nb, nc, L, LANES), F32), pltpu.VMEM((nb, nc, SUBLANES, L), F32)],
        compiler_params=_cparams(2),
        name="mlstm",
    )(*args)
    return h.reshape(B * T, M_WIDTH), c_new, n_new, m_new


def _sortable_key(score):
    bits = lax.bitcast_convert_type(score, I32)
    return bits ^ (lax.shift_right_arithmetic(bits, 31) & 0x7FFFFFFF)


def _key_to_score(key):
    return lax.bitcast_convert_type(key ^ (lax.shift_right_arithmetic(key, 31) & 0x7FFFFFFF), F32)


def _build_bias_strip(strip_ref, rb_ref, off, key_axis, log2_relative=False):
    ntiles, tile = strip_ref.shape[1], strip_ref.shape[2:]
    i = lax.broadcasted_iota(I32, tile, 1 - key_axis)
    x = lax.broadcasted_iota(I32, tile, key_axis)

    def entry(b, h):
        if log2_relative:
            return (rb_ref[b, h] - rb_ref[N_BUCKETS - 1, h]) * LOG2E
        return rb_ref[b, h]

    for t in range(ntiles):
        dist = i + (off - LANES * t) - x
        for h in range(N_HEADS):
            val = jnp.full(tile, entry(0, h), F32)
            for b in range(1, N_BUCKETS):
                val = jnp.where(dist >= BUCKET_BOUNDS[b], entry(b, h), val)
            strip_ref[h, t] = val


def _counter(score_ref, nk, ck, key_axis):
    def count(pred):
        def body(c, acc):
            c0 = pl.multiple_of(c * ck, ck)
            sc = score_ref[:, pl.ds(c0, ck)] if key_axis == 1 else score_ref[pl.ds(c0, ck), :]
            idx = c0 + lax.broadcasted_iota(I32, sc.shape, key_axis)
            hit = jnp.where(pred(sc, idx), 1, 0)
            if key_axis == 1:
                for j in range(ck // LANES):
                    acc = acc + hit[:, j * LANES:(j + 1) * LANES]
                return acc
            return acc + jnp.sum(hit.reshape(ck // SUBLANES, SUBLANES, hit.shape[1]), axis=0)

        nq = score_ref.shape[1 - key_axis]
        acc0 = jnp.zeros((nq, LANES) if key_axis == 1 else (SUBLANES, nq), I32)
        return jnp.sum(lax.fori_loop(0, nk, body, acc0), axis=key_axis, keepdims=True)

    return count


IMIN = jnp.iinfo(jnp.int32).min


def _kth_largest_by_count(count, qshape, n_sel):
    def bit_step(i, key):
        cand = key + lax.shift_left(jnp.int32(1), 31 - i)
        cand_score = _key_to_score(cand)
        return jnp.where(count(lambda sc, idx: sc >= cand_score) >= n_sel, cand, key)

    return _key_to_score(lax.fori_loop(0, 32, bit_step, jnp.full(qshape, IMIN, I32)))


def _bit_planes(words):
    a = list(words)
    j, m = 16, 0x0000FFFF
    while j:
        k = 0
        while k < 32:
            t = (a[k] ^ lax.shift_right_logical(a[k + j], j)) & m
            a[k] = a[k] ^ t
            a[k + j] = a[k + j] ^ lax.shift_left(t, j)
            k = (k + j + 1) & ~j
        j >>= 1
        m = (m ^ (m << j)) & 0xFFFFFFFF
        m = m - (1 << 32) if m >= (1 << 31) else m
    return a[::-1]


def _kth_largest_by_planes(score_ref, planes_ref, nk, ck, n_sel):
    nc = planes_ref.shape[1]
    nq = score_ref.shape[1]
    assert ck == 32 * SUBLANES

    def pack_chunk(c, _):
        c0 = pl.multiple_of(c * ck, ck)
        u = _sortable_key(score_ref[pl.ds(c0, ck), :]) ^ IMIN
        u = u.reshape(32, SUBLANES, nq)
        for b, plane in enumerate(_bit_planes([u[v] for v in range(32)])):
            planes_ref[b, c] = plane
        return 0

    lax.fori_loop(0, nk, pack_chunk, 0)
    cand0 = tuple(jnp.where(c < nk, jnp.full((SUBLANES, nq), -1, I32), 0) for c in range(nc))
    return _plane_search(planes_ref, cand0, n_sel, 0)


def _kth_largest_by_planes_lanes(score_ref, planes_ref, n_sel):
    nq, ntiles = score_ref.shape[0], score_ref.shape[1] // LANES
    cand0 = []
    for g in range(planes_ref.shape[1]):
        real = min(32, ntiles - 32 * g)
        words = [_sortable_key(score_ref[:, (32 * g + v) * LANES:(32 * g + v + 1) * LANES]) ^ IMIN
                 if v < real else jnp.zeros((nq, LANES), I32) for v in range(32)]
        for b, plane in enumerate(_bit_planes(words)):
            planes_ref[b, g] = plane
        cand0.append(jnp.full((nq, LANES), -(1 << (32 - real)), I32))
    return _plane_search(planes_ref, tuple(cand0), n_sel, 1)


def _plane_search(planes_ref, cand0, n_sel, key_axis):
    def bit_step(i, carry):
        cand, n_above, thr_u = carry
        b = 31 - i
        ones = [m & planes_ref[b, c] for c, m in enumerate(cand)]
        pop = lax.population_count(ones[0])
        for o in ones[1:]:
            pop = pop + lax.population_count(o)
        tot = jnp.sum(pop, axis=key_axis, keepdims=True)
        take = n_above + tot >= n_sel
        cand = tuple(jnp.where(take, o, m ^ o) for o, m in zip(ones, cand))
        n_above = jnp.where(take, n_above, n_above + tot)
        thr_u = thr_u | jnp.where(take, lax.shift_left(jnp.int32(1), b), 0)
        return cand, n_above, thr_u

    zero = jnp.zeros_like(jnp.sum(cand0[0], axis=key_axis, keepdims=True))
    _, _, thr_u = lax.fori_loop(0, 32, bit_step, (cand0, zero, zero))
    return _key_to_score(thr_u ^ IMIN)


def _select_topk(thr, check, count, qshape, n_sel, idx_bits):
    imax = jnp.iinfo(jnp.int32).max

    def with_counts(t):
        return t, count(lambda sc, idx: sc > t), count(lambda sc, idx: sc >= t)

    thr, n_gt, n_ge = with_counts(thr)
    if check:
        good = jnp.min(jnp.where(n_gt < n_sel, jnp.where(n_ge >= n_sel, 1, 0), 0)) > 0
        thr, n_gt, n_ge = lax.cond(
            good, lambda _: (thr, n_gt, n_ge),
            lambda _: with_counts(_kth_largest_by_count(count, qshape, n_sel)), 0)
    need = n_sel - n_gt

    def tie_search(_):
        def idx_step(i, lo):
            cand = lo + lax.shift_left(jnp.int32(1), idx_bits - 1 - i)
            cnt = count(lambda sc, idx: jnp.where(sc == thr, idx, imax) < cand)
            return jnp.where(cnt < need, cand, lo)

        return lax.fori_loop(0, idx_bits, idx_step, jnp.zeros(qshape, I32))

    all_ties = jnp.full(qshape, imax, I32)
    jstar = lax.cond(jnp.max(n_ge) > n_sel, tie_search, lambda _: all_ties, 0)
    jstar = jnp.where(n_ge > n_sel, jstar, all_ties)
    return thr, jstar


def _valid_mask(scores, idx, thr, jstar, qpos):
    sel = jnp.where(scores > thr, 1, jnp.where(scores == thr, jnp.where(idx <= jstar, 1, 0), 0))
    return jnp.where(idx <= qpos, sel, 0) > 0


def _group_queries(qa, tq):
    lane = lax.broadcasted_iota(I32, (tq, LANES), 1)
    out = []
    for n in range(N_KV_HEADS):
        keep = (lane < HEAD_DIM) if n == 0 else (lane >= HEAD_DIM)
        tiles = [jnp.where(keep, qa[:, j * LANES:(j + 1) * LANES], jnp.zeros((), qa.dtype))
                 for j in range(GROUP)]
        out.append(jnp.concatenate(tiles, axis=0).astype(BF16))
    return out


def _write_attn(out_ref, carries, tq):
    lane = lax.broadcasted_iota(I32, (tq, LANES), 1)
    res = [acc / l for (_, l, acc) in carries]
    for j in range(GROUP):
        tile = jnp.where(lane < HEAD_DIM, res[0][j * tq:(j + 1) * tq], res[1][j * tq:(j + 1) * tq])
        out_ref[:, j * LANES:(j + 1) * LANES] = tile.astype(out_ref.dtype)


def _dsa_prompt_kernel(n_sel, idx_bits, rb_ref, qat_ref, qit_ref, misct_ref, ki_ref, k_ref, vt_ref,
                       out_ref, score_ref, planes_ref, strip_ref, sa_ref, sb_ref, m_ref, acc_ref):
    qb = pl.program_id(1)
    q0 = qb * TQ
    nk = (q0 + TQ + CKP - 1) // CKP
    tiles_per_chunk = CKP // LANES
    back_tiles = STRIP_BACK
    strip_off = LANES * back_tiles

    @pl.when((pl.program_id(0) == 0) & (qb == 0))
    def _():
        _build_bias_strip(strip_ref, rb_ref, strip_off, 0, log2_relative=True)
        planes_ref[...] = jnp.zeros(planes_ref.shape, I32)

    qpos = q0 + lax.broadcasted_iota(I32, (1, TQ), 1)

    qit = qit_ref[0]
    qstack = jnp.concatenate([qit[h * IDX_DIM:(h + 1) * IDX_DIM] for h in range(N_IDX_HEADS)],
                             axis=1)
    w = misct_ref[0, MISC_W:MISC_W + N_IDX_HEADS, :] * (N_IDX_HEADS ** -0.5 * IDX_DIM ** -0.5)

    def score_chunk(c, _):
        c0 = pl.multiple_of(c * CKP, CKP)
        d = jnp.dot(ki_ref[pl.ds(c0, CKP), :], qstack, preferred_element_type=F32)
        d = jnp.maximum(d, 0.0)
        s = jnp.zeros((CKP, TQ), F32)
        for h in range(N_IDX_HEADS):
            s = s + d[:, h * TQ:(h + 1) * TQ] * w[h:h + 1, :]
        idx = c0 + lax.broadcasted_iota(I32, (CKP, TQ), 0)
        score_ref[pl.ds(c0, CKP), :] = jnp.where(idx <= qpos, s, NEG)
        return 0

    lax.fori_loop(0, nk, score_chunk, 0)

    thr = _kth_largest_by_planes(score_ref, planes_ref, nk, CKP, n_sel)
    thr, jstar = _select_topk(thr, True, _counter(score_ref, nk, CKP, 0), (1, TQ), n_sel, idx_bits)

    qat = qat_ref[0]
    zeros = jnp.zeros((HEAD_DIM, TQ), qat.dtype)
    qgroups = []
    for n in range(N_KV_HEADS):
        tiles = []
        for g in range(GROUP):
            h = n * GROUP + g
            x = qat[h * HEAD_DIM:(h + 1) * HEAD_DIM]
            tiles.append(jnp.concatenate([x, zeros] if n == 0 else [zeros, x], axis=0))
        qgroups.append(jnp.concatenate(tiles, axis=1))

    def logits_into(s_ref, c):
        kc = k_ref[pl.ds(pl.multiple_of(c * CKP, CKP), CKP), :]
        for n in range(N_KV_HEADS):
            s_ref[n] = jnp.dot(kc, qgroups[n], preferred_element_type=F32)

    def attend_chunk(c, s_cur_ref, s_next_ref):
        logits_into(s_next_ref, jnp.minimum(c + 1, nk - 1))
        c0 = pl.multiple_of(c * CKP, CKP)
        idx = c0 + lax.broadcasted_iota(I32, (CKP, TQ), 0)
        valid = _valid_mask(score_ref[pl.ds(c0, CKP), :], idx, thr, jstar, qpos)
        vct = vt_ref[0, c]
        tiles = [jnp.maximum(back_tiles + j - (qb * (TQ // LANES) - c * tiles_per_chunk), 0)
                 for j in range(tiles_per_chunk)]
        for n in range(N_KV_HEADS):
            m_old = m_ref[n]
            parts = []
            for g in range(GROUP):
                bias = jnp.concatenate([strip_ref[n * GROUP + g, t] for t in tiles], axis=0)
                parts.append(jnp.where(valid, s_cur_ref[n, :, g * TQ:(g + 1) * TQ] + bias, NEG))
            sm = jnp.concatenate(parts, axis=1)
            m_new = jnp.maximum(m_old, jnp.max(sm, axis=0, keepdims=True))
            p = jnp.exp2((sm - m_new).astype(BF16))
            acc_ref[n] = jnp.exp2(m_old - m_new) * acc_ref[n] + jnp.dot(vct, p, preferred_element_type=F32)
            m_ref[n] = m_new

    m_ref[...] = jnp.full(m_ref.shape, NEG, F32)
    acc_ref[...] = jnp.zeros(acc_ref.shape, F32)
    logits_into(sa_ref, 0)

    def attend_pair(i, _):
        attend_chunk(2 * i, sa_ref, sb_ref)
        attend_chunk(2 * i + 1, sb_ref, sa_ref)
        return 0

    lax.fori_loop(0, nk // 2, attend_pair, 0)

    @pl.when(nk % 2 == 1)
    def _():
        attend_chunk(nk - 1, sa_ref, sb_ref)

    carries = [(None, acc_ref[n]) for n in range(N_KV_HEADS)]
    res = [acc[:LANES] / acc[LANES:LANES + 1] for (_, acc) in carries]
    row = lax.broadcasted_iota(I32, (LANES, TQ), 0)
    for j in range(GROUP):
        cols = slice(j * TQ, (j + 1) * TQ)
        tile_t = jnp.where(row < HEAD_DIM, res[0][:, cols], res[1][:, cols])
        out_ref[:, j * LANES:(j + 1) * LANES] = tile_t.T.astype(out_ref.dtype)


def _dsa_prompt(rel_bias, qat, qit, misct, kib, kb, vtb, B, T):
    assert T % CKP == 0 and T % TQ == 0
    nq = T // TQ
    n_sel = min(TOPK_MAX, T // 4)
    idx_bits = max(1, (T - 1).bit_length())
    qcols = lambda w: pl.BlockSpec((1, w, TQ), lambda b, q: (b, 0, q))
    seq = lambda w: pl.BlockSpec((T, w), lambda b, q: (b, 0))
    return pl.pallas_call(
        functools.partial(_dsa_prompt_kernel, n_sel, idx_bits),
        grid=(B, nq),
        in_specs=[pl.BlockSpec(memory_space=pltpu.SMEM), qcols(ATT_WIDTH), qcols(N_IDX_HEADS * IDX_DIM),
                  qcols(LANES), seq(IDX_DIM), seq(LANES),
                  pl.BlockSpec((1, T // CKP, VT_ROWS, CKP), lambda b, q: (b, 0, 0, 0))],
        out_specs=pl.BlockSpec((TQ, ATT_WIDTH), lambda b, q: (b * nq + q, 0)),
        out_shape=jax.ShapeDtypeStruct((B * T, ATT_WIDTH), BF16),
        scratch_shapes=[pltpu.VMEM((T, TQ), F32),
                        pltpu.VMEM((32, T // CKP, SUBLANES, TQ), I32),
                        pltpu.VMEM((N_HEADS, STRIP_BACK + max(TQ, CKP) // LANES, LANES, TQ), F32),
                        pltpu.VMEM((N_KV_HEADS, CKP, GROUP * TQ), F32),
                        pltpu.VMEM((N_KV_HEADS, CKP, GROUP * TQ), F32),
                        pltpu.VMEM((N_KV_HEADS, 1, GROUP * TQ), F32),
                        pltpu.VMEM((N_KV_HEADS, VT_ROWS, GROUP * TQ), F32)],
        compiler_params=_cparams(2),
        name="dsa_prompt",
    )(rel_bias, qat, qit, misct, kib, kb, vtb)


def _page_pipeline(pt_ref, n_pages, caches, bufs, sems):
    def copies(bb, sl, j):
        pid = pt_ref[bb, j]
        cols = pl.ds(pl.multiple_of(j * PAGE_SIZE, PAGE_SIZE), PAGE_SIZE)
        return [pltpu.make_async_copy(c.at[pid], buf.at[sl, :, cols], sems.at[sl, i])
                for i, (c, buf) in enumerate(zip(caches, bufs))]

    def start_all(bb, sl):
        def body(j, _):
            for cp in copies(bb, sl, j):
                cp.start()
            return 0
        lax.fori_loop(0, n_pages, body, 0)

    def wait_all(bb, sl):
        def body(j, _):
            for cp in copies(bb, sl, j):
                cp.wait()
            return 0
        lax.fori_loop(0, n_pages, body, 0)

    def step():
        b = pl.program_id(0)
        slot = b % 2

        @pl.when(b == 0)
        def _():
            start_all(0, 0)

        @pl.when(b + 1 < pl.num_programs(0))
        def _():
            start_all(b + 1, 1 - slot)

        wait_all(b, slot)
        return slot

    return step


def _pad_rows(x, rows):
    return jnp.concatenate([x, jnp.zeros((rows - x.shape[0], x.shape[1]), x.dtype)], axis=0)


def _sample_score_kernel(n_sel, idx_bits, n_pages, ts, ck, pt_ref, qi_ref, misc_ref, kin_ref,
                         ckit_hbm, score_ref, thr_ref, jst_ref, ki_buf, sems, planes_ref):
    b = pl.program_id(0)
    past = n_pages * PAGE_SIZE
    slot = _page_pipeline(pt_ref, n_pages, [ckit_hbm], [ki_buf], sems)()

    qi = qi_ref[...]
    qstack = jnp.concatenate([qi[:, h * IDX_DIM:(h + 1) * IDX_DIM] for h in range(N_IDX_HEADS)],
                             axis=0).astype(BF16)
    w = misc_ref[:, MISC_W:MISC_W + N_IDX_HEADS] * (N_IDX_HEADS ** -0.5 * IDX_DIM ** -0.5)
    d_past = jnp.dot(qstack, ki_buf[slot].astype(BF16), preferred_element_type=F32)
    d_own = lax.dot_general(qstack, _pad_rows(kin_ref[...], PAGE_SIZE).astype(BF16), NT_DIMS,
                            preferred_element_type=F32)
    d = jnp.maximum(jnp.concatenate([d_past, d_own], axis=1), 0.0)
    s = jnp.zeros((ts, past + PAGE_SIZE), F32)
    for h in range(N_IDX_HEADS):
        s = s + d[h * ts:(h + 1) * ts] * w[:, h:h + 1]
    idx = lax.broadcasted_iota(I32, s.shape, 1)
    qpos = past + lax.broadcasted_iota(I32, (ts, 1), 0)
    score_ref[pl.ds(pl.multiple_of(b * ts, ts), ts), :] = jnp.where(idx <= qpos, s, NEG)

    @pl.when(b == pl.num_programs(0) - 1)
    def _():
        count = _counter(score_ref, score_ref.shape[1] // ck, ck, 1)
        qshape = (score_ref.shape[0], 1)
        thr = _kth_largest_by_planes_lanes(score_ref, planes_ref, n_sel)
        thr, jstar = _select_topk(thr, True, count, qshape, n_sel, idx_bits)
        thr_ref[...] = jnp.broadcast_to(thr, thr_ref.shape)
        jst_ref[...] = jnp.broadcast_to(jstar, jst_ref.shape)


def _sample_attend_kernel(n_pages, ts, pt_ref, rb_ref, qa_ref, score_ref, thr_ref, jst_ref, kn_ref,
                          vn_ref, ckt_hbm, cvt_hbm, out_ref, k_buf, v_buf, sems, strip_ref):
    b = pl.program_id(0)
    past = n_pages * PAGE_SIZE
    rows = N_HEADS * ts

    @pl.when(b == 0)
    def _():
        _build_bias_strip(strip_ref, rb_ref, LANES * (strip_ref.shape[1] - 1), 1)

    slot = _page_pipeline(pt_ref, n_pages, [ckt_hbm, cvt_hbm], [k_buf, v_buf], sems)()

    q2 = jnp.concatenate(_group_queries(qa_ref[...], ts), axis=0)
    k_own = _pad_rows(kn_ref[...], PAGE_SIZE).astype(BF16)
    v_own = _pad_rows(vn_ref[...], PAGE_SIZE).astype(BF16)
    s_past = jnp.dot(q2, k_buf[slot].astype(BF16), preferred_element_type=F32)
    s_own = lax.dot_general(q2, k_own, NT_DIMS, preferred_element_type=F32)
    far = strip_ref[:, 0].reshape(rows, LANES)[:, 0:1]
    near = [strip_ref[:, t].reshape(rows, LANES) for t in (1, 2)]
    s = jnp.concatenate([s_past[:, :past - PAGE_SIZE] + far, s_past[:, past - PAGE_SIZE:] + near[0],
                         s_own + near[1]], axis=1)

    qpos = past + lax.broadcasted_iota(I32, (ts, 1), 0)
    scores = score_ref[...]
    valid = _valid_mask(scores, lax.broadcasted_iota(I32, scores.shape, 1), thr_ref[:, 0:1],
                        jst_ref[:, 0:1], qpos)
    s = jnp.where(valid[None], s.reshape(N_HEADS, ts, past + PAGE_SIZE), NEG).reshape(rows, -1)
    m = jnp.max(s, axis=1, keepdims=True)
    p = jnp.exp(s - m)
    l = jnp.sum(p, axis=1, keepdims=True)
    pb = p.astype(BF16)
    pv = lax.dot_general(pb[:, :past], v_buf[slot].astype(BF16), NT_DIMS, preferred_element_type=F32)
    pv = pv + jnp.dot(pb[:, past:], v_own, preferred_element_type=F32)
    half = GROUP * ts
    carries = [(None, l[n * half:(n + 1) * half], pv[n * half:(n + 1) * half]) for n in range(N_KV_HEADS)]
    _write_attn(out_ref, carries, ts)


def _dsa_sample(page_table, rel_bias, qa, qi, misc, ki_new, k_new, v_new, ckit, ckt, cvt, DB, ts):
    n_pages = page_table.shape[1]
    past = n_pages * PAGE_SIZE
    n_sel = min(TOPK_MAX, (past + ts) // 4)
    lpad = past + PAGE_SIZE
    idx_bits = max(1, (lpad - 1).bit_length())
    ck = LANES * math.gcd(lpad // LANES, 5)
    assert ts % SUBLANES == 0 and ts <= PAGE_SIZE and n_pages >= 1
    blk = lambda w: pl.BlockSpec((ts, w), lambda b, pt: (b, 0))
    whole = lambda w: pl.BlockSpec((DB * ts, w), lambda b, pt: (0, 0))
    hbm = pl.BlockSpec(memory_space=pl.ANY)
    keys, thr, jstar = pl.pallas_call(
        functools.partial(_sample_score_kernel, n_sel, idx_bits, n_pages, ts, ck),
        grid_spec=pltpu.PrefetchScalarGridSpec(
            num_scalar_prefetch=1,
            grid=(DB,),
            in_specs=[blk(N_IDX_HEADS * IDX_DIM), blk(LANES), blk(IDX_DIM), hbm],
            out_specs=[whole(lpad), whole(LANES), whole(LANES)],
            scratch_shapes=[pltpu.VMEM((2, IDX_DIM, past), F32), pltpu.SemaphoreType.DMA((2, 1)),
                            pltpu.VMEM((32, -(-lpad // (32 * LANES)), DB * ts, LANES), I32)]),
        out_shape=[jax.ShapeDtypeStruct((DB * ts, lpad), F32),
                   jax.ShapeDtypeStruct((DB * ts, LANES), F32),
                   jax.ShapeDtypeStruct((DB * ts, LANES), I32)],
        compiler_params=_cparams(1),
        name="sample_score",
    )(page_table, qi, misc, ki_new, ckit)
    return pl.pallas_call(
        functools.partial(_sample_attend_kernel, n_pages, ts),
        grid_spec=pltpu.PrefetchScalarGridSpec(
            num_scalar_prefetch=1,
            grid=(DB,),
            in_specs=[pl.BlockSpec(memory_space=pltpu.SMEM), blk(ATT_WIDTH), blk(lpad), blk(LANES),
                      blk(LANES), blk(LANES), blk(LANES), hbm, hbm],
            out_specs=blk(ATT_WIDTH),
            scratch_shapes=[pltpu.VMEM((2, LANES, past), F32), pltpu.VMEM((2, LANES, past), F32),
                            pltpu.SemaphoreType.DMA((2, 2)),
                            pltpu.VMEM((N_HEADS, 3, ts, LANES), F32)]),
        out_shape=jax.ShapeDtypeStruct((DB * ts, ATT_WIDTH), F32),
        compiler_params=_cparams(1),
        name="sample_attend",
    )(page_table, rel_bias, qa, keys, thr, jstar, k_new, v_new, ckt, cvt)


def _pack_layer_weights(w_in, b_i, b_f, w_out, w_up, w_down):
    D = w_in.shape[0]
    sizes = (ATT_WIDTH, N_KV_HEADS * HEAD_DIM, N_KV_HEADS * HEAD_DIM, N_IDX_HEADS * IDX_DIM, IDX_DIM,
             N_IDX_HEADS, M_WIDTH, M_WIDTH, M_WIDTH, M_WIDTH, M_HEADS, M_HEADS)
    assert w_in.shape[1] == sum(sizes)
    pts = np.cumsum((0,) + sizes)
    seg = [w_in[:, pts[i]:pts[i + 1]] for i in range(len(sizes))]
    qa, k, v, qi, ki, wi, qm, km, vm, om, im, fm = seg
    perm = np.asarray(HEAD_PERM)
    qa = qa.reshape(D, N_HEADS, HEAD_DIM)[:, perm].reshape(D, ATT_WIDTH)
    misc = jnp.concatenate([wi, im, fm, jnp.zeros((D, LANES - N_IDX_HEADS - 2 * M_HEADS), w_in.dtype)], axis=1)
    wp = jnp.concatenate([qa, k, v, qi, ki, ki, misc, qm, km, vm, om], axis=1).astype(BF16)
    assert wp.shape[1] == N_PACK
    gate_bias = jnp.broadcast_to(jnp.concatenate([b_i, b_f]).astype(F32)[:, None], (2 * M_HEADS, LANES))
    woa = w_out[:ATT_WIDTH].reshape(N_HEADS, HEAD_DIM, -1)[perm].reshape(ATT_WIDTH, -1).astype(BF16)
    woh = w_out[ATT_WIDTH:].astype(BF16)
    wt = jnp.concatenate([seg[1].T, seg[2].T, ki.T, jnp.zeros((LANES - IDX_DIM, D), w_in.dtype),
                          seg[0].T, qi.T, misc.T], axis=0).astype(BF16)
    assert wt.shape[0] == N_TPACK
    return wp, wt, gate_bias, woa, woh, w_up.astype(BF16), w_down.astype(BF16)


def _layer(x, packed, g1, g2, mnorm, rel_bias, gf, final_norm, past):
    wp, wt, gate_bias, woa, woh, wup, wdn = packed
    B, T, D = x.shape
    x2 = x.reshape(B * T, D)
    tm = math.gcd(T if past is None else B * T, 512)
    kv_w = N_KV_HEADS * HEAD_DIM
    if past is None:
        (misc, qm, km, vm, om, kb, kib, kt, vt, kit, vtb, qat, qit, misct) = _inproj(
            x2, g1.reshape(1, D), wp, wt, B, T, tm)
        attn = _dsa_prompt(rel_bias, qat, qit, misct, kib, kb, vtb, B, T)
        state = None
        k_new = kt.reshape(B, N_KV_HEADS, HEAD_DIM, T).transpose(0, 3, 1, 2)
        v_new = vt.reshape(B, N_KV_HEADS, HEAD_DIM, T).transpose(0, 3, 1, 2)
        ki_new = kit.transpose(0, 2, 1)
    else:
        (misc, qm, km, vm, om, qa, qi, k, v, ki) = _inproj(x2, g1.reshape(1, D), wp, None, B, T, tm)
        page_table, cache_k, cache_v, cache_kidx, c0, n0, m0 = past
        n_pool = cache_k.shape[0]
        ckt = cache_k.transpose(0, 2, 3, 1).reshape(n_pool, kv_w, PAGE_SIZE)
        cvt = cache_v.transpose(0, 2, 3, 1).reshape(n_pool, kv_w, PAGE_SIZE)
        ckit = cache_kidx.transpose(0, 2, 1)
        attn = _dsa_sample(page_table, rel_bias, qa, qi, misc, ki, k, v, ckit, ckt, cvt, B, T)
        state = (c0, n0, jnp.broadcast_to(m0[..., None], m0.shape + (LANES,)))
        k_new = k.reshape(B, T, N_KV_HEADS, HEAD_DIM)
        v_new = v.reshape(B, T, N_KV_HEADS, HEAD_DIM)
        ki_new = ki.reshape(B, T, IDX_DIM)
    h, c_new, n_new, m_new = _mlstm(qm, km, vm, om, misc, gate_bias, mnorm.reshape(1, M_WIDTH), state,
                                    B, T, BF16 if past is None else F32)
    y = _post(x2, attn, h, woa, woh, g2.reshape(1, D), wup, wdn, gf.reshape(1, D), final_norm, tm)
    return (y.reshape(B, T, D), k_new, v_new, ki_new, c_new, n_new, m_new[..., 0])


def kernel(x_prompt, x_sample, cache_k, cache_v, cache_kidx, page_table, state_C, state_n, state_m,
           w_in, b_igate, b_fgate, mlstm_norm, rel_bias, w_out, norm1, norm2, w_up, w_down, norm_f):
    depth = w_in.shape[0]
    xp, xs = x_prompt, x_sample
    outs_p, outs_s = [], []
    for l in range(depth):
        packed = _pack_layer_weights(w_in[l], b_igate[l], b_fgate[l], w_out[l], w_up[l], w_down[l])
        last = l == depth - 1
        common = (packed, norm1[l], norm2[l], mlstm_norm[l], rel_bias, norm_f, last)
        rp = _layer(xp, *common, None)
        rs = _layer(xs, *common, (page_table, cache_k[l], cache_v[l], cache_kidx[l],
                                  state_C[l], state_n[l], state_m[l]))
        xp, xs = rp[0], rs[0]
        outs_p.append(rp[1:])
        outs_s.append(rs[1:])
    stack = lambda outs, i: jnp.stack([o[i] for o in outs])
    return ((xp, xs) + tuple(stack(outs_p, i) for i in range(6))
            + tuple(stack(outs_s, i) for i in range(6)))
```

```python
import functools
import math

import numpy as np
import jax
import jax.numpy as jnp
from jax import lax
from jax.experimental import pallas as pl
from jax.experimental.pallas import tpu as pltpu

F32 = jnp.float32
BF16 = jnp.bfloat16
I32 = jnp.int32

N_HEADS = 8
HEAD_DIM = 64
N_KV_HEADS = 2
GROUP = N_HEADS // N_KV_HEADS
N_IDX_HEADS = 8
IDX_DIM = 64
TOPK_MAX = 256
N_BUCKETS = 32
MAX_DISTANCE = 128
M_HEADS = 4
M_HEAD_DIM = 128
PAGE_SIZE = 128
EPS = 1e-6
NEG = -1e30
LOG2E = math.log2(math.e)
ATT_WIDTH = N_HEADS * HEAD_DIM
M_WIDTH = M_HEADS * M_HEAD_DIM

LANES = 128
SUBLANES = 8
VMEM_LIMIT = 56 * 1024 * 1024

C_QA = 0
C_K = C_QA + ATT_WIDTH
C_V = C_K + LANES
C_QI = C_V + LANES
C_KI2 = C_QI + N_IDX_HEADS * IDX_DIM
C_MISC = C_KI2 + LANES
C_QM = C_MISC + LANES
C_KM = C_QM + M_WIDTH
C_VM = C_KM + M_WIDTH
C_OM = C_VM + M_WIDTH
N_PACK = C_OM + M_WIDTH
MISC_W = 0
MISC_I = 8
MISC_F = 12

HEAD_PERM = (0, 4, 1, 5, 2, 6, 3, 7)

TQ = 256
STRIP_BACK = 2
assert LANES * (STRIP_BACK - 1) >= MAX_DISTANCE
CKP = 256
VT_ROWS = LANES + 16
ML = 256

NT_DIMS = (((1,), (1,)), ((), ()))
TN_DIMS = (((0,), (0,)), ((), ()))


def _bucket_bounds():
    max_exact = N_BUCKETS // 2
    scale = (N_BUCKETS - max_exact) / math.log(MAX_DISTANCE / max_exact)

    def bucket(n, dt):
        if n < max_exact:
            return n
        val = np.log(np.asarray(max(n, 1), dt) / dt(max_exact)) * dt(scale)
        return min(max_exact + int(val), N_BUCKETS - 1)

    table = [bucket(n, np.float32) for n in range(MAX_DISTANCE + 2)]
    assert table == [bucket(n, np.float64) for n in range(MAX_DISTANCE + 2)]
    assert table[MAX_DISTANCE] == N_BUCKETS - 1
    return [next(d for d, b in enumerate(table) if b >= k) for k in range(N_BUCKETS)]


BUCKET_BOUNDS = _bucket_bounds()


def _cparams(n_axes):
    return pltpu.CompilerParams(dimension_semantics=("arbitrary",) * n_axes,
                                vmem_limit_bytes=VMEM_LIMIT)


def _const_spec(shape):
    nd = len(shape)
    return pl.BlockSpec(shape, lambda *_: (0,) * nd, pipeline_mode=pl.Buffered(1))


def _rms(x, g):
    return x * lax.rsqrt(jnp.mean(x * x, axis=-1, keepdims=True) + EPS) * g


def _inproj_mlstm(mm, misc_ref, qm_ref, km_ref, vm_ref, om_ref):
    misc_ref[...] = mm(C_MISC, LANES)
    qm_ref[...] = mm(C_QM, M_WIDTH).astype(qm_ref.dtype)
    km_ref[...] = (mm(C_KM, M_WIDTH) * (M_HEAD_DIM ** -0.5)).astype(km_ref.dtype)
    vm_ref[...] = mm(C_VM, M_WIDTH).astype(vm_ref.dtype)
    om_ref[...] = mm(C_OM, M_WIDTH)


def _inproj_rows_kernel(x_ref, g_ref, w_ref, misc_ref, qm_ref, km_ref, vm_ref, om_ref,
                        qa_ref, qi_ref, k_ref, v_ref, ki_ref):
    ub = _rms(x_ref[...], g_ref[...]).astype(BF16)
    mm = lambda c0, n: jnp.dot(ub, w_ref[:, c0:c0 + n], preferred_element_type=F32)
    _inproj_mlstm(mm, misc_ref, qm_ref, km_ref, vm_ref, om_ref)
    qa_ref[...] = mm(C_QA, ATT_WIDTH) * (HEAD_DIM ** -0.5)
    qi_ref[...] = mm(C_QI, N_IDX_HEADS * IDX_DIM)
    k_ref[...] = mm(C_K, LANES)
    v_ref[...] = mm(C_V, LANES)
    ki_ref[...] = mm(C_KI2, LANES)[:, :IDX_DIM]


R_K = 0
R_V = R_K + LANES
R_KI = R_V + LANES
R_QA = R_KI + LANES
R_QI = R_QA + ATT_WIDTH
R_MISC = R_QI + N_IDX_HEADS * IDX_DIM
N_TPACK = R_MISC + LANES


def _inproj_cols_kernel(x_ref, g_ref, w_ref, wt_ref, misc_ref, qm_ref, km_ref, vm_ref, om_ref,
                        kb_ref, kib_ref, kt_ref, vt_ref, kit_ref, vtb_ref, qat_ref, qit_ref,
                        misct_ref):
    ub = _rms(x_ref[...], g_ref[...]).astype(BF16)
    mm = lambda c0, n: jnp.dot(ub, w_ref[:, c0:c0 + n], preferred_element_type=F32)
    _inproj_mlstm(mm, misc_ref, qm_ref, km_ref, vm_ref, om_ref)
    kb_ref[...] = mm(C_K, LANES).astype(BF16)
    kib_ref[...] = mm(C_KI2, LANES)[:, :IDX_DIM].astype(BF16)

    def mt(r0, n):
        return lax.dot_general(wt_ref[r0:r0 + n, :], ub, NT_DIMS, preferred_element_type=F32)

    kt_ref[0] = mt(R_K, LANES)
    vt = mt(R_V, LANES)
    vt_ref[0] = vt
    ones = jnp.ones((VT_ROWS - LANES, CKP), BF16)
    for j in range(vtb_ref.shape[1]):
        vtb_ref[0, j] = jnp.concatenate([vt[:, j * CKP:(j + 1) * CKP].astype(BF16), ones], axis=0)
    kit_ref[0] = mt(R_KI, IDX_DIM)
    qat_ref[0] = (mt(R_QA, ATT_WIDTH) * (HEAD_DIM ** -0.5 * LOG2E)).astype(BF16)
    qit_ref[0] = mt(R_QI, N_IDX_HEADS * IDX_DIM).astype(BF16)
    misct_ref[0] = mt(R_MISC, LANES)


def _inproj(x2, g1, wp, wt, B, T, tm):
    R, D = x2.shape
    assert R == B * T and R % tm == 0
    mdt = F32 if wt is None else BF16
    row = lambda i: (i, 0)
    outs = [(LANES, F32), (M_WIDTH, mdt), (M_WIDTH, mdt), (M_WIDTH, mdt), (M_WIDTH, F32)]
    in_specs = [pl.BlockSpec((tm, D), row), _const_spec((1, D)), _const_spec((D, N_PACK))]
    if wt is None:
        kern, args = _inproj_rows_kernel, (x2, g1, wp)
        outs += [(ATT_WIDTH, F32), (N_IDX_HEADS * IDX_DIM, F32), (LANES, F32), (LANES, F32), (IDX_DIM, F32)]
    else:
        kern, args = _inproj_cols_kernel, (x2, g1, wp, wt)
        in_specs.append(_const_spec(wt.shape))
        outs += [(LANES, BF16), (IDX_DIM, BF16)]
    out_specs = [pl.BlockSpec((tm, w), row) for w, _ in outs]
    out_shape = [jax.ShapeDtypeStruct((R, w), dt) for w, dt in outs]
    if wt is not None:
        assert T % tm == 0 and tm % CKP == 0
        tpb, cpt = T // tm, tm // CKP
        cols = lambda i: (i // tpb, 0, i % tpb)
        for w, dt in ((LANES, F32), (LANES, F32), (IDX_DIM, F32)):
            out_specs.append(pl.BlockSpec((1, w, tm), cols))
            out_shape.append(jax.ShapeDtypeStruct((B, w, T), dt))
        out_specs.append(pl.BlockSpec((1, cpt, VT_ROWS, CKP), lambda i: (i // tpb, i % tpb, 0, 0)))
        out_shape.append(jax.ShapeDtypeStruct((B, T // CKP, VT_ROWS, CKP), BF16))
        for w, dt in ((ATT_WIDTH, BF16), (N_IDX_HEADS * IDX_DIM, BF16), (LANES, F32)):
            out_specs.append(pl.BlockSpec((1, w, tm), cols))
            out_shape.append(jax.ShapeDtypeStruct((B, w, T), dt))
    return pl.pallas_call(
        kern,
        grid=(R // tm,),
        in_specs=in_specs,
        out_specs=out_specs,
        out_shape=out_shape,
        compiler_params=_cparams(1),
        name="inproj",
    )(*args)


def _post_kernel(ff_chunk, final_norm, x_ref, a_ref, h_ref, woa_ref, woh_ref, g2_ref, wup_ref,
                 wdn_ref, gf_ref, y_ref):
    mix = jnp.dot(a_ref[...].astype(BF16), woa_ref[...], preferred_element_type=F32)
    mix = mix + jnp.dot(h_ref[...].astype(BF16), woh_ref[...], preferred_element_type=F32)
    hres = x_ref[...] + mix
    f = _rms(hres, g2_ref[...]).astype(BF16)
    acc = hres
    for c0 in range(0, wup_ref.shape[1], ff_chunk):
        up = jnp.dot(f, wup_ref[:, c0:c0 + ff_chunk], preferred_element_type=F32)
        r = jnp.maximum(up, 0.0)
        acc = acc + jnp.dot((r * r).astype(BF16), wdn_ref[c0:c0 + ff_chunk, :],
                            preferred_element_type=F32)
    y_ref[...] = _rms(acc, gf_ref[...]) if final_norm else acc


def _post(x2, attn, h, woa, woh, g2, wup, wdn, gf, final_norm, tm):
    R, D = x2.shape
    dff = wup.shape[1]
    assert R % tm == 0
    row = lambda i: (i, 0)
    return pl.pallas_call(
        functools.partial(_post_kernel, min(dff, 1024), final_norm),
        grid=(R // tm,),
        in_specs=[pl.BlockSpec((tm, D), row), pl.BlockSpec((tm, ATT_WIDTH), row),
                  pl.BlockSpec((tm, M_WIDTH), row), _const_spec(woa.shape), _const_spec(woh.shape),
                  _const_spec((1, D)), _const_spec(wup.shape), _const_spec(wdn.shape),
                  _const_spec((1, D))],
        out_specs=pl.BlockSpec((tm, D), row),
        out_shape=jax.ShapeDtypeStruct((R, D), F32),
        compiler_params=_cparams(1),
        name="post",
    )(x2, attn, h, woa, woh, g2, wup, wdn, gf)


def _log_sigmoid(x):
    return -(jnp.maximum(-x, 0.0) + jnp.log1p(jnp.exp(-jnp.abs(x))))


def _mlstm_kernel(nvalid, has_state, *refs):
    if has_state:
        (q_ref, k_ref, v_ref, o_ref, misc_ref, gb_ref, mn_ref, c0_ref, n0_ref, m0_ref,
         h_ref, c_ref, n_ref, m_ref, z_scr, rows_scr) = refs
    else:
        (q_ref, k_ref, v_ref, o_ref, misc_ref, gb_ref, mn_ref,
         h_ref, c_ref, n_ref, m_ref, z_scr, rows_scr) = refs
    nb, nc, L = z_scr.shape[0], z_scr.shape[1], z_scr.shape[2]
    c = pl.program_id(1)

    def padded(x, dt):
        x = x.astype(dt)
        if nvalid == L:
            return x
        return jnp.concatenate([x, jnp.zeros((L - nvalid, x.shape[1]), dt)], axis=0)

    r2 = lax.broadcasted_iota(I32, (L, L), 0)
    c2 = lax.broadcasted_iota(I32, (L, L), 1)
    tril = r2 >= c2

    @pl.when(c == 0)
    def _():
        if has_state:
            c_ref[...] = c0_ref[...]
            n_ref[...] = n0_ref[...]
            m_ref[...] = m0_ref[...]
        else:
            c_ref[...] = jnp.zeros_like(c_ref)
            n_ref[...] = jnp.zeros_like(n_ref)
            m_ref[...] = jnp.zeros_like(m_ref)
        assert MISC_F == MISC_I + M_HEADS and MISC_I % SUBLANES == 0
        row8 = lax.broadcasted_iota(I32, (SUBLANES, L), 0)
        tok8 = lax.broadcasted_iota(I32, (SUBLANES, L), 1)
        bias8 = jnp.concatenate([gb_ref[...]] * (L // LANES), axis=1)
        triu = (r2 <= c2).astype(F32)
        for s in range(nb):
            for cc in range(nc):
                misc_t = padded(misc_ref[s, cc * nvalid:(cc + 1) * nvalid, :], F32).T
                gx = misc_t[MISC_I:MISC_I + SUBLANES] + bias8
                gates = jnp.where(row8 >= M_HEADS, _log_sigmoid(gx), gx)
                if nvalid != L:
                    gates = jnp.where(tok8 < nvalid, gates, jnp.where(row8 >= M_HEADS, 0.0, NEG))
                cum = jnp.dot(gates, triu, precision=lax.Precision.HIGHEST,
                              preferred_element_type=F32)
                rows = jnp.where(row8 >= M_HEADS, cum, gates)
                rows_scr[s, cc] = rows
                z_scr[s, cc] = jnp.concatenate([rows, jnp.zeros((LANES - SUBLANES, L), F32)], axis=0).T

    loaded = []
    for s in range(nb):
        loaded.append((z_scr[s, c], rows_scr[s, c], padded(q_ref[s], BF16), padded(k_ref[s], BF16),
                       padded(v_ref[s], BF16), padded(o_ref[s], F32), m_ref[s], c_ref[s], n_ref[s]))
    chains = [(s, hd) for s in range(nb) for hd in range(M_HEADS)]

    def operands(s, hd):
        z, rows, qb, kb, vb, ob, m_all, c_all, n_all = loaded[s]
        sl = slice(hd * M_HEAD_DIM, (hd + 1) * M_HEAD_DIM)
        return dict(
            sl=sl, q=qb[:, sl], k=kb[:, sl], v=vb[:, sl], o=ob[:, sl],
            icol=z[:, hd:hd + 1], bcol=z[:, M_HEADS + hd:M_HEADS + hd + 1],
            irow=rows[hd:hd + 1, :], brow=rows[M_HEADS + hd:M_HEADS + hd + 1, :],
            m_prev=m_all[hd:hd + 1, 0:1], s_prev=c_all[hd], n_prev=n_all[hd:hd + 1, :])

    def lane_sum(x):
        ones = jnp.ones((x.shape[1], LANES), BF16)
        head = x.astype(BF16)
        tail = (x - head.astype(F32)).astype(BF16)
        return (jnp.dot(head, ones, preferred_element_type=F32)
                + jnp.dot(tail, ones, preferred_element_type=F32))

    st = [operands(s, hd) for s, hd in chains]
    for x in st:
        x["qk"] = lax.dot_general(x["q"], x["k"], NT_DIMS, preferred_element_type=F32)
        x["qs"] = jnp.dot(x["q"], x["s_prev"].astype(BF16), preferred_element_type=F32)
        x["qn"] = lane_sum(x["q"].astype(F32) * x["n_prev"])
    for x in st:
        g = x["bcol"] + x["m_prev"]
        dm = jnp.where(tril, x["bcol"] - x["brow"] + x["irow"], NEG)
        x["mt"] = jnp.maximum(g, jnp.max(dm, axis=1, keepdims=True))
        x["gw"] = jnp.exp(g - x["mt"])
        x["qk"] = x["qk"] * jnp.exp(dm - x["mt"])
        b_last = x["bcol"][L - 1:L, :]
        g_last = b_last + x["m_prev"]
        a = b_last - x["bcol"] + x["icol"]
        x["m_new"] = jnp.maximum(g_last, jnp.max(a, axis=0, keepdims=True))
        x["sw"] = jnp.exp(g_last - x["m_new"])
        x["ak"] = jnp.exp(a - x["m_new"]) * x["k"].astype(F32)
    for x in st:
        x["pv"] = jnp.dot(x["qk"].astype(BF16), x["v"], preferred_element_type=F32)
        x["qksum"] = lane_sum(x["qk"])
        x["kv"] = lax.dot_general(x["ak"].astype(BF16), x["v"], TN_DIMS, preferred_element_type=F32)
    stores = []
    for (s, hd), x in zip(chains, st):
        num = x["gw"] * x["qs"] + x["pv"]
        den = x["gw"] * x["qn"] + x["qksum"]
        hh = num / jnp.maximum(jnp.abs(den), jnp.exp(-x["mt"]))
        hh = hh * lax.rsqrt(lane_sum(hh * hh) * (1.0 / M_HEAD_DIM) + EPS)
        hh = hh * mn_ref[:, x["sl"]] * jax.nn.sigmoid(x["o"])
        c_new = x["sw"] * x["s_prev"] + x["kv"]
        n_new = x["sw"] * x["n_prev"] + jnp.sum(x["ak"], axis=0, keepdims=True)
        stores.append((s, hd, x["sl"], hh[:nvalid].astype(h_ref.dtype), c_new, n_new,
                       jnp.broadcast_to(x["m_new"], (1, LANES))))
    for s, hd, sl, h_new, c_new, n_new, m_new in stores:
        h_ref[s, :, sl] = h_new
        c_ref[s, hd] = c_new
        n_ref[s, hd:hd + 1, :] = n_new
        m_ref[s, hd:hd + 1, :] = m_new


def _mlstm(qm, km, vm, om, misc, gate_bias, mnorm, state, B, T, hdt):
    L = next((c for c in (ML, LANES) if T % c == 0), LANES)
    nvalid = L if T % L == 0 else T
    assert nvalid <= L and T % nvalid == 0 and nvalid % SUBLANES == 0
    nc = T // nvalid
    nb = next(n for n in ((2, 1) if nvalid == L else (4, 2, 1)) if B % n == 0)
    seq3 = lambda a: a.reshape(B, T, a.shape[-1])
    blk = lambda w: pl.BlockSpec((nb, nvalid, w), lambda b, c: (b, c, 0))
    st_specs = [pl.BlockSpec((nb, M_HEADS, M_HEAD_DIM, M_HEAD_DIM), lambda b, c: (b, 0, 0, 0)),
                pl.BlockSpec((nb, M_HEADS, M_HEAD_DIM), lambda b, c: (b, 0, 0)),
                pl.BlockSpec((nb, M_HEADS, LANES), lambda b, c: (b, 0, 0))]
    in_specs = [blk(M_WIDTH), blk(M_WIDTH), blk(M_WIDTH), blk(M_WIDTH),
                pl.BlockSpec((nb, T, LANES), lambda b, c: (b, 0, 0)),
                pl.BlockSpec((SUBLANES, LANES), lambda b, c: (0, 0)),
                pl.BlockSpec((1, M_WIDTH), lambda b, c: (0, 0))]
    args = [seq3(qm), seq3(km), seq3(vm), seq3(om), seq3(misc), gate_bias, mnorm]
    if state is not None:
        in_specs += st_specs
        args += list(state)
    h, c_new, n_new, m_new = pl.pallas_call(
        functools.partial(_mlstm_kernel, nvalid, state is not None),
        grid=(B // nb, nc),
        in_specs=in_specs,
        out_specs=[blk(M_WIDTH)] + st_specs,
        out_shape=[jax.ShapeDtypeStruct((B, T, M_WIDTH), hdt),
                   jax.ShapeDtypeStruct((B, M_HEADS, M_HEAD_DIM, M_HEAD_DIM), F32),
                   jax.ShapeDtypeStruct((B, M_HEADS, M_HEAD_DIM), F32),
                   jax.ShapeDtypeStruct((B, M_HEADS, LANES), F32)],
        scratch_shapes=[pltpu.VMEM((nb, nc, L, LANES), F32), pltpu.VMEM((nb, nc, SUBLANES, L), F32)],
        compiler_params=_cparams(2),
        name="mlstm",
    )(*args)
    return h.reshape(B * T, M_WIDTH), c_new, n_new, m_new


def _sortable_key(score):
    bits = lax.bitcast_convert_type(score, I32)
    return bits ^ (lax.shift_right_arithmetic(bits, 31) & 0x7FFFFFFF)


def _key_to_score(key):
    return lax.bitcast_convert_type(key ^ (lax.shift_right_arithmetic(key, 31) & 0x7FFFFFFF), F32)


def _build_bias_strip(strip_ref, rb_ref, off, key_axis, log2_relative=False):
    ntiles, tile = strip_ref.shape[1], strip_ref.shape[2:]
    i = lax.broadcasted_iota(I32, tile, 1 - key_axis)
    x = lax.broadcasted_iota(I32, tile, key_axis)

    def entry(b, h):
        if log2_relative:
            return (rb_ref[b, h] - rb_ref[N_BUCKETS - 1, h]) * LOG2E
        return rb_ref[b, h]

    for t in range(ntiles):
        dist = i + (off - LANES * t) - x
        for h in range(N_HEADS):
            val = jnp.full(tile, entry(0, h), F32)
            for b in range(1, N_BUCKETS):
                val = jnp.where(dist >= BUCKET_BOUNDS[b], entry(b, h), val)
            strip_ref[h, t] = val


def _counter(score_ref, nk, ck, key_axis):
    def count(pred):
        def body(c, acc):
            c0 = pl.multiple_of(c * ck, ck)
            sc = score_ref[:, pl.ds(c0, ck)] if key_axis == 1 else score_ref[pl.ds(c0, ck), :]
            idx = c0 + lax.broadcasted_iota(I32, sc.shape, key_axis)
            hit = jnp.where(pred(sc, idx), 1, 0)
            if key_axis == 1:
                for j in range(ck // LANES):
                    acc = acc + hit[:, j * LANES:(j + 1) * LANES]
                return acc
            return acc + jnp.sum(hit.reshape(ck // SUBLANES, SUBLANES, hit.shape[1]), axis=0)

        nq = score_ref.shape[1 - key_axis]
        acc0 = jnp.zeros((nq, LANES) if key_axis == 1 else (SUBLANES, nq), I32)
        return jnp.sum(lax.fori_loop(0, nk, body, acc0), axis=key_axis, keepdims=True)

    return count


IMIN = jnp.iinfo(jnp.int32).min


def _kth_largest_by_count(count, qshape, n_sel):
    def bit_step(i, key):
        cand = key + lax.shift_left(jnp.int32(1), 31 - i)
        cand_score = _key_to_score(cand)
        return jnp.where(count(lambda sc, idx: sc >= cand_score) >= n_sel, cand, key)

    return _key_to_score(lax.fori_loop(0, 32, bit_step, jnp.full(qshape, IMIN, I32)))


def _bit_planes(words):
    a = list(words)
    j, m = 16, 0x0000FFFF
    while j:
        k = 0
        while k < 32:
            t = (a[k] ^ lax.shift_right_logical(a[k + j], j)) & m
            a[k] = a[k] ^ t
            a[k + j] = a[k + j] ^ lax.shift_left(t, j)
            k = (k + j + 1) & ~j
        j >>= 1
        m = (m ^ (m << j)) & 0xFFFFFFFF
        m = m - (1 << 32) if m >= (1 << 31) else m
    return a[::-1]


def _kth_largest_by_planes(score_ref, planes_ref, nk, ck, n_sel):
    nc = planes_ref.shape[1]
    nq = score_ref.shape[1]
    assert ck == 32 * SUBLANES

    def pack_chunk(c, _):
        c0 = pl.multiple_of(c * ck, ck)
        u = _sortable_key(score_ref[pl.ds(c0, ck), :]) ^ IMIN
        u = u.reshape(32, SUBLANES, nq)
        for b, plane in enumerate(_bit_planes([u[v] for v in range(32)])):
            planes_ref[b, c] = plane
        return 0

    lax.fori_loop(0, nk, pack_chunk, 0)
    cand0 = tuple(jnp.where(c < nk, jnp.full((SUBLANES, nq), -1, I32), 0) for c in range(nc))
    return _plane_search(planes_ref, cand0, n_sel, 0)


def _kth_largest_by_planes_lanes(score_ref, planes_ref, n_sel):
    nq, ntiles = score_ref.shape[0], score_ref.shape[1] // LANES
    cand0 = []
    for g in range(planes_ref.shape[1]):
        real = min(32, ntiles - 32 * g)
        words = [_sortable_key(score_ref[:, (32 * g + v) * LANES:(32 * g + v + 1) * LANES]) ^ IMIN
                 if v < real else jnp.zeros((nq, LANES), I32) for v in range(32)]
        for b, plane in enumerate(_bit_planes(words)):
            planes_ref[b, g] = plane
        cand0.append(jnp.full((nq, LANES), -(1 << (32 - real)), I32))
    return _plane_search(planes_ref, tuple(cand0), n_sel, 1)


def _plane_search(planes_ref, cand0, n_sel, key_axis):
    def bit_step(i, carry):
        cand, n_above, thr_u = carry
        b = 31 - i
        ones = [m & planes_ref[b, c] for c, m in enumerate(cand)]
        pop = lax.population_count(ones[0])
        for o in ones[1:]:
            pop = pop + lax.population_count(o)
        tot = jnp.sum(pop, axis=key_axis, keepdims=True)
        take = n_above + tot >= n_sel
        cand = tuple(jnp.where(take, o, m ^ o) for o, m in zip(ones, cand))
        n_above = jnp.where(take, n_above, n_above + tot)
        thr_u = thr_u | jnp.where(take, lax.shift_left(jnp.int32(1), b), 0)
        return cand, n_above, thr_u

    zero = jnp.zeros_like(jnp.sum(cand0[0], axis=key_axis, keepdims=True))
    _, _, thr_u = lax.fori_loop(0, 32, bit_step, (cand0, zero, zero))
    return _key_to_score(thr_u ^ IMIN)


def _select_topk(thr, check, count, qshape, n_sel, idx_bits):
    imax = jnp.iinfo(jnp.int32).max

    def with_counts(t):
        return t, count(lambda sc, idx: sc > t), count(lambda sc, idx: sc >= t)

    thr, n_gt, n_ge = with_counts(thr)
    if check:
        good = jnp.min(jnp.where(n_gt < n_sel, jnp.where(n_ge >= n_sel, 1, 0), 0)) > 0
        thr, n_gt, n_ge = lax.cond(
            good, lambda _: (thr, n_gt, n_ge),
            lambda _: with_counts(_kth_largest_by_count(count, qshape, n_sel)), 0)
    need = n_sel - n_gt

    def tie_search(_):
        def idx_step(i, lo):
            cand = lo + lax.shift_left(jnp.int32(1), idx_bits - 1 - i)
            cnt = count(lambda sc, idx: jnp.where(sc == thr, idx, imax) < cand)
            return jnp.where(cnt < need, cand, lo)

        return lax.fori_loop(0, idx_bits, idx_step, jnp.zeros(qshape, I32))

    all_ties = jnp.full(qshape, imax, I32)
    jstar = lax.cond(jnp.max(n_ge) > n_sel, tie_search, lambda _: all_ties, 0)
    jstar = jnp.where(n_ge > n_sel, jstar, all_ties)
    return thr, jstar


def _valid_mask(scores, idx, thr, jstar, qpos):
    sel = jnp.where(scores > thr, 1, jnp.where(scores == thr, jnp.where(idx <= jstar, 1, 0), 0))
    return jnp.where(idx <= qpos, sel, 0) > 0


def _group_queries(qa, tq):
    lane = lax.broadcasted_iota(I32, (tq, LANES), 1)
    out = []
    for n in range(N_KV_HEADS):
        keep = (lane < HEAD_DIM) if n == 0 else (lane >= HEAD_DIM)
        tiles = [jnp.where(keep, qa[:, j * LANES:(j + 1) * LANES], jnp.zeros((), qa.dtype))
                 for j in range(GROUP)]
        out.append(jnp.concatenate(tiles, axis=0).astype(BF16))
    return out


def _write_attn(out_ref, carries, tq):
    lane = lax.broadcasted_iota(I32, (tq, LANES), 1)
    res = [acc / l for (_, l, acc) in carries]
    for j in range(GROUP):
        tile = jnp.where(lane < HEAD_DIM, res[0][j * tq:(j + 1) * tq], res[1][j * tq:(j + 1) * tq])
        out_ref[:, j * LANES:(j + 1) * LANES] = tile.astype(out_ref.dtype)


def _dsa_prompt_kernel(n_sel, idx_bits, rb_ref, qat_ref, qit_ref, misct_ref, ki_ref, k_ref, vt_ref,
                       out_ref, score_ref, planes_ref, strip_ref, sa_ref, sb_ref, m_ref, acc_ref):
    qb = pl.program_id(1)
    q0 = qb * TQ
    nk = (q0 + TQ + CKP - 1) // CKP
    tiles_per_chunk = CKP // LANES
    back_tiles = STRIP_BACK
    strip_off = LANES * back_tiles

    @pl.when((pl.program_id(0) == 0) & (qb == 0))
    def _():
        _build_bias_strip(strip_ref, rb_ref, strip_off, 0, log2_relative=True)
        planes_ref[...] = jnp.zeros(planes_ref.shape, I32)

    qpos = q0 + lax.broadcasted_iota(I32, (1, TQ), 1)

    qit = qit_ref[0]
    qstack = jnp.concatenate([qit[h * IDX_DIM:(h + 1) * IDX_DIM] for h in range(N_IDX_HEADS)],
                             axis=1)
    w = misct_ref[0, MISC_W:MISC_W + N_IDX_HEADS, :] * (N_IDX_HEADS ** -0.5 * IDX_DIM ** -0.5)

    halves = sa_ref.shape[0]
    hph = N_IDX_HEADS // halves
    assert hph * TQ == sa_ref.shape[2]

    def dots_into(d_ref, c):
        kic = ki_ref[pl.ds(pl.multiple_of(c * CKP, CKP), CKP), :]
        for j in range(halves):
            d_ref[j] = jnp.dot(kic, qstack[:, j * hph * TQ:(j + 1) * hph * TQ], preferred_element_type=F32)

    def score_chunk(c, d_cur_ref, d_next_ref):
        if d_next_ref is not None:
            dots_into(d_next_ref, jnp.minimum(c + 1, nk - 1))
        s = jnp.zeros((CKP, TQ), F32)
        for h in range(N_IDX_HEADS):
            d = d_cur_ref[h // hph, :, (h % hph) * TQ:(h % hph + 1) * TQ]
            s = s + jnp.maximum(d, 0.0) * w[h:h + 1, :]
        c0 = pl.multiple_of(c * CKP, CKP)
        idx = c0 + lax.broadcasted_iota(I32, (CKP, TQ), 0)
        score_ref[pl.ds(c0, CKP), :] = jnp.where(idx <= qpos, s, NEG)

    dots_into(sa_ref, 0)

    def score_pair(i, _):
        score_chunk(2 * i, sa_ref, sb_ref)
        score_chunk(2 * i + 1, sb_ref, sa_ref)
        return 0

    lax.fori_loop(0, nk // 2, score_pair, 0)

    @pl.when(nk % 2 == 1)
    def _():
        score_chunk(nk - 1, sa_ref, None)

    thr = _kth_largest_by_planes(score_ref, planes_ref, nk, CKP, n_sel)
    thr, jstar = _select_topk(thr, True, _counter(score_ref, nk, CKP, 0), (1, TQ), n_sel, idx_bits)

    qat = qat_ref[0]
    zeros = jnp.zeros((HEAD_DIM, TQ), qat.dtype)
    qgroups = []
    for n in range(N_KV_HEADS):
        tiles = []
        for g in range(GROUP):
            h = n * GROUP + g
            x = qat[h * HEAD_DIM:(h + 1) * HEAD_DIM]
            tiles.append(jnp.concatenate([x, zeros] if n == 0 else [zeros, x], axis=0))
        qgroups.append(jnp.concatenate(tiles, axis=1))

    def logits_into(s_ref, c):
        kc = k_ref[pl.ds(pl.multiple_of(c * CKP, CKP), CKP), :]
        for n in range(N_KV_HEADS):
            s_ref[n] = jnp.dot(kc, qgroups[n], preferred_element_type=F32)

    def attend_chunk(c, s_cur_ref, s_next_ref):
        if s_next_ref is not None:
            logits_into(s_next_ref, jnp.minimum(c + 1, nk - 1))
        c0 = pl.multiple_of(c * CKP, CKP)
        idx = c0 + lax.broadcasted_iota(I32, (CKP, TQ), 0)
        valid = _valid_mask(score_ref[pl.ds(c0, CKP), :], idx, thr, jstar, qpos)
        vct = vt_ref[0, c]
        tiles = [jnp.maximum(back_tiles + j - (qb * (TQ // LANES) - c * tiles_per_chunk), 0)
                 for j in range(tiles_per_chunk)]
        for n in range(N_KV_HEADS):
            m_old = m_ref[n]
            parts = []
            for g in range(GROUP):
                bias = jnp.concatenate([strip_ref[n * GROUP + g, t] for t in tiles], axis=0)
                parts.append(jnp.where(valid, s_cur_ref[n, :, g * TQ:(g + 1) * TQ] + bias, NEG))
            sm = jnp.concatenate(parts, axis=1)
            m_new = jnp.maximum(m_old, jnp.max(sm, axis=0, keepdims=True))
            p = jnp.exp2((sm - m_new).astype(BF16))
            acc_ref[n] = jnp.exp2(m_old - m_new) * acc_ref[n] + jnp.dot(vct, p, preferred_element_type=F32)
            m_ref[n] = m_new

    m_ref[...] = jnp.full(m_ref.shape, NEG, F32)
    acc_ref[...] = jnp.zeros(acc_ref.shape, F32)
    logits_into(sa_ref, 0)

    def attend_pair(i, _):
        attend_chunk(2 * i, sa_ref, sb_ref)
        attend_chunk(2 * i + 1, sb_ref, sa_ref)
        return 0

    lax.fori_loop(0, nk // 2, attend_pair, 0)

    @pl.when(nk % 2 == 1)
    def _():
        attend_chunk(nk - 1, sa_ref, None)

    carries = [(None, acc_ref[n]) for n in range(N_KV_HEADS)]
    res = [acc[:LANES] / acc[LANES:LANES + 1] for (_, acc) in carries]
    row = lax.broadcasted_iota(I32, (LANES, TQ), 0)
    for j in range(GROUP):
        cols = slice(j * TQ, (j + 1) * TQ)
        tile_t = jnp.where(row < HEAD_DIM, res[0][:, cols], res[1][:, cols])
        out_ref[:, j * LANES:(j + 1) * LANES] = tile_t.T.astype(out_ref.dtype)


def _dsa_prompt(rel_bias, qat, qit, misct, kib, kb, vtb, B, T):
    assert T % CKP == 0 and T % TQ == 0
    nq = T // TQ
    n_sel = min(TOPK_MAX, T // 4)
    idx_bits = max(1, (T - 1).bit_length())
    qcols = lambda w: pl.BlockSpec((1, w, TQ), lambda b, q: (b, 0, q))
    seq = lambda w: pl.BlockSpec((T, w), lambda b, q: (b, 0))
    return pl.pallas_call(
        functools.partial(_dsa_prompt_kernel, n_sel, idx_bits),
        grid=(B, nq),
        in_specs=[pl.BlockSpec(memory_space=pltpu.SMEM), qcols(ATT_WIDTH), qcols(N_IDX_HEADS * IDX_DIM),
                  qcols(LANES), seq(IDX_DIM), seq(LANES),
                  pl.BlockSpec((1, T // CKP, VT_ROWS, CKP), lambda b, q: (b, 0, 0, 0))],
        out_specs=pl.BlockSpec((TQ, ATT_WIDTH), lambda b, q: (b * nq + q, 0)),
        out_shape=jax.ShapeDtypeStruct((B * T, ATT_WIDTH), BF16),
        scratch_shapes=[pltpu.VMEM((T, TQ), F32),
                        pltpu.VMEM((32, T // CKP, SUBLANES, TQ), I32),
                        pltpu.VMEM((N_HEADS, STRIP_BACK + max(TQ, CKP) // LANES, LANES, TQ), F32),
                        pltpu.VMEM((N_KV_HEADS, CKP, GROUP * TQ), F32),
                        pltpu.VMEM((N_KV_HEADS, CKP, GROUP * TQ), F32),
                        pltpu.VMEM((N_KV_HEADS, 1, GROUP * TQ), F32),
                        pltpu.VMEM((N_KV_HEADS, VT_ROWS, GROUP * TQ), F32)],
        compiler_params=_cparams(2),
        name="dsa_prompt",
    )(rel_bias, qat, qit, misct, kib, kb, vtb)


def _page_pipeline(pt_ref, n_pages, caches, bufs, sems):
    def copies(bb, sl, j):
        pid = pt_ref[bb, j]
        cols = pl.ds(pl.multiple_of(j * PAGE_SIZE, PAGE_SIZE), PAGE_SIZE)
        return [pltpu.make_async_copy(c.at[pid], buf.at[sl, :, cols], sems.at[sl, i])
                for i, (c, buf) in enumerate(zip(caches, bufs))]

    def start_all(bb, sl):
        def body(j, _):
            for cp in copies(bb, sl, j):
                cp.start()
            return 0
        lax.fori_loop(0, n_pages, body, 0)

    def wait_all(bb, sl):
        def body(j, _):
            for cp in copies(bb, sl, j):
                cp.wait()
            return 0
        lax.fori_loop(0, n_pages, body, 0)

    def step():
        b = pl.program_id(0)
        slot = b % 2

        @pl.when(b == 0)
        def _():
            start_all(0, 0)

        @pl.when(b + 1 < pl.num_programs(0))
        def _():
            start_all(b + 1, 1 - slot)

        wait_all(b, slot)
        return slot

    return step


def _pad_rows(x, rows):
    return jnp.concatenate([x, jnp.zeros((rows - x.shape[0], x.shape[1]), x.dtype)], axis=0)


def _sample_score_kernel(n_sel, idx_bits, n_pages, ts, ck, pt_ref, qi_ref, misc_ref, kin_ref,
                         ckit_hbm, score_ref, thr_ref, jst_ref, ki_buf, sems, planes_ref):
    b = pl.program_id(0)
    past = n_pages * PAGE_SIZE
    slot = _page_pipeline(pt_ref, n_pages, [ckit_hbm], [ki_buf], sems)()

    qi = qi_ref[...]
    qstack = jnp.concatenate([qi[:, h * IDX_DIM:(h + 1) * IDX_DIM] for h in range(N_IDX_HEADS)],
                             axis=0).astype(BF16)
    w = misc_ref[:, MISC_W:MISC_W + N_IDX_HEADS] * (N_IDX_HEADS ** -0.5 * IDX_DIM ** -0.5)
    d_past = jnp.dot(qstack, ki_buf[slot].astype(BF16), preferred_element_type=F32)
    d_own = lax.dot_general(qstack, _pad_rows(kin_ref[...], PAGE_SIZE).astype(BF16), NT_DIMS,
                            preferred_element_type=F32)
    d = jnp.maximum(jnp.concatenate([d_past, d_own], axis=1), 0.0)
    s = jnp.zeros((ts, past + PAGE_SIZE), F32)
    for h in range(N_IDX_HEADS):
        s = s + d[h * ts:(h + 1) * ts] * w[:, h:h + 1]
    idx = lax.broadcasted_iota(I32, s.shape, 1)
    qpos = past + lax.broadcasted_iota(I32, (ts, 1), 0)
    score_ref[pl.ds(pl.multiple_of(b * ts, ts), ts), :] = jnp.where(idx <= qpos, s, NEG)

    @pl.when(b == pl.num_programs(0) - 1)
    def _():
        count = _counter(score_ref, score_ref.shape[1] // ck, ck, 1)
        qshape = (score_ref.shape[0], 1)
        thr = _kth_largest_by_planes_lanes(score_ref, planes_ref, n_sel)
        thr, jstar = _select_topk(thr, True, count, qshape, n_sel, idx_bits)
        thr_ref[...] = jnp.broadcast_to(thr, thr_ref.shape)
        jst_ref[...] = jnp.broadcast_to(jstar, jst_ref.shape)


def _sample_attend_kernel(n_pages, ts, pt_ref, rb_ref, qa_ref, score_ref, thr_ref, jst_ref, kn_ref,
                          vn_ref, ckt_hbm, cvt_hbm, out_ref, k_buf, v_buf, sems, strip_ref):
    b = pl.program_id(0)
    past = n_pages * PAGE_SIZE
    rows = N_HEADS * ts

    @pl.when(b == 0)
    def _():
        _build_bias_strip(strip_ref, rb_ref, LANES * (strip_ref.shape[1] - 1), 1)

    slot = _page_pipeline(pt_ref, n_pages, [ckt_hbm, cvt_hbm], [k_buf, v_buf], sems)()

    q2 = jnp.concatenate(_group_queries(qa_ref[...], ts), axis=0)
    k_own = _pad_rows(kn_ref[...], PAGE_SIZE).astype(BF16)
    v_own = _pad_rows(vn_ref[...], PAGE_SIZE).astype(BF16)
    s_past = jnp.dot(q2, k_buf[slot].astype(BF16), preferred_element_type=F32)
    s_own = lax.dot_general(q2, k_own, NT_DIMS, preferred_element_type=F32)
    far = strip_ref[:, 0].reshape(rows, LANES)[:, 0:1]
    near = [strip_ref[:, t].reshape(rows, LANES) for t in (1, 2)]
    s = jnp.concatenate([s_past[:, :past - PAGE_SIZE] + far, s_past[:, past - PAGE_SIZE:] + near[0],
                         s_own + near[1]], axis=1)

    qpos = past + lax.broadcasted_iota(I32, (ts, 1), 0)
    scores = score_ref[...]
    valid = _valid_mask(scores, lax.broadcasted_iota(I32, scores.shape, 1), thr_ref[:, 0:1],
                        jst_ref[:, 0:1], qpos)
    s = jnp.where(valid[None], s.reshape(N_HEADS, ts, past + PAGE_SIZE), NEG).reshape(rows, -1)
    m = jnp.max(s, axis=1, keepdims=True)
    p = jnp.exp(s - m)
    l = jnp.sum(p, axis=1, keepdims=True)
    pb = p.astype(BF16)
    pv = lax.dot_general(pb[:, :past], v_buf[slot].astype(BF16), NT_DIMS, preferred_element_type=F32)
    pv = pv + jnp.dot(pb[:, past:], v_own, preferred_element_type=F32)
    half = GROUP * ts
    carries = [(None, l[n * half:(n + 1) * half], pv[n * half:(n + 1) * half]) for n in range(N_KV_HEADS)]
    _write_attn(out_ref, carries, ts)


def _dsa_sample(page_table, rel_bias, qa, qi, misc, ki_new, k_new, v_new, ckit, ckt, cvt, DB, ts):
    n_pages = page_table.shape[1]
    past = n_pages * PAGE_SIZE
    n_sel = min(TOPK_MAX, (past + ts) // 4)
    lpad = past + PAGE_SIZE
    idx_bits = max(1, (lpad - 1).bit_length())
    ck = LANES * math.gcd(lpad // LANES, 5)
    assert ts % SUBLANES == 0 and ts <= PAGE_SIZE and n_pages >= 1
    blk = lambda w: pl.BlockSpec((ts, w), lambda b, pt: (b, 0))
    whole = lambda w: pl.BlockSpec((DB * ts, w), lambda b, pt: (0, 0))
    hbm = pl.BlockSpec(memory_space=pl.ANY)
    keys, thr, jstar = pl.pallas_call(
        functools.partial(_sample_score_kernel, n_sel, idx_bits, n_pages, ts, ck),
        grid_spec=pltpu.PrefetchScalarGridSpec(
            num_scalar_prefetch=1,
            grid=(DB,),
            in_specs=[blk(N_IDX_HEADS * IDX_DIM), blk(LANES), blk(IDX_DIM), hbm],
            out_specs=[whole(lpad), whole(LANES), whole(LANES)],
            scratch_shapes=[pltpu.VMEM((2, IDX_DIM, past), F32), pltpu.SemaphoreType.DMA((2, 1)),
                            pltpu.VMEM((32, -(-lpad // (32 * LANES)), DB * ts, LANES), I32)]),
        out_shape=[jax.ShapeDtypeStruct((DB * ts, lpad), F32),
                   jax.ShapeDtypeStruct((DB * ts, LANES), F32),
                   jax.ShapeDtypeStruct((DB * ts, LANES), I32)],
        compiler_params=_cparams(1),
        name="sample_score",
    )(page_table, qi, misc, ki_new, ckit)
    return pl.pallas_call(
        functools.partial(_sample_attend_kernel, n_pages, ts),
        grid_spec=pltpu.PrefetchScalarGridSpec(
            num_scalar_prefetch=1,
            grid=(DB,),
            in_specs=[pl.BlockSpec(memory_space=pltpu.SMEM), blk(ATT_WIDTH), blk(lpad), blk(LANES),
                      blk(LANES), blk(LANES), blk(LANES), hbm, hbm],
            out_specs=blk(ATT_WIDTH),
            scratch_shapes=[pltpu.VMEM((2, LANES, past), F32), pltpu.VMEM((2, LANES, past), F32),
                            pltpu.SemaphoreType.DMA((2, 2)),
                            pltpu.VMEM((N_HEADS, 3, ts, LANES), F32)]),
        out_shape=jax.ShapeDtypeStruct((DB * ts, ATT_WIDTH), F32),
        compiler_params=_cparams(1),
        name="sample_attend",
    )(page_table, rel_bias, qa, keys, thr, jstar, k_new, v_new, ckt, cvt)


def _pack_layer_weights(w_in, b_i, b_f, w_out, w_up, w_down):
    D = w_in.shape[0]
    sizes = (ATT_WIDTH, N_KV_HEADS * HEAD_DIM, N_KV_HEADS * HEAD_DIM, N_IDX_HEADS * IDX_DIM, IDX_DIM,
             N_IDX_HEADS, M_WIDTH, M_WIDTH, M_WIDTH, M_WIDTH, M_HEADS, M_HEADS)
    assert w_in.shape[1] == sum(sizes)
    pts = np.cumsum((0,) + sizes)
    seg = [w_in[:, pts[i]:pts[i + 1]] for i in range(len(sizes))]
    qa, k, v, qi, ki, wi, qm, km, vm, om, im, fm = seg
    perm = np.asarray(HEAD_PERM)
    qa = qa.reshape(D, N_HEADS, HEAD_DIM)[:, perm].reshape(D, ATT_WIDTH)
    misc = jnp.concatenate([wi, im, fm, jnp.zeros((D, LANES - N_IDX_HEADS - 2 * M_HEADS), w_in.dtype)], axis=1)
    wp = jnp.concatenate([qa, k, v, qi, ki, ki, misc, qm, km, vm, om], axis=1).astype(BF16)
    assert wp.shape[1] == N_PACK
    gate_bias = jnp.broadcast_to(jnp.concatenate([b_i, b_f]).astype(F32)[:, None], (2 * M_HEADS, LANES))
    woa = w_out[:ATT_WIDTH].reshape(N_HEADS, HEAD_DIM, -1)[perm].reshape(ATT_WIDTH, -1).astype(BF16)
    woh = w_out[ATT_WIDTH:].astype(BF16)
    wt = jnp.concatenate([seg[1].T, seg[2].T, ki.T, jnp.zeros((LANES - IDX_DIM, D), w_in.dtype),
                          seg[0].T, qi.T, misc.T], axis=0).astype(BF16)
    assert wt.shape[0] == N_TPACK
    return wp, wt, gate_bias, woa, woh, w_up.astype(BF16), w_down.astype(BF16)


def _layer(x, packed, g1, g2, mnorm, rel_bias, gf, final_norm, past):
    wp, wt, gate_bias, woa, woh, wup, wdn = packed
    B, T, D = x.shape
    x2 = x.reshape(B * T, D)
    tm = math.gcd(T if past is None else B * T, 512)
    kv_w = N_KV_HEADS * HEAD_DIM
    if past is None:
        (misc, qm, km, vm, om, kb, kib, kt, vt, kit, vtb, qat, qit, misct) = _inproj(
            x2, g1.reshape(1, D), wp, wt, B, T, tm)
        attn = _dsa_prompt(rel_bias, qat, qit, misct, kib, kb, vtb, B, T)
        state = None
        k_new = kt.reshape(B, N_KV_HEADS, HEAD_DIM, T).transpose(0, 3, 1, 2)
        v_new = vt.reshape(B, N_KV_HEADS, HEAD_DIM, T).transpose(0, 3, 1, 2)
        ki_new = kit.transpose(0, 2, 1)
    else:
        (misc, qm, km, vm, om, qa, qi, k, v, ki) = _inproj(x2, g1.reshape(1, D), wp, None, B, T, tm)
        page_table, cache_k, cache_v, cache_kidx, c0, n0, m0 = past
        n_pool = cache_k.shape[0]
        ckt = cache_k.transpose(0, 2, 3, 1).reshape(n_pool, kv_w, PAGE_SIZE)
        cvt = cache_v.transpose(0, 2, 3, 1).reshape(n_pool, kv_w, PAGE_SIZE)
        ckit = cache_kidx.transpose(0, 2, 1)
        attn = _dsa_sample(page_table, rel_bias, qa, qi, misc, ki, k, v, ckit, ckt, cvt, B, T)
        state = (c0, n0, jnp.broadcast_to(m0[..., None], m0.shape + (LANES,)))
        k_new = k.reshape(B, T, N_KV_HEADS, HEAD_DIM)
        v_new = v.reshape(B, T, N_KV_HEADS, HEAD_DIM)
        ki_new = ki.reshape(B, T, IDX_DIM)
    h, c_new, n_new, m_new = _mlstm(qm, km, vm, om, misc, gate_bias, mnorm.reshape(1, M_WIDTH), state,
                                    B, T, BF16 if past is None else F32)
    y = _post(x2, attn, h, woa, woh, g2.reshape(1, D), wup, wdn, gf.reshape(1, D), final_norm, tm)
    return (y.reshape(B, T, D), k_new, v_new, ki_new, c_new, n_new, m_new[..., 0])


def kernel(x_prompt, x_sample, cache_k, cache_v, cache_kidx, page_table, state_C, state_n, state_m,
           w_in, b_igate, b_fgate, mlstm_norm, rel_bias, w_out, norm1, norm2, w_up, w_down, norm_f):
    depth = w_in.shape[0]
    xp, xs = x_prompt, x_sample
    outs_p, outs_s = [], []
    for l in range(depth):
        packed = _pack_layer_weights(w_in[l], b_igate[l], b_fgate[l], w_out[l], w_up[l], w_down[l])
        last = l == depth - 1
        common = (packed, norm1[l], norm2[l], mlstm_norm[l], rel_bias, norm_f, last)
        rp = _layer(xp, *common, None)
        rs = _layer(xs, *common, (page_table, cache_k[l], cache_v[l], cache_kidx[l],
                                  state_C[l], state_n[l], state_m[l]))
        xp, xs = rp[0], rs[0]
        outs_p.append(rp[1:])
        outs_s.append(rs[1:])
    stack = lambda outs, i: jnp.stack([o[i] for o in outs])
    return ((xp, xs) + tuple(stack(outs_p, i) for i in range(6))
            + tuple(stack(outs_s, i) for i in range(6)))
```

```python
import functools
import math

import numpy as np
import jax
import jax.numpy as jnp
from jax import lax
from jax.experimental import pallas as pl
from jax.experimental.pallas import tpu as pltpu

F32 = jnp.float32
BF16 = jnp.bfloat16
I32 = jnp.int32

N_HEADS = 8
HEAD_DIM = 64
N_KV_HEADS = 2
GROUP = N_HEADS // N_KV_HEADS
N_IDX_HEADS = 8
IDX_DIM = 64
TOPK_MAX = 256
N_BUCKETS = 32
MAX_DISTANCE = 128
M_HEADS = 4
M_HEAD_DIM = 128
PAGE_SIZE = 128
EPS = 1e-6
NEG = -1e30
LOG2E = math.log2(math.e)
ATT_WIDTH = N_HEADS * HEAD_DIM
M_WIDTH = M_HEADS * M_HEAD_DIM

LANES = 128
SUBLANES = 8
VMEM_LIMIT = 56 * 1024 * 1024

C_QA = 0
C_V = C_QA + ATT_WIDTH
C_QI = C_V + LANES
C_K = C_QI + N_IDX_HEADS * IDX_DIM
C_KI2 = C_K + LANES
C_MISC = C_KI2 + LANES
C_QM = C_MISC + LANES
C_KM = C_QM + M_WIDTH
C_VM = C_KM + M_WIDTH
C_OM = C_VM + M_WIDTH
N_PACK = C_OM + M_WIDTH
MISC_W = 0
MISC_I = 8
MISC_F = 12

HEAD_PERM = (0, 4, 1, 5, 2, 6, 3, 7)

TQ = 256
STRIP_BACK = 2
assert LANES * (STRIP_BACK - 1) >= MAX_DISTANCE
CKP = 256
VT_ROWS = LANES + 16
ML = 256

NT_DIMS = (((1,), (1,)), ((), ()))
TN_DIMS = (((0,), (0,)), ((), ()))


def _bucket_bounds():
    max_exact = N_BUCKETS // 2
    scale = (N_BUCKETS - max_exact) / math.log(MAX_DISTANCE / max_exact)

    def bucket(n, dt):
        if n < max_exact:
            return n
        val = np.log(np.asarray(max(n, 1), dt) / dt(max_exact)) * dt(scale)
        return min(max_exact + int(val), N_BUCKETS - 1)

    table = [bucket(n, np.float32) for n in range(MAX_DISTANCE + 2)]
    assert table == [bucket(n, np.float64) for n in range(MAX_DISTANCE + 2)]
    assert table[MAX_DISTANCE] == N_BUCKETS - 1
    return [next(d for d, b in enumerate(table) if b >= k) for k in range(N_BUCKETS)]


BUCKET_BOUNDS = _bucket_bounds()


def _cparams(n_axes):
    return pltpu.CompilerParams(dimension_semantics=("arbitrary",) * n_axes,
                                vmem_limit_bytes=VMEM_LIMIT)


def _const_spec(shape):
    nd = len(shape)
    return pl.BlockSpec(shape, lambda *_: (0,) * nd, pipeline_mode=pl.Buffered(1))


def _rms(x, g):
    return x * lax.rsqrt(jnp.mean(x * x, axis=-1, keepdims=True) + EPS) * g


def _inproj_mlstm(mm, misc_ref, qm_ref, km_ref, vm_ref, om_ref):
    misc_ref[...] = mm(C_MISC, LANES)
    qm_ref[...] = mm(C_QM, M_WIDTH).astype(qm_ref.dtype)
    km_ref[...] = (mm(C_KM, M_WIDTH) * (M_HEAD_DIM ** -0.5)).astype(km_ref.dtype)
    vm_ref[...] = mm(C_VM, M_WIDTH).astype(vm_ref.dtype)
    om_ref[...] = mm(C_OM, M_WIDTH)


def _inproj_rows_kernel(x_ref, g_ref, w_ref, misc_ref, qm_ref, km_ref, vm_ref, om_ref,
                        qa_ref, qi_ref, k_ref, v_ref, ki_ref):
    ub = _rms(x_ref[...], g_ref[...]).astype(BF16)
    z = jnp.dot(ub, w_ref[...], preferred_element_type=F32)
    mm = lambda c0, n: z[:, c0:c0 + n]
    _inproj_mlstm(mm, misc_ref, qm_ref, km_ref, vm_ref, om_ref)
    qa_ref[...] = mm(C_QA, ATT_WIDTH) * (HEAD_DIM ** -0.5)
    qi_ref[...] = mm(C_QI, N_IDX_HEADS * IDX_DIM)
    k_ref[...] = mm(C_K, LANES)
    v_ref[...] = mm(C_V, LANES)
    ki_ref[...] = mm(C_KI2, LANES)[:, :IDX_DIM]


R_K = 0
R_V = R_K + LANES
R_KI = R_V + LANES
R_QA = R_KI + LANES
R_QI = R_QA + ATT_WIDTH
R_MISC = R_QI + N_IDX_HEADS * IDX_DIM
N_TPACK = R_MISC + LANES


def _inproj_cols_kernel(x_ref, g_ref, w_ref, wt_ref, misc_ref, qm_ref, km_ref, vm_ref, om_ref,
                        kb_ref, kib_ref, kt_ref, vt_ref, kit_ref, vtb_ref, qat_ref, qit_ref,
                        misct_ref):
    ub = _rms(x_ref[...], g_ref[...]).astype(BF16)
    z = jnp.dot(ub, w_ref[:, C_K:], preferred_element_type=F32)
    mm = lambda c0, n: z[:, c0 - C_K:c0 - C_K + n]
    _inproj_mlstm(mm, misc_ref, qm_ref, km_ref, vm_ref, om_ref)
    kb_ref[...] = mm(C_K, LANES).astype(BF16)
    kib_ref[...] = mm(C_KI2, LANES)[:, :IDX_DIM].astype(BF16)

    zt = lax.dot_general(wt_ref[...], ub, NT_DIMS, preferred_element_type=F32)
    mt = lambda r0, n: zt[r0:r0 + n]

    kt_ref[0] = mt(R_K, LANES)
    vt = mt(R_V, LANES)
    vt_ref[0] = vt
    ones = jnp.ones((VT_ROWS - LANES, CKP), BF16)
    for j in range(vtb_ref.shape[1]):
        vtb_ref[0, j] = jnp.concatenate([vt[:, j * CKP:(j + 1) * CKP].astype(BF16), ones], axis=0)
    kit_ref[0] = mt(R_KI, IDX_DIM)
    qat_ref[0] = (mt(R_QA, ATT_WIDTH) * (HEAD_DIM ** -0.5 * LOG2E)).astype(BF16)
    qit_ref[0] = mt(R_QI, N_IDX_HEADS * IDX_DIM).astype(BF16)
    misct_ref[0] = mt(R_MISC, LANES)


def _inproj(x2, g1, wp, wt, B, T, tm):
    R, D = x2.shape
    assert R == B * T and R % tm == 0
    mdt = F32 if wt is None else BF16
    row = lambda i: (i, 0)
    outs = [(LANES, F32), (M_WIDTH, mdt), (M_WIDTH, mdt), (M_WIDTH, mdt), (M_WIDTH, F32)]
    in_specs = [pl.BlockSpec((tm, D), row), _const_spec((1, D)), _const_spec((D, N_PACK))]
    if wt is None:
        kern, args = _inproj_rows_kernel, (x2, g1, wp)
        outs += [(ATT_WIDTH, F32), (N_IDX_HEADS * IDX_DIM, F32), (LANES, F32), (LANES, F32), (IDX_DIM, F32)]
    else:
        kern, args = _inproj_cols_kernel, (x2, g1, wp, wt)
        in_specs.append(_const_spec(wt.shape))
        outs += [(LANES, BF16), (IDX_DIM, BF16)]
    out_specs = [pl.BlockSpec((tm, w), row) for w, _ in outs]
    out_shape = [jax.ShapeDtypeStruct((R, w), dt) for w, dt in outs]
    if wt is not None:
        assert T % tm == 0 and tm % CKP == 0
        tpb, cpt = T // tm, tm // CKP
        cols = lambda i: (i // tpb, 0, i % tpb)
        for w, dt in ((LANES, F32), (LANES, F32), (IDX_DIM, F32)):
            out_specs.append(pl.BlockSpec((1, w, tm), cols))
            out_shape.append(jax.ShapeDtypeStruct((B, w, T), dt))
        out_specs.append(pl.BlockSpec((1, cpt, VT_ROWS, CKP), lambda i: (i // tpb, i % tpb, 0, 0)))
        out_shape.append(jax.ShapeDtypeStruct((B, T // CKP, VT_ROWS, CKP), BF16))
        for w, dt in ((ATT_WIDTH, BF16), (N_IDX_HEADS * IDX_DIM, BF16), (LANES, F32)):
            out_specs.append(pl.BlockSpec((1, w, tm), cols))
            out_shape.append(jax.ShapeDtypeStruct((B, w, T), dt))
    return pl.pallas_call(
        kern,
        grid=(R // tm,),
        in_specs=in_specs,
        out_specs=out_specs,
        out_shape=out_shape,
        compiler_params=_cparams(1),
        name="inproj",
    )(*args)


def _post_kernel(ff_chunk, final_norm, x_ref, a_ref, h_ref, woa_ref, woh_ref, g2_ref, wup_ref,
                 wdn_ref, gf_ref, y_ref):
    mix = jnp.dot(a_ref[...].astype(BF16), woa_ref[...], preferred_element_type=F32)
    mix = mix + jnp.dot(h_ref[...].astype(BF16), woh_ref[...], preferred_element_type=F32)
    hres = x_ref[...] + mix
    f = _rms(hres, g2_ref[...]).astype(BF16)
    acc = hres
    for c0 in range(0, wup_ref.shape[1], ff_chunk):
        up = jnp.dot(f, wup_ref[:, c0:c0 + ff_chunk], preferred_element_type=F32)
        r = jnp.maximum(up, 0.0)
        acc = acc + jnp.dot((r * r).astype(BF16), wdn_ref[c0:c0 + ff_chunk, :],
                            preferred_element_type=F32)
    y_ref[...] = _rms(acc, gf_ref[...]) if final_norm else acc


def _post(x2, attn, h, woa, woh, g2, wup, wdn, gf, final_norm, tm):
    R, D = x2.shape
    dff = wup.shape[1]
    assert R % tm == 0
    row = lambda i: (i, 0)
    return pl.pallas_call(
        functools.partial(_post_kernel, min(dff, 1024), final_norm),
        grid=(R // tm,),
        in_specs=[pl.BlockSpec((tm, D), row), pl.BlockSpec((tm, ATT_WIDTH), row),
                  pl.BlockSpec((tm, M_WIDTH), row), _const_spec(woa.shape), _const_spec(woh.shape),
                  _const_spec((1, D)), _const_spec(wup.shape), _const_spec(wdn.shape),
                  _const_spec((1, D))],
        out_specs=pl.BlockSpec((tm, D), row),
        out_shape=jax.ShapeDtypeStruct((R, D), F32),
        compiler_params=_cparams(1),
        name="post",
    )(x2, attn, h, woa, woh, g2, wup, wdn, gf)


def _log_sigmoid(x):
    return -(jnp.maximum(-x, 0.0) + jnp.log1p(jnp.exp(-jnp.abs(x))))


def _mlstm_kernel(nvalid, has_state, *refs):
    if has_state:
        (q_ref, k_ref, v_ref, o_ref, misc_ref, gb_ref, mn_ref, c0_ref, n0_ref, m0_ref,
         h_ref, c_ref, n_ref, m_ref, z_scr, rows_scr) = refs
    else:
        (q_ref, k_ref, v_ref, o_ref, misc_ref, gb_ref, mn_ref,
         h_ref, c_ref, n_ref, m_ref, z_scr, rows_scr) = refs
    nb, nc, L = z_scr.shape[0], z_scr.shape[1], z_scr.shape[2]
    c = pl.program_id(1)

    def padded(x, dt):
        x = x.astype(dt)
        if nvalid == L:
            return x
        return jnp.concatenate([x, jnp.zeros((L - nvalid, x.shape[1]), dt)], axis=0)

    r2 = lax.broadcasted_iota(I32, (L, L), 0)
    c2 = lax.broadcasted_iota(I32, (L, L), 1)
    tril = r2 >= c2

    @pl.when(c == 0)
    def _():
        if has_state:
            c_ref[...] = c0_ref[...]
            n_ref[...] = n0_ref[...]
            m_ref[...] = m0_ref[...]
        else:
            c_ref[...] = jnp.zeros_like(c_ref)
            n_ref[...] = jnp.zeros_like(n_ref)
            m_ref[...] = jnp.zeros_like(m_ref)
        assert MISC_F == MISC_I + M_HEADS and MISC_I % SUBLANES == 0
        row8 = lax.broadcasted_iota(I32, (SUBLANES, L), 0)
        tok8 = lax.broadcasted_iota(I32, (SUBLANES, L), 1)
        bias8 = jnp.concatenate([gb_ref[...]] * (L // LANES), axis=1)
        triu = (r2 <= c2).astype(F32)
        for s in range(nb):
            for cc in range(nc):
                misc_t = padded(misc_ref[s, cc * nvalid:(cc + 1) * nvalid, :], F32).T
                gx = misc_t[MISC_I:MISC_I + SUBLANES] + bias8
                gates = jnp.where(row8 >= M_HEADS, _log_sigmoid(gx), gx)
                if nvalid != L:
                    gates = jnp.where(tok8 < nvalid, gates, jnp.where(row8 >= M_HEADS, 0.0, NEG))
                cum = jnp.dot(gates, triu, precision=lax.Precision.HIGHEST,
                              preferred_element_type=F32)
                rows = jnp.where(row8 >= M_HEADS, cum, gates)
                rows_scr[s, cc] = rows
                z_scr[s, cc] = jnp.concatenate([rows, jnp.zeros((LANES - SUBLANES, L), F32)], axis=0).T

    loaded = []
    for s in range(nb):
        loaded.append((z_scr[s, c], rows_scr[s, c], padded(q_ref[s], BF16), padded(k_ref[s], BF16),
                       padded(v_ref[s], BF16), padded(o_ref[s], F32), m_ref[s], c_ref[s], n_ref[s]))
    chains = [(s, hd) for s in range(nb) for hd in range(M_HEADS)]

    def operands(s, hd):
        z, rows, qb, kb, vb, ob, m_all, c_all, n_all = loaded[s]
        sl = slice(hd * M_HEAD_DIM, (hd + 1) * M_HEAD_DIM)
        return dict(
            sl=sl, q=qb[:, sl], k=kb[:, sl], v=vb[:, sl], o=ob[:, sl],
            icol=z[:, hd:hd + 1], bcol=z[:, M_HEADS + hd:M_HEADS + hd + 1],
            irow=rows[hd:hd + 1, :], brow=rows[M_HEADS + hd:M_HEADS + hd + 1, :],
            m_prev=m_all[hd:hd + 1, 0:1], s_prev=c_all[hd], n_prev=n_all[hd:hd + 1, :])

    def lane_sum(x):
        ones = jnp.ones((x.shape[1], LANES), BF16)
        head = x.astype(BF16)
        tail = (x - head.astype(F32)).astype(BF16)
        return (jnp.dot(head, ones, preferred_element_type=F32)
                + jnp.dot(tail, ones, preferred_element_type=F32))

    st = [operands(s, hd) for s, hd in chains]
    for x in st:
        x["qk"] = lax.dot_general(x["q"], x["k"], NT_DIMS, preferred_element_type=F32)
        x["qs"] = jnp.dot(x["q"], x["s_prev"].astype(BF16), preferred_element_type=F32)
        x["qn"] = lane_sum(x["q"].astype(F32) * x["n_prev"])
    for x in st:
        g = x["bcol"] + x["m_prev"]
        dm = jnp.where(tril, x["bcol"] - x["brow"] + x["irow"], NEG)
        x["mt"] = jnp.maximum(g, jnp.max(dm, axis=1, keepdims=True))
        x["gw"] = jnp.exp(g - x["mt"])
        x["qk"] = x["qk"] * jnp.exp(dm - x["mt"])
        b_last = x["bcol"][L - 1:L, :]
        g_last = b_last + x["m_prev"]
        a = b_last - x["bcol"] + x["icol"]
        x["m_new"] = jnp.maximum(g_last, jnp.max(a, axis=0, keepdims=True))
        x["sw"] = jnp.exp(g_last - x["m_new"])
        x["ak"] = jnp.exp(a - x["m_new"]) * x["k"].astype(F32)
    for x in st:
        x["pv"] = jnp.dot(x["qk"].astype(BF16), x["v"], preferred_element_type=F32)
        x["qksum"] = lane_sum(x["qk"])
        x["kv"] = lax.dot_general(x["ak"].astype(BF16), x["v"], TN_DIMS, preferred_element_type=F32)
    stores = []
    for (s, hd), x in zip(chains, st):
        num = x["gw"] * x["qs"] + x["pv"]
        den = x["gw"] * x["qn"] + x["qksum"]
        hh = num / jnp.maximum(jnp.abs(den), jnp.exp(-x["mt"]))
        hh = hh * lax.rsqrt(lane_sum(hh * hh) * (1.0 / M_HEAD_DIM) + EPS)
        hh = hh * mn_ref[:, x["sl"]] * jax.nn.sigmoid(x["o"])
        c_new = x["sw"] * x["s_prev"] + x["kv"]
        n_new = x["sw"] * x["n_prev"] + jnp.sum(x["ak"], axis=0, keepdims=True)
        stores.append((s, hd, x["sl"], hh[:nvalid].astype(h_ref.dtype), c_new, n_new,
                       jnp.broadcast_to(x["m_new"], (1, LANES))))
    for s, hd, sl, h_new, c_new, n_new, m_new in stores:
        h_ref[s, :, sl] = h_new
        c_ref[s, hd] = c_new
        n_ref[s, hd:hd + 1, :] = n_new
        m_ref[s, hd:hd + 1, :] = m_new


def _mlstm(qm, km, vm, om, misc, gate_bias, mnorm, state, B, T, hdt):
    L = next((c for c in (ML, LANES) if T % c == 0), LANES)
    nvalid = L if T % L == 0 else T
    assert nvalid <= L and T % nvalid == 0 and nvalid % SUBLANES == 0
    nc = T // nvalid
    nb = next(n for n in ((2, 1) if nvalid == L else (4, 2, 1)) if B % n == 0)
    seq3 = lambda a: a.reshape(B, T, a.shape[-1])
    blk = lambda w: pl.BlockSpec((nb, nvalid, w), lambda b, c: (b, c, 0))
    st_specs = [pl.BlockSpec((nb, M_HEADS, M_HEAD_DIM, M_HEAD_DIM), lambda b, c: (b, 0, 0, 0)),
                pl.BlockSpec((nb, M_HEADS, M_HEAD_DIM), lambda b, c: (b, 0, 0)),
                pl.BlockSpec((nb, M_HEADS, LANES), lambda b, c: (b, 0, 0))]
    in_specs = [blk(M_WIDTH), blk(M_WIDTH), blk(M_WIDTH), blk(M_WIDTH),
                pl.BlockSpec((nb, T, LANES), lambda b, c: (b, 0, 0)),
                pl.BlockSpec((SUBLANES, LANES), lambda b, c: (0, 0)),
                pl.BlockSpec((1, M_WIDTH), lambda b, c: (0, 0))]
    args = [seq3(qm), seq3(km), seq3(vm), seq3(om), seq3(misc), gate_bias, mnorm]
    if state is not None:
        in_specs += st_specs
        args += list(state)
    h, c_new, n_new, m_new = pl.pallas_call(
        functools.partial(_mlstm_kernel, nvalid, state is not None),
        grid=(B // nb, nc),
        in_specs=in_specs,
        out_specs=[blk(M_WIDTH)] + st_specs,
        out_shape=[jax.ShapeDtypeStruct((B, T, M_WIDTH), hdt),
                   jax.ShapeDtypeStruct((B, M_HEADS, M_HEAD_DIM, M_HEAD_DIM), F32),
                   jax.ShapeDtypeStruct((B, M_HEADS, M_HEAD_DIM), F32),
                   jax.ShapeDtypeStruct((B, M_HEADS, LANES), F32)],
        scratch_shapes=[pltpu.VMEM((nb, nc, L, LANES), F32), pltpu.VMEM((nb, nc, SUBLANES, L), F32)],
        compiler_params=_cparams(2),
        name="mlstm",
    )(*args)
    return h.reshape(B * T, M_WIDTH), c_new, n_new, m_new


def _sortable_key(score):
    bits = lax.bitcast_convert_type(score, I32)
    return bits ^ (lax.shift_right_arithmetic(bits, 31) & 0x7FFFFFFF)


def _key_to_score(key):
    return lax.bitcast_convert_type(key ^ (lax.shift_right_arithmetic(key, 31) & 0x7FFFFFFF), F32)


def _build_bias_strip(strip_ref, rb_ref, off, key_axis, log2_relative=False):
    ntiles, tile = strip_ref.shape[1], strip_ref.shape[2:]
    i = lax.broadcasted_iota(I32, tile, 1 - key_axis)
    x = lax.broadcasted_iota(I32, tile, key_axis)

    def entry(b, h):
        if log2_relative:
            return (rb_ref[b, h] - rb_ref[N_BUCKETS - 1, h]) * LOG2E
        return rb_ref[b, h]

    for t in range(ntiles):
        dist = i + (off - LANES * t) - x
        for h in range(N_HEADS):
            val = jnp.full(tile, entry(0, h), F32)
            for b in range(1, N_BUCKETS):
                val = jnp.where(dist >= BUCKET_BOUNDS[b], entry(b, h), val)
            strip_ref[h, t] = val


def _counter(score_ref, nk, ck, key_axis):
    def count(pred):
        def body(c, acc):
            c0 = pl.multiple_of(c * ck, ck)
            sc = score_ref[:, pl.ds(c0, ck)] if key_axis == 1 else score_ref[pl.ds(c0, ck), :]
            idx = c0 + lax.broadcasted_iota(I32, sc.shape, key_axis)
            hit = jnp.where(pred(sc, idx), 1, 0)
            if key_axis == 1:
                for j in range(ck // LANES):
                    acc = acc + hit[:, j * LANES:(j + 1) * LANES]
                return acc
            return acc + jnp.sum(hit.reshape(ck // SUBLANES, SUBLANES, hit.shape[1]), axis=0)

        nq = score_ref.shape[1 - key_axis]
        acc0 = jnp.zeros((nq, LANES) if key_axis == 1 else (SUBLANES, nq), I32)
        return jnp.sum(lax.fori_loop(0, nk, body, acc0), axis=key_axis, keepdims=True)

    return count


IMIN = jnp.iinfo(jnp.int32).min


def _kth_largest_by_count(count, qshape, n_sel):
    def bit_step(i, key):
        cand = key + lax.shift_left(jnp.int32(1), 31 - i)
        cand_score = _key_to_score(cand)
        return jnp.where(count(lambda sc, idx: sc >= cand_score) >= n_sel, cand, key)

    return _key_to_score(lax.fori_loop(0, 32, bit_step, jnp.full(qshape, IMIN, I32)))


def _bit_planes(words):
    a = list(words)
    j, m = 16, 0x0000FFFF
    while j:
        k = 0
        while k < 32:
            t = (a[k] ^ lax.shift_right_logical(a[k + j], j)) & m
            a[k] = a[k] ^ t
            a[k + j] = a[k + j] ^ lax.shift_left(t, j)
            k = (k + j + 1) & ~j
        j >>= 1
        m = (m ^ (m << j)) & 0xFFFFFFFF
        m = m - (1 << 32) if m >= (1 << 31) else m
    return a[::-1]


def _kth_largest_by_planes(score_ref, planes_ref, nk, ck, n_sel):
    nc = planes_ref.shape[1]
    nq = score_ref.shape[1]
    assert ck == 32 * SUBLANES

    def pack_chunk(c, _):
        c0 = pl.multiple_of(c * ck, ck)
        u = _sortable_key(score_ref[pl.ds(c0, ck), :]) ^ IMIN
        u = u.reshape(32, SUBLANES, nq)
        for b, plane in enumerate(_bit_planes([u[v] for v in range(32)])):
            planes_ref[b, c] = plane
        return 0

    lax.fori_loop(0, nk, pack_chunk, 0)
    cand0 = tuple(jnp.where(c < nk, jnp.full((SUBLANES, nq), -1, I32), 0) for c in range(nc))
    return _plane_search(planes_ref, cand0, n_sel, 0)


def _kth_largest_by_planes_lanes(score_ref, planes_ref, n_sel):
    nq, ntiles = score_ref.shape[0], score_ref.shape[1] // LANES
    cand0 = []
    for g in range(planes_ref.shape[1]):
        real = min(32, ntiles - 32 * g)
        words = [_sortable_key(score_ref[:, (32 * g + v) * LANES:(32 * g + v + 1) * LANES]) ^ IMIN
                 if v < real else jnp.zeros((nq, LANES), I32) for v in range(32)]
        for b, plane in enumerate(_bit_planes(words)):
            planes_ref[b, g] = plane
        cand0.append(jnp.full((nq, LANES), -(1 << (32 - real)), I32))
    return _plane_search(planes_ref, tuple(cand0), n_sel, 1)


def _plane_search(planes_ref, cand0, n_sel, key_axis):
    def bit_step(i, carry):
        cand, n_above, thr_u = carry
        b = 31 - i
        ones = [m & planes_ref[b, c] for c, m in enumerate(cand)]
        pop = lax.population_count(ones[0])
        for o in ones[1:]:
            pop = pop + lax.population_count(o)
        tot = jnp.sum(pop, axis=key_axis, keepdims=True)
        take = n_above + tot >= n_sel
        cand = tuple(jnp.where(take, o, m ^ o) for o, m in zip(ones, cand))
        n_above = jnp.where(take, n_above, n_above + tot)
        thr_u = thr_u | jnp.where(take, lax.shift_left(jnp.int32(1), b), 0)
        return cand, n_above, thr_u

    zero = jnp.zeros_like(jnp.sum(cand0[0], axis=key_axis, keepdims=True))
    _, _, thr_u = lax.fori_loop(0, 32, bit_step, (cand0, zero, zero))
    return _key_to_score(thr_u ^ IMIN)


def _select_topk(thr, check, count, qshape, n_sel, idx_bits):
    imax = jnp.iinfo(jnp.int32).max

    def with_counts(t):
        return t, count(lambda sc, idx: sc > t), count(lambda sc, idx: sc >= t)

    thr, n_gt, n_ge = with_counts(thr)
    if check:
        good = jnp.min(jnp.where(n_gt < n_sel, jnp.where(n_ge >= n_sel, 1, 0), 0)) > 0
        thr, n_gt, n_ge = lax.cond(
            good, lambda _: (thr, n_gt, n_ge),
            lambda _: with_counts(_kth_largest_by_count(count, qshape, n_sel)), 0)
    need = n_sel - n_gt

    def tie_search(_):
        def idx_step(i, lo):
            cand = lo + lax.shift_left(jnp.int32(1), idx_bits - 1 - i)
            cnt = count(lambda sc, idx: jnp.where(sc == thr, idx, imax) < cand)
            return jnp.where(cnt < need, cand, lo)

        return lax.fori_loop(0, idx_bits, idx_step, jnp.zeros(qshape, I32))

    all_ties = jnp.full(qshape, imax, I32)
    jstar = lax.cond(jnp.max(n_ge) > n_sel, tie_search, lambda _: all_ties, 0)
    jstar = jnp.where(n_ge > n_sel, jstar, all_ties)
    return thr, jstar


def _valid_mask(scores, idx, thr, jstar, qpos):
    sel = jnp.where(scores > thr, 1, jnp.where(scores == thr, jnp.where(idx <= jstar, 1, 0), 0))
    return jnp.where(idx <= qpos, sel, 0) > 0


def _group_queries(qa, tq):
    lane = lax.broadcasted_iota(I32, (tq, LANES), 1)
    out = []
    for n in range(N_KV_HEADS):
        keep = (lane < HEAD_DIM) if n == 0 else (lane >= HEAD_DIM)
        tiles = [jnp.where(keep, qa[:, j * LANES:(j + 1) * LANES], jnp.zeros((), qa.dtype))
                 for j in range(GROUP)]
        out.append(jnp.concatenate(tiles, axis=0).astype(BF16))
    return out


def _write_attn(out_ref, carries, tq):
    lane = lax.broadcasted_iota(I32, (tq, LANES), 1)
    res = [acc / l for (_, l, acc) in carries]
    for j in range(GROUP):
        tile = jnp.where(lane < HEAD_DIM, res[0][j * tq:(j + 1) * tq], res[1][j * tq:(j + 1) * tq])
        out_ref[:, j * LANES:(j + 1) * LANES] = tile.astype(out_ref.dtype)


def _dsa_prompt_kernel(n_sel, idx_bits, rb_ref, qat_ref, qit_ref, misct_ref, ki_ref, k_ref, vt_ref,
                       out_ref, score_ref, planes_ref, strip_ref, sa_ref, sb_ref, m_ref, acc_ref):
    qb = pl.program_id(1)
    q0 = qb * TQ
    nk = (q0 + TQ + CKP - 1) // CKP
    tiles_per_chunk = CKP // LANES
    back_tiles = STRIP_BACK
    strip_off = LANES * back_tiles

    @pl.when((pl.program_id(0) == 0) & (qb == 0))
    def _():
        _build_bias_strip(strip_ref, rb_ref, strip_off, 0, log2_relative=True)
        planes_ref[...] = jnp.zeros(planes_ref.shape, I32)

    qpos = q0 + lax.broadcasted_iota(I32, (1, TQ), 1)

    qit = qit_ref[0]
    qstack = jnp.concatenate([qit[h * IDX_DIM:(h + 1) * IDX_DIM] for h in range(N_IDX_HEADS)],
                             axis=1)
    w = misct_ref[0, MISC_W:MISC_W + N_IDX_HEADS, :] * (N_IDX_HEADS ** -0.5 * IDX_DIM ** -0.5)

    def score_chunk(c, _):
        c0 = pl.multiple_of(c * CKP, CKP)
        d = jnp.dot(ki_ref[pl.ds(c0, CKP), :], qstack, preferred_element_type=F32)
        d = jnp.maximum(d, 0.0)
        s = jnp.zeros((CKP, TQ), F32)
        for h in range(N_IDX_HEADS):
            s = s + d[:, h * TQ:(h + 1) * TQ] * w[h:h + 1, :]
        idx = c0 + lax.broadcasted_iota(I32, (CKP, TQ), 0)
        score_ref[pl.ds(c0, CKP), :] = jnp.where(idx <= qpos, s, NEG)
        return 0

    lax.fori_loop(0, nk, score_chunk, 0)

    thr = _kth_largest_by_planes(score_ref, planes_ref, nk, CKP, n_sel)
    thr, jstar = _select_topk(thr, True, _counter(score_ref, nk, CKP, 0), (1, TQ), n_sel, idx_bits)

    qat = qat_ref[0]
    zeros = jnp.zeros((HEAD_DIM, TQ), qat.dtype)
    qgroups = []
    for n in range(N_KV_HEADS):
        tiles = []
        for g in range(GROUP):
            h = n * GROUP + g
            x = qat[h * HEAD_DIM:(h + 1) * HEAD_DIM]
            tiles.append(jnp.concatenate([x, zeros] if n == 0 else [zeros, x], axis=0))
        qgroups.append(jnp.concatenate(tiles, axis=1))

    def logits_into(s_ref, c):
        kc = k_ref[pl.ds(pl.multiple_of(c * CKP, CKP), CKP), :]
        for n in range(N_KV_HEADS):
            s_ref[n] = jnp.dot(kc, qgroups[n], preferred_element_type=F32)

    def attend_chunk(c, s_cur_ref, s_next_ref):
        if s_next_ref is not None:
            logits_into(s_next_ref, jnp.minimum(c + 1, nk - 1))
        c0 = pl.multiple_of(c * CKP, CKP)
        idx = c0 + lax.broadcasted_iota(I32, (CKP, TQ), 0)
        valid = _valid_mask(score_ref[pl.ds(c0, CKP), :], idx, thr, jstar, qpos)
        vct = vt_ref[0, c]
        tiles = [jnp.maximum(back_tiles + j - (qb * (TQ // LANES) - c * tiles_per_chunk), 0)
                 for j in range(tiles_per_chunk)]
        for n in range(N_KV_HEADS):
            m_old = m_ref[n]
            parts = []
            for g in range(GROUP):
                bias = jnp.concatenate([strip_ref[n * GROUP + g, t] for t in tiles], axis=0)
                parts.append(jnp.where(valid, s_cur_ref[n, :, g * TQ:(g + 1) * TQ] + bias, NEG))
            sm = jnp.concatenate(parts, axis=1)
            m_new = jnp.maximum(m_old, jnp.max(sm, axis=0, keepdims=True))
            p = jnp.exp2((sm - m_new).astype(BF16))
            acc_ref[n] = jnp.exp2(m_old - m_new) * acc_ref[n] + jnp.dot(vct, p, preferred_element_type=F32)
            m_ref[n] = m_new

    m_ref[...] = jnp.full(m_ref.shape, NEG, F32)
    acc_ref[...] = jnp.zeros(acc_ref.shape, F32)
    logits_into(sa_ref, 0)

    def attend_pair(i, _):
        attend_chunk(2 * i, sa_ref, sb_ref)
        attend_chunk(2 * i + 1, sb_ref, sa_ref)
        return 0

    lax.fori_loop(0, nk // 2, attend_pair, 0)

    @pl.when(nk % 2 == 1)
    def _():
        attend_chunk(nk - 1, sa_ref, None)

    carries = [(None, acc_ref[n]) for n in range(N_KV_HEADS)]
    res = [acc[:LANES] / acc[LANES:LANES + 1] for (_, acc) in carries]
    row = lax.broadcasted_iota(I32, (LANES, TQ), 0)
    for j in range(GROUP):
        cols = slice(j * TQ, (j + 1) * TQ)
        tile_t = jnp.where(row < HEAD_DIM, res[0][:, cols], res[1][:, cols])
        out_ref[:, j * LANES:(j + 1) * LANES] = tile_t.T.astype(out_ref.dtype)


def _dsa_prompt(rel_bias, qat, qit, misct, kib, kb, vtb, B, T):
    assert T % CKP == 0 and T % TQ == 0
    nq = T // TQ
    n_sel = min(TOPK_MAX, T // 4)
    idx_bits = max(1, (T - 1).bit_length())
    qcols = lambda w: pl.BlockSpec((1, w, TQ), lambda b, q: (b, 0, q))
    seq = lambda w: pl.BlockSpec((T, w), lambda b, q: (b, 0))
    return pl.pallas_call(
        functools.partial(_dsa_prompt_kernel, n_sel, idx_bits),
        grid=(B, nq),
        in_specs=[pl.BlockSpec(memory_space=pltpu.SMEM), qcols(ATT_WIDTH), qcols(N_IDX_HEADS * IDX_DIM),
                  qcols(LANES), seq(IDX_DIM), seq(LANES),
                  pl.BlockSpec((1, T // CKP, VT_ROWS, CKP), lambda b, q: (b, 0, 0, 0))],
        out_specs=pl.BlockSpec((TQ, ATT_WIDTH), lambda b, q: (b * nq + q, 0)),
        out_shape=jax.ShapeDtypeStruct((B * T, ATT_WIDTH), BF16),
        scratch_shapes=[pltpu.VMEM((T, TQ), F32),
                        pltpu.VMEM((32, T // CKP, SUBLANES, TQ), I32),
                        pltpu.VMEM((N_HEADS, STRIP_BACK + max(TQ, CKP) // LANES, LANES, TQ), F32),
                        pltpu.VMEM((N_KV_HEADS, CKP, GROUP * TQ), F32),
                        pltpu.VMEM((N_KV_HEADS, CKP, GROUP * TQ), F32),
                        pltpu.VMEM((N_KV_HEADS, 1, GROUP * TQ), F32),
                        pltpu.VMEM((N_KV_HEADS, VT_ROWS, GROUP * TQ), F32)],
        compiler_params=_cparams(2),
        name="dsa_prompt",
    )(rel_bias, qat, qit, misct, kib, kb, vtb)


def _page_pipeline(pt_ref, n_pages, caches, bufs, sems):
    def copies(bb, sl, j):
        pid = pt_ref[bb, j]
        cols = pl.ds(pl.multiple_of(j * PAGE_SIZE, PAGE_SIZE), PAGE_SIZE)
        return [pltpu.make_async_copy(c.at[pid], buf.at[sl, :, cols], sems.at[sl, i])
                for i, (c, buf) in enumerate(zip(caches, bufs))]

    def start_all(bb, sl):
        def body(j, _):
            for cp in copies(bb, sl, j):
                cp.start()
            return 0
        lax.fori_loop(0, n_pages, body, 0)

    def wait_all(bb, sl):
        def body(j, _):
            for cp in copies(bb, sl, j):
                cp.wait()
            return 0
        lax.fori_loop(0, n_pages, body, 0)

    def step():
        b = pl.program_id(0)
        slot = b % 2

        @pl.when(b == 0)
        def _():
            start_all(0, 0)

        @pl.when(b + 1 < pl.num_programs(0))
        def _():
            start_all(b + 1, 1 - slot)

        wait_all(b, slot)
        return slot

    return step


def _pad_rows(x, rows):
    return jnp.concatenate([x, jnp.zeros((rows - x.shape[0], x.shape[1]), x.dtype)], axis=0)


def _sample_score_kernel(n_sel, idx_bits, n_pages, ts, ck, pt_ref, qi_ref, misc_ref, kin_ref,
                         ckit_hbm, score_ref, thr_ref, jst_ref, ki_buf, sems, planes_ref):
    b = pl.program_id(0)
    past = n_pages * PAGE_SIZE
    slot = _page_pipeline(pt_ref, n_pages, [ckit_hbm], [ki_buf], sems)()

    qi = qi_ref[...]
    qstack = jnp.concatenate([qi[:, h * IDX_DIM:(h + 1) * IDX_DIM] for h in range(N_IDX_HEADS)],
                             axis=0).astype(BF16)
    w = misc_ref[:, MISC_W:MISC_W + N_IDX_HEADS] * (N_IDX_HEADS ** -0.5 * IDX_DIM ** -0.5)
    d_past = jnp.dot(qstack, ki_buf[slot].astype(BF16), preferred_element_type=F32)
    d_own = lax.dot_general(qstack, _pad_rows(kin_ref[...], PAGE_SIZE).astype(BF16), NT_DIMS,
                            preferred_element_type=F32)
    d = jnp.maximum(jnp.concatenate([d_past, d_own], axis=1), 0.0)
    s = jnp.zeros((ts, past + PAGE_SIZE), F32)
    for h in range(N_IDX_HEADS):
        s = s + d[h * ts:(h + 1) * ts] * w[:, h:h + 1]
    idx = lax.broadcasted_iota(I32, s.shape, 1)
    qpos = past + lax.broadcasted_iota(I32, (ts, 1), 0)
    score_ref[pl.ds(pl.multiple_of(b * ts, ts), ts), :] = jnp.where(idx <= qpos, s, NEG)

    @pl.when(b == pl.num_programs(0) - 1)
    def _():
        count = _counter(score_ref, score_ref.shape[1] // ck, ck, 1)
        qshape = (score_ref.shape[0], 1)
        thr = _kth_largest_by_planes_lanes(score_ref, planes_ref, n_sel)
        thr, jstar = _select_topk(thr, True, count, qshape, n_sel, idx_bits)
        thr_ref[...] = jnp.broadcast_to(thr, thr_ref.shape)
        jst_ref[...] = jnp.broadcast_to(jstar, jst_ref.shape)


def _sample_attend_kernel(n_pages, ts, pt_ref, rb_ref, qa_ref, score_ref, thr_ref, jst_ref, kn_ref,
                          vn_ref, ckt_hbm, cvt_hbm, out_ref, k_buf, v_buf, sems, strip_ref):
    b = pl.program_id(0)
    past = n_pages * PAGE_SIZE
    rows = N_HEADS * ts

    @pl.when(b == 0)
    def _():
        _build_bias_strip(strip_ref, rb_ref, LANES * (strip_ref.shape[1] - 1), 1)

    slot = _page_pipeline(pt_ref, n_pages, [ckt_hbm, cvt_hbm], [k_buf, v_buf], sems)()

    q2 = jnp.concatenate(_group_queries(qa_ref[...], ts), axis=0)
    k_own = _pad_rows(kn_ref[...], PAGE_SIZE).astype(BF16)
    v_own = _pad_rows(vn_ref[...], PAGE_SIZE).astype(BF16)
    s_past = jnp.dot(q2, k_buf[slot].astype(BF16), preferred_element_type=F32)
    s_own = lax.dot_general(q2, k_own, NT_DIMS, preferred_element_type=F32)
    far = strip_ref[:, 0].reshape(rows, LANES)[:, 0:1]
    near = [strip_ref[:, t].reshape(rows, LANES) for t in (1, 2)]
    s = jnp.concatenate([s_past[:, :past - PAGE_SIZE] + far, s_past[:, past - PAGE_SIZE:] + near[0],
                         s_own + near[1]], axis=1)

    qpos = past + lax.broadcasted_iota(I32, (ts, 1), 0)
    scores = score_ref[...]
    valid = _valid_mask(scores, lax.broadcasted_iota(I32, scores.shape, 1), thr_ref[:, 0:1],
                        jst_ref[:, 0:1], qpos)
    s = jnp.where(valid[None], s.reshape(N_HEADS, ts, past + PAGE_SIZE), NEG).reshape(rows, -1)
    m = jnp.max(s, axis=1, keepdims=True)
    p = jnp.exp(s - m)
    l = jnp.sum(p, axis=1, keepdims=True)
    pb = p.astype(BF16)
    pv = lax.dot_general(pb[:, :past], v_buf[slot].astype(BF16), NT_DIMS, preferred_element_type=F32)
    pv = pv + jnp.dot(pb[:, past:], v_own, preferred_element_type=F32)
    half = GROUP * ts
    carries = [(None, l[n * half:(n + 1) * half], pv[n * half:(n + 1) * half]) for n in range(N_KV_HEADS)]
    _write_attn(out_ref, carries, ts)


def _dsa_sample(page_table, rel_bias, qa, qi, misc, ki_new, k_new, v_new, ckit, ckt, cvt, DB, ts):
    n_pages = page_table.shape[1]
    past = n_pages * PAGE_SIZE
    n_sel = min(TOPK_MAX, (past + ts) // 4)
    lpad = past + PAGE_SIZE
    idx_bits = max(1, (lpad - 1).bit_length())
    ck = LANES * math.gcd(lpad // LANES, 5)
    assert ts % SUBLANES == 0 and ts <= PAGE_SIZE and n_pages >= 1
    blk = lambda w: pl.BlockSpec((ts, w), lambda b, pt: (b, 0))
    whole = lambda w: pl.BlockSpec((DB * ts, w), lambda b, pt: (0, 0))
    hbm = pl.BlockSpec(memory_space=pl.ANY)
    keys, thr, jstar = pl.pallas_call(
        functools.partial(_sample_score_kernel, n_sel, idx_bits, n_pages, ts, ck),
        grid_spec=pltpu.PrefetchScalarGridSpec(
            num_scalar_prefetch=1,
            grid=(DB,),
            in_specs=[blk(N_IDX_HEADS * IDX_DIM), blk(LANES), blk(IDX_DIM), hbm],
            out_specs=[whole(lpad), whole(LANES), whole(LANES)],
            scratch_shapes=[pltpu.VMEM((2, IDX_DIM, past), F32), pltpu.SemaphoreType.DMA((2, 1)),
                            pltpu.VMEM((32, -(-lpad // (32 * LANES)), DB * ts, LANES), I32)]),
        out_shape=[jax.ShapeDtypeStruct((DB * ts, lpad), F32),
                   jax.ShapeDtypeStruct((DB * ts, LANES), F32),
                   jax.ShapeDtypeStruct((DB * ts, LANES), I32)],
        compiler_params=_cparams(1),
        name="sample_score",
    )(page_table, qi, misc, ki_new, ckit)
    return pl.pallas_call(
        functools.partial(_sample_attend_kernel, n_pages, ts),
        grid_spec=pltpu.PrefetchScalarGridSpec(
            num_scalar_prefetch=1,
            grid=(DB,),
            in_specs=[pl.BlockSpec(memory_space=pltpu.SMEM), blk(ATT_WIDTH), blk(lpad), blk(LANES),
                      blk(LANES), blk(LANES), blk(LANES), hbm, hbm],
            out_specs=blk(ATT_WIDTH),
            scratch_shapes=[pltpu.VMEM((2, LANES, past), F32), pltpu.VMEM((2, LANES, past), F32),
                            pltpu.SemaphoreType.DMA((2, 2)),
                            pltpu.VMEM((N_HEADS, 3, ts, LANES), F32)]),
        out_shape=jax.ShapeDtypeStruct((DB * ts, ATT_WIDTH), F32),
        compiler_params=_cparams(1),
        name="sample_attend",
    )(page_table, rel_bias, qa, keys, thr, jstar, k_new, v_new, ckt, cvt)


def _pack_layer_weights(w_in, b_i, b_f, w_out, w_up, w_down):
    D = w_in.shape[0]
    sizes = (ATT_WIDTH, N_KV_HEADS * HEAD_DIM, N_KV_HEADS * HEAD_DIM, N_IDX_HEADS * IDX_DIM, IDX_DIM,
             N_IDX_HEADS, M_WIDTH, M_WIDTH, M_WIDTH, M_WIDTH, M_HEADS, M_HEADS)
    assert w_in.shape[1] == sum(sizes)
    pts = np.cumsum((0,) + sizes)
    seg = [w_in[:, pts[i]:pts[i + 1]] for i in range(len(sizes))]
    qa, k, v, qi, ki, wi, qm, km, vm, om, im, fm = seg
    perm = np.asarray(HEAD_PERM)
    qa = qa.reshape(D, N_HEADS, HEAD_DIM)[:, perm].reshape(D, ATT_WIDTH)
    misc = jnp.concatenate([wi, im, fm, jnp.zeros((D, LANES - N_IDX_HEADS - 2 * M_HEADS), w_in.dtype)], axis=1)
    wp = jnp.concatenate([qa, v, qi, k, ki, ki, misc, qm, km, vm, om], axis=1).astype(BF16)
    assert wp.shape[1] == N_PACK
    gate_bias = jnp.broadcast_to(jnp.concatenate([b_i, b_f]).astype(F32)[:, None], (2 * M_HEADS, LANES))
    woa = w_out[:ATT_WIDTH].reshape(N_HEADS, HEAD_DIM, -1)[perm].reshape(ATT_WIDTH, -1).astype(BF16)
    woh = w_out[ATT_WIDTH:].astype(BF16)
    wt = jnp.concatenate([seg[1].T, seg[2].T, ki.T, jnp.zeros((LANES - IDX_DIM, D), w_in.dtype),
                          seg[0].T, qi.T, misc.T], axis=0).astype(BF16)
    assert wt.shape[0] == N_TPACK
    return wp, wt, gate_bias, woa, woh, w_up.astype(BF16), w_down.astype(BF16)


def _layer(x, packed, g1, g2, mnorm, rel_bias, gf, final_norm, past):
    wp, wt, gate_bias, woa, woh, wup, wdn = packed
    B, T, D = x.shape
    x2 = x.reshape(B * T, D)
    tm = math.gcd(T if past is None else B * T, 512)
    kv_w = N_KV_HEADS * HEAD_DIM
    if past is None:
        (misc, qm, km, vm, om, kb, kib, kt, vt, kit, vtb, qat, qit, misct) = _inproj(
            x2, g1.reshape(1, D), wp, wt, B, T, tm)
        attn = _dsa_prompt(rel_bias, qat, qit, misct, kib, kb, vtb, B, T)
        state = None
        k_new = kt.reshape(B, N_KV_HEADS, HEAD_DIM, T).transpose(0, 3, 1, 2)
        v_new = vt.reshape(B, N_KV_HEADS, HEAD_DIM, T).transpose(0, 3, 1, 2)
        ki_new = kit.transpose(0, 2, 1)
    else:
        (misc, qm, km, vm, om, qa, qi, k, v, ki) = _inproj(x2, g1.reshape(1, D), wp, None, B, T, tm)
        page_table, cache_k, cache_v, cache_kidx, c0, n0, m0 = past
        n_pool = cache_k.shape[0]
        ckt = cache_k.transpose(0, 2, 3, 1).reshape(n_pool, kv_w, PAGE_SIZE)
        cvt = cache_v.transpose(0, 2, 3, 1).reshape(n_pool, kv_w, PAGE_SIZE)
        ckit = cache_kidx.transpose(0, 2, 1)
        attn = _dsa_sample(page_table, rel_bias, qa, qi, misc, ki, k, v, ckit, ckt, cvt, B, T)
        state = (c0, n0, jnp.broadcast_to(m0[..., None], m0.shape + (LANES,)))
        k_new = k.reshape(B, T, N_KV_HEADS, HEAD_DIM)
        v_new = v.reshape(B, T, N_KV_HEADS, HEAD_DIM)
        ki_new = ki.reshape(B, T, IDX_DIM)
    h, c_new, n_new, m_new = _mlstm(qm, km, vm, om, misc, gate_bias, mnorm.reshape(1, M_WIDTH), state,
                                    B, T, BF16 if past is None else F32)
    y = _post(x2, attn, h, woa, woh, g2.reshape(1, D), wup, wdn, gf.reshape(1, D), final_norm, tm)
    return (y.reshape(B, T, D), k_new, v_new, ki_new, c_new, n_new, m_new[..., 0])


def kernel(x_prompt, x_sample, cache_k, cache_v, cache_kidx, page_table, state_C, state_n, state_m,
           w_in, b_igate, b_fgate, mlstm_norm, rel_bias, w_out, norm1, norm2, w_up, w_down, norm_f):
    depth = w_in.shape[0]
    xp, xs = x_prompt, x_sample
    outs_p, outs_s = [], []
    for l in range(depth):
        packed = _pack_layer_weights(w_in[l], b_igate[l], b_fgate[l], w_out[l], w_up[l], w_down[l])
        last = l == depth - 1
        common = (packed, norm1[l], norm2[l], mlstm_norm[l], rel_bias, norm_f, last)
        rp = _layer(xp, *common, None)
        rs = _layer(xs, *common, (page_table, cache_k[l], cache_v[l], cache_kidx[l],
                                  state_C[l], state_n[l], state_m[l]))
        xp, xs = rp[0], rs[0]
        outs_p.append(rp[1:])
        outs_s.append(rs[1:])
    stack = lambda outs, i: jnp.stack([o[i] for o in outs])
    return ((xp, xs) + tuple(stack(outs_p, i) for i in range(6))
            + tuple(stack(outs_s, i) for i in range(6)))
```

```python
import functools
import math

import numpy as np
import jax
import jax.numpy as jnp
from jax import lax
from jax.experimental import pallas as pl
from jax.experimental.pallas import tpu as pltpu

F32 = jnp.float32
BF16 = jnp.bfloat16
I32 = jnp.int32

N_HEADS = 8
HEAD_DIM = 64
N_KV_HEADS = 2
GROUP = N_HEADS // N_KV_HEADS
N_IDX_HEADS = 8
IDX_DIM = 64
TOPK_MAX = 256
N_BUCKETS = 32
MAX_DISTANCE = 128
M_HEADS = 4
M_HEAD_DIM = 128
PAGE_SIZE = 128
EPS = 1e-6
NEG = -1e30
LOG2E = math.log2(math.e)
ATT_WIDTH = N_HEADS * HEAD_DIM
M_WIDTH = M_HEADS * M_HEAD_DIM

LANES = 128
SUBLANES = 8
VMEM_LIMIT = 56 * 1024 * 1024

C_QA = 0
C_V = C_QA + ATT_WIDTH
C_QI = C_V + LANES
C_K = C_QI + N_IDX_HEADS * IDX_DIM
C_KI2 = C_K + LANES
C_MISC = C_KI2 + LANES
C_QM = C_MISC + LANES
C_KM = C_QM + M_WIDTH
C_VM = C_KM + M_WIDTH
C_OM = C_VM + M_WIDTH
N_PACK = C_OM + M_WIDTH
MISC_W = 0
MISC_I = 8
MISC_F = 12

HEAD_PERM = (0, 4, 1, 5, 2, 6, 3, 7)

TQ = 256
STRIP_BACK = 2
assert LANES * (STRIP_BACK - 1) >= MAX_DISTANCE
CKP = 256
VT_ROWS = LANES + 16
ML = 256

NT_DIMS = (((1,), (1,)), ((), ()))
TN_DIMS = (((0,), (0,)), ((), ()))


def _bucket_bounds():
    max_exact = N_BUCKETS // 2
    scale = (N_BUCKETS - max_exact) / math.log(MAX_DISTANCE / max_exact)

    def bucket(n, dt):
        if n < max_exact:
            return n
        val = np.log(np.asarray(max(n, 1), dt) / dt(max_exact)) * dt(scale)
        return min(max_exact + int(val), N_BUCKETS - 1)

    table = [bucket(n, np.float32) for n in range(MAX_DISTANCE + 2)]
    assert table == [bucket(n, np.float64) for n in range(MAX_DISTANCE + 2)]
    assert table[MAX_DISTANCE] == N_BUCKETS - 1
    return [next(d for d, b in enumerate(table) if b >= k) for k in range(N_BUCKETS)]


BUCKET_BOUNDS = _bucket_bounds()


def _cparams(n_axes):
    return pltpu.CompilerParams(dimension_semantics=("arbitrary",) * n_axes,
                                vmem_limit_bytes=VMEM_LIMIT)


def _const_spec(shape):
    nd = len(shape)
    return pl.BlockSpec(shape, lambda *_: (0,) * nd, pipeline_mode=pl.Buffered(1))


def _rms(x, g):
    return x * lax.rsqrt(jnp.mean(x * x, axis=-1, keepdims=True) + EPS) * g


def _inproj_mlstm(mm, misc_ref, qm_ref, km_ref, vm_ref, om_ref):
    misc_ref[...] = mm(C_MISC, LANES)
    qm_ref[...] = mm(C_QM, M_WIDTH).astype(qm_ref.dtype)
    km_ref[...] = (mm(C_KM, M_WIDTH) * (M_HEAD_DIM ** -0.5)).astype(km_ref.dtype)
    vm_ref[...] = mm(C_VM, M_WIDTH).astype(vm_ref.dtype)
    om_ref[...] = mm(C_OM, M_WIDTH)


def _inproj_rows_kernel(x_ref, g_ref, w_ref, misc_ref, qm_ref, km_ref, vm_ref, om_ref,
                        qa_ref, qi_ref, k_ref, v_ref, ki_ref):
    ub = _rms(x_ref[...], g_ref[...]).astype(BF16)
    z = jnp.dot(ub, w_ref[...], preferred_element_type=F32)
    mm = lambda c0, n: z[:, c0:c0 + n]
    _inproj_mlstm(mm, misc_ref, qm_ref, km_ref, vm_ref, om_ref)
    qa_ref[...] = mm(C_QA, ATT_WIDTH) * (HEAD_DIM ** -0.5)
    qi_ref[...] = mm(C_QI, N_IDX_HEADS * IDX_DIM)
    k_ref[...] = mm(C_K, LANES)
    v_ref[...] = mm(C_V, LANES)
    ki_ref[...] = mm(C_KI2, LANES)[:, :IDX_DIM]


R_K = 0
R_V = R_K + LANES
R_KI = R_V + LANES
R_QA = R_KI + LANES
R_QI = R_QA + ATT_WIDTH
R_MISC = R_QI + N_IDX_HEADS * IDX_DIM
N_TPACK = R_MISC + LANES


def _inproj_cols_kernel(x_ref, g_ref, w_ref, wt_ref, misc_ref, qm_ref, km_ref, vm_ref, om_ref,
                        kb_ref, kib_ref, kt_ref, vt_ref, kit_ref, vtb_ref, qat_ref, qit_ref,
                        misct_ref):
    ub = _rms(x_ref[...], g_ref[...]).astype(BF16)
    z = jnp.dot(ub, w_ref[:, C_K:], preferred_element_type=F32)
    mm = lambda c0, n: z[:, c0 - C_K:c0 - C_K + n]
    _inproj_mlstm(mm, misc_ref, qm_ref, km_ref, vm_ref, om_ref)
    kb_ref[...] = mm(C_K, LANES).astype(BF16)
    kib_ref[...] = mm(C_KI2, LANES)[:, :IDX_DIM].astype(BF16)

    zt = lax.dot_general(wt_ref[...], ub, NT_DIMS, preferred_element_type=F32)
    mt = lambda r0, n: zt[r0:r0 + n]

    kt_ref[0] = mt(R_K, LANES)
    vt = mt(R_V, LANES)
    vt_ref[0] = vt
    ones = jnp.ones((VT_ROWS - LANES, CKP), BF16)
    for j in range(vtb_ref.shape[1]):
        vtb_ref[0, j] = jnp.concatenate([vt[:, j * CKP:(j + 1) * CKP].astype(BF16), ones], axis=0)
    kit_ref[0] = mt(R_KI, IDX_DIM)
    qat_ref[0] = (mt(R_QA, ATT_WIDTH) * (HEAD_DIM ** -0.5 * LOG2E)).astype(BF16)
    qit_ref[0] = mt(R_QI, N_IDX_HEADS * IDX_DIM).astype(BF16)
    misct_ref[0] = mt(R_MISC, LANES)


def _inproj(x2, g1, wp, wt, B, T, tm):
    R, D = x2.shape
    assert R == B * T and R % tm == 0
    mdt = F32 if wt is None else BF16
    row = lambda i: (i, 0)
    outs = [(LANES, F32), (M_WIDTH, mdt), (M_WIDTH, mdt), (M_WIDTH, mdt), (M_WIDTH, F32)]
    in_specs = [pl.BlockSpec((tm, D), row), _const_spec((1, D)), _const_spec((D, N_PACK))]
    if wt is None:
        kern, args = _inproj_rows_kernel, (x2, g1, wp)
        outs += [(ATT_WIDTH, F32), (N_IDX_HEADS * IDX_DIM, F32), (LANES, F32), (LANES, F32), (IDX_DIM, F32)]
    else:
        kern, args = _inproj_cols_kernel, (x2, g1, wp, wt)
        in_specs.append(_const_spec(wt.shape))
        outs += [(LANES, BF16), (IDX_DIM, BF16)]
    out_specs = [pl.BlockSpec((tm, w), row) for w, _ in outs]
    out_shape = [jax.ShapeDtypeStruct((R, w), dt) for w, dt in outs]
    if wt is not None:
        assert T % tm == 0 and tm % CKP == 0
        tpb, cpt = T // tm, tm // CKP
        cols = lambda i: (i // tpb, 0, i % tpb)
        for w, dt in ((LANES, F32), (LANES, F32), (IDX_DIM, F32)):
            out_specs.append(pl.BlockSpec((1, w, tm), cols))
            out_shape.append(jax.ShapeDtypeStruct((B, w, T), dt))
        out_specs.append(pl.BlockSpec((1, cpt, VT_ROWS, CKP), lambda i: (i // tpb, i % tpb, 0, 0)))
        out_shape.append(jax.ShapeDtypeStruct((B, T // CKP, VT_ROWS, CKP), BF16))
        for w, dt in ((ATT_WIDTH, BF16), (N_IDX_HEADS * IDX_DIM, BF16), (LANES, F32)):
            out_specs.append(pl.BlockSpec((1, w, tm), cols))
            out_shape.append(jax.ShapeDtypeStruct((B, w, T), dt))
    return pl.pallas_call(
        kern,
        grid=(R // tm,),
        in_specs=in_specs,
        out_specs=out_specs,
        out_shape=out_shape,
        compiler_params=_cparams(1),
        name="inproj",
    )(*args)


def _post_kernel(ff_chunk, final_norm, x_ref, a_ref, h_ref, woa_ref, woh_ref, g2_ref, wup_ref,
                 wdn_ref, gf_ref, y_ref):
    mix = jnp.dot(a_ref[...].astype(BF16), woa_ref[...], preferred_element_type=F32)
    mix = mix + jnp.dot(h_ref[...].astype(BF16), woh_ref[...], preferred_element_type=F32)
    hres = x_ref[...] + mix
    f = _rms(hres, g2_ref[...]).astype(BF16)
    acc = hres
    for c0 in range(0, wup_ref.shape[1], ff_chunk):
        up = jnp.dot(f, wup_ref[:, c0:c0 + ff_chunk], preferred_element_type=F32)
        r = jnp.maximum(up, 0.0)
        acc = acc + jnp.dot((r * r).astype(BF16), wdn_ref[c0:c0 + ff_chunk, :],
                            preferred_element_type=F32)
    y_ref[...] = _rms(acc, gf_ref[...]) if final_norm else acc


def _post(x2, attn, h, woa, woh, g2, wup, wdn, gf, final_norm, tm):
    R, D = x2.shape
    dff = wup.shape[1]
    assert R % tm == 0
    row = lambda i: (i, 0)
    return pl.pallas_call(
        functools.partial(_post_kernel, min(dff, 1024), final_norm),
        grid=(R // tm,),
        in_specs=[pl.BlockSpec((tm, D), row), pl.BlockSpec((tm, ATT_WIDTH), row),
                  pl.BlockSpec((tm, M_WIDTH), row), _const_spec(woa.shape), _const_spec(woh.shape),
                  _const_spec((1, D)), _const_spec(wup.shape), _const_spec(wdn.shape),
                  _const_spec((1, D))],
        out_specs=pl.BlockSpec((tm, D), row),
        out_shape=jax.ShapeDtypeStruct((R, D), F32),
        compiler_params=_cparams(1),
        name="post",
    )(x2, attn, h, woa, woh, g2, wup, wdn, gf)


def _log_sigmoid(x):
    return -(jnp.maximum(-x, 0.0) + jnp.log1p(jnp.exp(-jnp.abs(x))))


def _mlstm_kernel(nvalid, has_state, *refs):
    if has_state:
        (q_ref, k_ref, v_ref, o_ref, misc_ref, gb_ref, mn_ref, c0_ref, n0_ref, m0_ref,
         h_ref, c_ref, n_ref, m_ref, z_scr, rows_scr) = refs
    else:
        (q_ref, k_ref, v_ref, o_ref, misc_ref, gb_ref, mn_ref,
         h_ref, c_ref, n_ref, m_ref, z_scr, rows_scr) = refs
    nb, nc, L = z_scr.shape[0], z_scr.shape[1], z_scr.shape[2]
    c = pl.program_id(1)

    def padded(x, dt):
        x = x.astype(dt)
        if nvalid == L:
            return x
        return jnp.concatenate([x, jnp.zeros((L - nvalid, x.shape[1]), dt)], axis=0)

    r2 = lax.broadcasted_iota(I32, (L, L), 0)
    c2 = lax.broadcasted_iota(I32, (L, L), 1)
    tril = r2 >= c2

    @pl.when(c == 0)
    def _():
        if has_state:
            c_ref[...] = c0_ref[...]
            n_ref[...] = n0_ref[...]
            m_ref[...] = m0_ref[...]
        else:
            c_ref[...] = jnp.zeros_like(c_ref)
            n_ref[...] = jnp.zeros_like(n_ref)
            m_ref[...] = jnp.zeros_like(m_ref)
        assert MISC_F == MISC_I + M_HEADS and MISC_I % SUBLANES == 0
        row8 = lax.broadcasted_iota(I32, (SUBLANES, L), 0)
        tok8 = lax.broadcasted_iota(I32, (SUBLANES, L), 1)
        bias8 = jnp.concatenate([gb_ref[...]] * (L // LANES), axis=1)
        triu = (r2 <= c2).astype(F32)
        for s in range(nb):
            for cc in range(nc):
                misc_t = padded(misc_ref[s, cc * nvalid:(cc + 1) * nvalid, :], F32).T
                gx = misc_t[MISC_I:MISC_I + SUBLANES] + bias8
                gates = jnp.where(row8 >= M_HEADS, _log_sigmoid(gx), gx)
                if nvalid != L:
                    gates = jnp.where(tok8 < nvalid, gates, jnp.where(row8 >= M_HEADS, 0.0, NEG))
                cum = jnp.dot(gates, triu, precision=lax.Precision.HIGHEST,
                              preferred_element_type=F32)
                rows = jnp.where(row8 >= M_HEADS, cum, gates)
                rows_scr[s, cc] = rows
                z_scr[s, cc] = jnp.concatenate([rows, jnp.zeros((LANES - SUBLANES, L), F32)], axis=0).T

    loaded = []
    for s in range(nb):
        loaded.append((z_scr[s, c], rows_scr[s, c], padded(q_ref[s], BF16), padded(k_ref[s], BF16),
                       padded(v_ref[s], BF16), padded(o_ref[s], F32), m_ref[s], c_ref[s], n_ref[s]))
    chains = [(s, hd) for s in range(nb) for hd in range(M_HEADS)]

    def operands(s, hd):
        z, rows, qb, kb, vb, ob, m_all, c_all, n_all = loaded[s]
        sl = slice(hd * M_HEAD_DIM, (hd + 1) * M_HEAD_DIM)
        return dict(
            sl=sl, q=qb[:, sl], k=kb[:, sl], v=vb[:, sl], o=ob[:, sl],
            icol=z[:, hd:hd + 1], bcol=z[:, M_HEADS + hd:M_HEADS + hd + 1],
            irow=rows[hd:hd + 1, :], brow=rows[M_HEADS + hd:M_HEADS + hd + 1, :],
            m_prev=m_all[hd:hd + 1, 0:1], s_prev=c_all[hd], n_prev=n_all[hd:hd + 1, :])

    def lane_sum(x):
        ones = jnp.ones((x.shape[1], LANES), BF16)
        head = x.astype(BF16)
        tail = (x - head.astype(F32)).astype(BF16)
        return (jnp.dot(head, ones, preferred_element_type=F32)
                + jnp.dot(tail, ones, preferred_element_type=F32))

    st = [operands(s, hd) for s, hd in chains]
    for x in st:
        x["qk"] = lax.dot_general(x["q"], x["k"], NT_DIMS, preferred_element_type=F32)
        x["qs"] = jnp.dot(x["q"], x["s_prev"].astype(BF16), preferred_element_type=F32)
        x["qn"] = lane_sum(x["q"].astype(F32) * x["n_prev"])
    for x in st:
        g = x["bcol"] + x["m_prev"]
        dm = jnp.where(tril, x["bcol"] - x["brow"] + x["irow"], NEG)
        x["mt"] = jnp.maximum(g, jnp.max(dm, axis=1, keepdims=True))
        x["gw"] = jnp.exp(g - x["mt"])
        x["qk"] = x["qk"] * jnp.exp(dm - x["mt"])
        b_last = x["bcol"][L - 1:L, :]
        g_last = b_last + x["m_prev"]
        a = b_last - x["bcol"] + x["icol"]
        x["m_new"] = jnp.maximum(g_last, jnp.max(a, axis=0, keepdims=True))
        x["sw"] = jnp.exp(g_last - x["m_new"])
        x["ak"] = jnp.exp(a - x["m_new"]) * x["k"].astype(F32)
    for x in st:
        x["pv"] = jnp.dot(x["qk"].astype(BF16), x["v"], preferred_element_type=F32)
        x["qksum"] = lane_sum(x["qk"])
        x["kv"] = lax.dot_general(x["ak"].astype(BF16), x["v"], TN_DIMS, preferred_element_type=F32)
    stores = []
    for (s, hd), x in zip(chains, st):
        num = x["gw"] * x["qs"] + x["pv"]
        den = x["gw"] * x["qn"] + x["qksum"]
        hh = num / jnp.maximum(jnp.abs(den), jnp.exp(-x["mt"]))
        hh = hh * lax.rsqrt(lane_sum(hh * hh) * (1.0 / M_HEAD_DIM) + EPS)
        hh = hh * mn_ref[:, x["sl"]] * jax.nn.sigmoid(x["o"])
        c_new = x["sw"] * x["s_prev"] + x["kv"]
        n_new = x["sw"] * x["n_prev"] + jnp.sum(x["ak"], axis=0, keepdims=True)
        stores.append((s, hd, x["sl"], hh[:nvalid].astype(h_ref.dtype), c_new, n_new,
                       jnp.broadcast_to(x["m_new"], (1, LANES))))
    for s, hd, sl, h_new, c_new, n_new, m_new in stores:
        h_ref[s, :, sl] = h_new
        c_ref[s, hd] = c_new
        n_ref[s, hd:hd + 1, :] = n_new
        m_ref[s, hd:hd + 1, :] = m_new


def _mlstm(qm, km, vm, om, misc, gate_bias, mnorm, state, B, T, hdt):
    L = next((c for c in (ML, LANES) if T % c == 0), LANES)
    nvalid = L if T % L == 0 else T
    assert nvalid <= L and T % nvalid == 0 and nvalid % SUBLANES == 0
    nc = T // nvalid
    nb = next(n for n in ((2, 1) if nvalid == L else (4, 2, 1)) if B % n == 0)
    seq3 = lambda a: a.reshape(B, T, a.shape[-1])
    blk = lambda w: pl.BlockSpec((nb, nvalid, w), lambda b, c: (b, c, 0))
    st_specs = [pl.BlockSpec((nb, M_HEADS, M_HEAD_DIM, M_HEAD_DIM), lambda b, c: (b, 0, 0, 0)),
                pl.BlockSpec((nb, M_HEADS, M_HEAD_DIM), lambda b, c: (b, 0, 0)),
                pl.BlockSpec((nb, M_HEADS, LANES), lambda b, c: (b, 0, 0))]
    in_specs = [blk(M_WIDTH), blk(M_WIDTH), blk(M_WIDTH), blk(M_WIDTH),
                pl.BlockSpec((nb, T, LANES), lambda b, c: (b, 0, 0)),
                pl.BlockSpec((SUBLANES, LANES), lambda b, c: (0, 0)),
                pl.BlockSpec((1, M_WIDTH), lambda b, c: (0, 0))]
    args = [seq3(qm), seq3(km), seq3(vm), seq3(om), seq3(misc), gate_bias, mnorm]
    if state is not None:
        in_specs += st_specs
        args += list(state)
    h, c_new, n_new, m_new = pl.pallas_call(
        functools.partial(_mlstm_kernel, nvalid, state is not None),
        grid=(B // nb, nc),
        in_specs=in_specs,
        out_specs=[blk(M_WIDTH)] + st_specs,
        out_shape=[jax.ShapeDtypeStruct((B, T, M_WIDTH), hdt),
                   jax.ShapeDtypeStruct((B, M_HEADS, M_HEAD_DIM, M_HEAD_DIM), F32),
                   jax.ShapeDtypeStruct((B, M_HEADS, M_HEAD_DIM), F32),
                   jax.ShapeDtypeStruct((B, M_HEADS, LANES), F32)],
        scratch_shapes=[pltpu.VMEM((nb, nc, L, LANES), F32), pltpu.VMEM((nb, nc, SUBLANES, L), F32)],
        compiler_params=_cparams(2),
        name="mlstm",
    )(*args)
    return h.reshape(B * T, M_WIDTH), c_new, n_new, m_new


def _sortable_key(score):
    bits = lax.bitcast_convert_type(score, I32)
    return bits ^ (lax.shift_right_arithmetic(bits, 31) & 0x7FFFFFFF)


def _key_to_score(key):
    return lax.bitcast_convert_type(key ^ (lax.shift_right_arithmetic(key, 31) & 0x7FFFFFFF), F32)


def _build_bias_strip(strip_ref, rb_ref, off, key_axis, log2_relative=False):
    ntiles, tile = strip_ref.shape[1], strip_ref.shape[2:]
    i = lax.broadcasted_iota(I32, tile, 1 - key_axis)
    x = lax.broadcasted_iota(I32, tile, key_axis)

    def entry(b, h):
        if log2_relative:
            return (rb_ref[b, h] - rb_ref[N_BUCKETS - 1, h]) * LOG2E
        return rb_ref[b, h]

    for t in range(ntiles):
        dist = i + (off - LANES * t) - x
        for h in range(N_HEADS):
            val = jnp.full(tile, entry(0, h), F32)
            for b in range(1, N_BUCKETS):
                val = jnp.where(dist >= BUCKET_BOUNDS[b], entry(b, h), val)
            strip_ref[h, t] = val


def _counter(score_ref, nk, ck, key_axis):
    def reduce(fn, fold, fold_all, init):
        def body(c, acc):
            c0 = pl.multiple_of(c * ck, ck)
            sc = score_ref[:, pl.ds(c0, ck)] if key_axis == 1 else score_ref[pl.ds(c0, ck), :]
            val = fn(sc, c0 + lax.broadcasted_iota(I32, sc.shape, key_axis))
            if key_axis == 1:
                for j in range(ck // LANES):
                    acc = fold(acc, val[:, j * LANES:(j + 1) * LANES])
                return acc
            return fold(acc, fold_all(val.reshape(ck // SUBLANES, SUBLANES, val.shape[1]), axis=0))

        nq = score_ref.shape[1 - key_axis]
        acc0 = jnp.full((nq, LANES) if key_axis == 1 else (SUBLANES, nq), init, I32)
        return fold_all(lax.fori_loop(0, nk, body, acc0), axis=key_axis, keepdims=True)

    def count(pred):
        return reduce(lambda sc, idx: jnp.where(pred(sc, idx), 1, 0), jnp.add, jnp.sum, 0)

    def lowest(fn):
        return reduce(fn, jnp.minimum, jnp.min, jnp.iinfo(jnp.int32).max)

    return count, lowest


IMIN = jnp.iinfo(jnp.int32).min


def _kth_largest_by_count(count, qshape, n_sel):
    def bit_step(i, key):
        cand = key + lax.shift_left(jnp.int32(1), 31 - i)
        cand_score = _key_to_score(cand)
        return jnp.where(count(lambda sc, idx: sc >= cand_score) >= n_sel, cand, key)

    return _key_to_score(lax.fori_loop(0, 32, bit_step, jnp.full(qshape, IMIN, I32)))


def _bit_planes(words):
    a = list(words)
    j, m = 16, 0x0000FFFF
    while j:
        k = 0
        while k < 32:
            t = (a[k] ^ lax.shift_right_logical(a[k + j], j)) & m
            a[k] = a[k] ^ t
            a[k + j] = a[k + j] ^ lax.shift_left(t, j)
            k = (k + j + 1) & ~j
        j >>= 1
        m = (m ^ (m << j)) & 0xFFFFFFFF
        m = m - (1 << 32) if m >= (1 << 31) else m
    return a[::-1]


def _kth_largest_by_planes(score_ref, planes_ref, nk, ck, n_sel):
    nc = planes_ref.shape[1]
    nq = score_ref.shape[1]
    assert ck == 32 * SUBLANES

    def pack_chunk(c, _):
        c0 = pl.multiple_of(c * ck, ck)
        u = _sortable_key(score_ref[pl.ds(c0, ck), :]) ^ IMIN
        u = u.reshape(32, SUBLANES, nq)
        for b, plane in enumerate(_bit_planes([u[v] for v in range(32)])):
            planes_ref[b, c] = plane
        return 0

    lax.fori_loop(0, nk, pack_chunk, 0)
    cand0 = tuple(jnp.where(c < nk, jnp.full((SUBLANES, nq), -1, I32), 0) for c in range(nc))
    return _plane_search(planes_ref, cand0, n_sel, 0)


def _kth_largest_by_planes_lanes(score_ref, planes_ref, n_sel):
    nq, ntiles = score_ref.shape[0], score_ref.shape[1] // LANES
    cand0 = []
    for g in range(planes_ref.shape[1]):
        real = min(32, ntiles - 32 * g)
        words = [_sortable_key(score_ref[:, (32 * g + v) * LANES:(32 * g + v + 1) * LANES]) ^ IMIN
                 if v < real else jnp.zeros((nq, LANES), I32) for v in range(32)]
        for b, plane in enumerate(_bit_planes(words)):
            planes_ref[b, g] = plane
        cand0.append(jnp.full((nq, LANES), -(1 << (32 - real)), I32))
    return _plane_search(planes_ref, tuple(cand0), n_sel, 1)


def _plane_search(planes_ref, cand0, n_sel, key_axis):
    def bit_step(i, carry):
        cand, n_above, thr_u = carry
        b = 31 - i
        ones = [m & planes_ref[b, c] for c, m in enumerate(cand)]
        pop = lax.population_count(ones[0])
        for o in ones[1:]:
            pop = pop + lax.population_count(o)
        tot = jnp.sum(pop, axis=key_axis, keepdims=True)
        take = n_above + tot >= n_sel
        cand = tuple(jnp.where(take, o, m ^ o) for o, m in zip(ones, cand))
        n_above = jnp.where(take, n_above, n_above + tot)
        thr_u = thr_u | jnp.where(take, lax.shift_left(jnp.int32(1), b), 0)
        return cand, n_above, thr_u

    zero = jnp.zeros_like(jnp.sum(cand0[0], axis=key_axis, keepdims=True))
    _, _, thr_u = lax.fori_loop(0, 32, bit_step, (cand0, zero, zero))
    return _key_to_score(thr_u ^ IMIN)


def _select_topk(thr, check, reducers, qshape, n_sel, idx_bits):
    count, lowest = reducers
    imax = jnp.iinfo(jnp.int32).max

    def with_counts(t):
        return t, count(lambda sc, idx: sc > t), count(lambda sc, idx: sc >= t)

    thr, n_gt, n_ge = with_counts(thr)
    if check:
        good = jnp.min(jnp.where(n_gt < n_sel, jnp.where(n_ge >= n_sel, 1, 0), 0)) > 0
        thr, n_gt, n_ge = lax.cond(
            good, lambda _: (thr, n_gt, n_ge),
            lambda _: with_counts(_kth_largest_by_count(count, qshape, n_sel)), 0)
    split = n_ge > n_sel
    need = jnp.where(split, n_sel - n_gt, 0)
    most = jnp.max(need)

    def by_extraction(_):
        def step(k, last):
            nxt = lowest(lambda sc, idx: jnp.where(sc == thr, jnp.where(idx > last, idx, imax), imax))
            return jnp.where(k < need, nxt, last)

        return lax.fori_loop(0, most, step, jnp.full(qshape, -1, I32))

    def by_bisection(_):
        def idx_step(i, lo):
            cand = lo + lax.shift_left(jnp.int32(1), idx_bits - 1 - i)
            cnt = count(lambda sc, idx: jnp.where(sc == thr, idx, imax) < cand)
            return jnp.where(cnt < need, cand, lo)

        return lax.fori_loop(0, idx_bits, idx_step, jnp.zeros(qshape, I32))

    jstar = lax.cond(most <= idx_bits, by_extraction, by_bisection, 0)
    return thr, jnp.where(split, jstar, imax)


def _valid_mask(scores, idx, thr, jstar, qpos):
    sel = jnp.where(scores > thr, 1, jnp.where(scores == thr, jnp.where(idx <= jstar, 1, 0), 0))
    return jnp.where(idx <= qpos, sel, 0) > 0


def _group_queries(qa, tq):
    lane = lax.broadcasted_iota(I32, (tq, LANES), 1)
    out = []
    for n in range(N_KV_HEADS):
        keep = (lane < HEAD_DIM) if n == 0 else (lane >= HEAD_DIM)
        tiles = [jnp.where(keep, qa[:, j * LANES:(j + 1) * LANES], jnp.zeros((), qa.dtype))
                 for j in range(GROUP)]
        out.append(jnp.concatenate(tiles, axis=0).astype(BF16))
    return out


def _write_attn(out_ref, carries, tq):
    lane = lax.broadcasted_iota(I32, (tq, LANES), 1)
    res = [acc / l for (_, l, acc) in carries]
    for j in range(GROUP):
        tile = jnp.where(lane < HEAD_DIM, res[0][j * tq:(j + 1) * tq], res[1][j * tq:(j + 1) * tq])
        out_ref[:, j * LANES:(j + 1) * LANES] = tile.astype(out_ref.dtype)


def _dsa_prompt_kernel(n_sel, idx_bits, rb_ref, qat_ref, qit_ref, misct_ref, ki_ref, k_ref, vt_ref,
                       out_ref, score_ref, planes_ref, strip_ref, sa_ref, sb_ref, m_ref, acc_ref):
    qb = pl.program_id(1)
    q0 = qb * TQ
    nk = (q0 + TQ + CKP - 1) // CKP
    tiles_per_chunk = CKP // LANES
    back_tiles = STRIP_BACK
    strip_off = LANES * back_tiles

    @pl.when((pl.program_id(0) == 0) & (qb == 0))
    def _():
        _build_bias_strip(strip_ref, rb_ref, strip_off, 0, log2_relative=True)
        planes_ref[...] = jnp.zeros(planes_ref.shape, I32)

    qpos = q0 + lax.broadcasted_iota(I32, (1, TQ), 1)

    qit = qit_ref[0]
    qstack = jnp.concatenate([qit[h * IDX_DIM:(h + 1) * IDX_DIM] for h in range(N_IDX_HEADS)],
                             axis=1)
    w = misct_ref[0, MISC_W:MISC_W + N_IDX_HEADS, :] * (N_IDX_HEADS ** -0.5 * IDX_DIM ** -0.5)

    def score_chunk(c, _):
        c0 = pl.multiple_of(c * CKP, CKP)
        d = jnp.dot(ki_ref[pl.ds(c0, CKP), :], qstack, preferred_element_type=F32)
        d = jnp.maximum(d, 0.0)
        s = jnp.zeros((CKP, TQ), F32)
        for h in range(N_IDX_HEADS):
            s = s + d[:, h * TQ:(h + 1) * TQ] * w[h:h + 1, :]
        idx = c0 + lax.broadcasted_iota(I32, (CKP, TQ), 0)
        score_ref[pl.ds(c0, CKP), :] = jnp.where(idx <= qpos, s, NEG)
        return 0

    lax.fori_loop(0, nk, score_chunk, 0)

    thr = _kth_largest_by_planes(score_ref, planes_ref, nk, CKP, n_sel)
    thr, jstar = _select_topk(thr, True, _counter(score_ref, nk, CKP, 0), (1, TQ), n_sel, idx_bits)

    qat = qat_ref[0]
    zeros = jnp.zeros((HEAD_DIM, TQ), qat.dtype)
    qgroups = []
    for n in range(N_KV_HEADS):
        tiles = []
        for g in range(GROUP):
            h = n * GROUP + g
            x = qat[h * HEAD_DIM:(h + 1) * HEAD_DIM]
            tiles.append(jnp.concatenate([x, zeros] if n == 0 else [zeros, x], axis=0))
        qgroups.append(jnp.concatenate(tiles, axis=1))

    def logits_into(s_ref, c):
        kc = k_ref[pl.ds(pl.multiple_of(c * CKP, CKP), CKP), :]
        for n in range(N_KV_HEADS):
            s_ref[n] = jnp.dot(kc, qgroups[n], preferred_element_type=F32)

    def attend_chunk(c, s_cur_ref, s_next_ref):
        if s_next_ref is not None:
            logits_into(s_next_ref, jnp.minimum(c + 1, nk - 1))
        c0 = pl.multiple_of(c * CKP, CKP)
        idx = c0 + lax.broadcasted_iota(I32, (CKP, TQ), 0)
        valid = _valid_mask(score_ref[pl.ds(c0, CKP), :], idx, thr, jstar, qpos)
        vct = vt_ref[0, c]
        tiles = [jnp.maximum(back_tiles + j - (qb * (TQ // LANES) - c * tiles_per_chunk), 0)
                 for j in range(tiles_per_chunk)]
        for n in range(N_KV_HEADS):
            m_old = m_ref[n]
            parts = []
            for g in range(GROUP):
                bias = jnp.concatenate([strip_ref[n * GROUP + g, t] for t in tiles], axis=0)
                parts.append(jnp.where(valid, s_cur_ref[n, :, g * TQ:(g + 1) * TQ] + bias, NEG))
            sm = jnp.concatenate(parts, axis=1)
            m_new = jnp.maximum(m_old, jnp.max(sm, axis=0, keepdims=True))
            p = jnp.exp2((sm - m_new).astype(BF16))
            acc_ref[n] = jnp.exp2(m_old - m_new) * acc_ref[n] + jnp.dot(vct, p, preferred_element_type=F32)
            m_ref[n] = m_new

    m_ref[...] = jnp.full(m_ref.shape, NEG, F32)
    acc_ref[...] = jnp.zeros(acc_ref.shape, F32)
    logits_into(sa_ref, 0)

    def attend_pair(i, _):
        attend_chunk(2 * i, sa_ref, sb_ref)
        attend_chunk(2 * i + 1, sb_ref, sa_ref)
        return 0

    lax.fori_loop(0, nk // 2, attend_pair, 0)

    @pl.when(nk % 2 == 1)
    def _():
        attend_chunk(nk - 1, sa_ref, None)

    carries = [(None, acc_ref[n]) for n in range(N_KV_HEADS)]
    res = [acc[:LANES] / acc[LANES:LANES + 1] for (_, acc) in carries]
    row = lax.broadcasted_iota(I32, (LANES, TQ), 0)
    for j in range(GROUP):
        cols = slice(j * TQ, (j + 1) * TQ)
        tile_t = jnp.where(row < HEAD_DIM, res[0][:, cols], res[1][:, cols])
        out_ref[:, j * LANES:(j + 1) * LANES] = tile_t.T.astype(out_ref.dtype)


def _dsa_prompt(rel_bias, qat, qit, misct, kib, kb, vtb, B, T):
    assert T % CKP == 0 and T % TQ == 0
    nq = T // TQ
    n_sel = min(TOPK_MAX, T // 4)
    qcols = lambda w: pl.BlockSpec((1, w, TQ), lambda b, q: (b, 0, q))
    seq = lambda w: pl.BlockSpec((T, w), lambda b, q: (b, 0))
    return pl.pallas_call(
        functools.partial(_dsa_prompt_kernel, n_sel, max(1, (T - 1).bit_length())),
        grid=(B, nq),
        in_specs=[pl.BlockSpec(memory_space=pltpu.SMEM), qcols(ATT_WIDTH), qcols(N_IDX_HEADS * IDX_DIM),
                  qcols(LANES), seq(IDX_DIM), seq(LANES),
                  pl.BlockSpec((1, T // CKP, VT_ROWS, CKP), lambda b, q: (b, 0, 0, 0))],
        out_specs=pl.BlockSpec((TQ, ATT_WIDTH), lambda b, q: (b * nq + q, 0)),
        out_shape=jax.ShapeDtypeStruct((B * T, ATT_WIDTH), BF16),
        scratch_shapes=[pltpu.VMEM((T, TQ), F32),
                        pltpu.VMEM((32, T // CKP, SUBLANES, TQ), I32),
                        pltpu.VMEM((N_HEADS, STRIP_BACK + max(TQ, CKP) // LANES, LANES, TQ), F32),
                        pltpu.VMEM((N_KV_HEADS, CKP, GROUP * TQ), F32),
                        pltpu.VMEM((N_KV_HEADS, CKP, GROUP * TQ), F32),
                        pltpu.VMEM((N_KV_HEADS, 1, GROUP * TQ), F32),
                        pltpu.VMEM((N_KV_HEADS, VT_ROWS, GROUP * TQ), F32)],
        compiler_params=_cparams(2),
        name="dsa_prompt",
    )(rel_bias, qat, qit, misct, kib, kb, vtb)


def _page_pipeline(pt_ref, n_pages, caches, bufs, sems):
    def copies(bb, sl, j):
        pid = pt_ref[bb, j]
        cols = pl.ds(pl.multiple_of(j * PAGE_SIZE, PAGE_SIZE), PAGE_SIZE)
        return [pltpu.make_async_copy(c.at[pid], buf.at[sl, :, cols], sems.at[sl, i])
                for i, (c, buf) in enumerate(zip(caches, bufs))]

    def start_all(bb, sl):
        def body(j, _):
            for cp in copies(bb, sl, j):
                cp.start()
            return 0
        lax.fori_loop(0, n_pages, body, 0)

    def wait_all(bb, sl):
        def body(j, _):
            for cp in copies(bb, sl, j):
                cp.wait()
            return 0
        lax.fori_loop(0, n_pages, body, 0)

    def step():
        b = pl.program_id(0)
        slot = b % 2

        @pl.when(b == 0)
        def _():
            start_all(0, 0)

        @pl.when(b + 1 < pl.num_programs(0))
        def _():
            start_all(b + 1, 1 - slot)

        wait_all(b, slot)
        return slot

    return step


def _pad_rows(x, rows):
    return jnp.concatenate([x, jnp.zeros((rows - x.shape[0], x.shape[1]), x.dtype)], axis=0)


def _sample_score_kernel(n_sel, idx_bits, n_pages, ts, ck, pt_ref, qi_ref, misc_ref, kin_ref,
                         ckit_hbm, score_ref, thr_ref, jst_ref, ki_buf, sems, planes_ref):
    b = pl.program_id(0)
    past = n_pages * PAGE_SIZE
    slot = _page_pipeline(pt_ref, n_pages, [ckit_hbm], [ki_buf], sems)()

    qi = qi_ref[...]
    qstack = jnp.concatenate([qi[:, h * IDX_DIM:(h + 1) * IDX_DIM] for h in range(N_IDX_HEADS)],
                             axis=0).astype(BF16)
    w = misc_ref[:, MISC_W:MISC_W + N_IDX_HEADS] * (N_IDX_HEADS ** -0.5 * IDX_DIM ** -0.5)
    d_past = jnp.dot(qstack, ki_buf[slot].astype(BF16), preferred_element_type=F32)
    d_own = lax.dot_general(qstack, _pad_rows(kin_ref[...], PAGE_SIZE).astype(BF16), NT_DIMS,
                            preferred_element_type=F32)
    d = jnp.maximum(jnp.concatenate([d_past, d_own], axis=1), 0.0)
    s = jnp.zeros((ts, past + PAGE_SIZE), F32)
    for h in range(N_IDX_HEADS):
        s = s + d[h * ts:(h + 1) * ts] * w[:, h:h + 1]
    idx = lax.broadcasted_iota(I32, s.shape, 1)
    qpos = past + lax.broadcasted_iota(I32, (ts, 1), 0)
    score_ref[pl.ds(pl.multiple_of(b * ts, ts), ts), :] = jnp.where(idx <= qpos, s, NEG)

    @pl.when(b == pl.num_programs(0) - 1)
    def _():
        reducers = _counter(score_ref, score_ref.shape[1] // ck, ck, 1)
        qshape = (score_ref.shape[0], 1)
        thr = _kth_largest_by_planes_lanes(score_ref, planes_ref, n_sel)
        thr, jstar = _select_topk(thr, True, reducers, qshape, n_sel, idx_bits)
        thr_ref[...] = jnp.broadcast_to(thr, thr_ref.shape)
        jst_ref[...] = jnp.broadcast_to(jstar, jst_ref.shape)


def _sample_attend_kernel(n_pages, ts, pt_ref, rb_ref, qa_ref, score_ref, thr_ref, jst_ref, kn_ref,
                          vn_ref, ckt_hbm, cvt_hbm, out_ref, k_buf, v_buf, sems, strip_ref):
    b = pl.program_id(0)
    past = n_pages * PAGE_SIZE
    rows = N_HEADS * ts

    @pl.when(b == 0)
    def _():
        _build_bias_strip(strip_ref, rb_ref, LANES * (strip_ref.shape[1] - 1), 1)

    slot = _page_pipeline(pt_ref, n_pages, [ckt_hbm, cvt_hbm], [k_buf, v_buf], sems)()

    q2 = jnp.concatenate(_group_queries(qa_ref[...], ts), axis=0)
    k_own = _pad_rows(kn_ref[...], PAGE_SIZE).astype(BF16)
    v_own = _pad_rows(vn_ref[...], PAGE_SIZE).astype(BF16)
    s_past = jnp.dot(q2, k_buf[slot].astype(BF16), preferred_element_type=F32)
    s_own = lax.dot_general(q2, k_own, NT_DIMS, preferred_element_type=F32)
    far = strip_ref[:, 0].reshape(rows, LANES)[:, 0:1]
    near = [strip_ref[:, t].reshape(rows, LANES) for t in (1, 2)]
    s = jnp.concatenate([s_past[:, :past - PAGE_SIZE] + far, s_past[:, past - PAGE_SIZE:] + near[0],
                         s_own + near[1]], axis=1)

    qpos = past + lax.broadcasted_iota(I32, (ts, 1), 0)
    scores = score_ref[...]
    valid = _valid_mask(scores, lax.broadcasted_iota(I32, scores.shape, 1), thr_ref[:, 0:1],
                        jst_ref[:, 0:1], qpos)
    s = jnp.where(valid[None], s.reshape(N_HEADS, ts, past + PAGE_SIZE), NEG).reshape(rows, -1)
    m = jnp.max(s, axis=1, keepdims=True)
    p = jnp.exp(s - m)
    l = jnp.sum(p, axis=1, keepdims=True)
    pb = p.astype(BF16)
    pv = lax.dot_general(pb[:, :past], v_buf[slot].astype(BF16), NT_DIMS, preferred_element_type=F32)
    pv = pv + jnp.dot(pb[:, past:], v_own, preferred_element_type=F32)
    half = GROUP * ts
    carries = [(None, l[n * half:(n + 1) * half], pv[n * half:(n + 1) * half]) for n in range(N_KV_HEADS)]
    _write_attn(out_ref, carries, ts)


def _dsa_sample(page_table, rel_bias, qa, qi, misc, ki_new, k_new, v_new, ckit, ckt, cvt, DB, ts):
    n_pages = page_table.shape[1]
    past = n_pages * PAGE_SIZE
    n_sel = min(TOPK_MAX, (past + ts) // 4)
    lpad = past + PAGE_SIZE
    idx_bits = max(1, (lpad - 1).bit_length())
    ck = LANES * math.gcd(lpad // LANES, 5)
    assert ts % SUBLANES == 0 and ts <= PAGE_SIZE and n_pages >= 1
    blk = lambda w: pl.BlockSpec((ts, w), lambda b, pt: (b, 0))
    whole = lambda w: pl.BlockSpec((DB * ts, w), lambda b, pt: (0, 0))
    hbm = pl.BlockSpec(memory_space=pl.ANY)
    keys, thr, jstar = pl.pallas_call(
        functools.partial(_sample_score_kernel, n_sel, idx_bits, n_pages, ts, ck),
        grid_spec=pltpu.PrefetchScalarGridSpec(
            num_scalar_prefetch=1,
            grid=(DB,),
            in_specs=[blk(N_IDX_HEADS * IDX_DIM), blk(LANES), blk(IDX_DIM), hbm],
            out_specs=[whole(lpad), whole(LANES), whole(LANES)],
            scratch_shapes=[pltpu.VMEM((2, IDX_DIM, past), F32), pltpu.SemaphoreType.DMA((2, 1)),
                            pltpu.VMEM((32, -(-lpad // (32 * LANES)), DB * ts, LANES), I32)]),
        out_shape=[jax.ShapeDtypeStruct((DB * ts, lpad), F32),
                   jax.ShapeDtypeStruct((DB * ts, LANES), F32),
                   jax.ShapeDtypeStruct((DB * ts, LANES), I32)],
        compiler_params=_cparams(1),
        name="sample_score",
    )(page_table, qi, misc, ki_new, ckit)
    return pl.pallas_call(
        functools.partial(_sample_attend_kernel, n_pages, ts),
        grid_spec=pltpu.PrefetchScalarGridSpec(
            num_scalar_prefetch=1,
            grid=(DB,),
            in_specs=[pl.BlockSpec(memory_space=pltpu.SMEM), blk(ATT_WIDTH), blk(lpad), blk(LANES),
                      blk(LANES), blk(LANES), blk(LANES), hbm, hbm],
            out_specs=blk(ATT_WIDTH),
            scratch_shapes=[pltpu.VMEM((2, LANES, past), F32), pltpu.VMEM((2, LANES, past), F32),
                            pltpu.SemaphoreType.DMA((2, 2)),
                            pltpu.VMEM((N_HEADS, 3, ts, LANES), F32)]),
        out_shape=jax.ShapeDtypeStruct((DB * ts, ATT_WIDTH), F32),
        compiler_params=_cparams(1),
        name="sample_attend",
    )(page_table, rel_bias, qa, keys, thr, jstar, k_new, v_new, ckt, cvt)


def _pack_layer_weights(w_in, b_i, b_f, w_out, w_up, w_down):
    D = w_in.shape[0]
    sizes = (ATT_WIDTH, N_KV_HEADS * HEAD_DIM, N_KV_HEADS * HEAD_DIM, N_IDX_HEADS * IDX_DIM, IDX_DIM,
             N_IDX_HEADS, M_WIDTH, M_WIDTH, M_WIDTH, M_WIDTH, M_HEADS, M_HEADS)
    assert w_in.shape[1] == sum(sizes)
    pts = np.cumsum((0,) + sizes)
    seg = [w_in[:, pts[i]:pts[i + 1]] for i in range(len(sizes))]
    qa, k, v, qi, ki, wi, qm, km, vm, om, im, fm = seg
    perm = np.asarray(HEAD_PERM)
    qa = qa.reshape(D, N_HEADS, HEAD_DIM)[:, perm].reshape(D, ATT_WIDTH)
    misc = jnp.concatenate([wi, im, fm, jnp.zeros((D, LANES - N_IDX_HEADS - 2 * M_HEADS), w_in.dtype)], axis=1)
    wp = jnp.concatenate([qa, v, qi, k, ki, ki, misc, qm, km, vm, om], axis=1).astype(BF16)
    assert wp.shape[1] == N_PACK
    gate_bias = jnp.broadcast_to(jnp.concatenate([b_i, b_f]).astype(F32)[:, None], (2 * M_HEADS, LANES))
    woa = w_out[:ATT_WIDTH].reshape(N_HEADS, HEAD_DIM, -1)[perm].reshape(ATT_WIDTH, -1).astype(BF16)
    woh = w_out[ATT_WIDTH:].astype(BF16)
    wt = jnp.concatenate([seg[1].T, seg[2].T, ki.T, jnp.zeros((LANES - IDX_DIM, D), w_in.dtype),
                          seg[0].T, qi.T, misc.T], axis=0).astype(BF16)
    assert wt.shape[0] == N_TPACK
    return wp, wt, gate_bias, woa, woh, w_up.astype(BF16), w_down.astype(BF16)


def _layer(x, packed, g1, g2, mnorm, rel_bias, gf, final_norm, past):
    wp, wt, gate_bias, woa, woh, wup, wdn = packed
    B, T, D = x.shape
    x2 = x.reshape(B * T, D)
    tm = math.gcd(T if past is None else B * T, 512)
    kv_w = N_KV_HEADS * HEAD_DIM
    if past is None:
        (misc, qm, km, vm, om, kb, kib, kt, vt, kit, vtb, qat, qit, misct) = _inproj(
            x2, g1.reshape(1, D), wp, wt, B, T, tm)
        attn = _dsa_prompt(rel_bias, qat, qit, misct, kib, kb, vtb, B, T)
        state = None
        k_new = kt.reshape(B, N_KV_HEADS, HEAD_DIM, T).transpose(0, 3, 1, 2)
        v_new = vt.reshape(B, N_KV_HEADS, HEAD_DIM, T).transpose(0, 3, 1, 2)
        ki_new = kit.transpose(0, 2, 1)
    else:
        (misc, qm, km, vm, om, qa, qi, k, v, ki) = _inproj(x2, g1.reshape(1, D), wp, None, B, T, tm)
        page_table, cache_k, cache_v, cache_kidx, c0, n0, m0 = past
        n_pool = cache_k.shape[0]
        ckt = cache_k.transpose(0, 2, 3, 1).reshape(n_pool, kv_w, PAGE_SIZE)
        cvt = cache_v.transpose(0, 2, 3, 1).reshape(n_pool, kv_w, PAGE_SIZE)
        ckit = cache_kidx.transpose(0, 2, 1)
        attn = _dsa_sample(page_table, rel_bias, qa, qi, misc, ki, k, v, ckit, ckt, cvt, B, T)
        state = (c0, n0, jnp.broadcast_to(m0[..., None], m0.shape + (LANES,)))
        k_new = k.reshape(B, T, N_KV_HEADS, HEAD_DIM)
        v_new = v.reshape(B, T, N_KV_HEADS, HEAD_DIM)
        ki_new = ki.reshape(B, T, IDX_DIM)
    h, c_new, n_new, m_new = _mlstm(qm, km, vm, om, misc, gate_bias, mnorm.reshape(1, M_WIDTH), state,
                                    B, T, BF16 if past is None else F32)
    y = _post(x2, attn, h, woa, woh, g2.reshape(1, D), wup, wdn, gf.reshape(1, D), final_norm, tm)
    return (y.reshape(B, T, D), k_new, v_new, ki_new, c_new, n_new, m_new[..., 0])


def kernel(x_prompt, x_sample, cache_k, cache_v, cache_kidx, page_table, state_C, state_n, state_m,
           w_in, b_igate, b_fgate, mlstm_norm, rel_bias, w_out, norm1, norm2, w_up, w_down, norm_f):
    depth = w_in.shape[0]
    xp, xs = x_prompt, x_sample
    outs_p, outs_s = [], []
    for l in range(depth):
        packed = _pack_layer_weights(w_in[l], b_igate[l], b_fgate[l], w_out[l], w_up[l], w_down[l])
        last = l == depth - 1
        common = (packed, norm1[l], norm2[l], mlstm_norm[l], rel_bias, norm_f, last)
        rp = _layer(xp, *common, None)
        rs = _layer(xs, *common, (page_table, cache_k[l], cache_v[l], cache_kidx[l],
                                  state_C[l], state_n[l], state_m[l]))
        xp, xs = rp[0], rs[0]
        outs_p.append(rp[1:])
        outs_s.append(rs[1:])
    stack = lambda outs, i: jnp.stack([o[i] for o in outs])
    return ((xp, xs) + tuple(stack(outs_p, i) for i in range(6))
            + tuple(stack(outs_s, i) for i in range(6)))
```

```python
import functools
import math

import numpy as np
import jax
import jax.numpy as jnp
from jax import lax
from jax.experimental import pallas as pl
from jax.experimental.pallas import tpu as pltpu

F32 = jnp.float32
BF16 = jnp.bfloat16
I32 = jnp.int32

N_HEADS = 8
HEAD_DIM = 64
N_KV_HEADS = 2
GROUP = N_HEADS // N_KV_HEADS
N_IDX_HEADS = 8
IDX_DIM = 64
TOPK_MAX = 256
N_BUCKETS = 32
MAX_DISTANCE = 128
M_HEADS = 4
M_HEAD_DIM = 128
PAGE_SIZE = 128
EPS = 1e-6
NEG = -1e30
LOG2E = math.log2(math.e)
ATT_WIDTH = N_HEADS * HEAD_DIM
M_WIDTH = M_HEADS * M_HEAD_DIM

LANES = 128
SUBLANES = 8
VMEM_LIMIT = 56 * 1024 * 1024

C_QA = 0
C_V = C_QA + ATT_WIDTH
C_QI = C_V + LANES
C_K = C_QI + N_IDX_HEADS * IDX_DIM
C_KI2 = C_K + LANES
C_MISC = C_KI2 + LANES
C_QM = C_MISC + LANES
C_KM = C_QM + M_WIDTH
C_VM = C_KM + M_WIDTH
C_OM = C_VM + M_WIDTH
N_PACK = C_OM + M_WIDTH
MISC_W = 0
MISC_I = 8
MISC_F = 12

HEAD_PERM = (0, 4, 1, 5, 2, 6, 3, 7)

TQ = 256
STRIP_BACK = 2
assert LANES * (STRIP_BACK - 1) >= MAX_DISTANCE
CKP = 256
VT_ROWS = LANES + 16
ML = 256

NT_DIMS = (((1,), (1,)), ((), ()))
TN_DIMS = (((0,), (0,)), ((), ()))


def _bucket_bounds():
    max_exact = N_BUCKETS // 2
    scale = (N_BUCKETS - max_exact) / math.log(MAX_DISTANCE / max_exact)

    def bucket(n, dt):
        if n < max_exact:
            return n
        val = np.log(np.asarray(max(n, 1), dt) / dt(max_exact)) * dt(scale)
        return min(max_exact + int(val), N_BUCKETS - 1)

    table = [bucket(n, np.float32) for n in range(MAX_DISTANCE + 2)]
    assert table == [bucket(n, np.float64) for n in range(MAX_DISTANCE + 2)]
    assert table[MAX_DISTANCE] == N_BUCKETS - 1
    return [next(d for d, b in enumerate(table) if b >= k) for k in range(N_BUCKETS)]


BUCKET_BOUNDS = _bucket_bounds()


def _cparams(n_axes):
    return pltpu.CompilerParams(dimension_semantics=("arbitrary",) * n_axes,
                                vmem_limit_bytes=VMEM_LIMIT)


def _const_spec(shape):
    nd = len(shape)
    return pl.BlockSpec(shape, lambda *_: (0,) * nd, pipeline_mode=pl.Buffered(1))


def _rms(x, g):
    return x * lax.rsqrt(jnp.mean(x * x, axis=-1, keepdims=True) + EPS) * g


def _inproj_mlstm(mm, misc_ref, qm_ref, km_ref, vm_ref, om_ref):
    misc_ref[...] = mm(C_MISC, LANES)
    qm_ref[...] = mm(C_QM, M_WIDTH).astype(qm_ref.dtype)
    km_ref[...] = (mm(C_KM, M_WIDTH) * (M_HEAD_DIM ** -0.5)).astype(km_ref.dtype)
    vm_ref[...] = mm(C_VM, M_WIDTH).astype(vm_ref.dtype)
    om_ref[...] = mm(C_OM, M_WIDTH)


def _inproj_rows_kernel(x_ref, g_ref, w_ref, misc_ref, qm_ref, km_ref, vm_ref, om_ref,
                        qa_ref, qi_ref, k_ref, v_ref, ki_ref):
    ub = _rms(x_ref[...], g_ref[...]).astype(BF16)
    z = jnp.dot(ub, w_ref[...], preferred_element_type=F32)
    mm = lambda c0, n: z[:, c0:c0 + n]
    _inproj_mlstm(mm, misc_ref, qm_ref, km_ref, vm_ref, om_ref)
    qa_ref[...] = mm(C_QA, ATT_WIDTH) * (HEAD_DIM ** -0.5)
    qi_ref[...] = mm(C_QI, N_IDX_HEADS * IDX_DIM)
    k_ref[...] = mm(C_K, LANES)
    v_ref[...] = mm(C_V, LANES)
    ki_ref[...] = mm(C_KI2, LANES)[:, :IDX_DIM]


R_K = 0
R_V = R_K + LANES
R_KI = R_V + LANES
R_QA = R_KI + LANES
R_QI = R_QA + ATT_WIDTH
R_MISC = R_QI + N_IDX_HEADS * IDX_DIM
N_TPACK = R_MISC + LANES


def _inproj_cols_kernel(x_ref, g_ref, w_ref, wt_ref, misc_ref, qm_ref, km_ref, vm_ref, om_ref,
                        kb_ref, kib_ref, kt_ref, vt_ref, kit_ref, vtb_ref, qat_ref, qit_ref,
                        misct_ref):
    ub = _rms(x_ref[...], g_ref[...]).astype(BF16)
    z = jnp.dot(ub, w_ref[:, C_K:], preferred_element_type=F32)
    mm = lambda c0, n: z[:, c0 - C_K:c0 - C_K + n]
    _inproj_mlstm(mm, misc_ref, qm_ref, km_ref, vm_ref, om_ref)
    kb_ref[...] = mm(C_K, LANES).astype(BF16)
    kib_ref[...] = mm(C_KI2, LANES)[:, :IDX_DIM].astype(BF16)

    zt = lax.dot_general(wt_ref[...], ub, NT_DIMS, preferred_element_type=F32)
    mt = lambda r0, n: zt[r0:r0 + n]

    kt_ref[0] = mt(R_K, LANES)
    vt = mt(R_V, LANES)
    vt_ref[0] = vt
    ones = jnp.ones((VT_ROWS - LANES, CKP), BF16)
    for j in range(vtb_ref.shape[1]):
        vtb_ref[0, j] = jnp.concatenate([vt[:, j * CKP:(j + 1) * CKP].astype(BF16), ones], axis=0)
    kit_ref[0] = mt(R_KI, IDX_DIM)
    qat_ref[0] = (mt(R_QA, ATT_WIDTH) * (HEAD_DIM ** -0.5 * LOG2E)).astype(BF16)
    qit_ref[0] = mt(R_QI, N_IDX_HEADS * IDX_DIM).astype(BF16)
    misct_ref[0] = mt(R_MISC, LANES)


def _inproj(x2, g1, wp, wt, B, T, tm):
    R, D = x2.shape
    assert R == B * T and R % tm == 0
    mdt = F32 if wt is None else BF16
    row = lambda i: (i, 0)
    outs = [(LANES, F32), (M_WIDTH, mdt), (M_WIDTH, mdt), (M_WIDTH, mdt), (M_WIDTH, F32)]
    in_specs = [pl.BlockSpec((tm, D), row), _const_spec((1, D)), _const_spec((D, N_PACK))]
    if wt is None:
        kern, args = _inproj_rows_kernel, (x2, g1, wp)
        outs += [(ATT_WIDTH, F32), (N_IDX_HEADS * IDX_DIM, F32), (LANES, F32), (LANES, F32), (IDX_DIM, F32)]
    else:
        kern, args = _inproj_cols_kernel, (x2, g1, wp, wt)
        in_specs.append(_const_spec(wt.shape))
        outs += [(LANES, BF16), (IDX_DIM, BF16)]
    out_specs = [pl.BlockSpec((tm, w), row) for w, _ in outs]
    out_shape = [jax.ShapeDtypeStruct((R, w), dt) for w, dt in outs]
    if wt is not None:
        assert T % tm == 0 and tm % CKP == 0
        tpb, cpt = T // tm, tm // CKP
        cols = lambda i: (i // tpb, 0, i % tpb)
        for w, dt in ((LANES, F32), (LANES, F32), (IDX_DIM, F32)):
            out_specs.append(pl.BlockSpec((1, w, tm), cols))
            out_shape.append(jax.ShapeDtypeStruct((B, w, T), dt))
        out_specs.append(pl.BlockSpec((1, cpt, VT_ROWS, CKP), lambda i: (i // tpb, i % tpb, 0, 0)))
        out_shape.append(jax.ShapeDtypeStruct((B, T // CKP, VT_ROWS, CKP), BF16))
        for w, dt in ((ATT_WIDTH, BF16), (N_IDX_HEADS * IDX_DIM, BF16), (LANES, F32)):
            out_specs.append(pl.BlockSpec((1, w, tm), cols))
            out_shape.append(jax.ShapeDtypeStruct((B, w, T), dt))
    return pl.pallas_call(
        kern,
        grid=(R // tm,),
        in_specs=in_specs,
        out_specs=out_specs,
        out_shape=out_shape,
        compiler_params=_cparams(1),
        name="inproj",
    )(*args)


def _post_kernel(ff_chunk, final_norm, x_ref, a_ref, h_ref, woa_ref, woh_ref, g2_ref, wup_ref,
                 wdn_ref, gf_ref, y_ref):
    mix = jnp.dot(a_ref[...].astype(BF16), woa_ref[...], preferred_element_type=F32)
    mix = mix + jnp.dot(h_ref[...].astype(BF16), woh_ref[...], preferred_element_type=F32)
    hres = x_ref[...] + mix
    f = _rms(hres, g2_ref[...]).astype(BF16)
    acc = hres
    for c0 in range(0, wup_ref.shape[1], ff_chunk):
        up = jnp.dot(f, wup_ref[:, c0:c0 + ff_chunk], preferred_element_type=F32)
        r = jnp.maximum(up, 0.0)
        acc = acc + jnp.dot((r * r).astype(BF16), wdn_ref[c0:c0 + ff_chunk, :],
                            preferred_element_type=F32)
    y_ref[...] = _rms(acc, gf_ref[...]) if final_norm else acc


def _post(x2, attn, h, woa, woh, g2, wup, wdn, gf, final_norm, tm):
    R, D = x2.shape
    dff = wup.shape[1]
    assert R % tm == 0
    row = lambda i: (i, 0)
    return pl.pallas_call(
        functools.partial(_post_kernel, min(dff, 1024), final_norm),
        grid=(R // tm,),
        in_specs=[pl.BlockSpec((tm, D), row), pl.BlockSpec((tm, ATT_WIDTH), row),
                  pl.BlockSpec((tm, M_WIDTH), row), _const_spec(woa.shape), _const_spec(woh.shape),
                  _const_spec((1, D)), _const_spec(wup.shape), _const_spec(wdn.shape),
                  _const_spec((1, D))],
        out_specs=pl.BlockSpec((tm, D), row),
        out_shape=jax.ShapeDtypeStruct((R, D), F32),
        compiler_params=_cparams(1),
        name="post",
    )(x2, attn, h, woa, woh, g2, wup, wdn, gf)


def _log_sigmoid(x):
    return -(jnp.maximum(-x, 0.0) + jnp.log1p(jnp.exp(-jnp.abs(x))))


def _mlstm_kernel(nvalid, has_state, *refs):
    if has_state:
        (q_ref, k_ref, v_ref, o_ref, misc_ref, gb_ref, mn_ref, c0_ref, n0_ref, m0_ref,
         h_ref, c_ref, n_ref, m_ref, z_scr, rows_scr) = refs
    else:
        (q_ref, k_ref, v_ref, o_ref, misc_ref, gb_ref, mn_ref,
         h_ref, c_ref, n_ref, m_ref, z_scr, rows_scr) = refs
    nb, nc, L = z_scr.shape[0], z_scr.shape[1], z_scr.shape[2]
    c = pl.program_id(1)

    def padded(x, dt):
        x = x.astype(dt)
        if nvalid == L:
            return x
        return jnp.concatenate([x, jnp.zeros((L - nvalid, x.shape[1]), dt)], axis=0)

    r2 = lax.broadcasted_iota(I32, (L, L), 0)
    c2 = lax.broadcasted_iota(I32, (L, L), 1)
    tril = r2 >= c2

    @pl.when(c == 0)
    def _():
        if has_state:
            c_ref[...] = c0_ref[...]
            n_ref[...] = n0_ref[...]
            m_ref[...] = m0_ref[...]
        else:
            c_ref[...] = jnp.zeros_like(c_ref)
            n_ref[...] = jnp.zeros_like(n_ref)
            m_ref[...] = jnp.zeros_like(m_ref)
        assert MISC_F == MISC_I + M_HEADS and MISC_I % SUBLANES == 0
        row8 = lax.broadcasted_iota(I32, (SUBLANES, L), 0)
        tok8 = lax.broadcasted_iota(I32, (SUBLANES, L), 1)
        bias8 = jnp.concatenate([gb_ref[...]] * (L // LANES), axis=1)
        triu = (r2 <= c2).astype(F32)
        for s in range(nb):
            for cc in range(nc):
                misc_t = padded(misc_ref[s, cc * nvalid:(cc + 1) * nvalid, :], F32).T
                gx = misc_t[MISC_I:MISC_I + SUBLANES] + bias8
                gates = jnp.where(row8 >= M_HEADS, _log_sigmoid(gx), gx)
                if nvalid != L:
                    gates = jnp.where(tok8 < nvalid, gates, jnp.where(row8 >= M_HEADS, 0.0, NEG))
                cum = jnp.dot(gates, triu, precision=lax.Precision.HIGHEST,
                              preferred_element_type=F32)
                rows = jnp.where(row8 >= M_HEADS, cum, gates)
                rows_scr[s, cc] = rows
                z_scr[s, cc] = jnp.concatenate([rows, jnp.zeros((LANES - SUBLANES, L), F32)], axis=0).T

    loaded = []
    for s in range(nb):
        loaded.append((z_scr[s, c], rows_scr[s, c], padded(q_ref[s], BF16), padded(k_ref[s], BF16),
                       padded(v_ref[s], BF16), padded(o_ref[s], F32), m_ref[s], c_ref[s], n_ref[s]))
    chains = [(s, hd) for s in range(nb) for hd in range(M_HEADS)]

    def operands(s, hd):
        z, rows, qb, kb, vb, ob, m_all, c_all, n_all = loaded[s]
        sl = slice(hd * M_HEAD_DIM, (hd + 1) * M_HEAD_DIM)
        return dict(
            sl=sl, q=qb[:, sl], k=kb[:, sl], v=vb[:, sl], o=ob[:, sl],
            icol=z[:, hd:hd + 1], bcol=z[:, M_HEADS + hd:M_HEADS + hd + 1],
            irow=rows[hd:hd + 1, :], brow=rows[M_HEADS + hd:M_HEADS + hd + 1, :],
            m_prev=m_all[hd:hd + 1, 0:1], s_prev=c_all[hd], n_prev=n_all[hd:hd + 1, :])

    def lane_sum(x):
        ones = jnp.ones((x.shape[1], LANES), BF16)
        head = x.astype(BF16)
        tail = (x - head.astype(F32)).astype(BF16)
        return (jnp.dot(head, ones, preferred_element_type=F32)
                + jnp.dot(tail, ones, preferred_element_type=F32))

    st = [operands(s, hd) for s, hd in chains]
    for x in st:
        x["qk"] = lax.dot_general(x["q"], x["k"], NT_DIMS, preferred_element_type=F32)
        x["qs"] = jnp.dot(x["q"], x["s_prev"].astype(BF16), preferred_element_type=F32)
        x["qn"] = lane_sum(x["q"].astype(F32) * x["n_prev"])
    for x in st:
        g = x["bcol"] + x["m_prev"]
        dm = jnp.where(tril, x["bcol"] - x["brow"] + x["irow"], NEG)
        x["mt"] = jnp.maximum(g, jnp.max(dm, axis=1, keepdims=True))
        x["gw"] = jnp.exp(g - x["mt"])
        x["qk"] = x["qk"] * jnp.exp(dm - x["mt"])
        b_last = x["bcol"][L - 1:L, :]
        g_last = b_last + x["m_prev"]
        a = b_last - x["bcol"] + x["icol"]
        x["m_new"] = jnp.maximum(g_last, jnp.max(a, axis=0, keepdims=True))
        x["sw"] = jnp.exp(g_last - x["m_new"])
        x["ak"] = jnp.exp(a - x["m_new"]) * x["k"].astype(F32)
    for x in st:
        x["pv"] = jnp.dot(x["qk"].astype(BF16), x["v"], preferred_element_type=F32)
        x["qksum"] = lane_sum(x["qk"])
        x["kv"] = lax.dot_general(x["ak"].astype(BF16), x["v"], TN_DIMS, preferred_element_type=F32)
    stores = []
    for (s, hd), x in zip(chains, st):
        num = x["gw"] * x["qs"] + x["pv"]
        den = x["gw"] * x["qn"] + x["qksum"]
        hh = num / jnp.maximum(jnp.abs(den), jnp.exp(-x["mt"]))
        hh = hh * lax.rsqrt(lane_sum(hh * hh) * (1.0 / M_HEAD_DIM) + EPS)
        hh = hh * mn_ref[:, x["sl"]] * jax.nn.sigmoid(x["o"])
        c_new = x["sw"] * x["s_prev"] + x["kv"]
        n_new = x["sw"] * x["n_prev"] + jnp.sum(x["ak"], axis=0, keepdims=True)
        stores.append((s, hd, x["sl"], hh[:nvalid].astype(h_ref.dtype), c_new, n_new,
                       jnp.broadcast_to(x["m_new"], (1, LANES))))
    for s, hd, sl, h_new, c_new, n_new, m_new in stores:
        h_ref[s, :, sl] = h_new
        c_ref[s, hd] = c_new
        n_ref[s, hd:hd + 1, :] = n_new
        m_ref[s, hd:hd + 1, :] = m_new


def _mlstm(qm, km, vm, om, misc, gate_bias, mnorm, state, B, T, hdt):
    L = next((c for c in (ML, LANES) if T % c == 0), LANES)
    nvalid = L if T % L == 0 else T
    assert nvalid <= L and T % nvalid == 0 and nvalid % SUBLANES == 0
    nc = T // nvalid
    nb = next(n for n in ((2, 1) if nvalid == L else (4, 2, 1)) if B % n == 0)
    seq3 = lambda a: a.reshape(B, T, a.shape[-1])
    blk = lambda w: pl.BlockSpec((nb, nvalid, w), lambda b, c: (b, c, 0))
    st_specs = [pl.BlockSpec((nb, M_HEADS, M_HEAD_DIM, M_HEAD_DIM), lambda b, c: (b, 0, 0, 0)),
                pl.BlockSpec((nb, M_HEADS, M_HEAD_DIM), lambda b, c: (b, 0, 0)),
                pl.BlockSpec((nb, M_HEADS, LANES), lambda b, c: (b, 0, 0))]
    in_specs = [blk(M_WIDTH), blk(M_WIDTH), blk(M_WIDTH), blk(M_WIDTH),
                pl.BlockSpec((nb, T, LANES), lambda b, c: (b, 0, 0)),
                pl.BlockSpec((SUBLANES, LANES), lambda b, c: (0, 0)),
                pl.BlockSpec((1, M_WIDTH), lambda b, c: (0, 0))]
    args = [seq3(qm), seq3(km), seq3(vm), seq3(om), seq3(misc), gate_bias, mnorm]
    if state is not None:
        in_specs += st_specs
        args += list(state)
    h, c_new, n_new, m_new = pl.pallas_call(
        functools.partial(_mlstm_kernel, nvalid, state is not None),
        grid=(B // nb, nc),
        in_specs=in_specs,
        out_specs=[blk(M_WIDTH)] + st_specs,
        out_shape=[jax.ShapeDtypeStruct((B, T, M_WIDTH), hdt),
                   jax.ShapeDtypeStruct((B, M_HEADS, M_HEAD_DIM, M_HEAD_DIM), F32),
                   jax.ShapeDtypeStruct((B, M_HEADS, M_HEAD_DIM), F32),
                   jax.ShapeDtypeStruct((B, M_HEADS, LANES), F32)],
        scratch_shapes=[pltpu.VMEM((nb, nc, L, LANES), F32), pltpu.VMEM((nb, nc, SUBLANES, L), F32)],
        compiler_params=_cparams(2),
        name="mlstm",
    )(*args)
    return h.reshape(B * T, M_WIDTH), c_new, n_new, m_new


def _sortable_key(score):
    bits = lax.bitcast_convert_type(score, I32)
    return bits ^ (lax.shift_right_arithmetic(bits, 31) & 0x7FFFFFFF)


def _key_to_score(key):
    return lax.bitcast_convert_type(key ^ (lax.shift_right_arithmetic(key, 31) & 0x7FFFFFFF), F32)


def _build_bias_strip(strip_ref, rb_ref, off, key_axis, log2_relative=False):
    ntiles, tile = strip_ref.shape[1], strip_ref.shape[2:]
    i = lax.broadcasted_iota(I32, tile, 1 - key_axis)
    x = lax.broadcasted_iota(I32, tile, key_axis)

    def entry(b, h):
        if log2_relative:
            return (rb_ref[b, h] - rb_ref[N_BUCKETS - 1, h]) * LOG2E
        return rb_ref[b, h]

    for t in range(ntiles):
        dist = i + (off - LANES * t) - x
        for h in range(N_HEADS):
            val = jnp.full(tile, entry(0, h), F32)
            for b in range(1, N_BUCKETS):
                val = jnp.where(dist >= BUCKET_BOUNDS[b], entry(b, h), val)
            strip_ref[h, t] = val


def _counter(score_ref, nk, ck, key_axis):
    def reduce(fn, fold, fold_all, init):
        def body(c, acc):
            c0 = pl.multiple_of(c * ck, ck)
            sc = score_ref[:, pl.ds(c0, ck)] if key_axis == 1 else score_ref[pl.ds(c0, ck), :]
            val = fn(sc, c0 + lax.broadcasted_iota(I32, sc.shape, key_axis))
            if key_axis == 1:
                for j in range(ck // LANES):
                    acc = fold(acc, val[:, j * LANES:(j + 1) * LANES])
                return acc
            return fold(acc, fold_all(val.reshape(ck // SUBLANES, SUBLANES, val.shape[1]), axis=0))

        nq = score_ref.shape[1 - key_axis]
        acc0 = jnp.full((nq, LANES) if key_axis == 1 else (SUBLANES, nq), init, I32)
        return fold_all(lax.fori_loop(0, nk, body, acc0), axis=key_axis, keepdims=True)

    def count(pred):
        return reduce(lambda sc, idx: jnp.where(pred(sc, idx), 1, 0), jnp.add, jnp.sum, 0)

    def lowest(fn):
        return reduce(fn, jnp.minimum, jnp.min, jnp.iinfo(jnp.int32).max)

    return count, lowest


IMIN = jnp.iinfo(jnp.int32).min


def _kth_largest_by_count(count, qshape, n_sel):
    def bit_step(i, key):
        cand = key + lax.shift_left(jnp.int32(1), 31 - i)
        cand_score = _key_to_score(cand)
        return jnp.where(count(lambda sc, idx: sc >= cand_score) >= n_sel, cand, key)

    return _key_to_score(lax.fori_loop(0, 32, bit_step, jnp.full(qshape, IMIN, I32)))


def _bit_planes(words):
    a = list(words)
    j, m = 16, 0x0000FFFF
    while j:
        k = 0
        while k < 32:
            t = (a[k] ^ lax.shift_right_logical(a[k + j], j)) & m
            a[k] = a[k] ^ t
            a[k + j] = a[k + j] ^ lax.shift_left(t, j)
            k = (k + j + 1) & ~j
        j >>= 1
        m = (m ^ (m << j)) & 0xFFFFFFFF
        m = m - (1 << 32) if m >= (1 << 31) else m
    return a[::-1]


def _kth_largest_by_planes(score_ref, planes_ref, nk, ck, n_sel):
    nc = planes_ref.shape[1]
    nq = score_ref.shape[1]
    assert ck == 32 * SUBLANES

    def pack_chunk(c, _):
        c0 = pl.multiple_of(c * ck, ck)
        u = _sortable_key(score_ref[pl.ds(c0, ck), :]) ^ IMIN
        u = u.reshape(32, SUBLANES, nq)
        for b, plane in enumerate(_bit_planes([u[v] for v in range(32)])):
            planes_ref[b, c] = plane
        return 0

    lax.fori_loop(0, nk, pack_chunk, 0)
    cand0 = tuple(jnp.where(c < nk, jnp.full((SUBLANES, nq), -1, I32), 0) for c in range(nc))
    return _plane_search(planes_ref, cand0, n_sel, 0)


def _kth_largest_by_planes_lanes(score_ref, planes_ref, n_sel):
    nq, ntiles = score_ref.shape[0], score_ref.shape[1] // LANES
    cand0 = []
    for g in range(planes_ref.shape[1]):
        real = min(32, ntiles - 32 * g)
        words = [_sortable_key(score_ref[:, (32 * g + v) * LANES:(32 * g + v + 1) * LANES]) ^ IMIN
                 if v < real else jnp.zeros((nq, LANES), I32) for v in range(32)]
        for b, plane in enumerate(_bit_planes(words)):
            planes_ref[b, g] = plane
        cand0.append(jnp.full((nq, LANES), -(1 << (32 - real)), I32))
    return _plane_search(planes_ref, tuple(cand0), n_sel, 1)


def _plane_search(planes_ref, cand0, n_sel, key_axis):
    def bit_step(i, carry):
        cand, n_above, thr_u = carry
        b = 31 - i
        ones = [m & planes_ref[b, c] for c, m in enumerate(cand)]
        pop = lax.population_count(ones[0])
        for o in ones[1:]:
            pop = pop + lax.population_count(o)
        tot = jnp.sum(pop, axis=key_axis, keepdims=True)
        take = n_above + tot >= n_sel
        cand = tuple(jnp.where(take, o, m ^ o) for o, m in zip(ones, cand))
        n_above = jnp.where(take, n_above, n_above + tot)
        thr_u = thr_u | jnp.where(take, lax.shift_left(jnp.int32(1), b), 0)
        return cand, n_above, thr_u

    zero = jnp.zeros_like(jnp.sum(cand0[0], axis=key_axis, keepdims=True))
    _, _, thr_u = lax.fori_loop(0, 32, bit_step, (cand0, zero, zero))
    return _key_to_score(thr_u ^ IMIN)


def _select_topk(thr, check, reducers, qshape, n_sel, idx_bits):
    count, lowest = reducers
    imax = jnp.iinfo(jnp.int32).max

    def with_counts(t):
        return t, count(lambda sc, idx: sc > t), count(lambda sc, idx: sc >= t)

    thr, n_gt, n_ge = with_counts(thr)
    if check:
        good = jnp.min(jnp.where(n_gt < n_sel, jnp.where(n_ge >= n_sel, 1, 0), 0)) > 0
        thr, n_gt, n_ge = lax.cond(
            good, lambda _: (thr, n_gt, n_ge),
            lambda _: with_counts(_kth_largest_by_count(count, qshape, n_sel)), 0)
    split = n_ge > n_sel
    need = jnp.where(split, n_sel - n_gt, 0)
    most = jnp.max(need)

    def by_extraction(_):
        def step(k, last):
            nxt = lowest(lambda sc, idx: jnp.where(sc == thr, jnp.where(idx > last, idx, imax), imax))
            return jnp.where(k < need, nxt, last)

        return lax.fori_loop(0, most, step, jnp.full(qshape, -1, I32))

    def by_bisection(_):
        def idx_step(i, lo):
            cand = lo + lax.shift_left(jnp.int32(1), idx_bits - 1 - i)
            cnt = count(lambda sc, idx: jnp.where(sc == thr, idx, imax) < cand)
            return jnp.where(cnt < need, cand, lo)

        return lax.fori_loop(0, idx_bits, idx_step, jnp.zeros(qshape, I32))

    jstar = lax.cond(most <= idx_bits, by_extraction, by_bisection, 0)
    return thr, jnp.where(split, jstar, imax)


def _valid_mask(scores, idx, thr, jstar, qpos):
    sel = jnp.where(scores > thr, 1, jnp.where(scores == thr, jnp.where(idx <= jstar, 1, 0), 0))
    return jnp.where(idx <= qpos, sel, 0) > 0


def _group_queries(qa, tq):
    lane = lax.broadcasted_iota(I32, (tq, LANES), 1)
    out = []
    for n in range(N_KV_HEADS):
        keep = (lane < HEAD_DIM) if n == 0 else (lane >= HEAD_DIM)
        tiles = [jnp.where(keep, qa[:, j * LANES:(j + 1) * LANES], jnp.zeros((), qa.dtype))
                 for j in range(GROUP)]
        out.append(jnp.concatenate(tiles, axis=0).astype(BF16))
    return out


def _write_attn(out_ref, carries, tq):
    lane = lax.broadcasted_iota(I32, (tq, LANES), 1)
    res = [acc / l for (_, l, acc) in carries]
    for j in range(GROUP):
        tile = jnp.where(lane < HEAD_DIM, res[0][j * tq:(j + 1) * tq], res[1][j * tq:(j + 1) * tq])
        out_ref[:, j * LANES:(j + 1) * LANES] = tile.astype(out_ref.dtype)


def _dsa_prompt_kernel(n_sel, idx_bits, rb_ref, qat_ref, qit_ref, misct_ref, ki_ref, k_ref, vt_ref,
                       out_ref, score_ref, planes_ref, strip_ref, sa_ref, sb_ref, m_ref, acc_ref):
    qb = pl.program_id(1)
    q0 = qb * TQ
    nk = (q0 + TQ + CKP - 1) // CKP
    tiles_per_chunk = CKP // LANES
    back_tiles = STRIP_BACK
    strip_off = LANES * back_tiles

    @pl.when((pl.program_id(0) == 0) & (qb == 0))
    def _():
        _build_bias_strip(strip_ref, rb_ref, strip_off, 0, log2_relative=True)
        planes_ref[...] = jnp.zeros(planes_ref.shape, I32)

    qpos = q0 + lax.broadcasted_iota(I32, (1, TQ), 1)

    qit = qit_ref[0]
    qstack = jnp.concatenate([qit[h * IDX_DIM:(h + 1) * IDX_DIM] for h in range(N_IDX_HEADS)],
                             axis=1)
    w = misct_ref[0, MISC_W:MISC_W + N_IDX_HEADS, :] * (N_IDX_HEADS ** -0.5 * IDX_DIM ** -0.5)

    def score_chunk(c, _):
        c0 = pl.multiple_of(c * CKP, CKP)
        d = jnp.dot(ki_ref[pl.ds(c0, CKP), :], qstack, preferred_element_type=F32)
        d = jnp.maximum(d, 0.0)
        s = jnp.zeros((CKP, TQ), F32)
        for h in range(N_IDX_HEADS):
            s = s + d[:, h * TQ:(h + 1) * TQ] * w[h:h + 1, :]
        idx = c0 + lax.broadcasted_iota(I32, (CKP, TQ), 0)
        score_ref[pl.ds(c0, CKP), :] = jnp.where(idx <= qpos, s, NEG)
        return 0

    lax.fori_loop(0, nk, score_chunk, 0)

    thr = _kth_largest_by_planes(score_ref, planes_ref, nk, CKP, n_sel)
    thr, jstar = _select_topk(thr, True, _counter(score_ref, nk, CKP, 0), (1, TQ), n_sel, idx_bits)

    qat = qat_ref[0]
    zeros = jnp.zeros((HEAD_DIM, TQ), qat.dtype)
    qgroups = []
    for n in range(N_KV_HEADS):
        tiles = []
        for g in range(GROUP):
            h = n * GROUP + g
            x = qat[h * HEAD_DIM:(h + 1) * HEAD_DIM]
            tiles.append(jnp.concatenate([x, zeros] if n == 0 else [zeros, x], axis=0))
        qgroups.append(jnp.concatenate(tiles, axis=1))

    def logits_into(s_ref, c):
        kc = k_ref[pl.ds(pl.multiple_of(c * CKP, CKP), CKP), :]
        for n in range(N_KV_HEADS):
            s_ref[n] = jnp.dot(kc, qgroups[n], preferred_element_type=F32)

    def attend_chunk(c, s_cur_ref, s_next_ref):
        if s_next_ref is not None:
            logits_into(s_next_ref, jnp.minimum(c + 1, nk - 1))
        c0 = pl.multiple_of(c * CKP, CKP)
        idx = c0 + lax.broadcasted_iota(I32, (CKP, TQ), 0)
        valid = _valid_mask(score_ref[pl.ds(c0, CKP), :], idx, thr, jstar, qpos)
        vct = vt_ref[0, c]
        tiles = [jnp.maximum(back_tiles + j - (qb * (TQ // LANES) - c * tiles_per_chunk), 0)
                 for j in range(tiles_per_chunk)]
        for n in range(N_KV_HEADS):
            m_old = m_ref[n]
            parts = []
            for g in range(GROUP):
                bias = jnp.concatenate([strip_ref[n * GROUP + g, t] for t in tiles], axis=0)
                parts.append(jnp.where(valid, s_cur_ref[n, :, g * TQ:(g + 1) * TQ] + bias, NEG))
            sm = jnp.concatenate(parts, axis=1)
            m_new = jnp.maximum(m_old, jnp.max(sm, axis=0, keepdims=True))
            p = jnp.exp2((sm - m_new).astype(BF16))
            acc_ref[n] = jnp.exp2(m_old - m_new) * acc_ref[n] + jnp.dot(vct, p, preferred_element_type=F32)
            m_ref[n] = m_new

    m_ref[...] = jnp.full(m_ref.shape, NEG, F32)
    acc_ref[...] = jnp.zeros(acc_ref.shape, F32)
    logits_into(sa_ref, 0)

    def attend_pair(i, _):
        attend_chunk(2 * i, sa_ref, sb_ref)
        attend_chunk(2 * i + 1, sb_ref, sa_ref)
        return 0

    lax.fori_loop(0, nk // 2, attend_pair, 0)

    @pl.when(nk % 2 == 1)
    def _():
        attend_chunk(nk - 1, sa_ref, None)

    carries = [(None, acc_ref[n]) for n in range(N_KV_HEADS)]
    res = [acc[:LANES] / acc[LANES:LANES + 1] for (_, acc) in carries]
    row = lax.broadcasted_iota(I32, (LANES, TQ), 0)
    for j in range(GROUP):
        cols = slice(j * TQ, (j + 1) * TQ)
        tile_t = jnp.where(row < HEAD_DIM, res[0][:, cols], res[1][:, cols])
        out_ref[:, j * LANES:(j + 1) * LANES] = tile_t.T.astype(out_ref.dtype)


def _dsa_prompt(rel_bias, qat, qit, misct, kib, kb, vtb, B, T):
    assert T % CKP == 0 and T % TQ == 0
    nq = T // TQ
    n_sel = min(TOPK_MAX, T // 4)
    qcols = lambda w: pl.BlockSpec((1, w, TQ), lambda b, q: (b, 0, q))
    seq = lambda w: pl.BlockSpec((T, w), lambda b, q: (b, 0))
    return pl.pallas_call(
        functools.partial(_dsa_prompt_kernel, n_sel, max(1, (T - 1).bit_length())),
        grid=(B, nq),
        in_specs=[pl.BlockSpec(memory_space=pltpu.SMEM), qcols(ATT_WIDTH), qcols(N_IDX_HEADS * IDX_DIM),
                  qcols(LANES), seq(IDX_DIM), seq(LANES),
                  pl.BlockSpec((1, T // CKP, VT_ROWS, CKP), lambda b, q: (b, 0, 0, 0))],
        out_specs=pl.BlockSpec((TQ, ATT_WIDTH), lambda b, q: (b * nq + q, 0)),
        out_shape=jax.ShapeDtypeStruct((B * T, ATT_WIDTH), BF16),
        scratch_shapes=[pltpu.VMEM((T, TQ), F32),
                        pltpu.VMEM((32, T // CKP, SUBLANES, TQ), I32),
                        pltpu.VMEM((N_HEADS, STRIP_BACK + max(TQ, CKP) // LANES, LANES, TQ), F32),
                        pltpu.VMEM((N_KV_HEADS, CKP, GROUP * TQ), F32),
                        pltpu.VMEM((N_KV_HEADS, CKP, GROUP * TQ), F32),
                        pltpu.VMEM((N_KV_HEADS, 1, GROUP * TQ), F32),
                        pltpu.VMEM((N_KV_HEADS, VT_ROWS, GROUP * TQ), F32)],
        compiler_params=_cparams(2),
        name="dsa_prompt",
    )(rel_bias, qat, qit, misct, kib, kb, vtb)


def _page_pipeline(pt_ref, n_pages, caches, bufs, sems):
    def copies(bb, sl, j):
        pid = pt_ref[bb, j]
        cols = pl.ds(j * PAGE_SIZE, PAGE_SIZE)
        return [pltpu.make_async_copy(c.at[pid], buf.at[sl, :, cols], sems.at[sl, i])
                for i, (c, buf) in enumerate(zip(caches, bufs))]

    def start_all(bb, sl):
        for j in range(n_pages):
            for cp in copies(bb, sl, j):
                cp.start()

    def wait_all(bb, sl):
        for j in range(n_pages):
            for cp in copies(bb, sl, j):
                cp.wait()

    def step():
        b = pl.program_id(0)
        slot = b % 2

        @pl.when(b == 0)
        def _():
            start_all(0, 0)

        @pl.when(b + 1 < pl.num_programs(0))
        def _():
            start_all(b + 1, 1 - slot)

        wait_all(b, slot)
        return slot

    return step


def _pad_rows(x, rows):
    return jnp.concatenate([x, jnp.zeros((rows - x.shape[0], x.shape[1]), x.dtype)], axis=0)


def _sample_score_kernel(n_sel, idx_bits, n_pages, ts, ck, pt_ref, qi_ref, misc_ref, kin_ref,
                         ckit_hbm, score_ref, thr_ref, jst_ref, ki_buf, sems, planes_ref):
    b = pl.program_id(0)
    past = n_pages * PAGE_SIZE
    slot = _page_pipeline(pt_ref, n_pages, [ckit_hbm], [ki_buf], sems)()

    qi = qi_ref[...]
    qstack = jnp.concatenate([qi[:, h * IDX_DIM:(h + 1) * IDX_DIM] for h in range(N_IDX_HEADS)],
                             axis=0).astype(BF16)
    w = misc_ref[:, MISC_W:MISC_W + N_IDX_HEADS] * (N_IDX_HEADS ** -0.5 * IDX_DIM ** -0.5)
    d_past = jnp.dot(qstack, ki_buf[slot].astype(BF16), preferred_element_type=F32)
    d_own = lax.dot_general(qstack, _pad_rows(kin_ref[...], PAGE_SIZE).astype(BF16), NT_DIMS,
                            preferred_element_type=F32)
    d = jnp.maximum(jnp.concatenate([d_past, d_own], axis=1), 0.0)
    s = jnp.zeros((ts, past + PAGE_SIZE), F32)
    for h in range(N_IDX_HEADS):
        s = s + d[h * ts:(h + 1) * ts] * w[:, h:h + 1]
    idx = lax.broadcasted_iota(I32, s.shape, 1)
    qpos = past + lax.broadcasted_iota(I32, (ts, 1), 0)
    score_ref[pl.ds(pl.multiple_of(b * ts, ts), ts), :] = jnp.where(idx <= qpos, s, NEG)

    @pl.when(b == pl.num_programs(0) - 1)
    def _():
        reducers = _counter(score_ref, score_ref.shape[1] // ck, ck, 1)
        qshape = (score_ref.shape[0], 1)
        thr = _kth_largest_by_planes_lanes(score_ref, planes_ref, n_sel)
        thr, jstar = _select_topk(thr, True, reducers, qshape, n_sel, idx_bits)
        thr_ref[...] = jnp.broadcast_to(thr, thr_ref.shape)
        jst_ref[...] = jnp.broadcast_to(jstar, jst_ref.shape)


def _sample_attend_kernel(n_pages, ts, pt_ref, rb_ref, qa_ref, score_ref, thr_ref, jst_ref, kn_ref,
                          vn_ref, ckt_hbm, cvt_hbm, out_ref, k_buf, v_buf, sems, strip_ref):
    b = pl.program_id(0)
    past = n_pages * PAGE_SIZE
    rows = N_HEADS * ts

    @pl.when(b == 0)
    def _():
        _build_bias_strip(strip_ref, rb_ref, LANES * (strip_ref.shape[1] - 1), 1)

    slot = _page_pipeline(pt_ref, n_pages, [ckt_hbm, cvt_hbm], [k_buf, v_buf], sems)()

    q2 = jnp.concatenate(_group_queries(qa_ref[...], ts), axis=0)
    k_own = _pad_rows(kn_ref[...], PAGE_SIZE).astype(BF16)
    v_own = _pad_rows(vn_ref[...], PAGE_SIZE).astype(BF16)
    s_past = jnp.dot(q2, k_buf[slot].astype(BF16), preferred_element_type=F32)
    s_own = lax.dot_general(q2, k_own, NT_DIMS, preferred_element_type=F32)
    far = strip_ref[:, 0].reshape(rows, LANES)[:, 0:1]
    near = [strip_ref[:, t].reshape(rows, LANES) for t in (1, 2)]
    s = jnp.concatenate([s_past[:, :past - PAGE_SIZE] + far, s_past[:, past - PAGE_SIZE:] + near[0],
                         s_own + near[1]], axis=1)

    qpos = past + lax.broadcasted_iota(I32, (ts, 1), 0)
    scores = score_ref[...]
    valid = _valid_mask(scores, lax.broadcasted_iota(I32, scores.shape, 1), thr_ref[:, 0:1],
                        jst_ref[:, 0:1], qpos)
    s = jnp.where(valid[None], s.reshape(N_HEADS, ts, past + PAGE_SIZE), NEG).reshape(rows, -1)
    m = jnp.max(s, axis=1, keepdims=True)
    p = jnp.exp(s - m)
    l = jnp.sum(p, axis=1, keepdims=True)
    pb = p.astype(BF16)
    pv = lax.dot_general(pb[:, :past], v_buf[slot].astype(BF16), NT_DIMS, preferred_element_type=F32)
    pv = pv + jnp.dot(pb[:, past:], v_own, preferred_element_type=F32)
    half = GROUP * ts
    carries = [(None, l[n * half:(n + 1) * half], pv[n * half:(n + 1) * half]) for n in range(N_KV_HEADS)]
    _write_attn(out_ref, carries, ts)


def _dsa_sample(page_table, rel_bias, qa, qi, misc, ki_new, k_new, v_new, ckit, ckt, cvt, DB, ts):
    n_pages = page_table.shape[1]
    past = n_pages * PAGE_SIZE
    n_sel = min(TOPK_MAX, (past + ts) // 4)
    lpad = past + PAGE_SIZE
    idx_bits = max(1, (lpad - 1).bit_length())
    ck = LANES * math.gcd(lpad // LANES, 5)
    assert ts % SUBLANES == 0 and ts <= PAGE_SIZE and n_pages >= 1
    blk = lambda w: pl.BlockSpec((ts, w), lambda b, pt: (b, 0))
    whole = lambda w: pl.BlockSpec((DB * ts, w), lambda b, pt: (0, 0))
    hbm = pl.BlockSpec(memory_space=pl.ANY)
    keys, thr, jstar = pl.pallas_call(
        functools.partial(_sample_score_kernel, n_sel, idx_bits, n_pages, ts, ck),
        grid_spec=pltpu.PrefetchScalarGridSpec(
            num_scalar_prefetch=1,
            grid=(DB,),
            in_specs=[blk(N_IDX_HEADS * IDX_DIM), blk(LANES), blk(IDX_DIM), hbm],
            out_specs=[whole(lpad), whole(LANES), whole(LANES)],
            scratch_shapes=[pltpu.VMEM((2, IDX_DIM, past), F32), pltpu.SemaphoreType.DMA((2, 1)),
                            pltpu.VMEM((32, -(-lpad // (32 * LANES)), DB * ts, LANES), I32)]),
        out_shape=[jax.ShapeDtypeStruct((DB * ts, lpad), F32),
                   jax.ShapeDtypeStruct((DB * ts, LANES), F32),
                   jax.ShapeDtypeStruct((DB * ts, LANES), I32)],
        compiler_params=_cparams(1),
        name="sample_score",
    )(page_table, qi, misc, ki_new, ckit)
    return pl.pallas_call(
        functools.partial(_sample_attend_kernel, n_pages, ts),
        grid_spec=pltpu.PrefetchScalarGridSpec(
            num_scalar_prefetch=1,
            grid=(DB,),
            in_specs=[pl.BlockSpec(memory_space=pltpu.SMEM), blk(ATT_WIDTH), blk(lpad), blk(LANES),
                      blk(LANES), blk(LANES), blk(LANES), hbm, hbm],
            out_specs=blk(ATT_WIDTH),
            scratch_shapes=[pltpu.VMEM((2, LANES, past), F32), pltpu.VMEM((2, LANES, past), F32),
                            pltpu.SemaphoreType.DMA((2, 2)),
                            pltpu.VMEM((N_HEADS, 3, ts, LANES), F32)]),
        out_shape=jax.ShapeDtypeStruct((DB * ts, ATT_WIDTH), F32),
        compiler_params=_cparams(1),
        name="sample_attend",
    )(page_table, rel_bias, qa, keys, thr, jstar, k_new, v_new, ckt, cvt)


def _pack_layer_weights(w_in, b_i, b_f, w_out, w_up, w_down):
    D = w_in.shape[0]
    sizes = (ATT_WIDTH, N_KV_HEADS * HEAD_DIM, N_KV_HEADS * HEAD_DIM, N_IDX_HEADS * IDX_DIM, IDX_DIM,
             N_IDX_HEADS, M_WIDTH, M_WIDTH, M_WIDTH, M_WIDTH, M_HEADS, M_HEADS)
    assert w_in.shape[1] == sum(sizes)
    pts = np.cumsum((0,) + sizes)
    seg = [w_in[:, pts[i]:pts[i + 1]] for i in range(len(sizes))]
    qa, k, v, qi, ki, wi, qm, km, vm, om, im, fm = seg
    perm = np.asarray(HEAD_PERM)
    qa = qa.reshape(D, N_HEADS, HEAD_DIM)[:, perm].reshape(D, ATT_WIDTH)
    misc = jnp.concatenate([wi, im, fm, jnp.zeros((D, LANES - N_IDX_HEADS - 2 * M_HEADS), w_in.dtype)], axis=1)
    wp = jnp.concatenate([qa, v, qi, k, ki, ki, misc, qm, km, vm, om], axis=1).astype(BF16)
    assert wp.shape[1] == N_PACK
    gate_bias = jnp.broadcast_to(jnp.concatenate([b_i, b_f]).astype(F32)[:, None], (2 * M_HEADS, LANES))
    woa = w_out[:ATT_WIDTH].reshape(N_HEADS, HEAD_DIM, -1)[perm].reshape(ATT_WIDTH, -1).astype(BF16)
    woh = w_out[ATT_WIDTH:].astype(BF16)
    wt = jnp.concatenate([seg[1].T, seg[2].T, ki.T, jnp.zeros((LANES - IDX_DIM, D), w_in.dtype),
                          seg[0].T, qi.T, misc.T], axis=0).astype(BF16)
    assert wt.shape[0] == N_TPACK
    return wp, wt, gate_bias, woa, woh, w_up.astype(BF16), w_down.astype(BF16)


def _layer(x, packed, g1, g2, mnorm, rel_bias, gf, final_norm, past):
    wp, wt, gate_bias, woa, woh, wup, wdn = packed
    B, T, D = x.shape
    x2 = x.reshape(B * T, D)
    tm = math.gcd(T if past is None else B * T, 512)
    kv_w = N_KV_HEADS * HEAD_DIM
    if past is None:
        (misc, qm, km, vm, om, kb, kib, kt, vt, kit, vtb, qat, qit, misct) = _inproj(
            x2, g1.reshape(1, D), wp, wt, B, T, tm)
        attn = _dsa_prompt(rel_bias, qat, qit, misct, kib, kb, vtb, B, T)
        state = None
        k_new = kt.reshape(B, N_KV_HEADS, HEAD_DIM, T).transpose(0, 3, 1, 2)
        v_new = vt.reshape(B, N_KV_HEADS, HEAD_DIM, T).transpose(0, 3, 1, 2)
        ki_new = kit.transpose(0, 2, 1)
    else:
        (misc, qm, km, vm, om, qa, qi, k, v, ki) = _inproj(x2, g1.reshape(1, D), wp, None, B, T, tm)
        page_table, cache_k, cache_v, cache_kidx, c0, n0, m0 = past
        n_pool = cache_k.shape[0]
        ckt = cache_k.transpose(0, 2, 3, 1).reshape(n_pool, kv_w, PAGE_SIZE)
        cvt = cache_v.transpose(0, 2, 3, 1).reshape(n_pool, kv_w, PAGE_SIZE)
        ckit = cache_kidx.transpose(0, 2, 1)
        attn = _dsa_sample(page_table, rel_bias, qa, qi, misc, ki, k, v, ckit, ckt, cvt, B, T)
        state = (c0, n0, jnp.broadcast_to(m0[..., None], m0.shape + (LANES,)))
        k_new = k.reshape(B, T, N_KV_HEADS, HEAD_DIM)
        v_new = v.reshape(B, T, N_KV_HEADS, HEAD_DIM)
        ki_new = ki.reshape(B, T, IDX_DIM)
    h, c_new, n_new, m_new = _mlstm(qm, km, vm, om, misc, gate_bias, mnorm.reshape(1, M_WIDTH), state,
                                    B, T, BF16 if past is None else F32)
    y = _post(x2, attn, h, woa, woh, g2.reshape(1, D), wup, wdn, gf.reshape(1, D), final_norm, tm)
    return (y.reshape(B, T, D), k_new, v_new, ki_new, c_new, n_new, m_new[..., 0])


def kernel(x_prompt, x_sample, cache_k, cache_v, cache_kidx, page_table, state_C, state_n, state_m,
           w_in, b_igate, b_fgate, mlstm_norm, rel_bias, w_out, norm1, norm2, w_up, w_down, norm_f):
    depth = w_in.shape[0]
    xp, xs = x_prompt, x_sample
    outs_p, outs_s = [], []
    for l in range(depth):
        packed = _pack_layer_weights(w_in[l], b_igate[l], b_fgate[l], w_out[l], w_up[l], w_down[l])
        last = l == depth - 1
        common = (packed, norm1[l], norm2[l], mlstm_norm[l], rel_bias, norm_f, last)
        rp = _layer(xp, *common, None)
        rs = _layer(xs, *common, (page_table, cache_k[l], cache_v[l], cache_kidx[l],
                                  state_C[l], state_n[l], state_m[l]))
        xp, xs = rp[0], rs[0]
        outs_p.append(rp[1:])
        outs_s.append(rs[1:])
    stack = lambda outs, i: jnp.stack([o[i] for o in outs])
    return ((xp, xs) + tuple(stack(outs_p, i) for i in range(6))
            + tuple(stack(outs_s, i) for i in range(6)))
```

```python
import functools
import math

import numpy as np
import jax
import jax.numpy as jnp
from jax import lax
from jax.experimental import pallas as pl
from jax.experimental.pallas import tpu as pltpu

F32 = jnp.float32
BF16 = jnp.bfloat16
I32 = jnp.int32

N_HEADS = 8
HEAD_DIM = 64
N_KV_HEADS = 2
GROUP = N_HEADS // N_KV_HEADS
N_IDX_HEADS = 8
IDX_DIM = 64
TOPK_MAX = 256
N_BUCKETS = 32
MAX_DISTANCE = 128
M_HEADS = 4
M_HEAD_DIM = 128
PAGE_SIZE = 128
EPS = 1e-6
NEG = -1e30
LOG2E = math.log2(math.e)
ATT_WIDTH = N_HEADS * HEAD_DIM
M_WIDTH = M_HEADS * M_HEAD_DIM

LANES = 128
SUBLANES = 8
VMEM_LIMIT = 56 * 1024 * 1024

C_QA = 0
C_V = C_QA + ATT_WIDTH
C_QI = C_V + LANES
C_K = C_QI + N_IDX_HEADS * IDX_DIM
C_KI2 = C_K + LANES
C_MISC = C_KI2 + LANES
C_QM = C_MISC + LANES
C_KM = C_QM + M_WIDTH
C_VM = C_KM + M_WIDTH
C_OM = C_VM + M_WIDTH
N_PACK = C_OM + M_WIDTH
MISC_W = 0
MISC_I = 8
MISC_F = 12

HEAD_PERM = (0, 4, 1, 5, 2, 6, 3, 7)

TQ = 256
STRIP_BACK = 2
assert LANES * (STRIP_BACK - 1) >= MAX_DISTANCE
CKP = 256
VT_ROWS = LANES + 16
ML = 256

NT_DIMS = (((1,), (1,)), ((), ()))
TN_DIMS = (((0,), (0,)), ((), ()))


def _bucket_bounds():
    max_exact = N_BUCKETS // 2
    scale = (N_BUCKETS - max_exact) / math.log(MAX_DISTANCE / max_exact)

    def bucket(n, dt):
        if n < max_exact:
            return n
        val = np.log(np.asarray(max(n, 1), dt) / dt(max_exact)) * dt(scale)
        return min(max_exact + int(val), N_BUCKETS - 1)

    table = [bucket(n, np.float32) for n in range(MAX_DISTANCE + 2)]
    assert table == [bucket(n, np.float64) for n in range(MAX_DISTANCE + 2)]
    assert table[MAX_DISTANCE] == N_BUCKETS - 1
    return [next(d for d, b in enumerate(table) if b >= k) for k in range(N_BUCKETS)]


BUCKET_BOUNDS = _bucket_bounds()


def _cparams(n_axes):
    return pltpu.CompilerParams(dimension_semantics=("arbitrary",) * n_axes,
                                vmem_limit_bytes=VMEM_LIMIT)


def _const_spec(shape):
    nd = len(shape)
    return pl.BlockSpec(shape, lambda *_: (0,) * nd, pipeline_mode=pl.Buffered(1))


def _rms(x, g):
    return x * lax.rsqrt(jnp.mean(x * x, axis=-1, keepdims=True) + EPS) * g


def _inproj_mlstm(mm, misc_ref, qm_ref, km_ref, vm_ref, om_ref):
    misc_ref[...] = mm(C_MISC, LANES)
    qm_ref[...] = mm(C_QM, M_WIDTH).astype(qm_ref.dtype)
    km_ref[...] = (mm(C_KM, M_WIDTH) * (M_HEAD_DIM ** -0.5)).astype(km_ref.dtype)
    vm_ref[...] = mm(C_VM, M_WIDTH).astype(vm_ref.dtype)
    om_ref[...] = mm(C_OM, M_WIDTH)


def _inproj_rows_kernel(x_ref, g_ref, w_ref, misc_ref, qm_ref, km_ref, vm_ref, om_ref,
                        qa_ref, qi_ref, k_ref, v_ref, ki_ref):
    ub = _rms(x_ref[...], g_ref[...]).astype(BF16)
    z = jnp.dot(ub, w_ref[...], preferred_element_type=F32)
    mm = lambda c0, n: z[:, c0:c0 + n]
    _inproj_mlstm(mm, misc_ref, qm_ref, km_ref, vm_ref, om_ref)
    qa_ref[...] = mm(C_QA, ATT_WIDTH) * (HEAD_DIM ** -0.5)
    qi_ref[...] = mm(C_QI, N_IDX_HEADS * IDX_DIM)
    k_ref[...] = mm(C_K, LANES)
    v_ref[...] = mm(C_V, LANES)
    ki_ref[...] = mm(C_KI2, LANES)[:, :IDX_DIM]


R_K = 0
R_V = R_K + LANES
R_KI = R_V + LANES
R_QA = R_KI + LANES
R_QI = R_QA + ATT_WIDTH
R_MISC = R_QI + N_IDX_HEADS * IDX_DIM
N_TPACK = R_MISC + LANES


def _inproj_cols_kernel(x_ref, g_ref, w_ref, wt_ref, misc_ref, qm_ref, km_ref, vm_ref, om_ref,
                        kb_ref, kib_ref, kt_ref, vt_ref, kit_ref, vtb_ref, qat_ref, qit_ref,
                        misct_ref):
    ub = _rms(x_ref[...], g_ref[...]).astype(BF16)
    z = jnp.dot(ub, w_ref[:, C_K:], preferred_element_type=F32)
    mm = lambda c0, n: z[:, c0 - C_K:c0 - C_K + n]
    _inproj_mlstm(mm, misc_ref, qm_ref, km_ref, vm_ref, om_ref)
    kb_ref[...] = mm(C_K, LANES).astype(BF16)
    kib_ref[...] = mm(C_KI2, LANES)[:, :IDX_DIM].astype(BF16)

    zt = lax.dot_general(wt_ref[...], ub, NT_DIMS, preferred_element_type=F32)
    mt = lambda r0, n: zt[r0:r0 + n]

    kt_ref[0] = mt(R_K, LANES)
    vt = mt(R_V, LANES)
    vt_ref[0] = vt
    ones = jnp.ones((VT_ROWS - LANES, CKP), BF16)
    for j in range(vtb_ref.shape[1]):
        vtb_ref[0, j] = jnp.concatenate([vt[:, j * CKP:(j + 1) * CKP].astype(BF16), ones], axis=0)
    kit_ref[0] = mt(R_KI, IDX_DIM)
    qat_ref[0] = (mt(R_QA, ATT_WIDTH) * (HEAD_DIM ** -0.5 * LOG2E)).astype(BF16)
    qit_ref[0] = mt(R_QI, N_IDX_HEADS * IDX_DIM).astype(BF16)
    misct_ref[0] = mt(R_MISC, LANES)


def _inproj(x2, g1, wp, wt, B, T, tm):
    R, D = x2.shape
    assert R == B * T and R % tm == 0
    mdt = F32 if wt is None else BF16
    row = lambda i: (i, 0)
    outs = [(LANES, F32), (M_WIDTH, mdt), (M_WIDTH, mdt), (M_WIDTH, mdt), (M_WIDTH, F32)]
    in_specs = [pl.BlockSpec((tm, D), row), _const_spec((1, D)), _const_spec((D, N_PACK))]
    if wt is None:
        kern, args = _inproj_rows_kernel, (x2, g1, wp)
        outs += [(ATT_WIDTH, F32), (N_IDX_HEADS * IDX_DIM, F32), (LANES, F32), (LANES, F32), (IDX_DIM, F32)]
    else:
        kern, args = _inproj_cols_kernel, (x2, g1, wp, wt)
        in_specs.append(_const_spec(wt.shape))
        outs += [(LANES, BF16), (IDX_DIM, BF16)]
    out_specs = [pl.BlockSpec((tm, w), row) for w, _ in outs]
    out_shape = [jax.ShapeDtypeStruct((R, w), dt) for w, dt in outs]
    if wt is not None:
        assert T % tm == 0 and tm % CKP == 0
        tpb, cpt = T // tm, tm // CKP
        cols = lambda i: (i // tpb, 0, i % tpb)
        for w, dt in ((LANES, F32), (LANES, F32), (IDX_DIM, F32)):
            out_specs.append(pl.BlockSpec((1, w, tm), cols))
            out_shape.append(jax.ShapeDtypeStruct((B, w, T), dt))
        out_specs.append(pl.BlockSpec((1, cpt, VT_ROWS, CKP), lambda i: (i // tpb, i % tpb, 0, 0)))
        out_shape.append(jax.ShapeDtypeStruct((B, T // CKP, VT_ROWS, CKP), BF16))
        for w, dt in ((ATT_WIDTH, BF16), (N_IDX_HEADS * IDX_DIM, BF16), (LANES, F32)):
            out_specs.append(pl.BlockSpec((1, w, tm), cols))
            out_shape.append(jax.ShapeDtypeStruct((B, w, T), dt))
    return pl.pallas_call(
        kern,
        grid=(R // tm,),
        in_specs=in_specs,
        out_specs=out_specs,
        out_shape=out_shape,
        compiler_params=_cparams(1),
        name="inproj",
    )(*args)


def _post_kernel(ff_chunk, final_norm, x_ref, a_ref, h_ref, woa_ref, woh_ref, g2_ref, wup_ref,
                 wdn_ref, gf_ref, y_ref):
    mix = jnp.dot(a_ref[...].astype(BF16), woa_ref[...], preferred_element_type=F32)
    mix = mix + jnp.dot(h_ref[...].astype(BF16), woh_ref[...], preferred_element_type=F32)
    hres = x_ref[...] + mix
    f = _rms(hres, g2_ref[...]).astype(BF16)
    acc = hres
    for c0 in range(0, wup_ref.shape[1], ff_chunk):
        up = jnp.dot(f, wup_ref[:, c0:c0 + ff_chunk], preferred_element_type=F32)
        r = jnp.maximum(up, 0.0)
        acc = acc + jnp.dot((r * r).astype(BF16), wdn_ref[c0:c0 + ff_chunk, :],
                            preferred_element_type=F32)
    y_ref[...] = _rms(acc, gf_ref[...]) if final_norm else acc


def _post(x2, attn, h, woa, woh, g2, wup, wdn, gf, final_norm, tm):
    R, D = x2.shape
    dff = wup.shape[1]
    assert R % tm == 0
    row = lambda i: (i, 0)
    return pl.pallas_call(
        functools.partial(_post_kernel, min(dff, 1024), final_norm),
        grid=(R // tm,),
        in_specs=[pl.BlockSpec((tm, D), row), pl.BlockSpec((tm, ATT_WIDTH), row),
                  pl.BlockSpec((tm, M_WIDTH), row), _const_spec(woa.shape), _const_spec(woh.shape),
                  _const_spec((1, D)), _const_spec(wup.shape), _const_spec(wdn.shape),
                  _const_spec((1, D))],
        out_specs=pl.BlockSpec((tm, D), row),
        out_shape=jax.ShapeDtypeStruct((R, D), F32),
        compiler_params=_cparams(1),
        name="post",
    )(x2, attn, h, woa, woh, g2, wup, wdn, gf)


def _log_sigmoid(x):
    return -(jnp.maximum(-x, 0.0) + jnp.log1p(jnp.exp(-jnp.abs(x))))


def _mlstm_kernel(nvalid, has_state, *refs):
    if has_state:
        (q_ref, k_ref, v_ref, o_ref, misc_ref, gb_ref, mn_ref, c0_ref, n0_ref, m0_ref,
         h_ref, c_ref, n_ref, m_ref, z_scr, rows_scr) = refs
    else:
        (q_ref, k_ref, v_ref, o_ref, misc_ref, gb_ref, mn_ref,
         h_ref, c_ref, n_ref, m_ref, z_scr, rows_scr) = refs
    nb, nc, L = z_scr.shape[0], z_scr.shape[1], z_scr.shape[2]
    c = pl.program_id(1)

    def padded(x, dt):
        x = x.astype(dt)
        if nvalid == L:
            return x
        return jnp.concatenate([x, jnp.zeros((L - nvalid, x.shape[1]), dt)], axis=0)

    r2 = lax.broadcasted_iota(I32, (L, L), 0)
    c2 = lax.broadcasted_iota(I32, (L, L), 1)
    tril = r2 >= c2

    @pl.when(c == 0)
    def _():
        if has_state:
            c_ref[...] = c0_ref[...]
            n_ref[...] = n0_ref[...]
            m_ref[...] = m0_ref[...]
        else:
            c_ref[...] = jnp.zeros_like(c_ref)
            n_ref[...] = jnp.zeros_like(n_ref)
            m_ref[...] = jnp.zeros_like(m_ref)
        assert MISC_F == MISC_I + M_HEADS and MISC_I % SUBLANES == 0
        row8 = lax.broadcasted_iota(I32, (SUBLANES, L), 0)
        tok8 = lax.broadcasted_iota(I32, (SUBLANES, L), 1)
        bias8 = jnp.concatenate([gb_ref[...]] * (L // LANES), axis=1)
        triu = (r2 <= c2).astype(F32)
        for s in range(nb):
            for cc in range(nc):
                misc_t = padded(misc_ref[s, cc * nvalid:(cc + 1) * nvalid, :], F32).T
                gx = misc_t[MISC_I:MISC_I + SUBLANES] + bias8
                gates = jnp.where(row8 >= M_HEADS, _log_sigmoid(gx), gx)
                if nvalid != L:
                    gates = jnp.where(tok8 < nvalid, gates, jnp.where(row8 >= M_HEADS, 0.0, NEG))
                cum = jnp.dot(gates, triu, precision=lax.Precision.HIGHEST,
                              preferred_element_type=F32)
                rows = jnp.where(row8 >= M_HEADS, cum, gates)
                rows_scr[s, cc] = rows
                z_scr[s, cc] = jnp.concatenate([rows, jnp.zeros((LANES - SUBLANES, L), F32)], axis=0).T

    loaded = []
    for s in range(nb):
        loaded.append((z_scr[s, c], rows_scr[s, c], padded(q_ref[s], BF16), padded(k_ref[s], BF16),
                       padded(v_ref[s], BF16), padded(o_ref[s], F32), m_ref[s], c_ref[s], n_ref[s]))
    chains = [(s, hd) for s in range(nb) for hd in range(M_HEADS)]

    def operands(s, hd):
        z, rows, qb, kb, vb, ob, m_all, c_all, n_all = loaded[s]
        sl = slice(hd * M_HEAD_DIM, (hd + 1) * M_HEAD_DIM)
        return dict(
            sl=sl, q=qb[:, sl], k=kb[:, sl], v=vb[:, sl], o=ob[:, sl],
            icol=z[:, hd:hd + 1], bcol=z[:, M_HEADS + hd:M_HEADS + hd + 1],
            irow=rows[hd:hd + 1, :], brow=rows[M_HEADS + hd:M_HEADS + hd + 1, :],
            m_prev=m_all[hd:hd + 1, 0:1], s_prev=c_all[hd], n_prev=n_all[hd:hd + 1, :])

    def lane_sum(x):
        ones = jnp.ones((x.shape[1], LANES), BF16)
        head = x.astype(BF16)
        tail = (x - head.astype(F32)).astype(BF16)
        return (jnp.dot(head, ones, preferred_element_type=F32)
                + jnp.dot(tail, ones, preferred_element_type=F32))

    st = [operands(s, hd) for s, hd in chains]
    for x in st:
        x["qk"] = lax.dot_general(x["q"], x["k"], NT_DIMS, preferred_element_type=F32)
        x["qs"] = jnp.dot(x["q"], x["s_prev"].astype(BF16), preferred_element_type=F32)
        x["qn"] = lane_sum(x["q"].astype(F32) * x["n_prev"])
    for x in st:
        g = x["bcol"] + x["m_prev"]
        dm = jnp.where(tril, x["bcol"] - x["brow"] + x["irow"], NEG)
        x["mt"] = jnp.maximum(g, jnp.max(dm, axis=1, keepdims=True))
        x["gw"] = jnp.exp(g - x["mt"])
        x["qk"] = x["qk"] * jnp.exp(dm - x["mt"])
        b_last = x["bcol"][L - 1:L, :]
        g_last = b_last + x["m_prev"]
        a = b_last - x["bcol"] + x["icol"]
        x["m_new"] = jnp.maximum(g_last, jnp.max(a, axis=0, keepdims=True))
        x["sw"] = jnp.exp(g_last - x["m_new"])
        x["ak"] = jnp.exp(a - x["m_new"]) * x["k"].astype(F32)
    for x in st:
        x["pv"] = jnp.dot(x["qk"].astype(BF16), x["v"], preferred_element_type=F32)
        x["qksum"] = lane_sum(x["qk"])
        x["kv"] = lax.dot_general(x["ak"].astype(BF16), x["v"], TN_DIMS, preferred_element_type=F32)
    stores = []
    for (s, hd), x in zip(chains, st):
        num = x["gw"] * x["qs"] + x["pv"]
        den = x["gw"] * x["qn"] + x["qksum"]
        hh = num / jnp.maximum(jnp.abs(den), jnp.exp(-x["mt"]))
        hh = hh * lax.rsqrt(lane_sum(hh * hh) * (1.0 / M_HEAD_DIM) + EPS)
        hh = hh * mn_ref[:, x["sl"]] * jax.nn.sigmoid(x["o"])
        c_new = x["sw"] * x["s_prev"] + x["kv"]
        n_new = x["sw"] * x["n_prev"] + jnp.sum(x["ak"], axis=0, keepdims=True)
        stores.append((s, hd, x["sl"], hh[:nvalid].astype(h_ref.dtype), c_new, n_new,
                       jnp.broadcast_to(x["m_new"], (1, LANES))))
    for s, hd, sl, h_new, c_new, n_new, m_new in stores:
        h_ref[s, :, sl] = h_new
        c_ref[s, hd] = c_new
        n_ref[s, hd:hd + 1, :] = n_new
        m_ref[s, hd:hd + 1, :] = m_new


def _mlstm(qm, km, vm, om, misc, gate_bias, mnorm, state, B, T, hdt):
    L = next((c for c in (ML, LANES) if T % c == 0), LANES)
    nvalid = L if T % L == 0 else T
    assert nvalid <= L and T % nvalid == 0 and nvalid % SUBLANES == 0
    nc = T // nvalid
    nb = next(n for n in ((2, 1) if nvalid == L else (4, 2, 1)) if B % n == 0)
    seq3 = lambda a: a.reshape(B, T, a.shape[-1])
    blk = lambda w: pl.BlockSpec((nb, nvalid, w), lambda b, c: (b, c, 0))
    st_specs = [pl.BlockSpec((nb, M_HEADS, M_HEAD_DIM, M_HEAD_DIM), lambda b, c: (b, 0, 0, 0)),
                pl.BlockSpec((nb, M_HEADS, M_HEAD_DIM), lambda b, c: (b, 0, 0)),
                pl.BlockSpec((nb, M_HEADS, LANES), lambda b, c: (b, 0, 0))]
    in_specs = [blk(M_WIDTH), blk(M_WIDTH), blk(M_WIDTH), blk(M_WIDTH),
                pl.BlockSpec((nb, T, LANES), lambda b, c: (b, 0, 0)),
                pl.BlockSpec((SUBLANES, LANES), lambda b, c: (0, 0)),
                pl.BlockSpec((1, M_WIDTH), lambda b, c: (0, 0))]
    args = [seq3(qm), seq3(km), seq3(vm), seq3(om), seq3(misc), gate_bias, mnorm]
    if state is not None:
        in_specs += st_specs
        args += list(state)
    h, c_new, n_new, m_new = pl.pallas_call(
        functools.partial(_mlstm_kernel, nvalid, state is not None),
        grid=(B // nb, nc),
        in_specs=in_specs,
        out_specs=[blk(M_WIDTH)] + st_specs,
        out_shape=[jax.ShapeDtypeStruct((B, T, M_WIDTH), hdt),
                   jax.ShapeDtypeStruct((B, M_HEADS, M_HEAD_DIM, M_HEAD_DIM), F32),
                   jax.ShapeDtypeStruct((B, M_HEADS, M_HEAD_DIM), F32),
                   jax.ShapeDtypeStruct((B, M_HEADS, LANES), F32)],
        scratch_shapes=[pltpu.VMEM((nb, nc, L, LANES), F32), pltpu.VMEM((nb, nc, SUBLANES, L), F32)],
        compiler_params=_cparams(2),
        name="mlstm",
    )(*args)
    return h.reshape(B * T, M_WIDTH), c_new, n_new, m_new


def _sortable_key(score):
    bits = lax.bitcast_convert_type(score, I32)
    return bits ^ (lax.shift_right_arithmetic(bits, 31) & 0x7FFFFFFF)


def _key_to_score(key):
    return lax.bitcast_convert_type(key ^ (lax.shift_right_arithmetic(key, 31) & 0x7FFFFFFF), F32)


def _build_bias_strip(strip_ref, rb_ref, off, key_axis, log2_relative=False):
    ntiles, tile = strip_ref.shape[1], strip_ref.shape[2:]
    i = lax.broadcasted_iota(I32, tile, 1 - key_axis)
    x = lax.broadcasted_iota(I32, tile, key_axis)

    def entry(b, h):
        if log2_relative:
            return (rb_ref[b, h] - rb_ref[N_BUCKETS - 1, h]) * LOG2E
        return rb_ref[b, h]

    for t in range(ntiles):
        dist = i + (off - LANES * t) - x
        for h in range(N_HEADS):
            val = jnp.full(tile, entry(0, h), F32)
            for b in range(1, N_BUCKETS):
                val = jnp.where(dist >= BUCKET_BOUNDS[b], entry(b, h), val)
            strip_ref[h, t] = val


def _counter(score_ref, nk, ck, key_axis):
    def reduce(fn, fold, fold_all, init):
        def body(c, acc):
            c0 = pl.multiple_of(c * ck, ck)
            sc = score_ref[:, pl.ds(c0, ck)] if key_axis == 1 else score_ref[pl.ds(c0, ck), :]
            val = fn(sc, c0 + lax.broadcasted_iota(I32, sc.shape, key_axis))
            if key_axis == 1:
                for j in range(ck // LANES):
                    acc = fold(acc, val[:, j * LANES:(j + 1) * LANES])
                return acc
            return fold(acc, fold_all(val.reshape(ck // SUBLANES, SUBLANES, val.shape[1]), axis=0))

        nq = score_ref.shape[1 - key_axis]
        acc0 = jnp.full((nq, LANES) if key_axis == 1 else (SUBLANES, nq), init, I32)
        return fold_all(lax.fori_loop(0, nk, body, acc0), axis=key_axis, keepdims=True)

    def count(pred):
        return reduce(lambda sc, idx: jnp.where(pred(sc, idx), 1, 0), jnp.add, jnp.sum, 0)

    def lowest(fn):
        return reduce(fn, jnp.minimum, jnp.min, jnp.iinfo(jnp.int32).max)

    return count, lowest


IMIN = jnp.iinfo(jnp.int32).min


def _kth_largest_by_count(count, qshape, n_sel):
    def bit_step(i, key):
        cand = key + lax.shift_left(jnp.int32(1), 31 - i)
        cand_score = _key_to_score(cand)
        return jnp.where(count(lambda sc, idx: sc >= cand_score) >= n_sel, cand, key)

    return _key_to_score(lax.fori_loop(0, 32, bit_step, jnp.full(qshape, IMIN, I32)))


def _bit_planes(words):
    a = list(words)
    j, m = 16, 0x0000FFFF
    while j:
        k = 0
        while k < 32:
            t = (a[k] ^ lax.shift_right_logical(a[k + j], j)) & m
            a[k] = a[k] ^ t
            a[k + j] = a[k + j] ^ lax.shift_left(t, j)
            k = (k + j + 1) & ~j
        j >>= 1
        m = (m ^ (m << j)) & 0xFFFFFFFF
        m = m - (1 << 32) if m >= (1 << 31) else m
    return a[::-1]


def _kth_largest_by_planes(score_ref, planes_ref, nk, ck, n_sel):
    nc = planes_ref.shape[1]
    nq = score_ref.shape[1]
    assert ck == 32 * SUBLANES

    def pack_chunk(c, _):
        c0 = pl.multiple_of(c * ck, ck)
        u = _sortable_key(score_ref[pl.ds(c0, ck), :]) ^ IMIN
        u = u.reshape(32, SUBLANES, nq)
        for b, plane in enumerate(_bit_planes([u[v] for v in range(32)])):
            planes_ref[b, c] = plane
        return 0

    lax.fori_loop(0, nk, pack_chunk, 0)
    cand0 = tuple(jnp.where(c < nk, jnp.full((SUBLANES, nq), -1, I32), 0) for c in range(nc))
    return _plane_search(planes_ref, cand0, n_sel, 0)


def _kth_largest_by_planes_lanes(score_ref, planes_ref, n_sel):
    nq, ntiles = score_ref.shape[0], score_ref.shape[1] // LANES
    cand0 = []
    for g in range(planes_ref.shape[1]):
        real = min(32, ntiles - 32 * g)
        words = [_sortable_key(score_ref[:, (32 * g + v) * LANES:(32 * g + v + 1) * LANES]) ^ IMIN
                 if v < real else jnp.zeros((nq, LANES), I32) for v in range(32)]
        for b, plane in enumerate(_bit_planes(words)):
            planes_ref[b, g] = plane
        cand0.append(jnp.full((nq, LANES), -(1 << (32 - real)), I32))
    return _plane_search(planes_ref, tuple(cand0), n_sel, 1)


def _plane_search(planes_ref, cand0, n_sel, key_axis):
    def bit_step(i, carry):
        cand, n_above, thr_u = carry
        b = 31 - i
        ones = [m & planes_ref[b, c] for c, m in enumerate(cand)]
        pop = lax.population_count(ones[0])
        for o in ones[1:]:
            pop = pop + lax.population_count(o)
        tot = jnp.sum(pop, axis=key_axis, keepdims=True)
        take = n_above + tot >= n_sel
        cand = tuple(jnp.where(take, o, m ^ o) for o, m in zip(ones, cand))
        n_above = jnp.where(take, n_above, n_above + tot)
        thr_u = thr_u | jnp.where(take, lax.shift_left(jnp.int32(1), b), 0)
        return cand, n_above, thr_u

    zero = jnp.zeros_like(jnp.sum(cand0[0], axis=key_axis, keepdims=True))
    _, _, thr_u = lax.fori_loop(0, 32, bit_step, (cand0, zero, zero))
    return _key_to_score(thr_u ^ IMIN)


def _select_topk(thr, check, reducers, qshape, n_sel, idx_bits):
    count, lowest = reducers
    imax = jnp.iinfo(jnp.int32).max

    def with_counts(t):
        return t, count(lambda sc, idx: sc > t), count(lambda sc, idx: sc >= t)

    thr, n_gt, n_ge = with_counts(thr)
    if check:
        good = jnp.min(jnp.where(n_gt < n_sel, jnp.where(n_ge >= n_sel, 1, 0), 0)) > 0
        thr, n_gt, n_ge = lax.cond(
            good, lambda _: (thr, n_gt, n_ge),
            lambda _: with_counts(_kth_largest_by_count(count, qshape, n_sel)), 0)
    split = n_ge > n_sel
    need = jnp.where(split, n_sel - n_gt, 0)
    most = jnp.max(need)

    def by_extraction(_):
        def step(k, last):
            nxt = lowest(lambda sc, idx: jnp.where(sc == thr, jnp.where(idx > last, idx, imax), imax))
            return jnp.where(k < need, nxt, last)

        return lax.fori_loop(0, most, step, jnp.full(qshape, -1, I32))

    def by_bisection(_):
        def idx_step(i, lo):
            cand = lo + lax.shift_left(jnp.int32(1), idx_bits - 1 - i)
            cnt = count(lambda sc, idx: jnp.where(sc == thr, idx, imax) < cand)
            return jnp.where(cnt < need, cand, lo)

        return lax.fori_loop(0, idx_bits, idx_step, jnp.zeros(qshape, I32))

    jstar = lax.cond(most <= idx_bits, by_extraction, by_bisection, 0)
    return thr, jnp.where(split, jstar, imax)


def _valid_mask(scores, idx, thr, jstar, qpos):
    sel = jnp.where(scores > thr, 1, jnp.where(scores == thr, jnp.where(idx <= jstar, 1, 0), 0))
    return jnp.where(idx <= qpos, sel, 0) > 0


def _group_queries(qa, tq):
    lane = lax.broadcasted_iota(I32, (tq, LANES), 1)
    out = []
    for n in range(N_KV_HEADS):
        keep = (lane < HEAD_DIM) if n == 0 else (lane >= HEAD_DIM)
        tiles = [jnp.where(keep, qa[:, j * LANES:(j + 1) * LANES], jnp.zeros((), qa.dtype))
                 for j in range(GROUP)]
        out.append(jnp.concatenate(tiles, axis=0).astype(BF16))
    return out


def _write_attn(out_ref, carries, tq):
    lane = lax.broadcasted_iota(I32, (tq, LANES), 1)
    res = [acc / l for (_, l, acc) in carries]
    for j in range(GROUP):
        tile = jnp.where(lane < HEAD_DIM, res[0][j * tq:(j + 1) * tq], res[1][j * tq:(j + 1) * tq])
        out_ref[:, j * LANES:(j + 1) * LANES] = tile.astype(out_ref.dtype)


def _dsa_prompt_kernel(n_sel, idx_bits, rb_ref, qat_ref, qit_ref, misct_ref, ki_ref, k_ref, vt_ref,
                       out_ref, score_ref, planes_ref, strip_ref, sa_ref, sb_ref, m_ref, acc_ref):
    qb = pl.program_id(1)
    q0 = qb * TQ
    nk = (q0 + TQ + CKP - 1) // CKP
    tiles_per_chunk = CKP // LANES
    back_tiles = STRIP_BACK
    strip_off = LANES * back_tiles

    @pl.when((pl.program_id(0) == 0) & (qb == 0))
    def _():
        _build_bias_strip(strip_ref, rb_ref, strip_off, 0, log2_relative=True)
        planes_ref[...] = jnp.zeros(planes_ref.shape, I32)

    qpos = q0 + lax.broadcasted_iota(I32, (1, TQ), 1)

    qit = qit_ref[0]
    qstack = jnp.concatenate([qit[h * IDX_DIM:(h + 1) * IDX_DIM] for h in range(N_IDX_HEADS)],
                             axis=1)
    w = misct_ref[0, MISC_W:MISC_W + N_IDX_HEADS, :] * (N_IDX_HEADS ** -0.5 * IDX_DIM ** -0.5)

    def score_chunk(c, _):
        c0 = pl.multiple_of(c * CKP, CKP)
        d = jnp.dot(ki_ref[pl.ds(c0, CKP), :], qstack, preferred_element_type=F32)
        d = jnp.maximum(d, 0.0)
        s = jnp.zeros((CKP, TQ), F32)
        for h in range(N_IDX_HEADS):
            s = s + d[:, h * TQ:(h + 1) * TQ] * w[h:h + 1, :]
        idx = c0 + lax.broadcasted_iota(I32, (CKP, TQ), 0)
        score_ref[pl.ds(c0, CKP), :] = jnp.where(idx <= qpos, s, NEG)
        return 0

    lax.fori_loop(0, nk, score_chunk, 0)

    def search(_):
        proposal = _kth_largest_by_planes(score_ref, planes_ref, nk, CKP, n_sel)
        return _select_topk(proposal, True, _counter(score_ref, nk, CKP, 0), (1, TQ), n_sel, idx_bits)

    def everything(_):
        return jnp.full((1, TQ), NEG, F32), jnp.full((1, TQ), jnp.iinfo(jnp.int32).max, I32)

    thr, jstar = lax.cond(q0 + TQ <= n_sel, everything, search, 0)

    qat = qat_ref[0]
    zeros = jnp.zeros((HEAD_DIM, TQ), qat.dtype)
    qgroups = []
    for n in range(N_KV_HEADS):
        tiles = []
        for g in range(GROUP):
            h = n * GROUP + g
            x = qat[h * HEAD_DIM:(h + 1) * HEAD_DIM]
            tiles.append(jnp.concatenate([x, zeros] if n == 0 else [zeros, x], axis=0))
        qgroups.append(jnp.concatenate(tiles, axis=1))

    def logits_into(s_ref, c):
        kc = k_ref[pl.ds(pl.multiple_of(c * CKP, CKP), CKP), :]
        for n in range(N_KV_HEADS):
            s_ref[n] = jnp.dot(kc, qgroups[n], preferred_element_type=F32)

    def attend_chunk(c, s_cur_ref, s_next_ref):
        if s_next_ref is not None:
            logits_into(s_next_ref, jnp.minimum(c + 1, nk - 1))
        c0 = pl.multiple_of(c * CKP, CKP)
        idx = c0 + lax.broadcasted_iota(I32, (CKP, TQ), 0)
        valid = _valid_mask(score_ref[pl.ds(c0, CKP), :], idx, thr, jstar, qpos)
        vct = vt_ref[0, c]
        tiles = [jnp.maximum(back_tiles + j - (qb * (TQ // LANES) - c * tiles_per_chunk), 0)
                 for j in range(tiles_per_chunk)]
        for n in range(N_KV_HEADS):
            m_old = m_ref[n]
            parts = []
            for g in range(GROUP):
                bias = jnp.concatenate([strip_ref[n * GROUP + g, t] for t in tiles], axis=0)
                parts.append(jnp.where(valid, s_cur_ref[n, :, g * TQ:(g + 1) * TQ] + bias, NEG))
            sm = jnp.concatenate(parts, axis=1)
            m_new = jnp.maximum(m_old, jnp.max(sm, axis=0, keepdims=True))
            p = jnp.exp2((sm - m_new).astype(BF16))
            acc_ref[n] = jnp.exp2(m_old - m_new) * acc_ref[n] + jnp.dot(vct, p, preferred_element_type=F32)
            m_ref[n] = m_new

    m_ref[...] = jnp.full(m_ref.shape, NEG, F32)
    acc_ref[...] = jnp.zeros(acc_ref.shape, F32)
    logits_into(sa_ref, 0)

    def attend_pair(i, _):
        attend_chunk(2 * i, sa_ref, sb_ref)
        attend_chunk(2 * i + 1, sb_ref, sa_ref)
        return 0

    lax.fori_loop(0, nk // 2, attend_pair, 0)

    @pl.when(nk % 2 == 1)
    def _():
        attend_chunk(nk - 1, sa_ref, None)

    carries = [(None, acc_ref[n]) for n in range(N_KV_HEADS)]
    res = [acc[:LANES] / acc[LANES:LANES + 1] for (_, acc) in carries]
    row = lax.broadcasted_iota(I32, (LANES, TQ), 0)
    for j in range(GROUP):
        cols = slice(j * TQ, (j + 1) * TQ)
        tile_t = jnp.where(row < HEAD_DIM, res[0][:, cols], res[1][:, cols])
        out_ref[:, j * LANES:(j + 1) * LANES] = tile_t.T.astype(out_ref.dtype)


def _dsa_prompt(rel_bias, qat, qit, misct, kib, kb, vtb, B, T):
    assert T % CKP == 0 and T % TQ == 0
    nq = T // TQ
    n_sel = min(TOPK_MAX, T // 4)
    qcols = lambda w: pl.BlockSpec((1, w, TQ), lambda b, q: (b, 0, q))
    seq = lambda w: pl.BlockSpec((T, w), lambda b, q: (b, 0))
    return pl.pallas_call(
        functools.partial(_dsa_prompt_kernel, n_sel, max(1, (T - 1).bit_length())),
        grid=(B, nq),
        in_specs=[pl.BlockSpec(memory_space=pltpu.SMEM), qcols(ATT_WIDTH), qcols(N_IDX_HEADS * IDX_DIM),
                  qcols(LANES), seq(IDX_DIM), seq(LANES),
                  pl.BlockSpec((1, T // CKP, VT_ROWS, CKP), lambda b, q: (b, 0, 0, 0))],
        out_specs=pl.BlockSpec((TQ, ATT_WIDTH), lambda b, q: (b * nq + q, 0)),
        out_shape=jax.ShapeDtypeStruct((B * T, ATT_WIDTH), BF16),
        scratch_shapes=[pltpu.VMEM((T, TQ), F32),
                        pltpu.VMEM((32, T // CKP, SUBLANES, TQ), I32),
                        pltpu.VMEM((N_HEADS, STRIP_BACK + max(TQ, CKP) // LANES, LANES, TQ), F32),
                        pltpu.VMEM((N_KV_HEADS, CKP, GROUP * TQ), F32),
                        pltpu.VMEM((N_KV_HEADS, CKP, GROUP * TQ), F32),
                        pltpu.VMEM((N_KV_HEADS, 1, GROUP * TQ), F32),
                        pltpu.VMEM((N_KV_HEADS, VT_ROWS, GROUP * TQ), F32)],
        compiler_params=_cparams(2),
        name="dsa_prompt",
    )(rel_bias, qat, qit, misct, kib, kb, vtb)


def _page_pipeline(pt_ref, n_pages, caches, bufs, sems):
    def copies(bb, sl, j):
        pid = pt_ref[bb, j]
        cols = pl.ds(j * PAGE_SIZE, PAGE_SIZE)
        return [pltpu.make_async_copy(c.at[pid], buf.at[sl, :, cols], sems.at[sl, i])
                for i, (c, buf) in enumerate(zip(caches, bufs))]

    def start_all(bb, sl):
        for j in range(n_pages):
            for cp in copies(bb, sl, j):
                cp.start()

    def wait_all(bb, sl):
        for j in range(n_pages):
            for cp in copies(bb, sl, j):
                cp.wait()

    def step():
        b = pl.program_id(0)
        slot = b % 2

        @pl.when(b == 0)
        def _():
            start_all(0, 0)

        @pl.when(b + 1 < pl.num_programs(0))
        def _():
            start_all(b + 1, 1 - slot)

        wait_all(b, slot)
        return slot

    return step


def _pad_rows(x, rows):
    return jnp.concatenate([x, jnp.zeros((rows - x.shape[0], x.shape[1]), x.dtype)], axis=0)


def _sample_score_kernel(n_sel, idx_bits, n_pages, ts, ck, pt_ref, qi_ref, misc_ref, kin_ref,
                         ckit_hbm, score_ref, thr_ref, jst_ref, ki_buf, sems, planes_ref):
    b = pl.program_id(0)
    past = n_pages * PAGE_SIZE
    slot = _page_pipeline(pt_ref, n_pages, [ckit_hbm], [ki_buf], sems)()

    qi = qi_ref[...]
    qstack = jnp.concatenate([qi[:, h * IDX_DIM:(h + 1) * IDX_DIM] for h in range(N_IDX_HEADS)],
                             axis=0).astype(BF16)
    w = misc_ref[:, MISC_W:MISC_W + N_IDX_HEADS] * (N_IDX_HEADS ** -0.5 * IDX_DIM ** -0.5)
    d_past = jnp.dot(qstack, ki_buf[slot].astype(BF16), preferred_element_type=F32)
    d_own = lax.dot_general(qstack, _pad_rows(kin_ref[...], PAGE_SIZE).astype(BF16), NT_DIMS,
                            preferred_element_type=F32)
    d = jnp.maximum(jnp.concatenate([d_past, d_own], axis=1), 0.0)
    s = jnp.zeros((ts, past + PAGE_SIZE), F32)
    for h in range(N_IDX_HEADS):
        s = s + d[h * ts:(h + 1) * ts] * w[:, h:h + 1]
    idx = lax.broadcasted_iota(I32, s.shape, 1)
    qpos = past + lax.broadcasted_iota(I32, (ts, 1), 0)
    score_ref[pl.ds(pl.multiple_of(b * ts, ts), ts), :] = jnp.where(idx <= qpos, s, NEG)

    @pl.when(b == pl.num_programs(0) - 1)
    def _():
        reducers = _counter(score_ref, score_ref.shape[1] // ck, ck, 1)
        qshape = (score_ref.shape[0], 1)
        thr = _kth_largest_by_planes_lanes(score_ref, planes_ref, n_sel)
        thr, jstar = _select_topk(thr, True, reducers, qshape, n_sel, idx_bits)
        thr_ref[...] = jnp.broadcast_to(thr, thr_ref.shape)
        jst_ref[...] = jnp.broadcast_to(jstar, jst_ref.shape)


def _sample_attend_kernel(n_pages, ts, pt_ref, rb_ref, qa_ref, score_ref, thr_ref, jst_ref, kn_ref,
                          vn_ref, ckt_hbm, cvt_hbm, out_ref, k_buf, v_buf, sems, strip_ref):
    b = pl.program_id(0)
    past = n_pages * PAGE_SIZE
    rows = N_HEADS * ts

    @pl.when(b == 0)
    def _():
        _build_bias_strip(strip_ref, rb_ref, LANES * (strip_ref.shape[1] - 1), 1)

    slot = _page_pipeline(pt_ref, n_pages, [ckt_hbm, cvt_hbm], [k_buf, v_buf], sems)()

    q2 = jnp.concatenate(_group_queries(qa_ref[...], ts), axis=0)
    k_own = _pad_rows(kn_ref[...], PAGE_SIZE).astype(BF16)
    v_own = _pad_rows(vn_ref[...], PAGE_SIZE).astype(BF16)
    s_past = jnp.dot(q2, k_buf[slot].astype(BF16), preferred_element_type=F32)
    s_own = lax.dot_general(q2, k_own, NT_DIMS, preferred_element_type=F32)
    far = strip_ref[:, 0].reshape(rows, LANES)[:, 0:1]
    near = [strip_ref[:, t].reshape(rows, LANES) for t in (1, 2)]
    s = jnp.concatenate([s_past[:, :past - PAGE_SIZE] + far, s_past[:, past - PAGE_SIZE:] + near[0],
                         s_own + near[1]], axis=1)

    qpos = past + lax.broadcasted_iota(I32, (ts, 1), 0)
    scores = score_ref[...]
    valid = _valid_mask(scores, lax.broadcasted_iota(I32, scores.shape, 1), thr_ref[:, 0:1],
                        jst_ref[:, 0:1], qpos)
    s = jnp.where(valid[None], s.reshape(N_HEADS, ts, past + PAGE_SIZE), NEG).reshape(rows, -1)
    m = jnp.max(s, axis=1, keepdims=True)
    p = jnp.exp(s - m)
    l = jnp.sum(p, axis=1, keepdims=True)
    pb = p.astype(BF16)
    pv = lax.dot_general(pb[:, :past], v_buf[slot].astype(BF16), NT_DIMS, preferred_element_type=F32)
    pv = pv + jnp.dot(pb[:, past:], v_own, preferred_element_type=F32)
    half = GROUP * ts
    carries = [(None, l[n * half:(n + 1) * half], pv[n * half:(n + 1) * half]) for n in range(N_KV_HEADS)]
    _write_attn(out_ref, carries, ts)


def _dsa_sample(page_table, rel_bias, qa, qi, misc, ki_new, k_new, v_new, ckit, ckt, cvt, DB, ts):
    n_pages = page_table.shape[1]
    past = n_pages * PAGE_SIZE
    n_sel = min(TOPK_MAX, (past + ts) // 4)
    lpad = past + PAGE_SIZE
    idx_bits = max(1, (lpad - 1).bit_length())
    ck = LANES * math.gcd(lpad // LANES, 5)
    assert ts % SUBLANES == 0 and ts <= PAGE_SIZE and n_pages >= 1
    blk = lambda w: pl.BlockSpec((ts, w), lambda b, pt: (b, 0))
    whole = lambda w: pl.BlockSpec((DB * ts, w), lambda b, pt: (0, 0))
    hbm = pl.BlockSpec(memory_space=pl.ANY)
    keys, thr, jstar = pl.pallas_call(
        functools.partial(_sample_score_kernel, n_sel, idx_bits, n_pages, ts, ck),
        grid_spec=pltpu.PrefetchScalarGridSpec(
            num_scalar_prefetch=1,
            grid=(DB,),
            in_specs=[blk(N_IDX_HEADS * IDX_DIM), blk(LANES), blk(IDX_DIM), hbm],
            out_specs=[whole(lpad), whole(LANES), whole(LANES)],
            scratch_shapes=[pltpu.VMEM((2, IDX_DIM, past), F32), pltpu.SemaphoreType.DMA((2, 1)),
                            pltpu.VMEM((32, -(-lpad // (32 * LANES)), DB * ts, LANES), I32)]),
        out_shape=[jax.ShapeDtypeStruct((DB * ts, lpad), F32),
                   jax.ShapeDtypeStruct((DB * ts, LANES), F32),
                   jax.ShapeDtypeStruct((DB * ts, LANES), I32)],
        compiler_params=_cparams(1),
        name="sample_score",
    )(page_table, qi, misc, ki_new, ckit)
    return pl.pallas_call(
        functools.partial(_sample_attend_kernel, n_pages, ts),
        grid_spec=pltpu.PrefetchScalarGridSpec(
            num_scalar_prefetch=1,
            grid=(DB,),
            in_specs=[pl.BlockSpec(memory_space=pltpu.SMEM), blk(ATT_WIDTH), blk(lpad), blk(LANES),
                      blk(LANES), blk(LANES), blk(LANES), hbm, hbm],
            out_specs=blk(ATT_WIDTH),
            scratch_shapes=[pltpu.VMEM((2, LANES, past), F32), pltpu.VMEM((2, LANES, past), F32),
                            pltpu.SemaphoreType.DMA((2, 2)),
                            pltpu.VMEM((N_HEADS, 3, ts, LANES), F32)]),
        out_shape=jax.ShapeDtypeStruct((DB * ts, ATT_WIDTH), F32),
        compiler_params=_cparams(1),
        name="sample_attend",
    )(page_table, rel_bias, qa, keys, thr, jstar, k_new, v_new, ckt, cvt)


def _pack_layer_weights(w_in, b_i, b_f, w_out, w_up, w_down):
    D = w_in.shape[0]
    sizes = (ATT_WIDTH, N_KV_HEADS * HEAD_DIM, N_KV_HEADS * HEAD_DIM, N_IDX_HEADS * IDX_DIM, IDX_DIM,
             N_IDX_HEADS, M_WIDTH, M_WIDTH, M_WIDTH, M_WIDTH, M_HEADS, M_HEADS)
    assert w_in.shape[1] == sum(sizes)
    pts = np.cumsum((0,) + sizes)
    seg = [w_in[:, pts[i]:pts[i + 1]] for i in range(len(sizes))]
    qa, k, v, qi, ki, wi, qm, km, vm, om, im, fm = seg
    perm = np.asarray(HEAD_PERM)
    qa = qa.reshape(D, N_HEADS, HEAD_DIM)[:, perm].reshape(D, ATT_WIDTH)
    misc = jnp.concatenate([wi, im, fm, jnp.zeros((D, LANES - N_IDX_HEADS - 2 * M_HEADS), w_in.dtype)], axis=1)
    wp = jnp.concatenate([qa, v, qi, k, ki, ki, misc, qm, km, vm, om], axis=1).astype(BF16)
    assert wp.shape[1] == N_PACK
    gate_bias = jnp.broadcast_to(jnp.concatenate([b_i, b_f]).astype(F32)[:, None], (2 * M_HEADS, LANES))
    woa = w_out[:ATT_WIDTH].reshape(N_HEADS, HEAD_DIM, -1)[perm].reshape(ATT_WIDTH, -1).astype(BF16)
    woh = w_out[ATT_WIDTH:].astype(BF16)
    wt = jnp.concatenate([seg[1].T, seg[2].T, ki.T, jnp.zeros((LANES - IDX_DIM, D), w_in.dtype),
                          seg[0].T, qi.T, misc.T], axis=0).astype(BF16)
    assert wt.shape[0] == N_TPACK
    return wp, wt, gate_bias, woa, woh, w_up.astype(BF16), w_down.astype(BF16)


def _layer(x, packed, g1, g2, mnorm, rel_bias, gf, final_norm, past):
    wp, wt, gate_bias, woa, woh, wup, wdn = packed
    B, T, D = x.shape
    x2 = x.reshape(B * T, D)
    tm = math.gcd(T if past is None else B * T, 512)
    kv_w = N_KV_HEADS * HEAD_DIM
    if past is None:
        (misc, qm, km, vm, om, kb, kib, kt, vt, kit, vtb, qat, qit, misct) = _inproj(
            x2, g1.reshape(1, D), wp, wt, B, T, tm)
        attn = _dsa_prompt(rel_bias, qat, qit, misct, kib, kb, vtb, B, T)
        state = None
        k_new = kt.reshape(B, N_KV_HEADS, HEAD_DIM, T).transpose(0, 3, 1, 2)
        v_new = vt.reshape(B, N_KV_HEADS, HEAD_DIM, T).transpose(0, 3, 1, 2)
        ki_new = kit.transpose(0, 2, 1)
    else:
        (misc, qm, km, vm, om, qa, qi, k, v, ki) = _inproj(x2, g1.reshape(1, D), wp, None, B, T, tm)
        page_table, cache_k, cache_v, cache_kidx, c0, n0, m0 = past
        n_pool = cache_k.shape[0]
        ckt = cache_k.transpose(0, 2, 3, 1).reshape(n_pool, kv_w, PAGE_SIZE)
        cvt = cache_v.transpose(0, 2, 3, 1).reshape(n_pool, kv_w, PAGE_SIZE)
        ckit = cache_kidx.transpose(0, 2, 1)
        attn = _dsa_sample(page_table, rel_bias, qa, qi, misc, ki, k, v, ckit, ckt, cvt, B, T)
        state = (c0, n0, jnp.broadcast_to(m0[..., None], m0.shape + (LANES,)))
        k_new = k.reshape(B, T, N_KV_HEADS, HEAD_DIM)
        v_new = v.reshape(B, T, N_KV_HEADS, HEAD_DIM)
        ki_new = ki.reshape(B, T, IDX_DIM)
    h, c_new, n_new, m_new = _mlstm(qm, km, vm, om, misc, gate_bias, mnorm.reshape(1, M_WIDTH), state,
                                    B, T, BF16 if past is None else F32)
    y = _post(x2, attn, h, woa, woh, g2.reshape(1, D), wup, wdn, gf.reshape(1, D), final_norm, tm)
    return (y.reshape(B, T, D), k_new, v_new, ki_new, c_new, n_new, m_new[..., 0])


def kernel(x_prompt, x_sample, cache_k, cache_v, cache_kidx, page_table, state_C, state_n, state_m,
           w_in, b_igate, b_fgate, mlstm_norm, rel_bias, w_out, norm1, norm2, w_up, w_down, norm_f):
    depth = w_in.shape[0]
    xp, xs = x_prompt, x_sample
    outs_p, outs_s = [], []
    for l in range(depth):
        packed = _pack_layer_weights(w_in[l], b_igate[l], b_fgate[l], w_out[l], w_up[l], w_down[l])
        last = l == depth - 1
        common = (packed, norm1[l], norm2[l], mlstm_norm[l], rel_bias, norm_f, last)
        rp = _layer(xp, *common, None)
        rs = _layer(xs, *common, (page_table, cache_k[l], cache_v[l], cache_kidx[l],
                                  state_C[l], state_n[l], state_m[l]))
        xp, xs = rp[0], rs[0]
        outs_p.append(rp[1:])
        outs_s.append(rs[1:])
    stack = lambda outs, i: jnp.stack([o[i] for o in outs])
    return ((xp, xs) + tuple(stack(outs_p, i) for i in range(6))
            + tuple(stack(outs_s, i) for i in range(6)))
```

```python
import functools
import math

import numpy as np
import jax
import jax.numpy as jnp
from jax import lax
from jax.experimental import pallas as pl
from jax.experimental.pallas import tpu as pltpu

F32 = jnp.float32
BF16 = jnp.bfloat16
I32 = jnp.int32

N_HEADS = 8
HEAD_DIM = 64
N_KV_HEADS = 2
GROUP = N_HEADS // N_KV_HEADS
N_IDX_HEADS = 8
IDX_DIM = 64
TOPK_MAX = 256
N_BUCKETS = 32
MAX_DISTANCE = 128
M_HEADS = 4
M_HEAD_DIM = 128
PAGE_SIZE = 128
EPS = 1e-6
NEG = -1e30
LOG2E = math.log2(math.e)
ATT_WIDTH = N_HEADS * HEAD_DIM
M_WIDTH = M_HEADS * M_HEAD_DIM

LANES = 128
SUBLANES = 8
VMEM_LIMIT = 56 * 1024 * 1024

C_QA = 0
C_V = C_QA + ATT_WIDTH
C_QI = C_V + LANES
C_K = C_QI + N_IDX_HEADS * IDX_DIM
C_KI2 = C_K + LANES
C_MISC = C_KI2 + LANES
C_QM = C_MISC + LANES
C_KM = C_QM + M_WIDTH
C_VM = C_KM + M_WIDTH
C_OM = C_VM + M_WIDTH
N_PACK = C_OM + M_WIDTH
MISC_W = 0
MISC_I = 8
MISC_F = 12

HEAD_PERM = (0, 4, 1, 5, 2, 6, 3, 7)

TQ = 256
STRIP_BACK = 2
assert LANES * (STRIP_BACK - 1) >= MAX_DISTANCE
CKP = 256
VT_ROWS = LANES + 16
ML = 256

NT_DIMS = (((1,), (1,)), ((), ()))
TN_DIMS = (((0,), (0,)), ((), ()))


def _bucket_bounds():
    max_exact = N_BUCKETS // 2
    scale = (N_BUCKETS - max_exact) / math.log(MAX_DISTANCE / max_exact)

    def bucket(n, dt):
        if n < max_exact:
            return n
        val = np.log(np.asarray(max(n, 1), dt) / dt(max_exact)) * dt(scale)
        return min(max_exact + int(val), N_BUCKETS - 1)

    table = [bucket(n, np.float32) for n in range(MAX_DISTANCE + 2)]
    assert table == [bucket(n, np.float64) for n in range(MAX_DISTANCE + 2)]
    assert table[MAX_DISTANCE] == N_BUCKETS - 1
    return [next(d for d, b in enumerate(table) if b >= k) for k in range(N_BUCKETS)]


BUCKET_BOUNDS = _bucket_bounds()


def _cparams(n_axes):
    return pltpu.CompilerParams(dimension_semantics=("arbitrary",) * n_axes,
                                vmem_limit_bytes=VMEM_LIMIT)


def _const_spec(shape):
    nd = len(shape)
    return pl.BlockSpec(shape, lambda *_: (0,) * nd, pipeline_mode=pl.Buffered(1))


def _rms(x, g):
    return x * lax.rsqrt(jnp.mean(x * x, axis=-1, keepdims=True) + EPS) * g


def _inproj_mlstm(mm, misc_ref, qm_ref, km_ref, vm_ref, om_ref):
    misc_ref[...] = mm(C_MISC, LANES)
    qm_ref[...] = mm(C_QM, M_WIDTH).astype(qm_ref.dtype)
    km_ref[...] = (mm(C_KM, M_WIDTH) * (M_HEAD_DIM ** -0.5)).astype(km_ref.dtype)
    vm_ref[...] = mm(C_VM, M_WIDTH).astype(vm_ref.dtype)
    om_ref[...] = mm(C_OM, M_WIDTH)


def _inproj_rows_kernel(x_ref, g_ref, w_ref, misc_ref, qm_ref, km_ref, vm_ref, om_ref,
                        qa_ref, qi_ref, k_ref, v_ref, ki_ref):
    ub = _rms(x_ref[...], g_ref[...]).astype(BF16)
    z = jnp.dot(ub, w_ref[...], preferred_element_type=F32)
    mm = lambda c0, n: z[:, c0:c0 + n]
    _inproj_mlstm(mm, misc_ref, qm_ref, km_ref, vm_ref, om_ref)
    qa_ref[...] = mm(C_QA, ATT_WIDTH) * (HEAD_DIM ** -0.5)
    qi_ref[...] = mm(C_QI, N_IDX_HEADS * IDX_DIM)
    k_ref[...] = mm(C_K, LANES)
    v_ref[...] = mm(C_V, LANES)
    ki_ref[...] = mm(C_KI2, LANES)[:, :IDX_DIM]


R_K = 0
R_V = R_K + LANES
R_KI = R_V + LANES
R_QA = R_KI + LANES
R_QI = R_QA + ATT_WIDTH
R_MISC = R_QI + N_IDX_HEADS * IDX_DIM
N_TPACK = R_MISC + LANES


def _inproj_cols_kernel(x_ref, g_ref, w_ref, wt_ref, misc_ref, qm_ref, km_ref, vm_ref, om_ref,
                        kb_ref, kib_ref, kt_ref, vt_ref, kit_ref, vtb_ref, qat_ref, qit_ref,
                        misct_ref):
    ub = _rms(x_ref[...], g_ref[...]).astype(BF16)
    z = jnp.dot(ub, w_ref[:, C_K:], preferred_element_type=F32)
    mm = lambda c0, n: z[:, c0 - C_K:c0 - C_K + n]
    _inproj_mlstm(mm, misc_ref, qm_ref, km_ref, vm_ref, om_ref)
    kb_ref[...] = mm(C_K, LANES).astype(BF16)
    kib_ref[...] = mm(C_KI2, LANES)[:, :IDX_DIM].astype(BF16)

    zt = lax.dot_general(wt_ref[...], ub, NT_DIMS, preferred_element_type=F32)
    mt = lambda r0, n: zt[r0:r0 + n]

    kt_ref[0] = mt(R_K, LANES)
    vt = mt(R_V, LANES)
    vt_ref[0] = vt
    ones = jnp.ones((VT_ROWS - LANES, CKP), BF16)
    for j in range(vtb_ref.shape[1]):
        vtb_ref[0, j] = jnp.concatenate([vt[:, j * CKP:(j + 1) * CKP].astype(BF16), ones], axis=0)
    kit_ref[0] = mt(R_KI, IDX_DIM)
    qat_ref[0] = (mt(R_QA, ATT_WIDTH) * (HEAD_DIM ** -0.5 * LOG2E)).astype(BF16)
    qit_ref[0] = mt(R_QI, N_IDX_HEADS * IDX_DIM).astype(BF16)
    misct_ref[0] = mt(R_MISC, LANES)


def _inproj(x2, g1, wp, wt, B, T, tm):
    R, D = x2.shape
    assert R == B * T and R % tm == 0
    mdt = F32 if wt is None else BF16
    row = lambda i: (i, 0)
    outs = [(LANES, F32), (M_WIDTH, mdt), (M_WIDTH, mdt), (M_WIDTH, mdt), (M_WIDTH, F32)]
    in_specs = [pl.BlockSpec((tm, D), row), _const_spec((1, D)), _const_spec((D, N_PACK))]
    if wt is None:
        kern, args = _inproj_rows_kernel, (x2, g1, wp)
        outs += [(ATT_WIDTH, F32), (N_IDX_HEADS * IDX_DIM, F32), (LANES, F32), (LANES, F32), (IDX_DIM, F32)]
    else:
        kern, args = _inproj_cols_kernel, (x2, g1, wp, wt)
        in_specs.append(_const_spec(wt.shape))
        outs += [(LANES, BF16), (IDX_DIM, BF16)]
    out_specs = [pl.BlockSpec((tm, w), row) for w, _ in outs]
    out_shape = [jax.ShapeDtypeStruct((R, w), dt) for w, dt in outs]
    if wt is not None:
        assert T % tm == 0 and tm % CKP == 0
        tpb, cpt = T // tm, tm // CKP
        cols = lambda i: (i // tpb, 0, i % tpb)
        for w, dt in ((LANES, F32), (LANES, F32), (IDX_DIM, F32)):
            out_specs.append(pl.BlockSpec((1, w, tm), cols))
            out_shape.append(jax.ShapeDtypeStruct((B, w, T), dt))
        out_specs.append(pl.BlockSpec((1, cpt, VT_ROWS, CKP), lambda i: (i // tpb, i % tpb, 0, 0)))
        out_shape.append(jax.ShapeDtypeStruct((B, T // CKP, VT_ROWS, CKP), BF16))
        for w, dt in ((ATT_WIDTH, BF16), (N_IDX_HEADS * IDX_DIM, BF16), (LANES, F32)):
            out_specs.append(pl.BlockSpec((1, w, tm), cols))
            out_shape.append(jax.ShapeDtypeStruct((B, w, T), dt))
    return pl.pallas_call(
        kern,
        grid=(R // tm,),
        in_specs=in_specs,
        out_specs=out_specs,
        out_shape=out_shape,
        compiler_params=_cparams(1),
        name="inproj",
    )(*args)


def _post_kernel(ff_chunk, final_norm, x_ref, a_ref, h_ref, woa_ref, woh_ref, g2_ref, wup_ref,
                 wdn_ref, gf_ref, y_ref):
    mix = jnp.dot(a_ref[...].astype(BF16), woa_ref[...], preferred_element_type=F32)
    mix = mix + jnp.dot(h_ref[...].astype(BF16), woh_ref[...], preferred_element_type=F32)
    hres = x_ref[...] + mix
    f = _rms(hres, g2_ref[...]).astype(BF16)
    acc = hres
    for c0 in range(0, wup_ref.shape[1], ff_chunk):
        up = jnp.dot(f, wup_ref[:, c0:c0 + ff_chunk], preferred_element_type=F32)
        r = jnp.maximum(up, 0.0)
        acc = acc + jnp.dot((r * r).astype(BF16), wdn_ref[c0:c0 + ff_chunk, :],
                            preferred_element_type=F32)
    y_ref[...] = _rms(acc, gf_ref[...]) if final_norm else acc


def _post(x2, attn, h, woa, woh, g2, wup, wdn, gf, final_norm, tm):
    R, D = x2.shape
    dff = wup.shape[1]
    assert R % tm == 0
    row = lambda i: (i, 0)
    return pl.pallas_call(
        functools.partial(_post_kernel, min(dff, 1024), final_norm),
        grid=(R // tm,),
        in_specs=[pl.BlockSpec((tm, D), row), pl.BlockSpec((tm, ATT_WIDTH), row),
                  pl.BlockSpec((tm, M_WIDTH), row), _const_spec(woa.shape), _const_spec(woh.shape),
                  _const_spec((1, D)), _const_spec(wup.shape), _const_spec(wdn.shape),
                  _const_spec((1, D))],
        out_specs=pl.BlockSpec((tm, D), row),
        out_shape=jax.ShapeDtypeStruct((R, D), F32),
        compiler_params=_cparams(1),
        name="post",
    )(x2, attn, h, woa, woh, g2, wup, wdn, gf)


def _log_sigmoid(x):
    return -(jnp.maximum(-x, 0.0) + jnp.log1p(jnp.exp(-jnp.abs(x))))


def _mlstm_kernel(nvalid, has_state, *refs):
    if has_state:
        (q_ref, k_ref, v_ref, o_ref, misc_ref, gb_ref, mn_ref, c0_ref, n0_ref, m0_ref,
         h_ref, c_ref, n_ref, m_ref, z_scr, rows_scr) = refs
    else:
        (q_ref, k_ref, v_ref, o_ref, misc_ref, gb_ref, mn_ref,
         h_ref, c_ref, n_ref, m_ref, z_scr, rows_scr) = refs
    nb, nc, L = z_scr.shape[0], z_scr.shape[1], z_scr.shape[2]
    c = pl.program_id(1)

    def padded(x, dt):
        x = x.astype(dt)
        if nvalid == L:
            return x
        return jnp.concatenate([x, jnp.zeros((L - nvalid, x.shape[1]), dt)], axis=0)

    r2 = lax.broadcasted_iota(I32, (L, L), 0)
    c2 = lax.broadcasted_iota(I32, (L, L), 1)
    tril = r2 >= c2

    @pl.when(c == 0)
    def _():
        if has_state:
            c_ref[...] = c0_ref[...]
            n_ref[...] = n0_ref[...]
            m_ref[...] = m0_ref[...]
        else:
            c_ref[...] = jnp.zeros_like(c_ref)
            n_ref[...] = jnp.zeros_like(n_ref)
            m_ref[...] = jnp.zeros_like(m_ref)
        assert MISC_F == MISC_I + M_HEADS and MISC_I % SUBLANES == 0
        row8 = lax.broadcasted_iota(I32, (SUBLANES, L), 0)
        tok8 = lax.broadcasted_iota(I32, (SUBLANES, L), 1)
        bias8 = jnp.concatenate([gb_ref[...]] * (L // LANES), axis=1)
        triu = (r2 <= c2).astype(F32)
        for s in range(nb):
            for cc in range(nc):
                misc_t = padded(misc_ref[s, cc * nvalid:(cc + 1) * nvalid, :], F32).T
                gx = misc_t[MISC_I:MISC_I + SUBLANES] + bias8
                gates = jnp.where(row8 >= M_HEADS, _log_sigmoid(gx), gx)
                if nvalid != L:
                    gates = jnp.where(tok8 < nvalid, gates, jnp.where(row8 >= M_HEADS, 0.0, NEG))
                cum = jnp.dot(gates, triu, precision=lax.Precision.HIGHEST,
                              preferred_element_type=F32)
                rows = jnp.where(row8 >= M_HEADS, cum, gates)
                rows_scr[s, cc] = rows
                z_scr[s, cc] = jnp.concatenate([rows, jnp.zeros((LANES - SUBLANES, L), F32)], axis=0).T

    loaded = []
    for s in range(nb):
        loaded.append((z_scr[s, c], rows_scr[s, c], padded(q_ref[s], BF16), padded(k_ref[s], BF16),
                       padded(v_ref[s], BF16), padded(o_ref[s], F32), m_ref[s], c_ref[s], n_ref[s]))
    chains = [(s, hd) for s in range(nb) for hd in range(M_HEADS)]

    def operands(s, hd):
        z, rows, qb, kb, vb, ob, m_all, c_all, n_all = loaded[s]
        sl = slice(hd * M_HEAD_DIM, (hd + 1) * M_HEAD_DIM)
        return dict(
            sl=sl, q=qb[:, sl], k=kb[:, sl], v=vb[:, sl], o=ob[:, sl],
            icol=z[:, hd:hd + 1], bcol=z[:, M_HEADS + hd:M_HEADS + hd + 1],
            irow=rows[hd:hd + 1, :], brow=rows[M_HEADS + hd:M_HEADS + hd + 1, :],
            m_prev=m_all[hd:hd + 1, 0:1], s_prev=c_all[hd], n_prev=n_all[hd:hd + 1, :])

    def lane_sum(x):
        ones = jnp.ones((x.shape[1], LANES), BF16)
        head = x.astype(BF16)
        tail = (x - head.astype(F32)).astype(BF16)
        return (jnp.dot(head, ones, preferred_element_type=F32)
                + jnp.dot(tail, ones, preferred_element_type=F32))

    st = [operands(s, hd) for s, hd in chains]
    for x in st:
        x["qk"] = lax.dot_general(x["q"], x["k"], NT_DIMS, preferred_element_type=F32)
        x["qs"] = jnp.dot(x["q"], x["s_prev"].astype(BF16), preferred_element_type=F32)
        x["qn"] = lane_sum(x["q"].astype(F32) * x["n_prev"])
    for x in st:
        g = x["bcol"] + x["m_prev"]
        dm = jnp.where(tril, x["bcol"] - x["brow"] + x["irow"], NEG)
        x["mt"] = jnp.maximum(g, jnp.max(dm, axis=1, keepdims=True))
        x["gw"] = jnp.exp(g - x["mt"])
        x["qk"] = x["qk"] * jnp.exp(dm - x["mt"])
        b_last = x["bcol"][L - 1:L, :]
        g_last = b_last + x["m_prev"]
        a = b_last - x["bcol"] + x["icol"]
        x["m_new"] = jnp.maximum(g_last, jnp.max(a, axis=0, keepdims=True))
        x["sw"] = jnp.exp(g_last - x["m_new"])
        x["ak"] = jnp.exp(a - x["m_new"]) * x["k"].astype(F32)
    for x in st:
        x["pv"] = jnp.dot(x["qk"].astype(BF16), x["v"], preferred_element_type=F32)
        x["qksum"] = lane_sum(x["qk"])
        x["kv"] = lax.dot_general(x["ak"].astype(BF16), x["v"], TN_DIMS, preferred_element_type=F32)
    stores = []
    for (s, hd), x in zip(chains, st):
        num = x["gw"] * x["qs"] + x["pv"]
        den = x["gw"] * x["qn"] + x["qksum"]
        hh = num / jnp.maximum(jnp.abs(den), jnp.exp(-x["mt"]))
        hh = hh * lax.rsqrt(lane_sum(hh * hh) * (1.0 / M_HEAD_DIM) + EPS)
        hh = hh * mn_ref[:, x["sl"]] * jax.nn.sigmoid(x["o"])
        c_new = x["sw"] * x["s_prev"] + x["kv"]
        n_new = x["sw"] * x["n_prev"] + jnp.sum(x["ak"], axis=0, keepdims=True)
        stores.append((s, hd, x["sl"], hh[:nvalid].astype(h_ref.dtype), c_new, n_new,
                       jnp.broadcast_to(x["m_new"], (1, LANES))))
    for s, hd, sl, h_new, c_new, n_new, m_new in stores:
        h_ref[s, :, sl] = h_new
        c_ref[s, hd] = c_new
        n_ref[s, hd:hd + 1, :] = n_new
        m_ref[s, hd:hd + 1, :] = m_new


def _mlstm(qm, km, vm, om, misc, gate_bias, mnorm, state, B, T, hdt):
    L = next((c for c in (ML, LANES) if T % c == 0), LANES)
    nvalid = L if T % L == 0 else T
    assert nvalid <= L and T % nvalid == 0 and nvalid % SUBLANES == 0
    nc = T // nvalid
    nb = next(n for n in ((2, 1) if nvalid == L else (4, 2, 1)) if B % n == 0)
    seq3 = lambda a: a.reshape(B, T, a.shape[-1])
    blk = lambda w: pl.BlockSpec((nb, nvalid, w), lambda b, c: (b, c, 0))
    st_specs = [pl.BlockSpec((nb, M_HEADS, M_HEAD_DIM, M_HEAD_DIM), lambda b, c: (b, 0, 0, 0)),
                pl.BlockSpec((nb, M_HEADS, M_HEAD_DIM), lambda b, c: (b, 0, 0)),
                pl.BlockSpec((nb, M_HEADS, LANES), lambda b, c: (b, 0, 0))]
    in_specs = [blk(M_WIDTH), blk(M_WIDTH), blk(M_WIDTH), blk(M_WIDTH),
                pl.BlockSpec((nb, T, LANES), lambda b, c: (b, 0, 0)),
                pl.BlockSpec((SUBLANES, LANES), lambda b, c: (0, 0)),
                pl.BlockSpec((1, M_WIDTH), lambda b, c: (0, 0))]
    args = [seq3(qm), seq3(km), seq3(vm), seq3(om), seq3(misc), gate_bias, mnorm]
    if state is not None:
        in_specs += st_specs
        args += list(state)
    h, c_new, n_new, m_new = pl.pallas_call(
        functools.partial(_mlstm_kernel, nvalid, state is not None),
        grid=(B // nb, nc),
        in_specs=in_specs,
        out_specs=[blk(M_WIDTH)] + st_specs,
        out_shape=[jax.ShapeDtypeStruct((B, T, M_WIDTH), hdt),
                   jax.ShapeDtypeStruct((B, M_HEADS, M_HEAD_DIM, M_HEAD_DIM), F32),
                   jax.ShapeDtypeStruct((B, M_HEADS, M_HEAD_DIM), F32),
                   jax.ShapeDtypeStruct((B, M_HEADS, LANES), F32)],
        scratch_shapes=[pltpu.VMEM((nb, nc, L, LANES), F32), pltpu.VMEM((nb, nc, SUBLANES, L), F32)],
        compiler_params=_cparams(2),
        name="mlstm",
    )(*args)
    return h.reshape(B * T, M_WIDTH), c_new, n_new, m_new


def _sortable_key(score):
    bits = lax.bitcast_convert_type(score, I32)
    return bits ^ (lax.shift_right_arithmetic(bits, 31) & 0x7FFFFFFF)


def _key_to_score(key):
    return lax.bitcast_convert_type(key ^ (lax.shift_right_arithmetic(key, 31) & 0x7FFFFFFF), F32)


def _build_bias_strip(strip_ref, rb_ref, off, key_axis, log2_relative=False):
    ntiles, tile = strip_ref.shape[1], strip_ref.shape[2:]
    i = lax.broadcasted_iota(I32, tile, 1 - key_axis)
    x = lax.broadcasted_iota(I32, tile, key_axis)

    def entry(b, h):
        if log2_relative:
            return (rb_ref[b, h] - rb_ref[N_BUCKETS - 1, h]) * LOG2E
        return rb_ref[b, h]

    for t in range(ntiles):
        dist = i + (off - LANES * t) - x
        for h in range(N_HEADS):
            val = jnp.full(tile, entry(0, h), F32)
            for b in range(1, N_BUCKETS):
                val = jnp.where(dist >= BUCKET_BOUNDS[b], entry(b, h), val)
            strip_ref[h, t] = val


def _counter(score_ref, nk, ck, key_axis):
    def reduce(fn, fold, fold_all, init):
        def body(c, acc):
            c0 = pl.multiple_of(c * ck, ck)
            sc = score_ref[:, pl.ds(c0, ck)] if key_axis == 1 else score_ref[pl.ds(c0, ck), :]
            val = fn(sc, c0 + lax.broadcasted_iota(I32, sc.shape, key_axis))
            if key_axis == 1:
                for j in range(ck // LANES):
                    acc = fold(acc, val[:, j * LANES:(j + 1) * LANES])
                return acc
            return fold(acc, fold_all(val.reshape(ck // SUBLANES, SUBLANES, val.shape[1]), axis=0))

        nq = score_ref.shape[1 - key_axis]
        acc0 = jnp.full((nq, LANES) if key_axis == 1 else (SUBLANES, nq), init, I32)
        return fold_all(lax.fori_loop(0, nk, body, acc0), axis=key_axis, keepdims=True)

    def count(pred):
        return reduce(lambda sc, idx: jnp.where(pred(sc, idx), 1, 0), jnp.add, jnp.sum, 0)

    def lowest(fn):
        return reduce(fn, jnp.minimum, jnp.min, jnp.iinfo(jnp.int32).max)

    return count, lowest


IMIN = jnp.iinfo(jnp.int32).min


def _kth_largest_by_count(count, qshape, n_sel):
    def bit_step(i, key):
        cand = key + lax.shift_left(jnp.int32(1), 31 - i)
        cand_score = _key_to_score(cand)
        return jnp.where(count(lambda sc, idx: sc >= cand_score) >= n_sel, cand, key)

    return _key_to_score(lax.fori_loop(0, 32, bit_step, jnp.full(qshape, IMIN, I32)))


def _bit_planes(words):
    a = list(words)
    j, m = 16, 0x0000FFFF
    while j:
        k = 0
        while k < 32:
            t = (a[k] ^ lax.shift_right_logical(a[k + j], j)) & m
            a[k] = a[k] ^ t
            a[k + j] = a[k + j] ^ lax.shift_left(t, j)
            k = (k + j + 1) & ~j
        j >>= 1
        m = (m ^ (m << j)) & 0xFFFFFFFF
        m = m - (1 << 32) if m >= (1 << 31) else m
    return a[::-1]


def _kth_largest_by_planes(score_ref, planes_ref, nk, ck, n_sel):
    nc = planes_ref.shape[1]
    nq = score_ref.shape[1]
    assert ck == 32 * SUBLANES

    def pack_chunk(c, _):
        c0 = pl.multiple_of(c * ck, ck)
        u = _sortable_key(score_ref[pl.ds(c0, ck), :]) ^ IMIN
        u = u.reshape(32, SUBLANES, nq)
        for b, plane in enumerate(_bit_planes([u[v] for v in range(32)])):
            planes_ref[b, c] = plane
        return 0

    lax.fori_loop(0, nk, pack_chunk, 0)
    cand0 = tuple(jnp.where(c < nk, jnp.full((SUBLANES, nq), -1, I32), 0) for c in range(nc))
    return _plane_search(planes_ref, cand0, n_sel, 0)


def _kth_largest_by_planes_lanes(score_ref, planes_ref, n_sel):
    nq, ntiles = score_ref.shape[0], score_ref.shape[1] // LANES
    cand0 = []
    for g in range(planes_ref.shape[1]):
        real = min(32, ntiles - 32 * g)
        words = [_sortable_key(score_ref[:, (32 * g + v) * LANES:(32 * g + v + 1) * LANES]) ^ IMIN
                 if v < real else jnp.zeros((nq, LANES), I32) for v in range(32)]
        for b, plane in enumerate(_bit_planes(words)):
            planes_ref[b, g] = plane
        cand0.append(jnp.full((nq, LANES), -(1 << (32 - real)), I32))
    return _plane_search(planes_ref, tuple(cand0), n_sel, 1)


def _plane_search(planes_ref, cand0, n_sel, key_axis):
    def bit_step(i, carry):
        cand, n_above, thr_u = carry
        b = 31 - i
        ones = [m & planes_ref[b, c] for c, m in enumerate(cand)]
        pop = lax.population_count(ones[0])
        for o in ones[1:]:
            pop = pop + lax.population_count(o)
        tot = jnp.sum(pop, axis=key_axis, keepdims=True)
        take = n_above + tot >= n_sel
        cand = tuple(jnp.where(take, o, m ^ o) for o, m in zip(ones, cand))
        n_above = jnp.where(take, n_above, n_above + tot)
        thr_u = thr_u | jnp.where(take, lax.shift_left(jnp.int32(1), b), 0)
        return cand, n_above, thr_u

    zero = jnp.zeros_like(jnp.sum(cand0[0], axis=key_axis, keepdims=True))
    _, _, thr_u = lax.fori_loop(0, 32, bit_step, (cand0, zero, zero))
    return _key_to_score(thr_u ^ IMIN)


def _select_topk(thr, check, reducers, qshape, n_sel, idx_bits):
    count, lowest = reducers
    imax = jnp.iinfo(jnp.int32).max

    def with_counts(t):
        return t, count(lambda sc, idx: sc > t), count(lambda sc, idx: sc >= t)

    thr, n_gt, n_ge = with_counts(thr)
    if check:
        good = jnp.min(jnp.where(n_gt < n_sel, jnp.where(n_ge >= n_sel, 1, 0), 0)) > 0
        thr, n_gt, n_ge = lax.cond(
            good, lambda _: (thr, n_gt, n_ge),
            lambda _: with_counts(_kth_largest_by_count(count, qshape, n_sel)), 0)
    split = n_ge > n_sel
    need = jnp.where(split, n_sel - n_gt, 0)
    most = jnp.max(need)

    def by_extraction(_):
        def step(k, last):
            nxt = lowest(lambda sc, idx: jnp.where(sc == thr, jnp.where(idx > last, idx, imax), imax))
            return jnp.where(k < need, nxt, last)

        return lax.fori_loop(0, most, step, jnp.full(qshape, -1, I32))

    def by_bisection(_):
        def idx_step(i, lo):
            cand = lo + lax.shift_left(jnp.int32(1), idx_bits - 1 - i)
            cnt = count(lambda sc, idx: jnp.where(sc == thr, idx, imax) < cand)
            return jnp.where(cnt < need, cand, lo)

        return lax.fori_loop(0, idx_bits, idx_step, jnp.zeros(qshape, I32))

    jstar = lax.cond(most <= idx_bits, by_extraction, by_bisection, 0)
    return thr, jnp.where(split, jstar, imax)


def _valid_mask(scores, idx, thr, jstar, qpos):
    sel = jnp.where(scores > thr, 1, jnp.where(scores == thr, jnp.where(idx <= jstar, 1, 0), 0))
    return jnp.where(idx <= qpos, sel, 0) > 0


def _group_queries(qa, tq):
    lane = lax.broadcasted_iota(I32, (tq, LANES), 1)
    out = []
    for n in range(N_KV_HEADS):
        keep = (lane < HEAD_DIM) if n == 0 else (lane >= HEAD_DIM)
        tiles = [jnp.where(keep, qa[:, j * LANES:(j + 1) * LANES], jnp.zeros((), qa.dtype))
                 for j in range(GROUP)]
        out.append(jnp.concatenate(tiles, axis=0).astype(BF16))
    return out


def _write_attn(out_ref, carries, tq):
    lane = lax.broadcasted_iota(I32, (tq, LANES), 1)
    res = [acc / l for (_, l, acc) in carries]
    for j in range(GROUP):
        tile = jnp.where(lane < HEAD_DIM, res[0][j * tq:(j + 1) * tq], res[1][j * tq:(j + 1) * tq])
        out_ref[:, j * LANES:(j + 1) * LANES] = tile.astype(out_ref.dtype)


def _dsa_prompt_kernel(n_sel, idx_bits, rb_ref, qat_ref, qit_ref, misct_ref, ki_ref, k_ref, vt_ref,
                       out_ref, score_ref, planes_ref, strip_ref, sa_ref, sb_ref, m_ref, acc_ref):
    qb = pl.program_id(1)
    q0 = qb * TQ
    nk = (q0 + TQ + CKP - 1) // CKP
    tiles_per_chunk = CKP // LANES
    back_tiles = STRIP_BACK
    strip_off = LANES * back_tiles

    @pl.when((pl.program_id(0) == 0) & (qb == 0))
    def _():
        _build_bias_strip(strip_ref, rb_ref, strip_off, 0, log2_relative=True)
        planes_ref[...] = jnp.zeros(planes_ref.shape, I32)

    qpos = q0 + lax.broadcasted_iota(I32, (1, TQ), 1)

    qit = qit_ref[0]
    qstack = jnp.concatenate([qit[h * IDX_DIM:(h + 1) * IDX_DIM] for h in range(N_IDX_HEADS)],
                             axis=1)
    w = misct_ref[0, MISC_W:MISC_W + N_IDX_HEADS, :] * (N_IDX_HEADS ** -0.5 * IDX_DIM ** -0.5)

    def score_chunk(c, _):
        c0 = pl.multiple_of(c * CKP, CKP)
        d = jnp.dot(ki_ref[pl.ds(c0, CKP), :], qstack, preferred_element_type=F32)
        d = jnp.maximum(d, 0.0)
        s = jnp.zeros((CKP, TQ), F32)
        for h in range(N_IDX_HEADS):
            s = s + d[:, h * TQ:(h + 1) * TQ] * w[h:h + 1, :]
        idx = c0 + lax.broadcasted_iota(I32, (CKP, TQ), 0)
        score_ref[pl.ds(c0, CKP), :] = jnp.where(idx <= qpos, s, NEG)
        return 0

    lax.fori_loop(0, nk, score_chunk, 0)

    def search(_):
        proposal = _kth_largest_by_planes(score_ref, planes_ref, nk, CKP, n_sel)
        return _select_topk(proposal, True, _counter(score_ref, nk, CKP, 0), (1, TQ), n_sel, idx_bits)

    def everything(_):
        return jnp.full((1, TQ), NEG, F32), jnp.full((1, TQ), jnp.iinfo(jnp.int32).max, I32)

    thr, jstar = lax.cond(q0 + TQ <= n_sel, everything, search, 0)

    qat = qat_ref[0]
    zeros = jnp.zeros((HEAD_DIM, TQ), qat.dtype)
    qgroups = []
    for n in range(N_KV_HEADS):
        tiles = []
        for g in range(GROUP):
            h = n * GROUP + g
            x = qat[h * HEAD_DIM:(h + 1) * HEAD_DIM]
            tiles.append(jnp.concatenate([x, zeros] if n == 0 else [zeros, x], axis=0))
        qgroups.append(jnp.concatenate(tiles, axis=1))

    def logits_into(s_ref, c):
        kc = k_ref[pl.ds(pl.multiple_of(c * CKP, CKP), CKP), :]
        for n in range(N_KV_HEADS):
            s_ref[n] = jnp.dot(kc, qgroups[n], preferred_element_type=F32)

    def attend_chunk(c, s_cur_ref, s_next_ref):
        if s_next_ref is not None:
            logits_into(s_next_ref, jnp.minimum(c + 1, nk - 1))
        c0 = pl.multiple_of(c * CKP, CKP)
        idx = c0 + lax.broadcasted_iota(I32, (CKP, TQ), 0)
        valid = _valid_mask(score_ref[pl.ds(c0, CKP), :], idx, thr, jstar, qpos)
        vct = vt_ref[0, c]
        tiles = [jnp.maximum(back_tiles + j - (qb * (TQ // LANES) - c * tiles_per_chunk), 0)
                 for j in range(tiles_per_chunk)]
        for n in range(N_KV_HEADS):
            m_old = m_ref[n]
            parts = []
            for g in range(GROUP):
                bias = jnp.concatenate([strip_ref[n * GROUP + g, t] for t in tiles], axis=0)
                parts.append(jnp.where(valid, s_cur_ref[n, :, g * TQ:(g + 1) * TQ] + bias, NEG))
            sm = jnp.concatenate(parts, axis=1)
            m_new = jnp.maximum(m_old, jnp.max(sm, axis=0, keepdims=True))
            p = jnp.exp2((sm - m_new).astype(BF16))
            acc_ref[n] = jnp.exp2(m_old - m_new) * acc_ref[n] + jnp.dot(vct, p, preferred_element_type=F32)
            m_ref[n] = m_new

    m_ref[...] = jnp.full(m_ref.shape, NEG, F32)
    acc_ref[...] = jnp.zeros(acc_ref.shape, F32)
    logits_into(sa_ref, 0)

    def attend_pair(i, _):
        attend_chunk(2 * i, sa_ref, sb_ref)
        attend_chunk(2 * i + 1, sb_ref, sa_ref)
        return 0

    lax.fori_loop(0, nk // 2, attend_pair, 0)

    @pl.when(nk % 2 == 1)
    def _():
        attend_chunk(nk - 1, sa_ref, None)

    carries = [(None, acc_ref[n]) for n in range(N_KV_HEADS)]
    res = [acc[:LANES] / acc[LANES:LANES + 1] for (_, acc) in carries]
    row = lax.broadcasted_iota(I32, (LANES, TQ), 0)
    for j in range(GROUP):
        cols = slice(j * TQ, (j + 1) * TQ)
        tile_t = jnp.where(row < HEAD_DIM, res[0][:, cols], res[1][:, cols])
        out_ref[:, j * LANES:(j + 1) * LANES] = tile_t.T.astype(out_ref.dtype)


def _dsa_prompt(rel_bias, qat, qit, misct, kib, kb, vtb, B, T):
    assert T % CKP == 0 and T % TQ == 0
    nq = T // TQ
    n_sel = min(TOPK_MAX, T // 4)
    qcols = lambda w: pl.BlockSpec((1, w, TQ), lambda b, q: (b, 0, q))
    seq = lambda w: pl.BlockSpec((T, w), lambda b, q: (b, 0))
    return pl.pallas_call(
        functools.partial(_dsa_prompt_kernel, n_sel, max(1, (T - 1).bit_length())),
        grid=(B, nq),
        in_specs=[pl.BlockSpec(memory_space=pltpu.SMEM), qcols(ATT_WIDTH), qcols(N_IDX_HEADS * IDX_DIM),
                  qcols(LANES), seq(IDX_DIM), seq(LANES),
                  pl.BlockSpec((1, T // CKP, VT_ROWS, CKP), lambda b, q: (b, 0, 0, 0))],
        out_specs=pl.BlockSpec((TQ, ATT_WIDTH), lambda b, q: (b * nq + q, 0)),
        out_shape=jax.ShapeDtypeStruct((B * T, ATT_WIDTH), BF16),
        scratch_shapes=[pltpu.VMEM((T, TQ), F32),
                        pltpu.VMEM((32, T // CKP, SUBLANES, TQ), I32),
                        pltpu.VMEM((N_HEADS, STRIP_BACK + max(TQ, CKP) // LANES, LANES, TQ), F32),
                        pltpu.VMEM((N_KV_HEADS, CKP, GROUP * TQ), F32),
                        pltpu.VMEM((N_KV_HEADS, CKP, GROUP * TQ), F32),
                        pltpu.VMEM((N_KV_HEADS, 1, GROUP * TQ), F32),
                        pltpu.VMEM((N_KV_HEADS, VT_ROWS, GROUP * TQ), F32)],
        compiler_params=_cparams(2),
        name="dsa_prompt",
    )(rel_bias, qat, qit, misct, kib, kb, vtb)


def _page_pipeline(pt_ref, n_pages, caches, bufs, sems):
    def copies(bb, sl, j):
        pid = pt_ref[bb, j]
        cols = pl.ds(j * PAGE_SIZE, PAGE_SIZE)
        return [pltpu.make_async_copy(c.at[pid], buf.at[sl, :, cols], sems.at[sl, i])
                for i, (c, buf) in enumerate(zip(caches, bufs))]

    def start_all(bb, sl):
        for j in range(n_pages):
            for cp in copies(bb, sl, j):
                cp.start()

    def wait_all(bb, sl):
        for j in range(n_pages):
            for cp in copies(bb, sl, j):
                cp.wait()

    def step():
        b = pl.program_id(0)
        slot = b % 2

        @pl.when(b == 0)
        def _():
            start_all(0, 0)

        @pl.when(b + 1 < pl.num_programs(0))
        def _():
            start_all(b + 1, 1 - slot)

        wait_all(b, slot)
        return slot

    return step


def _pad_rows(x, rows):
    return jnp.concatenate([x, jnp.zeros((rows - x.shape[0], x.shape[1]), x.dtype)], axis=0)


def _sample_score_kernel(n_sel, idx_bits, n_pages, ts, ck, pt_ref, qi_ref, misc_ref, kin_ref,
                         ckit_hbm, score_ref, thr_ref, jst_ref, ki_buf, sems, planes_ref):
    b = pl.program_id(0)
    past = n_pages * PAGE_SIZE
    slot = _page_pipeline(pt_ref, n_pages, [ckit_hbm], [ki_buf], sems)()

    qi = qi_ref[...]
    qstack = jnp.concatenate([qi[:, h * IDX_DIM:(h + 1) * IDX_DIM] for h in range(N_IDX_HEADS)],
                             axis=0).astype(BF16)
    w = misc_ref[:, MISC_W:MISC_W + N_IDX_HEADS] * (N_IDX_HEADS ** -0.5 * IDX_DIM ** -0.5)
    d_past = jnp.dot(qstack, ki_buf[slot].astype(BF16), preferred_element_type=F32)
    d_own = lax.dot_general(qstack, _pad_rows(kin_ref[...], PAGE_SIZE).astype(BF16), NT_DIMS,
                            preferred_element_type=F32)
    d = jnp.maximum(jnp.concatenate([d_past, d_own], axis=1), 0.0)
    s = jnp.zeros((ts, past + PAGE_SIZE), F32)
    for h in range(N_IDX_HEADS):
        s = s + d[h * ts:(h + 1) * ts] * w[:, h:h + 1]
    idx = lax.broadcasted_iota(I32, s.shape, 1)
    qpos = past + lax.broadcasted_iota(I32, (ts, 1), 0)
    score_ref[pl.ds(pl.multiple_of(b * ts, ts), ts), :] = jnp.where(idx <= qpos, s, NEG)

    @pl.when(b == pl.num_programs(0) - 1)
    def _():
        reducers = _counter(score_ref, score_ref.shape[1] // ck, ck, 1)
        qshape = (score_ref.shape[0], 1)
        thr = _kth_largest_by_planes_lanes(score_ref, planes_ref, n_sel)
        thr, jstar = _select_topk(thr, True, reducers, qshape, n_sel, idx_bits)
        thr_ref[...] = jnp.broadcast_to(thr, thr_ref.shape)
        jst_ref[...] = jnp.broadcast_to(jstar, jst_ref.shape)


def _sample_attend_kernel(n_pages, ts, pt_ref, rb_ref, qa_ref, score_ref, thr_ref, jst_ref, kn_ref,
                          vn_ref, ckt_hbm, cvt_hbm, out_ref, k_buf, v_buf, sems, strip_ref):
    b = pl.program_id(0)
    past = n_pages * PAGE_SIZE
    rows = N_HEADS * ts

    @pl.when(b == 0)
    def _():
        _build_bias_strip(strip_ref, rb_ref, LANES * (strip_ref.shape[1] - 1), 1)

    slot = _page_pipeline(pt_ref, n_pages, [ckt_hbm, cvt_hbm], [k_buf, v_buf], sems)()

    q2 = jnp.concatenate(_group_queries(qa_ref[...], ts), axis=0)
    k_own = _pad_rows(kn_ref[...], PAGE_SIZE).astype(BF16)
    v_own = _pad_rows(vn_ref[...], PAGE_SIZE).astype(BF16)
    s_past = jnp.dot(q2, k_buf[slot].astype(BF16), preferred_element_type=F32)
    s_own = lax.dot_general(q2, k_own, NT_DIMS, preferred_element_type=F32)
    far = strip_ref[:, 0].reshape(rows, LANES)[:, 0:1]
    near = [strip_ref[:, t].reshape(rows, LANES) for t in (1, 2)]
    s = jnp.concatenate([s_past[:, :past - PAGE_SIZE] + far, s_past[:, past - PAGE_SIZE:] + near[0],
                         s_own + near[1]], axis=1)

    qpos = past + lax.broadcasted_iota(I32, (ts, 1), 0)
    scores = score_ref[...]
    valid = _valid_mask(scores, lax.broadcasted_iota(I32, scores.shape, 1), thr_ref[:, 0:1],
                        jst_ref[:, 0:1], qpos)
    s = jnp.where(valid[None], s.reshape(N_HEADS, ts, past + PAGE_SIZE), NEG).reshape(rows, -1)
    m = jnp.max(s, axis=1, keepdims=True)
    p = jnp.exp(s - m)
    l = jnp.sum(p, axis=1, keepdims=True)
    pb = p.astype(BF16)
    pv = lax.dot_general(pb[:, :past], v_buf[slot].astype(BF16), NT_DIMS, preferred_element_type=F32)
    pv = pv + jnp.dot(pb[:, past:], v_own, preferred_element_type=F32)
    half = GROUP * ts
    carries = [(None, l[n * half:(n + 1) * half], pv[n * half:(n + 1) * half]) for n in range(N_KV_HEADS)]
    _write_attn(out_ref, carries, ts)


def _dsa_sample(page_table, rel_bias, qa, qi, misc, ki_new, k_new, v_new, ckit, ckt, cvt, DB, ts):
    n_pages = page_table.shape[1]
    past = n_pages * PAGE_SIZE
    n_sel = min(TOPK_MAX, (past + ts) // 4)
    lpad = past + PAGE_SIZE
    idx_bits = max(1, (lpad - 1).bit_length())
    ck = LANES * math.gcd(lpad // LANES, 5)
    assert ts % SUBLANES == 0 and ts <= PAGE_SIZE and n_pages >= 1
    blk = lambda w: pl.BlockSpec((ts, w), lambda b, pt: (b, 0))
    whole = lambda w: pl.BlockSpec((DB * ts, w), lambda b, pt: (0, 0))
    hbm = pl.BlockSpec(memory_space=pl.ANY)
    keys, thr, jstar = pl.pallas_call(
        functools.partial(_sample_score_kernel, n_sel, idx_bits, n_pages, ts, ck),
        grid_spec=pltpu.PrefetchScalarGridSpec(
            num_scalar_prefetch=1,
            grid=(DB,),
            in_specs=[blk(N_IDX_HEADS * IDX_DIM), blk(LANES), blk(IDX_DIM), hbm],
            out_specs=[whole(lpad), whole(LANES), whole(LANES)],
            scratch_shapes=[pltpu.VMEM((2, IDX_DIM, past), F32), pltpu.SemaphoreType.DMA((2, 1)),
                            pltpu.VMEM((32, -(-lpad // (32 * LANES)), DB * ts, LANES), I32)]),
        out_shape=[jax.ShapeDtypeStruct((DB * ts, lpad), F32),
                   jax.ShapeDtypeStruct((DB * ts, LANES), F32),
                   jax.ShapeDtypeStruct((DB * ts, LANES), I32)],
        compiler_params=_cparams(1),
        name="sample_score",
    )(page_table, qi, misc, ki_new, ckit)
    return pl.pallas_call(
        functools.partial(_sample_attend_kernel, n_pages, ts),
        grid_spec=pltpu.PrefetchScalarGridSpec(
            num_scalar_prefetch=1,
            grid=(DB,),
            in_specs=[pl.BlockSpec(memory_space=pltpu.SMEM), blk(ATT_WIDTH), blk(lpad), blk(LANES),
                      blk(LANES), blk(LANES), blk(LANES), hbm, hbm],
            out_specs=blk(ATT_WIDTH),
            scratch_shapes=[pltpu.VMEM((2, LANES, past), F32), pltpu.VMEM((2, LANES, past), F32),
                            pltpu.SemaphoreType.DMA((2, 2)),
                            pltpu.VMEM((N_HEADS, 3, ts, LANES), F32)]),
        out_shape=jax.ShapeDtypeStruct((DB * ts, ATT_WIDTH), F32),
        compiler_params=_cparams(1),
        name="sample_attend",
    )(page_table, rel_bias, qa, keys, thr, jstar, k_new, v_new, ckt, cvt)


def _pack_layer_weights(w_in, b_i, b_f, w_out, w_up, w_down):
    D = w_in.shape[0]
    sizes = (ATT_WIDTH, N_KV_HEADS * HEAD_DIM, N_KV_HEADS * HEAD_DIM, N_IDX_HEADS * IDX_DIM, IDX_DIM,
             N_IDX_HEADS, M_WIDTH, M_WIDTH, M_WIDTH, M_WIDTH, M_HEADS, M_HEADS)
    assert w_in.shape[1] == sum(sizes)
    pts = np.cumsum((0,) + sizes)
    w_t = w_in.T.astype(BF16)
    qa, k, v, qi, ki, wi, qm, km, vm, om, im, fm = [w_t[pts[i]:pts[i + 1]] for i in range(len(sizes))]
    perm = np.asarray(HEAD_PERM)
    qa_perm = qa.reshape(N_HEADS, HEAD_DIM, D)[perm].reshape(ATT_WIDTH, D)
    misc = jnp.concatenate([wi, im, fm, jnp.zeros((LANES - N_IDX_HEADS - 2 * M_HEADS, D), BF16)], axis=0)
    wp = jnp.concatenate([qa_perm, v, qi, k, ki, ki, misc, qm, km, vm, om], axis=0).T
    assert wp.shape[1] == N_PACK
    gate_bias = jnp.broadcast_to(jnp.concatenate([b_i, b_f]).astype(F32)[:, None], (2 * M_HEADS, LANES))
    woa = w_out[:ATT_WIDTH].reshape(N_HEADS, HEAD_DIM, -1)[perm].reshape(ATT_WIDTH, -1).astype(BF16)
    woh = w_out[ATT_WIDTH:].astype(BF16)
    wt = jnp.concatenate([k, v, ki, jnp.zeros((LANES - IDX_DIM, D), BF16), qa, qi, misc], axis=0)
    assert wt.shape[0] == N_TPACK
    return wp, wt, gate_bias, woa, woh, w_up.astype(BF16), w_down.astype(BF16)


def _layer(x, packed, g1, g2, mnorm, rel_bias, gf, final_norm, past):
    wp, wt, gate_bias, woa, woh, wup, wdn = packed
    B, T, D = x.shape
    x2 = x.reshape(B * T, D)
    tm = math.gcd(T if past is None else B * T, 512)
    kv_w = N_KV_HEADS * HEAD_DIM
    if past is None:
        (misc, qm, km, vm, om, kb, kib, kt, vt, kit, vtb, qat, qit, misct) = _inproj(
            x2, g1.reshape(1, D), wp, wt, B, T, tm)
        attn = _dsa_prompt(rel_bias, qat, qit, misct, kib, kb, vtb, B, T)
        state = None
        k_new = kt.reshape(B, N_KV_HEADS, HEAD_DIM, T).transpose(0, 3, 1, 2)
        v_new = vt.reshape(B, N_KV_HEADS, HEAD_DIM, T).transpose(0, 3, 1, 2)
        ki_new = kit.transpose(0, 2, 1)
    else:
        (misc, qm, km, vm, om, qa, qi, k, v, ki) = _inproj(x2, g1.reshape(1, D), wp, None, B, T, tm)
        page_table, cache_k, cache_v, cache_kidx, c0, n0, m0 = past
        n_pool = cache_k.shape[0]
        ckt = cache_k.transpose(0, 2, 3, 1).reshape(n_pool, kv_w, PAGE_SIZE)
        cvt = cache_v.transpose(0, 2, 3, 1).reshape(n_pool, kv_w, PAGE_SIZE)
        ckit = cache_kidx.transpose(0, 2, 1)
        attn = _dsa_sample(page_table, rel_bias, qa, qi, misc, ki, k, v, ckit, ckt, cvt, B, T)
        state = (c0, n0, jnp.broadcast_to(m0[..., None], m0.shape + (LANES,)))
        k_new = k.reshape(B, T, N_KV_HEADS, HEAD_DIM)
        v_new = v.reshape(B, T, N_KV_HEADS, HEAD_DIM)
        ki_new = ki.reshape(B, T, IDX_DIM)
    h, c_new, n_new, m_new = _mlstm(qm, km, vm, om, misc, gate_bias, mnorm.reshape(1, M_WIDTH), state,
                                    B, T, BF16 if past is None else F32)
    y = _post(x2, attn, h, woa, woh, g2.reshape(1, D), wup, wdn, gf.reshape(1, D), final_norm, tm)
    return (y.reshape(B, T, D), k_new, v_new, ki_new, c_new, n_new, m_new[..., 0])


def kernel(x_prompt, x_sample, cache_k, cache_v, cache_kidx, page_table, state_C, state_n, state_m,
           w_in, b_igate, b_fgate, mlstm_norm, rel_bias, w_out, norm1, norm2, w_up, w_down, norm_f):
    depth = w_in.shape[0]
    xp, xs = x_prompt, x_sample
    outs_p, outs_s = [], []
    for l in range(depth):
        packed = _pack_layer_weights(w_in[l], b_igate[l], b_fgate[l], w_out[l], w_up[l], w_down[l])
        last = l == depth - 1
        common = (packed, norm1[l], norm2[l], mlstm_norm[l], rel_bias, norm_f, last)
        rp = _layer(xp, *common, None)
        rs = _layer(xs, *common, (page_table, cache_k[l], cache_v[l], cache_kidx[l],
                                  state_C[l], state_n[l], state_m[l]))
        xp, xs = rp[0], rs[0]
        outs_p.append(rp[1:])
        outs_s.append(rs[1:])
    stack = lambda outs, i: jnp.stack([o[i] for o in outs])
    return ((xp, xs) + tuple(stack(outs_p, i) for i in range(6))
            + tuple(stack(outs_s, i) for i in range(6)))
```

```python
import functools
import math

import numpy as np
import jax
import jax.numpy as jnp
from jax import lax
from jax.experimental import pallas as pl
from jax.experimental.pallas import tpu as pltpu

F32 = jnp.float32
BF16 = jnp.bfloat16
I32 = jnp.int32

N_HEADS = 8
HEAD_DIM = 64
N_KV_HEADS = 2
GROUP = N_HEADS // N_KV_HEADS
N_IDX_HEADS = 8
IDX_DIM = 64
TOPK_MAX = 256
N_BUCKETS = 32
MAX_DISTANCE = 128
M_HEADS = 4
M_HEAD_DIM = 128
PAGE_SIZE = 128
EPS = 1e-6
NEG = -1e30
LOG2E = math.log2(math.e)
ATT_WIDTH = N_HEADS * HEAD_DIM
M_WIDTH = M_HEADS * M_HEAD_DIM

LANES = 128
SUBLANES = 8
VMEM_LIMIT = 56 * 1024 * 1024

C_QA = 0
C_V = C_QA + ATT_WIDTH
C_QI = C_V + LANES
C_K = C_QI + N_IDX_HEADS * IDX_DIM
C_KI2 = C_K + LANES
C_MISC = C_KI2 + LANES
C_QM = C_MISC + LANES
C_KM = C_QM + M_WIDTH
C_VM = C_KM + M_WIDTH
C_OM = C_VM + M_WIDTH
N_PACK = C_OM + M_WIDTH
MISC_W = 0
MISC_I = 8
MISC_F = 12

HEAD_PERM = (0, 4, 1, 5, 2, 6, 3, 7)

TQ = 256
STRIP_BACK = 2
assert LANES * (STRIP_BACK - 1) >= MAX_DISTANCE
CKP = 256
VT_ROWS = LANES + 16
ML = 256

NT_DIMS = (((1,), (1,)), ((), ()))
TN_DIMS = (((0,), (0,)), ((), ()))


def _bucket_bounds():
    max_exact = N_BUCKETS // 2
    scale = (N_BUCKETS - max_exact) / math.log(MAX_DISTANCE / max_exact)

    def bucket(n, dt):
        if n < max_exact:
            return n
        val = np.log(np.asarray(max(n, 1), dt) / dt(max_exact)) * dt(scale)
        return min(max_exact + int(val), N_BUCKETS - 1)

    table = [bucket(n, np.float32) for n in range(MAX_DISTANCE + 2)]
    assert table == [bucket(n, np.float64) for n in range(MAX_DISTANCE + 2)]
    assert table[MAX_DISTANCE] == N_BUCKETS - 1
    return [next(d for d, b in enumerate(table) if b >= k) for k in range(N_BUCKETS)]


BUCKET_BOUNDS = _bucket_bounds()


def _cparams(n_axes):
    return pltpu.CompilerParams(dimension_semantics=("arbitrary",) * n_axes,
                                vmem_limit_bytes=VMEM_LIMIT)


def _const_spec(shape):
    nd = len(shape)
    return pl.BlockSpec(shape, lambda *_: (0,) * nd, pipeline_mode=pl.Buffered(1))


def _rms(x, g):
    return x * lax.rsqrt(jnp.mean(x * x, axis=-1, keepdims=True) + EPS) * g


def _inproj_mlstm(mm, misc_ref, qm_ref, km_ref, vm_ref, om_ref):
    misc_ref[...] = mm(C_MISC, LANES)
    qm_ref[...] = mm(C_QM, M_WIDTH).astype(qm_ref.dtype)
    km_ref[...] = (mm(C_KM, M_WIDTH) * (M_HEAD_DIM ** -0.5)).astype(km_ref.dtype)
    vm_ref[...] = mm(C_VM, M_WIDTH).astype(vm_ref.dtype)
    om_ref[...] = mm(C_OM, M_WIDTH)


def _inproj_rows_kernel(x_ref, g_ref, w_ref, misc_ref, qm_ref, km_ref, vm_ref, om_ref,
                        qa_ref, qi_ref, k_ref, v_ref, ki_ref):
    ub = _rms(x_ref[...], g_ref[...]).astype(BF16)
    z = jnp.dot(ub, w_ref[...], preferred_element_type=F32)
    mm = lambda c0, n: z[:, c0:c0 + n]
    _inproj_mlstm(mm, misc_ref, qm_ref, km_ref, vm_ref, om_ref)
    qa_ref[...] = mm(C_QA, ATT_WIDTH) * (HEAD_DIM ** -0.5)
    qi_ref[...] = mm(C_QI, N_IDX_HEADS * IDX_DIM)
    k_ref[...] = mm(C_K, LANES)
    v_ref[...] = mm(C_V, LANES)
    ki_ref[...] = mm(C_KI2, LANES)[:, :IDX_DIM]


R_K = 0
R_V = R_K + LANES
R_KI = R_V + LANES
R_QA = R_KI + LANES
R_QI = R_QA + ATT_WIDTH
R_MISC = R_QI + N_IDX_HEADS * IDX_DIM
N_TPACK = R_MISC + LANES


def _inproj_cols_kernel(x_ref, g_ref, w_ref, wt_ref, misc_ref, qm_ref, km_ref, vm_ref, om_ref,
                        kb_ref, kib_ref, kt_ref, vt_ref, kit_ref, vtb_ref, qat_ref, qit_ref,
                        misct_ref):
    ub = _rms(x_ref[...], g_ref[...]).astype(BF16)
    z = jnp.dot(ub, w_ref[:, C_K:], preferred_element_type=F32)
    mm = lambda c0, n: z[:, c0 - C_K:c0 - C_K + n]
    _inproj_mlstm(mm, misc_ref, qm_ref, km_ref, vm_ref, om_ref)
    kb_ref[...] = mm(C_K, LANES).astype(BF16)
    kib_ref[...] = mm(C_KI2, LANES)[:, :IDX_DIM].astype(BF16)

    zt = lax.dot_general(wt_ref[...], ub, NT_DIMS, preferred_element_type=F32)
    mt = lambda r0, n: zt[r0:r0 + n]

    kt_ref[0] = mt(R_K, LANES)
    vt = mt(R_V, LANES)
    vt_ref[0] = vt
    ones = jnp.ones((VT_ROWS - LANES, CKP), BF16)
    for j in range(vtb_ref.shape[1]):
        vtb_ref[0, j] = jnp.concatenate([vt[:, j * CKP:(j + 1) * CKP].astype(BF16), ones], axis=0)
    kit_ref[0] = mt(R_KI, IDX_DIM)
    qat_ref[0] = (mt(R_QA, ATT_WIDTH) * (HEAD_DIM ** -0.5 * LOG2E)).astype(BF16)
    qit_ref[0] = mt(R_QI, N_IDX_HEADS * IDX_DIM).astype(BF16)
    misct_ref[0] = mt(R_MISC, LANES)


def _inproj(x2, g1, wp, wt, B, T, tm):
    R, D = x2.shape
    assert R == B * T and R % tm == 0
    mdt = F32 if wt is None else BF16
    row = lambda i: (i, 0)
    outs = [(LANES, F32), (M_WIDTH, mdt), (M_WIDTH, mdt), (M_WIDTH, mdt), (M_WIDTH, F32)]
    in_specs = [pl.BlockSpec((tm, D), row), _const_spec((1, D)), _const_spec((D, N_PACK))]
    if wt is None:
        kern, args = _inproj_rows_kernel, (x2, g1, wp)
        outs += [(ATT_WIDTH, F32), (N_IDX_HEADS * IDX_DIM, F32), (LANES, F32), (LANES, F32), (IDX_DIM, F32)]
    else:
        kern, args = _inproj_cols_kernel, (x2, g1, wp, wt)
        in_specs.append(_const_spec(wt.shape))
        outs += [(LANES, BF16), (IDX_DIM, BF16)]
    out_specs = [pl.BlockSpec((tm, w), row) for w, _ in outs]
    out_shape = [jax.ShapeDtypeStruct((R, w), dt) for w, dt in outs]
    if wt is not None:
        assert T % tm == 0 and tm % CKP == 0
        tpb, cpt = T // tm, tm // CKP
        cols = lambda i: (i // tpb, 0, i % tpb)
        for w, dt in ((LANES, F32), (LANES, F32), (IDX_DIM, F32)):
            out_specs.append(pl.BlockSpec((1, w, tm), cols))
            out_shape.append(jax.ShapeDtypeStruct((B, w, T), dt))
        out_specs.append(pl.BlockSpec((1, cpt, VT_ROWS, CKP), lambda i: (i // tpb, i % tpb, 0, 0)))
        out_shape.append(jax.ShapeDtypeStruct((B, T // CKP, VT_ROWS, CKP), BF16))
        for w, dt in ((ATT_WIDTH, BF16), (N_IDX_HEADS * IDX_DIM, BF16), (LANES, F32)):
            out_specs.append(pl.BlockSpec((1, w, tm), cols))
            out_shape.append(jax.ShapeDtypeStruct((B, w, T), dt))
    return pl.pallas_call(
        kern,
        grid=(R // tm,),
        in_specs=in_specs,
        out_specs=out_specs,
        out_shape=out_shape,
        compiler_params=_cparams(1),
        name="inproj",
    )(*args)


def _post_kernel(ff_chunk, final_norm, x_ref, a_ref, h_ref, woa_ref, woh_ref, g2_ref, wup_ref,
                 wdn_ref, gf_ref, y_ref):
    mix = jnp.dot(a_ref[...].astype(BF16), woa_ref[...], preferred_element_type=F32)
    mix = mix + jnp.dot(h_ref[...].astype(BF16), woh_ref[...], preferred_element_type=F32)
    hres = x_ref[...] + mix
    f = _rms(hres, g2_ref[...]).astype(BF16)
    acc = hres
    for c0 in range(0, wup_ref.shape[1], ff_chunk):
        up = jnp.dot(f, wup_ref[:, c0:c0 + ff_chunk], preferred_element_type=F32)
        r = jnp.maximum(up, 0.0)
        acc = acc + jnp.dot((r * r).astype(BF16), wdn_ref[c0:c0 + ff_chunk, :],
                            preferred_element_type=F32)
    y_ref[...] = _rms(acc, gf_ref[...]) if final_norm else acc


def _post(x2, attn, h, woa, woh, g2, wup, wdn, gf, final_norm, tm):
    R, D = x2.shape
    dff = wup.shape[1]
    assert R % tm == 0
    row = lambda i: (i, 0)
    return pl.pallas_call(
        functools.partial(_post_kernel, min(dff, 1024), final_norm),
        grid=(R // tm,),
        in_specs=[pl.BlockSpec((tm, D), row), pl.BlockSpec((tm, ATT_WIDTH), row),
                  pl.BlockSpec((tm, M_WIDTH), row), _const_spec(woa.shape), _const_spec(woh.shape),
                  _const_spec((1, D)), _const_spec(wup.shape), _const_spec(wdn.shape),
                  _const_spec((1, D))],
        out_specs=pl.BlockSpec((tm, D), row),
        out_shape=jax.ShapeDtypeStruct((R, D), F32),
        compiler_params=_cparams(1),
        name="post",
    )(x2, attn, h, woa, woh, g2, wup, wdn, gf)


def _log_sigmoid(x):
    return -(jnp.maximum(-x, 0.0) + jnp.log1p(jnp.exp(-jnp.abs(x))))


def _mlstm_kernel(nvalid, has_state, *refs):
    if has_state:
        (q_ref, k_ref, v_ref, o_ref, misc_ref, gb_ref, mn_ref, c0_ref, n0_ref, m0_ref,
         h_ref, c_ref, n_ref, m_ref, z_scr, rows_scr) = refs
    else:
        (q_ref, k_ref, v_ref, o_ref, misc_ref, gb_ref, mn_ref,
         h_ref, c_ref, n_ref, m_ref, z_scr, rows_scr) = refs
    nb, nc, L = z_scr.shape[0], z_scr.shape[1], z_scr.shape[2]
    c = pl.program_id(1)

    def padded(x, dt):
        x = x.astype(dt)
        if nvalid == L:
            return x
        return jnp.concatenate([x, jnp.zeros((L - nvalid, x.shape[1]), dt)], axis=0)

    r2 = lax.broadcasted_iota(I32, (L, L), 0)
    c2 = lax.broadcasted_iota(I32, (L, L), 1)
    tril = r2 >= c2

    @pl.when(c == 0)
    def _():
        if has_state:
            c_ref[...] = c0_ref[...]
            n_ref[...] = n0_ref[...]
            m_ref[...] = m0_ref[...]
        else:
            c_ref[...] = jnp.zeros_like(c_ref)
            n_ref[...] = jnp.zeros_like(n_ref)
            m_ref[...] = jnp.zeros_like(m_ref)
        assert MISC_F == MISC_I + M_HEADS and MISC_I % SUBLANES == 0
        row8 = lax.broadcasted_iota(I32, (SUBLANES, L), 0)
        tok8 = lax.broadcasted_iota(I32, (SUBLANES, L), 1)
        bias8 = jnp.concatenate([gb_ref[...]] * (L // LANES), axis=1) if L >= LANES else gb_ref[:, :L]
        triu = (r2 <= c2).astype(F32)
        for s in range(nb):
            for cc in range(nc):
                misc_t = padded(misc_ref[s, cc * nvalid:(cc + 1) * nvalid, :], F32).T
                gx = misc_t[MISC_I:MISC_I + SUBLANES] + bias8
                gates = jnp.where(row8 >= M_HEADS, _log_sigmoid(gx), gx)
                if nvalid != L:
                    gates = jnp.where(tok8 < nvalid, gates, jnp.where(row8 >= M_HEADS, 0.0, NEG))
                cum = jnp.dot(gates, triu, precision=lax.Precision.HIGHEST,
                              preferred_element_type=F32)
                rows = jnp.where(row8 >= M_HEADS, cum, gates)
                rows_scr[s, cc] = rows
                z_scr[s, cc] = jnp.concatenate([rows, jnp.zeros((LANES - SUBLANES, L), F32)], axis=0).T

    loaded = []
    for s in range(nb):
        loaded.append((z_scr[s, c], rows_scr[s, c], padded(q_ref[s], BF16), padded(k_ref[s], BF16),
                       padded(v_ref[s], BF16), padded(o_ref[s], F32), m_ref[s], c_ref[s], n_ref[s]))
    chains = [(s, hd) for s in range(nb) for hd in range(M_HEADS)]

    def operands(s, hd):
        z, rows, qb, kb, vb, ob, m_all, c_all, n_all = loaded[s]
        sl = slice(hd * M_HEAD_DIM, (hd + 1) * M_HEAD_DIM)
        return dict(
            sl=sl, q=qb[:, sl], k=kb[:, sl], v=vb[:, sl], o=ob[:, sl],
            icol=z[:, hd:hd + 1], bcol=z[:, M_HEADS + hd:M_HEADS + hd + 1],
            irow=rows[hd:hd + 1, :], brow=rows[M_HEADS + hd:M_HEADS + hd + 1, :],
            m_prev=m_all[hd:hd + 1, 0:1], s_prev=c_all[hd], n_prev=n_all[hd:hd + 1, :])

    def lane_sum(x):
        ones = jnp.ones((x.shape[1], LANES), BF16)
        head = x.astype(BF16)
        tail = (x - head.astype(F32)).astype(BF16)
        return (jnp.dot(head, ones, preferred_element_type=F32)
                + jnp.dot(tail, ones, preferred_element_type=F32))

    st = [operands(s, hd) for s, hd in chains]
    for x in st:
        x["qk"] = lax.dot_general(x["q"], x["k"], NT_DIMS, preferred_element_type=F32)
        x["qs"] = jnp.dot(x["q"], x["s_prev"].astype(BF16), preferred_element_type=F32)
        x["qn"] = lane_sum(x["q"].astype(F32) * x["n_prev"])
    for x in st:
        g = x["bcol"] + x["m_prev"]
        dm = jnp.where(tril, x["bcol"] - x["brow"] + x["irow"], NEG)
        x["mt"] = jnp.maximum(g, jnp.max(dm, axis=1, keepdims=True))
        x["gw"] = jnp.exp(g - x["mt"])
        x["qk"] = x["qk"] * jnp.exp(dm - x["mt"])
        b_last = x["bcol"][L - 1:L, :]
        g_last = b_last + x["m_prev"]
        a = b_last - x["bcol"] + x["icol"]
        x["m_new"] = jnp.maximum(g_last, jnp.max(a, axis=0, keepdims=True))
        x["sw"] = jnp.exp(g_last - x["m_new"])
        x["ak"] = jnp.exp(a - x["m_new"]) * x["k"].astype(F32)
    for x in st:
        x["pv"] = jnp.dot(x["qk"].astype(BF16), x["v"], preferred_element_type=F32)
        x["qksum"] = lane_sum(x["qk"])
        x["kv"] = lax.dot_general(x["ak"].astype(BF16), x["v"], TN_DIMS, preferred_element_type=F32)
    stores = []
    for (s, hd), x in zip(chains, st):
        num = x["gw"] * x["qs"] + x["pv"]
        den = x["gw"] * x["qn"] + x["qksum"]
        hh = num / jnp.maximum(jnp.abs(den), jnp.exp(-x["mt"]))
        hh = hh * lax.rsqrt(lane_sum(hh * hh) * (1.0 / M_HEAD_DIM) + EPS)
        hh = hh * mn_ref[:, x["sl"]] * jax.nn.sigmoid(x["o"])
        c_new = x["sw"] * x["s_prev"] + x["kv"]
        n_new = x["sw"] * x["n_prev"] + jnp.sum(x["ak"], axis=0, keepdims=True)
        stores.append((s, hd, x["sl"], hh[:nvalid].astype(h_ref.dtype), c_new, n_new,
                       jnp.broadcast_to(x["m_new"], (1, LANES))))
    for s, hd, sl, h_new, c_new, n_new, m_new in stores:
        h_ref[s, :, sl] = h_new
        c_ref[s, hd] = c_new
        n_ref[s, hd:hd + 1, :] = n_new
        m_ref[s, hd:hd + 1, :] = m_new


def _mlstm(qm, km, vm, om, misc, gate_bias, mnorm, state, B, T, hdt):
    L = next((c for c in (ML, LANES) if T % c == 0), T if T % SUBLANES == 0 else LANES)
    nvalid = L if T % L == 0 else T
    assert nvalid <= L and T % nvalid == 0 and nvalid % SUBLANES == 0
    nc = T // nvalid
    nb = next(n for n in ((2, 1) if L >= LANES and nvalid == L else (8, 4, 2, 1)) if B % n == 0)
    seq3 = lambda a: a.reshape(B, T, a.shape[-1])
    blk = lambda w: pl.BlockSpec((nb, nvalid, w), lambda b, c: (b, c, 0))
    st_specs = [pl.BlockSpec((nb, M_HEADS, M_HEAD_DIM, M_HEAD_DIM), lambda b, c: (b, 0, 0, 0)),
                pl.BlockSpec((nb, M_HEADS, M_HEAD_DIM), lambda b, c: (b, 0, 0)),
                pl.BlockSpec((nb, M_HEADS, LANES), lambda b, c: (b, 0, 0))]
    in_specs = [blk(M_WIDTH), blk(M_WIDTH), blk(M_WIDTH), blk(M_WIDTH),
                pl.BlockSpec((nb, T, LANES), lambda b, c: (b, 0, 0)),
                pl.BlockSpec((SUBLANES, LANES), lambda b, c: (0, 0)),
                pl.BlockSpec((1, M_WIDTH), lambda b, c: (0, 0))]
    args = [seq3(qm), seq3(km), seq3(vm), seq3(om), seq3(misc), gate_bias, mnorm]
    if state is not None:
        in_specs += st_specs
        args += list(state)
    h, c_new, n_new, m_new = pl.pallas_call(
        functools.partial(_mlstm_kernel, nvalid, state is not None),
        grid=(B // nb, nc),
        in_specs=in_specs,
        out_specs=[blk(M_WIDTH)] + st_specs,
        out_shape=[jax.ShapeDtypeStruct((B, T, M_WIDTH), hdt),
                   jax.ShapeDtypeStruct((B, M_HEADS, M_HEAD_DIM, M_HEAD_DIM), F32),
                   jax.ShapeDtypeStruct((B, M_HEADS, M_HEAD_DIM), F32),
                   jax.ShapeDtypeStruct((B, M_HEADS, LANES), F32)],
        scratch_shapes=[pltpu.VMEM((nb, nc, L, LANES), F32), pltpu.VMEM((nb, nc, SUBLANES, L), F32)],
        compiler_params=_cparams(2),
        name="mlstm",
    )(*args)
    return h.reshape(B * T, M_WIDTH), c_new, n_new, m_new


def _sortable_key(score):
    bits = lax.bitcast_convert_type(score, I32)
    return bits ^ (lax.shift_right_arithmetic(bits, 31) & 0x7FFFFFFF)


def _key_to_score(key):
    return lax.bitcast_convert_type(key ^ (lax.shift_right_arithmetic(key, 31) & 0x7FFFFFFF), F32)


def _build_bias_strip(strip_ref, rb_ref, off, key_axis, log2_relative=False):
    ntiles, tile = strip_ref.shape[1], strip_ref.shape[2:]
    i = lax.broadcasted_iota(I32, tile, 1 - key_axis)
    x = lax.broadcasted_iota(I32, tile, key_axis)

    def entry(b, h):
        if log2_relative:
            return (rb_ref[b, h] - rb_ref[N_BUCKETS - 1, h]) * LOG2E
        return rb_ref[b, h]

    for t in range(ntiles):
        dist = i + (off - LANES * t) - x
        for h in range(N_HEADS):
            val = jnp.full(tile, entry(0, h), F32)
            for b in range(1, N_BUCKETS):
                val = jnp.where(dist >= BUCKET_BOUNDS[b], entry(b, h), val)
            strip_ref[h, t] = val


def _counter(score_ref, nk, ck, key_axis):
    def reduce(fn, fold, fold_all, init):
        def body(c, acc):
            c0 = pl.multiple_of(c * ck, ck)
            sc = score_ref[:, pl.ds(c0, ck)] if key_axis == 1 else score_ref[pl.ds(c0, ck), :]
            val = fn(sc, c0 + lax.broadcasted_iota(I32, sc.shape, key_axis))
            if key_axis == 1:
                for j in range(ck // LANES):
                    acc = fold(acc, val[:, j * LANES:(j + 1) * LANES])
                return acc
            return fold(acc, fold_all(val.reshape(ck // SUBLANES, SUBLANES, val.shape[1]), axis=0))

        nq = score_ref.shape[1 - key_axis]
        acc0 = jnp.full((nq, LANES) if key_axis == 1 else (SUBLANES, nq), init, I32)
        return fold_all(lax.fori_loop(0, nk, body, acc0), axis=key_axis, keepdims=True)

    def count(pred):
        return reduce(lambda sc, idx: jnp.where(pred(sc, idx), 1, 0), jnp.add, jnp.sum, 0)

    def lowest(fn):
        return reduce(fn, jnp.minimum, jnp.min, jnp.iinfo(jnp.int32).max)

    return count, lowest


IMIN = jnp.iinfo(jnp.int32).min


def _kth_largest_by_count(count, qshape, n_sel):
    def bit_step(i, key):
        cand = key + lax.shift_left(jnp.int32(1), 31 - i)
        cand_score = _key_to_score(cand)
        return jnp.where(count(lambda sc, idx: sc >= cand_score) >= n_sel, cand, key)

    return _key_to_score(lax.fori_loop(0, 32, bit_step, jnp.full(qshape, IMIN, I32)))


def _bit_planes(words):
    a = list(words)
    j, m = 16, 0x0000FFFF
    while j:
        k = 0
        while k < 32:
            t = (a[k] ^ lax.shift_right_logical(a[k + j], j)) & m
            a[k] = a[k] ^ t
            a[k + j] = a[k + j] ^ lax.shift_left(t, j)
            k = (k + j + 1) & ~j
        j >>= 1
        m = (m ^ (m << j)) & 0xFFFFFFFF
        m = m - (1 << 32) if m >= (1 << 31) else m
    return a[::-1]


def _kth_largest_by_planes(score_ref, planes_ref, nk, ck, n_sel):
    nc = planes_ref.shape[1]
    nq = score_ref.shape[1]
    assert ck == 32 * SUBLANES

    def pack_chunk(c, _):
        c0 = pl.multiple_of(c * ck, ck)
        u = _sortable_key(score_ref[pl.ds(c0, ck), :]) ^ IMIN
        u = u.reshape(32, SUBLANES, nq)
        for b, plane in enumerate(_bit_planes([u[v] for v in range(32)])):
            planes_ref[b, c] = plane
        return 0

    lax.fori_loop(0, nk, pack_chunk, 0)
    cand0 = tuple(jnp.where(c < nk, jnp.full((SUBLANES, nq), -1, I32), 0) for c in range(nc))
    return _plane_search(planes_ref, cand0, n_sel, 0)


def _kth_largest_by_planes_lanes(score_ref, planes_ref, n_sel):
    nq, ntiles = score_ref.shape[0], score_ref.shape[1] // LANES
    cand0 = []
    for g in range(planes_ref.shape[1]):
        real = min(32, ntiles - 32 * g)
        words = [_sortable_key(score_ref[:, (32 * g + v) * LANES:(32 * g + v + 1) * LANES]) ^ IMIN
                 if v < real else jnp.zeros((nq, LANES), I32) for v in range(32)]
        for b, plane in enumerate(_bit_planes(words)):
            planes_ref[b, g] = plane
        cand0.append(jnp.full((nq, LANES), -(1 << (32 - real)), I32))
    return _plane_search(planes_ref, tuple(cand0), n_sel, 1)


def _plane_search(planes_ref, cand0, n_sel, key_axis):
    def bit_step(i, carry):
        cand, n_above, thr_u = carry
        b = 31 - i
        ones = [m & planes_ref[b, c] for c, m in enumerate(cand)]
        pop = lax.population_count(ones[0])
        for o in ones[1:]:
            pop = pop + lax.population_count(o)
        tot = jnp.sum(pop, axis=key_axis, keepdims=True)
        take = n_above + tot >= n_sel
        cand = tuple(jnp.where(take, o, m ^ o) for o, m in zip(ones, cand))
        n_above = jnp.where(take, n_above, n_above + tot)
        thr_u = thr_u | jnp.where(take, lax.shift_left(jnp.int32(1), b), 0)
        return cand, n_above, thr_u

    zero = jnp.zeros_like(jnp.sum(cand0[0], axis=key_axis, keepdims=True))
    _, _, thr_u = lax.fori_loop(0, 32, bit_step, (cand0, zero, zero))
    return _key_to_score(thr_u ^ IMIN)


def _select_topk(thr, check, reducers, qshape, n_sel, idx_bits):
    count, lowest = reducers
    imax = jnp.iinfo(jnp.int32).max

    def with_counts(t):
        return t, count(lambda sc, idx: sc > t), count(lambda sc, idx: sc >= t)

    thr, n_gt, n_ge = with_counts(thr)
    if check:
        good = jnp.min(jnp.where(n_gt < n_sel, jnp.where(n_ge >= n_sel, 1, 0), 0)) > 0
        thr, n_gt, n_ge = lax.cond(
            good, lambda _: (thr, n_gt, n_ge),
            lambda _: with_counts(_kth_largest_by_count(count, qshape, n_sel)), 0)
    split = n_ge > n_sel
    need = jnp.where(split, n_sel - n_gt, 0)
    most = jnp.max(need)

    def by_extraction(_):
        def step(k, last):
            nxt = lowest(lambda sc, idx: jnp.where(sc == thr, jnp.where(idx > last, idx, imax), imax))
            return jnp.where(k < need, nxt, last)

        return lax.fori_loop(0, most, step, jnp.full(qshape, -1, I32))

    def by_bisection(_):
        def idx_step(i, lo):
            cand = lo + lax.shift_left(jnp.int32(1), idx_bits - 1 - i)
            cnt = count(lambda sc, idx: jnp.where(sc == thr, idx, imax) < cand)
            return jnp.where(cnt < need, cand, lo)

        return lax.fori_loop(0, idx_bits, idx_step, jnp.zeros(qshape, I32))

    jstar = lax.cond(most <= idx_bits, by_extraction, by_bisection, 0)
    return thr, jnp.where(split, jstar, imax)


def _valid_mask(scores, idx, thr, jstar, qpos):
    sel = jnp.where(scores > thr, 1, jnp.where(scores == thr, jnp.where(idx <= jstar, 1, 0), 0))
    return jnp.where(idx <= qpos, sel, 0) > 0


def _group_queries(qa, tq):
    lane = lax.broadcasted_iota(I32, (tq, LANES), 1)
    out = []
    for n in range(N_KV_HEADS):
        keep = (lane < HEAD_DIM) if n == 0 else (lane >= HEAD_DIM)
        tiles = [jnp.where(keep, qa[:, j * LANES:(j + 1) * LANES], jnp.zeros((), qa.dtype))
                 for j in range(GROUP)]
        out.append(jnp.concatenate(tiles, axis=0).astype(BF16))
    return out


def _write_attn(out_ref, carries, tq):
    lane = lax.broadcasted_iota(I32, (tq, LANES), 1)
    res = [acc / l for (_, l, acc) in carries]
    for j in range(GROUP):
        tile = jnp.where(lane < HEAD_DIM, res[0][j * tq:(j + 1) * tq], res[1][j * tq:(j + 1) * tq])
        out_ref[:, j * LANES:(j + 1) * LANES] = tile.astype(out_ref.dtype)


def _dsa_prompt_kernel(n_sel, idx_bits, rb_ref, qat_ref, qit_ref, misct_ref, ki_ref, k_ref, vt_ref,
                       out_ref, score_ref, planes_ref, strip_ref, sa_ref, sb_ref, m_ref, acc_ref):
    qb = pl.program_id(1)
    q0 = qb * TQ
    nk = (q0 + TQ + CKP - 1) // CKP
    tiles_per_chunk = CKP // LANES
    back_tiles = STRIP_BACK
    strip_off = LANES * back_tiles

    @pl.when((pl.program_id(0) == 0) & (qb == 0))
    def _():
        _build_bias_strip(strip_ref, rb_ref, strip_off, 0, log2_relative=True)
        planes_ref[...] = jnp.zeros(planes_ref.shape, I32)

    qpos = q0 + lax.broadcasted_iota(I32, (1, TQ), 1)

    qit = qit_ref[0]
    qstack = jnp.concatenate([qit[h * IDX_DIM:(h + 1) * IDX_DIM] for h in range(N_IDX_HEADS)],
                             axis=1)
    w = misct_ref[0, MISC_W:MISC_W + N_IDX_HEADS, :] * (N_IDX_HEADS ** -0.5 * IDX_DIM ** -0.5)

    def score_chunk(c, _):
        c0 = pl.multiple_of(c * CKP, CKP)
        d = jnp.dot(ki_ref[pl.ds(c0, CKP), :], qstack, preferred_element_type=F32)
        d = jnp.maximum(d, 0.0)
        s = jnp.zeros((CKP, TQ), F32)
        for h in range(N_IDX_HEADS):
            s = s + d[:, h * TQ:(h + 1) * TQ] * w[h:h + 1, :]
        idx = c0 + lax.broadcasted_iota(I32, (CKP, TQ), 0)
        score_ref[pl.ds(c0, CKP), :] = jnp.where(idx <= qpos, s, NEG)
        return 0

    lax.fori_loop(0, nk, score_chunk, 0)

    def search(_):
        proposal = _kth_largest_by_planes(score_ref, planes_ref, nk, CKP, n_sel)
        return _select_topk(proposal, True, _counter(score_ref, nk, CKP, 0), (1, TQ), n_sel, idx_bits)

    def everything(_):
        return jnp.full((1, TQ), NEG, F32), jnp.full((1, TQ), jnp.iinfo(jnp.int32).max, I32)

    thr, jstar = lax.cond(q0 + TQ <= n_sel, everything, search, 0)

    qat = qat_ref[0]
    zeros = jnp.zeros((HEAD_DIM, TQ), qat.dtype)
    qgroups = []
    for n in range(N_KV_HEADS):
        tiles = []
        for g in range(GROUP):
            h = n * GROUP + g
            x = qat[h * HEAD_DIM:(h + 1) * HEAD_DIM]
            tiles.append(jnp.concatenate([x, zeros] if n == 0 else [zeros, x], axis=0))
        qgroups.append(jnp.concatenate(tiles, axis=1))

    def logits_into(s_ref, c):
        kc = k_ref[pl.ds(pl.multiple_of(c * CKP, CKP), CKP), :]
        for n in range(N_KV_HEADS):
            s_ref[n] = jnp.dot(kc, qgroups[n], preferred_element_type=F32)

    def attend_chunk(c, s_cur_ref, s_next_ref):
        if s_next_ref is not None:
            logits_into(s_next_ref, jnp.minimum(c + 1, nk - 1))
        c0 = pl.multiple_of(c * CKP, CKP)
        idx = c0 + lax.broadcasted_iota(I32, (CKP, TQ), 0)
        valid = _valid_mask(score_ref[pl.ds(c0, CKP), :], idx, thr, jstar, qpos)
        vct = vt_ref[0, c]
        tiles = [jnp.maximum(back_tiles + j - (qb * (TQ // LANES) - c * tiles_per_chunk), 0)
                 for j in range(tiles_per_chunk)]
        for n in range(N_KV_HEADS):
            m_old = m_ref[n]
            parts = []
            for g in range(GROUP):
                bias = jnp.concatenate([strip_ref[n * GROUP + g, t] for t in tiles], axis=0)
                parts.append(jnp.where(valid, s_cur_ref[n, :, g * TQ:(g + 1) * TQ] + bias, NEG))
            sm = jnp.concatenate(parts, axis=1)
            m_new = jnp.maximum(m_old, jnp.max(sm, axis=0, keepdims=True))
            p = jnp.exp2((sm - m_new).astype(BF16))
            acc_ref[n] = jnp.exp2(m_old - m_new) * acc_ref[n] + jnp.dot(vct, p, preferred_element_type=F32)
            m_ref[n] = m_new

    m_ref[...] = jnp.full(m_ref.shape, NEG, F32)
    acc_ref[...] = jnp.zeros(acc_ref.shape, F32)
    logits_into(sa_ref, 0)

    def attend_pair(i, _):
        attend_chunk(2 * i, sa_ref, sb_ref)
        attend_chunk(2 * i + 1, sb_ref, sa_ref)
        return 0

    lax.fori_loop(0, nk // 2, attend_pair, 0)

    @pl.when(nk % 2 == 1)
    def _():
        attend_chunk(nk - 1, sa_ref, None)

    carries = [(None, acc_ref[n]) for n in range(N_KV_HEADS)]
    res = [acc[:LANES] / acc[LANES:LANES + 1] for (_, acc) in carries]
    row = lax.broadcasted_iota(I32, (LANES, TQ), 0)
    for j in range(GROUP):
        cols = slice(j * TQ, (j + 1) * TQ)
        tile_t = jnp.where(row < HEAD_DIM, res[0][:, cols], res[1][:, cols])
        out_ref[:, j * LANES:(j + 1) * LANES] = tile_t.T.astype(out_ref.dtype)


def _dsa_prompt(rel_bias, qat, qit, misct, kib, kb, vtb, B, T):
    assert T % CKP == 0 and T % TQ == 0
    nq = T // TQ
    n_sel = min(TOPK_MAX, T // 4)
    qcols = lambda w: pl.BlockSpec((1, w, TQ), lambda b, q: (b, 0, q))
    seq = lambda w: pl.BlockSpec((T, w), lambda b, q: (b, 0))
    return pl.pallas_call(
        functools.partial(_dsa_prompt_kernel, n_sel, max(1, (T - 1).bit_length())),
        grid=(B, nq),
        in_specs=[pl.BlockSpec(memory_space=pltpu.SMEM), qcols(ATT_WIDTH), qcols(N_IDX_HEADS * IDX_DIM),
                  qcols(LANES), seq(IDX_DIM), seq(LANES),
                  pl.BlockSpec((1, T // CKP, VT_ROWS, CKP), lambda b, q: (b, 0, 0, 0))],
        out_specs=pl.BlockSpec((TQ, ATT_WIDTH), lambda b, q: (b * nq + q, 0)),
        out_shape=jax.ShapeDtypeStruct((B * T, ATT_WIDTH), BF16),
        scratch_shapes=[pltpu.VMEM((T, TQ), F32),
                        pltpu.VMEM((32, T // CKP, SUBLANES, TQ), I32),
                        pltpu.VMEM((N_HEADS, STRIP_BACK + max(TQ, CKP) // LANES, LANES, TQ), F32),
                        pltpu.VMEM((N_KV_HEADS, CKP, GROUP * TQ), F32),
                        pltpu.VMEM((N_KV_HEADS, CKP, GROUP * TQ), F32),
                        pltpu.VMEM((N_KV_HEADS, 1, GROUP * TQ), F32),
                        pltpu.VMEM((N_KV_HEADS, VT_ROWS, GROUP * TQ), F32)],
        compiler_params=_cparams(2),
        name="dsa_prompt",
    )(rel_bias, qat, qit, misct, kib, kb, vtb)


def _page_pipeline(pt_ref, n_pages, caches, bufs, sems):
    def copies(bb, sl, j):
        pid = pt_ref[bb, j]
        cols = pl.ds(j * PAGE_SIZE, PAGE_SIZE)
        return [pltpu.make_async_copy(c.at[pid], buf.at[sl, :, cols], sems.at[sl, i])
                for i, (c, buf) in enumerate(zip(caches, bufs))]

    def start_all(bb, sl):
        for j in range(n_pages):
            for cp in copies(bb, sl, j):
                cp.start()

    def wait_all(bb, sl):
        for j in range(n_pages):
            for cp in copies(bb, sl, j):
                cp.wait()

    def step():
        b = pl.program_id(0)
        slot = b % 2

        @pl.when(b == 0)
        def _():
            start_all(0, 0)

        @pl.when(b + 1 < pl.num_programs(0))
        def _():
            start_all(b + 1, 1 - slot)

        wait_all(b, slot)
        return slot

    return step


def _pad_rows(x, rows):
    return jnp.concatenate([x, jnp.zeros((rows - x.shape[0], x.shape[1]), x.dtype)], axis=0)


def _sample_score_kernel(n_sel, idx_bits, n_pages, ts, ck, pt_ref, qi_ref, misc_ref, kin_ref,
                         ckit_hbm, score_ref, thr_ref, jst_ref, ki_buf, sems, planes_ref):
    b = pl.program_id(0)
    past = n_pages * PAGE_SIZE
    slot = _page_pipeline(pt_ref, n_pages, [ckit_hbm], [ki_buf], sems)()

    qi = qi_ref[...]
    qstack = jnp.concatenate([qi[:, h * IDX_DIM:(h + 1) * IDX_DIM] for h in range(N_IDX_HEADS)],
                             axis=0).astype(BF16)
    w = misc_ref[:, MISC_W:MISC_W + N_IDX_HEADS] * (N_IDX_HEADS ** -0.5 * IDX_DIM ** -0.5)
    d_past = jnp.dot(qstack, ki_buf[slot].astype(BF16), preferred_element_type=F32)
    d_own = lax.dot_general(qstack, _pad_rows(kin_ref[...], PAGE_SIZE).astype(BF16), NT_DIMS,
                            preferred_element_type=F32)
    d = jnp.maximum(jnp.concatenate([d_past, d_own], axis=1), 0.0)
    s = jnp.zeros((ts, past + PAGE_SIZE), F32)
    for h in range(N_IDX_HEADS):
        s = s + d[h * ts:(h + 1) * ts] * w[:, h:h + 1]
    idx = lax.broadcasted_iota(I32, s.shape, 1)
    qpos = past + lax.broadcasted_iota(I32, (ts, 1), 0)
    score_ref[pl.ds(pl.multiple_of(b * ts, ts), ts), :] = jnp.where(idx <= qpos, s, NEG)

    @pl.when(b == pl.num_programs(0) - 1)
    def _():
        reducers = _counter(score_ref, score_ref.shape[1] // ck, ck, 1)
        qshape = (score_ref.shape[0], 1)
        thr = _kth_largest_by_planes_lanes(score_ref, planes_ref, n_sel)
        thr, jstar = _select_topk(thr, True, reducers, qshape, n_sel, idx_bits)
        thr_ref[...] = jnp.broadcast_to(thr, thr_ref.shape)
        jst_ref[...] = jnp.broadcast_to(jstar, jst_ref.shape)


def _sample_attend_kernel(n_pages, ts, pt_ref, rb_ref, qa_ref, score_ref, thr_ref, jst_ref, kn_ref,
                          vn_ref, ckt_hbm, cvt_hbm, out_ref, k_buf, v_buf, sems, strip_ref):
    b = pl.program_id(0)
    past = n_pages * PAGE_SIZE
    rows = N_HEADS * ts

    @pl.when(b == 0)
    def _():
        _build_bias_strip(strip_ref, rb_ref, LANES * (strip_ref.shape[1] - 1), 1)

    slot = _page_pipeline(pt_ref, n_pages, [ckt_hbm, cvt_hbm], [k_buf, v_buf], sems)()

    q2 = jnp.concatenate(_group_queries(qa_ref[...], ts), axis=0)
    k_own = _pad_rows(kn_ref[...], PAGE_SIZE).astype(BF16)
    v_own = _pad_rows(vn_ref[...], PAGE_SIZE).astype(BF16)
    s_past = jnp.dot(q2, k_buf[slot].astype(BF16), preferred_element_type=F32)
    s_own = lax.dot_general(q2, k_own, NT_DIMS, preferred_element_type=F32)
    far = strip_ref[:, 0].reshape(rows, LANES)[:, 0:1]
    near = [strip_ref[:, t].reshape(rows, LANES) for t in (1, 2)]
    s = jnp.concatenate([s_past[:, :past - PAGE_SIZE] + far, s_past[:, past - PAGE_SIZE:] + near[0],
                         s_own + near[1]], axis=1)

    qpos = past + lax.broadcasted_iota(I32, (ts, 1), 0)
    scores = score_ref[...]
    valid = _valid_mask(scores, lax.broadcasted_iota(I32, scores.shape, 1), thr_ref[:, 0:1],
                        jst_ref[:, 0:1], qpos)
    s = jnp.where(valid[None], s.reshape(N_HEADS, ts, past + PAGE_SIZE), NEG).reshape(rows, -1)
    m = jnp.max(s, axis=1, keepdims=True)
    p = jnp.exp(s - m)
    l = jnp.sum(p, axis=1, keepdims=True)
    pb = p.astype(BF16)
    pv = lax.dot_general(pb[:, :past], v_buf[slot].astype(BF16), NT_DIMS, preferred_element_type=F32)
    pv = pv + jnp.dot(pb[:, past:], v_own, preferred_element_type=F32)
    half = GROUP * ts
    carries = [(None, l[n * half:(n + 1) * half], pv[n * half:(n + 1) * half]) for n in range(N_KV_HEADS)]
    _write_attn(out_ref, carries, ts)


def _dsa_sample(page_table, rel_bias, qa, qi, misc, ki_new, k_new, v_new, ckit, ckt, cvt, DB, ts):
    n_pages = page_table.shape[1]
    past = n_pages * PAGE_SIZE
    n_sel = min(TOPK_MAX, (past + ts) // 4)
    lpad = past + PAGE_SIZE
    idx_bits = max(1, (lpad - 1).bit_length())
    ck = LANES * math.gcd(lpad // LANES, 5)
    assert ts % SUBLANES == 0 and ts <= PAGE_SIZE and n_pages >= 1
    blk = lambda w: pl.BlockSpec((ts, w), lambda b, pt: (b, 0))
    whole = lambda w: pl.BlockSpec((DB * ts, w), lambda b, pt: (0, 0))
    hbm = pl.BlockSpec(memory_space=pl.ANY)
    keys, thr, jstar = pl.pallas_call(
        functools.partial(_sample_score_kernel, n_sel, idx_bits, n_pages, ts, ck),
        grid_spec=pltpu.PrefetchScalarGridSpec(
            num_scalar_prefetch=1,
            grid=(DB,),
            in_specs=[blk(N_IDX_HEADS * IDX_DIM), blk(LANES), blk(IDX_DIM), hbm],
            out_specs=[whole(lpad), whole(LANES), whole(LANES)],
            scratch_shapes=[pltpu.VMEM((2, IDX_DIM, past), F32), pltpu.SemaphoreType.DMA((2, 1)),
                            pltpu.VMEM((32, -(-lpad // (32 * LANES)), DB * ts, LANES), I32)]),
        out_shape=[jax.ShapeDtypeStruct((DB * ts, lpad), F32),
                   jax.ShapeDtypeStruct((DB * ts, LANES), F32),
                   jax.ShapeDtypeStruct((DB * ts, LANES), I32)],
        compiler_params=_cparams(1),
        name="sample_score",
    )(page_table, qi, misc, ki_new, ckit)
    return pl.pallas_call(
        functools.partial(_sample_attend_kernel, n_pages, ts),
        grid_spec=pltpu.PrefetchScalarGridSpec(
            num_scalar_prefetch=1,
            grid=(DB,),
            in_specs=[pl.BlockSpec(memory_space=pltpu.SMEM), blk(ATT_WIDTH), blk(lpad), blk(LANES),
                      blk(LANES), blk(LANES), blk(LANES), hbm, hbm],
            out_specs=blk(ATT_WIDTH),
            scratch_shapes=[pltpu.VMEM((2, LANES, past), F32), pltpu.VMEM((2, LANES, past), F32),
                            pltpu.SemaphoreType.DMA((2, 2)),
                            pltpu.VMEM((N_HEADS, 3, ts, LANES), F32)]),
        out_shape=jax.ShapeDtypeStruct((DB * ts, ATT_WIDTH), F32),
        compiler_params=_cparams(1),
        name="sample_attend",
    )(page_table, rel_bias, qa, keys, thr, jstar, k_new, v_new, ckt, cvt)


def _pack_layer_weights(w_in, b_i, b_f, w_out, w_up, w_down):
    D = w_in.shape[0]
    sizes = (ATT_WIDTH, N_KV_HEADS * HEAD_DIM, N_KV_HEADS * HEAD_DIM, N_IDX_HEADS * IDX_DIM, IDX_DIM,
             N_IDX_HEADS, M_WIDTH, M_WIDTH, M_WIDTH, M_WIDTH, M_HEADS, M_HEADS)
    assert w_in.shape[1] == sum(sizes)
    pts = np.cumsum((0,) + sizes)
    w_t = w_in.T.astype(BF16)
    qa, k, v, qi, ki, wi, qm, km, vm, om, im, fm = [w_t[pts[i]:pts[i + 1]] for i in range(len(sizes))]
    perm = np.asarray(HEAD_PERM)
    qa_perm = qa.reshape(N_HEADS, HEAD_DIM, D)[perm].reshape(ATT_WIDTH, D)
    misc = jnp.concatenate([wi, im, fm, jnp.zeros((LANES - N_IDX_HEADS - 2 * M_HEADS, D), BF16)], axis=0)
    wp = jnp.concatenate([qa_perm, v, qi, k, ki, ki, misc, qm, km, vm, om], axis=0).T
    assert wp.shape[1] == N_PACK
    gate_bias = jnp.broadcast_to(jnp.concatenate([b_i, b_f]).astype(F32)[:, None], (2 * M_HEADS, LANES))
    woa = w_out[:ATT_WIDTH].reshape(N_HEADS, HEAD_DIM, -1)[perm].reshape(ATT_WIDTH, -1).astype(BF16)
    woh = w_out[ATT_WIDTH:].astype(BF16)
    wt = jnp.concatenate([k, v, ki, jnp.zeros((LANES - IDX_DIM, D), BF16), qa, qi, misc], axis=0)
    assert wt.shape[0] == N_TPACK
    return wp, wt, gate_bias, woa, woh, w_up.astype(BF16), w_down.astype(BF16)


def _layer(x, packed, g1, g2, mnorm, rel_bias, gf, final_norm, past):
    wp, wt, gate_bias, woa, woh, wup, wdn = packed
    B, T, D = x.shape
    x2 = x.reshape(B * T, D)
    tm = math.gcd(T if past is None else B * T, 512)
    kv_w = N_KV_HEADS * HEAD_DIM
    if past is None:
        (misc, qm, km, vm, om, kb, kib, kt, vt, kit, vtb, qat, qit, misct) = _inproj(
            x2, g1.reshape(1, D), wp, wt, B, T, tm)
        attn = _dsa_prompt(rel_bias, qat, qit, misct, kib, kb, vtb, B, T)
        state = None
        k_new = kt.reshape(B, N_KV_HEADS, HEAD_DIM, T).transpose(0, 3, 1, 2)
        v_new = vt.reshape(B, N_KV_HEADS, HEAD_DIM, T).transpose(0, 3, 1, 2)
        ki_new = kit.transpose(0, 2, 1)
    else:
        (misc, qm, km, vm, om, qa, qi, k, v, ki) = _inproj(x2, g1.reshape(1, D), wp, None, B, T, tm)
        page_table, cache_k, cache_v, cache_kidx, c0, n0, m0 = past
        n_pool = cache_k.shape[0]
        ckt = cache_k.transpose(0, 2, 3, 1).reshape(n_pool, kv_w, PAGE_SIZE)
        cvt = cache_v.transpose(0, 2, 3, 1).reshape(n_pool, kv_w, PAGE_SIZE)
        ckit = cache_kidx.transpose(0, 2, 1)
        attn = _dsa_sample(page_table, rel_bias, qa, qi, misc, ki, k, v, ckit, ckt, cvt, B, T)
        state = (c0, n0, jnp.broadcast_to(m0[..., None], m0.shape + (LANES,)))
        k_new = k.reshape(B, T, N_KV_HEADS, HEAD_DIM)
        v_new = v.reshape(B, T, N_KV_HEADS, HEAD_DIM)
        ki_new = ki.reshape(B, T, IDX_DIM)
    h, c_new, n_new, m_new = _mlstm(qm, km, vm, om, misc, gate_bias, mnorm.reshape(1, M_WIDTH), state,
                                    B, T, BF16 if past is None else F32)
    y = _post(x2, attn, h, woa, woh, g2.reshape(1, D), wup, wdn, gf.reshape(1, D), final_norm, tm)
    return (y.reshape(B, T, D), k_new, v_new, ki_new, c_new, n_new, m_new[..., 0])


def kernel(x_prompt, x_sample, cache_k, cache_v, cache_kidx, page_table, state_C, state_n, state_m,
           w_in, b_igate, b_fgate, mlstm_norm, rel_bias, w_out, norm1, norm2, w_up, w_down, norm_f):
    depth = w_in.shape[0]
    xp, xs = x_prompt, x_sample
    outs_p, outs_s = [], []
    for l in range(depth):
        packed = _pack_layer_weights(w_in[l], b_igate[l], b_fgate[l], w_out[l], w_up[l], w_down[l])
        last = l == depth - 1
        common = (packed, norm1[l], norm2[l], mlstm_norm[l], rel_bias, norm_f, last)
        rp = _layer(xp, *common, None)
        rs = _layer(xs, *common, (page_table, cache_k[l], cache_v[l], cache_kidx[l],
                                  state_C[l], state_n[l], state_m[l]))
        xp, xs = rp[0], rs[0]
        outs_p.append(rp[1:])
        outs_s.append(rs[1:])
    stack = lambda outs, i: jnp.stack([o[i] for o in outs])
    return ((xp, xs) + tuple(stack(outs_p, i) for i in range(6))
            + tuple(stack(outs_s, i) for i in range(6)))
```

```python
import functools
import math

import numpy as np
import jax
import jax.numpy as jnp
from jax import lax
from jax.experimental import pallas as pl
from jax.experimental.pallas import tpu as pltpu

F32 = jnp.float32
BF16 = jnp.bfloat16
I32 = jnp.int32

N_HEADS = 8
HEAD_DIM = 64
N_KV_HEADS = 2
GROUP = N_HEADS // N_KV_HEADS
N_IDX_HEADS = 8
IDX_DIM = 64
TOPK_MAX = 256
N_BUCKETS = 32
MAX_DISTANCE = 128
M_HEADS = 4
M_HEAD_DIM = 128
PAGE_SIZE = 128
EPS = 1e-6
NEG = -1e30
LOG2E = math.log2(math.e)
ATT_WIDTH = N_HEADS * HEAD_DIM
M_WIDTH = M_HEADS * M_HEAD_DIM

LANES = 128
SUBLANES = 8
VMEM_LIMIT = 56 * 1024 * 1024

C_QA = 0
C_V = C_QA + ATT_WIDTH
C_QI = C_V + LANES
C_K = C_QI + N_IDX_HEADS * IDX_DIM
C_KI2 = C_K + LANES
C_MISC = C_KI2 + LANES
C_QM = C_MISC + LANES
C_KM = C_QM + M_WIDTH
C_VM = C_KM + M_WIDTH
C_OM = C_VM + M_WIDTH
N_PACK = C_OM + M_WIDTH
MISC_W = 0
MISC_I = 8
MISC_F = 12

HEAD_PERM = (0, 4, 1, 5, 2, 6, 3, 7)

TQ = 256
STRIP_BACK = 2
assert LANES * (STRIP_BACK - 1) >= MAX_DISTANCE
CKP = 256
VT_ROWS = LANES + 16
ML = 256

NT_DIMS = (((1,), (1,)), ((), ()))
TN_DIMS = (((0,), (0,)), ((), ()))


def _bucket_bounds():
    max_exact = N_BUCKETS // 2
    scale = (N_BUCKETS - max_exact) / math.log(MAX_DISTANCE / max_exact)

    def bucket(n, dt):
        if n < max_exact:
            return n
        val = np.log(np.asarray(max(n, 1), dt) / dt(max_exact)) * dt(scale)
        return min(max_exact + int(val), N_BUCKETS - 1)

    table = [bucket(n, np.float32) for n in range(MAX_DISTANCE + 2)]
    assert table == [bucket(n, np.float64) for n in range(MAX_DISTANCE + 2)]
    assert table[MAX_DISTANCE] == N_BUCKETS - 1
    return [next(d for d, b in enumerate(table) if b >= k) for k in range(N_BUCKETS)]


BUCKET_BOUNDS = _bucket_bounds()


def _cparams(n_axes):
    return pltpu.CompilerParams(dimension_semantics=("arbitrary",) * n_axes,
                                vmem_limit_bytes=VMEM_LIMIT)


def _const_spec(shape):
    nd = len(shape)
    return pl.BlockSpec(shape, lambda *_: (0,) * nd, pipeline_mode=pl.Buffered(1))


def _rms(x, g):
    return x * lax.rsqrt(jnp.mean(x * x, axis=-1, keepdims=True) + EPS) * g


def _inproj_mlstm(mm, misc_ref, qm_ref, km_ref, vm_ref, om_ref):
    misc_ref[...] = mm(C_MISC, LANES)
    qm_ref[...] = mm(C_QM, M_WIDTH).astype(qm_ref.dtype)
    km_ref[...] = (mm(C_KM, M_WIDTH) * (M_HEAD_DIM ** -0.5)).astype(km_ref.dtype)
    vm_ref[...] = mm(C_VM, M_WIDTH).astype(vm_ref.dtype)
    om_ref[...] = mm(C_OM, M_WIDTH)


def _inproj_rows_kernel(x_ref, g_ref, w_ref, misc_ref, qm_ref, km_ref, vm_ref, om_ref,
                        qa_ref, qi_ref, k_ref, v_ref, ki_ref):
    ub = _rms(x_ref[...], g_ref[...]).astype(BF16)
    z = jnp.dot(ub, w_ref[...], preferred_element_type=F32)
    mm = lambda c0, n: z[:, c0:c0 + n]
    _inproj_mlstm(mm, misc_ref, qm_ref, km_ref, vm_ref, om_ref)
    qa_ref[...] = mm(C_QA, ATT_WIDTH) * (HEAD_DIM ** -0.5)
    qi_ref[...] = mm(C_QI, N_IDX_HEADS * IDX_DIM)
    k_ref[...] = mm(C_K, LANES)
    v_ref[...] = mm(C_V, LANES)
    ki_ref[...] = mm(C_KI2, LANES)[:, :IDX_DIM]


R_K = 0
R_V = R_K + LANES
R_KI = R_V + LANES
R_QA = R_KI + LANES
R_QI = R_QA + ATT_WIDTH
R_MISC = R_QI + N_IDX_HEADS * IDX_DIM
N_TPACK = R_MISC + LANES


def _inproj_cols_kernel(x_ref, g_ref, w_ref, wt_ref, misc_ref, qm_ref, km_ref, vm_ref, om_ref,
                        kb_ref, kib_ref, kt_ref, vt_ref, kit_ref, vtb_ref, qat_ref, qit_ref,
                        misct_ref):
    ub = _rms(x_ref[...], g_ref[...]).astype(BF16)
    z = jnp.dot(ub, w_ref[:, C_K:], preferred_element_type=F32)
    mm = lambda c0, n: z[:, c0 - C_K:c0 - C_K + n]
    _inproj_mlstm(mm, misc_ref, qm_ref, km_ref, vm_ref, om_ref)
    kb_ref[...] = mm(C_K, LANES).astype(BF16)
    kib_ref[...] = mm(C_KI2, LANES)[:, :IDX_DIM].astype(BF16)

    zt = lax.dot_general(wt_ref[...], ub, NT_DIMS, preferred_element_type=F32)
    mt = lambda r0, n: zt[r0:r0 + n]

    kt_ref[0] = mt(R_K, LANES)
    vt = mt(R_V, LANES)
    vt_ref[0] = vt
    ones = jnp.ones((VT_ROWS - LANES, CKP), BF16)
    for j in range(vtb_ref.shape[1]):
        vtb_ref[0, j] = jnp.concatenate([vt[:, j * CKP:(j + 1) * CKP].astype(BF16), ones], axis=0)
    kit_ref[0] = mt(R_KI, IDX_DIM)
    qat_ref[0] = (mt(R_QA, ATT_WIDTH) * (HEAD_DIM ** -0.5 * LOG2E)).astype(BF16)
    qit_ref[0] = mt(R_QI, N_IDX_HEADS * IDX_DIM).astype(BF16)
    misct_ref[0] = mt(R_MISC, LANES)


def _inproj(x2, g1, wp, wt, B, T, tm):
    R, D = x2.shape
    assert R == B * T and R % tm == 0
    mdt = F32 if wt is None else BF16
    row = lambda i: (i, 0)
    outs = [(LANES, F32), (M_WIDTH, mdt), (M_WIDTH, mdt), (M_WIDTH, mdt), (M_WIDTH, F32)]
    in_specs = [pl.BlockSpec((tm, D), row), _const_spec((1, D)), _const_spec((D, N_PACK))]
    if wt is None:
        kern, args = _inproj_rows_kernel, (x2, g1, wp)
        outs += [(ATT_WIDTH, F32), (N_IDX_HEADS * IDX_DIM, F32), (LANES, F32), (LANES, F32), (IDX_DIM, F32)]
    else:
        kern, args = _inproj_cols_kernel, (x2, g1, wp, wt)
        in_specs.append(_const_spec(wt.shape))
        outs += [(LANES, BF16), (IDX_DIM, BF16)]
    out_specs = [pl.BlockSpec((tm, w), row) for w, _ in outs]
    out_shape = [jax.ShapeDtypeStruct((R, w), dt) for w, dt in outs]
    if wt is not None:
        assert T % tm == 0 and tm % CKP == 0
        tpb, cpt = T // tm, tm // CKP
        cols = lambda i: (i // tpb, 0, i % tpb)
        for w, dt in ((LANES, F32), (LANES, F32), (IDX_DIM, F32)):
            out_specs.append(pl.BlockSpec((1, w, tm), cols))
            out_shape.append(jax.ShapeDtypeStruct((B, w, T), dt))
        out_specs.append(pl.BlockSpec((1, cpt, VT_ROWS, CKP), lambda i: (i // tpb, i % tpb, 0, 0)))
        out_shape.append(jax.ShapeDtypeStruct((B, T // CKP, VT_ROWS, CKP), BF16))
        for w, dt in ((ATT_WIDTH, BF16), (N_IDX_HEADS * IDX_DIM, BF16), (LANES, F32)):
            out_specs.append(pl.BlockSpec((1, w, tm), cols))
            out_shape.append(jax.ShapeDtypeStruct((B, w, T), dt))
    return pl.pallas_call(
        kern,
        grid=(R // tm,),
        in_specs=in_specs,
        out_specs=out_specs,
        out_shape=out_shape,
        compiler_params=_cparams(1),
        name="inproj",
    )(*args)


def _post_kernel(ff_chunk, final_norm, x_ref, a_ref, h_ref, woa_ref, woh_ref, g2_ref, wup_ref,
                 wdn_ref, gf_ref, y_ref):
    mix = jnp.dot(a_ref[...].astype(BF16), woa_ref[...], preferred_element_type=F32)
    mix = mix + jnp.dot(h_ref[...].astype(BF16), woh_ref[...], preferred_element_type=F32)
    hres = x_ref[...] + mix
    f = _rms(hres, g2_ref[...]).astype(BF16)
    acc = hres
    for c0 in range(0, wup_ref.shape[1], ff_chunk):
        up = jnp.dot(f, wup_ref[:, c0:c0 + ff_chunk], preferred_element_type=F32)
        r = jnp.maximum(up, 0.0)
        acc = acc + jnp.dot((r * r).astype(BF16), wdn_ref[c0:c0 + ff_chunk, :],
                            preferred_element_type=F32)
    y_ref[...] = _rms(acc, gf_ref[...]) if final_norm else acc


def _post(x2, attn, h, woa, woh, g2, wup, wdn, gf, final_norm, tm):
    R, D = x2.shape
    dff = wup.shape[1]
    assert R % tm == 0
    row = lambda i: (i, 0)
    return pl.pallas_call(
        functools.partial(_post_kernel, min(dff, 1024), final_norm),
        grid=(R // tm,),
        in_specs=[pl.BlockSpec((tm, D), row), pl.BlockSpec((tm, ATT_WIDTH), row),
                  pl.BlockSpec((tm, M_WIDTH), row), _const_spec(woa.shape), _const_spec(woh.shape),
                  _const_spec((1, D)), _const_spec(wup.shape), _const_spec(wdn.shape),
                  _const_spec((1, D))],
        out_specs=pl.BlockSpec((tm, D), row),
        out_shape=jax.ShapeDtypeStruct((R, D), F32),
        compiler_params=_cparams(1),
        name="post",
    )(x2, attn, h, woa, woh, g2, wup, wdn, gf)


def _log_sigmoid(x):
    return -(jnp.maximum(-x, 0.0) + jnp.log1p(jnp.exp(-jnp.abs(x))))


def _mlstm_kernel(nvalid, has_state, *refs):
    if has_state:
        (q_ref, k_ref, v_ref, o_ref, misc_ref, gb_ref, mn_ref, c0_ref, n0_ref, m0_ref,
         h_ref, c_ref, n_ref, m_ref, z_scr, rows_scr) = refs
    else:
        (q_ref, k_ref, v_ref, o_ref, misc_ref, gb_ref, mn_ref,
         h_ref, c_ref, n_ref, m_ref, z_scr, rows_scr) = refs
    nb, nc, L = z_scr.shape[0], z_scr.shape[1], z_scr.shape[2]
    c = pl.program_id(1)

    def padded(x, dt):
        x = x.astype(dt)
        if nvalid == L:
            return x
        return jnp.concatenate([x, jnp.zeros((L - nvalid, x.shape[1]), dt)], axis=0)

    r2 = lax.broadcasted_iota(I32, (L, L), 0)
    c2 = lax.broadcasted_iota(I32, (L, L), 1)
    tril = r2 >= c2

    @pl.when(c == 0)
    def _():
        if has_state:
            c_ref[...] = c0_ref[...]
            n_ref[...] = n0_ref[...]
            m_ref[...] = m0_ref[...]
        else:
            c_ref[...] = jnp.zeros_like(c_ref)
            n_ref[...] = jnp.zeros_like(n_ref)
            m_ref[...] = jnp.zeros_like(m_ref)
        assert MISC_F == MISC_I + M_HEADS and MISC_I % SUBLANES == 0
        row8 = lax.broadcasted_iota(I32, (SUBLANES, L), 0)
        tok8 = lax.broadcasted_iota(I32, (SUBLANES, L), 1)
        bias8 = jnp.concatenate([gb_ref[...]] * (L // LANES), axis=1) if L >= LANES else gb_ref[:, :L]
        triu = (r2 <= c2).astype(F32)
        for s in range(nb):
            for cc in range(nc):
                misc_t = padded(misc_ref[s, cc * nvalid:(cc + 1) * nvalid, :], F32).T
                gx = misc_t[MISC_I:MISC_I + SUBLANES] + bias8
                gates = jnp.where(row8 >= M_HEADS, _log_sigmoid(gx), gx)
                if nvalid != L:
                    gates = jnp.where(tok8 < nvalid, gates, jnp.where(row8 >= M_HEADS, 0.0, NEG))
                cum = jnp.dot(gates, triu, precision=lax.Precision.HIGHEST,
                              preferred_element_type=F32)
                rows = jnp.where(row8 >= M_HEADS, cum, gates)
                rows_scr[s, cc] = rows
                z_scr[s, cc] = jnp.concatenate([rows, jnp.zeros((LANES - SUBLANES, L), F32)], axis=0).T

    loaded = []
    for s in range(nb):
        loaded.append((z_scr[s, c], rows_scr[s, c], padded(q_ref[s], BF16), padded(k_ref[s], BF16),
                       padded(v_ref[s], BF16), padded(o_ref[s], F32), m_ref[s], c_ref[s], n_ref[s]))
    chains = [(s, hd) for s in range(nb) for hd in range(M_HEADS)]

    def operands(s, hd):
        z, rows, qb, kb, vb, ob, m_all, c_all, n_all = loaded[s]
        sl = slice(hd * M_HEAD_DIM, (hd + 1) * M_HEAD_DIM)
        return dict(
            sl=sl, q=qb[:, sl], k=kb[:, sl], v=vb[:, sl], o=ob[:, sl],
            icol=z[:, hd:hd + 1], bcol=z[:, M_HEADS + hd:M_HEADS + hd + 1],
            irow=rows[hd:hd + 1, :], brow=rows[M_HEADS + hd:M_HEADS + hd + 1, :],
            m_prev=m_all[hd:hd + 1, 0:1], s_prev=c_all[hd], n_prev=n_all[hd:hd + 1, :])

    def lane_sum(x):
        ones = jnp.ones((x.shape[1], LANES), BF16)
        head = x.astype(BF16)
        tail = (x - head.astype(F32)).astype(BF16)
        return (jnp.dot(head, ones, preferred_element_type=F32)
                + jnp.dot(tail, ones, preferred_element_type=F32))

    st = [operands(s, hd) for s, hd in chains]
    for x in st:
        x["qk"] = lax.dot_general(x["q"], x["k"], NT_DIMS, preferred_element_type=F32)
        x["qs"] = jnp.dot(x["q"], x["s_prev"].astype(BF16), preferred_element_type=F32)
        x["qn"] = lane_sum(x["q"].astype(F32) * x["n_prev"])
    for x in st:
        g = x["bcol"] + x["m_prev"]
        dm = jnp.where(tril, x["bcol"] - x["brow"] + x["irow"], NEG)
        x["mt"] = jnp.maximum(g, jnp.max(dm, axis=1, keepdims=True))
        x["gw"] = jnp.exp(g - x["mt"])
        x["qk"] = x["qk"] * jnp.exp(dm - x["mt"])
        b_last = x["bcol"][L - 1:L, :]
        g_last = b_last + x["m_prev"]
        a = b_last - x["bcol"] + x["icol"]
        x["m_new"] = jnp.maximum(g_last, jnp.max(a, axis=0, keepdims=True))
        x["sw"] = jnp.exp(g_last - x["m_new"])
        x["ak"] = jnp.exp(a - x["m_new"]) * x["k"].astype(F32)
    for x in st:
        x["pv"] = jnp.dot(x["qk"].astype(BF16), x["v"], preferred_element_type=F32)
        x["qksum"] = lane_sum(x["qk"])
        x["kv"] = lax.dot_general(x["ak"].astype(BF16), x["v"], TN_DIMS, preferred_element_type=F32)
    stores = []
    for (s, hd), x in zip(chains, st):
        num = x["gw"] * x["qs"] + x["pv"]
        den = x["gw"] * x["qn"] + x["qksum"]
        hh = num / jnp.maximum(jnp.abs(den), jnp.exp(-x["mt"]))
        hh = hh * lax.rsqrt(lane_sum(hh * hh) * (1.0 / M_HEAD_DIM) + EPS)
        hh = hh * mn_ref[:, x["sl"]] * jax.nn.sigmoid(x["o"])
        c_new = x["sw"] * x["s_prev"] + x["kv"]
        n_new = x["sw"] * x["n_prev"] + jnp.sum(x["ak"], axis=0, keepdims=True)
        stores.append((s, hd, x["sl"], hh[:nvalid].astype(h_ref.dtype), c_new, n_new,
                       jnp.broadcast_to(x["m_new"], (1, LANES))))
    for s, hd, sl, h_new, c_new, n_new, m_new in stores:
        h_ref[s, :, sl] = h_new
        c_ref[s, hd] = c_new
        n_ref[s, hd:hd + 1, :] = n_new
        m_ref[s, hd:hd + 1, :] = m_new


def _mlstm(qm, km, vm, om, misc, gate_bias, mnorm, state, B, T, hdt):
    L = next((c for c in (ML, LANES) if T % c == 0), T if T % SUBLANES == 0 else LANES)
    nvalid = L if T % L == 0 else T
    assert nvalid <= L and T % nvalid == 0 and nvalid % SUBLANES == 0
    nc = T // nvalid
    nb = next(n for n in ((4, 2, 1) if L >= LANES and nvalid == L else (8, 4, 2, 1)) if B % n == 0)
    seq3 = lambda a: a.reshape(B, T, a.shape[-1])
    blk = lambda w: pl.BlockSpec((nb, nvalid, w), lambda b, c: (b, c, 0))
    st_specs = [pl.BlockSpec((nb, M_HEADS, M_HEAD_DIM, M_HEAD_DIM), lambda b, c: (b, 0, 0, 0)),
                pl.BlockSpec((nb, M_HEADS, M_HEAD_DIM), lambda b, c: (b, 0, 0)),
                pl.BlockSpec((nb, M_HEADS, LANES), lambda b, c: (b, 0, 0))]
    in_specs = [blk(M_WIDTH), blk(M_WIDTH), blk(M_WIDTH), blk(M_WIDTH),
                pl.BlockSpec((nb, T, LANES), lambda b, c: (b, 0, 0)),
                pl.BlockSpec((SUBLANES, LANES), lambda b, c: (0, 0)),
                pl.BlockSpec((1, M_WIDTH), lambda b, c: (0, 0))]
    args = [seq3(qm), seq3(km), seq3(vm), seq3(om), seq3(misc), gate_bias, mnorm]
    if state is not None:
        in_specs += st_specs
        args += list(state)
    h, c_new, n_new, m_new = pl.pallas_call(
        functools.partial(_mlstm_kernel, nvalid, state is not None),
        grid=(B // nb, nc),
        in_specs=in_specs,
        out_specs=[blk(M_WIDTH)] + st_specs,
        out_shape=[jax.ShapeDtypeStruct((B, T, M_WIDTH), hdt),
                   jax.ShapeDtypeStruct((B, M_HEADS, M_HEAD_DIM, M_HEAD_DIM), F32),
                   jax.ShapeDtypeStruct((B, M_HEADS, M_HEAD_DIM), F32),
                   jax.ShapeDtypeStruct((B, M_HEADS, LANES), F32)],
        scratch_shapes=[pltpu.VMEM((nb, nc, L, LANES), F32), pltpu.VMEM((nb, nc, SUBLANES, L), F32)],
        compiler_params=_cparams(2),
        name="mlstm",
    )(*args)
    return h.reshape(B * T, M_WIDTH), c_new, n_new, m_new


def _sortable_key(score):
    bits = lax.bitcast_convert_type(score, I32)
    return bits ^ (lax.shift_right_arithmetic(bits, 31) & 0x7FFFFFFF)


def _key_to_score(key):
    return lax.bitcast_convert_type(key ^ (lax.shift_right_arithmetic(key, 31) & 0x7FFFFFFF), F32)


def _build_bias_strip(strip_ref, rb_ref, off, key_axis, log2_relative=False):
    ntiles, tile = strip_ref.shape[1], strip_ref.shape[2:]
    i = lax.broadcasted_iota(I32, tile, 1 - key_axis)
    x = lax.broadcasted_iota(I32, tile, key_axis)

    def entry(b, h):
        if log2_relative:
            return (rb_ref[b, h] - rb_ref[N_BUCKETS - 1, h]) * LOG2E
        return rb_ref[b, h]

    for t in range(ntiles):
        dist = i + (off - LANES * t) - x
        for h in range(N_HEADS):
            val = jnp.full(tile, entry(0, h), F32)
            for b in range(1, N_BUCKETS):
                val = jnp.where(dist >= BUCKET_BOUNDS[b], entry(b, h), val)
            strip_ref[h, t] = val


def _counter(score_ref, nk, ck, key_axis):
    def reduce(fn, fold, fold_all, init):
        def body(c, acc):
            c0 = pl.multiple_of(c * ck, ck)
            sc = score_ref[:, pl.ds(c0, ck)] if key_axis == 1 else score_ref[pl.ds(c0, ck), :]
            val = fn(sc, c0 + lax.broadcasted_iota(I32, sc.shape, key_axis))
            if key_axis == 1:
                for j in range(ck // LANES):
                    acc = fold(acc, val[:, j * LANES:(j + 1) * LANES])
                return acc
            return fold(acc, fold_all(val.reshape(ck // SUBLANES, SUBLANES, val.shape[1]), axis=0))

        nq = score_ref.shape[1 - key_axis]
        acc0 = jnp.full((nq, LANES) if key_axis == 1 else (SUBLANES, nq), init, I32)
        return fold_all(lax.fori_loop(0, nk, body, acc0), axis=key_axis, keepdims=True)

    def count(pred):
        return reduce(lambda sc, idx: jnp.where(pred(sc, idx), 1, 0), jnp.add, jnp.sum, 0)

    def lowest(fn):
        return reduce(fn, jnp.minimum, jnp.min, jnp.iinfo(jnp.int32).max)

    return count, lowest


IMIN = jnp.iinfo(jnp.int32).min


def _kth_largest_by_count(count, qshape, n_sel):
    def bit_step(i, key):
        cand = key + lax.shift_left(jnp.int32(1), 31 - i)
        cand_score = _key_to_score(cand)
        return jnp.where(count(lambda sc, idx: sc >= cand_score) >= n_sel, cand, key)

    return _key_to_score(lax.fori_loop(0, 32, bit_step, jnp.full(qshape, IMIN, I32)))


def _bit_planes(words):
    a = list(words)
    j, m = 16, 0x0000FFFF
    while j:
        k = 0
        while k < 32:
            t = (a[k] ^ lax.shift_right_logical(a[k + j], j)) & m
            a[k] = a[k] ^ t
            a[k + j] = a[k + j] ^ lax.shift_left(t, j)
            k = (k + j + 1) & ~j
        j >>= 1
        m = (m ^ (m << j)) & 0xFFFFFFFF
        m = m - (1 << 32) if m >= (1 << 31) else m
    return a[::-1]


def _kth_largest_by_planes(score_ref, planes_ref, nk, ck, n_sel):
    nc = planes_ref.shape[1]
    nq = score_ref.shape[1]
    assert ck == 32 * SUBLANES

    def pack_chunk(c, _):
        c0 = pl.multiple_of(c * ck, ck)
        u = _sortable_key(score_ref[pl.ds(c0, ck), :]) ^ IMIN
        u = u.reshape(32, SUBLANES, nq)
        for b, plane in enumerate(_bit_planes([u[v] for v in range(32)])):
            planes_ref[b, c] = plane
        return 0

    lax.fori_loop(0, nk, pack_chunk, 0)
    cand0 = tuple(jnp.where(c < nk, jnp.full((SUBLANES, nq), -1, I32), 0) for c in range(nc))
    return _plane_search(planes_ref, cand0, n_sel, 0)


def _kth_largest_by_planes_lanes(score_ref, planes_ref, n_sel):
    nq, ntiles = score_ref.shape[0], score_ref.shape[1] // LANES
    cand0 = []
    for g in range(planes_ref.shape[1]):
        real = min(32, ntiles - 32 * g)
        words = [_sortable_key(score_ref[:, (32 * g + v) * LANES:(32 * g + v + 1) * LANES]) ^ IMIN
                 if v < real else jnp.zeros((nq, LANES), I32) for v in range(32)]
        for b, plane in enumerate(_bit_planes(words)):
            planes_ref[b, g] = plane
        cand0.append(jnp.full((nq, LANES), -(1 << (32 - real)), I32))
    return _plane_search(planes_ref, tuple(cand0), n_sel, 1)


def _plane_search(planes_ref, cand0, n_sel, key_axis):
    def bit_step(i, carry):
        cand, n_above, thr_u = carry
        b = 31 - i
        ones = [m & planes_ref[b, c] for c, m in enumerate(cand)]
        pop = lax.population_count(ones[0])
        for o in ones[1:]:
            pop = pop + lax.population_count(o)
        tot = jnp.sum(pop, axis=key_axis, keepdims=True)
        take = n_above + tot >= n_sel
        cand = tuple(jnp.where(take, o, m ^ o) for o, m in zip(ones, cand))
        n_above = jnp.where(take, n_above, n_above + tot)
        thr_u = thr_u | jnp.where(take, lax.shift_left(jnp.int32(1), b), 0)
        return cand, n_above, thr_u

    zero = jnp.zeros_like(jnp.sum(cand0[0], axis=key_axis, keepdims=True))
    _, _, thr_u = lax.fori_loop(0, 32, bit_step, (cand0, zero, zero))
    return _key_to_score(thr_u ^ IMIN)


def _select_topk(thr, check, reducers, qshape, n_sel, idx_bits):
    count, lowest = reducers
    imax = jnp.iinfo(jnp.int32).max

    def with_counts(t):
        return t, count(lambda sc, idx: sc > t), count(lambda sc, idx: sc >= t)

    thr, n_gt, n_ge = with_counts(thr)
    if check:
        good = jnp.min(jnp.where(n_gt < n_sel, jnp.where(n_ge >= n_sel, 1, 0), 0)) > 0
        thr, n_gt, n_ge = lax.cond(
            good, lambda _: (thr, n_gt, n_ge),
            lambda _: with_counts(_kth_largest_by_count(count, qshape, n_sel)), 0)
    split = n_ge > n_sel
    need = jnp.where(split, n_sel - n_gt, 0)
    most = jnp.max(need)

    def by_extraction(_):
        def step(k, last):
            nxt = lowest(lambda sc, idx: jnp.where(sc == thr, jnp.where(idx > last, idx, imax), imax))
            return jnp.where(k < need, nxt, last)

        return lax.fori_loop(0, most, step, jnp.full(qshape, -1, I32))

    def by_bisection(_):
        def idx_step(i, lo):
            cand = lo + lax.shift_left(jnp.int32(1), idx_bits - 1 - i)
            cnt = count(lambda sc, idx: jnp.where(sc == thr, idx, imax) < cand)
            return jnp.where(cnt < need, cand, lo)

        return lax.fori_loop(0, idx_bits, idx_step, jnp.zeros(qshape, I32))

    jstar = lax.cond(most <= idx_bits, by_extraction, by_bisection, 0)
    return thr, jnp.where(split, jstar, imax)


def _valid_mask(scores, idx, thr, jstar, qpos):
    sel = jnp.where(scores > thr, 1, jnp.where(scores == thr, jnp.where(idx <= jstar, 1, 0), 0))
    return jnp.where(idx <= qpos, sel, 0) > 0


def _group_queries(qa, tq):
    lane = lax.broadcasted_iota(I32, (tq, LANES), 1)
    out = []
    for n in range(N_KV_HEADS):
        keep = (lane < HEAD_DIM) if n == 0 else (lane >= HEAD_DIM)
        tiles = [jnp.where(keep, qa[:, j * LANES:(j + 1) * LANES], jnp.zeros((), qa.dtype))
                 for j in range(GROUP)]
        out.append(jnp.concatenate(tiles, axis=0).astype(BF16))
    return out


def _write_attn(out_ref, carries, tq):
    lane = lax.broadcasted_iota(I32, (tq, LANES), 1)
    res = [acc / l for (_, l, acc) in carries]
    for j in range(GROUP):
        tile = jnp.where(lane < HEAD_DIM, res[0][j * tq:(j + 1) * tq], res[1][j * tq:(j + 1) * tq])
        out_ref[:, j * LANES:(j + 1) * LANES] = tile.astype(out_ref.dtype)


def _dsa_prompt_kernel(n_sel, idx_bits, rb_ref, qat_ref, qit_ref, misct_ref, ki_ref, k_ref, vt_ref,
                       out_ref, score_ref, planes_ref, strip_ref, sa_ref, sb_ref, m_ref, acc_ref):
    qb = pl.program_id(1)
    q0 = qb * TQ
    nk = (q0 + TQ + CKP - 1) // CKP
    tiles_per_chunk = CKP // LANES
    back_tiles = STRIP_BACK
    strip_off = LANES * back_tiles

    @pl.when((pl.program_id(0) == 0) & (qb == 0))
    def _():
        _build_bias_strip(strip_ref, rb_ref, strip_off, 0, log2_relative=True)
        planes_ref[...] = jnp.zeros(planes_ref.shape, I32)

    qpos = q0 + lax.broadcasted_iota(I32, (1, TQ), 1)

    qit = qit_ref[0]
    qstack = jnp.concatenate([qit[h * IDX_DIM:(h + 1) * IDX_DIM] for h in range(N_IDX_HEADS)],
                             axis=1)
    w = misct_ref[0, MISC_W:MISC_W + N_IDX_HEADS, :] * (N_IDX_HEADS ** -0.5 * IDX_DIM ** -0.5)

    def score_chunk(c, _):
        c0 = pl.multiple_of(c * CKP, CKP)
        d = jnp.dot(ki_ref[pl.ds(c0, CKP), :], qstack, preferred_element_type=F32)
        d = jnp.maximum(d, 0.0)
        s = jnp.zeros((CKP, TQ), F32)
        for h in range(N_IDX_HEADS):
            s = s + d[:, h * TQ:(h + 1) * TQ] * w[h:h + 1, :]
        idx = c0 + lax.broadcasted_iota(I32, (CKP, TQ), 0)
        score_ref[pl.ds(c0, CKP), :] = jnp.where(idx <= qpos, s, NEG)
        return 0

    lax.fori_loop(0, nk, score_chunk, 0)

    def search(_):
        proposal = _kth_largest_by_planes(score_ref, planes_ref, nk, CKP, n_sel)
        return _select_topk(proposal, True, _counter(score_ref, nk, CKP, 0), (1, TQ), n_sel, idx_bits)

    def everything(_):
        return jnp.full((1, TQ), NEG, F32), jnp.full((1, TQ), jnp.iinfo(jnp.int32).max, I32)

    thr, jstar = lax.cond(q0 + TQ <= n_sel, everything, search, 0)

    qat = qat_ref[0]
    zeros = jnp.zeros((HEAD_DIM, TQ), qat.dtype)
    qgroups = []
    for n in range(N_KV_HEADS):
        tiles = []
        for g in range(GROUP):
            h = n * GROUP + g
            x = qat[h * HEAD_DIM:(h + 1) * HEAD_DIM]
            tiles.append(jnp.concatenate([x, zeros] if n == 0 else [zeros, x], axis=0))
        qgroups.append(jnp.concatenate(tiles, axis=1))

    def logits_into(s_ref, c):
        kc = k_ref[pl.ds(pl.multiple_of(c * CKP, CKP), CKP), :]
        for n in range(N_KV_HEADS):
            s_ref[n] = jnp.dot(kc, qgroups[n], preferred_element_type=F32)

    def attend_chunk(c, s_cur_ref, s_next_ref):
        if s_next_ref is not None:
            logits_into(s_next_ref, jnp.minimum(c + 1, nk - 1))
        c0 = pl.multiple_of(c * CKP, CKP)
        idx = c0 + lax.broadcasted_iota(I32, (CKP, TQ), 0)
        valid = _valid_mask(score_ref[pl.ds(c0, CKP), :], idx, thr, jstar, qpos)
        vct = vt_ref[0, c]
        tiles = [jnp.maximum(back_tiles + j - (qb * (TQ // LANES) - c * tiles_per_chunk), 0)
                 for j in range(tiles_per_chunk)]
        for n in range(N_KV_HEADS):
            m_old = m_ref[n]
            parts = []
            for g in range(GROUP):
                bias = jnp.concatenate([strip_ref[n * GROUP + g, t] for t in tiles], axis=0)
                parts.append(jnp.where(valid, s_cur_ref[n, :, g * TQ:(g + 1) * TQ] + bias, NEG))
            sm = jnp.concatenate(parts, axis=1)
            m_new = jnp.maximum(m_old, jnp.max(sm, axis=0, keepdims=True))
            p = jnp.exp2((sm - m_new).astype(BF16))
            acc_ref[n] = jnp.exp2(m_old - m_new) * acc_ref[n] + jnp.dot(vct, p, preferred_element_type=F32)
            m_ref[n] = m_new

    m_ref[...] = jnp.full(m_ref.shape, NEG, F32)
    acc_ref[...] = jnp.zeros(acc_ref.shape, F32)
    logits_into(sa_ref, 0)

    def attend_pair(i, _):
        attend_chunk(2 * i, sa_ref, sb_ref)
        attend_chunk(2 * i + 1, sb_ref, sa_ref)
        return 0

    lax.fori_loop(0, nk // 2, attend_pair, 0)

    @pl.when(nk % 2 == 1)
    def _():
        attend_chunk(nk - 1, sa_ref, None)

    carries = [(None, acc_ref[n]) for n in range(N_KV_HEADS)]
    res = [acc[:LANES] / acc[LANES:LANES + 1] for (_, acc) in carries]
    row = lax.broadcasted_iota(I32, (LANES, TQ), 0)
    for j in range(GROUP):
        cols = slice(j * TQ, (j + 1) * TQ)
        tile_t = jnp.where(row < HEAD_DIM, res[0][:, cols], res[1][:, cols])
        out_ref[:, j * LANES:(j + 1) * LANES] = tile_t.T.astype(out_ref.dtype)


def _dsa_prompt(rel_bias, qat, qit, misct, kib, kb, vtb, B, T):
    assert T % CKP == 0 and T % TQ == 0
    nq = T // TQ
    n_sel = min(TOPK_MAX, T // 4)
    qcols = lambda w: pl.BlockSpec((1, w, TQ), lambda b, q: (b, 0, q))
    seq = lambda w: pl.BlockSpec((T, w), lambda b, q: (b, 0))
    return pl.pallas_call(
        functools.partial(_dsa_prompt_kernel, n_sel, max(1, (T - 1).bit_length())),
        grid=(B, nq),
        in_specs=[pl.BlockSpec(memory_space=pltpu.SMEM), qcols(ATT_WIDTH), qcols(N_IDX_HEADS * IDX_DIM),
                  qcols(LANES), seq(IDX_DIM), seq(LANES),
                  pl.BlockSpec((1, T // CKP, VT_ROWS, CKP), lambda b, q: (b, 0, 0, 0))],
        out_specs=pl.BlockSpec((TQ, ATT_WIDTH), lambda b, q: (b * nq + q, 0)),
        out_shape=jax.ShapeDtypeStruct((B * T, ATT_WIDTH), BF16),
        scratch_shapes=[pltpu.VMEM((T, TQ), F32),
                        pltpu.VMEM((32, T // CKP, SUBLANES, TQ), I32),
                        pltpu.VMEM((N_HEADS, STRIP_BACK + max(TQ, CKP) // LANES, LANES, TQ), F32),
                        pltpu.VMEM((N_KV_HEADS, CKP, GROUP * TQ), F32),
                        pltpu.VMEM((N_KV_HEADS, CKP, GROUP * TQ), F32),
                        pltpu.VMEM((N_KV_HEADS, 1, GROUP * TQ), F32),
                        pltpu.VMEM((N_KV_HEADS, VT_ROWS, GROUP * TQ), F32)],
        compiler_params=_cparams(2),
        name="dsa_prompt",
    )(rel_bias, qat, qit, misct, kib, kb, vtb)


def _page_pipeline(pt_ref, n_pages, caches, bufs, sems):
    def copies(bb, sl, j):
        pid = pt_ref[bb, j]
        cols = pl.ds(j * PAGE_SIZE, PAGE_SIZE)
        return [pltpu.make_async_copy(c.at[pid], buf.at[sl, :, cols], sems.at[sl, i])
                for i, (c, buf) in enumerate(zip(caches, bufs))]

    def start_all(bb, sl):
        for j in range(n_pages):
            for cp in copies(bb, sl, j):
                cp.start()

    def wait_all(bb, sl):
        for j in range(n_pages):
            for cp in copies(bb, sl, j):
                cp.wait()

    def step():
        b = pl.program_id(0)
        slot = b % 2

        @pl.when(b == 0)
        def _():
            start_all(0, 0)

        @pl.when(b + 1 < pl.num_programs(0))
        def _():
            start_all(b + 1, 1 - slot)

        wait_all(b, slot)
        return slot

    return step


def _pad_rows(x, rows):
    return jnp.concatenate([x, jnp.zeros((rows - x.shape[0], x.shape[1]), x.dtype)], axis=0)


def _sample_score_kernel(n_sel, idx_bits, n_pages, ts, ck, pt_ref, qi_ref, misc_ref, kin_ref,
                         ckit_hbm, score_ref, thr_ref, jst_ref, ki_buf, sems, planes_ref):
    b = pl.program_id(0)
    past = n_pages * PAGE_SIZE
    slot = _page_pipeline(pt_ref, n_pages, [ckit_hbm], [ki_buf], sems)()

    qi = qi_ref[...]
    qstack = jnp.concatenate([qi[:, h * IDX_DIM:(h + 1) * IDX_DIM] for h in range(N_IDX_HEADS)],
                             axis=0).astype(BF16)
    w = misc_ref[:, MISC_W:MISC_W + N_IDX_HEADS] * (N_IDX_HEADS ** -0.5 * IDX_DIM ** -0.5)
    d_past = jnp.dot(qstack, ki_buf[slot].astype(BF16), preferred_element_type=F32)
    d_own = lax.dot_general(qstack, _pad_rows(kin_ref[...], PAGE_SIZE).astype(BF16), NT_DIMS,
                            preferred_element_type=F32)
    d = jnp.maximum(jnp.concatenate([d_past, d_own], axis=1), 0.0)
    s = jnp.zeros((ts, past + PAGE_SIZE), F32)
    for h in range(N_IDX_HEADS):
        s = s + d[h * ts:(h + 1) * ts] * w[:, h:h + 1]
    idx = lax.broadcasted_iota(I32, s.shape, 1)
    qpos = past + lax.broadcasted_iota(I32, (ts, 1), 0)
    score_ref[pl.ds(pl.multiple_of(b * ts, ts), ts), :] = jnp.where(idx <= qpos, s, NEG)

    @pl.when(b == pl.num_programs(0) - 1)
    def _():
        reducers = _counter(score_ref, score_ref.shape[1] // ck, ck, 1)
        qshape = (score_ref.shape[0], 1)
        thr = _kth_largest_by_planes_lanes(score_ref, planes_ref, n_sel)
        thr, jstar = _select_topk(thr, True, reducers, qshape, n_sel, idx_bits)
        thr_ref[...] = jnp.broadcast_to(thr, thr_ref.shape)
        jst_ref[...] = jnp.broadcast_to(jstar, jst_ref.shape)


def _sample_attend_kernel(n_pages, ts, pt_ref, rb_ref, qa_ref, score_ref, thr_ref, jst_ref, kn_ref,
                          vn_ref, ckt_hbm, cvt_hbm, out_ref, k_buf, v_buf, sems, strip_ref):
    b = pl.program_id(0)
    past = n_pages * PAGE_SIZE
    rows = N_HEADS * ts

    @pl.when(b == 0)
    def _():
        _build_bias_strip(strip_ref, rb_ref, LANES * (strip_ref.shape[1] - 1), 1)

    slot = _page_pipeline(pt_ref, n_pages, [ckt_hbm, cvt_hbm], [k_buf, v_buf], sems)()

    q2 = jnp.concatenate(_group_queries(qa_ref[...], ts), axis=0)
    k_own = _pad_rows(kn_ref[...], PAGE_SIZE).astype(BF16)
    v_own = _pad_rows(vn_ref[...], PAGE_SIZE).astype(BF16)
    s_past = jnp.dot(q2, k_buf[slot].astype(BF16), preferred_element_type=F32)
    s_own = lax.dot_general(q2, k_own, NT_DIMS, preferred_element_type=F32)
    far = strip_ref[:, 0].reshape(rows, LANES)[:, 0:1]
    near = [strip_ref[:, t].reshape(rows, LANES) for t in (1, 2)]
    s = jnp.concatenate([s_past[:, :past - PAGE_SIZE] + far, s_past[:, past - PAGE_SIZE:] + near[0],
                         s_own + near[1]], axis=1)

    qpos = past + lax.broadcasted_iota(I32, (ts, 1), 0)
    scores = score_ref[...]
    valid = _valid_mask(scores, lax.broadcasted_iota(I32, scores.shape, 1), thr_ref[:, 0:1],
                        jst_ref[:, 0:1], qpos)
    s = jnp.where(valid[None], s.reshape(N_HEADS, ts, past + PAGE_SIZE), NEG).reshape(rows, -1)
    m = jnp.max(s, axis=1, keepdims=True)
    p = jnp.exp(s - m)
    l = jnp.sum(p, axis=1, keepdims=True)
    pb = p.astype(BF16)
    pv = lax.dot_general(pb[:, :past], v_buf[slot].astype(BF16), NT_DIMS, preferred_element_type=F32)
    pv = pv + jnp.dot(pb[:, past:], v_own, preferred_element_type=F32)
    half = GROUP * ts
    carries = [(None, l[n * half:(n + 1) * half], pv[n * half:(n + 1) * half]) for n in range(N_KV_HEADS)]
    _write_attn(out_ref, carries, ts)


def _dsa_sample(page_table, rel_bias, qa, qi, misc, ki_new, k_new, v_new, ckit, ckt, cvt, DB, ts):
    n_pages = page_table.shape[1]
    past = n_pages * PAGE_SIZE
    n_sel = min(TOPK_MAX, (past + ts) // 4)
    lpad = past + PAGE_SIZE
    idx_bits = max(1, (lpad - 1).bit_length())
    ck = LANES * math.gcd(lpad // LANES, 5)
    assert ts % SUBLANES == 0 and ts <= PAGE_SIZE and n_pages >= 1
    blk = lambda w: pl.BlockSpec((ts, w), lambda b, pt: (b, 0))
    whole = lambda w: pl.BlockSpec((DB * ts, w), lambda b, pt: (0, 0))
    hbm = pl.BlockSpec(memory_space=pl.ANY)
    keys, thr, jstar = pl.pallas_call(
        functools.partial(_sample_score_kernel, n_sel, idx_bits, n_pages, ts, ck),
        grid_spec=pltpu.PrefetchScalarGridSpec(
            num_scalar_prefetch=1,
            grid=(DB,),
            in_specs=[blk(N_IDX_HEADS * IDX_DIM), blk(LANES), blk(IDX_DIM), hbm],
            out_specs=[whole(lpad), whole(LANES), whole(LANES)],
            scratch_shapes=[pltpu.VMEM((2, IDX_DIM, past), F32), pltpu.SemaphoreType.DMA((2, 1)),
                            pltpu.VMEM((32, -(-lpad // (32 * LANES)), DB * ts, LANES), I32)]),
        out_shape=[jax.ShapeDtypeStruct((DB * ts, lpad), F32),
                   jax.ShapeDtypeStruct((DB * ts, LANES), F32),
                   jax.ShapeDtypeStruct((DB * ts, LANES), I32)],
        compiler_params=_cparams(1),
        name="sample_score",
    )(page_table, qi, misc, ki_new, ckit)
    return pl.pallas_call(
        functools.partial(_sample_attend_kernel, n_pages, ts),
        grid_spec=pltpu.PrefetchScalarGridSpec(
            num_scalar_prefetch=1,
            grid=(DB,),
            in_specs=[pl.BlockSpec(memory_space=pltpu.SMEM), blk(ATT_WIDTH), blk(lpad), blk(LANES),
                      blk(LANES), blk(LANES), blk(LANES), hbm, hbm],
            out_specs=blk(ATT_WIDTH),
            scratch_shapes=[pltpu.VMEM((2, LANES, past), F32), pltpu.VMEM((2, LANES, past), F32),
                            pltpu.SemaphoreType.DMA((2, 2)),
                            pltpu.VMEM((N_HEADS, 3, ts, LANES), F32)]),
        out_shape=jax.ShapeDtypeStruct((DB * ts, ATT_WIDTH), F32),
        compiler_params=_cparams(1),
        name="sample_attend",
    )(page_table, rel_bias, qa, keys, thr, jstar, k_new, v_new, ckt, cvt)


def _pack_layer_weights(w_in, b_i, b_f, w_out, w_up, w_down):
    D = w_in.shape[0]
    sizes = (ATT_WIDTH, N_KV_HEADS * HEAD_DIM, N_KV_HEADS * HEAD_DIM, N_IDX_HEADS * IDX_DIM, IDX_DIM,
             N_IDX_HEADS, M_WIDTH, M_WIDTH, M_WIDTH, M_WIDTH, M_HEADS, M_HEADS)
    assert w_in.shape[1] == sum(sizes)
    pts = np.cumsum((0,) + sizes)
    w_t = w_in.T.astype(BF16)
    qa, k, v, qi, ki, wi, qm, km, vm, om, im, fm = [w_t[pts[i]:pts[i + 1]] for i in range(len(sizes))]
    perm = np.asarray(HEAD_PERM)
    qa_perm = qa.reshape(N_HEADS, HEAD_DIM, D)[perm].reshape(ATT_WIDTH, D)
    misc = jnp.concatenate([wi, im, fm, jnp.zeros((LANES - N_IDX_HEADS - 2 * M_HEADS, D), BF16)], axis=0)
    wp = jnp.concatenate([qa_perm, v, qi, k, ki, ki, misc, qm, km, vm, om], axis=0).T
    assert wp.shape[1] == N_PACK
    gate_bias = jnp.broadcast_to(jnp.concatenate([b_i, b_f]).astype(F32)[:, None], (2 * M_HEADS, LANES))
    woa = w_out[:ATT_WIDTH].reshape(N_HEADS, HEAD_DIM, -1)[perm].reshape(ATT_WIDTH, -1).astype(BF16)
    woh = w_out[ATT_WIDTH:].astype(BF16)
    wt = jnp.concatenate([k, v, ki, jnp.zeros((LANES - IDX_DIM, D), BF16), qa, qi, misc], axis=0)
    assert wt.shape[0] == N_TPACK
    return wp, wt, gate_bias, woa, woh, w_up.astype(BF16), w_down.astype(BF16)


def _layer(x, packed, g1, g2, mnorm, rel_bias, gf, final_norm, past):
    wp, wt, gate_bias, woa, woh, wup, wdn = packed
    B, T, D = x.shape
    x2 = x.reshape(B * T, D)
    tm = math.gcd(T if past is None else B * T, 512)
    kv_w = N_KV_HEADS * HEAD_DIM
    if past is None:
        (misc, qm, km, vm, om, kb, kib, kt, vt, kit, vtb, qat, qit, misct) = _inproj(
            x2, g1.reshape(1, D), wp, wt, B, T, tm)
        attn = _dsa_prompt(rel_bias, qat, qit, misct, kib, kb, vtb, B, T)
        state = None
        k_new = kt.reshape(B, N_KV_HEADS, HEAD_DIM, T).transpose(0, 3, 1, 2)
        v_new = vt.reshape(B, N_KV_HEADS, HEAD_DIM, T).transpose(0, 3, 1, 2)
        ki_new = kit.transpose(0, 2, 1)
    else:
        (misc, qm, km, vm, om, qa, qi, k, v, ki) = _inproj(x2, g1.reshape(1, D), wp, None, B, T, tm)
        page_table, cache_k, cache_v, cache_kidx, c0, n0, m0 = past
        n_pool = cache_k.shape[0]
        ckt = cache_k.transpose(0, 2, 3, 1).reshape(n_pool, kv_w, PAGE_SIZE)
        cvt = cache_v.transpose(0, 2, 3, 1).reshape(n_pool, kv_w, PAGE_SIZE)
        ckit = cache_kidx.transpose(0, 2, 1)
        attn = _dsa_sample(page_table, rel_bias, qa, qi, misc, ki, k, v, ckit, ckt, cvt, B, T)
        state = (c0, n0, jnp.broadcast_to(m0[..., None], m0.shape + (LANES,)))
        k_new = k.reshape(B, T, N_KV_HEADS, HEAD_DIM)
        v_new = v.reshape(B, T, N_KV_HEADS, HEAD_DIM)
        ki_new = ki.reshape(B, T, IDX_DIM)
    h, c_new, n_new, m_new = _mlstm(qm, km, vm, om, misc, gate_bias, mnorm.reshape(1, M_WIDTH), state,
                                    B, T, BF16 if past is None else F32)
    y = _post(x2, attn, h, woa, woh, g2.reshape(1, D), wup, wdn, gf.reshape(1, D), final_norm, tm)
    return (y.reshape(B, T, D), k_new, v_new, ki_new, c_new, n_new, m_new[..., 0])


def kernel(x_prompt, x_sample, cache_k, cache_v, cache_kidx, page_table, state_C, state_n, state_m,
           w_in, b_igate, b_fgate, mlstm_norm, rel_bias, w_out, norm1, norm2, w_up, w_down, norm_f):
    depth = w_in.shape[0]
    xp, xs = x_prompt, x_sample
    outs_p, outs_s = [], []
    for l in range(depth):
        packed = _pack_layer_weights(w_in[l], b_igate[l], b_fgate[l], w_out[l], w_up[l], w_down[l])
        last = l == depth - 1
        common = (packed, norm1[l], norm2[l], mlstm_norm[l], rel_bias, norm_f, last)
        rp = _layer(xp, *common, None)
        rs = _layer(xs, *common, (page_table, cache_k[l], cache_v[l], cache_kidx[l],
                                  state_C[l], state_n[l], state_m[l]))
        xp, xs = rp[0], rs[0]
        outs_p.append(rp[1:])
        outs_s.append(rs[1:])
    stack = lambda outs, i: jnp.stack([o[i] for o in outs])
    return ((xp, xs) + tuple(stack(outs_p, i) for i in range(6))
            + tuple(stack(outs_s, i) for i in range(6)))
```

```python
import functools
import math

import numpy as np
import jax
import jax.numpy as jnp
from jax import lax
from jax.experimental import pallas as pl
from jax.experimental.pallas import tpu as pltpu

F32 = jnp.float32
BF16 = jnp.bfloat16
I32 = jnp.int32

N_HEADS = 8
HEAD_DIM = 64
N_KV_HEADS = 2
GROUP = N_HEADS // N_KV_HEADS
N_IDX_HEADS = 8
IDX_DIM = 64
TOPK_MAX = 256
N_BUCKETS = 32
MAX_DISTANCE = 128
M_HEADS = 4
M_HEAD_DIM = 128
PAGE_SIZE = 128
EPS = 1e-6
NEG = -1e30
LOG2E = math.log2(math.e)
ATT_WIDTH = N_HEADS * HEAD_DIM
M_WIDTH = M_HEADS * M_HEAD_DIM

LANES = 128
SUBLANES = 8
VMEM_LIMIT = 56 * 1024 * 1024

C_QA = 0
C_V = C_QA + ATT_WIDTH
C_QI = C_V + LANES
C_K = C_QI + N_IDX_HEADS * IDX_DIM
C_KI2 = C_K + LANES
C_MISC = C_KI2 + LANES
C_QM = C_MISC + LANES
C_KM = C_QM + M_WIDTH
C_VM = C_KM + M_WIDTH
C_OM = C_VM + M_WIDTH
N_PACK = C_OM + M_WIDTH
MISC_W = 0
MISC_I = 8
MISC_F = 12

HEAD_PERM = (0, 4, 1, 5, 2, 6, 3, 7)

TQ = 256
STRIP_BACK = 2
assert LANES * (STRIP_BACK - 1) >= MAX_DISTANCE
CKP = 256
VT_ROWS = LANES + 16
ML = 256
KV_SLOTS = 3

NT_DIMS = (((1,), (1,)), ((), ()))
TN_DIMS = (((0,), (0,)), ((), ()))


def _bucket_bounds():
    max_exact = N_BUCKETS // 2
    scale = (N_BUCKETS - max_exact) / math.log(MAX_DISTANCE / max_exact)

    def bucket(n, dt):
        if n < max_exact:
            return n
        val = np.log(np.asarray(max(n, 1), dt) / dt(max_exact)) * dt(scale)
        return min(max_exact + int(val), N_BUCKETS - 1)

    table = [bucket(n, np.float32) for n in range(MAX_DISTANCE + 2)]
    assert table == [bucket(n, np.float64) for n in range(MAX_DISTANCE + 2)]
    assert table[MAX_DISTANCE] == N_BUCKETS - 1
    return [next(d for d, b in enumerate(table) if b >= k) for k in range(N_BUCKETS)]


BUCKET_BOUNDS = _bucket_bounds()


def _cparams(n_axes):
    return pltpu.CompilerParams(dimension_semantics=("arbitrary",) * n_axes,
                                vmem_limit_bytes=VMEM_LIMIT)


def _const_spec(shape):
    nd = len(shape)
    return pl.BlockSpec(shape, lambda *_: (0,) * nd, pipeline_mode=pl.Buffered(1))


def _rms(x, g):
    return x * lax.rsqrt(jnp.mean(x * x, axis=-1, keepdims=True) + EPS) * g


def _inproj_mlstm(mm, misc_ref, qm_ref, km_ref, vm_ref, om_ref):
    misc_ref[...] = mm(C_MISC, LANES)
    qm_ref[...] = mm(C_QM, M_WIDTH).astype(qm_ref.dtype)
    km_ref[...] = (mm(C_KM, M_WIDTH) * (M_HEAD_DIM ** -0.5)).astype(km_ref.dtype)
    vm_ref[...] = mm(C_VM, M_WIDTH).astype(vm_ref.dtype)
    om_ref[...] = mm(C_OM, M_WIDTH)


def _inproj_rows_kernel(x_ref, g_ref, w_ref, misc_ref, qm_ref, km_ref, vm_ref, om_ref,
                        qa_ref, qi_ref, k_ref, v_ref, ki_ref):
    ub = _rms(x_ref[...], g_ref[...]).astype(BF16)
    z = jnp.dot(ub, w_ref[...], preferred_element_type=F32)
    mm = lambda c0, n: z[:, c0:c0 + n]
    _inproj_mlstm(mm, misc_ref, qm_ref, km_ref, vm_ref, om_ref)
    qa_ref[...] = mm(C_QA, ATT_WIDTH) * (HEAD_DIM ** -0.5)
    qi_ref[...] = mm(C_QI, N_IDX_HEADS * IDX_DIM)
    k_ref[...] = mm(C_K, LANES)
    v_ref[...] = mm(C_V, LANES)
    ki_ref[...] = mm(C_KI2, LANES)[:, :IDX_DIM]


R_K = 0
R_V = R_K + LANES
R_KI = R_V + LANES
R_QA = R_KI + LANES
R_QI = R_QA + ATT_WIDTH
R_MISC = R_QI + N_IDX_HEADS * IDX_DIM
N_TPACK = R_MISC + LANES


def _inproj_cols_kernel(x_ref, g_ref, w_ref, wt_ref, misc_ref, qm_ref, km_ref, vm_ref, om_ref,
                        kb_ref, kib_ref, kt_ref, vt_ref, kit_ref, vtb_ref, qat_ref, qit_ref,
                        misct_ref):
    ub = _rms(x_ref[...], g_ref[...]).astype(BF16)
    z = jnp.dot(ub, w_ref[:, C_K:], preferred_element_type=F32)
    mm = lambda c0, n: z[:, c0 - C_K:c0 - C_K + n]
    _inproj_mlstm(mm, misc_ref, qm_ref, km_ref, vm_ref, om_ref)
    kb_ref[...] = mm(C_K, LANES).astype(BF16)
    kib_ref[...] = mm(C_KI2, LANES)[:, :IDX_DIM].astype(BF16)

    zt = lax.dot_general(wt_ref[...], ub, NT_DIMS, preferred_element_type=F32)
    mt = lambda r0, n: zt[r0:r0 + n]

    kt_ref[0] = mt(R_K, LANES)
    vt = mt(R_V, LANES)
    vt_ref[0] = vt
    ones = jnp.ones((VT_ROWS - LANES, CKP), BF16)
    for j in range(vtb_ref.shape[1]):
        vtb_ref[0, j] = jnp.concatenate([vt[:, j * CKP:(j + 1) * CKP].astype(BF16), ones], axis=0)
    kit_ref[0] = mt(R_KI, IDX_DIM)
    qat_ref[0] = (mt(R_QA, ATT_WIDTH) * (HEAD_DIM ** -0.5 * LOG2E)).astype(BF16)
    qit_ref[0] = mt(R_QI, N_IDX_HEADS * IDX_DIM).astype(BF16)
    misct_ref[0] = mt(R_MISC, LANES)


def _inproj(x2, g1, wp, wt, B, T, tm):
    R, D = x2.shape
    assert R == B * T and R % tm == 0
    mdt = F32 if wt is None else BF16
    row = lambda i: (i, 0)
    outs = [(LANES, F32), (M_WIDTH, mdt), (M_WIDTH, mdt), (M_WIDTH, mdt), (M_WIDTH, F32)]
    in_specs = [pl.BlockSpec((tm, D), row), _const_spec((1, D)), _const_spec((D, N_PACK))]
    if wt is None:
        kern, args = _inproj_rows_kernel, (x2, g1, wp)
        outs += [(ATT_WIDTH, F32), (N_IDX_HEADS * IDX_DIM, F32), (LANES, F32), (LANES, F32), (IDX_DIM, F32)]
    else:
        kern, args = _inproj_cols_kernel, (x2, g1, wp, wt)
        in_specs.append(_const_spec(wt.shape))
        outs += [(LANES, BF16), (IDX_DIM, BF16)]
    out_specs = [pl.BlockSpec((tm, w), row) for w, _ in outs]
    out_shape = [jax.ShapeDtypeStruct((R, w), dt) for w, dt in outs]
    if wt is not None:
        assert T % tm == 0 and tm % CKP == 0
        tpb, cpt = T // tm, tm // CKP
        cols = lambda i: (i // tpb, 0, i % tpb)
        for w, dt in ((LANES, F32), (LANES, F32), (IDX_DIM, F32)):
            out_specs.append(pl.BlockSpec((1, w, tm), cols))
            out_shape.append(jax.ShapeDtypeStruct((B, w, T), dt))
        out_specs.append(pl.BlockSpec((1, cpt, VT_ROWS, CKP), lambda i: (i // tpb, i % tpb, 0, 0)))
        out_shape.append(jax.ShapeDtypeStruct((B, T // CKP, VT_ROWS, CKP), BF16))
        for w, dt in ((ATT_WIDTH, BF16), (N_IDX_HEADS * IDX_DIM, BF16), (LANES, F32)):
            out_specs.append(pl.BlockSpec((1, w, tm), cols))
            out_shape.append(jax.ShapeDtypeStruct((B, w, T), dt))
    return pl.pallas_call(
        kern,
        grid=(R // tm,),
        in_specs=in_specs,
        out_specs=out_specs,
        out_shape=out_shape,
        compiler_params=_cparams(1),
        name="inproj",
    )(*args)


def _post_kernel(ff_chunk, final_norm, x_ref, a_ref, h_ref, woa_ref, woh_ref, g2_ref, wup_ref,
                 wdn_ref, gf_ref, y_ref):
    mix = jnp.dot(a_ref[...].astype(BF16), woa_ref[...], preferred_element_type=F32)
    mix = mix + jnp.dot(h_ref[...].astype(BF16), woh_ref[...], preferred_element_type=F32)
    hres = x_ref[...] + mix
    f = _rms(hres, g2_ref[...]).astype(BF16)
    acc = hres
    for c0 in range(0, wup_ref.shape[1], ff_chunk):
        up = jnp.dot(f, wup_ref[:, c0:c0 + ff_chunk], preferred_element_type=F32)
        r = jnp.maximum(up, 0.0)
        acc = acc + jnp.dot((r * r).astype(BF16), wdn_ref[c0:c0 + ff_chunk, :],
                            preferred_element_type=F32)
    y_ref[...] = _rms(acc, gf_ref[...]) if final_norm else acc


def _post(x2, attn, h, woa, woh, g2, wup, wdn, gf, final_norm, tm):
    R, D = x2.shape
    dff = wup.shape[1]
    assert R % tm == 0
    row = lambda i: (i, 0)
    return pl.pallas_call(
        functools.partial(_post_kernel, min(dff, 1024), final_norm),
        grid=(R // tm,),
        in_specs=[pl.BlockSpec((tm, D), row), pl.BlockSpec((tm, ATT_WIDTH), row),
                  pl.BlockSpec((tm, M_WIDTH), row), _const_spec(woa.shape), _const_spec(woh.shape),
                  _const_spec((1, D)), _const_spec(wup.shape), _const_spec(wdn.shape),
                  _const_spec((1, D))],
        out_specs=pl.BlockSpec((tm, D), row),
        out_shape=jax.ShapeDtypeStruct((R, D), F32),
        compiler_params=_cparams(1),
        name="post",
    )(x2, attn, h, woa, woh, g2, wup, wdn, gf)


def _log_sigmoid(x):
    return -(jnp.maximum(-x, 0.0) + jnp.log1p(jnp.exp(-jnp.abs(x))))


def _mlstm_kernel(nvalid, has_state, *refs):
    if has_state:
        (q_ref, k_ref, v_ref, o_ref, misc_ref, gb_ref, mn_ref, c0_ref, n0_ref, m0_ref,
         h_ref, c_ref, n_ref, m_ref, z_scr, rows_scr) = refs
    else:
        (q_ref, k_ref, v_ref, o_ref, misc_ref, gb_ref, mn_ref,
         h_ref, c_ref, n_ref, m_ref, z_scr, rows_scr) = refs
    nb, nc, L = z_scr.shape[0], z_scr.shape[1], z_scr.shape[2]
    c = pl.program_id(1)

    def padded(x, dt):
        x = x.astype(dt)
        if nvalid == L:
            return x
        return jnp.concatenate([x, jnp.zeros((L - nvalid, x.shape[1]), dt)], axis=0)

    r2 = lax.broadcasted_iota(I32, (L, L), 0)
    c2 = lax.broadcasted_iota(I32, (L, L), 1)
    tril = r2 >= c2

    @pl.when(c == 0)
    def _():
        if has_state:
            c_ref[...] = c0_ref[...]
            n_ref[...] = n0_ref[...]
            m_ref[...] = m0_ref[...]
        else:
            c_ref[...] = jnp.zeros_like(c_ref)
            n_ref[...] = jnp.zeros_like(n_ref)
            m_ref[...] = jnp.zeros_like(m_ref)
        assert MISC_F == MISC_I + M_HEADS and MISC_I % SUBLANES == 0
        row8 = lax.broadcasted_iota(I32, (SUBLANES, L), 0)
        tok8 = lax.broadcasted_iota(I32, (SUBLANES, L), 1)
        bias8 = jnp.concatenate([gb_ref[...]] * (L // LANES), axis=1) if L >= LANES else gb_ref[:, :L]
        triu = (r2 <= c2).astype(F32)
        for s in range(nb):
            for cc in range(nc):
                misc_t = padded(misc_ref[s, cc * nvalid:(cc + 1) * nvalid, :], F32).T
                gx = misc_t[MISC_I:MISC_I + SUBLANES] + bias8
                gates = jnp.where(row8 >= M_HEADS, _log_sigmoid(gx), gx)
                if nvalid != L:
                    gates = jnp.where(tok8 < nvalid, gates, jnp.where(row8 >= M_HEADS, 0.0, NEG))
                cum = jnp.dot(gates, triu, precision=lax.Precision.HIGHEST,
                              preferred_element_type=F32)
                rows = jnp.where(row8 >= M_HEADS, cum, gates)
                rows_scr[s, cc] = rows
                z_scr[s, cc] = jnp.concatenate([rows, jnp.zeros((LANES - SUBLANES, L), F32)], axis=0).T

    loaded = []
    for s in range(nb):
        loaded.append((z_scr[s, c], rows_scr[s, c], padded(q_ref[s], BF16), padded(k_ref[s], BF16),
                       padded(v_ref[s], BF16), padded(o_ref[s], F32), m_ref[s], c_ref[s], n_ref[s]))
    chains = [(s, hd) for s in range(nb) for hd in range(M_HEADS)]

    def operands(s, hd):
        z, rows, qb, kb, vb, ob, m_all, c_all, n_all = loaded[s]
        sl = slice(hd * M_HEAD_DIM, (hd + 1) * M_HEAD_DIM)
        return dict(
            sl=sl, q=qb[:, sl], k=kb[:, sl], v=vb[:, sl], o=ob[:, sl],
            icol=z[:, hd:hd + 1], bcol=z[:, M_HEADS + hd:M_HEADS + hd + 1],
            irow=rows[hd:hd + 1, :], brow=rows[M_HEADS + hd:M_HEADS + hd + 1, :],
            m_prev=m_all[hd:hd + 1, 0:1], s_prev=c_all[hd], n_prev=n_all[hd:hd + 1, :])

    def lane_sum(x):
        ones = jnp.ones((x.shape[1], LANES), BF16)
        head = x.astype(BF16)
        tail = (x - head.astype(F32)).astype(BF16)
        return (jnp.dot(head, ones, preferred_element_type=F32)
                + jnp.dot(tail, ones, preferred_element_type=F32))

    st = [operands(s, hd) for s, hd in chains]
    for x in st:
        x["qk"] = lax.dot_general(x["q"], x["k"], NT_DIMS, preferred_element_type=F32)
        x["qs"] = jnp.dot(x["q"], x["s_prev"].astype(BF16), preferred_element_type=F32)
        x["qn"] = lane_sum(x["q"].astype(F32) * x["n_prev"])
    for x in st:
        g = x["bcol"] + x["m_prev"]
        dm = jnp.where(tril, x["bcol"] - x["brow"] + x["irow"], NEG)
        x["mt"] = jnp.maximum(g, jnp.max(dm, axis=1, keepdims=True))
        x["gw"] = jnp.exp(g - x["mt"])
        x["qk"] = x["qk"] * jnp.exp(dm - x["mt"])
        b_last = x["bcol"][L - 1:L, :]
        g_last = b_last + x["m_prev"]
        a = b_last - x["bcol"] + x["icol"]
        x["m_new"] = jnp.maximum(g_last, jnp.max(a, axis=0, keepdims=True))
        x["sw"] = jnp.exp(g_last - x["m_new"])
        x["ak"] = jnp.exp(a - x["m_new"]) * x["k"].astype(F32)
    for x in st:
        x["pv"] = jnp.dot(x["qk"].astype(BF16), x["v"], preferred_element_type=F32)
        x["qksum"] = lane_sum(x["qk"])
        x["kv"] = lax.dot_general(x["ak"].astype(BF16), x["v"], TN_DIMS, preferred_element_type=F32)
    stores = []
    for (s, hd), x in zip(chains, st):
        num = x["gw"] * x["qs"] + x["pv"]
        den = x["gw"] * x["qn"] + x["qksum"]
        hh = num / jnp.maximum(jnp.abs(den), jnp.exp(-x["mt"]))
        hh = hh * lax.rsqrt(lane_sum(hh * hh) * (1.0 / M_HEAD_DIM) + EPS)
        hh = hh * mn_ref[:, x["sl"]] * jax.nn.sigmoid(x["o"])
        c_new = x["sw"] * x["s_prev"] + x["kv"]
        n_new = x["sw"] * x["n_prev"] + jnp.sum(x["ak"], axis=0, keepdims=True)
        stores.append((s, hd, x["sl"], hh[:nvalid].astype(h_ref.dtype), c_new, n_new,
                       jnp.broadcast_to(x["m_new"], (1, LANES))))
    for s, hd, sl, h_new, c_new, n_new, m_new in stores:
        h_ref[s, :, sl] = h_new
        c_ref[s, hd] = c_new
        n_ref[s, hd:hd + 1, :] = n_new
        m_ref[s, hd:hd + 1, :] = m_new


def _mlstm(qm, km, vm, om, misc, gate_bias, mnorm, state, B, T, hdt):
    L = next((c for c in (ML, LANES) if T % c == 0), T if T % SUBLANES == 0 else LANES)
    nvalid = L if T % L == 0 else T
    assert nvalid <= L and T % nvalid == 0 and nvalid % SUBLANES == 0
    nc = T // nvalid
    nb = next(n for n in ((4, 2, 1) if L >= LANES and nvalid == L else (8, 4, 2, 1)) if B % n == 0)
    seq3 = lambda a: a.reshape(B, T, a.shape[-1])
    blk = lambda w: pl.BlockSpec((nb, nvalid, w), lambda b, c: (b, c, 0))
    st_specs = [pl.BlockSpec((nb, M_HEADS, M_HEAD_DIM, M_HEAD_DIM), lambda b, c: (b, 0, 0, 0)),
                pl.BlockSpec((nb, M_HEADS, M_HEAD_DIM), lambda b, c: (b, 0, 0)),
                pl.BlockSpec((nb, M_HEADS, LANES), lambda b, c: (b, 0, 0))]
    in_specs = [blk(M_WIDTH), blk(M_WIDTH), blk(M_WIDTH), blk(M_WIDTH),
                pl.BlockSpec((nb, T, LANES), lambda b, c: (b, 0, 0)),
                pl.BlockSpec((SUBLANES, LANES), lambda b, c: (0, 0)),
                pl.BlockSpec((1, M_WIDTH), lambda b, c: (0, 0))]
    args = [seq3(qm), seq3(km), seq3(vm), seq3(om), seq3(misc), gate_bias, mnorm]
    if state is not None:
        in_specs += st_specs
        args += list(state)
    h, c_new, n_new, m_new = pl.pallas_call(
        functools.partial(_mlstm_kernel, nvalid, state is not None),
        grid=(B // nb, nc),
        in_specs=in_specs,
        out_specs=[blk(M_WIDTH)] + st_specs,
        out_shape=[jax.ShapeDtypeStruct((B, T, M_WIDTH), hdt),
                   jax.ShapeDtypeStruct((B, M_HEADS, M_HEAD_DIM, M_HEAD_DIM), F32),
                   jax.ShapeDtypeStruct((B, M_HEADS, M_HEAD_DIM), F32),
                   jax.ShapeDtypeStruct((B, M_HEADS, LANES), F32)],
        scratch_shapes=[pltpu.VMEM((nb, nc, L, LANES), F32), pltpu.VMEM((nb, nc, SUBLANES, L), F32)],
        compiler_params=_cparams(2),
        name="mlstm",
    )(*args)
    return h.reshape(B * T, M_WIDTH), c_new, n_new, m_new


def _sortable_key(score):
    bits = lax.bitcast_convert_type(score, I32)
    return bits ^ (lax.shift_right_arithmetic(bits, 31) & 0x7FFFFFFF)


def _key_to_score(key):
    return lax.bitcast_convert_type(key ^ (lax.shift_right_arithmetic(key, 31) & 0x7FFFFFFF), F32)


def _build_bias_strip(strip_ref, rb_ref, off, key_axis, log2_relative=False):
    ntiles, tile = strip_ref.shape[1], strip_ref.shape[2:]
    i = lax.broadcasted_iota(I32, tile, 1 - key_axis)
    x = lax.broadcasted_iota(I32, tile, key_axis)

    def entry(b, h):
        if log2_relative:
            return (rb_ref[b, h] - rb_ref[N_BUCKETS - 1, h]) * LOG2E
        return rb_ref[b, h]

    for t in range(ntiles):
        dist = i + (off - LANES * t) - x
        for h in range(N_HEADS):
            val = jnp.full(tile, entry(0, h), F32)
            for b in range(1, N_BUCKETS):
                val = jnp.where(dist >= BUCKET_BOUNDS[b], entry(b, h), val)
            strip_ref[h, t] = val


def _counter(score_ref, nk, ck, key_axis):
    def reduce(fn, fold, fold_all, init):
        def body(c, acc):
            c0 = pl.multiple_of(c * ck, ck)
            sc = score_ref[:, pl.ds(c0, ck)] if key_axis == 1 else score_ref[pl.ds(c0, ck), :]
            val = fn(sc, c0 + lax.broadcasted_iota(I32, sc.shape, key_axis))
            if key_axis == 1:
                for j in range(ck // LANES):
                    acc = fold(acc, val[:, j * LANES:(j + 1) * LANES])
                return acc
            return fold(acc, fold_all(val.reshape(ck // SUBLANES, SUBLANES, val.shape[1]), axis=0))

        nq = score_ref.shape[1 - key_axis]
        acc0 = jnp.full((nq, LANES) if key_axis == 1 else (SUBLANES, nq), init, I32)
        return fold_all(lax.fori_loop(0, nk, body, acc0), axis=key_axis, keepdims=True)

    def count(pred):
        return reduce(lambda sc, idx: jnp.where(pred(sc, idx), 1, 0), jnp.add, jnp.sum, 0)

    def lowest(fn):
        return reduce(fn, jnp.minimum, jnp.min, jnp.iinfo(jnp.int32).max)

    return count, lowest


IMIN = jnp.iinfo(jnp.int32).min


def _kth_largest_by_count(count, qshape, n_sel):
    def bit_step(i, key):
        cand = key + lax.shift_left(jnp.int32(1), 31 - i)
        cand_score = _key_to_score(cand)
        return jnp.where(count(lambda sc, idx: sc >= cand_score) >= n_sel, cand, key)

    return _key_to_score(lax.fori_loop(0, 32, bit_step, jnp.full(qshape, IMIN, I32)))


def _bit_planes(words):
    a = list(words)
    j, m = 16, 0x0000FFFF
    while j:
        k = 0
        while k < 32:
            t = (a[k] ^ lax.shift_right_logical(a[k + j], j)) & m
            a[k] = a[k] ^ t
            a[k + j] = a[k + j] ^ lax.shift_left(t, j)
            k = (k + j + 1) & ~j
        j >>= 1
        m = (m ^ (m << j)) & 0xFFFFFFFF
        m = m - (1 << 32) if m >= (1 << 31) else m
    return a[::-1]


def _kth_largest_by_planes(score_ref, planes_ref, nk, ck, n_sel):
    nc = planes_ref.shape[1]
    nq = score_ref.shape[1]
    assert ck == 32 * SUBLANES

    def pack_chunk(c, _):
        c0 = pl.multiple_of(c * ck, ck)
        u = _sortable_key(score_ref[pl.ds(c0, ck), :]) ^ IMIN
        u = u.reshape(32, SUBLANES, nq)
        for b, plane in enumerate(_bit_planes([u[v] for v in range(32)])):
            planes_ref[b, c] = plane
        return 0

    lax.fori_loop(0, nk, pack_chunk, 0)
    cand0 = tuple(jnp.where(c < nk, jnp.full((SUBLANES, nq), -1, I32), 0) for c in range(nc))
    return _plane_search(planes_ref, cand0, n_sel, 0)


def _kth_largest_by_planes_lanes(score_ref, planes_ref, n_sel):
    nq, ntiles = score_ref.shape[0], score_ref.shape[1] // LANES
    cand0 = []
    for g in range(planes_ref.shape[1]):
        real = min(32, ntiles - 32 * g)
        words = [_sortable_key(score_ref[:, (32 * g + v) * LANES:(32 * g + v + 1) * LANES]) ^ IMIN
                 if v < real else jnp.zeros((nq, LANES), I32) for v in range(32)]
        for b, plane in enumerate(_bit_planes(words)):
            planes_ref[b, g] = plane
        cand0.append(jnp.full((nq, LANES), -(1 << (32 - real)), I32))
    return _plane_search(planes_ref, tuple(cand0), n_sel, 1)


def _plane_search(planes_ref, cand0, n_sel, key_axis):
    def bit_step(i, carry):
        cand, n_above, thr_u = carry
        b = 31 - i
        ones = [m & planes_ref[b, c] for c, m in enumerate(cand)]
        pop = lax.population_count(ones[0])
        for o in ones[1:]:
            pop = pop + lax.population_count(o)
        tot = jnp.sum(pop, axis=key_axis, keepdims=True)
        take = n_above + tot >= n_sel
        cand = tuple(jnp.where(take, o, m ^ o) for o, m in zip(ones, cand))
        n_above = jnp.where(take, n_above, n_above + tot)
        thr_u = thr_u | jnp.where(take, lax.shift_left(jnp.int32(1), b), 0)
        return cand, n_above, thr_u

    zero = jnp.zeros_like(jnp.sum(cand0[0], axis=key_axis, keepdims=True))
    _, _, thr_u = lax.fori_loop(0, 32, bit_step, (cand0, zero, zero))
    return _key_to_score(thr_u ^ IMIN)


def _select_topk(thr, check, reducers, qshape, n_sel, idx_bits):
    count, lowest = reducers
    imax = jnp.iinfo(jnp.int32).max

    def with_counts(t):
        return t, count(lambda sc, idx: sc > t), count(lambda sc, idx: sc >= t)

    thr, n_gt, n_ge = with_counts(thr)
    if check:
        good = jnp.min(jnp.where(n_gt < n_sel, jnp.where(n_ge >= n_sel, 1, 0), 0)) > 0
        thr, n_gt, n_ge = lax.cond(
            good, lambda _: (thr, n_gt, n_ge),
            lambda _: with_counts(_kth_largest_by_count(count, qshape, n_sel)), 0)
    split = n_ge > n_sel
    need = jnp.where(split, n_sel - n_gt, 0)
    most = jnp.max(need)

    def by_extraction(_):
        def step(k, last):
            nxt = lowest(lambda sc, idx: jnp.where(sc == thr, jnp.where(idx > last, idx, imax), imax))
            return jnp.where(k < need, nxt, last)

        return lax.fori_loop(0, most, step, jnp.full(qshape, -1, I32))

    def by_bisection(_):
        def idx_step(i, lo):
            cand = lo + lax.shift_left(jnp.int32(1), idx_bits - 1 - i)
            cnt = count(lambda sc, idx: jnp.where(sc == thr, idx, imax) < cand)
            return jnp.where(cnt < need, cand, lo)

        return lax.fori_loop(0, idx_bits, idx_step, jnp.zeros(qshape, I32))

    jstar = lax.cond(most <= idx_bits, by_extraction, by_bisection, 0)
    return thr, jnp.where(split, jstar, imax)


def _valid_mask(scores, idx, thr, jstar, qpos):
    sel = jnp.where(scores > thr, 1, jnp.where(scores == thr, jnp.where(idx <= jstar, 1, 0), 0))
    return jnp.where(idx <= qpos, sel, 0) > 0


def _group_queries(qa, tq):
    lane = lax.broadcasted_iota(I32, (tq, LANES), 1)
    out = []
    for n in range(N_KV_HEADS):
        keep = (lane < HEAD_DIM) if n == 0 else (lane >= HEAD_DIM)
        tiles = [jnp.where(keep, qa[:, j * LANES:(j + 1) * LANES], jnp.zeros((), qa.dtype))
                 for j in range(GROUP)]
        out.append(jnp.concatenate(tiles, axis=0).astype(BF16))
    return out


def _write_attn(out_ref, carries, tq):
    lane = lax.broadcasted_iota(I32, (tq, LANES), 1)
    res = [acc / l for (_, l, acc) in carries]
    for j in range(GROUP):
        tile = jnp.where(lane < HEAD_DIM, res[0][j * tq:(j + 1) * tq], res[1][j * tq:(j + 1) * tq])
        out_ref[:, j * LANES:(j + 1) * LANES] = tile.astype(out_ref.dtype)


def _dsa_prompt_kernel(n_sel, idx_bits, rb_ref, qat_ref, qit_ref, misct_ref, ki_ref, k_ref, vt_ref,
                       out_ref, score_ref, planes_ref, strip_ref, sa_ref, sb_ref, m_ref, acc_ref):
    qb = pl.program_id(1)
    q0 = qb * TQ
    nk = (q0 + TQ + CKP - 1) // CKP
    tiles_per_chunk = CKP // LANES
    back_tiles = STRIP_BACK
    strip_off = LANES * back_tiles

    @pl.when((pl.program_id(0) == 0) & (qb == 0))
    def _():
        _build_bias_strip(strip_ref, rb_ref, strip_off, 0, log2_relative=True)
        planes_ref[...] = jnp.zeros(planes_ref.shape, I32)

    qpos = q0 + lax.broadcasted_iota(I32, (1, TQ), 1)

    qit = qit_ref[0]
    qstack = jnp.concatenate([qit[h * IDX_DIM:(h + 1) * IDX_DIM] for h in range(N_IDX_HEADS)],
                             axis=1)
    w = misct_ref[0, MISC_W:MISC_W + N_IDX_HEADS, :] * (N_IDX_HEADS ** -0.5 * IDX_DIM ** -0.5)

    def score_chunk(c, _):
        c0 = pl.multiple_of(c * CKP, CKP)
        d = jnp.dot(ki_ref[pl.ds(c0, CKP), :], qstack, preferred_element_type=F32)
        d = jnp.maximum(d, 0.0)
        s = jnp.zeros((CKP, TQ), F32)
        for h in range(N_IDX_HEADS):
            s = s + d[:, h * TQ:(h + 1) * TQ] * w[h:h + 1, :]
        idx = c0 + lax.broadcasted_iota(I32, (CKP, TQ), 0)
        score_ref[pl.ds(c0, CKP), :] = jnp.where(idx <= qpos, s, NEG)
        return 0

    lax.fori_loop(0, nk, score_chunk, 0)

    def search(_):
        proposal = _kth_largest_by_planes(score_ref, planes_ref, nk, CKP, n_sel)
        return _select_topk(proposal, True, _counter(score_ref, nk, CKP, 0), (1, TQ), n_sel, idx_bits)

    def everything(_):
        return jnp.full((1, TQ), NEG, F32), jnp.full((1, TQ), jnp.iinfo(jnp.int32).max, I32)

    thr, jstar = lax.cond(q0 + TQ <= n_sel, everything, search, 0)

    qat = qat_ref[0]
    zeros = jnp.zeros((HEAD_DIM, TQ), qat.dtype)
    qgroups = []
    for n in range(N_KV_HEADS):
        tiles = []
        for g in range(GROUP):
            h = n * GROUP + g
            x = qat[h * HEAD_DIM:(h + 1) * HEAD_DIM]
            tiles.append(jnp.concatenate([x, zeros] if n == 0 else [zeros, x], axis=0))
        qgroups.append(jnp.concatenate(tiles, axis=1))

    def logits_into(s_ref, c):
        kc = k_ref[pl.ds(pl.multiple_of(c * CKP, CKP), CKP), :]
        for n in range(N_KV_HEADS):
            s_ref[n] = jnp.dot(kc, qgroups[n], preferred_element_type=F32)

    def attend_chunk(c, s_cur_ref, s_next_ref):
        if s_next_ref is not None:
            logits_into(s_next_ref, jnp.minimum(c + 1, nk - 1))
        c0 = pl.multiple_of(c * CKP, CKP)
        idx = c0 + lax.broadcasted_iota(I32, (CKP, TQ), 0)
        valid = _valid_mask(score_ref[pl.ds(c0, CKP), :], idx, thr, jstar, qpos)
        vct = vt_ref[0, c]
        tiles = [jnp.maximum(back_tiles + j - (qb * (TQ // LANES) - c * tiles_per_chunk), 0)
                 for j in range(tiles_per_chunk)]
        for n in range(N_KV_HEADS):
            m_old = m_ref[n]
            parts = []
            for g in range(GROUP):
                bias = jnp.concatenate([strip_ref[n * GROUP + g, t] for t in tiles], axis=0)
                parts.append(jnp.where(valid, s_cur_ref[n, :, g * TQ:(g + 1) * TQ] + bias, NEG))
            sm = jnp.concatenate(parts, axis=1)
            m_new = jnp.maximum(m_old, jnp.max(sm, axis=0, keepdims=True))
            p = jnp.exp2((sm - m_new).astype(BF16))
            acc_ref[n] = jnp.exp2(m_old - m_new) * acc_ref[n] + jnp.dot(vct, p, preferred_element_type=F32)
            m_ref[n] = m_new

    m_ref[...] = jnp.full(m_ref.shape, NEG, F32)
    acc_ref[...] = jnp.zeros(acc_ref.shape, F32)
    logits_into(sa_ref, 0)

    def attend_pair(i, _):
        attend_chunk(2 * i, sa_ref, sb_ref)
        attend_chunk(2 * i + 1, sb_ref, sa_ref)
        return 0

    lax.fori_loop(0, nk // 2, attend_pair, 0)

    @pl.when(nk % 2 == 1)
    def _():
        attend_chunk(nk - 1, sa_ref, None)

    carries = [(None, acc_ref[n]) for n in range(N_KV_HEADS)]
    res = [acc[:LANES] / acc[LANES:LANES + 1] for (_, acc) in carries]
    row = lax.broadcasted_iota(I32, (LANES, TQ), 0)
    for j in range(GROUP):
        cols = slice(j * TQ, (j + 1) * TQ)
        tile_t = jnp.where(row < HEAD_DIM, res[0][:, cols], res[1][:, cols])
        out_ref[:, j * LANES:(j + 1) * LANES] = tile_t.T.astype(out_ref.dtype)


def _dsa_prompt(rel_bias, qat, qit, misct, kib, kb, vtb, B, T):
    assert T % CKP == 0 and T % TQ == 0
    nq = T // TQ
    n_sel = min(TOPK_MAX, T // 4)
    qcols = lambda w: pl.BlockSpec((1, w, TQ), lambda b, q: (b, 0, q))
    seq = lambda w: pl.BlockSpec((T, w), lambda b, q: (b, 0))
    return pl.pallas_call(
        functools.partial(_dsa_prompt_kernel, n_sel, max(1, (T - 1).bit_length())),
        grid=(B, nq),
        in_specs=[pl.BlockSpec(memory_space=pltpu.SMEM), qcols(ATT_WIDTH), qcols(N_IDX_HEADS * IDX_DIM),
                  qcols(LANES), seq(IDX_DIM), seq(LANES),
                  pl.BlockSpec((1, T // CKP, VT_ROWS, CKP), lambda b, q: (b, 0, 0, 0))],
        out_specs=pl.BlockSpec((TQ, ATT_WIDTH), lambda b, q: (b * nq + q, 0)),
        out_shape=jax.ShapeDtypeStruct((B * T, ATT_WIDTH), BF16),
        scratch_shapes=[pltpu.VMEM((T, TQ), F32),
                        pltpu.VMEM((32, T // CKP, SUBLANES, TQ), I32),
                        pltpu.VMEM((N_HEADS, STRIP_BACK + max(TQ, CKP) // LANES, LANES, TQ), F32),
                        pltpu.VMEM((N_KV_HEADS, CKP, GROUP * TQ), F32),
                        pltpu.VMEM((N_KV_HEADS, CKP, GROUP * TQ), F32),
                        pltpu.VMEM((N_KV_HEADS, 1, GROUP * TQ), F32),
                        pltpu.VMEM((N_KV_HEADS, VT_ROWS, GROUP * TQ), F32)],
        compiler_params=_cparams(2),
        name="dsa_prompt",
    )(rel_bias, qat, qit, misct, kib, kb, vtb)


def _page_pipeline(pt_ref, n_pages, caches, bufs, sems):
    depth = bufs[0].shape[0]

    def copies(bb, sl, j):
        pid = pt_ref[bb, j]
        cols = pl.ds(j * PAGE_SIZE, PAGE_SIZE)
        return [pltpu.make_async_copy(c.at[pid], buf.at[sl, :, cols], sems.at[sl, i])
                for i, (c, buf) in enumerate(zip(caches, bufs))]

    def start_all(bb, sl):
        for j in range(n_pages):
            for cp in copies(bb, sl, j):
                cp.start()

    def wait_all(bb, sl):
        for j in range(n_pages):
            for cp in copies(bb, sl, j):
                cp.wait()

    def step():
        b = pl.program_id(0)
        nb = pl.num_programs(0)
        slot = b % depth

        @pl.when(b == 0)
        def _():
            for ahead in range(depth - 1):
                @pl.when(ahead < nb)
                def _():
                    start_all(ahead, ahead)

        @pl.when(b + depth - 1 < nb)
        def _():
            start_all(b + depth - 1, (b + depth - 1) % depth)

        wait_all(b, slot)
        return slot

    return step


def _pad_rows(x, rows):
    return jnp.concatenate([x, jnp.zeros((rows - x.shape[0], x.shape[1]), x.dtype)], axis=0)


def _sample_score_kernel(n_sel, idx_bits, n_pages, ts, ck, pt_ref, qi_ref, misc_ref, kin_ref,
                         ckit_hbm, score_ref, thr_ref, jst_ref, ki_buf, sems, planes_ref):
    b = pl.program_id(0)
    past = n_pages * PAGE_SIZE
    slot = _page_pipeline(pt_ref, n_pages, [ckit_hbm], [ki_buf], sems)()

    qi = qi_ref[...]
    qstack = jnp.concatenate([qi[:, h * IDX_DIM:(h + 1) * IDX_DIM] for h in range(N_IDX_HEADS)],
                             axis=0).astype(BF16)
    w = misc_ref[:, MISC_W:MISC_W + N_IDX_HEADS] * (N_IDX_HEADS ** -0.5 * IDX_DIM ** -0.5)
    d_past = jnp.dot(qstack, ki_buf[slot].astype(BF16), preferred_element_type=F32)
    d_own = lax.dot_general(qstack, _pad_rows(kin_ref[...], PAGE_SIZE).astype(BF16), NT_DIMS,
                            preferred_element_type=F32)
    d = jnp.maximum(jnp.concatenate([d_past, d_own], axis=1), 0.0)
    s = jnp.zeros((ts, past + PAGE_SIZE), F32)
    for h in range(N_IDX_HEADS):
        s = s + d[h * ts:(h + 1) * ts] * w[:, h:h + 1]
    idx = lax.broadcasted_iota(I32, s.shape, 1)
    qpos = past + lax.broadcasted_iota(I32, (ts, 1), 0)
    score_ref[pl.ds(pl.multiple_of(b * ts, ts), ts), :] = jnp.where(idx <= qpos, s, NEG)

    @pl.when(b == pl.num_programs(0) - 1)
    def _():
        reducers = _counter(score_ref, score_ref.shape[1] // ck, ck, 1)
        qshape = (score_ref.shape[0], 1)
        thr = _kth_largest_by_planes_lanes(score_ref, planes_ref, n_sel)
        thr, jstar = _select_topk(thr, True, reducers, qshape, n_sel, idx_bits)
        thr_ref[...] = jnp.broadcast_to(thr, thr_ref.shape)
        jst_ref[...] = jnp.broadcast_to(jstar, jst_ref.shape)


def _sample_attend_kernel(n_pages, ts, pt_ref, rb_ref, qa_ref, score_ref, thr_ref, jst_ref, kn_ref,
                          vn_ref, ckt_hbm, cvt_hbm, out_ref, k_buf, v_buf, sems, strip_ref):
    b = pl.program_id(0)
    past = n_pages * PAGE_SIZE
    rows = N_HEADS * ts

    @pl.when(b == 0)
    def _():
        _build_bias_strip(strip_ref, rb_ref, LANES * (strip_ref.shape[1] - 1), 1)

    slot = _page_pipeline(pt_ref, n_pages, [ckt_hbm, cvt_hbm], [k_buf, v_buf], sems)()

    q2 = jnp.concatenate(_group_queries(qa_ref[...], ts), axis=0)
    k_own = _pad_rows(kn_ref[...], PAGE_SIZE).astype(BF16)
    v_own = _pad_rows(vn_ref[...], PAGE_SIZE).astype(BF16)
    s_past = jnp.dot(q2, k_buf[slot].astype(BF16), preferred_element_type=F32)
    s_own = lax.dot_general(q2, k_own, NT_DIMS, preferred_element_type=F32)
    far = strip_ref[:, 0].reshape(rows, LANES)[:, 0:1]
    near = [strip_ref[:, t].reshape(rows, LANES) for t in (1, 2)]
    s = jnp.concatenate([s_past[:, :past - PAGE_SIZE] + far, s_past[:, past - PAGE_SIZE:] + near[0],
                         s_own + near[1]], axis=1)

    qpos = past + lax.broadcasted_iota(I32, (ts, 1), 0)
    scores = score_ref[...]
    valid = _valid_mask(scores, lax.broadcasted_iota(I32, scores.shape, 1), thr_ref[:, 0:1],
                        jst_ref[:, 0:1], qpos)
    s = jnp.where(valid[None], s.reshape(N_HEADS, ts, past + PAGE_SIZE), NEG).reshape(rows, -1)
    m = jnp.max(s, axis=1, keepdims=True)
    p = jnp.exp(s - m)
    l = jnp.sum(p, axis=1, keepdims=True)
    pb = p.astype(BF16)
    pv = lax.dot_general(pb[:, :past], v_buf[slot].astype(BF16), NT_DIMS, preferred_element_type=F32)
    pv = pv + jnp.dot(pb[:, past:], v_own, preferred_element_type=F32)
    half = GROUP * ts
    carries = [(None, l[n * half:(n + 1) * half], pv[n * half:(n + 1) * half]) for n in range(N_KV_HEADS)]
    _write_attn(out_ref, carries, ts)


def _dsa_sample(page_table, rel_bias, qa, qi, misc, ki_new, k_new, v_new, ckit, ckt, cvt, DB, ts):
    n_pages = page_table.shape[1]
    past = n_pages * PAGE_SIZE
    n_sel = min(TOPK_MAX, (past + ts) // 4)
    lpad = past + PAGE_SIZE
    idx_bits = max(1, (lpad - 1).bit_length())
    ck = LANES * math.gcd(lpad // LANES, 5)
    assert ts % SUBLANES == 0 and ts <= PAGE_SIZE and n_pages >= 1
    blk = lambda w: pl.BlockSpec((ts, w), lambda b, pt: (b, 0))
    whole = lambda w: pl.BlockSpec((DB * ts, w), lambda b, pt: (0, 0))
    hbm = pl.BlockSpec(memory_space=pl.ANY)
    keys, thr, jstar = pl.pallas_call(
        functools.partial(_sample_score_kernel, n_sel, idx_bits, n_pages, ts, ck),
        grid_spec=pltpu.PrefetchScalarGridSpec(
            num_scalar_prefetch=1,
            grid=(DB,),
            in_specs=[blk(N_IDX_HEADS * IDX_DIM), blk(LANES), blk(IDX_DIM), hbm],
            out_specs=[whole(lpad), whole(LANES), whole(LANES)],
            scratch_shapes=[pltpu.VMEM((2, IDX_DIM, past), F32), pltpu.SemaphoreType.DMA((2, 1)),
                            pltpu.VMEM((32, -(-lpad // (32 * LANES)), DB * ts, LANES), I32)]),
        out_shape=[jax.ShapeDtypeStruct((DB * ts, lpad), F32),
                   jax.ShapeDtypeStruct((DB * ts, LANES), F32),
                   jax.ShapeDtypeStruct((DB * ts, LANES), I32)],
        compiler_params=_cparams(1),
        name="sample_score",
    )(page_table, qi, misc, ki_new, ckit)
    return pl.pallas_call(
        functools.partial(_sample_attend_kernel, n_pages, ts),
        grid_spec=pltpu.PrefetchScalarGridSpec(
            num_scalar_prefetch=1,
            grid=(DB,),
            in_specs=[pl.BlockSpec(memory_space=pltpu.SMEM), blk(ATT_WIDTH), blk(lpad), blk(LANES),
                      blk(LANES), blk(LANES), blk(LANES), hbm, hbm],
            out_specs=blk(ATT_WIDTH),
            scratch_shapes=[pltpu.VMEM((KV_SLOTS, LANES, past), F32), pltpu.VMEM((KV_SLOTS, LANES, past), F32),
                            pltpu.SemaphoreType.DMA((KV_SLOTS, 2)),
                            pltpu.VMEM((N_HEADS, 3, ts, LANES), F32)]),
        out_shape=jax.ShapeDtypeStruct((DB * ts, ATT_WIDTH), F32),
        compiler_params=_cparams(1),
        name="sample_attend",
    )(page_table, rel_bias, qa, keys, thr, jstar, k_new, v_new, ckt, cvt)


def _pack_layer_weights(w_in, b_i, b_f, w_out, w_up, w_down):
    D = w_in.shape[0]
    sizes = (ATT_WIDTH, N_KV_HEADS * HEAD_DIM, N_KV_HEADS * HEAD_DIM, N_IDX_HEADS * IDX_DIM, IDX_DIM,
             N_IDX_HEADS, M_WIDTH, M_WIDTH, M_WIDTH, M_WIDTH, M_HEADS, M_HEADS)
    assert w_in.shape[1] == sum(sizes)
    pts = np.cumsum((0,) + sizes)
    w_t = w_in.T.astype(BF16)
    qa, k, v, qi, ki, wi, qm, km, vm, om, im, fm = [w_t[pts[i]:pts[i + 1]] for i in range(len(sizes))]
    perm = np.asarray(HEAD_PERM)
    qa_perm = qa.reshape(N_HEADS, HEAD_DIM, D)[perm].reshape(ATT_WIDTH, D)
    misc = jnp.concatenate([wi, im, fm, jnp.zeros((LANES - N_IDX_HEADS - 2 * M_HEADS, D), BF16)], axis=0)
    wp = jnp.concatenate([qa_perm, v, qi, k, ki, ki, misc, qm, km, vm, om], axis=0).T
    assert wp.shape[1] == N_PACK
    gate_bias = jnp.broadcast_to(jnp.concatenate([b_i, b_f]).astype(F32)[:, None], (2 * M_HEADS, LANES))
    woa = w_out[:ATT_WIDTH].reshape(N_HEADS, HEAD_DIM, -1)[perm].reshape(ATT_WIDTH, -1).astype(BF16)
    woh = w_out[ATT_WIDTH:].astype(BF16)
    wt = jnp.concatenate([k, v, ki, jnp.zeros((LANES - IDX_DIM, D), BF16), qa, qi, misc], axis=0)
    assert wt.shape[0] == N_TPACK
    return wp, wt, gate_bias, woa, woh, w_up.astype(BF16), w_down.astype(BF16)


def _layer(x, packed, g1, g2, mnorm, rel_bias, gf, final_norm, past):
    wp, wt, gate_bias, woa, woh, wup, wdn = packed
    B, T, D = x.shape
    x2 = x.reshape(B * T, D)
    tm = math.gcd(T if past is None else B * T, 512)
    kv_w = N_KV_HEADS * HEAD_DIM
    if past is None:
        (misc, qm, km, vm, om, kb, kib, kt, vt, kit, vtb, qat, qit, misct) = _inproj(
            x2, g1.reshape(1, D), wp, wt, B, T, tm)
        attn = _dsa_prompt(rel_bias, qat, qit, misct, kib, kb, vtb, B, T)
        state = None
        k_new = kt.reshape(B, N_KV_HEADS, HEAD_DIM, T).transpose(0, 3, 1, 2)
        v_new = vt.reshape(B, N_KV_HEADS, HEAD_DIM, T).transpose(0, 3, 1, 2)
        ki_new = kit.transpose(0, 2, 1)
    else:
        (misc, qm, km, vm, om, qa, qi, k, v, ki) = _inproj(x2, g1.reshape(1, D), wp, None, B, T, tm)
        page_table, cache_k, cache_v, cache_kidx, c0, n0, m0 = past
        n_pool = cache_k.shape[0]
        ckt = cache_k.transpose(0, 2, 3, 1).reshape(n_pool, kv_w, PAGE_SIZE)
        cvt = cache_v.transpose(0, 2, 3, 1).reshape(n_pool, kv_w, PAGE_SIZE)
        ckit = cache_kidx.transpose(0, 2, 1)
        attn = _dsa_sample(page_table, rel_bias, qa, qi, misc, ki, k, v, ckit, ckt, cvt, B, T)
        state = (c0, n0, jnp.broadcast_to(m0[..., None], m0.shape + (LANES,)))
        k_new = k.reshape(B, T, N_KV_HEADS, HEAD_DIM)
        v_new = v.reshape(B, T, N_KV_HEADS, HEAD_DIM)
        ki_new = ki.reshape(B, T, IDX_DIM)
    h, c_new, n_new, m_new = _mlstm(qm, km, vm, om, misc, gate_bias, mnorm.reshape(1, M_WIDTH), state,
                                    B, T, BF16 if past is None else F32)
    y = _post(x2, attn, h, woa, woh, g2.reshape(1, D), wup, wdn, gf.reshape(1, D), final_norm, tm)
    return (y.reshape(B, T, D), k_new, v_new, ki_new, c_new, n_new, m_new[..., 0])


def kernel(x_prompt, x_sample, cache_k, cache_v, cache_kidx, page_table, state_C, state_n, state_m,
           w_in, b_igate, b_fgate, mlstm_norm, rel_bias, w_out, norm1, norm2, w_up, w_down, norm_f):
    depth = w_in.shape[0]
    xp, xs = x_prompt, x_sample
    outs_p, outs_s = [], []
    for l in range(depth):
        packed = _pack_layer_weights(w_in[l], b_igate[l], b_fgate[l], w_out[l], w_up[l], w_down[l])
        last = l == depth - 1
        common = (packed, norm1[l], norm2[l], mlstm_norm[l], rel_bias, norm_f, last)
        rp = _layer(xp, *common, None)
        rs = _layer(xs, *common, (page_table, cache_k[l], cache_v[l], cache_kidx[l],
                                  state_C[l], state_n[l], state_m[l]))
        xp, xs = rp[0], rs[0]
        outs_p.append(rp[1:])
        outs_s.append(rs[1:])
    stack = lambda outs, i: jnp.stack([o[i] for o in outs])
    return ((xp, xs) + tuple(stack(outs_p, i) for i in range(6))
            + tuple(stack(outs_s, i) for i in range(6)))
```

```python
import functools
import math

import numpy as np
import jax
import jax.numpy as jnp
from jax import lax
from jax.experimental import pallas as pl
from jax.experimental.pallas import tpu as pltpu

F32 = jnp.float32
BF16 = jnp.bfloat16
I32 = jnp.int32

N_HEADS = 8
HEAD_DIM = 64
N_KV_HEADS = 2
GROUP = N_HEADS // N_KV_HEADS
N_IDX_HEADS = 8
IDX_DIM = 64
TOPK_MAX = 256
N_BUCKETS = 32
MAX_DISTANCE = 128
M_HEADS = 4
M_HEAD_DIM = 128
PAGE_SIZE = 128
EPS = 1e-6
NEG = -1e30
LOG2E = math.log2(math.e)
ATT_WIDTH = N_HEADS * HEAD_DIM
M_WIDTH = M_HEADS * M_HEAD_DIM

LANES = 128
SUBLANES = 8
VMEM_LIMIT = 56 * 1024 * 1024

C_QA = 0
C_V = C_QA + ATT_WIDTH
C_QI = C_V + LANES
C_K = C_QI + N_IDX_HEADS * IDX_DIM
C_KI2 = C_K + LANES
C_MISC = C_KI2 + LANES
C_QM = C_MISC + LANES
C_KM = C_QM + M_WIDTH
C_VM = C_KM + M_WIDTH
C_OM = C_VM + M_WIDTH
N_PACK = C_OM + M_WIDTH
MISC_W = 0
MISC_I = 8
MISC_F = 12

HEAD_PERM = (0, 4, 1, 5, 2, 6, 3, 7)

TQ = 256
STRIP_BACK = 2
assert LANES * (STRIP_BACK - 1) >= MAX_DISTANCE
CKP = 256
VT_ROWS = LANES + 16
ML = 256
KV_SLOTS = 4

NT_DIMS = (((1,), (1,)), ((), ()))
TN_DIMS = (((0,), (0,)), ((), ()))


def _bucket_bounds():
    max_exact = N_BUCKETS // 2
    scale = (N_BUCKETS - max_exact) / math.log(MAX_DISTANCE / max_exact)

    def bucket(n, dt):
        if n < max_exact:
            return n
        val = np.log(np.asarray(max(n, 1), dt) / dt(max_exact)) * dt(scale)
        return min(max_exact + int(val), N_BUCKETS - 1)

    table = [bucket(n, np.float32) for n in range(MAX_DISTANCE + 2)]
    assert table == [bucket(n, np.float64) for n in range(MAX_DISTANCE + 2)]
    assert table[MAX_DISTANCE] == N_BUCKETS - 1
    return [next(d for d, b in enumerate(table) if b >= k) for k in range(N_BUCKETS)]


BUCKET_BOUNDS = _bucket_bounds()


def _cparams(n_axes):
    return pltpu.CompilerParams(dimension_semantics=("arbitrary",) * n_axes,
                                vmem_limit_bytes=VMEM_LIMIT)


def _const_spec(shape):
    nd = len(shape)
    return pl.BlockSpec(shape, lambda *_: (0,) * nd, pipeline_mode=pl.Buffered(1))


def _rms(x, g):
    return x * lax.rsqrt(jnp.mean(x * x, axis=-1, keepdims=True) + EPS) * g


def _inproj_mlstm(mm, misc_ref, qm_ref, km_ref, vm_ref, om_ref):
    misc_ref[...] = mm(C_MISC, LANES)
    qm_ref[...] = mm(C_QM, M_WIDTH).astype(qm_ref.dtype)
    km_ref[...] = (mm(C_KM, M_WIDTH) * (M_HEAD_DIM ** -0.5)).astype(km_ref.dtype)
    vm_ref[...] = mm(C_VM, M_WIDTH).astype(vm_ref.dtype)
    om_ref[...] = mm(C_OM, M_WIDTH)


def _inproj_rows_kernel(x_ref, g_ref, w_ref, misc_ref, qm_ref, km_ref, vm_ref, om_ref,
                        qa_ref, qi_ref, k_ref, v_ref, ki_ref):
    ub = _rms(x_ref[...], g_ref[...]).astype(BF16)
    z = jnp.dot(ub, w_ref[...], preferred_element_type=F32)
    mm = lambda c0, n: z[:, c0:c0 + n]
    _inproj_mlstm(mm, misc_ref, qm_ref, km_ref, vm_ref, om_ref)
    qa_ref[...] = mm(C_QA, ATT_WIDTH) * (HEAD_DIM ** -0.5)
    qi_ref[...] = mm(C_QI, N_IDX_HEADS * IDX_DIM)
    k_ref[...] = mm(C_K, LANES)
    v_ref[...] = mm(C_V, LANES)
    ki_ref[...] = mm(C_KI2, LANES)[:, :IDX_DIM]


R_K = 0
R_V = R_K + LANES
R_KI = R_V + LANES
R_QA = R_KI + LANES
R_QI = R_QA + ATT_WIDTH
R_MISC = R_QI + N_IDX_HEADS * IDX_DIM
N_TPACK = R_MISC + LANES


def _inproj_cols_kernel(x_ref, g_ref, w_ref, wt_ref, misc_ref, qm_ref, km_ref, vm_ref, om_ref,
                        kb_ref, kib_ref, kt_ref, vt_ref, kit_ref, vtb_ref, qat_ref, qit_ref,
                        misct_ref):
    ub = _rms(x_ref[...], g_ref[...]).astype(BF16)
    z = jnp.dot(ub, w_ref[:, C_K:], preferred_element_type=F32)
    mm = lambda c0, n: z[:, c0 - C_K:c0 - C_K + n]
    _inproj_mlstm(mm, misc_ref, qm_ref, km_ref, vm_ref, om_ref)
    kb_ref[...] = mm(C_K, LANES).astype(BF16)
    kib_ref[...] = mm(C_KI2, LANES)[:, :IDX_DIM].astype(BF16)

    zt = lax.dot_general(wt_ref[...], ub, NT_DIMS, preferred_element_type=F32)
    mt = lambda r0, n: zt[r0:r0 + n]

    kt_ref[0] = mt(R_K, LANES)
    vt = mt(R_V, LANES)
    vt_ref[0] = vt
    ones = jnp.ones((VT_ROWS - LANES, CKP), BF16)
    for j in range(vtb_ref.shape[1]):
        vtb_ref[0, j] = jnp.concatenate([vt[:, j * CKP:(j + 1) * CKP].astype(BF16), ones], axis=0)
    kit_ref[0] = mt(R_KI, IDX_DIM)
    qat_ref[0] = (mt(R_QA, ATT_WIDTH) * (HEAD_DIM ** -0.5 * LOG2E)).astype(BF16)
    qit_ref[0] = mt(R_QI, N_IDX_HEADS * IDX_DIM).astype(BF16)
    misct_ref[0] = mt(R_MISC, LANES)


def _inproj(x2, g1, wp, wt, B, T, tm):
    R, D = x2.shape
    assert R == B * T and R % tm == 0
    mdt = F32 if wt is None else BF16
    row = lambda i: (i, 0)
    outs = [(LANES, F32), (M_WIDTH, mdt), (M_WIDTH, mdt), (M_WIDTH, mdt), (M_WIDTH, F32)]
    in_specs = [pl.BlockSpec((tm, D), row), _const_spec((1, D)), _const_spec((D, N_PACK))]
    if wt is None:
        kern, args = _inproj_rows_kernel, (x2, g1, wp)
        outs += [(ATT_WIDTH, F32), (N_IDX_HEADS * IDX_DIM, F32), (LANES, F32), (LANES, F32), (IDX_DIM, F32)]
    else:
        kern, args = _inproj_cols_kernel, (x2, g1, wp, wt)
        in_specs.append(_const_spec(wt.shape))
        outs += [(LANES, BF16), (IDX_DIM, BF16)]
    out_specs = [pl.BlockSpec((tm, w), row) for w, _ in outs]
    out_shape = [jax.ShapeDtypeStruct((R, w), dt) for w, dt in outs]
    if wt is not None:
        assert T % tm == 0 and tm % CKP == 0
        tpb, cpt = T // tm, tm // CKP
        cols = lambda i: (i // tpb, 0, i % tpb)
        for w, dt in ((LANES, F32), (LANES, F32), (IDX_DIM, F32)):
            out_specs.append(pl.BlockSpec((1, w, tm), cols))
            out_shape.append(jax.ShapeDtypeStruct((B, w, T), dt))
        out_specs.append(pl.BlockSpec((1, cpt, VT_ROWS, CKP), lambda i: (i // tpb, i % tpb, 0, 0)))
        out_shape.append(jax.ShapeDtypeStruct((B, T // CKP, VT_ROWS, CKP), BF16))
        for w, dt in ((ATT_WIDTH, BF16), (N_IDX_HEADS * IDX_DIM, BF16), (LANES, F32)):
            out_specs.append(pl.BlockSpec((1, w, tm), cols))
            out_shape.append(jax.ShapeDtypeStruct((B, w, T), dt))
    return pl.pallas_call(
        kern,
        grid=(R // tm,),
        in_specs=in_specs,
        out_specs=out_specs,
        out_shape=out_shape,
        compiler_params=_cparams(1),
        name="inproj",
    )(*args)


def _post_kernel(ff_chunk, final_norm, x_ref, a_ref, h_ref, woa_ref, woh_ref, g2_ref, wup_ref,
                 wdn_ref, gf_ref, y_ref):
    mix = jnp.dot(a_ref[...].astype(BF16), woa_ref[...], preferred_element_type=F32)
    mix = mix + jnp.dot(h_ref[...].astype(BF16), woh_ref[...], preferred_element_type=F32)
    hres = x_ref[...] + mix
    f = _rms(hres, g2_ref[...]).astype(BF16)
    acc = hres
    for c0 in range(0, wup_ref.shape[1], ff_chunk):
        up = jnp.dot(f, wup_ref[:, c0:c0 + ff_chunk], preferred_element_type=F32)
        r = jnp.maximum(up, 0.0)
        acc = acc + jnp.dot((r * r).astype(BF16), wdn_ref[c0:c0 + ff_chunk, :],
                            preferred_element_type=F32)
    y_ref[...] = _rms(acc, gf_ref[...]) if final_norm else acc


def _post(x2, attn, h, woa, woh, g2, wup, wdn, gf, final_norm, tm):
    R, D = x2.shape
    dff = wup.shape[1]
    assert R % tm == 0
    row = lambda i: (i, 0)
    return pl.pallas_call(
        functools.partial(_post_kernel, min(dff, 1024), final_norm),
        grid=(R // tm,),
        in_specs=[pl.BlockSpec((tm, D), row), pl.BlockSpec((tm, ATT_WIDTH), row),
                  pl.BlockSpec((tm, M_WIDTH), row), _const_spec(woa.shape), _const_spec(woh.shape),
                  _const_spec((1, D)), _const_spec(wup.shape), _const_spec(wdn.shape),
                  _const_spec((1, D))],
        out_specs=pl.BlockSpec((tm, D), row),
        out_shape=jax.ShapeDtypeStruct((R, D), F32),
        compiler_params=_cparams(1),
        name="post",
    )(x2, attn, h, woa, woh, g2, wup, wdn, gf)


def _log_sigmoid(x):
    return -(jnp.maximum(-x, 0.0) + jnp.log1p(jnp.exp(-jnp.abs(x))))


def _mlstm_kernel(nvalid, has_state, *refs):
    if has_state:
        (q_ref, k_ref, v_ref, o_ref, misc_ref, gb_ref, mn_ref, c0_ref, n0_ref, m0_ref,
         h_ref, c_ref, n_ref, m_ref, z_scr, rows_scr) = refs
    else:
        (q_ref, k_ref, v_ref, o_ref, misc_ref, gb_ref, mn_ref,
         h_ref, c_ref, n_ref, m_ref, z_scr, rows_scr) = refs
    nb, nc, L = z_scr.shape[0], z_scr.shape[1], z_scr.shape[2]
    c = pl.program_id(1)

    def padded(x, dt):
        x = x.astype(dt)
        if nvalid == L:
            return x
        return jnp.concatenate([x, jnp.zeros((L - nvalid, x.shape[1]), dt)], axis=0)

    r2 = lax.broadcasted_iota(I32, (L, L), 0)
    c2 = lax.broadcasted_iota(I32, (L, L), 1)
    tril = r2 >= c2

    @pl.when(c == 0)
    def _():
        if has_state:
            c_ref[...] = c0_ref[...]
            n_ref[...] = n0_ref[...]
            m_ref[...] = m0_ref[...]
        else:
            c_ref[...] = jnp.zeros_like(c_ref)
            n_ref[...] = jnp.zeros_like(n_ref)
            m_ref[...] = jnp.zeros_like(m_ref)
        assert MISC_F == MISC_I + M_HEADS and MISC_I % SUBLANES == 0
        row8 = lax.broadcasted_iota(I32, (SUBLANES, L), 0)
        tok8 = lax.broadcasted_iota(I32, (SUBLANES, L), 1)
        bias8 = jnp.concatenate([gb_ref[...]] * (L // LANES), axis=1) if L >= LANES else gb_ref[:, :L]
        triu = (r2 <= c2).astype(F32)
        for s in range(nb):
            for cc in range(nc):
                misc_t = padded(misc_ref[s, cc * nvalid:(cc + 1) * nvalid, :], F32).T
                gx = misc_t[MISC_I:MISC_I + SUBLANES] + bias8
                gates = jnp.where(row8 >= M_HEADS, _log_sigmoid(gx), gx)
                if nvalid != L:
                    gates = jnp.where(tok8 < nvalid, gates, jnp.where(row8 >= M_HEADS, 0.0, NEG))
                cum = jnp.dot(gates, triu, precision=lax.Precision.HIGHEST,
                              preferred_element_type=F32)
                rows = jnp.where(row8 >= M_HEADS, cum, gates)
                rows_scr[s, cc] = rows
                z_scr[s, cc] = jnp.concatenate([rows, jnp.zeros((LANES - SUBLANES, L), F32)], axis=0).T

    loaded = []
    for s in range(nb):
        loaded.append((z_scr[s, c], rows_scr[s, c], padded(q_ref[s], BF16), padded(k_ref[s], BF16),
                       padded(v_ref[s], BF16), padded(o_ref[s], F32), m_ref[s], c_ref[s], n_ref[s]))
    chains = [(s, hd) for s in range(nb) for hd in range(M_HEADS)]

    def operands(s, hd):
        z, rows, qb, kb, vb, ob, m_all, c_all, n_all = loaded[s]
        sl = slice(hd * M_HEAD_DIM, (hd + 1) * M_HEAD_DIM)
        return dict(
            sl=sl, q=qb[:, sl], k=kb[:, sl], v=vb[:, sl], o=ob[:, sl],
            icol=z[:, hd:hd + 1], bcol=z[:, M_HEADS + hd:M_HEADS + hd + 1],
            irow=rows[hd:hd + 1, :], brow=rows[M_HEADS + hd:M_HEADS + hd + 1, :],
            m_prev=m_all[hd:hd + 1, 0:1], s_prev=c_all[hd], n_prev=n_all[hd:hd + 1, :])

    def lane_sum(x):
        ones = jnp.ones((x.shape[1], LANES), BF16)
        head = x.astype(BF16)
        tail = (x - head.astype(F32)).astype(BF16)
        return (jnp.dot(head, ones, preferred_element_type=F32)
                + jnp.dot(tail, ones, preferred_element_type=F32))

    st = [operands(s, hd) for s, hd in chains]
    for x in st:
        x["qk"] = lax.dot_general(x["q"], x["k"], NT_DIMS, preferred_element_type=F32)
        x["qs"] = jnp.dot(x["q"], x["s_prev"].astype(BF16), preferred_element_type=F32)
        x["qn"] = lane_sum(x["q"].astype(F32) * x["n_prev"])
    for x in st:
        g = x["bcol"] + x["m_prev"]
        dm = jnp.where(tril, x["bcol"] - x["brow"] + x["irow"], NEG)
        x["mt"] = jnp.maximum(g, jnp.max(dm, axis=1, keepdims=True))
        x["gw"] = jnp.exp(g - x["mt"])
        x["qk"] = x["qk"] * jnp.exp(dm - x["mt"])
        b_last = x["bcol"][L - 1:L, :]
        g_last = b_last + x["m_prev"]
        a = b_last - x["bcol"] + x["icol"]
        x["m_new"] = jnp.maximum(g_last, jnp.max(a, axis=0, keepdims=True))
        x["sw"] = jnp.exp(g_last - x["m_new"])
        x["ak"] = jnp.exp(a - x["m_new"]) * x["k"].astype(F32)
    for x in st:
        x["pv"] = jnp.dot(x["qk"].astype(BF16), x["v"], preferred_element_type=F32)
        x["qksum"] = lane_sum(x["qk"])
        x["kv"] = lax.dot_general(x["ak"].astype(BF16), x["v"], TN_DIMS, preferred_element_type=F32)
    stores = []
    for (s, hd), x in zip(chains, st):
        num = x["gw"] * x["qs"] + x["pv"]
        den = x["gw"] * x["qn"] + x["qksum"]
        hh = num / jnp.maximum(jnp.abs(den), jnp.exp(-x["mt"]))
        hh = hh * lax.rsqrt(lane_sum(hh * hh) * (1.0 / M_HEAD_DIM) + EPS)
        hh = hh * mn_ref[:, x["sl"]] * jax.nn.sigmoid(x["o"])
        c_new = x["sw"] * x["s_prev"] + x["kv"]
        n_new = x["sw"] * x["n_prev"] + jnp.sum(x["ak"], axis=0, keepdims=True)
        stores.append((s, hd, x["sl"], hh[:nvalid].astype(h_ref.dtype), c_new, n_new,
                       jnp.broadcast_to(x["m_new"], (1, LANES))))
    for s, hd, sl, h_new, c_new, n_new, m_new in stores:
        h_ref[s, :, sl] = h_new
        c_ref[s, hd] = c_new
        n_ref[s, hd:hd + 1, :] = n_new
        m_ref[s, hd:hd + 1, :] = m_new


def _mlstm(qm, km, vm, om, misc, gate_bias, mnorm, state, B, T, hdt):
    L = next((c for c in (ML, LANES) if T % c == 0), T if T % SUBLANES == 0 else LANES)
    nvalid = L if T % L == 0 else T
    assert nvalid <= L and T % nvalid == 0 and nvalid % SUBLANES == 0
    nc = T // nvalid
    nb = next(n for n in ((4, 2, 1) if L >= LANES and nvalid == L else (8, 4, 2, 1)) if B % n == 0)
    seq3 = lambda a: a.reshape(B, T, a.shape[-1])
    blk = lambda w: pl.BlockSpec((nb, nvalid, w), lambda b, c: (b, c, 0))
    st_specs = [pl.BlockSpec((nb, M_HEADS, M_HEAD_DIM, M_HEAD_DIM), lambda b, c: (b, 0, 0, 0)),
                pl.BlockSpec((nb, M_HEADS, M_HEAD_DIM), lambda b, c: (b, 0, 0)),
                pl.BlockSpec((nb, M_HEADS, LANES), lambda b, c: (b, 0, 0))]
    in_specs = [blk(M_WIDTH), blk(M_WIDTH), blk(M_WIDTH), blk(M_WIDTH),
                pl.BlockSpec((nb, T, LANES), lambda b, c: (b, 0, 0)),
                pl.BlockSpec((SUBLANES, LANES), lambda b, c: (0, 0)),
                pl.BlockSpec((1, M_WIDTH), lambda b, c: (0, 0))]
    args = [seq3(qm), seq3(km), seq3(vm), seq3(om), seq3(misc), gate_bias, mnorm]
    if state is not None:
        in_specs += st_specs
        args += list(state)
    h, c_new, n_new, m_new = pl.pallas_call(
        functools.partial(_mlstm_kernel, nvalid, state is not None),
        grid=(B // nb, nc),
        in_specs=in_specs,
        out_specs=[blk(M_WIDTH)] + st_specs,
        out_shape=[jax.ShapeDtypeStruct((B, T, M_WIDTH), hdt),
                   jax.ShapeDtypeStruct((B, M_HEADS, M_HEAD_DIM, M_HEAD_DIM), F32),
                   jax.ShapeDtypeStruct((B, M_HEADS, M_HEAD_DIM), F32),
                   jax.ShapeDtypeStruct((B, M_HEADS, LANES), F32)],
        scratch_shapes=[pltpu.VMEM((nb, nc, L, LANES), F32), pltpu.VMEM((nb, nc, SUBLANES, L), F32)],
        compiler_params=_cparams(2),
        name="mlstm",
    )(*args)
    return h.reshape(B * T, M_WIDTH), c_new, n_new, m_new


def _sortable_key(score):
    bits = lax.bitcast_convert_type(score, I32)
    return bits ^ (lax.shift_right_arithmetic(bits, 31) & 0x7FFFFFFF)


def _key_to_score(key):
    return lax.bitcast_convert_type(key ^ (lax.shift_right_arithmetic(key, 31) & 0x7FFFFFFF), F32)


def _build_bias_strip(strip_ref, rb_ref, off, key_axis, log2_relative=False):
    ntiles, tile = strip_ref.shape[1], strip_ref.shape[2:]
    i = lax.broadcasted_iota(I32, tile, 1 - key_axis)
    x = lax.broadcasted_iota(I32, tile, key_axis)

    def entry(b, h):
        if log2_relative:
            return (rb_ref[b, h] - rb_ref[N_BUCKETS - 1, h]) * LOG2E
        return rb_ref[b, h]

    for t in range(ntiles):
        dist = i + (off - LANES * t) - x
        for h in range(N_HEADS):
            val = jnp.full(tile, entry(0, h), F32)
            for b in range(1, N_BUCKETS):
                val = jnp.where(dist >= BUCKET_BOUNDS[b], entry(b, h), val)
            strip_ref[h, t] = val


def _counter(score_ref, nk, ck, key_axis):
    def reduce(fn, fold, fold_all, init):
        def body(c, acc):
            c0 = pl.multiple_of(c * ck, ck)
            sc = score_ref[:, pl.ds(c0, ck)] if key_axis == 1 else score_ref[pl.ds(c0, ck), :]
            val = fn(sc, c0 + lax.broadcasted_iota(I32, sc.shape, key_axis))
            if key_axis == 1:
                for j in range(ck // LANES):
                    acc = fold(acc, val[:, j * LANES:(j + 1) * LANES])
                return acc
            return fold(acc, fold_all(val.reshape(ck // SUBLANES, SUBLANES, val.shape[1]), axis=0))

        nq = score_ref.shape[1 - key_axis]
        acc0 = jnp.full((nq, LANES) if key_axis == 1 else (SUBLANES, nq), init, I32)
        return fold_all(lax.fori_loop(0, nk, body, acc0), axis=key_axis, keepdims=True)

    def count(pred):
        return reduce(lambda sc, idx: jnp.where(pred(sc, idx), 1, 0), jnp.add, jnp.sum, 0)

    def lowest(fn):
        return reduce(fn, jnp.minimum, jnp.min, jnp.iinfo(jnp.int32).max)

    return count, lowest


IMIN = jnp.iinfo(jnp.int32).min


def _kth_largest_by_count(count, qshape, n_sel):
    def bit_step(i, key):
        cand = key + lax.shift_left(jnp.int32(1), 31 - i)
        cand_score = _key_to_score(cand)
        return jnp.where(count(lambda sc, idx: sc >= cand_score) >= n_sel, cand, key)

    return _key_to_score(lax.fori_loop(0, 32, bit_step, jnp.full(qshape, IMIN, I32)))


def _bit_planes(words):
    a = list(words)
    j, m = 16, 0x0000FFFF
    while j:
        k = 0
        while k < 32:
            t = (a[k] ^ lax.shift_right_logical(a[k + j], j)) & m
            a[k] = a[k] ^ t
            a[k + j] = a[k + j] ^ lax.shift_left(t, j)
            k = (k + j + 1) & ~j
        j >>= 1
        m = (m ^ (m << j)) & 0xFFFFFFFF
        m = m - (1 << 32) if m >= (1 << 31) else m
    return a[::-1]


def _kth_largest_by_planes(score_ref, planes_ref, nk, ck, n_sel):
    nc = planes_ref.shape[1]
    nq = score_ref.shape[1]
    assert ck == 32 * SUBLANES

    def pack_chunk(c, _):
        c0 = pl.multiple_of(c * ck, ck)
        u = _sortable_key(score_ref[pl.ds(c0, ck), :]) ^ IMIN
        u = u.reshape(32, SUBLANES, nq)
        for b, plane in enumerate(_bit_planes([u[v] for v in range(32)])):
            planes_ref[b, c] = plane
        return 0

    lax.fori_loop(0, nk, pack_chunk, 0)
    cand0 = tuple(jnp.where(c < nk, jnp.full((SUBLANES, nq), -1, I32), 0) for c in range(nc))
    return _plane_search(planes_ref, cand0, n_sel, 0)


def _kth_largest_by_planes_lanes(score_ref, planes_ref, n_sel):
    nq, ntiles = score_ref.shape[0], score_ref.shape[1] // LANES
    cand0 = []
    for g in range(planes_ref.shape[1]):
        real = min(32, ntiles - 32 * g)
        words = [_sortable_key(score_ref[:, (32 * g + v) * LANES:(32 * g + v + 1) * LANES]) ^ IMIN
                 if v < real else jnp.zeros((nq, LANES), I32) for v in range(32)]
        for b, plane in enumerate(_bit_planes(words)):
            planes_ref[b, g] = plane
        cand0.append(jnp.full((nq, LANES), -(1 << (32 - real)), I32))
    return _plane_search(planes_ref, tuple(cand0), n_sel, 1)


def _plane_search(planes_ref, cand0, n_sel, key_axis):
    def bit_step(i, carry):
        cand, n_above, thr_u = carry
        b = 31 - i
        ones = [m & planes_ref[b, c] for c, m in enumerate(cand)]
        pop = lax.population_count(ones[0])
        for o in ones[1:]:
            pop = pop + lax.population_count(o)
        tot = jnp.sum(pop, axis=key_axis, keepdims=True)
        take = n_above + tot >= n_sel
        cand = tuple(jnp.where(take, o, m ^ o) for o, m in zip(ones, cand))
        n_above = jnp.where(take, n_above, n_above + tot)
        thr_u = thr_u | jnp.where(take, lax.shift_left(jnp.int32(1), b), 0)
        return cand, n_above, thr_u

    zero = jnp.zeros_like(jnp.sum(cand0[0], axis=key_axis, keepdims=True))
    _, _, thr_u = lax.fori_loop(0, 32, bit_step, (cand0, zero, zero))
    return _key_to_score(thr_u ^ IMIN)


def _select_topk(thr, check, reducers, qshape, n_sel, idx_bits):
    count, lowest = reducers
    imax = jnp.iinfo(jnp.int32).max

    def with_counts(t):
        return t, count(lambda sc, idx: sc > t), count(lambda sc, idx: sc >= t)

    thr, n_gt, n_ge = with_counts(thr)
    if check:
        good = jnp.min(jnp.where(n_gt < n_sel, jnp.where(n_ge >= n_sel, 1, 0), 0)) > 0
        thr, n_gt, n_ge = lax.cond(
            good, lambda _: (thr, n_gt, n_ge),
            lambda _: with_counts(_kth_largest_by_count(count, qshape, n_sel)), 0)
    split = n_ge > n_sel
    need = jnp.where(split, n_sel - n_gt, 0)
    most = jnp.max(need)

    def by_extraction(_):
        def step(k, last):
            nxt = lowest(lambda sc, idx: jnp.where(sc == thr, jnp.where(idx > last, idx, imax), imax))
            return jnp.where(k < need, nxt, last)

        return lax.fori_loop(0, most, step, jnp.full(qshape, -1, I32))

    def by_bisection(_):
        def idx_step(i, lo):
            cand = lo + lax.shift_left(jnp.int32(1), idx_bits - 1 - i)
            cnt = count(lambda sc, idx: jnp.where(sc == thr, idx, imax) < cand)
            return jnp.where(cnt < need, cand, lo)

        return lax.fori_loop(0, idx_bits, idx_step, jnp.zeros(qshape, I32))

    jstar = lax.cond(most <= idx_bits, by_extraction, by_bisection, 0)
    return thr, jnp.where(split, jstar, imax)


def _valid_mask(scores, idx, thr, jstar, qpos):
    sel = jnp.where(scores > thr, 1, jnp.where(scores == thr, jnp.where(idx <= jstar, 1, 0), 0))
    return jnp.where(idx <= qpos, sel, 0) > 0


def _group_queries(qa, tq):
    lane = lax.broadcasted_iota(I32, (tq, LANES), 1)
    out = []
    for n in range(N_KV_HEADS):
        keep = (lane < HEAD_DIM) if n == 0 else (lane >= HEAD_DIM)
        tiles = [jnp.where(keep, qa[:, j * LANES:(j + 1) * LANES], jnp.zeros((), qa.dtype))
                 for j in range(GROUP)]
        out.append(jnp.concatenate(tiles, axis=0).astype(BF16))
    return out


def _write_attn(out_ref, carries, tq):
    lane = lax.broadcasted_iota(I32, (tq, LANES), 1)
    res = [acc / l for (_, l, acc) in carries]
    for j in range(GROUP):
        tile = jnp.where(lane < HEAD_DIM, res[0][j * tq:(j + 1) * tq], res[1][j * tq:(j + 1) * tq])
        out_ref[:, j * LANES:(j + 1) * LANES] = tile.astype(out_ref.dtype)


def _dsa_prompt_kernel(n_sel, idx_bits, rb_ref, qat_ref, qit_ref, misct_ref, ki_ref, k_ref, vt_ref,
                       out_ref, score_ref, planes_ref, strip_ref, sa_ref, sb_ref, m_ref, acc_ref):
    qb = pl.program_id(1)
    q0 = qb * TQ
    nk = (q0 + TQ + CKP - 1) // CKP
    tiles_per_chunk = CKP // LANES
    back_tiles = STRIP_BACK
    strip_off = LANES * back_tiles

    @pl.when((pl.program_id(0) == 0) & (qb == 0))
    def _():
        _build_bias_strip(strip_ref, rb_ref, strip_off, 0, log2_relative=True)
        planes_ref[...] = jnp.zeros(planes_ref.shape, I32)

    qpos = q0 + lax.broadcasted_iota(I32, (1, TQ), 1)

    qit = qit_ref[0]
    qstack = jnp.concatenate([qit[h * IDX_DIM:(h + 1) * IDX_DIM] for h in range(N_IDX_HEADS)],
                             axis=1)
    w = misct_ref[0, MISC_W:MISC_W + N_IDX_HEADS, :] * (N_IDX_HEADS ** -0.5 * IDX_DIM ** -0.5)

    def score_chunk(c, _):
        c0 = pl.multiple_of(c * CKP, CKP)
        d = jnp.dot(ki_ref[pl.ds(c0, CKP), :], qstack, preferred_element_type=F32)
        d = jnp.maximum(d, 0.0)
        s = jnp.zeros((CKP, TQ), F32)
        for h in range(N_IDX_HEADS):
            s = s + d[:, h * TQ:(h + 1) * TQ] * w[h:h + 1, :]
        idx = c0 + lax.broadcasted_iota(I32, (CKP, TQ), 0)
        score_ref[pl.ds(c0, CKP), :] = jnp.where(idx <= qpos, s, NEG)
        return 0

    lax.fori_loop(0, nk, score_chunk, 0)

    def search(_):
        proposal = _kth_largest_by_planes(score_ref, planes_ref, nk, CKP, n_sel)
        return _select_topk(proposal, True, _counter(score_ref, nk, CKP, 0), (1, TQ), n_sel, idx_bits)

    def everything(_):
        return jnp.full((1, TQ), NEG, F32), jnp.full((1, TQ), jnp.iinfo(jnp.int32).max, I32)

    thr, jstar = lax.cond(q0 + TQ <= n_sel, everything, search, 0)

    qat = qat_ref[0]
    zeros = jnp.zeros((HEAD_DIM, TQ), qat.dtype)
    qgroups = []
    for n in range(N_KV_HEADS):
        tiles = []
        for g in range(GROUP):
            h = n * GROUP + g
            x = qat[h * HEAD_DIM:(h + 1) * HEAD_DIM]
            tiles.append(jnp.concatenate([x, zeros] if n == 0 else [zeros, x], axis=0))
        qgroups.append(jnp.concatenate(tiles, axis=1))

    def logits_into(s_ref, c):
        kc = k_ref[pl.ds(pl.multiple_of(c * CKP, CKP), CKP), :]
        for n in range(N_KV_HEADS):
            s_ref[n] = jnp.dot(kc, qgroups[n], preferred_element_type=F32)

    def attend_chunk(c, s_cur_ref, s_next_ref):
        if s_next_ref is not None:
            logits_into(s_next_ref, jnp.minimum(c + 1, nk - 1))
        c0 = pl.multiple_of(c * CKP, CKP)
        idx = c0 + lax.broadcasted_iota(I32, (CKP, TQ), 0)
        valid = _valid_mask(score_ref[pl.ds(c0, CKP), :], idx, thr, jstar, qpos)
        vct = vt_ref[0, c]
        tiles = [jnp.maximum(back_tiles + j - (qb * (TQ // LANES) - c * tiles_per_chunk), 0)
                 for j in range(tiles_per_chunk)]
        for n in range(N_KV_HEADS):
            m_old = m_ref[n]
            parts = []
            for g in range(GROUP):
                bias = jnp.concatenate([strip_ref[n * GROUP + g, t] for t in tiles], axis=0)
                parts.append(jnp.where(valid, s_cur_ref[n, :, g * TQ:(g + 1) * TQ] + bias, NEG))
            sm = jnp.concatenate(parts, axis=1)
            m_new = jnp.maximum(m_old, jnp.max(sm, axis=0, keepdims=True))
            p = jnp.exp2((sm - m_new).astype(BF16))
            acc_ref[n] = jnp.exp2(m_old - m_new) * acc_ref[n] + jnp.dot(vct, p, preferred_element_type=F32)
            m_ref[n] = m_new

    m_ref[...] = jnp.full(m_ref.shape, NEG, F32)
    acc_ref[...] = jnp.zeros(acc_ref.shape, F32)
    logits_into(sa_ref, 0)

    def attend_pair(i, _):
        attend_chunk(2 * i, sa_ref, sb_ref)
        attend_chunk(2 * i + 1, sb_ref, sa_ref)
        return 0

    lax.fori_loop(0, nk // 2, attend_pair, 0)

    @pl.when(nk % 2 == 1)
    def _():
        attend_chunk(nk - 1, sa_ref, None)

    carries = [(None, acc_ref[n]) for n in range(N_KV_HEADS)]
    res = [acc[:LANES] / acc[LANES:LANES + 1] for (_, acc) in carries]
    row = lax.broadcasted_iota(I32, (LANES, TQ), 0)
    for j in range(GROUP):
        cols = slice(j * TQ, (j + 1) * TQ)
        tile_t = jnp.where(row < HEAD_DIM, res[0][:, cols], res[1][:, cols])
        out_ref[:, j * LANES:(j + 1) * LANES] = tile_t.T.astype(out_ref.dtype)


def _dsa_prompt(rel_bias, qat, qit, misct, kib, kb, vtb, B, T):
    assert T % CKP == 0 and T % TQ == 0
    nq = T // TQ
    n_sel = min(TOPK_MAX, T // 4)
    qcols = lambda w: pl.BlockSpec((1, w, TQ), lambda b, q: (b, 0, q))
    seq = lambda w: pl.BlockSpec((T, w), lambda b, q: (b, 0))
    return pl.pallas_call(
        functools.partial(_dsa_prompt_kernel, n_sel, max(1, (T - 1).bit_length())),
        grid=(B, nq),
        in_specs=[pl.BlockSpec(memory_space=pltpu.SMEM), qcols(ATT_WIDTH), qcols(N_IDX_HEADS * IDX_DIM),
                  qcols(LANES), seq(IDX_DIM), seq(LANES),
                  pl.BlockSpec((1, T // CKP, VT_ROWS, CKP), lambda b, q: (b, 0, 0, 0))],
        out_specs=pl.BlockSpec((TQ, ATT_WIDTH), lambda b, q: (b * nq + q, 0)),
        out_shape=jax.ShapeDtypeStruct((B * T, ATT_WIDTH), BF16),
        scratch_shapes=[pltpu.VMEM((T, TQ), F32),
                        pltpu.VMEM((32, T // CKP, SUBLANES, TQ), I32),
                        pltpu.VMEM((N_HEADS, STRIP_BACK + max(TQ, CKP) // LANES, LANES, TQ), F32),
                        pltpu.VMEM((N_KV_HEADS, CKP, GROUP * TQ), F32),
                        pltpu.VMEM((N_KV_HEADS, CKP, GROUP * TQ), F32),
                        pltpu.VMEM((N_KV_HEADS, 1, GROUP * TQ), F32),
                        pltpu.VMEM((N_KV_HEADS, VT_ROWS, GROUP * TQ), F32)],
        compiler_params=_cparams(2),
        name="dsa_prompt",
    )(rel_bias, qat, qit, misct, kib, kb, vtb)


def _page_pipeline(pt_ref, n_pages, caches, bufs, sems):
    depth = bufs[0].shape[0]

    def copies(bb, sl, j):
        pid = pt_ref[bb, j]
        cols = pl.ds(j * PAGE_SIZE, PAGE_SIZE)
        return [pltpu.make_async_copy(c.at[pid], buf.at[sl, :, cols], sems.at[sl, i])
                for i, (c, buf) in enumerate(zip(caches, bufs))]

    def start_all(bb, sl):
        for j in range(n_pages):
            for cp in copies(bb, sl, j):
                cp.start()

    def wait_all(bb, sl):
        for j in range(n_pages):
            for cp in copies(bb, sl, j):
                cp.wait()

    def step():
        b = pl.program_id(0)
        nb = pl.num_programs(0)
        slot = b % depth

        @pl.when(b == 0)
        def _():
            for ahead in range(depth - 1):
                @pl.when(ahead < nb)
                def _():
                    start_all(ahead, ahead)

        @pl.when(b + depth - 1 < nb)
        def _():
            start_all(b + depth - 1, (b + depth - 1) % depth)

        wait_all(b, slot)
        return slot

    return step


def _pad_rows(x, rows):
    return jnp.concatenate([x, jnp.zeros((rows - x.shape[0], x.shape[1]), x.dtype)], axis=0)


def _sample_score_kernel(n_sel, idx_bits, n_pages, ts, ck, pt_ref, qi_ref, misc_ref, kin_ref,
                         ckit_hbm, score_ref, thr_ref, jst_ref, ki_buf, sems, planes_ref):
    b = pl.program_id(0)
    past = n_pages * PAGE_SIZE
    slot = _page_pipeline(pt_ref, n_pages, [ckit_hbm], [ki_buf], sems)()

    qi = qi_ref[...]
    qstack = jnp.concatenate([qi[:, h * IDX_DIM:(h + 1) * IDX_DIM] for h in range(N_IDX_HEADS)],
                             axis=0).astype(BF16)
    w = misc_ref[:, MISC_W:MISC_W + N_IDX_HEADS] * (N_IDX_HEADS ** -0.5 * IDX_DIM ** -0.5)
    d_past = jnp.dot(qstack, ki_buf[slot].astype(BF16), preferred_element_type=F32)
    d_own = lax.dot_general(qstack, _pad_rows(kin_ref[...], PAGE_SIZE).astype(BF16), NT_DIMS,
                            preferred_element_type=F32)
    d = jnp.maximum(jnp.concatenate([d_past, d_own], axis=1), 0.0)
    s = jnp.zeros((ts, past + PAGE_SIZE), F32)
    for h in range(N_IDX_HEADS):
        s = s + d[h * ts:(h + 1) * ts] * w[:, h:h + 1]
    idx = lax.broadcasted_iota(I32, s.shape, 1)
    qpos = past + lax.broadcasted_iota(I32, (ts, 1), 0)
    score_ref[pl.ds(pl.multiple_of(b * ts, ts), ts), :] = jnp.where(idx <= qpos, s, NEG)

    @pl.when(b == pl.num_programs(0) - 1)
    def _():
        reducers = _counter(score_ref, score_ref.shape[1] // ck, ck, 1)
        qshape = (score_ref.shape[0], 1)
        thr = _kth_largest_by_planes_lanes(score_ref, planes_ref, n_sel)
        thr, jstar = _select_topk(thr, True, reducers, qshape, n_sel, idx_bits)
        thr_ref[...] = jnp.broadcast_to(thr, thr_ref.shape)
        jst_ref[...] = jnp.broadcast_to(jstar, jst_ref.shape)


def _sample_attend_kernel(n_pages, ts, pt_ref, rb_ref, qa_ref, score_ref, thr_ref, jst_ref, kn_ref,
                          vn_ref, ckt_hbm, cvt_hbm, out_ref, k_buf, v_buf, sems, strip_ref):
    b = pl.program_id(0)
    past = n_pages * PAGE_SIZE
    rows = N_HEADS * ts

    @pl.when(b == 0)
    def _():
        _build_bias_strip(strip_ref, rb_ref, LANES * (strip_ref.shape[1] - 1), 1)

    slot = _page_pipeline(pt_ref, n_pages, [ckt_hbm, cvt_hbm], [k_buf, v_buf], sems)()

    q2 = jnp.concatenate(_group_queries(qa_ref[...], ts), axis=0)
    k_own = _pad_rows(kn_ref[...], PAGE_SIZE).astype(BF16)
    v_own = _pad_rows(vn_ref[...], PAGE_SIZE).astype(BF16)
    s_past = jnp.dot(q2, k_buf[slot].astype(BF16), preferred_element_type=F32)
    s_own = lax.dot_general(q2, k_own, NT_DIMS, preferred_element_type=F32)
    far = strip_ref[:, 0].reshape(rows, LANES)[:, 0:1]
    near = [strip_ref[:, t].reshape(rows, LANES) for t in (1, 2)]
    s = jnp.concatenate([s_past[:, :past - PAGE_SIZE] + far, s_past[:, past - PAGE_SIZE:] + near[0],
                         s_own + near[1]], axis=1)

    qpos = past + lax.broadcasted_iota(I32, (ts, 1), 0)
    scores = score_ref[...]
    valid = _valid_mask(scores, lax.broadcasted_iota(I32, scores.shape, 1), thr_ref[:, 0:1],
                        jst_ref[:, 0:1], qpos)
    s = jnp.where(valid[None], s.reshape(N_HEADS, ts, past + PAGE_SIZE), NEG).reshape(rows, -1)
    m = jnp.max(s, axis=1, keepdims=True)
    p = jnp.exp(s - m)
    l = jnp.sum(p, axis=1, keepdims=True)
    pb = p.astype(BF16)
    pv = lax.dot_general(pb[:, :past], v_buf[slot].astype(BF16), NT_DIMS, preferred_element_type=F32)
    pv = pv + jnp.dot(pb[:, past:], v_own, preferred_element_type=F32)
    half = GROUP * ts
    carries = [(None, l[n * half:(n + 1) * half], pv[n * half:(n + 1) * half]) for n in range(N_KV_HEADS)]
    _write_attn(out_ref, carries, ts)


def _dsa_sample(page_table, rel_bias, qa, qi, misc, ki_new, k_new, v_new, ckit, ckt, cvt, DB, ts):
    n_pages = page_table.shape[1]
    past = n_pages * PAGE_SIZE
    n_sel = min(TOPK_MAX, (past + ts) // 4)
    lpad = past + PAGE_SIZE
    idx_bits = max(1, (lpad - 1).bit_length())
    ck = LANES * math.gcd(lpad // LANES, 5)
    assert ts % SUBLANES == 0 and ts <= PAGE_SIZE and n_pages >= 1
    blk = lambda w: pl.BlockSpec((ts, w), lambda b, pt: (b, 0))
    whole = lambda w: pl.BlockSpec((DB * ts, w), lambda b, pt: (0, 0))
    hbm = pl.BlockSpec(memory_space=pl.ANY)
    keys, thr, jstar = pl.pallas_call(
        functools.partial(_sample_score_kernel, n_sel, idx_bits, n_pages, ts, ck),
        grid_spec=pltpu.PrefetchScalarGridSpec(
            num_scalar_prefetch=1,
            grid=(DB,),
            in_specs=[blk(N_IDX_HEADS * IDX_DIM), blk(LANES), blk(IDX_DIM), hbm],
            out_specs=[whole(lpad), whole(LANES), whole(LANES)],
            scratch_shapes=[pltpu.VMEM((2, IDX_DIM, past), F32), pltpu.SemaphoreType.DMA((2, 1)),
                            pltpu.VMEM((32, -(-lpad // (32 * LANES)), DB * ts, LANES), I32)]),
        out_shape=[jax.ShapeDtypeStruct((DB * ts, lpad), F32),
                   jax.ShapeDtypeStruct((DB * ts, LANES), F32),
                   jax.ShapeDtypeStruct((DB * ts, LANES), I32)],
        compiler_params=_cparams(1),
        name="sample_score",
    )(page_table, qi, misc, ki_new, ckit)
    return pl.pallas_call(
        functools.partial(_sample_attend_kernel, n_pages, ts),
        grid_spec=pltpu.PrefetchScalarGridSpec(
            num_scalar_prefetch=1,
            grid=(DB,),
            in_specs=[pl.BlockSpec(memory_space=pltpu.SMEM), blk(ATT_WIDTH), blk(lpad), blk(LANES),
                      blk(LANES), blk(LANES), blk(LANES), hbm, hbm],
            out_specs=blk(ATT_WIDTH),
            scratch_shapes=[pltpu.VMEM((KV_SLOTS, LANES, past), F32), pltpu.VMEM((KV_SLOTS, LANES, past), F32),
                            pltpu.SemaphoreType.DMA((KV_SLOTS, 2)),
                            pltpu.VMEM((N_HEADS, 3, ts, LANES), F32)]),
        out_shape=jax.ShapeDtypeStruct((DB * ts, ATT_WIDTH), F32),
        compiler_params=_cparams(1),
        name="sample_attend",
    )(page_table, rel_bias, qa, keys, thr, jstar, k_new, v_new, ckt, cvt)


def _pack_layer_weights(w_in, b_i, b_f, w_out, w_up, w_down):
    D = w_in.shape[0]
    sizes = (ATT_WIDTH, N_KV_HEADS * HEAD_DIM, N_KV_HEADS * HEAD_DIM, N_IDX_HEADS * IDX_DIM, IDX_DIM,
             N_IDX_HEADS, M_WIDTH, M_WIDTH, M_WIDTH, M_WIDTH, M_HEADS, M_HEADS)
    assert w_in.shape[1] == sum(sizes)
    pts = np.cumsum((0,) + sizes)
    w_t = w_in.T.astype(BF16)
    qa, k, v, qi, ki, wi, qm, km, vm, om, im, fm = [w_t[pts[i]:pts[i + 1]] for i in range(len(sizes))]
    perm = np.asarray(HEAD_PERM)
    qa_perm = qa.reshape(N_HEADS, HEAD_DIM, D)[perm].reshape(ATT_WIDTH, D)
    misc = jnp.concatenate([wi, im, fm, jnp.zeros((LANES - N_IDX_HEADS - 2 * M_HEADS, D), BF16)], axis=0)
    wp = jnp.concatenate([qa_perm, v, qi, k, ki, ki, misc, qm, km, vm, om], axis=0).T
    assert wp.shape[1] == N_PACK
    gate_bias = jnp.broadcast_to(jnp.concatenate([b_i, b_f]).astype(F32)[:, None], (2 * M_HEADS, LANES))
    woa = w_out[:ATT_WIDTH].reshape(N_HEADS, HEAD_DIM, -1)[perm].reshape(ATT_WIDTH, -1).astype(BF16)
    woh = w_out[ATT_WIDTH:].astype(BF16)
    wt = jnp.concatenate([k, v, ki, jnp.zeros((LANES - IDX_DIM, D), BF16), qa, qi, misc], axis=0)
    assert wt.shape[0] == N_TPACK
    return wp, wt, gate_bias, woa, woh, w_up.astype(BF16), w_down.astype(BF16)


def _layer(x, packed, g1, g2, mnorm, rel_bias, gf, final_norm, past):
    wp, wt, gate_bias, woa, woh, wup, wdn = packed
    B, T, D = x.shape
    x2 = x.reshape(B * T, D)
    tm = math.gcd(T if past is None else B * T, 512)
    kv_w = N_KV_HEADS * HEAD_DIM
    if past is None:
        (misc, qm, km, vm, om, kb, kib, kt, vt, kit, vtb, qat, qit, misct) = _inproj(
            x2, g1.reshape(1, D), wp, wt, B, T, tm)
        attn = _dsa_prompt(rel_bias, qat, qit, misct, kib, kb, vtb, B, T)
        state = None
        k_new = kt.reshape(B, N_KV_HEADS, HEAD_DIM, T).transpose(0, 3, 1, 2)
        v_new = vt.reshape(B, N_KV_HEADS, HEAD_DIM, T).transpose(0, 3, 1, 2)
        ki_new = kit.transpose(0, 2, 1)
    else:
        (misc, qm, km, vm, om, qa, qi, k, v, ki) = _inproj(x2, g1.reshape(1, D), wp, None, B, T, tm)
        page_table, cache_k, cache_v, cache_kidx, c0, n0, m0 = past
        n_pool = cache_k.shape[0]
        ckt = cache_k.transpose(0, 2, 3, 1).reshape(n_pool, kv_w, PAGE_SIZE)
        cvt = cache_v.transpose(0, 2, 3, 1).reshape(n_pool, kv_w, PAGE_SIZE)
        ckit = cache_kidx.transpose(0, 2, 1)
        attn = _dsa_sample(page_table, rel_bias, qa, qi, misc, ki, k, v, ckit, ckt, cvt, B, T)
        state = (c0, n0, jnp.broadcast_to(m0[..., None], m0.shape + (LANES,)))
        k_new = k.reshape(B, T, N_KV_HEADS, HEAD_DIM)
        v_new = v.reshape(B, T, N_KV_HEADS, HEAD_DIM)
        ki_new = ki.reshape(B, T, IDX_DIM)
    h, c_new, n_new, m_new = _mlstm(qm, km, vm, om, misc, gate_bias, mnorm.reshape(1, M_WIDTH), state,
                                    B, T, BF16 if past is None else F32)
    y = _post(x2, attn, h, woa, woh, g2.reshape(1, D), wup, wdn, gf.reshape(1, D), final_norm, tm)
    return (y.reshape(B, T, D), k_new, v_new, ki_new, c_new, n_new, m_new[..., 0])


def kernel(x_prompt, x_sample, cache_k, cache_v, cache_kidx, page_table, state_C, state_n, state_m,
           w_in, b_igate, b_fgate, mlstm_norm, rel_bias, w_out, norm1, norm2, w_up, w_down, norm_f):
    depth = w_in.shape[0]
    xp, xs = x_prompt, x_sample
    outs_p, outs_s = [], []
    for l in range(depth):
        packed = _pack_layer_weights(w_in[l], b_igate[l], b_fgate[l], w_out[l], w_up[l], w_down[l])
        last = l == depth - 1
        common = (packed, norm1[l], norm2[l], mlstm_norm[l], rel_bias, norm_f, last)
        rp = _layer(xp, *common, None)
        rs = _layer(xs, *common, (page_table, cache_k[l], cache_v[l], cache_kidx[l],
                                  state_C[l], state_n[l], state_m[l]))
        xp, xs = rp[0], rs[0]
        outs_p.append(rp[1:])
        outs_s.append(rs[1:])
    stack = lambda outs, i: jnp.stack([o[i] for o in outs])
    return ((xp, xs) + tuple(stack(outs_p, i) for i in range(6))
            + tuple(stack(outs_s, i) for i in range(6)))
```
